```python
import jax, jax.numpy as jnp
from jax import lax
import numpy as np

D_MODEL = 1024
BATCH = 8
SEQ = 4096
DEPTH = 1

ROPE_THETA = 500000.0
NORM_EPS = 1e-6
Q_BLOCK = 128

MLA_HEADS = 8
MLA_Q_RANK = 384
MLA_KV_RANK = 256
MLA_NOPE_DIM = 64
MLA_ROPE_DIM = 32
MLA_V_DIM = 64
MLA_WIDTH = MLA_HEADS * MLA_V_DIM

DIL_PATTERNS = ((128, 1), (512, 4), (2048, 16))
DIL_GROUPS = 3
DIL_HEADS_PER_GROUP = 8
DIL_HEADS = DIL_GROUPS * DIL_HEADS_PER_GROUP
DIL_HEAD_DIM = 64
DIL_ROPE_DIM = DIL_HEAD_DIM // 4
DIL_WIDTH = DIL_HEADS_PER_GROUP * DIL_HEAD_DIM

IN_SPLITS = (
    MLA_Q_RANK,
    MLA_KV_RANK,
    MLA_ROPE_DIM,
    3 * DIL_HEADS * DIL_HEAD_DIM,
    MLA_WIDTH,
    DIL_WIDTH,
    D_MODEL,
    D_MODEL,
)
IN_WIDTH = sum(IN_SPLITS)

kernel_name = "hybrid_mla_dilated_gated_parallel"


def rms_norm(x, g):
    xf = x.astype(jnp.float32)
    xf = xf * lax.rsqrt(jnp.mean(xf * xf, axis=-1, keepdims=True) + NORM_EPS)
    return (xf * g.astype(jnp.float32)).astype(x.dtype)


def rope_tables(positions, rot_dim):
    inv_freq = ROPE_THETA ** (-jnp.arange(0, rot_dim, 2, dtype=jnp.float32) / rot_dim)
    ang = positions.astype(jnp.float32)[..., None] * inv_freq
    return jnp.cos(ang), jnp.sin(ang)


def apply_rope(x, cos, sin):
    half = x.shape[-1] // 2
    xf = x.astype(jnp.float32)
    x1, x2 = xf[..., :half], xf[..., half:]
    return jnp.concatenate([x1 * cos - x2 * sin, x2 * cos + x1 * sin], axis=-1).astype(x.dtype)


def partial_rope(x, cos, sin, rot_dim):
    return jnp.concatenate([apply_rope(x[..., :rot_dim], cos, sin), x[..., rot_dim:]], axis=-1)


def mla_attention(c_q, c_kv, k_rope, q_norm_g, w_uq, kv_norm_g, w_ukv, cos, sin):
    B, S, _ = c_q.shape
    q = (rms_norm(c_q, q_norm_g) @ w_uq).reshape(B, S, MLA_HEADS, MLA_NOPE_DIM + MLA_ROPE_DIM)
    q_nope, q_rope = q[..., :MLA_NOPE_DIM], q[..., MLA_NOPE_DIM:]
    q_rope = apply_rope(q_rope, cos[:, :, None, :], sin[:, :, None, :])
    k_rope = apply_rope(k_rope, cos, sin)
    kv = (rms_norm(c_kv, kv_norm_g) @ w_ukv).reshape(B, S, MLA_HEADS, MLA_NOPE_DIM + MLA_V_DIM)
    k_nope, v = kv[..., :MLA_NOPE_DIM], kv[..., MLA_NOPE_DIM:]
    scale = (MLA_NOPE_DIM + MLA_ROPE_DIM) ** -0.5
    nb = S // Q_BLOCK
    qn_b = q_nope.reshape(B, nb, Q_BLOCK, MLA_HEADS, MLA_NOPE_DIM).transpose(1, 0, 2, 3, 4)
    qr_b = q_rope.reshape(B, nb, Q_BLOCK, MLA_HEADS, MLA_ROPE_DIM).transpose(1, 0, 2, 3, 4)
    key_idx = jnp.arange(S)

    def one_block(args):
        qn, qr, start = args
        s = (jnp.einsum('bqhd,bkhd->bhqk', qn, k_nope)
             + jnp.einsum('bqhr,bkr->bhqk', qr, k_rope)).astype(jnp.float32) * scale
        q_idx = start + jnp.arange(Q_BLOCK)
        s = jnp.where(key_idx[None, :] <= q_idx[:, None], s, -jnp.inf)
        p = jax.nn.softmax(s, axis=-1)
        return jnp.einsum('bhqk,bkhd->bqhd', p.astype(v.dtype), v)

    out = lax.map(one_block, (qn_b, qr_b, jnp.arange(nb) * Q_BLOCK))
    return out.transpose(1, 0, 2, 3, 4).reshape(B, S, MLA_WIDTH)


def dilated_group(q, k, v, window, dilation):
    B, S, H, Dh = q.shape
    L = S // dilation
    sub_window = window // dilation
    L_pad = -(-L // Q_BLOCK) * Q_BLOCK
    BD = B * dilation
    nb = L_pad // Q_BLOCK

    def to_sub(t):
        t = t.reshape(B, L, dilation, H, Dh).transpose(0, 2, 3, 1, 4).reshape(BD, H, L, Dh)
        return jnp.pad(t, ((0, 0), (0, 0), (0, L_pad - L), (0, 0)))

    qs, ks, vs = to_sub(q), to_sub(k), to_sub(v)
    qb = qs.reshape(BD, H, nb, Q_BLOCK, Dh)

    def band(t):
        tp = jnp.pad(t, ((0, 0), (0, 0), (Q_BLOCK, 0), (0, 0)))
        prev = tp[:, :, :L_pad].reshape(BD, H, nb, Q_BLOCK, Dh)
        cur = t.reshape(BD, H, nb, Q_BLOCK, Dh)
        return jnp.concatenate([prev, cur], axis=3)

    kb, vb = band(ks), band(vs)
    s = jnp.einsum('zhnqd,zhnkd->zhnqk', qb, kb).astype(jnp.float32) * (Dh ** -0.5)
    blk = jnp.arange(nb)[:, None, None] * Q_BLOCK
    qi = blk + jnp.arange(Q_BLOCK)[None, :, None]
    kj = blk - Q_BLOCK + jnp.arange(2 * Q_BLOCK)[None, None, :]
    dist = qi - kj
    mask = (dist >= 0) & (dist <= sub_window) & (kj >= 0)
    s = jnp.where(mask, s, -jnp.inf)
    m = jnp.max(s, axis=-1, keepdims=True)
    p = jnp.exp(s - m)
    denom = jnp.sum(p, axis=-1, keepdims=True)
    o = jnp.einsum('zhnqk,zhnkd->zhnqd', (p / denom).astype(v.dtype), vb)
    lse = (m + jnp.log(denom))[..., 0]
    o = o.reshape(BD, H, L_pad, Dh)[:, :, :L]
    o = o.reshape(B, dilation, H, L, Dh).transpose(0, 3, 1, 2, 4).reshape(B, S, H, Dh)
    lse = lse.reshape(BD, H, L_pad)[:, :, :L]
    lse = lse.reshape(B, dilation, H, L).transpose(0, 3, 1, 2).reshape(B, S, H)
    return o, lse


def dilated_attention(qkv, cos, sin):
    B, S, _ = qkv.shape
    qkv = qkv.reshape(B, S, 3, DIL_GROUPS, DIL_HEADS_PER_GROUP, DIL_HEAD_DIM)
    c, s_ = cos[:, :, None, None, :], sin[:, :, None, None, :]
    q = partial_rope(qkv[:, :, 0], c, s_, DIL_ROPE_DIM)
    k = partial_rope(qkv[:, :, 1], c, s_, DIL_ROPE_DIM)
    v = qkv[:, :, 2]
    outs, lses = [], []
    for g, (window, dilation) in enumerate(DIL_PATTERNS):
        o, lse = dilated_group(q[:, :, g], k[:, :, g], v[:, :, g], window, dilation)
        outs.append(o)
        lses.append(lse)
    alpha = jax.nn.softmax(jnp.stack(lses, axis=0), axis=0)
    out = jnp.sum(alpha[..., None].astype(v.dtype) * jnp.stack(outs, axis=0), axis=0)
    return out.reshape(B, S, DIL_WIDTH)


def _fwd_setup_inputs(seed: int = 0) -> dict:
    key = jax.random.key(seed)
    ks = jax.random.split(key, 13)
    f32 = jnp.float32

    def w(k, shape, fan_in):
        return jax.random.normal(k, shape, f32) * (fan_in ** -0.5)

    def gain(k, n):
        return 1.0 + 0.02 * jax.random.normal(k, (DEPTH, n), f32)

    x = jax.random.normal(ks[0], (BATCH, SEQ, D_MODEL), f32)
    start = jax.random.randint(ks[1], (BATCH, 1), 0, 4096, dtype=jnp.int32)
    positions = start + jnp.arange(SEQ, dtype=jnp.int32)[None, :]
    return {
        "x": x,
        "positions": positions,
        "pre_norm_g": gain(ks[2], D_MODEL),
        "w_in": w(ks[3], (DEPTH, D_MODEL, IN_WIDTH), D_MODEL),
        "q_norm_g": gain(ks[4], MLA_Q_RANK),
        "w_uq": w(ks[5], (DEPTH, MLA_Q_RANK, MLA_HEADS * (MLA_NOPE_DIM + MLA_ROPE_DIM)), MLA_Q_RANK),
        "kv_norm_g": gain(ks[6], MLA_KV_RANK),
        "w_ukv": w(ks[7], (DEPTH, MLA_KV_RANK, MLA_HEADS * (MLA_NOPE_DIM + MLA_V_DIM)), MLA_KV_RANK),
        "w_proj_mla": w(ks[8], (DEPTH, MLA_WIDTH, D_MODEL), MLA_WIDTH),
        "w_proj_dil": w(ks[9], (DEPTH, DIL_WIDTH, D_MODEL), DIL_WIDTH),
        "w_out": w(ks[10], (DEPTH, D_MODEL, D_MODEL), D_MODEL),
        "post_norm_g": gain(ks[11], D_MODEL),
    }


def _fwd_reference(x, positions, pre_norm_g, w_in, q_norm_g, w_uq, kv_norm_g, w_ukv,
              w_proj_mla, w_proj_dil, w_out, post_norm_g):
    cos_mla, sin_mla = rope_tables(positions, MLA_ROPE_DIM)
    cos_dil, sin_dil = rope_tables(positions, DIL_ROPE_DIM)
    split_at = [int(i) for i in np.cumsum(IN_SPLITS)[:-1]]
    for layer in range(DEPTH):
        h = rms_norm(x, pre_norm_g[layer])
        proj = h @ w_in[layer]
        c_q, c_kv, k_rope, qkv_dil, z_mla, z_dil, g_mla, g_dil = jnp.split(proj, split_at, axis=-1)
        y_mla = mla_attention(c_q, c_kv, k_rope, q_norm_g[layer], w_uq[layer],
                              kv_norm_g[layer], w_ukv[layer], cos_mla, sin_mla)
        y_dil = dilated_attention(qkv_dil, cos_dil, sin_dil)
        y_mla = (y_mla * jax.nn.silu(z_mla)) @ w_proj_mla[layer]
        y_dil = (y_dil * jax.nn.silu(z_dil)) @ w_proj_dil[layer]
        merged = jax.nn.sigmoid(g_mla) * y_mla + jax.nn.sigmoid(g_dil) * y_dil
        x = x + rms_norm(merged @ w_out[layer], post_norm_g[layer])
    return x


import jax as _jax
import jax.numpy as _jnp

TWIN_FORMAT = 'train_step'
FWD_PARAMS = ['x', 'positions', 'pre_norm_g', 'w_in', 'q_norm_g', 'w_uq', 'kv_norm_g', 'w_ukv', 'w_proj_mla', 'w_proj_dil', 'w_out', 'post_norm_g']
TWIN_WEIGHTS = ['pre_norm_g', 'w_in', 'q_norm_g', 'w_uq', 'kv_norm_g', 'w_ukv', 'w_proj_mla', 'w_proj_dil', 'w_out', 'post_norm_g']
TWIN_DIFF_INPUT = 'x'
TWIN_INPUTS = ['x', 'positions', 'pre_norm_g', 'w_in', 'q_norm_g', 'w_uq', 'kv_norm_g', 'w_ukv', 'w_proj_mla', 'w_proj_dil', 'w_out', 'post_norm_g', 'loss_target', 'm_pre_norm_g', 'm_w_in', 'm_q_norm_g', 'm_w_uq', 'm_kv_norm_g', 'm_w_ukv', 'm_w_proj_mla', 'm_w_proj_dil', 'm_w_out', 'm_post_norm_g', 'v_pre_norm_g', 'v_w_in', 'v_q_norm_g', 'v_w_uq', 'v_kv_norm_g', 'v_w_ukv', 'v_w_proj_mla', 'v_w_proj_dil', 'v_w_out', 'v_post_norm_g']
TWIN_OUTPUTS = ['loss', 'grad_x', 'grad_pre_norm_g', 'grad_w_in', 'grad_q_norm_g', 'grad_w_uq', 'grad_kv_norm_g', 'grad_w_ukv', 'grad_w_proj_mla', 'grad_w_proj_dil', 'grad_w_out', 'grad_post_norm_g', 'delta_pre_norm_g', 'delta_w_in', 'delta_q_norm_g', 'delta_w_uq', 'delta_kv_norm_g', 'delta_w_ukv', 'delta_w_proj_mla', 'delta_w_proj_dil', 'delta_w_out', 'delta_post_norm_g', 'new_m_pre_norm_g', 'new_m_w_in', 'new_m_q_norm_g', 'new_m_w_uq', 'new_m_kv_norm_g', 'new_m_w_ukv', 'new_m_w_proj_mla', 'new_m_w_proj_dil', 'new_m_w_out', 'new_m_post_norm_g', 'new_v_pre_norm_g', 'new_v_w_in', 'new_v_q_norm_g', 'new_v_w_uq', 'new_v_kv_norm_g', 'new_v_w_ukv', 'new_v_w_proj_mla', 'new_v_w_proj_dil', 'new_v_w_out', 'new_v_post_norm_g']
TWIN_LEAF_KINDS = {'loss': 'loss', 'grad_x': 'grad_x', 'grad_pre_norm_g': 'grad_w', 'grad_w_in': 'grad_w', 'grad_q_norm_g': 'grad_w', 'grad_w_uq': 'grad_w', 'grad_kv_norm_g': 'grad_w', 'grad_w_ukv': 'grad_w', 'grad_w_proj_mla': 'grad_w', 'grad_w_proj_dil': 'grad_w', 'grad_w_out': 'grad_w', 'grad_post_norm_g': 'grad_w', 'delta_pre_norm_g': 'delta_w', 'delta_w_in': 'delta_w', 'delta_q_norm_g': 'delta_w', 'delta_w_uq': 'delta_w', 'delta_kv_norm_g': 'delta_w', 'delta_w_ukv': 'delta_w', 'delta_w_proj_mla': 'delta_w', 'delta_w_proj_dil': 'delta_w', 'delta_w_out': 'delta_w', 'delta_post_norm_g': 'delta_w', 'new_m_pre_norm_g': 'new_m', 'new_m_w_in': 'new_m', 'new_m_q_norm_g': 'new_m', 'new_m_w_uq': 'new_m', 'new_m_kv_norm_g': 'new_m', 'new_m_w_ukv': 'new_m', 'new_m_w_proj_mla': 'new_m', 'new_m_w_proj_dil': 'new_m', 'new_m_w_out': 'new_m', 'new_m_post_norm_g': 'new_m', 'new_v_pre_norm_g': 'new_v', 'new_v_w_in': 'new_v', 'new_v_q_norm_g': 'new_v', 'new_v_w_uq': 'new_v', 'new_v_kv_norm_g': 'new_v', 'new_v_w_ukv': 'new_v', 'new_v_w_proj_mla': 'new_v', 'new_v_w_proj_dil': 'new_v', 'new_v_w_out': 'new_v', 'new_v_post_norm_g': 'new_v'}


def _forward(args):
    return _fwd_reference(*[args[k] for k in FWD_PARAMS])


def _output_shape():
    out = _jax.eval_shape(lambda: _forward(_fwd_setup_inputs(0)))
    return out.shape, out.dtype

N_MICROBATCH = 1
ADAM_LR = 0.001
ADAM_B1 = 0.9
ADAM_B2 = 0.999
ADAM_EPS = 1e-08
ADAM_WD = 0.01
ADAM_STEP = 10
PER_EXAMPLE_BATCH_AXIS = {'x': 0, 'positions': 0, 'loss_target': 0}
SHARED_INPUTS = []
_WEIGHT_DTYPES = {'pre_norm_g': _jnp.float32, 'w_in': _jnp.float32, 'q_norm_g': _jnp.float32, 'w_uq': _jnp.float32, 'kv_norm_g': _jnp.float32, 'w_ukv': _jnp.float32, 'w_proj_mla': _jnp.float32, 'w_proj_dil': _jnp.float32, 'w_out': _jnp.float32, 'post_norm_g': _jnp.float32}
MOMENT_SCALE = {'pre_norm_g': 5.182777e-01, 'w_in': 1.797889e-01, 'q_norm_g': 2.756399e-01, 'w_uq': 2.038051e-01, 'kv_norm_g': 6.281098e-01, 'w_ukv': 2.410579e-01, 'w_proj_mla': 1.912800e-01, 'w_proj_dil': 2.096063e-01, 'w_out': 2.689251e-01, 'post_norm_g': 3.200251e+01}


def _to_microbatches(a, axis):
    t = _jnp.moveaxis(a, axis, 0)
    t = t.reshape((N_MICROBATCH, t.shape[0] // N_MICROBATCH) + t.shape[1:])
    return _jnp.moveaxis(t, 1, axis + 1)


def setup_inputs(seed: int = 0) -> dict:
    inp = _fwd_setup_inputs(seed)
    key = _jax.random.fold_in(_jax.random.key(seed), 7919)
    shape, _ = _output_shape()
    out = dict(inp)
    out["loss_target"] = _jax.random.normal(_jax.random.fold_in(key, 0), shape, _jnp.float32)
    for i, name in enumerate(TWIN_WEIGHTS):
        w = inp[name].astype(_jnp.float32)
        if MOMENT_SCALE is None:
            s = _jnp.sqrt(_jnp.mean(_jnp.square(w)) + 1e-30)
        else:
            s = MOMENT_SCALE[name]
        km, kv = _jax.random.split(_jax.random.fold_in(key, i + 1))
        out[name] = w
        out["m_" + name] = s * _jax.random.normal(km, w.shape, _jnp.float32)
        out["v_" + name] = (s * s) * _jax.random.uniform(kv, w.shape, _jnp.float32, 0.5, 1.5)
    if N_MICROBATCH > 1:
        for name, axis in PER_EXAMPLE_BATCH_AXIS.items():
            out[name] = _to_microbatches(out[name], axis)
    return {'x': out['x'], 'positions': out['positions'], 'pre_norm_g': out['pre_norm_g'], 'w_in': out['w_in'], 'q_norm_g': out['q_norm_g'], 'w_uq': out['w_uq'], 'kv_norm_g': out['kv_norm_g'], 'w_ukv': out['w_ukv'], 'w_proj_mla': out['w_proj_mla'], 'w_proj_dil': out['w_proj_dil'], 'w_out': out['w_out'], 'post_norm_g': out['post_norm_g'], 'loss_target': out['loss_target'], 'm_pre_norm_g': out['m_pre_norm_g'], 'm_w_in': out['m_w_in'], 'm_q_norm_g': out['m_q_norm_g'], 'm_w_uq': out['m_w_uq'], 'm_kv_norm_g': out['m_kv_norm_g'], 'm_w_ukv': out['m_w_ukv'], 'm_w_proj_mla': out['m_w_proj_mla'], 'm_w_proj_dil': out['m_w_proj_dil'], 'm_w_out': out['m_w_out'], 'm_post_norm_g': out['m_post_norm_g'], 'v_pre_norm_g': out['v_pre_norm_g'], 'v_w_in': out['v_w_in'], 'v_q_norm_g': out['v_q_norm_g'], 'v_w_uq': out['v_w_uq'], 'v_kv_norm_g': out['v_kv_norm_g'], 'v_w_ukv': out['v_w_ukv'], 'v_w_proj_mla': out['v_w_proj_mla'], 'v_w_proj_dil': out['v_w_proj_dil'], 'v_w_out': out['v_w_out'], 'v_post_norm_g': out['v_post_norm_g']}


def _loss(weights, diff, rest, loss_target):
    with _jax.named_scope("forward"):
        args = {**rest, TWIN_DIFF_INPUT: diff, **{k: w.astype(_WEIGHT_DTYPES[k]) for k, w in weights.items()}}
        y = _forward(args)
    with _jax.named_scope("loss_head"):
        err = _jnp.square(y.astype(_jnp.float32) - loss_target)
        return 0.5 * _jnp.sum(_jnp.mean(err, axis=-1)) if err.ndim else 0.5 * err


def _adamw(w, g, m, v):
    m = ADAM_B1 * m + (1.0 - ADAM_B1) * g
    v = ADAM_B2 * v + (1.0 - ADAM_B2) * _jnp.square(g)
    m_hat = m / (1.0 - ADAM_B1 ** ADAM_STEP)
    v_hat = v / (1.0 - ADAM_B2 ** ADAM_STEP)
    delta = -ADAM_LR * (m_hat / (_jnp.sqrt(v_hat) + ADAM_EPS) + ADAM_WD * w)
    return delta, m, v


def reference(x, positions, pre_norm_g, w_in, q_norm_g, w_uq, kv_norm_g, w_ukv, w_proj_mla, w_proj_dil, w_out, post_norm_g, loss_target, m_pre_norm_g, m_w_in, m_q_norm_g, m_w_uq, m_kv_norm_g, m_w_ukv, m_w_proj_mla, m_w_proj_dil, m_w_out, m_post_norm_g, v_pre_norm_g, v_w_in, v_q_norm_g, v_w_uq, v_kv_norm_g, v_w_ukv, v_w_proj_mla, v_w_proj_dil, v_w_out, v_post_norm_g):
    given = dict(x=x, positions=positions, pre_norm_g=pre_norm_g, w_in=w_in, q_norm_g=q_norm_g, w_uq=w_uq, kv_norm_g=kv_norm_g, w_ukv=w_ukv, w_proj_mla=w_proj_mla, w_proj_dil=w_proj_dil, w_out=w_out, post_norm_g=post_norm_g, loss_target=loss_target, m_pre_norm_g=m_pre_norm_g, m_w_in=m_w_in, m_q_norm_g=m_q_norm_g, m_w_uq=m_w_uq, m_kv_norm_g=m_kv_norm_g, m_w_ukv=m_w_ukv, m_w_proj_mla=m_w_proj_mla, m_w_proj_dil=m_w_proj_dil, m_w_out=m_w_out, m_post_norm_g=m_post_norm_g, v_pre_norm_g=v_pre_norm_g, v_w_in=v_w_in, v_q_norm_g=v_q_norm_g, v_w_uq=v_w_uq, v_kv_norm_g=v_kv_norm_g, v_w_ukv=v_w_ukv, v_w_proj_mla=v_w_proj_mla, v_w_proj_dil=v_w_proj_dil, v_w_out=v_w_out, v_post_norm_g=v_post_norm_g)
    weights = {n: given[n] for n in TWIN_WEIGHTS}
    shared = {n: given[n] for n in SHARED_INPUTS}
    per_example = {n: given[n] for n in ['x', 'positions']}
    grad_fn = _jax.value_and_grad(_loss, argnums=(0, 1))

    def one_microbatch(ex, loss_target):
        ex = dict(ex)
        diff = ex.pop(TWIN_DIFF_INPUT)
        return grad_fn(weights, diff, {**shared, **ex}, loss_target)

    if N_MICROBATCH == 1:
        loss, (grad_w, grad_x) = one_microbatch(per_example, given["loss_target"])
    else:
        def body(carry, xs):
            loss_sum, grad_sum = carry
            l_k, (gw_k, gx_k) = one_microbatch(xs[0], xs[1])
            with _jax.named_scope("update"):
                return (loss_sum + l_k, _jax.tree.map(_jnp.add, grad_sum, gw_k)), gx_k

        init = (_jnp.zeros((), _jnp.float32), _jax.tree.map(_jnp.zeros_like, weights))
        (loss, grad_w), grad_x = _jax.lax.scan(body, init, (per_example, given["loss_target"]))
    with _jax.named_scope("update"):
        delta_w, new_m, new_v = {}, {}, {}
        for n in TWIN_WEIGHTS:
            delta_w[n], new_m[n], new_v[n] = _adamw(weights[n], grad_w[n], given["m_" + n], given["v_" + n])
    return (loss, grad_x, *[grad_w[n] for n in TWIN_WEIGHTS], *[delta_w[n] for n in TWIN_WEIGHTS],
            *[new_m[n] for n in TWIN_WEIGHTS], *[new_v[n] for n in TWIN_WEIGHTS])
```

```python
import jax
import jax.numpy as jnp
from jax import lax
from jax.experimental import pallas as pl
from jax.experimental.pallas import tpu as pltpu

F32 = jnp.float32
BF16 = jnp.bfloat16

D_MODEL = 1024
NORM_EPS = 1e-6
ROPE_THETA = 500000.0
N_DEV = 8
LANES = 128
NEG = -1e30

MLA_HEADS = 8
MLA_Q_RANK = 384
MLA_KV_RANK = 256
MLA_SCALE = 96.0 ** -0.5
DIL_DILATIONS = (1, 4, 16)
DIL_SCALE = 0.125
Q_BLOCK = 128

Z_MLA, Z_DIL, G_MLA, G_DIL = 0, 512, 1024, 2048
Q_OFF, K_OFF, V_OFF = 3072, 4608, 6144
CQ_OFF, KR_OFF, CKV_OFF, IN_PAD = 7680, 8064, 8192, 8448
IN_WIDTH = 8352
IN_SEGS = ((0, 384, CQ_OFF), (384, 256, CKV_OFF), (640, 32, KR_OFF), (672, 1536, Q_OFF), (2208, 1536, K_OFF),
           (3744, 1536, V_OFF), (5280, 512, Z_MLA), (5792, 512, Z_DIL), (6304, 1024, G_MLA), (7328, 1024, G_DIL))

PACK_ROWS = (8352, 288, 256, 512, 512, 1024)
PACK_TOTAL = sum(PACK_ROWS)
SMALL_ROWS = 24

ADAM_LR, ADAM_B1, ADAM_B2, ADAM_EPS, ADAM_WD, ADAM_STEP = 0.001, 0.9, 0.999, 1e-08, 0.01, 10

VMEM_LIMIT_MB = 56


def _cparams(sem=None, vmem_mb=VMEM_LIMIT_MB):
    return pltpu.CompilerParams(dimension_semantics=sem, vmem_limit_bytes=vmem_mb * 1024 * 1024)


def _dot(a, b):
    return jnp.dot(a, b, preferred_element_type=F32)


def _dot_nt(a, b):
    return lax.dot_general(a, b, (((1,), (1,)), ((), ())), preferred_element_type=F32)


def _dot_tn(a, b):
    return lax.dot_general(a, b, (((0,), (0,)), ((), ())), preferred_element_type=F32)


def _tile_lanes(t, width):
    return t if width == t.shape[1] else jnp.tile(t, (1, width // t.shape[1]))


def _rope(x, c, sp, sm, a):
    n = x.shape[1]
    return x * c + pltpu.roll(x, a, 1) * sp + pltpu.roll(x, n - a, 1) * sm


def _unrope(dy, c, sp, sm, a):
    n = dy.shape[1]
    return dy * c + pltpu.roll(dy * sp, n - a, 1) + pltpu.roll(dy * sm, a, 1)


def _sigmoid(z):
    return 1.0 / (1.0 + jnp.exp(-z))


def _left_mask():
    return lax.broadcasted_iota(jnp.int32, (1, LANES), 1) < 64


def _expand_half(x, hh, left):
    r = pltpu.roll(x, 64, 1)
    return jnp.where(left, x, r) if hh == 0 else jnp.where(left, r, x)


def _rms(xv, g):
    r = lax.rsqrt(jnp.mean(xv * xv, axis=-1, keepdims=True) + NORM_EPS)
    xh = xv * r
    return xh * g, xh, r


def _rms_bwd(dout, g, xh, r):
    dxh = dout * g
    return r * (dxh - xh * jnp.mean(dxh * xh, axis=-1, keepdims=True))


def _full(shape, index_map):
    return pl.BlockSpec(shape, index_map)


def _pack_shards(parts, dtype):
    return jnp.concatenate([p.astype(dtype).reshape(-1, LANES) for p in parts], axis=0)


def _unpack_shards(buf):
    shapes = ((1, 1024, 1044), (1, 384, 96), (1, 256, 128), (1, 512, 128), (1, 512, 128), (1, 128, 1024))
    out, r0 = [], 0
    for rows, shp in zip(PACK_ROWS, shapes):
        out.append(buf[r0:r0 + rows].reshape(shp))
        r0 += rows
    return out


def _pad_w_in(w):
    parts = []
    for s, n, off in sorted(IN_SEGS, key=lambda t: t[2]):
        seg = w[:, s:s + n]
        if off == KR_OFF:
            seg = jnp.pad(seg, ((0, 0), (0, LANES - n)))
        parts.append(seg)
    return jnp.concatenate(parts, axis=1)


def _unpad_dw_in(dw):
    return jnp.concatenate([dw[:, off:off + n] for s, n, off in IN_SEGS], axis=1)


def _unpack_gathered(g):
    r = [0]
    for n in PACK_ROWS:
        r.append(r[-1] + n)
    w_in = g[:, r[0]:r[1]].reshape(N_DEV, 1024, 1044).transpose(1, 0, 2).reshape(1024, IN_WIDTH)
    w_uq = g[:, r[1]:r[2]].reshape(N_DEV, 384, 96).transpose(1, 0, 2).reshape(384, 8, 96)
    w_ukv = g[:, r[2]:r[3]].reshape(N_DEV, 256, 128).transpose(1, 0, 2).reshape(256, 8, 128)
    wp_mla = g[:, r[3]:r[4]].reshape(N_DEV, 512, 128).transpose(1, 0, 2).reshape(512, 1024)
    wp_dil = g[:, r[4]:r[5]].reshape(N_DEV, 512, 128).transpose(1, 0, 2).reshape(512, 1024)
    w_out = g[:, r[5]:r[6]].reshape(1024, 1024)
    w_uq_pad = jnp.pad(w_uq, ((0, 0), (0, 0), (0, 32))).reshape(384, 1024)
    w_uk_pad = jnp.pad(w_ukv[:, :, :64], ((0, 0), (0, 0), (0, 64))).reshape(256, 1024)
    w_uv = w_ukv[:, :, 64:].reshape(256, 512)
    return _pad_w_in(w_in), w_uq_pad, w_uk_pad, w_uv, wp_mla, wp_dil, w_out


def _pack_grads(dw_in_pad, dw_uq_pad, dw_uk_pad, dw_uv, dwp_mla, dwp_dil, dw_out):
    a = _unpad_dw_in(dw_in_pad).reshape(1024, N_DEV, 1044).transpose(1, 0, 2).reshape(N_DEV, -1, LANES)
    b = dw_uq_pad.reshape(384, 8, 128)[:, :, :96].transpose(1, 0, 2).reshape(N_DEV, -1, LANES)
    ukv = jnp.concatenate([dw_uk_pad.reshape(256, 8, 128)[:, :, :64], dw_uv.reshape(256, 8, 64)], axis=2)
    c = ukv.transpose(1, 0, 2).reshape(N_DEV, -1, LANES)
    d = dwp_mla.reshape(512, N_DEV, 128).transpose(1, 0, 2).reshape(N_DEV, -1, LANES)
    e = dwp_dil.reshape(512, N_DEV, 128).transpose(1, 0, 2).reshape(N_DEV, -1, LANES)
    f = dw_out.reshape(N_DEV, -1, LANES)
    return jnp.concatenate([a, b, c, d, e, f], axis=1)


def _rope_tables(pos):
    s = pos.shape[0]
    p = pos.astype(F32)[:, None]
    am = p * (ROPE_THETA ** (-jnp.arange(0, 32, 2, dtype=F32) / 32))
    ad = p * (ROPE_THETA ** (-jnp.arange(0, 16, 2, dtype=F32) / 16))
    cm, sm = jnp.cos(am), jnp.sin(am)
    cd, sd = jnp.cos(ad), jnp.sin(ad)
    z = lambda n: jnp.zeros((s, n), F32)
    o = lambda n: jnp.ones((s, n), F32)
    cat = lambda xs: jnp.concatenate(xs, axis=1)
    q_tab = (cat([o(64), cm, cm, z(32)]), cat([z(80), sm, z(32)]), cat([z(64), -sm, z(48)]))
    k_tab = (cat([cm, cm, z(96)]), cat([z(16), sm, z(96)]), cat([-sm, z(112)]))
    d1 = (cat([cd, cd, o(48)]), cat([z(8), sd, z(48)]), cat([-sd, z(56)]))
    d_tab = tuple(cat([t, t]) for t in d1)
    return q_tab, k_tab, d_tab


def _perm(a, d):
    if d == 1:
        return a
    s, c = a.shape
    return a.reshape(s // d, d, c).transpose(1, 0, 2).reshape(s, c)


def _unperm(a, d):
    if d == 1:
        return a
    s, c = a.shape
    return a.reshape(d, s // d, c).transpose(1, 0, 2).reshape(s, c)


def _inproj(x, gpre, w_pad, d_tab):
    s = x.shape[0]
    tm, tn = min(1024, s), 768
    rope_lo, rope_hi = Q_OFF // tn, V_OFF // tn

    def body(x_ref, g_ref, w_ref, c_ref, sp_ref, sm_ref, o_ref, h_ref):
        j = pl.program_id(1)

        @pl.when(j == 0)
        def _():
            hv, _, _ = _rms(x_ref[...], g_ref[...])
            h_ref[...] = hv.astype(BF16)

        acc = _dot(h_ref[...], w_ref[...])
        is_rope = jnp.logical_and(j >= rope_lo, j < rope_hi)

        @pl.when(is_rope)
        def _():
            o_ref[...] = _rope(acc, _tile_lanes(c_ref[...], tn), _tile_lanes(sp_ref[...], tn),
                               _tile_lanes(sm_ref[...], tn), 8).astype(BF16)

        @pl.when(jnp.logical_not(is_rope))
        def _():
            o_ref[...] = acc.astype(BF16)

    row = lambda i, j: (i, 0)
    return pl.pallas_call(
        body, name="inproj", grid=(s // tm, IN_PAD // tn),
        in_specs=[_full((tm, D_MODEL), row), _full((1, D_MODEL), lambda i, j: (0, 0)),
                  _full((D_MODEL, tn), lambda i, j: (0, j)),
                  _full((tm, LANES), row), _full((tm, LANES), row), _full((tm, LANES), row)],
        out_specs=[_full((tm, tn), lambda i, j: (i, j)), _full((tm, D_MODEL), row)],
        out_shape=[jax.ShapeDtypeStruct((s, IN_PAD), BF16), jax.ShapeDtypeStruct((s, D_MODEL), BF16)],
        compiler_params=_cparams(("parallel", "arbitrary")),
    )(x, gpre, w_pad, *d_tab)


def _mla_prep(proj, gq, gkv, w_uq, w_uk, w_uv, q_tab, k_tab):
    s = proj.shape[0]
    tm = min(512, s)

    def body(cq_ref, kr_ref, ckv_ref, gq_ref, gkv_ref, wq_ref, wk_ref, wv_ref,
             qc, qsp, qsm, kc, ksp, ksm, q_out, k_out, v_out):
        cqn, _, _ = _rms(cq_ref[...].astype(F32), gq_ref[...])
        q = _dot(cqn.astype(BF16), wq_ref[...])
        q = _rope(q, _tile_lanes(qc[...], 1024), _tile_lanes(qsp[...], 1024), _tile_lanes(qsm[...], 1024), 16)
        q_out[...] = q.astype(BF16)
        ckvn, _, _ = _rms(ckv_ref[...].astype(F32), gkv_ref[...])
        ckvn = ckvn.astype(BF16)
        kr = _rope(kr_ref[...].astype(F32), kc[...], ksp[...], ksm[...], 16)
        k = _dot(ckvn, wk_ref[...]) + _tile_lanes(pltpu.roll(kr, 64, 1), 1024)
        k_out[...] = k.astype(BF16)
        v_out[...] = _dot(ckvn, wv_ref[...]).astype(BF16)

    row = lambda i: (i, 0)
    cst = lambda i: (0, 0)
    tabs = [_full((tm, LANES), row)] * 6
    return pl.pallas_call(
        body, name="mla_prep", grid=(s // tm,),
        in_specs=[_full((tm, 384), lambda i: (i, CQ_OFF // 384)), _full((tm, 128), lambda i: (i, KR_OFF // 128)),
                  _full((tm, 256), lambda i: (i, CKV_OFF // 256)), _full((1, 384), cst), _full((1, 256), cst),
                  _full((384, 1024), cst), _full((256, 1024), cst), _full((256, 512), cst)] + tabs,
        out_specs=[_full((tm, 1024), row), _full((tm, 1024), row), _full((tm, 512), row)],
        out_shape=[jax.ShapeDtypeStruct((s, 1024), BF16), jax.ShapeDtypeStruct((s, 1024), BF16),
                   jax.ShapeDtypeStruct((s, 512), BF16)],
        compiler_params=_cparams(("parallel",)),
    )(proj, proj, proj, gq, gkv, w_uq, w_uk, w_uv, *q_tab, *k_tab)


def _mla_fwd(q, k, v):
    s = q.shape[0]
    tq = tk = min(256, s)
    nq = s // tq

    def body(q_ref, k_ref, v_ref, o_ref, lse_ref):
        left = _left_mask()
        causal = (lax.broadcasted_iota(jnp.int32, (tq, tk), 1) <= lax.broadcasted_iota(jnp.int32, (tq, tk), 0))

        def q_step(i, _):
            r0 = pl.multiple_of(i * tq, tq)
            outs = []
            for hh in range(2):
                hm = left if hh == 0 else jnp.logical_not(left)
                qh = q_ref[pl.ds(r0, tq), hh * 128:(hh + 1) * 128]

                def kv_step(j, carry, masked, hh=hh, hm=hm, qh=qh):
                    m, l, acc = carry
                    c0 = pl.multiple_of(j * tk, tk)
                    kh = k_ref[pl.ds(c0, tk), hh * 128:(hh + 1) * 128]
                    vv = v_ref[pl.ds(c0, tk), :]
                    vm = jnp.where(hm, vv, jnp.zeros_like(vv))
                    sc = _dot_nt(qh, kh) * MLA_SCALE
                    if masked:
                        sc = jnp.where(causal, sc, NEG)
                    m_new = jnp.maximum(m, jnp.max(sc, axis=1, keepdims=True))
                    alpha = jnp.exp(m - m_new)
                    p = jnp.exp(sc - _tile_lanes(m_new, tk))
                    l = alpha * l + jnp.sum(p, axis=1, keepdims=True)
                    acc = acc * alpha + _dot(p.astype(BF16), vm)
                    return m_new, l, acc

                init = (jnp.full((tq, LANES), NEG, F32), jnp.zeros((tq, LANES), F32), jnp.zeros((tq, LANES), F32))
                carry = lax.fori_loop(0, i, lambda j, c, f=kv_step: f(j, c, False), init)
                m, l, acc = kv_step(i, carry, True)
                outs.append((acc / l, m + jnp.log(l)))
            o_ref[pl.ds(r0, tq), :] = (outs[0][0] + outs[1][0]).astype(BF16)
            lse_ref[pl.ds(r0, tq), :] = jnp.where(left, outs[0][1], outs[1][1])
            return 0

        lax.fori_loop(0, nq, q_step, 0)

    return pl.pallas_call(
        body, name="mla_fwd", grid=(4,),
        in_specs=[_full((s, 256), lambda p: (0, p)), _full((s, 256), lambda p: (0, p)),
                  _full((s, 128), lambda p: (0, p))],
        out_specs=[_full((s, 128), lambda p: (0, p)), _full((s, 128), lambda p: (0, p))],
        out_shape=[jax.ShapeDtypeStruct((s, 512), BF16), jax.ShapeDtypeStruct((s, 512), F32)],
        compiler_params=_cparams(("parallel",)),
    )(q, k, v)


def _band_masks(has_prev):
    r = lax.broadcasted_iota(jnp.int32, (Q_BLOCK, Q_BLOCK), 0)
    c = lax.broadcasted_iota(jnp.int32, (Q_BLOCK, Q_BLOCK), 1)
    return c >= r + jnp.where(has_prev, 0, Q_BLOCK), c <= r


def _dil_fwd(qa, ka, va, qo, ko, vo, d, name):
    s = qa.shape[0]
    nblk = s // Q_BLOCK
    per_seq = nblk // d

    def body(q_ref, k_ref, v_ref, o_ref, lse_ref):
        left = _left_mask()

        def step(b, _):
            r0 = pl.multiple_of(b * Q_BLOCK, Q_BLOCK)
            p0 = pl.multiple_of(jnp.maximum(b - 1, 0) * Q_BLOCK, Q_BLOCK)
            pmask, cur_ok = _band_masks((b % per_seq) > 0)
            qb = q_ref[pl.ds(r0, Q_BLOCK), :]
            kp, kc = k_ref[pl.ds(p0, Q_BLOCK), :], k_ref[pl.ds(r0, Q_BLOCK), :]
            vp, vc = v_ref[pl.ds(p0, Q_BLOCK), :], v_ref[pl.ds(r0, Q_BLOCK), :]
            outs = []
            for hh in range(2):
                hm = left if hh == 0 else jnp.logical_not(left)
                zero = jnp.zeros_like(qb)
                qm = jnp.where(hm, qb, zero)
                sp = jnp.where(pmask, _dot_nt(qm, kp) * DIL_SCALE, NEG)
                sc = jnp.where(cur_ok, _dot_nt(qm, kc) * DIL_SCALE, NEG)
                m = jnp.maximum(jnp.max(sp, axis=1, keepdims=True), jnp.max(sc, axis=1, keepdims=True))
                pp, pc = jnp.exp(sp - m), jnp.exp(sc - m)
                l = jnp.sum(pp, axis=1, keepdims=True) + jnp.sum(pc, axis=1, keepdims=True)
                acc = _dot(pp.astype(BF16), jnp.where(hm, vp, zero)) + _dot(pc.astype(BF16), jnp.where(hm, vc, zero))
                outs.append((acc / l, jnp.broadcast_to(m + jnp.log(l), (Q_BLOCK, LANES))))
            o_ref[pl.ds(r0, Q_BLOCK), :] = (outs[0][0] + outs[1][0]).astype(BF16)
            lse_ref[pl.ds(r0, Q_BLOCK), :] = jnp.where(left, outs[0][1], outs[1][1])
            return 0

        lax.fori_loop(0, nblk, step, 0)

    blk = lambda off: _full((s, 128), lambda p, off=off: (0, off + p))
    return pl.pallas_call(
        body, name=name, grid=(4,),
        in_specs=[blk(qo), blk(ko), blk(vo)],
        out_specs=[blk(0), blk(0)],
        out_shape=[jax.ShapeDtypeStruct((s, 512), BF16), jax.ShapeDtypeStruct((s, 512), F32)],
        compiler_params=_cparams(("parallel",)),
    )(qa, ka, va)


def _merge_fwd(proj, o_mla, od, lsed, wp_mla, wp_dil):
    s = proj.shape[0]
    tm = min(512, s)

    def body(zm_ref, zd_ref, gm_ref, gd_ref, om_ref, o0, o1, o2, l0, l1, l2, wm_ref, wd_ref,
             mg_out, ya_out, yd_out, odil_out, lse_out):
        la, lb, lc = l0[...], l1[...], l2[...]
        lmax = jnp.maximum(jnp.maximum(la, lb), lc)
        ea, eb, ec = jnp.exp(la - lmax), jnp.exp(lb - lmax), jnp.exp(lc - lmax)
        den = ea + eb + ec
        o_dil = (ea * o0[...].astype(F32) + eb * o1[...].astype(F32) + ec * o2[...].astype(F32)) / den
        o_dil = o_dil.astype(BF16)
        odil_out[...] = o_dil
        lse_out[...] = lmax + jnp.log(den)
        zm, zd = zm_ref[...].astype(F32), zd_ref[...].astype(F32)
        pa = (om_ref[...].astype(F32) * (zm * _sigmoid(zm))).astype(BF16)
        pd = (o_dil.astype(F32) * (zd * _sigmoid(zd))).astype(BF16)
        ya = _dot(pa, wm_ref[...])
        yd = _dot(pd, wd_ref[...])
        ya_out[...] = ya.astype(BF16)
        yd_out[...] = yd.astype(BF16)
        mg_out[...] = (_sigmoid(gm_ref[...].astype(F32)) * ya + _sigmoid(gd_ref[...].astype(F32)) * yd).astype(BF16)

    row = lambda i: (i, 0)
    cst = lambda i: (0, 0)
    r512 = _full((tm, 512), row)
    r1024 = _full((tm, 1024), row)
    return pl.pallas_call(
        body, name="merge_fwd", grid=(s // tm,),
        in_specs=[_full((tm, 512), lambda i: (i, Z_MLA // 512)), _full((tm, 512), lambda i: (i, Z_DIL // 512)),
                  _full((tm, 1024), lambda i: (i, G_MLA // 1024)), _full((tm, 1024), lambda i: (i, G_DIL // 1024)),
                  r512, r512, r512, r512, r512, r512, r512, _full((512, 1024), cst), _full((512, 1024), cst)],
        out_specs=[r1024, r1024, r1024, r512, r512],
        out_shape=[jax.ShapeDtypeStruct((s, 1024), BF16), jax.ShapeDtypeStruct((s, 1024), BF16),
                   jax.ShapeDtypeStruct((s, 1024), BF16), jax.ShapeDtypeStruct((s, 512), BF16),
                   jax.ShapeDtypeStruct((s, 512), F32)],
        compiler_params=_cparams(("parallel",)),
    )(proj, proj, proj, proj, o_mla, *od, *lsed, wp_mla, wp_dil)


def _out_loss(merged, w_out, x, target, gpost):
    s = x.shape[0]
    tm = min(512, s)

    def body(mg_ref, w_ref, x_ref, t_ref, g_ref, do_out, dy_out, loss_out, dg_out):
        i = pl.program_id(0)

        @pl.when(i == 0)
        def _():
            loss_out[...] = jnp.zeros_like(loss_out)
            dg_out[...] = jnp.zeros_like(dg_out)

        o = _dot(mg_ref[...], w_ref[...])
        g = g_ref[...]
        n, u, r = _rms(o, g)
        e = (x_ref[...] + n) - t_ref[...]
        loss_out[...] += 0.5 * jnp.sum(jnp.mean(e * e, axis=-1, keepdims=True))
        dy = e * (1.0 / D_MODEL)
        dy_out[...] = dy
        dg_out[...] += jnp.sum(dy * u, axis=0, keepdims=True)
        do_out[...] = _rms_bwd(dy, g, u, r).astype(BF16)

    row = lambda i: (i, 0)
    cst = lambda i: (0, 0)
    return pl.pallas_call(
        body, name="out_loss", grid=(s // tm,),
        in_specs=[_full((tm, 1024), row), _full((1024, 1024), cst), _full((tm, 1024), row), _full((tm, 1024), row),
                  _full((1, 1024), cst)],
        out_specs=[_full((tm, 1024), row), _full((tm, 1024), row), _full((8, LANES), cst), _full((1, 1024), cst)],
        out_shape=[jax.ShapeDtypeStruct((s, 1024), BF16), jax.ShapeDtypeStruct((s, 1024), F32),
                   jax.ShapeDtypeStruct((8, LANES), F32), jax.ShapeDtypeStruct((1, 1024), F32)],
        compiler_params=_cparams(("arbitrary",)),
    )(merged, w_out, x, target, gpost)


def _seg_sum64(x, ones_bd):
    hi = x.astype(BF16)
    lo = (x - hi.astype(F32)).astype(BF16)
    return _dot(hi, ones_bd) + _dot(lo, ones_bd)


def _merge_bwd(do, w_out, merged, proj, ya, yd, o_mla, o_dil, wp_mla, wp_dil):
    s = do.shape[0]
    tm = min(256, s)
    seg = jnp.arange(512) // 64
    ones_bd = (seg[:, None] == seg[None, :]).astype(BF16)

    def body(do_ref, wo_ref, mg_ref, zm_ref, zd_ref, gm_ref, gd_ref, ya_ref, yd_ref, om_ref, od_ref, wm_ref, wd_ref,
             bd_ref, dzm_out, dzd_out, dgm_out, dgd_out, dom_out, dod_out, dm_out, dd_out, dwo_out, dwm_out, dwd_out):
        i = pl.program_id(0)

        @pl.when(i == 0)
        def _():
            dwo_out[...] = jnp.zeros_like(dwo_out)
            dwm_out[...] = jnp.zeros_like(dwm_out)
            dwd_out[...] = jnp.zeros_like(dwd_out)

        dov = do_ref[...]
        dwo_out[...] += _dot_tn(mg_ref[...], dov)
        dm = _dot_nt(dov, wo_ref[...])
        for g_ref, y_ref, z_ref, o_ref, w_ref, dz_out, dg_out, dob_out, dd_o, dw_out in (
                (gm_ref, ya_ref, zm_ref, om_ref, wm_ref, dzm_out, dgm_out, dom_out, dm_out, dwm_out),
                (gd_ref, yd_ref, zd_ref, od_ref, wd_ref, dzd_out, dgd_out, dod_out, dd_out, dwd_out)):
            sg = _sigmoid(g_ref[...].astype(F32))
            dg_out[...] = (dm * y_ref[...].astype(F32) * sg * (1.0 - sg)).astype(BF16)
            dy = (dm * sg).astype(BF16)
            z = z_ref[...].astype(F32)
            sz = _sigmoid(z)
            silu = z * sz
            ob = o_ref[...].astype(F32)
            dw_out[...] += _dot_tn((ob * silu).astype(BF16), dy)
            dp = _dot_nt(dy, w_ref[...])
            dz_out[...] = (dp * ob * (sz * (1.0 + z * (1.0 - sz)))).astype(BF16)
            dob = dp * silu
            dob_out[...] = dob.astype(BF16)
            dd_o[...] = _seg_sum64(dob * ob, bd_ref[...])

    row = lambda i: (i, 0)
    cst = lambda i: (0, 0)
    r512 = _full((tm, 512), row)
    r1024 = _full((tm, 1024), row)
    return pl.pallas_call(
        body, name="merge_bwd", grid=(s // tm,),
        in_specs=[r1024, _full((1024, 1024), cst), r1024,
                  _full((tm, 512), lambda i: (i, Z_MLA // 512)), _full((tm, 512), lambda i: (i, Z_DIL // 512)),
                  _full((tm, 1024), lambda i: (i, G_MLA // 1024)), _full((tm, 1024), lambda i: (i, G_DIL // 1024)),
                  r1024, r1024, r512, r512, _full((512, 1024), cst), _full((512, 1024), cst), _full((512, 512), cst)],
        out_specs=[r512, r512, r1024, r1024, r512, r512, r512, r512,
                   _full((1024, 1024), cst), _full((512, 1024), cst), _full((512, 1024), cst)],
        out_shape=[jax.ShapeDtypeStruct((s, 512), BF16), jax.ShapeDtypeStruct((s, 512), BF16),
                   jax.ShapeDtypeStruct((s, 1024), BF16), jax.ShapeDtypeStruct((s, 1024), BF16),
                   jax.ShapeDtypeStruct((s, 512), BF16), jax.ShapeDtypeStruct((s, 512), BF16),
                   jax.ShapeDtypeStruct((s, 512), F32), jax.ShapeDtypeStruct((s, 512), F32),
                   jax.ShapeDtypeStruct((1024, 1024), F32), jax.ShapeDtypeStruct((512, 1024), F32),
                   jax.ShapeDtypeStruct((512, 1024), F32)],
        compiler_params=_cparams(("arbitrary",)),
    )(do, w_out, merged, proj, proj, proj, proj, ya, yd, o_mla, o_dil, wp_mla, wp_dil, ones_bd)


def _mla_bwd(q, k, v, do, lse, dd):
    s = q.shape[0]
    tq = tk = min(256, s)
    nq = s // tq

    def body(q_ref, k_ref, v_ref, do_ref, lse_ref, dd_ref, dq_out, dk_out, dv_out, dq_acc):
        left = _left_mask()
        causal = (lax.broadcasted_iota(jnp.int32, (tq, tk), 1) <= lax.broadcasted_iota(jnp.int32, (tq, tk), 0))
        dq_acc[...] = jnp.zeros_like(dq_acc)

        def kv_step(j, _):
            c0 = pl.multiple_of(j * tk, tk)
            vv = v_ref[pl.ds(c0, tk), :]
            dv_tot = jnp.zeros((tk, LANES), F32)
            for hh in range(2):
                hm = left if hh == 0 else jnp.logical_not(left)
                cols = slice(hh * 128, (hh + 1) * 128)
                kh = k_ref[pl.ds(c0, tk), cols]
                vm = jnp.where(hm, vv, jnp.zeros_like(vv))

                def q_step(i, carry, masked, hh=hh, hm=hm, cols=cols, kh=kh, vm=vm):
                    dk, dv = carry
                    r0 = pl.multiple_of(i * tq, tq)
                    qh = q_ref[pl.ds(r0, tq), cols]
                    dov = do_ref[pl.ds(r0, tq), :]
                    dom = jnp.where(hm, dov, jnp.zeros_like(dov))
                    lse_h = _expand_half(lse_ref[pl.ds(r0, tq), :], hh, left)
                    dd_h = _expand_half(dd_ref[pl.ds(r0, tq), :], hh, left)
                    sc = _dot_nt(qh, kh) * MLA_SCALE
                    p = jnp.exp(sc - _tile_lanes(lse_h, tk))
                    if masked:
                        p = jnp.where(causal, p, 0.0)
                    dp = _dot_nt(dom, vm)
                    ds = (p * (dp - _tile_lanes(dd_h, tk)) * MLA_SCALE).astype(BF16)
                    dv = dv + _dot_tn(p.astype(BF16), dom)
                    dk = dk + _dot_tn(ds, qh)
                    dq_acc[pl.ds(r0, tq), cols] += _dot(ds, kh)
                    return dk, dv

                init = (jnp.zeros((tk, LANES), F32), jnp.zeros((tk, LANES), F32))
                carry = q_step(j, init, True)
                dk, dv = lax.fori_loop(j + 1, nq, lambda i, c, f=q_step: f(i, c, False), carry)
                dk_out[pl.ds(c0, tk), cols] = dk.astype(BF16)
                dv_tot = dv_tot + dv
            dv_out[pl.ds(c0, tk), :] = dv_tot.astype(BF16)
            return 0

        lax.fori_loop(0, nq, kv_step, 0)
        dq_out[...] = dq_acc[...].astype(BF16)

    b256 = _full((s, 256), lambda p: (0, p))
    b128 = _full((s, 128), lambda p: (0, p))
    return pl.pallas_call(
        body, name="mla_bwd", grid=(4,),
        in_specs=[b256, b256, b128, b128, b128, b128],
        out_specs=[b256, b256, b128],
        out_shape=[jax.ShapeDtypeStruct((s, 1024), BF16), jax.ShapeDtypeStruct((s, 1024), BF16),
                   jax.ShapeDtypeStruct((s, 512), BF16)],
        scratch_shapes=[pltpu.VMEM((s, 256), F32)],
        compiler_params=_cparams(("parallel",)),
    )(q, k, v, do, lse, dd)


def _mla_prep_bwd(dq, dk, dv, proj, gq, gkv, w_uq, w_uk, w_uv, q_tab, k_tab):
    s = dq.shape[0]
    tm = min(256, s)

    def body(dq_ref, dk_ref, dv_ref, cq_ref, ckv_ref, gq_ref, gkv_ref, wq_ref, wk_ref, wv_ref,
             qc, qsp, qsm, kc, ksp, ksm,
             dcq_out, dkr_out, dckv_out, dwq_out, dwk_out, dwv_out, dgq_out, dgkv_out):
        i = pl.program_id(0)

        @pl.when(i == 0)
        def _():
            for r in (dwq_out, dwk_out, dwv_out, dgq_out, dgkv_out):
                r[...] = jnp.zeros_like(r)

        dqu = _unrope(dq_ref[...].astype(F32), _tile_lanes(qc[...], 1024), _tile_lanes(qsp[...], 1024),
                      _tile_lanes(qsm[...], 1024), 16).astype(BF16)
        gq = gq_ref[...]
        cqn, xh, r = _rms(cq_ref[...].astype(F32), gq)
        dwq_out[...] += _dot_tn(cqn.astype(BF16), dqu)
        dcqn = _dot_nt(dqu, wq_ref[...])
        dgq_out[...] += jnp.sum(dcqn * xh, axis=0, keepdims=True)
        dcq_out[...] = _rms_bwd(dcqn, gq, xh, r).astype(BF16)

        dkb = dk_ref[...]
        dkf = dkb.astype(F32)
        dsum = dkf[:, 0:128]
        for h in range(1, MLA_HEADS):
            dsum = dsum + dkf[:, h * 128:(h + 1) * 128]
        dkr_out[...] = _unrope(pltpu.roll(dsum, 64, 1), kc[...], ksp[...], ksm[...], 16).astype(BF16)

        dvb = dv_ref[...]
        gkv = gkv_ref[...]
        ckvn, xh2, r2 = _rms(ckv_ref[...].astype(F32), gkv)
        ckvn = ckvn.astype(BF16)
        dwk_out[...] += _dot_tn(ckvn, dkb)
        dwv_out[...] += _dot_tn(ckvn, dvb)
        dckvn = _dot_nt(dkb, wk_ref[...]) + _dot_nt(dvb, wv_ref[...])
        dgkv_out[...] += jnp.sum(dckvn * xh2, axis=0, keepdims=True)
        dckv_out[...] = _rms_bwd(dckvn, gkv, xh2, r2).astype(BF16)

    row = lambda i: (i, 0)
    cst = lambda i: (0, 0)
    tabs = [_full((tm, LANES), row)] * 6
    return pl.pallas_call(
        body, name="mla_prep_bwd", grid=(s // tm,),
        in_specs=[_full((tm, 1024), row), _full((tm, 1024), row), _full((tm, 512), row),
                  _full((tm, 384), lambda i: (i, CQ_OFF // 384)), _full((tm, 256), lambda i: (i, CKV_OFF // 256)),
                  _full((1, 384), cst), _full((1, 256), cst),
                  _full((384, 1024), cst), _full((256, 1024), cst), _full((256, 512), cst)] + tabs,
        out_specs=[_full((tm, 384), row), _full((tm, 128), row), _full((tm, 256), row),
                   _full((384, 1024), cst), _full((256, 1024), cst), _full((256, 512), cst),
                   _full((1, 384), cst), _full((1, 256), cst)],
        out_shape=[jax.ShapeDtypeStruct((s, 384), BF16), jax.ShapeDtypeStruct((s, 128), BF16),
                   jax.ShapeDtypeStruct((s, 256), BF16),
                   jax.ShapeDtypeStruct((384, 1024), F32), jax.ShapeDtypeStruct((256, 1024), F32),
                   jax.ShapeDtypeStruct((256, 512), F32),
                   jax.ShapeDtypeStruct((1, 384), F32), jax.ShapeDtypeStruct((1, 256), F32)],
        compiler_params=_cparams(("arbitrary",)),
    )(dq, dk, dv, proj, proj, gq, gkv, w_uq, w_uk, w_uv, *q_tab, *k_tab)


def _dil_bwd(qa, ka, va, qo, ko, vo, do, lse, dd, tabs, d, name):
    s = qa.shape[0]
    nblk = s // Q_BLOCK
    per_seq = nblk // d

    def body(q_ref, k_ref, v_ref, do_ref, lse_ref, dd_ref, c_ref, sp_ref, sm_ref, dq_out, dk_out, dv_out,
             dk_acc, dv_acc):
        left = _left_mask()
        dk_acc[...] = jnp.zeros_like(dk_acc)
        dv_acc[...] = jnp.zeros_like(dv_acc)

        def step(b, _):
            r0 = pl.multiple_of(b * Q_BLOCK, Q_BLOCK)
            p0 = pl.multiple_of(jnp.maximum(b - 1, 0) * Q_BLOCK, Q_BLOCK)
            pmask, cur_ok = _band_masks((b % per_seq) > 0)
            rows, prow = pl.ds(r0, Q_BLOCK), pl.ds(p0, Q_BLOCK)
            qb, dob = q_ref[rows, :], do_ref[rows, :]
            kp, kc, vp, vc = k_ref[prow, :], k_ref[rows, :], v_ref[prow, :], v_ref[rows, :]
            lse_b, dd_b = lse_ref[rows, :], dd_ref[rows, :]
            zero = jnp.zeros_like(qb)
            dq = jnp.zeros((Q_BLOCK, LANES), F32)
            dkp = jnp.zeros((Q_BLOCK, LANES), F32)
            dkc = jnp.zeros((Q_BLOCK, LANES), F32)
            dvp = jnp.zeros((Q_BLOCK, LANES), F32)
            dvc = jnp.zeros((Q_BLOCK, LANES), F32)
            for hh in range(2):
                hm = left if hh == 0 else jnp.logical_not(left)
                qm, dom = jnp.where(hm, qb, zero), jnp.where(hm, dob, zero)
                lse_h, dd_h = _expand_half(lse_b, hh, left), _expand_half(dd_b, hh, left)
                pp = jnp.where(pmask, jnp.exp(_dot_nt(qm, kp) * DIL_SCALE - lse_h), 0.0)
                pc = jnp.where(cur_ok, jnp.exp(_dot_nt(qm, kc) * DIL_SCALE - lse_h), 0.0)
                dsp = (pp * (_dot_nt(dom, vp) - dd_h) * DIL_SCALE).astype(BF16)
                dsc = (pc * (_dot_nt(dom, vc) - dd_h) * DIL_SCALE).astype(BF16)
                dq = dq + _dot(dsp, jnp.where(hm, kp, zero)) + _dot(dsc, jnp.where(hm, kc, zero))
                dkp, dkc = dkp + _dot_tn(dsp, qm), dkc + _dot_tn(dsc, qm)
                dvp, dvc = dvp + _dot_tn(pp.astype(BF16), dom), dvc + _dot_tn(pc.astype(BF16), dom)
            dq_out[rows, :] = _unrope(dq, c_ref[rows, :], sp_ref[rows, :], sm_ref[rows, :], 8).astype(BF16)
            dk_acc[prow, :] += dkp
            dv_acc[prow, :] += dvp
            dk_acc[rows, :] += dkc
            dv_acc[rows, :] += dvc
            return 0

        lax.fori_loop(0, nblk, step, 0)
        dk_out[...] = _unrope(dk_acc[...], c_ref[...], sp_ref[...], sm_ref[...], 8).astype(BF16)
        dv_out[...] = dv_acc[...].astype(BF16)

    blk = lambda off: _full((s, 128), lambda p, off=off: (0, off + p))
    tab = _full((s, 128), lambda p: (0, 0))
    return pl.pallas_call(
        body, name=name, grid=(4,),
        in_specs=[blk(qo), blk(ko), blk(vo), blk(0), blk(0), blk(0), tab, tab, tab],
        out_specs=[blk(0), blk(0), blk(0)],
        out_shape=[jax.ShapeDtypeStruct((s, 512), BF16)] * 3,
        scratch_shapes=[pltpu.VMEM((s, 128), F32), pltpu.VMEM((s, 128), F32)],
        compiler_params=_cparams(("parallel",)),
    )(qa, ka, va, do, lse, dd, *tabs)


def _dh_bwd(dproj, w_pad, x, gpre, dy):
    s = x.shape[0]
    tm, tk = min(512, s), 1408
    nk = IN_PAD // tk

    def body(dp_ref, w_ref, x_ref, g_ref, dy_ref, gx_out, dg_out, acc):
        i, kk = pl.program_id(0), pl.program_id(1)

        @pl.when(jnp.logical_and(i == 0, kk == 0))
        def _():
            dg_out[...] = jnp.zeros_like(dg_out)

        @pl.when(kk == 0)
        def _():
            acc[...] = jnp.zeros_like(acc)

        acc[...] += _dot_nt(dp_ref[...], w_ref[...])

        @pl.when(kk == nk - 1)
        def _():
            g = g_ref[...]
            _, xh, r = _rms(x_ref[...], g)
            dh = acc[...]
            dg_out[...] += jnp.sum(dh * xh, axis=0, keepdims=True)
            gx_out[...] = dy_ref[...] + _rms_bwd(dh, g, xh, r)

    row = lambda i, k: (i, 0)
    return pl.pallas_call(
        body, name="dh_bwd", grid=(s // tm, nk),
        in_specs=[_full((tm, tk), lambda i, k: (i, k)), _full((1024, tk), lambda i, k: (0, k)),
                  _full((tm, 1024), row), _full((1, 1024), lambda i, k: (0, 0)), _full((tm, 1024), row)],
        out_specs=[_full((tm, 1024), row), _full((1, 1024), lambda i, k: (0, 0))],
        out_shape=[jax.ShapeDtypeStruct((s, 1024), F32), jax.ShapeDtypeStruct((1, 1024), F32)],
        scratch_shapes=[pltpu.VMEM((tm, 1024), F32)],
        compiler_params=_cparams(("arbitrary", "arbitrary")),
    )(dproj, w_pad, x, gpre, dy)


def _dw_in(h, dproj):
    s = h.shape[0]
    ts, tn = min(1024, s), 768
    ns = s // ts

    def body(h_ref, dp_ref, o_ref):
        @pl.when(pl.program_id(1) == 0)
        def _():
            o_ref[...] = jnp.zeros_like(o_ref)

        o_ref[...] += _dot_tn(h_ref[...], dp_ref[...])

    return pl.pallas_call(
        body, name="dw_in", grid=(IN_PAD // tn, ns),
        in_specs=[_full((ts, 1024), lambda j, k: (k, 0)), _full((ts, tn), lambda j, k: (k, j))],
        out_specs=_full((1024, tn), lambda j, k: (0, j)),
        out_shape=jax.ShapeDtypeStruct((1024, IN_PAD), F32),
        compiler_params=_cparams(("parallel", "arbitrary")),
    )(h, dproj)


def _exchange(big, small, name):
    per_peer = big.shape[0] == N_DEV
    n_small = 0 if small is None else 1

    def body(*refs):
        if small is None:
            big_ref, big_out, sems = refs
            small_ref = small_out = None
        else:
            big_ref, small_ref, big_out, small_out, sems = refs
        x, y, c = lax.axis_index("x"), lax.axis_index("y"), lax.axis_index("c")
        me = 4 * x + 2 * y + c
        flip = lambda v, b: (1 - v) if b else v

        def copies(j, outgoing):
            jx, jy, jc = (j >> 2) & 1, (j >> 1) & 1, j & 1
            peer = (flip(x, jx), flip(y, jy), flip(c, jc))
            peer_idx = 4 * peer[0] + 2 * peer[1] + peer[2]
            slot = me if outgoing else peer_idx
            src = big_ref.at[peer_idx] if per_peer else big_ref.at[0]
            out = [pltpu.make_async_remote_copy(src_ref=src, dst_ref=big_out.at[slot], send_sem=sems.at[0, j],
                                                recv_sem=sems.at[1, j], device_id=peer,
                                                device_id_type=pl.DeviceIdType.MESH)]
            if small is not None:
                out.append(pltpu.make_async_remote_copy(src_ref=small_ref, dst_ref=small_out.at[slot],
                                                        send_sem=sems.at[2, j], recv_sem=sems.at[3, j],
                                                        device_id=peer, device_id_type=pl.DeviceIdType.MESH))
            return out

        own = [pltpu.make_async_copy(big_ref.at[me] if per_peer else big_ref.at[0], big_out.at[me], sems.at[0, 0])]
        if small is not None:
            own.append(pltpu.make_async_copy(small_ref, small_out.at[me], sems.at[2, 0]))
        for cp in own:
            cp.start()
        sent = [cp for j in range(1, N_DEV) for cp in copies(j, True)]
        for cp in sent:
            cp.start()
        for j in range(1, N_DEV):
            for cp in copies(j, False):
                cp.wait_recv()
        for cp in sent:
            cp.wait_send()
        for cp in own:
            cp.wait()

    hbm = pl.BlockSpec(memory_space=pl.ANY)
    rows = big.shape[1]
    out_shape = [jax.ShapeDtypeStruct((N_DEV, rows, LANES), big.dtype)]
    args = [big]
    if small is not None:
        out_shape.append(jax.ShapeDtypeStruct((N_DEV,) + small.shape, small.dtype))
        args.append(small)
    res = pl.pallas_call(
        body, name=name, in_specs=[hbm] * len(args), out_specs=[hbm] * len(out_shape), out_shape=out_shape,
        scratch_shapes=[pltpu.SemaphoreType.DMA((2 + 2 * n_small, N_DEV))],
    )(*args)
    return res


def _adam_math(w, g, m, v):
    m = ADAM_B1 * m + (1.0 - ADAM_B1) * g
    v = ADAM_B2 * v + (1.0 - ADAM_B2) * (g * g)
    m_hat = m / (1.0 - ADAM_B1 ** ADAM_STEP)
    v_hat = v / (1.0 - ADAM_B2 ** ADAM_STEP)
    delta = -ADAM_LR * (m_hat / (jnp.sqrt(v_hat) + ADAM_EPS) + ADAM_WD * w)
    return delta, m, v


def _adam(recv, w, m, v, name, tr):
    rows = w.shape[0]

    def body(r_ref, w_ref, m_ref, v_ref, g_out, d_out, m_out, v_out):
        g = r_ref[0].astype(F32)
        for k in range(1, N_DEV):
            g = g + r_ref[k].astype(F32)
        g_out[...] = g
        d_out[...], m_out[...], v_out[...] = _adam_math(w_ref[...], g, m_ref[...], v_ref[...])

    row = _full((tr, LANES), lambda i: (i, 0))
    return pl.pallas_call(
        body, name=name, grid=(rows // tr,),
        in_specs=[_full((N_DEV, tr, LANES), lambda i: (0, i, 0)), row, row, row],
        out_specs=[row] * 4,
        out_shape=[jax.ShapeDtypeStruct((rows, LANES), F32)] * 4,
        compiler_params=_cparams(("parallel",)),
    )(recv, w, m, v)


def _local_step(x, positions, gains, weights, target):
    gpre, gq, gkv, gpost = gains
    w_pad, w_uq, w_uk, w_uv, wp_mla, wp_dil, w_out = weights
    q_tab, k_tab, d_tab = _rope_tables(positions)

    proj, h = _inproj(x, gpre, w_pad, d_tab)
    q, k, v = _mla_prep(proj, gq, gkv, w_uq, w_uk, w_uv, q_tab, k_tab)
    o_mla, lse_mla = _mla_fwd(q, k, v)

    qkv_perm, od, lsed = [], [], []
    for g, d in enumerate(DIL_DILATIONS):
        if d == 1:
            arr, offs = proj, ((Q_OFF + 512 * g) // 128, (K_OFF + 512 * g) // 128, (V_OFF + 512 * g) // 128)
        else:
            cols = jnp.concatenate([proj[:, o + 512 * g:o + 512 * (g + 1)] for o in (Q_OFF, K_OFF, V_OFF)], axis=1)
            arr, offs = _perm(cols, d), (0, 4, 8)
        qkv_perm.append((arr, offs))
        o_g, lse_g = _dil_fwd(arr, arr, arr, *offs, d, "dil_fwd_%d" % g)
        od.append(_unperm(o_g, d))
        lsed.append(_unperm(lse_g, d))

    merged, ya, yd, o_dil, lse_dil = _merge_fwd(proj, o_mla, od, lsed, wp_mla, wp_dil)
    do, dy, loss, dgpost = _out_loss(merged, w_out, x, target, gpost)

    (dz_mla, dz_dil, dg_mla, dg_dil, do_mla, do_dil, dd_mla, dd_dil, dw_out, dwp_mla, dwp_dil) = _merge_bwd(
        do, w_out, merged, proj, ya, yd, o_mla, o_dil, wp_mla, wp_dil)

    dq, dk, dv = _mla_bwd(q, k, v, do_mla, lse_mla, dd_mla)
    dcq, dkr, dckv, dw_uq, dw_uk, dw_uv, dgq, dgkv = _mla_prep_bwd(dq, dk, dv, proj, gq, gkv, w_uq, w_uk, w_uv,
                                                                   q_tab, k_tab)

    dqs, dks, dvs = [], [], []
    for g, d in enumerate(DIL_DILATIONS):
        arr, offs = qkv_perm[g]
        tabs = tuple(_perm(t, d) for t in d_tab)
        dq_g, dk_g, dv_g = _dil_bwd(arr, arr, arr, *offs, _perm(do_dil, d), _perm(lse_dil, d), _perm(dd_dil, d),
                                    tabs, d, "dil_bwd_%d" % g)
        dqs.append(_unperm(dq_g, d))
        dks.append(_unperm(dk_g, d))
        dvs.append(_unperm(dv_g, d))

    dproj = jnp.concatenate([dz_mla, dz_dil, dg_mla, dg_dil] + dqs + dks + dvs + [dcq, dkr, dckv], axis=1)
    grad_x, dgpre = _dh_bwd(dproj, w_pad, x, gpre, dy)
    dw_in = _dw_in(h, dproj)
    return loss, grad_x, (dgpre, dgq, dgkv, dgpost), (dw_in, dw_uq, dw_uk, dw_uv, dwp_mla, dwp_dil, dw_out)


def _pack_small(parts, loss):
    rows = [p.reshape(-1, LANES) for p in parts] + [loss[0:1]]
    n = sum(r.shape[0] for r in rows)
    return jnp.concatenate(rows + [jnp.zeros((SMALL_ROWS - n, LANES), F32)], axis=0)


def _pack_gains(parts):
    rows = [p.reshape(-1, LANES) for p in parts]
    n = sum(r.shape[0] for r in rows)
    return jnp.concatenate(rows + [jnp.zeros((SMALL_ROWS - n, LANES), F32)], axis=0)


def _unpack_gains(buf):
    return (buf[0:8].reshape(1, 1024), buf[8:11].reshape(1, 384), buf[11:13].reshape(1, 256),
            buf[13:21].reshape(1, 1024))


def kernel(x, positions, pre_norm_g, w_in, q_norm_g, w_uq, kv_norm_g, w_ukv, w_proj_mla, w_proj_dil, w_out, post_norm_g, loss_target, m_pre_norm_g, m_w_in, m_q_norm_g, m_w_uq, m_kv_norm_g, m_w_ukv, m_w_proj_mla, m_w_proj_dil, m_w_out, m_post_norm_g, v_pre_norm_g, v_w_in, v_q_norm_g, v_w_uq, v_kv_norm_g, v_w_ukv, v_w_proj_mla, v_w_proj_dil, v_w_out, v_post_norm_g):
    big_w = (w_in, w_uq, w_ukv, w_proj_mla, w_proj_dil, w_out)
    big_m = (m_w_in, m_w_uq, m_w_ukv, m_w_proj_mla, m_w_proj_dil, m_w_out)
    big_v = (v_w_in, v_w_uq, v_w_ukv, v_w_proj_mla, v_w_proj_dil, v_w_out)
    gains = (pre_norm_g, q_norm_g, kv_norm_g, post_norm_g)
    gains_m = (m_pre_norm_g, m_q_norm_g, m_kv_norm_g, m_post_norm_g)
    gains_v = (v_pre_norm_g, v_q_norm_g, v_kv_norm_g, v_post_norm_g)

    gathered, = _exchange(_pack_shards(big_w, BF16)[None], None, "gather_weights")
    weights = _unpack_gathered(gathered)

    loss, grad_x, dgains, dweights = _local_step(x[0], positions[0], gains, weights, loss_target[0])

    chunks = _pack_grads(*dweights).astype(BF16)
    recv_big, recv_small = _exchange(chunks, _pack_small(dgains, loss), "exchange_grads")

    g_big, d_big, m_big, v_big = _adam(recv_big, _pack_shards(big_w, F32), _pack_shards(big_m, F32),
                                       _pack_shards(big_v, F32), "adam_weights", 912)
    g_sm, d_sm, m_sm, v_sm = _adam(recv_small, _pack_gains(gains), _pack_gains(gains_m), _pack_gains(gains_v),
                                   "adam_gains", SMALL_ROWS)

    def interleave(small_buf, big_buf):
        s_pre, s_q, s_kv, s_post = _unpack_gains(small_buf)
        b_in, b_uq, b_ukv, b_pm, b_pd, b_out = _unpack_shards(big_buf)
        return [s_pre, b_in, s_q, b_uq, s_kv, b_ukv, b_pm, b_pd, b_out, s_post]

    total_loss = g_sm[21, 0]
    return (total_loss, grad_x[None], *interleave(g_sm, g_big), *interleave(d_sm, d_big),
            *interleave(m_sm, m_big), *interleave(v_sm, v_big))
```

```python
import jax
import jax.numpy as jnp
from jax import lax
from jax.experimental import pallas as pl
from jax.experimental.pallas import tpu as pltpu

F32 = jnp.float32
BF16 = jnp.bfloat16

D_MODEL = 1024
NORM_EPS = 1e-6
ROPE_THETA = 500000.0
N_DEV = 8
LANES = 128
NEG = -1e30

MLA_HEADS = 8
MLA_Q_RANK = 384
MLA_KV_RANK = 256
MLA_SCALE = 96.0 ** -0.5
DIL_DILATIONS = (1, 4, 16)
DIL_SCALE = 0.125
Q_BLOCK = 128

Z_MLA, Z_DIL, G_MLA, G_DIL = 0, 512, 1024, 2048
Q_OFF, K_OFF, V_OFF = 3072, 4608, 6144
CQ_OFF, KR_OFF, CKV_OFF, IN_PAD = 7680, 8064, 8192, 8448
IN_WIDTH = 8352
IN_SEGS = ((0, 384, CQ_OFF), (384, 256, CKV_OFF), (640, 32, KR_OFF), (672, 1536, Q_OFF), (2208, 1536, K_OFF),
           (3744, 1536, V_OFF), (5280, 512, Z_MLA), (5792, 512, Z_DIL), (6304, 1024, G_MLA), (7328, 1024, G_DIL))

PACK_ROWS = (8352, 288, 256, 512, 512, 1024)
PACK_TOTAL = sum(PACK_ROWS)
SMALL_ROWS = 24

ADAM_LR, ADAM_B1, ADAM_B2, ADAM_EPS, ADAM_WD, ADAM_STEP = 0.001, 0.9, 0.999, 1e-08, 0.01, 10

VMEM_LIMIT_MB = 56


def _cparams(sem=None, vmem_mb=VMEM_LIMIT_MB):
    return pltpu.CompilerParams(dimension_semantics=sem, vmem_limit_bytes=vmem_mb * 1024 * 1024)


def _dot(a, b):
    return jnp.dot(a, b, preferred_element_type=F32)


def _dot_nt(a, b):
    return lax.dot_general(a, b, (((1,), (1,)), ((), ())), preferred_element_type=F32)


def _dot_tn(a, b):
    return lax.dot_general(a, b, (((0,), (0,)), ((), ())), preferred_element_type=F32)


def _tile_lanes(t, width):
    return t if width == t.shape[1] else jnp.tile(t, (1, width // t.shape[1]))


def _rope(x, c, sp, sm, a):
    n = x.shape[1]
    return x * c + pltpu.roll(x, a, 1) * sp + pltpu.roll(x, n - a, 1) * sm


def _unrope(dy, c, sp, sm, a):
    n = dy.shape[1]
    return dy * c + pltpu.roll(dy * sp, n - a, 1) + pltpu.roll(dy * sm, a, 1)


def _sigmoid(z):
    return 1.0 / (1.0 + jnp.exp(-z))


def _left_mask():
    return lax.broadcasted_iota(jnp.int32, (1, LANES), 1) < 64


def _expand_half(x, hh, left):
    r = pltpu.roll(x, 64, 1)
    return jnp.where(left, x, r) if hh == 0 else jnp.where(left, r, x)


def _rms(xv, g):
    r = lax.rsqrt(jnp.mean(xv * xv, axis=-1, keepdims=True) + NORM_EPS)
    xh = xv * r
    return xh * g, xh, r


def _rms_bwd(dout, g, xh, r):
    dxh = dout * g
    return r * (dxh - xh * jnp.mean(dxh * xh, axis=-1, keepdims=True))


def _full(shape, index_map):
    return pl.BlockSpec(shape, index_map)


def _pack_shards(parts, dtype):
    return jnp.concatenate([p.astype(dtype).reshape(-1, LANES) for p in parts], axis=0)


def _unpack_shards(buf):
    shapes = ((1, 1024, 1044), (1, 384, 96), (1, 256, 128), (1, 512, 128), (1, 512, 128), (1, 128, 1024))
    out, r0 = [], 0
    for rows, shp in zip(PACK_ROWS, shapes):
        out.append(buf[r0:r0 + rows].reshape(shp))
        r0 += rows
    return out


def _pad_w_in(w):
    parts = []
    for s, n, off in sorted(IN_SEGS, key=lambda t: t[2]):
        seg = w[:, s:s + n]
        if off == KR_OFF:
            seg = jnp.pad(seg, ((0, 0), (0, LANES - n)))
        parts.append(seg)
    return jnp.concatenate(parts, axis=1)


def _unpad_dw_in(dw):
    return jnp.concatenate([dw[:, off:off + n] for s, n, off in IN_SEGS], axis=1)


def _unpack_gathered(g):
    r = [0]
    for n in PACK_ROWS:
        r.append(r[-1] + n)
    w_in = g[:, r[0]:r[1]].reshape(N_DEV, 1024, 1044).transpose(1, 0, 2).reshape(1024, IN_WIDTH)
    w_uq = g[:, r[1]:r[2]].reshape(N_DEV, 384, 96).transpose(1, 0, 2).reshape(384, 8, 96)
    w_ukv = g[:, r[2]:r[3]].reshape(N_DEV, 256, 128).transpose(1, 0, 2).reshape(256, 8, 128)
    wp_mla = g[:, r[3]:r[4]].reshape(N_DEV, 512, 128).transpose(1, 0, 2).reshape(512, 1024)
    wp_dil = g[:, r[4]:r[5]].reshape(N_DEV, 512, 128).transpose(1, 0, 2).reshape(512, 1024)
    w_out = g[:, r[5]:r[6]].reshape(1024, 1024)
    w_uq_pad = jnp.pad(w_uq, ((0, 0), (0, 0), (0, 32))).reshape(384, 1024)
    w_uk_pad = jnp.pad(w_ukv[:, :, :64], ((0, 0), (0, 0), (0, 64))).reshape(256, 1024)
    w_uv = w_ukv[:, :, 64:].reshape(256, 512)
    return _pad_w_in(w_in), w_uq_pad, w_uk_pad, w_uv, wp_mla, wp_dil, w_out


def _pack_grads(dw_in_pad, dw_uq_pad, dw_uk_pad, dw_uv, dwp_mla, dwp_dil, dw_out):
    a = _unpad_dw_in(dw_in_pad).reshape(1024, N_DEV, 1044).transpose(1, 0, 2).reshape(N_DEV, -1, LANES)
    b = dw_uq_pad.reshape(384, 8, 128)[:, :, :96].transpose(1, 0, 2).reshape(N_DEV, -1, LANES)
    ukv = jnp.concatenate([dw_uk_pad.reshape(256, 8, 128)[:, :, :64], dw_uv.reshape(256, 8, 64)], axis=2)
    c = ukv.transpose(1, 0, 2).reshape(N_DEV, -1, LANES)
    d = dwp_mla.reshape(512, N_DEV, 128).transpose(1, 0, 2).reshape(N_DEV, -1, LANES)
    e = dwp_dil.reshape(512, N_DEV, 128).transpose(1, 0, 2).reshape(N_DEV, -1, LANES)
    f = dw_out.reshape(N_DEV, -1, LANES)
    return jnp.concatenate([a, b, c, d, e, f], axis=1)


def _rope_tables(pos):
    s = pos.shape[0]
    p = pos.astype(F32)[:, None]
    am = p * (ROPE_THETA ** (-jnp.arange(0, 32, 2, dtype=F32) / 32))
    ad = p * (ROPE_THETA ** (-jnp.arange(0, 16, 2, dtype=F32) / 16))
    cm, sm = jnp.cos(am), jnp.sin(am)
    cd, sd = jnp.cos(ad), jnp.sin(ad)
    z = lambda n: jnp.zeros((s, n), F32)
    o = lambda n: jnp.ones((s, n), F32)
    cat = lambda xs: jnp.concatenate(xs, axis=1)
    q_tab = (cat([o(64), cm, cm, z(32)]), cat([z(80), sm, z(32)]), cat([z(64), -sm, z(48)]))
    k_tab = (cat([cm, cm, z(96)]), cat([z(16), sm, z(96)]), cat([-sm, z(112)]))
    d1 = (cat([cd, cd, o(48)]), cat([z(8), sd, z(48)]), cat([-sd, z(56)]))
    d_tab = tuple(cat([t, t]) for t in d1)
    return q_tab, k_tab, d_tab


def _perm(a, d):
    if d == 1:
        return a
    s, c = a.shape
    return a.reshape(s // d, d, c).transpose(1, 0, 2).reshape(s, c)


def _unperm(a, d):
    if d == 1:
        return a
    s, c = a.shape
    return a.reshape(d, s // d, c).transpose(1, 0, 2).reshape(s, c)


def _inproj(x, gpre, w_pad, d_tab):
    s = x.shape[0]
    tm, tn = min(1024, s), 768
    rope_lo, rope_hi = Q_OFF // tn, V_OFF // tn

    def body(x_ref, g_ref, w_ref, c_ref, sp_ref, sm_ref, o_ref, h_ref):
        j = pl.program_id(1)

        @pl.when(j == 0)
        def _():
            hv, _, _ = _rms(x_ref[...], g_ref[...])
            h_ref[...] = hv.astype(BF16)

        acc = _dot(h_ref[...], w_ref[...])
        is_rope = jnp.logical_and(j >= rope_lo, j < rope_hi)

        @pl.when(is_rope)
        def _():
            o_ref[...] = _rope(acc, _tile_lanes(c_ref[...], tn), _tile_lanes(sp_ref[...], tn),
                               _tile_lanes(sm_ref[...], tn), 8).astype(BF16)

        @pl.when(jnp.logical_not(is_rope))
        def _():
            o_ref[...] = acc.astype(BF16)

    row = lambda i, j: (i, 0)
    return pl.pallas_call(
        body, name="inproj", grid=(s // tm, IN_PAD // tn),
        in_specs=[_full((tm, D_MODEL), row), _full((1, D_MODEL), lambda i, j: (0, 0)),
                  _full((D_MODEL, tn), lambda i, j: (0, j)),
                  _full((tm, LANES), row), _full((tm, LANES), row), _full((tm, LANES), row)],
        out_specs=[_full((tm, tn), lambda i, j: (i, j)), _full((tm, D_MODEL), row)],
        out_shape=[jax.ShapeDtypeStruct((s, IN_PAD), BF16), jax.ShapeDtypeStruct((s, D_MODEL), BF16)],
        compiler_params=_cparams(("parallel", "arbitrary")),
    )(x, gpre, w_pad, *d_tab)


def _mla_prep(proj, gq, gkv, w_uq, w_uk, w_uv, q_tab, k_tab):
    s = proj.shape[0]
    tm = min(512, s)

    def body(cq_ref, kr_ref, ckv_ref, gq_ref, gkv_ref, wq_ref, wk_ref, wv_ref,
             qc, qsp, qsm, kc, ksp, ksm, q_out, k_out, v_out, qt_out, kt_out, vt_out):
        cqn, _, _ = _rms(cq_ref[...].astype(F32), gq_ref[...])
        q = _dot(cqn.astype(BF16), wq_ref[...])
        q = _rope(q, _tile_lanes(qc[...], 1024), _tile_lanes(qsp[...], 1024), _tile_lanes(qsm[...], 1024), 16)
        q_out[...] = q.astype(BF16)
        qt_out[...] = q.T.astype(BF16)
        ckvn, _, _ = _rms(ckv_ref[...].astype(F32), gkv_ref[...])
        ckvn = ckvn.astype(BF16)
        kr = _rope(kr_ref[...].astype(F32), kc[...], ksp[...], ksm[...], 16)
        k = _dot(ckvn, wk_ref[...]) + _tile_lanes(pltpu.roll(kr, 64, 1), 1024)
        k_out[...] = k.astype(BF16)
        kt_out[...] = k.T.astype(BF16)
        v = _dot(ckvn, wv_ref[...])
        v_out[...] = v.astype(BF16)
        vt_out[...] = v.T.astype(BF16)

    row = lambda i: (i, 0)
    col = lambda i: (0, i)
    cst = lambda i: (0, 0)
    tabs = [_full((tm, LANES), row)] * 6
    return pl.pallas_call(
        body, name="mla_prep", grid=(s // tm,),
        in_specs=[_full((tm, 384), lambda i: (i, CQ_OFF // 384)), _full((tm, 128), lambda i: (i, KR_OFF // 128)),
                  _full((tm, 256), lambda i: (i, CKV_OFF // 256)), _full((1, 384), cst), _full((1, 256), cst),
                  _full((384, 1024), cst), _full((256, 1024), cst), _full((256, 512), cst)] + tabs,
        out_specs=[_full((tm, 1024), row), _full((tm, 1024), row), _full((tm, 512), row),
                   _full((1024, tm), col), _full((1024, tm), col), _full((512, tm), col)],
        out_shape=[jax.ShapeDtypeStruct((s, 1024), BF16), jax.ShapeDtypeStruct((s, 1024), BF16),
                   jax.ShapeDtypeStruct((s, 512), BF16), jax.ShapeDtypeStruct((1024, s), BF16),
                   jax.ShapeDtypeStruct((1024, s), BF16), jax.ShapeDtypeStruct((512, s), BF16)],
        compiler_params=_cparams(("parallel",)),
    )(proj, proj, proj, gq, gkv, w_uq, w_uk, w_uv, *q_tab, *k_tab)


def _mla_fwd(q, k, vt):
    s = q.shape[0]
    tq = tk = min(256, s)
    nq = s // tq

    def body(q_ref, k_ref, vt_ref, o_ref, ot_ref, lse_ref):
        causal = (lax.broadcasted_iota(jnp.int32, (tk, tq), 0) <= lax.broadcasted_iota(jnp.int32, (tk, tq), 1))

        def q_step(i, _):
            r0 = pl.multiple_of(i * tq, tq)
            qs = [q_ref[pl.ds(r0, tq), hh * 128:(hh + 1) * 128] for hh in range(2)]

            def scores(j):
                c0 = pl.multiple_of(j * tk, tk)
                return tuple(_dot_nt(k_ref[pl.ds(c0, tk), hh * 128:(hh + 1) * 128], qs[hh]) * MLA_SCALE
                             for hh in range(2))

            def update(j, sts, stats, masked):
                c0 = pl.multiple_of(j * tk, tk)
                new = []
                for hh in range(2):
                    m, l, acc = stats[hh]
                    st = jnp.where(causal, sts[hh], NEG) if masked else sts[hh]
                    m_new = jnp.maximum(m, jnp.max(st, axis=0, keepdims=True))
                    alpha = jnp.exp(m - m_new)
                    p = jnp.exp(st - m_new)
                    l = alpha * l + jnp.sum(p, axis=0, keepdims=True)
                    acc = acc * alpha + _dot(vt_ref[hh * 64:(hh + 1) * 64, pl.ds(c0, tk)], p.astype(BF16))
                    new.append((m_new, l, acc))
                return tuple(new)

            def kv_step(j, carry):
                sts, stats = carry
                nxt = scores(j + 1)
                return nxt, update(j, sts, stats, False)

            init = tuple((jnp.full((1, tq), NEG, F32), jnp.zeros((1, tq), F32), jnp.zeros((64, tq), F32))
                         for _ in range(2))
            sts, stats = lax.fori_loop(0, i, kv_step, (scores(0), init))
            (ma, la, acca), (mb, lb, accb) = update(i, sts, stats, True)
            ot = jnp.concatenate([acca / la, accb / lb], axis=0)
            ot_ref[:, pl.ds(r0, tq)] = ot.astype(BF16)
            o_ref[pl.ds(r0, tq), :] = ot.T.astype(BF16)
            lse_ref[:, pl.ds(r0, tq)] = jnp.concatenate(
                [ma + jnp.log(la), mb + jnp.log(lb), jnp.zeros((6, tq), F32)], axis=0)
            return 0

        lax.fori_loop(0, nq, q_step, 0)

    return pl.pallas_call(
        body, name="mla_fwd", grid=(4,),
        in_specs=[_full((s, 256), lambda p: (0, p)), _full((s, 256), lambda p: (0, p)),
                  _full((128, s), lambda p: (p, 0))],
        out_specs=[_full((s, 128), lambda p: (0, p)), _full((128, s), lambda p: (p, 0)),
                   _full((8, s), lambda p: (p, 0))],
        out_shape=[jax.ShapeDtypeStruct((s, 512), BF16), jax.ShapeDtypeStruct((512, s), BF16),
                   jax.ShapeDtypeStruct((32, s), F32)],
        compiler_params=_cparams(("parallel",)),
    )(q, k, vt)


def _band_masks(has_prev):
    r = lax.broadcasted_iota(jnp.int32, (Q_BLOCK, Q_BLOCK), 0)
    c = lax.broadcasted_iota(jnp.int32, (Q_BLOCK, Q_BLOCK), 1)
    return c >= r + jnp.where(has_prev, 0, Q_BLOCK), c <= r


def _dil_fwd(qa, ka, va, qo, ko, vo, d, name):
    s = qa.shape[0]
    nblk = s // Q_BLOCK
    per_seq = nblk // d

    def body(q_ref, k_ref, v_ref, o_ref, lse_ref):
        left = _left_mask()

        hms = (left, jnp.logical_not(left))

        def scores(b):
            r0 = pl.multiple_of(b * Q_BLOCK, Q_BLOCK)
            p0 = pl.multiple_of(jnp.maximum(b - 1, 0) * Q_BLOCK, Q_BLOCK)
            qb = q_ref[pl.ds(r0, Q_BLOCK), :]
            kp, kc = k_ref[pl.ds(p0, Q_BLOCK), :], k_ref[pl.ds(r0, Q_BLOCK), :]
            out = []
            for hh in range(2):
                qm = jnp.where(hms[hh], qb, jnp.zeros_like(qb))
                out.append((_dot_nt(qm, kp) * DIL_SCALE, _dot_nt(qm, kc) * DIL_SCALE))
            return out

        def finish(b, tiles):
            r0 = pl.multiple_of(b * Q_BLOCK, Q_BLOCK)
            p0 = pl.multiple_of(jnp.maximum(b - 1, 0) * Q_BLOCK, Q_BLOCK)
            pmask, cur_ok = _band_masks((b % per_seq) > 0)
            vp, vc = v_ref[pl.ds(p0, Q_BLOCK), :], v_ref[pl.ds(r0, Q_BLOCK), :]
            zero = jnp.zeros_like(vp)
            outs = []
            for hh in range(2):
                sp = jnp.where(pmask, tiles[hh][0], NEG)
                sc = jnp.where(cur_ok, tiles[hh][1], NEG)
                m = jnp.maximum(jnp.max(sp, axis=1, keepdims=True), jnp.max(sc, axis=1, keepdims=True))
                pp, pc = jnp.exp(sp - m), jnp.exp(sc - m)
                l = jnp.sum(pp, axis=1, keepdims=True) + jnp.sum(pc, axis=1, keepdims=True)
                acc = (_dot(pp.astype(BF16), jnp.where(hms[hh], vp, zero))
                       + _dot(pc.astype(BF16), jnp.where(hms[hh], vc, zero)))
                outs.append((acc / l, jnp.broadcast_to(m + jnp.log(l), (Q_BLOCK, LANES))))
            o_ref[pl.ds(r0, Q_BLOCK), :] = (outs[0][0] + outs[1][0]).astype(BF16)
            lse_ref[pl.ds(r0, Q_BLOCK), :] = jnp.where(left, outs[0][1], outs[1][1])

        def step(t, _):
            ta, tb = scores(2 * t), scores(2 * t + 1)
            finish(2 * t, ta)
            finish(2 * t + 1, tb)
            return 0

        lax.fori_loop(0, nblk // 2, step, 0)

    blk = lambda off: _full((s, 128), lambda p, off=off: (0, off + p))
    return pl.pallas_call(
        body, name=name, grid=(4,),
        in_specs=[blk(qo), blk(ko), blk(vo)],
        out_specs=[blk(0), blk(0)],
        out_shape=[jax.ShapeDtypeStruct((s, 512), BF16), jax.ShapeDtypeStruct((s, 512), F32)],
        compiler_params=_cparams(("parallel",)),
    )(qa, ka, va)


def _merge_fwd(proj, o_mla, od, lsed, wp_mla, wp_dil):
    s = proj.shape[0]
    tm = min(512, s)

    def body(zm_ref, zd_ref, gm_ref, gd_ref, om_ref, o0, o1, o2, l0, l1, l2, wm_ref, wd_ref,
             mg_out, ya_out, yd_out, odil_out, lse_out):
        la, lb, lc = l0[...], l1[...], l2[...]
        lmax = jnp.maximum(jnp.maximum(la, lb), lc)
        ea, eb, ec = jnp.exp(la - lmax), jnp.exp(lb - lmax), jnp.exp(lc - lmax)
        den = ea + eb + ec
        o_dil = (ea * o0[...].astype(F32) + eb * o1[...].astype(F32) + ec * o2[...].astype(F32)) / den
        o_dil = o_dil.astype(BF16)
        odil_out[...] = o_dil
        lse_out[...] = lmax + jnp.log(den)
        zm, zd = zm_ref[...].astype(F32), zd_ref[...].astype(F32)
        pa = (om_ref[...].astype(F32) * (zm * _sigmoid(zm))).astype(BF16)
        pd = (o_dil.astype(F32) * (zd * _sigmoid(zd))).astype(BF16)
        ya = _dot(pa, wm_ref[...])
        yd = _dot(pd, wd_ref[...])
        ya_out[...] = ya.astype(BF16)
        yd_out[...] = yd.astype(BF16)
        mg_out[...] = (_sigmoid(gm_ref[...].astype(F32)) * ya + _sigmoid(gd_ref[...].astype(F32)) * yd).astype(BF16)

    row = lambda i: (i, 0)
    cst = lambda i: (0, 0)
    r512 = _full((tm, 512), row)
    r1024 = _full((tm, 1024), row)
    return pl.pallas_call(
        body, name="merge_fwd", grid=(s // tm,),
        in_specs=[_full((tm, 512), lambda i: (i, Z_MLA // 512)), _full((tm, 512), lambda i: (i, Z_DIL // 512)),
                  _full((tm, 1024), lambda i: (i, G_MLA // 1024)), _full((tm, 1024), lambda i: (i, G_DIL // 1024)),
                  r512, r512, r512, r512, r512, r512, r512, _full((512, 1024), cst), _full((512, 1024), cst)],
        out_specs=[r1024, r1024, r1024, r512, r512],
        out_shape=[jax.ShapeDtypeStruct((s, 1024), BF16), jax.ShapeDtypeStruct((s, 1024), BF16),
                   jax.ShapeDtypeStruct((s, 1024), BF16), jax.ShapeDtypeStruct((s, 512), BF16),
                   jax.ShapeDtypeStruct((s, 512), F32)],
        compiler_params=_cparams(("parallel",)),
    )(proj, proj, proj, proj, o_mla, *od, *lsed, wp_mla, wp_dil)


def _out_loss(merged, w_out, x, target, gpost):
    s = x.shape[0]
    tm = min(512, s)

    def body(mg_ref, w_ref, x_ref, t_ref, g_ref, do_out, dy_out, loss_out, dg_out):
        i = pl.program_id(0)

        @pl.when(i == 0)
        def _():
            loss_out[...] = jnp.zeros_like(loss_out)
            dg_out[...] = jnp.zeros_like(dg_out)

        o = _dot(mg_ref[...], w_ref[...])
        g = g_ref[...]
        n, u, r = _rms(o, g)
        e = (x_ref[...] + n) - t_ref[...]
        loss_out[...] += 0.5 * jnp.sum(jnp.mean(e * e, axis=-1, keepdims=True))
        dy = e * (1.0 / D_MODEL)
        dy_out[...] = dy
        dg_out[...] += jnp.sum(dy * u, axis=0, keepdims=True)
        do_out[...] = _rms_bwd(dy, g, u, r).astype(BF16)

    row = lambda i: (i, 0)
    cst = lambda i: (0, 0)
    return pl.pallas_call(
        body, name="out_loss", grid=(s // tm,),
        in_specs=[_full((tm, 1024), row), _full((1024, 1024), cst), _full((tm, 1024), row), _full((tm, 1024), row),
                  _full((1, 1024), cst)],
        out_specs=[_full((tm, 1024), row), _full((tm, 1024), row), _full((8, LANES), cst), _full((1, 1024), cst)],
        out_shape=[jax.ShapeDtypeStruct((s, 1024), BF16), jax.ShapeDtypeStruct((s, 1024), F32),
                   jax.ShapeDtypeStruct((8, LANES), F32), jax.ShapeDtypeStruct((1, 1024), F32)],
        compiler_params=_cparams(("arbitrary",)),
    )(merged, w_out, x, target, gpost)


def _seg_sum64(x, ones_bd):
    hi = x.astype(BF16)
    lo = (x - hi.astype(F32)).astype(BF16)
    return _dot(hi, ones_bd) + _dot(lo, ones_bd)


def _merge_bwd(do, w_out, merged, proj, ya, yd, o_mla, o_dil, wp_mla, wp_dil):
    s = do.shape[0]
    tm = min(256, s)
    seg = jnp.arange(512) // 64
    ones_bd = (seg[:, None] == seg[None, :]).astype(BF16)

    def body(do_ref, wo_ref, mg_ref, zm_ref, zd_ref, gm_ref, gd_ref, ya_ref, yd_ref, om_ref, od_ref, wm_ref, wd_ref,
             bd_ref, dzm_out, dzd_out, dgm_out, dgd_out, dom_out, dod_out, domt_out, dd_out, dwo_out, dwm_out,
             dwd_out):
        i = pl.program_id(0)

        @pl.when(i == 0)
        def _():
            dwo_out[...] = jnp.zeros_like(dwo_out)
            dwm_out[...] = jnp.zeros_like(dwm_out)
            dwd_out[...] = jnp.zeros_like(dwd_out)

        dov = do_ref[...]
        dwo_out[...] += _dot_tn(mg_ref[...], dov)
        dm = _dot_nt(dov, wo_ref[...])
        for g_ref, y_ref, z_ref, o_ref, w_ref, dz_out, dg_out, dob_out, dd_o, dw_out in (
                (gm_ref, ya_ref, zm_ref, om_ref, wm_ref, dzm_out, dgm_out, dom_out, None, dwm_out),
                (gd_ref, yd_ref, zd_ref, od_ref, wd_ref, dzd_out, dgd_out, dod_out, dd_out, dwd_out)):
            sg = _sigmoid(g_ref[...].astype(F32))
            dg_out[...] = (dm * y_ref[...].astype(F32) * sg * (1.0 - sg)).astype(BF16)
            dy = (dm * sg).astype(BF16)
            z = z_ref[...].astype(F32)
            sz = _sigmoid(z)
            silu = z * sz
            ob = o_ref[...].astype(F32)
            dw_out[...] += _dot_tn((ob * silu).astype(BF16), dy)
            dp = _dot_nt(dy, w_ref[...])
            dz_out[...] = (dp * ob * (sz * (1.0 + z * (1.0 - sz)))).astype(BF16)
            dob = dp * silu
            dob_out[...] = dob.astype(BF16)
            if dd_o is None:
                domt_out[...] = dob.T.astype(BF16)
            else:
                dd_o[...] = _seg_sum64(dob * ob, bd_ref[...])

    row = lambda i: (i, 0)
    cst = lambda i: (0, 0)
    r512 = _full((tm, 512), row)
    r1024 = _full((tm, 1024), row)
    return pl.pallas_call(
        body, name="merge_bwd", grid=(s // tm,),
        in_specs=[r1024, _full((1024, 1024), cst), r1024,
                  _full((tm, 512), lambda i: (i, Z_MLA // 512)), _full((tm, 512), lambda i: (i, Z_DIL // 512)),
                  _full((tm, 1024), lambda i: (i, G_MLA // 1024)), _full((tm, 1024), lambda i: (i, G_DIL // 1024)),
                  r1024, r1024, r512, r512, _full((512, 1024), cst), _full((512, 1024), cst), _full((512, 512), cst)],
        out_specs=[r512, r512, r1024, r1024, r512, r512, _full((512, tm), lambda i: (0, i)), r512,
                   _full((1024, 1024), cst), _full((512, 1024), cst), _full((512, 1024), cst)],
        out_shape=[jax.ShapeDtypeStruct((s, 512), BF16), jax.ShapeDtypeStruct((s, 512), BF16),
                   jax.ShapeDtypeStruct((s, 1024), BF16), jax.ShapeDtypeStruct((s, 1024), BF16),
                   jax.ShapeDtypeStruct((s, 512), BF16), jax.ShapeDtypeStruct((s, 512), BF16),
                   jax.ShapeDtypeStruct((512, s), BF16), jax.ShapeDtypeStruct((s, 512), F32),
                   jax.ShapeDtypeStruct((1024, 1024), F32), jax.ShapeDtypeStruct((512, 1024), F32),
                   jax.ShapeDtypeStruct((512, 1024), F32)],
        compiler_params=_cparams(("arbitrary",)),
    )(do, w_out, merged, proj, proj, proj, proj, ya, yd, o_mla, o_dil, wp_mla, wp_dil, ones_bd)


def _mla_bwd(q, qt, k, kt, v, do, dot, ot, lse):
    s = q.shape[0]
    tq = tk = min(256, s)
    nq = s // tq

    def body(q_ref, qt_ref, k_ref, kt_ref, v_ref, do_ref, dot_ref, ot_ref, lse_ref, dqt_out, dkt_out, dvt_out,
             dqt_acc):
        left = _left_mask()
        causal = (lax.broadcasted_iota(jnp.int32, (tk, tq), 0) <= lax.broadcasted_iota(jnp.int32, (tk, tq), 1))
        dqt_acc[...] = jnp.zeros_like(dqt_acc)

        def kv_step(j, _):
            c0 = pl.multiple_of(j * tk, tk)
            vv = v_ref[pl.ds(c0, tk), :]
            khs = [k_ref[pl.ds(c0, tk), hh * 128:(hh + 1) * 128] for hh in range(2)]
            kths = [kt_ref[hh * 128:(hh + 1) * 128, pl.ds(c0, tk)] for hh in range(2)]
            vms = [jnp.where(left if hh == 0 else jnp.logical_not(left), vv, jnp.zeros_like(vv)) for hh in range(2)]

            def scores(i):
                r0 = pl.multiple_of(jnp.minimum(i, nq - 1) * tq, tq)
                dov = do_ref[pl.ds(r0, tq), :]
                return tuple((_dot_nt(khs[hh], q_ref[pl.ds(r0, tq), hh * 128:(hh + 1) * 128]) * MLA_SCALE,
                              _dot_nt(vms[hh], dov)) for hh in range(2))

            def update(i, tiles, acc, masked):
                r0 = pl.multiple_of(i * tq, tq)
                new = []
                for hh in range(2):
                    dkt, dvt = acc[hh]
                    st, dp = tiles[hh]
                    hrows = slice(hh * 128, (hh + 1) * 128)
                    drows = slice(hh * 64, (hh + 1) * 64)
                    doth = dot_ref[drows, pl.ds(r0, tq)]
                    dd = jnp.sum(doth.astype(F32) * ot_ref[drows, pl.ds(r0, tq)].astype(F32), axis=0, keepdims=True)
                    p = jnp.exp(st - lse_ref[hh:hh + 1, pl.ds(r0, tq)])
                    if masked:
                        p = jnp.where(causal, p, 0.0)
                    ds = (p * (dp - dd) * MLA_SCALE).astype(BF16)
                    dvt = dvt + _dot_nt(doth, p.astype(BF16))
                    dkt = dkt + _dot_nt(qt_ref[hrows, pl.ds(r0, tq)], ds)
                    dqt_acc[hrows, pl.ds(r0, tq)] += _dot(kths[hh], ds)
                    new.append((dkt, dvt))
                return tuple(new)

            init = tuple((jnp.zeros((128, tk), F32), jnp.zeros((64, tk), F32)) for _ in range(2))
            acc = update(j, scores(j), init, True)
            acc = lax.fori_loop(j + 1, nq, lambda i, a: update(i, scores(i), a, False), acc)
            for hh in range(2):
                dkt_out[hh * 128:(hh + 1) * 128, pl.ds(c0, tk)] = acc[hh][0].astype(BF16)
                dvt_out[hh * 64:(hh + 1) * 64, pl.ds(c0, tk)] = acc[hh][1].astype(BF16)
            return 0

        lax.fori_loop(0, nq, kv_step, 0)
        dqt_out[...] = dqt_acc[...].astype(BF16)

    b256 = _full((s, 256), lambda p: (0, p))
    b128 = _full((s, 128), lambda p: (0, p))
    t256 = _full((256, s), lambda p: (p, 0))
    t128 = _full((128, s), lambda p: (p, 0))
    return pl.pallas_call(
        body, name="mla_bwd", grid=(4,),
        in_specs=[b256, t256, b256, t256, b128, b128, t128, t128, _full((8, s), lambda p: (p, 0))],
        out_specs=[t256, t256, t128],
        out_shape=[jax.ShapeDtypeStruct((1024, s), BF16), jax.ShapeDtypeStruct((1024, s), BF16),
                   jax.ShapeDtypeStruct((512, s), BF16)],
        scratch_shapes=[pltpu.VMEM((256, s), F32)],
        compiler_params=_cparams(("parallel",)),
    )(q, qt, k, kt, v, do, dot, ot, lse)


def _mla_prep_bwd(dq, dk, dv, proj, gq, gkv, w_uq, w_uk, w_uv, q_tab, k_tab):
    s = proj.shape[0]
    tm = min(256, s)

    def body(dqt_ref, dkt_ref, dvt_ref, cq_ref, ckv_ref, gq_ref, gkv_ref, wq_ref, wk_ref, wv_ref,
             qc, qsp, qsm, kc, ksp, ksm,
             dcq_out, dkr_out, dckv_out, dwq_out, dwk_out, dwv_out, dgq_out, dgkv_out):
        i = pl.program_id(0)

        @pl.when(i == 0)
        def _():
            for r in (dwq_out, dwk_out, dwv_out, dgq_out, dgkv_out):
                r[...] = jnp.zeros_like(r)

        dqu = _unrope(dqt_ref[...].astype(F32).T, _tile_lanes(qc[...], 1024), _tile_lanes(qsp[...], 1024),
                      _tile_lanes(qsm[...], 1024), 16).astype(BF16)
        gq = gq_ref[...]
        cqn, xh, r = _rms(cq_ref[...].astype(F32), gq)
        dwq_out[...] += _dot_tn(cqn.astype(BF16), dqu)
        dcqn = _dot_nt(dqu, wq_ref[...])
        dgq_out[...] += jnp.sum(dcqn * xh, axis=0, keepdims=True)
        dcq_out[...] = _rms_bwd(dcqn, gq, xh, r).astype(BF16)

        dkf = dkt_ref[...].astype(F32).T
        dkb = dkf.astype(BF16)
        dsum = dkf[:, 0:128]
        for h in range(1, MLA_HEADS):
            dsum = dsum + dkf[:, h * 128:(h + 1) * 128]
        dkr_out[...] = _unrope(pltpu.roll(dsum, 64, 1), kc[...], ksp[...], ksm[...], 16).astype(BF16)

        dvb = dvt_ref[...].astype(F32).T.astype(BF16)
        gkv = gkv_ref[...]
        ckvn, xh2, r2 = _rms(ckv_ref[...].astype(F32), gkv)
        ckvn = ckvn.astype(BF16)
        dwk_out[...] += _dot_tn(ckvn, dkb)
        dwv_out[...] += _dot_tn(ckvn, dvb)
        dckvn = _dot_nt(dkb, wk_ref[...]) + _dot_nt(dvb, wv_ref[...])
        dgkv_out[...] += jnp.sum(dckvn * xh2, axis=0, keepdims=True)
        dckv_out[...] = _rms_bwd(dckvn, gkv, xh2, r2).astype(BF16)

    row = lambda i: (i, 0)
    cst = lambda i: (0, 0)
    tabs = [_full((tm, LANES), row)] * 6
    return pl.pallas_call(
        body, name="mla_prep_bwd", grid=(s // tm,),
        in_specs=[_full((1024, tm), lambda i: (0, i)), _full((1024, tm), lambda i: (0, i)),
                  _full((512, tm), lambda i: (0, i)),
                  _full((tm, 384), lambda i: (i, CQ_OFF // 384)), _full((tm, 256), lambda i: (i, CKV_OFF // 256)),
                  _full((1, 384), cst), _full((1, 256), cst),
                  _full((384, 1024), cst), _full((256, 1024), cst), _full((256, 512), cst)] + tabs,
        out_specs=[_full((tm, 384), row), _full((tm, 128), row), _full((tm, 256), row),
                   _full((384, 1024), cst), _full((256, 1024), cst), _full((256, 512), cst),
                   _full((1, 384), cst), _full((1, 256), cst)],
        out_shape=[jax.ShapeDtypeStruct((s, 384), BF16), jax.ShapeDtypeStruct((s, 128), BF16),
                   jax.ShapeDtypeStruct((s, 256), BF16),
                   jax.ShapeDtypeStruct((384, 1024), F32), jax.ShapeDtypeStruct((256, 1024), F32),
                   jax.ShapeDtypeStruct((256, 512), F32),
                   jax.ShapeDtypeStruct((1, 384), F32), jax.ShapeDtypeStruct((1, 256), F32)],
        compiler_params=_cparams(("arbitrary",)),
    )(dq, dk, dv, proj, proj, gq, gkv, w_uq, w_uk, w_uv, *q_tab, *k_tab)


def _dil_bwd(qa, ka, va, qo, ko, vo, do, lse, dd, tabs, d, name):
    s = qa.shape[0]
    nblk = s // Q_BLOCK
    per_seq = nblk // d

    def body(q_ref, k_ref, v_ref, do_ref, lse_ref, dd_ref, c_ref, sp_ref, sm_ref, dq_out, dk_out, dv_out,
             dk_acc, dv_acc):
        left = _left_mask()
        dk_acc[...] = jnp.zeros_like(dk_acc)
        dv_acc[...] = jnp.zeros_like(dv_acc)

        hms = (left, jnp.logical_not(left))

        def rows_of(b):
            r0 = pl.multiple_of(b * Q_BLOCK, Q_BLOCK)
            p0 = pl.multiple_of(jnp.maximum(b - 1, 0) * Q_BLOCK, Q_BLOCK)
            return pl.ds(r0, Q_BLOCK), pl.ds(p0, Q_BLOCK)

        def scores(b):
            rows, prow = rows_of(b)
            qb, dob = q_ref[rows, :], do_ref[rows, :]
            kp, kc, vp, vc = k_ref[prow, :], k_ref[rows, :], v_ref[prow, :], v_ref[rows, :]
            zero = jnp.zeros_like(qb)
            out = []
            for hh in range(2):
                qm, dom = jnp.where(hms[hh], qb, zero), jnp.where(hms[hh], dob, zero)
                out.append((_dot_nt(qm, kp) * DIL_SCALE, _dot_nt(qm, kc) * DIL_SCALE,
                            _dot_nt(dom, vp), _dot_nt(dom, vc)))
            return out

        def finish(b, tiles):
            rows, prow = rows_of(b)
            pmask, cur_ok = _band_masks((b % per_seq) > 0)
            qb, dob = q_ref[rows, :], do_ref[rows, :]
            kp, kc = k_ref[prow, :], k_ref[rows, :]
            lse_b, dd_b = lse_ref[rows, :], dd_ref[rows, :]
            zero = jnp.zeros_like(qb)
            dq = jnp.zeros((Q_BLOCK, LANES), F32)
            dkp = jnp.zeros((Q_BLOCK, LANES), F32)
            dkc = jnp.zeros((Q_BLOCK, LANES), F32)
            dvp = jnp.zeros((Q_BLOCK, LANES), F32)
            dvc = jnp.zeros((Q_BLOCK, LANES), F32)
            for hh in range(2):
                hm = hms[hh]
                qm, dom = jnp.where(hm, qb, zero), jnp.where(hm, dob, zero)
                lse_h, dd_h = _expand_half(lse_b, hh, left), _expand_half(dd_b, hh, left)
                sp, sc, dpp, dpc = tiles[hh]
                pp = jnp.where(pmask, jnp.exp(sp - lse_h), 0.0)
                pc = jnp.where(cur_ok, jnp.exp(sc - lse_h), 0.0)
                dsp = (pp * (dpp - dd_h) * DIL_SCALE).astype(BF16)
                dsc = (pc * (dpc - dd_h) * DIL_SCALE).astype(BF16)
                dq = dq + _dot(dsp, jnp.where(hm, kp, zero)) + _dot(dsc, jnp.where(hm, kc, zero))
                dkp, dkc = dkp + _dot_tn(dsp, qm), dkc + _dot_tn(dsc, qm)
                dvp, dvc = dvp + _dot_tn(pp.astype(BF16), dom), dvc + _dot_tn(pc.astype(BF16), dom)
            dq_out[rows, :] = _unrope(dq, c_ref[rows, :], sp_ref[rows, :], sm_ref[rows, :], 8).astype(BF16)
            dk_acc[prow, :] += dkp
            dv_acc[prow, :] += dvp
            dk_acc[rows, :] += dkc
            dv_acc[rows, :] += dvc

        def step(t, _):
            ta, tb = scores(2 * t), scores(2 * t + 1)
            finish(2 * t, ta)
            finish(2 * t + 1, tb)
            return 0

        lax.fori_loop(0, nblk // 2, step, 0)
        dk_out[...] = _unrope(dk_acc[...], c_ref[...], sp_ref[...], sm_ref[...], 8).astype(BF16)
        dv_out[...] = dv_acc[...].astype(BF16)

    blk = lambda off: _full((s, 128), lambda p, off=off: (0, off + p))
    tab = _full((s, 128), lambda p: (0, 0))
    return pl.pallas_call(
        body, name=name, grid=(4,),
        in_specs=[blk(qo), blk(ko), blk(vo), blk(0), blk(0), blk(0), tab, tab, tab],
        out_specs=[blk(0), blk(0), blk(0)],
        out_shape=[jax.ShapeDtypeStruct((s, 512), BF16)] * 3,
        scratch_shapes=[pltpu.VMEM((s, 128), F32), pltpu.VMEM((s, 128), F32)],
        compiler_params=_cparams(("parallel",)),
    )(qa, ka, va, do, lse, dd, *tabs)


def _dh_bwd(dproj, w_pad, x, gpre, dy):
    s = x.shape[0]
    tm, tk = min(512, s), 1408
    nk = IN_PAD // tk

    def body(dp_ref, w_ref, x_ref, g_ref, dy_ref, gx_out, dg_out, acc):
        i, kk = pl.program_id(0), pl.program_id(1)

        @pl.when(jnp.logical_and(i == 0, kk == 0))
        def _():
            dg_out[...] = jnp.zeros_like(dg_out)

        @pl.when(kk == 0)
        def _():
            acc[...] = jnp.zeros_like(acc)

        acc[...] += _dot_nt(dp_ref[...], w_ref[...])

        @pl.when(kk == nk - 1)
        def _():
            g = g_ref[...]
            _, xh, r = _rms(x_ref[...], g)
            dh = acc[...]
            dg_out[...] += jnp.sum(dh * xh, axis=0, keepdims=True)
            gx_out[...] = dy_ref[...] + _rms_bwd(dh, g, xh, r)

    row = lambda i, k: (i, 0)
    return pl.pallas_call(
        body, name="dh_bwd", grid=(s // tm, nk),
        in_specs=[_full((tm, tk), lambda i, k: (i, k)), _full((1024, tk), lambda i, k: (0, k)),
                  _full((tm, 1024), row), _full((1, 1024), lambda i, k: (0, 0)), _full((tm, 1024), row)],
        out_specs=[_full((tm, 1024), row), _full((1, 1024), lambda i, k: (0, 0))],
        out_shape=[jax.ShapeDtypeStruct((s, 1024), F32), jax.ShapeDtypeStruct((1, 1024), F32)],
        scratch_shapes=[pltpu.VMEM((tm, 1024), F32)],
        compiler_params=_cparams(("arbitrary", "arbitrary")),
    )(dproj, w_pad, x, gpre, dy)


def _dw_in(h, dproj):
    s = h.shape[0]
    ts, tn = min(1024, s), 768
    ns = s // ts

    def body(h_ref, dp_ref, o_ref):
        @pl.when(pl.program_id(1) == 0)
        def _():
            o_ref[...] = jnp.zeros_like(o_ref)

        o_ref[...] += _dot_tn(h_ref[...], dp_ref[...])

    return pl.pallas_call(
        body, name="dw_in", grid=(IN_PAD // tn, ns),
        in_specs=[_full((ts, 1024), lambda j, k: (k, 0)), _full((ts, tn), lambda j, k: (k, j))],
        out_specs=_full((1024, tn), lambda j, k: (0, j)),
        out_shape=jax.ShapeDtypeStruct((1024, IN_PAD), F32),
        compiler_params=_cparams(("parallel", "arbitrary")),
    )(h, dproj)


def _exchange(big, small, name):
    per_peer = big.shape[0] == N_DEV
    n_small = 0 if small is None else 1

    def body(*refs):
        if small is None:
            big_ref, big_out, sems = refs
            small_ref = small_out = None
        else:
            big_ref, small_ref, big_out, small_out, sems = refs
        x, y, c = lax.axis_index("x"), lax.axis_index("y"), lax.axis_index("c")
        me = 4 * x + 2 * y + c
        flip = lambda v, b: (1 - v) if b else v

        def copies(j, outgoing):
            jx, jy, jc = (j >> 2) & 1, (j >> 1) & 1, j & 1
            peer = (flip(x, jx), flip(y, jy), flip(c, jc))
            peer_idx = 4 * peer[0] + 2 * peer[1] + peer[2]
            slot = me if outgoing else peer_idx
            src = big_ref.at[peer_idx] if per_peer else big_ref.at[0]
            out = [pltpu.make_async_remote_copy(src_ref=src, dst_ref=big_out.at[slot], send_sem=sems.at[0, j],
                                                recv_sem=sems.at[1, j], device_id=peer,
                                                device_id_type=pl.DeviceIdType.MESH)]
            if small is not None:
                out.append(pltpu.make_async_remote_copy(src_ref=small_ref, dst_ref=small_out.at[slot],
                                                        send_sem=sems.at[2, j], recv_sem=sems.at[3, j],
                                                        device_id=peer, device_id_type=pl.DeviceIdType.MESH))
            return out

        own = [pltpu.make_async_copy(big_ref.at[me] if per_peer else big_ref.at[0], big_out.at[me], sems.at[0, 0])]
        if small is not None:
            own.append(pltpu.make_async_copy(small_ref, small_out.at[me], sems.at[2, 0]))
        for cp in own:
            cp.start()
        sent = [cp for j in range(1, N_DEV) for cp in copies(j, True)]
        for cp in sent:
            cp.start()
        for j in range(1, N_DEV):
            for cp in copies(j, False):
                cp.wait_recv()
        for cp in sent:
            cp.wait_send()
        for cp in own:
            cp.wait()

    hbm = pl.BlockSpec(memory_space=pl.ANY)
    rows = big.shape[1]
    out_shape = [jax.ShapeDtypeStruct((N_DEV, rows, LANES), big.dtype)]
    args = [big]
    if small is not None:
        out_shape.append(jax.ShapeDtypeStruct((N_DEV,) + small.shape, small.dtype))
        args.append(small)
    res = pl.pallas_call(
        body, name=name, in_specs=[hbm] * len(args), out_specs=[hbm] * len(out_shape), out_shape=out_shape,
        scratch_shapes=[pltpu.SemaphoreType.DMA((2 + 2 * n_small, N_DEV))],
    )(*args)
    return res


def _adam_math(w, g, m, v):
    m = ADAM_B1 * m + (1.0 - ADAM_B1) * g
    v = ADAM_B2 * v + (1.0 - ADAM_B2) * (g * g)
    m_hat = m / (1.0 - ADAM_B1 ** ADAM_STEP)
    v_hat = v / (1.0 - ADAM_B2 ** ADAM_STEP)
    delta = -ADAM_LR * (m_hat / (jnp.sqrt(v_hat) + ADAM_EPS) + ADAM_WD * w)
    return delta, m, v


def _adam(recv, w, m, v, name, tr):
    rows = w.shape[0]

    def body(r_ref, w_ref, m_ref, v_ref, g_out, d_out, m_out, v_out):
        g = r_ref[0].astype(F32)
        for k in range(1, N_DEV):
            g = g + r_ref[k].astype(F32)
        g_out[...] = g
        d_out[...], m_out[...], v_out[...] = _adam_math(w_ref[...], g, m_ref[...], v_ref[...])

    row = _full((tr, LANES), lambda i: (i, 0))
    return pl.pallas_call(
        body, name=name, grid=(rows // tr,),
        in_specs=[_full((N_DEV, tr, LANES), lambda i: (0, i, 0)), row, row, row],
        out_specs=[row] * 4,
        out_shape=[jax.ShapeDtypeStruct((rows, LANES), F32)] * 4,
        compiler_params=_cparams(("parallel",)),
    )(recv, w, m, v)


def _local_step(x, positions, gains, weights, target):
    gpre, gq, gkv, gpost = gains
    w_pad, w_uq, w_uk, w_uv, wp_mla, wp_dil, w_out = weights
    q_tab, k_tab, d_tab = _rope_tables(positions)

    proj, h = _inproj(x, gpre, w_pad, d_tab)
    q, k, v, qt, kt, vt = _mla_prep(proj, gq, gkv, w_uq, w_uk, w_uv, q_tab, k_tab)
    o_mla, ot_mla, lse_mla = _mla_fwd(q, k, vt)

    qkv_perm, od, lsed = [], [], []
    for g, d in enumerate(DIL_DILATIONS):
        if d == 1:
            arr, offs = proj, ((Q_OFF + 512 * g) // 128, (K_OFF + 512 * g) // 128, (V_OFF + 512 * g) // 128)
        else:
            cols = jnp.concatenate([proj[:, o + 512 * g:o + 512 * (g + 1)] for o in (Q_OFF, K_OFF, V_OFF)], axis=1)
            arr, offs = _perm(cols, d), (0, 4, 8)
        qkv_perm.append((arr, offs))
        o_g, lse_g = _dil_fwd(arr, arr, arr, *offs, d, "dil_fwd_%d" % g)
        od.append(_unperm(o_g, d))
        lsed.append(_unperm(lse_g, d))

    merged, ya, yd, o_dil, lse_dil = _merge_fwd(proj, o_mla, od, lsed, wp_mla, wp_dil)
    do, dy, loss, dgpost = _out_loss(merged, w_out, x, target, gpost)

    (dz_mla, dz_dil, dg_mla, dg_dil, do_mla, do_dil, dot_mla, dd_dil, dw_out, dwp_mla, dwp_dil) = _merge_bwd(
        do, w_out, merged, proj, ya, yd, o_mla, o_dil, wp_mla, wp_dil)

    dq, dk, dv = _mla_bwd(q, qt, k, kt, v, do_mla, dot_mla, ot_mla, lse_mla)
    dcq, dkr, dckv, dw_uq, dw_uk, dw_uv, dgq, dgkv = _mla_prep_bwd(dq, dk, dv, proj, gq, gkv, w_uq, w_uk, w_uv,
                                                                   q_tab, k_tab)

    dqs, dks, dvs = [], [], []
    for g, d in enumerate(DIL_DILATIONS):
        arr, offs = qkv_perm[g]
        tabs = tuple(_perm(t, d) for t in d_tab)
        dq_g, dk_g, dv_g = _dil_bwd(arr, arr, arr, *offs, _perm(do_dil, d), _perm(lse_dil, d), _perm(dd_dil, d),
                                    tabs, d, "dil_bwd_%d" % g)
        dqs.append(_unperm(dq_g, d))
        dks.append(_unperm(dk_g, d))
        dvs.append(_unperm(dv_g, d))

    dproj = jnp.concatenate([dz_mla, dz_dil, dg_mla, dg_dil] + dqs + dks + dvs + [dcq, dkr, dckv], axis=1)
    grad_x, dgpre = _dh_bwd(dproj, w_pad, x, gpre, dy)
    dw_in = _dw_in(h, dproj)
    return loss, grad_x, (dgpre, dgq, dgkv, dgpost), (dw_in, dw_uq, dw_uk, dw_uv, dwp_mla, dwp_dil, dw_out)


def _pack_small(parts, loss):
    rows = [p.reshape(-1, LANES) for p in parts] + [loss[0:1]]
    n = sum(r.shape[0] for r in rows)
    return jnp.concatenate(rows + [jnp.zeros((SMALL_ROWS - n, LANES), F32)], axis=0)


def _pack_gains(parts):
    rows = [p.reshape(-1, LANES) for p in parts]
    n = sum(r.shape[0] for r in rows)
    return jnp.concatenate(rows + [jnp.zeros((SMALL_ROWS - n, LANES), F32)], axis=0)


def _unpack_gains(buf):
    return (buf[0:8].reshape(1, 1024), buf[8:11].reshape(1, 384), buf[11:13].reshape(1, 256),
            buf[13:21].reshape(1, 1024))


def kernel(x, positions, pre_norm_g, w_in, q_norm_g, w_uq, kv_norm_g, w_ukv, w_proj_mla, w_proj_dil, w_out, post_norm_g, loss_target, m_pre_norm_g, m_w_in, m_q_norm_g, m_w_uq, m_kv_norm_g, m_w_ukv, m_w_proj_mla, m_w_proj_dil, m_w_out, m_post_norm_g, v_pre_norm_g, v_w_in, v_q_norm_g, v_w_uq, v_kv_norm_g, v_w_ukv, v_w_proj_mla, v_w_proj_dil, v_w_out, v_post_norm_g):
    big_w = (w_in, w_uq, w_ukv, w_proj_mla, w_proj_dil, w_out)
    big_m = (m_w_in, m_w_uq, m_w_ukv, m_w_proj_mla, m_w_proj_dil, m_w_out)
    big_v = (v_w_in, v_w_uq, v_w_ukv, v_w_proj_mla, v_w_proj_dil, v_w_out)
    gains = (pre_norm_g, q_norm_g, kv_norm_g, post_norm_g)
    gains_m = (m_pre_norm_g, m_q_norm_g, m_kv_norm_g, m_post_norm_g)
    gains_v = (v_pre_norm_g, v_q_norm_g, v_kv_norm_g, v_post_norm_g)

    gathered, = _exchange(_pack_shards(big_w, BF16)[None], None, "gather_weights")
    weights = _unpack_gathered(gathered)

    loss, grad_x, dgains, dweights = _local_step(x[0], positions[0], gains, weights, loss_target[0])

    chunks = _pack_grads(*dweights).astype(BF16)
    recv_big, recv_small = _exchange(chunks, _pack_small(dgains, loss), "exchange_grads")

    g_big, d_big, m_big, v_big = _adam(recv_big, _pack_shards(big_w, F32), _pack_shards(big_m, F32),
                                       _pack_shards(big_v, F32), "adam_weights", 912)
    g_sm, d_sm, m_sm, v_sm = _adam(recv_small, _pack_gains(gains), _pack_gains(gains_m), _pack_gains(gains_v),
                                   "adam_gains", SMALL_ROWS)

    def interleave(small_buf, big_buf):
        s_pre, s_q, s_kv, s_post = _unpack_gains(small_buf)
        b_in, b_uq, b_ukv, b_pm, b_pd, b_out = _unpack_shards(big_buf)
        return [s_pre, b_in, s_q, b_uq, s_kv, b_ukv, b_pm, b_pd, b_out, s_post]

    total_loss = g_sm[21, 0]
    return (total_loss, grad_x[None], *interleave(g_sm, g_big), *interleave(d_sm, d_big),
            *interleave(m_sm, m_big), *interleave(v_sm, v_big))
```

```python
import numpy as np
import jax
import jax.numpy as jnp
from jax import lax
from jax.experimental import pallas as pl
from jax.experimental.pallas import tpu as pltpu

F32 = jnp.float32
BF16 = jnp.bfloat16

D_MODEL = 1024
NORM_EPS = 1e-6
ROPE_THETA = 500000.0
N_DEV = 8
LANES = 128
NEG = -1e30

MLA_HEADS = 8
MLA_Q_RANK = 384
MLA_KV_RANK = 256
MLA_SCALE = 96.0 ** -0.5
DIL_DILATIONS = (1, 4, 16)
DIL_SCALE = 0.125
Q_BLOCK = 128

Z_MLA, Z_DIL, G_MLA, G_DIL = 0, 512, 1024, 2048
Q_OFF, K_OFF, V_OFF = 3072, 4608, 6144
CQ_OFF, KR_OFF, CKV_OFF, IN_PAD = 7680, 8064, 8192, 8448
IN_WIDTH = 8352
SHARD_W = IN_WIDTH // 8
IN_SEGS = ((0, 384, CQ_OFF), (384, 256, CKV_OFF), (640, 32, KR_OFF), (672, 1536, Q_OFF), (2208, 1536, K_OFF),
           (3744, 1536, V_OFF), (5280, 512, Z_MLA), (5792, 512, Z_DIL), (6304, 1024, G_MLA), (7328, 1024, G_DIL))

GAIN_OFFS = (0, 1024, 1408, 1664)
GAIN_WIDTHS = (1024, 384, 256, 1024)
LOSS_OFF, PACKET = 2688, 2816

ADAM_LR, ADAM_B1, ADAM_B2, ADAM_EPS, ADAM_WD, ADAM_STEP = 0.001, 0.9, 0.999, 1e-08, 0.01, 10

VMEM_LIMIT_MB = 56


def _cparams(sem=None, vmem_mb=VMEM_LIMIT_MB):
    return pltpu.CompilerParams(dimension_semantics=sem, vmem_limit_bytes=vmem_mb * 1024 * 1024)


def _dot(a, b):
    return jnp.dot(a, b, preferred_element_type=F32)


def _dot_nt(a, b):
    return lax.dot_general(a, b, (((1,), (1,)), ((), ())), preferred_element_type=F32)


def _dot_tn(a, b):
    return lax.dot_general(a, b, (((0,), (0,)), ((), ())), preferred_element_type=F32)


def _tile_lanes(t, width):
    return t if width == t.shape[1] else jnp.tile(t, (1, width // t.shape[1]))


def _rope(x, c, sp, sm, a):
    n = x.shape[1]
    return x * c + pltpu.roll(x, a, 1) * sp + pltpu.roll(x, n - a, 1) * sm


def _unrope(dy, c, sp, sm, a):
    n = dy.shape[1]
    return dy * c + pltpu.roll(dy * sp, n - a, 1) + pltpu.roll(dy * sm, a, 1)


def _sigmoid(z):
    return 1.0 / (1.0 + jnp.exp(-z))


def _left_mask():
    return lax.broadcasted_iota(jnp.int32, (1, LANES), 1) < 64


def _expand_half(x, hh, left):
    r = pltpu.roll(x, 64, 1)
    return jnp.where(left, x, r) if hh == 0 else jnp.where(left, r, x)


def _rms(xv, g):
    r = lax.rsqrt(jnp.mean(xv * xv, axis=-1, keepdims=True) + NORM_EPS)
    xh = xv * r
    return xh * g, xh, r


def _rms_bwd(dout, g, xh, r):
    dxh = dout * g
    return r * (dxh - xh * jnp.mean(dxh * xh, axis=-1, keepdims=True))


def _full(shape, index_map):
    return pl.BlockSpec(shape, index_map)


def _w_in_pieces():
    out = []
    for s, n, off in sorted(IN_SEGS, key=lambda t: t[2]):
        c = s
        while c < s + n:
            k = c // SHARD_W
            e = min(s + n, (k + 1) * SHARD_W)
            out.append((k, c - k * SHARD_W, e - c, off + (c - s)))
            c = e
    return out


def _assemble_w_in(g):
    parts, cur = [], 0
    for k, a, w, off in _w_in_pieces():
        if off > cur:
            parts.append(jnp.zeros((D_MODEL, off - cur), g.dtype))
        parts.append(g[k, :, a:a + w])
        cur = off + w
    if cur < IN_PAD:
        parts.append(jnp.zeros((D_MODEL, IN_PAD - cur), g.dtype))
    return jnp.concatenate(parts, axis=1)


def _dw_in_chunks(dw):
    chunks = []
    for dev in range(N_DEV):
        mine = sorted((p for p in _w_in_pieces() if p[0] == dev), key=lambda p: p[1])
        chunks.append(jnp.concatenate([dw[:, off:off + w] for k, a, w, off in mine], axis=1))
    return jnp.stack(chunks)


def _assemble_weights(g_in, g_uq, g_ukv, g_pm, g_pd, g_out):
    w_uq_pad = jnp.pad(g_uq.transpose(1, 0, 2), ((0, 0), (0, 0), (0, 32))).reshape(384, 1024)
    ukv = g_ukv.transpose(1, 0, 2)
    w_uk_pad = jnp.pad(ukv[:, :, :64], ((0, 0), (0, 0), (0, 64))).reshape(256, 1024)
    w_uv = ukv[:, :, 64:].reshape(256, 512)
    wp_mla = g_pm.transpose(1, 0, 2).reshape(512, 1024)
    wp_dil = g_pd.transpose(1, 0, 2).reshape(512, 1024)
    return _assemble_w_in(g_in), w_uq_pad, w_uk_pad, w_uv, wp_mla, wp_dil, g_out.reshape(1024, 1024)


def _grad_chunks(dw_in_pad, dw_uq_pad, dw_uk_pad, dw_uv, dwp_mla, dwp_dil, dw_out):
    a = _dw_in_chunks(dw_in_pad)
    b = dw_uq_pad.reshape(384, 8, 128)[:, :, :96].transpose(1, 0, 2)
    c = jnp.concatenate([dw_uk_pad.reshape(256, 8, 128)[:, :, :64], dw_uv.reshape(256, 8, 64)], axis=2)
    c = c.transpose(1, 0, 2)
    d = dwp_mla.reshape(512, N_DEV, 128).transpose(1, 0, 2)
    e = dwp_dil.reshape(512, N_DEV, 128).transpose(1, 0, 2)
    f = dw_out.reshape(N_DEV, 128, 1024)
    return [t.astype(BF16) for t in (a, b, c, d, e, f)]


def _lane_consts(freqs, half, first, period):
    rel = (np.arange(LANES) % period) - first
    rot = (rel >= 0) & (rel < 2 * half)
    freq = np.where(rot, freqs[np.clip(rel, 0, 2 * half - 1) % half], 0.0).astype(np.float32)
    x1 = (rot & (rel < half)).astype(np.float32)
    x2 = (rot & (rel >= half)).astype(np.float32)
    return freq[None, :], x1[None, :], x2[None, :]


def _rope_tables(pos):
    p = pos.astype(F32)[:, None]
    inv_m = np.float32(ROPE_THETA) ** (-(np.arange(0, 32, 2, dtype=np.float32) / np.float32(32)))
    inv_d = np.float32(ROPE_THETA) ** (-(np.arange(0, 16, 2, dtype=np.float32) / np.float32(16)))
    lane = np.arange(LANES)
    tabs = []
    for freqs, half, first, period, keep in ((inv_m, 16, 64, 128, lane < 96), (inv_m, 16, 0, 128, lane < 32),
                                              (inv_d, 8, 0, 64, lane >= 0)):
        freq, x1, x2 = _lane_consts(freqs, half, first, period)
        ang = p * freq
        sin = jnp.sin(ang)
        tabs.append((jnp.cos(ang) * keep.astype(np.float32)[None, :], sin * x2, sin * (-x1)))
    return tuple(tabs)


def _perm(a, d):
    if d == 1:
        return a
    s, c = a.shape
    return a.reshape(s // d, d, c).transpose(1, 0, 2).reshape(s, c)


def _unperm(a, d):
    if d == 1:
        return a
    s, c = a.shape
    return a.reshape(d, s // d, c).transpose(1, 0, 2).reshape(s, c)


def _inproj(x, gpre, w_pad, d_tab):
    s = x.shape[0]
    tm, tn = min(1024, s), 768
    rope_lo, rope_hi = Q_OFF // tn, V_OFF // tn

    def body(x_ref, g_ref, w_ref, c_ref, sp_ref, sm_ref, o_ref, h_ref):
        j = pl.program_id(1)

        @pl.when(j == 0)
        def _():
            hv, _, _ = _rms(x_ref[...], g_ref[...])
            h_ref[...] = hv.astype(BF16)

        acc = _dot(h_ref[...], w_ref[...])
        is_rope = jnp.logical_and(j >= rope_lo, j < rope_hi)

        @pl.when(is_rope)
        def _():
            o_ref[...] = _rope(acc, _tile_lanes(c_ref[...], tn), _tile_lanes(sp_ref[...], tn),
                               _tile_lanes(sm_ref[...], tn), 8).astype(BF16)

        @pl.when(jnp.logical_not(is_rope))
        def _():
            o_ref[...] = acc.astype(BF16)

    row = lambda i, j: (i, 0)
    return pl.pallas_call(
        body, name="inproj", grid=(s // tm, IN_PAD // tn),
        in_specs=[_full((tm, D_MODEL), row), _full((1, D_MODEL), lambda i, j: (0, 0)),
                  _full((D_MODEL, tn), lambda i, j: (0, j)),
                  _full((tm, LANES), row), _full((tm, LANES), row), _full((tm, LANES), row)],
        out_specs=[_full((tm, tn), lambda i, j: (i, j)), _full((tm, D_MODEL), row)],
        out_shape=[jax.ShapeDtypeStruct((s, IN_PAD), BF16), jax.ShapeDtypeStruct((s, D_MODEL), BF16)],
        compiler_params=_cparams(("parallel", "arbitrary")),
    )(x, gpre, w_pad, *d_tab)


def _mla_prep(proj, gq, gkv, w_uq, w_uk, w_uv, q_tab, k_tab):
    s = proj.shape[0]
    tm = min(512, s)

    def body(cq_ref, kr_ref, ckv_ref, gq_ref, gkv_ref, wq_ref, wk_ref, wv_ref,
             qc, qsp, qsm, kc, ksp, ksm, q_out, k_out, v_out, qt_out, kt_out, vt_out):
        cqn, _, _ = _rms(cq_ref[...].astype(F32), gq_ref[...])
        q = _dot(cqn.astype(BF16), wq_ref[...])
        q = _rope(q, _tile_lanes(qc[...], 1024), _tile_lanes(qsp[...], 1024), _tile_lanes(qsm[...], 1024), 16)
        q_out[...] = q.astype(BF16)
        qt_out[...] = q.T.astype(BF16)
        ckvn, _, _ = _rms(ckv_ref[...].astype(F32), gkv_ref[...])
        ckvn = ckvn.astype(BF16)
        kr = _rope(kr_ref[...].astype(F32), kc[...], ksp[...], ksm[...], 16)
        k = _dot(ckvn, wk_ref[...]) + _tile_lanes(pltpu.roll(kr, 64, 1), 1024)
        k_out[...] = k.astype(BF16)
        kt_out[...] = k.T.astype(BF16)
        v = _dot(ckvn, wv_ref[...])
        v_out[...] = v.astype(BF16)
        vt_out[...] = v.T.astype(BF16)

    row = lambda i: (i, 0)
    col = lambda i: (0, i)
    cst = lambda i: (0, 0)
    tabs = [_full((tm, LANES), row)] * 6
    return pl.pallas_call(
        body, name="mla_prep", grid=(s // tm,),
        in_specs=[_full((tm, 384), lambda i: (i, CQ_OFF // 384)), _full((tm, 128), lambda i: (i, KR_OFF // 128)),
                  _full((tm, 256), lambda i: (i, CKV_OFF // 256)), _full((1, 384), cst), _full((1, 256), cst),
                  _full((384, 1024), cst), _full((256, 1024), cst), _full((256, 512), cst)] + tabs,
        out_specs=[_full((tm, 1024), row), _full((tm, 1024), row), _full((tm, 512), row),
                   _full((1024, tm), col), _full((1024, tm), col), _full((512, tm), col)],
        out_shape=[jax.ShapeDtypeStruct((s, 1024), BF16), jax.ShapeDtypeStruct((s, 1024), BF16),
                   jax.ShapeDtypeStruct((s, 512), BF16), jax.ShapeDtypeStruct((1024, s), BF16),
                   jax.ShapeDtypeStruct((1024, s), BF16), jax.ShapeDtypeStruct((512, s), BF16)],
        compiler_params=_cparams(("parallel",)),
    )(proj, proj, proj, gq, gkv, w_uq, w_uk, w_uv, *q_tab, *k_tab)


def _mla_fwd(q, k, vt):
    s = q.shape[0]
    tq = tk = min(256, s)
    nq = s // tq

    def body(q_ref, k_ref, vt_ref, o_ref, ot_ref, lse_ref):
        causal = (lax.broadcasted_iota(jnp.int32, (tk, tq), 0) <= lax.broadcasted_iota(jnp.int32, (tk, tq), 1))

        def q_step(i, _):
            r0 = pl.multiple_of(i * tq, tq)
            qs = [q_ref[pl.ds(r0, tq), hh * 128:(hh + 1) * 128] for hh in range(2)]

            def scores(j):
                c0 = pl.multiple_of(j * tk, tk)
                return tuple(_dot_nt(k_ref[pl.ds(c0, tk), hh * 128:(hh + 1) * 128], qs[hh]) * MLA_SCALE
                             for hh in range(2))

            def update(j, sts, stats, masked):
                c0 = pl.multiple_of(j * tk, tk)
                new = []
                for hh in range(2):
                    m, l, acc = stats[hh]
                    st = jnp.where(causal, sts[hh], NEG) if masked else sts[hh]
                    m_new = jnp.maximum(m, jnp.max(st, axis=0, keepdims=True))
                    alpha = jnp.exp(m - m_new)
                    p = jnp.exp(st - m_new)
                    l = alpha * l + jnp.sum(p, axis=0, keepdims=True)
                    acc = acc * alpha + _dot(vt_ref[hh * 64:(hh + 1) * 64, pl.ds(c0, tk)], p.astype(BF16))
                    new.append((m_new, l, acc))
                return tuple(new)

            def kv_step(j, carry):
                sts, stats = carry
                nxt = scores(j + 1)
                return nxt, update(j, sts, stats, False)

            init = tuple((jnp.full((1, tq), NEG, F32), jnp.zeros((1, tq), F32), jnp.zeros((64, tq), F32))
                         for _ in range(2))
            sts, stats = lax.fori_loop(0, i, kv_step, (scores(0), init))
            (ma, la, acca), (mb, lb, accb) = update(i, sts, stats, True)
            ot = jnp.concatenate([acca / la, accb / lb], axis=0)
            ot_ref[:, pl.ds(r0, tq)] = ot.astype(BF16)
            o_ref[pl.ds(r0, tq), :] = ot.T.astype(BF16)
            lse_ref[:, pl.ds(r0, tq)] = jnp.concatenate(
                [ma + jnp.log(la), mb + jnp.log(lb), jnp.zeros((6, tq), F32)], axis=0)
            return 0

        lax.fori_loop(0, nq, q_step, 0)

    return pl.pallas_call(
        body, name="mla_fwd", grid=(4,),
        in_specs=[_full((s, 256), lambda p: (0, p)), _full((s, 256), lambda p: (0, p)),
                  _full((128, s), lambda p: (p, 0))],
        out_specs=[_full((s, 128), lambda p: (0, p)), _full((128, s), lambda p: (p, 0)),
                   _full((8, s), lambda p: (p, 0))],
        out_shape=[jax.ShapeDtypeStruct((s, 512), BF16), jax.ShapeDtypeStruct((512, s), BF16),
                   jax.ShapeDtypeStruct((32, s), F32)],
        compiler_params=_cparams(("parallel",)),
    )(q, k, vt)


def _band_masks(has_prev):
    r = lax.broadcasted_iota(jnp.int32, (Q_BLOCK, Q_BLOCK), 0)
    c = lax.broadcasted_iota(jnp.int32, (Q_BLOCK, Q_BLOCK), 1)
    return c >= r + jnp.where(has_prev, 0, Q_BLOCK), c <= r


def _dil_fwd(qa, ka, va, qo, ko, vo, d, name):
    s = qa.shape[0]
    nblk = s // Q_BLOCK
    per_seq = nblk // d

    def body(q_ref, k_ref, v_ref, o_ref, lse_ref):
        left = _left_mask()

        hms = (left, jnp.logical_not(left))

        def scores(b):
            r0 = pl.multiple_of(b * Q_BLOCK, Q_BLOCK)
            p0 = pl.multiple_of(jnp.maximum(b - 1, 0) * Q_BLOCK, Q_BLOCK)
            qb = q_ref[pl.ds(r0, Q_BLOCK), :]
            kp, kc = k_ref[pl.ds(p0, Q_BLOCK), :], k_ref[pl.ds(r0, Q_BLOCK), :]
            out = []
            for hh in range(2):
                qm = jnp.where(hms[hh], qb, jnp.zeros_like(qb))
                out.append((_dot_nt(qm, kp) * DIL_SCALE, _dot_nt(qm, kc) * DIL_SCALE))
            return out

        def finish(b, tiles):
            r0 = pl.multiple_of(b * Q_BLOCK, Q_BLOCK)
            p0 = pl.multiple_of(jnp.maximum(b - 1, 0) * Q_BLOCK, Q_BLOCK)
            pmask, cur_ok = _band_masks((b % per_seq) > 0)
            vp, vc = v_ref[pl.ds(p0, Q_BLOCK), :], v_ref[pl.ds(r0, Q_BLOCK), :]
            zero = jnp.zeros_like(vp)
            outs = []
            for hh in range(2):
                sp = jnp.where(pmask, tiles[hh][0], NEG)
                sc = jnp.where(cur_ok, tiles[hh][1], NEG)
                m = jnp.maximum(jnp.max(sp, axis=1, keepdims=True), jnp.max(sc, axis=1, keepdims=True))
                pp, pc = jnp.exp(sp - m), jnp.exp(sc - m)
                l = jnp.sum(pp, axis=1, keepdims=True) + jnp.sum(pc, axis=1, keepdims=True)
                acc = (_dot(pp.astype(BF16), jnp.where(hms[hh], vp, zero))
                       + _dot(pc.astype(BF16), jnp.where(hms[hh], vc, zero)))
                outs.append((acc / l, jnp.broadcast_to(m + jnp.log(l), (Q_BLOCK, LANES))))
            o_ref[pl.ds(r0, Q_BLOCK), :] = (outs[0][0] + outs[1][0]).astype(BF16)
            lse_ref[pl.ds(r0, Q_BLOCK), :] = jnp.where(left, outs[0][1], outs[1][1])

        def step(t, _):
            ta, tb = scores(2 * t), scores(2 * t + 1)
            finish(2 * t, ta)
            finish(2 * t + 1, tb)
            return 0

        lax.fori_loop(0, nblk // 2, step, 0)

    blk = lambda off: _full((s, 128), lambda p, off=off: (0, off + p))
    return pl.pallas_call(
        body, name=name, grid=(4,),
        in_specs=[blk(qo), blk(ko), blk(vo)],
        out_specs=[blk(0), blk(0)],
        out_shape=[jax.ShapeDtypeStruct((s, 512), BF16), jax.ShapeDtypeStruct((s, 512), F32)],
        compiler_params=_cparams(("parallel",)),
    )(qa, ka, va)


def _merge_fwd(proj, o_mla, od, lsed, wp_mla, wp_dil):
    s = proj.shape[0]
    tm = min(512, s)

    def body(zm_ref, zd_ref, gm_ref, gd_ref, om_ref, o0, o1, o2, l0, l1, l2, wm_ref, wd_ref,
             mg_out, ya_out, yd_out, odil_out, lse_out):
        la, lb, lc = l0[...], l1[...], l2[...]
        lmax = jnp.maximum(jnp.maximum(la, lb), lc)
        ea, eb, ec = jnp.exp(la - lmax), jnp.exp(lb - lmax), jnp.exp(lc - lmax)
        den = ea + eb + ec
        o_dil = (ea * o0[...].astype(F32) + eb * o1[...].astype(F32) + ec * o2[...].astype(F32)) / den
        o_dil = o_dil.astype(BF16)
        odil_out[...] = o_dil
        lse_out[...] = lmax + jnp.log(den)
        zm, zd = zm_ref[...].astype(F32), zd_ref[...].astype(F32)
        pa = (om_ref[...].astype(F32) * (zm * _sigmoid(zm))).astype(BF16)
        pd = (o_dil.astype(F32) * (zd * _sigmoid(zd))).astype(BF16)
        ya = _dot(pa, wm_ref[...])
        yd = _dot(pd, wd_ref[...])
        ya_out[...] = ya.astype(BF16)
        yd_out[...] = yd.astype(BF16)
        mg_out[...] = (_sigmoid(gm_ref[...].astype(F32)) * ya + _sigmoid(gd_ref[...].astype(F32)) * yd).astype(BF16)

    row = lambda i: (i, 0)
    cst = lambda i: (0, 0)
    r512 = _full((tm, 512), row)
    r1024 = _full((tm, 1024), row)
    return pl.pallas_call(
        body, name="merge_fwd", grid=(s // tm,),
        in_specs=[_full((tm, 512), lambda i: (i, Z_MLA // 512)), _full((tm, 512), lambda i: (i, Z_DIL // 512)),
                  _full((tm, 1024), lambda i: (i, G_MLA // 1024)), _full((tm, 1024), lambda i: (i, G_DIL // 1024)),
                  r512, r512, r512, r512, r512, r512, r512, _full((512, 1024), cst), _full((512, 1024), cst)],
        out_specs=[r1024, r1024, r1024, r512, r512],
        out_shape=[jax.ShapeDtypeStruct((s, 1024), BF16), jax.ShapeDtypeStruct((s, 1024), BF16),
                   jax.ShapeDtypeStruct((s, 1024), BF16), jax.ShapeDtypeStruct((s, 512), BF16),
                   jax.ShapeDtypeStruct((s, 512), F32)],
        compiler_params=_cparams(("parallel",)),
    )(proj, proj, proj, proj, o_mla, *od, *lsed, wp_mla, wp_dil)


def _out_loss(merged, w_out, x, target, gpost):
    s = x.shape[0]
    tm = min(512, s)

    def body(mg_ref, w_ref, x_ref, t_ref, g_ref, do_out, dy_out, loss_out, dg_out):
        i = pl.program_id(0)

        @pl.when(i == 0)
        def _():
            loss_out[...] = jnp.zeros_like(loss_out)
            dg_out[...] = jnp.zeros_like(dg_out)

        o = _dot(mg_ref[...], w_ref[...])
        g = g_ref[...]
        n, u, r = _rms(o, g)
        e = (x_ref[...] + n) - t_ref[...]
        loss_out[...] += 0.5 * jnp.sum(jnp.mean(e * e, axis=-1, keepdims=True))
        dy = e * (1.0 / D_MODEL)
        dy_out[...] = dy
        dg_out[...] += jnp.sum(dy * u, axis=0, keepdims=True)
        do_out[...] = _rms_bwd(dy, g, u, r).astype(BF16)

    row = lambda i: (i, 0)
    cst = lambda i: (0, 0)
    return pl.pallas_call(
        body, name="out_loss", grid=(s // tm,),
        in_specs=[_full((tm, 1024), row), _full((1024, 1024), cst), _full((tm, 1024), row), _full((tm, 1024), row),
                  _full((1, 1024), cst)],
        out_specs=[_full((tm, 1024), row), _full((tm, 1024), row), _full((8, LANES), cst), _full((1, 1024), cst)],
        out_shape=[jax.ShapeDtypeStruct((s, 1024), BF16), jax.ShapeDtypeStruct((s, 1024), F32),
                   jax.ShapeDtypeStruct((8, LANES), F32), jax.ShapeDtypeStruct((1, 1024), F32)],
        compiler_params=_cparams(("arbitrary",)),
    )(merged, w_out, x, target, gpost)


def _seg_sum64(x, ones_bd):
    hi = x.astype(BF16)
    lo = (x - hi.astype(F32)).astype(BF16)
    return _dot(hi, ones_bd) + _dot(lo, ones_bd)


def _merge_bwd(do, w_out, merged, proj, ya, yd, o_mla, o_dil, wp_mla, wp_dil):
    s = do.shape[0]
    tm = min(256, s)
    seg = jnp.arange(512) // 64
    ones_bd = (seg[:, None] == seg[None, :]).astype(BF16)

    def body(do_ref, wo_ref, mg_ref, zm_ref, zd_ref, gm_ref, gd_ref, ya_ref, yd_ref, om_ref, od_ref, wm_ref, wd_ref,
             bd_ref, dzm_out, dzd_out, dgm_out, dgd_out, dom_out, dod_out, domt_out, dd_out, dwo_out, dwm_out,
             dwd_out):
        i = pl.program_id(0)

        @pl.when(i == 0)
        def _():
            dwo_out[...] = jnp.zeros_like(dwo_out)
            dwm_out[...] = jnp.zeros_like(dwm_out)
            dwd_out[...] = jnp.zeros_like(dwd_out)

        dov = do_ref[...]
        dwo_out[...] += _dot_tn(mg_ref[...], dov)
        dm = _dot_nt(dov, wo_ref[...])
        for g_ref, y_ref, z_ref, o_ref, w_ref, dz_out, dg_out, dob_out, dd_o, dw_out in (
                (gm_ref, ya_ref, zm_ref, om_ref, wm_ref, dzm_out, dgm_out, dom_out, None, dwm_out),
                (gd_ref, yd_ref, zd_ref, od_ref, wd_ref, dzd_out, dgd_out, dod_out, dd_out, dwd_out)):
            sg = _sigmoid(g_ref[...].astype(F32))
            dg_out[...] = (dm * y_ref[...].astype(F32) * sg * (1.0 - sg)).astype(BF16)
            dy = (dm * sg).astype(BF16)
            z = z_ref[...].astype(F32)
            sz = _sigmoid(z)
            silu = z * sz
            ob = o_ref[...].astype(F32)
            dw_out[...] += _dot_tn((ob * silu).astype(BF16), dy)
            dp = _dot_nt(dy, w_ref[...])
            dz_out[...] = (dp * ob * (sz * (1.0 + z * (1.0 - sz)))).astype(BF16)
            dob = dp * silu
            dob_out[...] = dob.astype(BF16)
            if dd_o is None:
                domt_out[...] = dob.T.astype(BF16)
            else:
                dd_o[...] = _seg_sum64(dob * ob, bd_ref[...])

    row = lambda i: (i, 0)
    cst = lambda i: (0, 0)
    r512 = _full((tm, 512), row)
    r1024 = _full((tm, 1024), row)
    return pl.pallas_call(
        body, name="merge_bwd", grid=(s // tm,),
        in_specs=[r1024, _full((1024, 1024), cst), r1024,
                  _full((tm, 512), lambda i: (i, Z_MLA // 512)), _full((tm, 512), lambda i: (i, Z_DIL // 512)),
                  _full((tm, 1024), lambda i: (i, G_MLA // 1024)), _full((tm, 1024), lambda i: (i, G_DIL // 1024)),
                  r1024, r1024, r512, r512, _full((512, 1024), cst), _full((512, 1024), cst), _full((512, 512), cst)],
        out_specs=[r512, r512, r1024, r1024, r512, r512, _full((512, tm), lambda i: (0, i)), r512,
                   _full((1024, 1024), cst), _full((512, 1024), cst), _full((512, 1024), cst)],
        out_shape=[jax.ShapeDtypeStruct((s, 512), BF16), jax.ShapeDtypeStruct((s, 512), BF16),
                   jax.ShapeDtypeStruct((s, 1024), BF16), jax.ShapeDtypeStruct((s, 1024), BF16),
                   jax.ShapeDtypeStruct((s, 512), BF16), jax.ShapeDtypeStruct((s, 512), BF16),
                   jax.ShapeDtypeStruct((512, s), BF16), jax.ShapeDtypeStruct((s, 512), F32),
                   jax.ShapeDtypeStruct((1024, 1024), F32), jax.ShapeDtypeStruct((512, 1024), F32),
                   jax.ShapeDtypeStruct((512, 1024), F32)],
        compiler_params=_cparams(("arbitrary",)),
    )(do, w_out, merged, proj, proj, proj, proj, ya, yd, o_mla, o_dil, wp_mla, wp_dil, ones_bd)


def _mla_bwd(q, qt, k, kt, v, do, dot, ot, lse):
    s = q.shape[0]
    tq = tk = min(256, s)
    nq = s // tq

    def body(q_ref, qt_ref, k_ref, kt_ref, v_ref, do_ref, dot_ref, ot_ref, lse_ref, dqt_out, dkt_out, dvt_out,
             dqt_acc):
        left = _left_mask()
        causal = (lax.broadcasted_iota(jnp.int32, (tk, tq), 0) <= lax.broadcasted_iota(jnp.int32, (tk, tq), 1))
        dqt_acc[...] = jnp.zeros_like(dqt_acc)

        def kv_step(j, _):
            c0 = pl.multiple_of(j * tk, tk)
            vv = v_ref[pl.ds(c0, tk), :]
            khs = [k_ref[pl.ds(c0, tk), hh * 128:(hh + 1) * 128] for hh in range(2)]
            kths = [kt_ref[hh * 128:(hh + 1) * 128, pl.ds(c0, tk)] for hh in range(2)]
            vms = [jnp.where(left if hh == 0 else jnp.logical_not(left), vv, jnp.zeros_like(vv)) for hh in range(2)]

            def scores(i):
                r0 = pl.multiple_of(jnp.minimum(i, nq - 1) * tq, tq)
                dov = do_ref[pl.ds(r0, tq), :]
                return tuple((_dot_nt(khs[hh], q_ref[pl.ds(r0, tq), hh * 128:(hh + 1) * 128]) * MLA_SCALE,
                              _dot_nt(vms[hh], dov)) for hh in range(2))

            def update(i, tiles, acc, masked):
                r0 = pl.multiple_of(i * tq, tq)
                new = []
                for hh in range(2):
                    dkt, dvt = acc[hh]
                    st, dp = tiles[hh]
                    hrows = slice(hh * 128, (hh + 1) * 128)
                    drows = slice(hh * 64, (hh + 1) * 64)
                    doth = dot_ref[drows, pl.ds(r0, tq)]
                    dd = jnp.sum(doth.astype(F32) * ot_ref[drows, pl.ds(r0, tq)].astype(F32), axis=0, keepdims=True)
                    p = jnp.exp(st - lse_ref[hh:hh + 1, pl.ds(r0, tq)])
                    if masked:
                        p = jnp.where(causal, p, 0.0)
                    ds = (p * (dp - dd) * MLA_SCALE).astype(BF16)
                    dvt = dvt + _dot_nt(doth, p.astype(BF16))
                    dkt = dkt + _dot_nt(qt_ref[hrows, pl.ds(r0, tq)], ds)
                    dqt_acc[hrows, pl.ds(r0, tq)] += _dot(kths[hh], ds)
                    new.append((dkt, dvt))
                return tuple(new)

            init = tuple((jnp.zeros((128, tk), F32), jnp.zeros((64, tk), F32)) for _ in range(2))
            acc = update(j, scores(j), init, True)
            acc = lax.fori_loop(j + 1, nq, lambda i, a: update(i, scores(i), a, False), acc)
            for hh in range(2):
                dkt_out[hh * 128:(hh + 1) * 128, pl.ds(c0, tk)] = acc[hh][0].astype(BF16)
                dvt_out[hh * 64:(hh + 1) * 64, pl.ds(c0, tk)] = acc[hh][1].astype(BF16)
            return 0

        lax.fori_loop(0, nq, kv_step, 0)
        dqt_out[...] = dqt_acc[...].astype(BF16)

    b256 = _full((s, 256), lambda p: (0, p))
    b128 = _full((s, 128), lambda p: (0, p))
    t256 = _full((256, s), lambda p: (p, 0))
    t128 = _full((128, s), lambda p: (p, 0))
    return pl.pallas_call(
        body, name="mla_bwd", grid=(4,),
        in_specs=[b256, t256, b256, t256, b128, b128, t128, t128, _full((8, s), lambda p: (p, 0))],
        out_specs=[t256, t256, t128],
        out_shape=[jax.ShapeDtypeStruct((1024, s), BF16), jax.ShapeDtypeStruct((1024, s), BF16),
                   jax.ShapeDtypeStruct((512, s), BF16)],
        scratch_shapes=[pltpu.VMEM((256, s), F32)],
        compiler_params=_cparams(("parallel",)),
    )(q, qt, k, kt, v, do, dot, ot, lse)


def _mla_prep_bwd(dq, dk, dv, proj, gq, gkv, w_uq, w_uk, w_uv, q_tab, k_tab):
    s = proj.shape[0]
    tm = min(256, s)

    def body(dqt_ref, dkt_ref, dvt_ref, cq_ref, ckv_ref, gq_ref, gkv_ref, wq_ref, wk_ref, wv_ref,
             qc, qsp, qsm, kc, ksp, ksm,
             dcq_out, dkr_out, dckv_out, dwq_out, dwk_out, dwv_out, dgq_out, dgkv_out):
        i = pl.program_id(0)

        @pl.when(i == 0)
        def _():
            for r in (dwq_out, dwk_out, dwv_out, dgq_out, dgkv_out):
                r[...] = jnp.zeros_like(r)

        dqu = _unrope(dqt_ref[...].astype(F32).T, _tile_lanes(qc[...], 1024), _tile_lanes(qsp[...], 1024),
                      _tile_lanes(qsm[...], 1024), 16).astype(BF16)
        gq = gq_ref[...]
        cqn, xh, r = _rms(cq_ref[...].astype(F32), gq)
        dwq_out[...] += _dot_tn(cqn.astype(BF16), dqu)
        dcqn = _dot_nt(dqu, wq_ref[...])
        dgq_out[...] += jnp.sum(dcqn * xh, axis=0, keepdims=True)
        dcq_out[...] = _rms_bwd(dcqn, gq, xh, r).astype(BF16)

        dkf = dkt_ref[...].astype(F32).T
        dkb = dkf.astype(BF16)
        dsum = dkf[:, 0:128]
        for h in range(1, MLA_HEADS):
            dsum = dsum + dkf[:, h * 128:(h + 1) * 128]
        dkr_out[...] = _unrope(pltpu.roll(dsum, 64, 1), kc[...], ksp[...], ksm[...], 16).astype(BF16)

        dvb = dvt_ref[...].astype(F32).T.astype(BF16)
        gkv = gkv_ref[...]
        ckvn, xh2, r2 = _rms(ckv_ref[...].astype(F32), gkv)
        ckvn = ckvn.astype(BF16)
        dwk_out[...] += _dot_tn(ckvn, dkb)
        dwv_out[...] += _dot_tn(ckvn, dvb)
        dckvn = _dot_nt(dkb, wk_ref[...]) + _dot_nt(dvb, wv_ref[...])
        dgkv_out[...] += jnp.sum(dckvn * xh2, axis=0, keepdims=True)
        dckv_out[...] = _rms_bwd(dckvn, gkv, xh2, r2).astype(BF16)

    row = lambda i: (i, 0)
    cst = lambda i: (0, 0)
    tabs = [_full((tm, LANES), row)] * 6
    return pl.pallas_call(
        body, name="mla_prep_bwd", grid=(s // tm,),
        in_specs=[_full((1024, tm), lambda i: (0, i)), _full((1024, tm), lambda i: (0, i)),
                  _full((512, tm), lambda i: (0, i)),
                  _full((tm, 384), lambda i: (i, CQ_OFF // 384)), _full((tm, 256), lambda i: (i, CKV_OFF // 256)),
                  _full((1, 384), cst), _full((1, 256), cst),
                  _full((384, 1024), cst), _full((256, 1024), cst), _full((256, 512), cst)] + tabs,
        out_specs=[_full((tm, 384), row), _full((tm, 128), row), _full((tm, 256), row),
                   _full((384, 1024), cst), _full((256, 1024), cst), _full((256, 512), cst),
                   _full((1, 384), cst), _full((1, 256), cst)],
        out_shape=[jax.ShapeDtypeStruct((s, 384), BF16), jax.ShapeDtypeStruct((s, 128), BF16),
                   jax.ShapeDtypeStruct((s, 256), BF16),
                   jax.ShapeDtypeStruct((384, 1024), F32), jax.ShapeDtypeStruct((256, 1024), F32),
                   jax.ShapeDtypeStruct((256, 512), F32),
                   jax.ShapeDtypeStruct((1, 384), F32), jax.ShapeDtypeStruct((1, 256), F32)],
        compiler_params=_cparams(("arbitrary",)),
    )(dq, dk, dv, proj, proj, gq, gkv, w_uq, w_uk, w_uv, *q_tab, *k_tab)


def _dil_bwd(qa, ka, va, qo, ko, vo, do, lse, dd, tabs, d, name):
    s = qa.shape[0]
    nblk = s // Q_BLOCK
    per_seq = nblk // d

    def body(q_ref, k_ref, v_ref, do_ref, lse_ref, dd_ref, c_ref, sp_ref, sm_ref, dq_out, dk_out, dv_out,
             dk_acc, dv_acc):
        left = _left_mask()
        dk_acc[...] = jnp.zeros_like(dk_acc)
        dv_acc[...] = jnp.zeros_like(dv_acc)

        hms = (left, jnp.logical_not(left))

        def rows_of(b):
            r0 = pl.multiple_of(b * Q_BLOCK, Q_BLOCK)
            p0 = pl.multiple_of(jnp.maximum(b - 1, 0) * Q_BLOCK, Q_BLOCK)
            return pl.ds(r0, Q_BLOCK), pl.ds(p0, Q_BLOCK)

        def scores(b):
            rows, prow = rows_of(b)
            qb, dob = q_ref[rows, :], do_ref[rows, :]
            kp, kc, vp, vc = k_ref[prow, :], k_ref[rows, :], v_ref[prow, :], v_ref[rows, :]
            zero = jnp.zeros_like(qb)
            out = []
            for hh in range(2):
                qm, dom = jnp.where(hms[hh], qb, zero), jnp.where(hms[hh], dob, zero)
                out.append((_dot_nt(qm, kp) * DIL_SCALE, _dot_nt(qm, kc) * DIL_SCALE,
                            _dot_nt(dom, vp), _dot_nt(dom, vc)))
            return out

        def finish(b, tiles):
            rows, prow = rows_of(b)
            pmask, cur_ok = _band_masks((b % per_seq) > 0)
            qb, dob = q_ref[rows, :], do_ref[rows, :]
            kp, kc = k_ref[prow, :], k_ref[rows, :]
            lse_b, dd_b = lse_ref[rows, :], dd_ref[rows, :]
            zero = jnp.zeros_like(qb)
            dq = jnp.zeros((Q_BLOCK, LANES), F32)
            dkp = jnp.zeros((Q_BLOCK, LANES), F32)
            dkc = jnp.zeros((Q_BLOCK, LANES), F32)
            dvp = jnp.zeros((Q_BLOCK, LANES), F32)
            dvc = jnp.zeros((Q_BLOCK, LANES), F32)
            for hh in range(2):
                hm = hms[hh]
                qm, dom = jnp.where(hm, qb, zero), jnp.where(hm, dob, zero)
                lse_h, dd_h = _expand_half(lse_b, hh, left), _expand_half(dd_b, hh, left)
                sp, sc, dpp, dpc = tiles[hh]
                pp = jnp.where(pmask, jnp.exp(sp - lse_h), 0.0)
                pc = jnp.where(cur_ok, jnp.exp(sc - lse_h), 0.0)
                dsp = (pp * (dpp - dd_h) * DIL_SCALE).astype(BF16)
                dsc = (pc * (dpc - dd_h) * DIL_SCALE).astype(BF16)
                dq = dq + _dot(dsp, jnp.where(hm, kp, zero)) + _dot(dsc, jnp.where(hm, kc, zero))
                dkp, dkc = dkp + _dot_tn(dsp, qm), dkc + _dot_tn(dsc, qm)
                dvp, dvc = dvp + _dot_tn(pp.astype(BF16), dom), dvc + _dot_tn(pc.astype(BF16), dom)
            dq_out[rows, :] = _unrope(dq, c_ref[rows, :], sp_ref[rows, :], sm_ref[rows, :], 8).astype(BF16)
            dk_acc[prow, :] += dkp
            dv_acc[prow, :] += dvp
            dk_acc[rows, :] += dkc
            dv_acc[rows, :] += dvc

        def step(t, _):
            ta, tb = scores(2 * t), scores(2 * t + 1)
            finish(2 * t, ta)
            finish(2 * t + 1, tb)
            return 0

        lax.fori_loop(0, nblk // 2, step, 0)
        dk_out[...] = _unrope(dk_acc[...], c_ref[...], sp_ref[...], sm_ref[...], 8).astype(BF16)
        dv_out[...] = dv_acc[...].astype(BF16)

    blk = lambda off: _full((s, 128), lambda p, off=off: (0, off + p))
    tab = _full((s, 128), lambda p: (0, 0))
    return pl.pallas_call(
        body, name=name, grid=(4,),
        in_specs=[blk(qo), blk(ko), blk(vo), blk(0), blk(0), blk(0), tab, tab, tab],
        out_specs=[blk(0), blk(0), blk(0)],
        out_shape=[jax.ShapeDtypeStruct((s, 512), BF16)] * 3,
        scratch_shapes=[pltpu.VMEM((s, 128), F32), pltpu.VMEM((s, 128), F32)],
        compiler_params=_cparams(("parallel",)),
    )(qa, ka, va, do, lse, dd, *tabs)


def _dh_bwd(dproj, w_pad, x, gpre, dy):
    s = x.shape[0]
    tm, tk = min(512, s), 1408
    nk = IN_PAD // tk

    def body(dp_ref, w_ref, x_ref, g_ref, dy_ref, gx_out, dg_out, acc):
        i, kk = pl.program_id(0), pl.program_id(1)

        @pl.when(jnp.logical_and(i == 0, kk == 0))
        def _():
            dg_out[...] = jnp.zeros_like(dg_out)

        @pl.when(kk == 0)
        def _():
            acc[...] = jnp.zeros_like(acc)

        acc[...] += _dot_nt(dp_ref[...], w_ref[...])

        @pl.when(kk == nk - 1)
        def _():
            g = g_ref[...]
            _, xh, r = _rms(x_ref[...], g)
            dh = acc[...]
            dg_out[...] += jnp.sum(dh * xh, axis=0, keepdims=True)
            gx_out[...] = dy_ref[...] + _rms_bwd(dh, g, xh, r)

    row = lambda i, k: (i, 0)
    return pl.pallas_call(
        body, name="dh_bwd", grid=(s // tm, nk),
        in_specs=[_full((tm, tk), lambda i, k: (i, k)), _full((1024, tk), lambda i, k: (0, k)),
                  _full((tm, 1024), row), _full((1, 1024), lambda i, k: (0, 0)), _full((tm, 1024), row)],
        out_specs=[_full((tm, 1024), row), _full((1, 1024), lambda i, k: (0, 0))],
        out_shape=[jax.ShapeDtypeStruct((s, 1024), F32), jax.ShapeDtypeStruct((1, 1024), F32)],
        scratch_shapes=[pltpu.VMEM((tm, 1024), F32)],
        compiler_params=_cparams(("arbitrary", "arbitrary")),
    )(dproj, w_pad, x, gpre, dy)


def _dw_in(h, dproj):
    s = h.shape[0]
    ts, tn = min(1024, s), 768
    ns = s // ts

    def body(h_ref, dp_ref, o_ref, acc):
        k = pl.program_id(1)

        @pl.when(k == 0)
        def _():
            acc[...] = jnp.zeros_like(acc)

        acc[...] += _dot_tn(h_ref[...], dp_ref[...])

        @pl.when(k == ns - 1)
        def _():
            o_ref[...] = acc[...].astype(BF16)

    return pl.pallas_call(
        body, name="dw_in", grid=(IN_PAD // tn, ns),
        in_specs=[_full((ts, 1024), lambda j, k: (k, 0)), _full((ts, tn), lambda j, k: (k, j))],
        out_specs=_full((1024, tn), lambda j, k: (0, j)),
        out_shape=jax.ShapeDtypeStruct((1024, IN_PAD), BF16),
        scratch_shapes=[pltpu.VMEM((1024, tn), F32)],
        compiler_params=_cparams(("parallel", "arbitrary")),
    )(h, dproj)


def _exchange(arrays, per_peer, name):
    n = len(arrays)

    def body(*refs):
        ins, outs, sems = refs[:n], refs[n:2 * n], refs[2 * n]
        x, y, c = lax.axis_index("x"), lax.axis_index("y"), lax.axis_index("c")
        me = 4 * x + 2 * y + c
        flip = lambda v, b: (1 - v) if b else v

        def copies(j, outgoing):
            jx, jy, jc = (j >> 2) & 1, (j >> 1) & 1, j & 1
            peer = (flip(x, jx), flip(y, jy), flip(c, jc))
            peer_idx = 4 * peer[0] + 2 * peer[1] + peer[2]
            slot = me if outgoing else peer_idx
            return [pltpu.make_async_remote_copy(
                src_ref=ins[a].at[peer_idx] if per_peer[a] else ins[a], dst_ref=outs[a].at[slot],
                send_sem=sems.at[2 * a, j], recv_sem=sems.at[2 * a + 1, j], device_id=peer,
                device_id_type=pl.DeviceIdType.MESH) for a in range(n)]

        own = [pltpu.make_async_copy(ins[a].at[me] if per_peer[a] else ins[a], outs[a].at[me], sems.at[2 * a, 0])
               for a in range(n)]
        for cp in own:
            cp.start()
        sent = [cp for j in range(1, N_DEV) for cp in copies(j, True)]
        for cp in sent:
            cp.start()
        for j in range(1, N_DEV):
            for cp in copies(j, False):
                cp.wait_recv()
        for cp in sent:
            cp.wait_send()
        for cp in own:
            cp.wait()

    hbm = pl.BlockSpec(memory_space=pl.ANY)
    out_shape = [jax.ShapeDtypeStruct(a.shape if p else (N_DEV,) + a.shape, a.dtype)
                 for a, p in zip(arrays, per_peer)]
    return pl.pallas_call(
        body, name=name, in_specs=[hbm] * n, out_specs=[hbm] * n, out_shape=out_shape,
        scratch_shapes=[pltpu.SemaphoreType.DMA((2 * n, N_DEV))],
    )(*arrays)


def _adam_math(w, g, m, v):
    m = ADAM_B1 * m + (1.0 - ADAM_B1) * g
    v = ADAM_B2 * v + (1.0 - ADAM_B2) * (g * g)
    m_hat = m / (1.0 - ADAM_B1 ** ADAM_STEP)
    v_hat = v / (1.0 - ADAM_B2 ** ADAM_STEP)
    delta = -ADAM_LR * (m_hat / (jnp.sqrt(v_hat) + ADAM_EPS) + ADAM_WD * w)
    return delta, m, v


def _adam(recv, w, m, v, name, tr):
    _, rows, cols = w.shape

    def body(r_ref, w_ref, m_ref, v_ref, g_out, d_out, m_out, v_out):
        g = r_ref[0].astype(F32)
        for k in range(1, N_DEV):
            g = g + r_ref[k].astype(F32)
        g_out[0] = g
        d_out[0], m_out[0], v_out[0] = _adam_math(w_ref[0], g, m_ref[0], v_ref[0])

    blk = _full((1, tr, cols), lambda i: (0, i, 0))
    return pl.pallas_call(
        body, name=name, grid=(rows // tr,),
        in_specs=[_full((N_DEV, tr, cols), lambda i: (0, i, 0)), blk, blk, blk],
        out_specs=[blk] * 4,
        out_shape=[jax.ShapeDtypeStruct(w.shape, F32)] * 4,
        compiler_params=_cparams(("parallel",)),
    )(recv, w, m, v)


def _adam_gains(recv, gains, gains_m, gains_v):
    def body(*refs):
        r_ref, w, m, v = refs[0], refs[1:5], refs[5:9], refs[9:13]
        g_out, d_out, m_out, v_out, loss_out = refs[13:17], refs[17:21], refs[21:25], refs[25:29], refs[29]
        tot = r_ref[0:1, :]
        for k in range(1, N_DEV):
            tot = tot + r_ref[k:k + 1, :]
        for t in range(4):
            g = tot[:, GAIN_OFFS[t]:GAIN_OFFS[t] + GAIN_WIDTHS[t]]
            g_out[t][...] = g
            d_out[t][...], m_out[t][...], v_out[t][...] = _adam_math(w[t][...], g, m[t][...], v[t][...])
        loss_out[...] = tot[:, LOSS_OFF:LOSS_OFF + LANES]

    shapes = [jax.ShapeDtypeStruct((1, n), F32) for n in GAIN_WIDTHS]
    return pl.pallas_call(
        body, name="adam_gains", out_shape=shapes * 4 + [jax.ShapeDtypeStruct((1, LANES), F32)],
    )(recv, *gains, *gains_m, *gains_v)


def _local_step(x, positions, gains, weights, target):
    gpre, gq, gkv, gpost = gains
    w_pad, w_uq, w_uk, w_uv, wp_mla, wp_dil, w_out = weights
    q_tab, k_tab, d_tab = _rope_tables(positions)

    proj, h = _inproj(x, gpre, w_pad, d_tab)
    q, k, v, qt, kt, vt = _mla_prep(proj, gq, gkv, w_uq, w_uk, w_uv, q_tab, k_tab)
    o_mla, ot_mla, lse_mla = _mla_fwd(q, k, vt)

    qkv_perm, od, lsed = [], [], []
    for g, d in enumerate(DIL_DILATIONS):
        if d == 1:
            arr, offs = proj, ((Q_OFF + 512 * g) // 128, (K_OFF + 512 * g) // 128, (V_OFF + 512 * g) // 128)
        else:
            cols = jnp.concatenate([proj[:, o + 512 * g:o + 512 * (g + 1)] for o in (Q_OFF, K_OFF, V_OFF)], axis=1)
            arr, offs = _perm(cols, d), (0, 4, 8)
        qkv_perm.append((arr, offs))
        o_g, lse_g = _dil_fwd(arr, arr, arr, *offs, d, "dil_fwd_%d" % g)
        od.append(_unperm(o_g, d))
        lsed.append(_unperm(lse_g, d))

    merged, ya, yd, o_dil, lse_dil = _merge_fwd(proj, o_mla, od, lsed, wp_mla, wp_dil)
    do, dy, loss, dgpost = _out_loss(merged, w_out, x, target, gpost)

    (dz_mla, dz_dil, dg_mla, dg_dil, do_mla, do_dil, dot_mla, dd_dil, dw_out, dwp_mla, dwp_dil) = _merge_bwd(
        do, w_out, merged, proj, ya, yd, o_mla, o_dil, wp_mla, wp_dil)

    dq, dk, dv = _mla_bwd(q, qt, k, kt, v, do_mla, dot_mla, ot_mla, lse_mla)
    dcq, dkr, dckv, dw_uq, dw_uk, dw_uv, dgq, dgkv = _mla_prep_bwd(dq, dk, dv, proj, gq, gkv, w_uq, w_uk, w_uv,
                                                                   q_tab, k_tab)

    dqs, dks, dvs = [], [], []
    for g, d in enumerate(DIL_DILATIONS):
        arr, offs = qkv_perm[g]
        tabs = tuple(_perm(t, d) for t in d_tab)
        dq_g, dk_g, dv_g = _dil_bwd(arr, arr, arr, *offs, _perm(do_dil, d), _perm(lse_dil, d), _perm(dd_dil, d),
                                    tabs, d, "dil_bwd_%d" % g)
        dqs.append(_unperm(dq_g, d))
        dks.append(_unperm(dk_g, d))
        dvs.append(_unperm(dv_g, d))

    dproj = jnp.concatenate([dz_mla, dz_dil, dg_mla, dg_dil] + dqs + dks + dvs + [dcq, dkr, dckv], axis=1)
    grad_x, dgpre = _dh_bwd(dproj, w_pad, x, gpre, dy)
    dw_in = _dw_in(h, dproj)
    return loss, grad_x, (dgpre, dgq, dgkv, dgpost), (dw_in, dw_uq, dw_uk, dw_uv, dwp_mla, dwp_dil, dw_out)


ADAM_ROWS = (256, 384, 256, 512, 512, 128)


def kernel(x, positions, pre_norm_g, w_in, q_norm_g, w_uq, kv_norm_g, w_ukv, w_proj_mla, w_proj_dil, w_out, post_norm_g, loss_target, m_pre_norm_g, m_w_in, m_q_norm_g, m_w_uq, m_kv_norm_g, m_w_ukv, m_w_proj_mla, m_w_proj_dil, m_w_out, m_post_norm_g, v_pre_norm_g, v_w_in, v_q_norm_g, v_w_uq, v_kv_norm_g, v_w_ukv, v_w_proj_mla, v_w_proj_dil, v_w_out, v_post_norm_g):
    big_w = (w_in, w_uq, w_ukv, w_proj_mla, w_proj_dil, w_out)
    big_m = (m_w_in, m_w_uq, m_w_ukv, m_w_proj_mla, m_w_proj_dil, m_w_out)
    big_v = (v_w_in, v_w_uq, v_w_ukv, v_w_proj_mla, v_w_proj_dil, v_w_out)
    gains = (pre_norm_g, q_norm_g, kv_norm_g, post_norm_g)
    gains_m = (m_pre_norm_g, m_q_norm_g, m_kv_norm_g, m_post_norm_g)
    gains_v = (v_pre_norm_g, v_q_norm_g, v_kv_norm_g, v_post_norm_g)

    gathered = _exchange([w[0].astype(BF16) for w in big_w], [False] * 6, "gather_weights")
    weights = _assemble_weights(*gathered)

    loss, grad_x, dgains, dweights = _local_step(x[0], positions[0], gains, weights, loss_target[0])

    packet = jnp.concatenate(list(dgains) + [loss[0:1]], axis=1)
    received = _exchange(_grad_chunks(*dweights) + [packet], [True] * 6 + [False], "exchange_grads")

    big = [_adam(received[t], big_w[t], big_m[t], big_v[t], "adam_%d" % t, ADAM_ROWS[t]) for t in range(6)]
    small = _adam_gains(received[6].reshape(N_DEV, PACKET), gains, gains_m, gains_v)

    def interleave(kind):
        s_pre, s_q, s_kv, s_post = small[4 * kind:4 * kind + 4]
        b_in, b_uq, b_ukv, b_pm, b_pd, b_out = (big[t][kind] for t in range(6))
        return [s_pre, b_in, s_q, b_uq, s_kv, b_ukv, b_pm, b_pd, b_out, s_post]

    return (small[16][0, 0], grad_x[None], *interleave(0), *interleave(1), *interleave(2), *interleave(3))
```

```python
import numpy as np
import jax
import jax.numpy as jnp
from jax import lax
from jax.experimental import pallas as pl
from jax.experimental.pallas import tpu as pltpu

F32 = jnp.float32
BF16 = jnp.bfloat16

D_MODEL = 1024
NORM_EPS = 1e-6
ROPE_THETA = 500000.0
N_DEV = 8
LANES = 128
NEG = -1e30

MLA_HEADS = 8
MLA_Q_RANK = 384
MLA_KV_RANK = 256
MLA_SCALE = 96.0 ** -0.5
DIL_DILATIONS = (1, 4, 16)
DIL_SCALE = 0.125
Q_BLOCK = 128

Z_MLA, Z_DIL, G_MLA, G_DIL = 0, 512, 1024, 2048
Q_OFF, K_OFF, V_OFF = 3072, 4608, 6144
CQ_OFF, KR_OFF, CKV_OFF, IN_PAD = 7680, 8064, 8192, 8448
IN_WIDTH = 8352
SHARD_W = IN_WIDTH // 8
IN_SEGS = ((0, 384, CQ_OFF), (384, 256, CKV_OFF), (640, 32, KR_OFF), (672, 1536, Q_OFF), (2208, 1536, K_OFF),
           (3744, 1536, V_OFF), (5280, 512, Z_MLA), (5792, 512, Z_DIL), (6304, 1024, G_MLA), (7328, 1024, G_DIL))

GAIN_OFFS = (0, 1024, 1408, 1664)
GAIN_WIDTHS = (1024, 384, 256, 1024)
LOSS_OFF, PACKET = 2688, 2816

ADAM_LR, ADAM_B1, ADAM_B2, ADAM_EPS, ADAM_WD, ADAM_STEP = 0.001, 0.9, 0.999, 1e-08, 0.01, 10

VMEM_LIMIT_MB = 56


def _cparams(sem=None, vmem_mb=VMEM_LIMIT_MB):
    return pltpu.CompilerParams(dimension_semantics=sem, vmem_limit_bytes=vmem_mb * 1024 * 1024)


def _dot(a, b):
    return jnp.dot(a, b, preferred_element_type=F32)


def _dot_nt(a, b):
    return lax.dot_general(a, b, (((1,), (1,)), ((), ())), preferred_element_type=F32)


def _dot_tn(a, b):
    return lax.dot_general(a, b, (((0,), (0,)), ((), ())), preferred_element_type=F32)


def _tile_lanes(t, width):
    return t if width == t.shape[1] else jnp.tile(t, (1, width // t.shape[1]))


def _rope(x, c, sp, sm, a):
    n = x.shape[1]
    return x * c + pltpu.roll(x, a, 1) * sp + pltpu.roll(x, n - a, 1) * sm


def _unrope(dy, c, sp, sm, a):
    n = dy.shape[1]
    return dy * c + pltpu.roll(dy * sp, n - a, 1) + pltpu.roll(dy * sm, a, 1)


def _sigmoid(z):
    return 1.0 / (1.0 + jnp.exp(-z))


def _left_mask():
    return lax.broadcasted_iota(jnp.int32, (1, LANES), 1) < 64


def _expand_half(x, hh, left):
    r = pltpu.roll(x, 64, 1)
    return jnp.where(left, x, r) if hh == 0 else jnp.where(left, r, x)


def _rms(xv, g):
    r = lax.rsqrt(jnp.mean(xv * xv, axis=-1, keepdims=True) + NORM_EPS)
    xh = xv * r
    return xh * g, xh, r


def _rms_bwd(dout, g, xh, r):
    dxh = dout * g
    return r * (dxh - xh * jnp.mean(dxh * xh, axis=-1, keepdims=True))


def _full(shape, index_map):
    return pl.BlockSpec(shape, index_map)


def _w_in_pieces():
    out = []
    for s, n, off in sorted(IN_SEGS, key=lambda t: t[2]):
        c = s
        while c < s + n:
            k = c // SHARD_W
            e = min(s + n, (k + 1) * SHARD_W)
            out.append((k, c - k * SHARD_W, e - c, off + (c - s)))
            c = e
    return out


def _assemble_w_in(g):
    parts, cur = [], 0
    for k, a, w, off in _w_in_pieces():
        if off > cur:
            parts.append(jnp.zeros((D_MODEL, off - cur), g.dtype))
        parts.append(g[k, :, a:a + w])
        cur = off + w
    if cur < IN_PAD:
        parts.append(jnp.zeros((D_MODEL, IN_PAD - cur), g.dtype))
    return jnp.concatenate(parts, axis=1)


def _dw_in_chunks(dw):
    chunks = []
    for dev in range(N_DEV):
        mine = sorted((p for p in _w_in_pieces() if p[0] == dev), key=lambda p: p[1])
        chunks.append(jnp.concatenate([dw[:, off:off + w] for k, a, w, off in mine], axis=1))
    return jnp.stack(chunks)


def _assemble_weights(g_in, g_uq, g_ukv, g_pm, g_pd, g_out):
    w_uq_pad = jnp.pad(g_uq.transpose(1, 0, 2), ((0, 0), (0, 0), (0, 32))).reshape(384, 1024)
    ukv = g_ukv.transpose(1, 0, 2)
    w_uk_pad = jnp.pad(ukv[:, :, :64], ((0, 0), (0, 0), (0, 64))).reshape(256, 1024)
    w_uv = ukv[:, :, 64:].reshape(256, 512)
    wp_mla = g_pm.transpose(1, 0, 2).reshape(512, 1024)
    wp_dil = g_pd.transpose(1, 0, 2).reshape(512, 1024)
    return _assemble_w_in(g_in), w_uq_pad, w_uk_pad, w_uv, wp_mla, wp_dil, g_out.reshape(1024, 1024)


def _grad_chunks(dw_in_pad, dw_uq_pad, dw_uk_pad, dw_uv, dwp_mla, dwp_dil, dw_out):
    a = _dw_in_chunks(dw_in_pad)
    b = dw_uq_pad.reshape(384, 8, 128)[:, :, :96].transpose(1, 0, 2)
    c = jnp.concatenate([dw_uk_pad.reshape(256, 8, 128)[:, :, :64], dw_uv.reshape(256, 8, 64)], axis=2)
    c = c.transpose(1, 0, 2)
    d = dwp_mla.reshape(512, N_DEV, 128).transpose(1, 0, 2)
    e = dwp_dil.reshape(512, N_DEV, 128).transpose(1, 0, 2)
    f = dw_out.reshape(N_DEV, 128, 1024)
    return [t.astype(BF16) for t in (a, b, c, d, e, f)]


def _lane_consts(freqs, half, first, period):
    rel = (np.arange(LANES) % period) - first
    rot = (rel >= 0) & (rel < 2 * half)
    freq = np.where(rot, freqs[np.clip(rel, 0, 2 * half - 1) % half], 0.0).astype(np.float32)
    x1 = (rot & (rel < half)).astype(np.float32)
    x2 = (rot & (rel >= half)).astype(np.float32)
    return freq[None, :], x1[None, :], x2[None, :]


def _rope_tables(pos):
    p = pos.astype(F32)[:, None]
    inv_m = np.float32(ROPE_THETA) ** (-(np.arange(0, 32, 2, dtype=np.float32) / np.float32(32)))
    inv_d = np.float32(ROPE_THETA) ** (-(np.arange(0, 16, 2, dtype=np.float32) / np.float32(16)))
    lane = np.arange(LANES)
    tabs = []
    for freqs, half, first, period, keep in ((inv_m, 16, 64, 128, lane < 96), (inv_m, 16, 0, 128, lane < 32),
                                              (inv_d, 8, 0, 64, lane >= 0)):
        freq, x1, x2 = _lane_consts(freqs, half, first, period)
        ang = p * freq
        sin = jnp.sin(ang)
        tabs.append((jnp.cos(ang) * keep.astype(np.float32)[None, :], sin * x2, sin * (-x1)))
    return tuple(tabs)


def _perm(a, d):
    if d == 1:
        return a
    s, c = a.shape
    return a.reshape(s // d, d, c).transpose(1, 0, 2).reshape(s, c)


def _unperm(a, d):
    if d == 1:
        return a
    s, c = a.shape
    return a.reshape(d, s // d, c).transpose(1, 0, 2).reshape(s, c)


def _inproj(x, gpre, w_pad, d_tab):
    s = x.shape[0]
    tm, tn = min(1024, s), 768
    rope_lo, rope_hi = Q_OFF // tn, V_OFF // tn

    def body(x_ref, g_ref, w_ref, c_ref, sp_ref, sm_ref, o_ref, h_ref):
        j = pl.program_id(1)

        @pl.when(j == 0)
        def _():
            hv, _, _ = _rms(x_ref[...], g_ref[...])
            h_ref[...] = hv.astype(BF16)

        acc = _dot(h_ref[...], w_ref[...])
        is_rope = jnp.logical_and(j >= rope_lo, j < rope_hi)

        @pl.when(is_rope)
        def _():
            o_ref[...] = _rope(acc, _tile_lanes(c_ref[...], tn), _tile_lanes(sp_ref[...], tn),
                               _tile_lanes(sm_ref[...], tn), 8).astype(BF16)

        @pl.when(jnp.logical_not(is_rope))
        def _():
            o_ref[...] = acc.astype(BF16)

    row = lambda i, j: (i, 0)
    return pl.pallas_call(
        body, name="inproj", grid=(s // tm, IN_PAD // tn),
        in_specs=[_full((tm, D_MODEL), row), _full((1, D_MODEL), lambda i, j: (0, 0)),
                  _full((D_MODEL, tn), lambda i, j: (0, j)),
                  _full((tm, LANES), row), _full((tm, LANES), row), _full((tm, LANES), row)],
        out_specs=[_full((tm, tn), lambda i, j: (i, j)), _full((tm, D_MODEL), row)],
        out_shape=[jax.ShapeDtypeStruct((s, IN_PAD), BF16), jax.ShapeDtypeStruct((s, D_MODEL), BF16)],
        compiler_params=_cparams(("parallel", "arbitrary")),
    )(x, gpre, w_pad, *d_tab)


def _mla_prep(proj, gq, gkv, w_uq, w_uk, w_uv, q_tab, k_tab):
    s = proj.shape[0]
    tm = min(512, s)

    def body(cq_ref, kr_ref, ckv_ref, gq_ref, gkv_ref, wq_ref, wk_ref, wv_ref,
             qc, qsp, qsm, kc, ksp, ksm, q_out, k_out, v_out, qt_out, kt_out, vt_out):
        cqn, _, _ = _rms(cq_ref[...].astype(F32), gq_ref[...])
        q = _dot(cqn.astype(BF16), wq_ref[...])
        q = _rope(q, _tile_lanes(qc[...], 1024), _tile_lanes(qsp[...], 1024), _tile_lanes(qsm[...], 1024), 16)
        q_out[...] = q.astype(BF16)
        qt_out[...] = q.T.astype(BF16)
        ckvn, _, _ = _rms(ckv_ref[...].astype(F32), gkv_ref[...])
        ckvn = ckvn.astype(BF16)
        kr = _rope(kr_ref[...].astype(F32), kc[...], ksp[...], ksm[...], 16)
        k = _dot(ckvn, wk_ref[...]) + _tile_lanes(pltpu.roll(kr, 64, 1), 1024)
        k_out[...] = k.astype(BF16)
        kt_out[...] = k.T.astype(BF16)
        v = _dot(ckvn, wv_ref[...])
        v_out[...] = v.astype(BF16)
        vt_out[...] = v.T.astype(BF16)

    row = lambda i: (i, 0)
    col = lambda i: (0, i)
    cst = lambda i: (0, 0)
    tabs = [_full((tm, LANES), row)] * 6
    return pl.pallas_call(
        body, name="mla_prep", grid=(s // tm,),
        in_specs=[_full((tm, 384), lambda i: (i, CQ_OFF // 384)), _full((tm, 128), lambda i: (i, KR_OFF // 128)),
                  _full((tm, 256), lambda i: (i, CKV_OFF // 256)), _full((1, 384), cst), _full((1, 256), cst),
                  _full((384, 1024), cst), _full((256, 1024), cst), _full((256, 512), cst)] + tabs,
        out_specs=[_full((tm, 1024), row), _full((tm, 1024), row), _full((tm, 512), row),
                   _full((1024, tm), col), _full((1024, tm), col), _full((512, tm), col)],
        out_shape=[jax.ShapeDtypeStruct((s, 1024), BF16), jax.ShapeDtypeStruct((s, 1024), BF16),
                   jax.ShapeDtypeStruct((s, 512), BF16), jax.ShapeDtypeStruct((1024, s), BF16),
                   jax.ShapeDtypeStruct((1024, s), BF16), jax.ShapeDtypeStruct((512, s), BF16)],
        compiler_params=_cparams(("parallel",)),
    )(proj, proj, proj, gq, gkv, w_uq, w_uk, w_uv, *q_tab, *k_tab)


def _mla_fwd(q, k, vt):
    s = q.shape[0]
    tq = tk = min(256, s)
    nq = s // tq

    def body(q_ref, k_ref, vt_ref, o_ref, ot_ref, lse_ref):
        causal = (lax.broadcasted_iota(jnp.int32, (tk, tq), 0) <= lax.broadcasted_iota(jnp.int32, (tk, tq), 1))

        def q_step(i, _):
            r0 = pl.multiple_of(i * tq, tq)
            qs = [q_ref[pl.ds(r0, tq), hh * 128:(hh + 1) * 128] for hh in range(2)]

            def scores(j):
                c0 = pl.multiple_of(j * tk, tk)
                return tuple(_dot_nt(k_ref[pl.ds(c0, tk), hh * 128:(hh + 1) * 128], qs[hh]) * MLA_SCALE
                             for hh in range(2))

            def update(j, sts, stats, masked):
                c0 = pl.multiple_of(j * tk, tk)
                new = []
                for hh in range(2):
                    m, l, acc = stats[hh]
                    st = jnp.where(causal, sts[hh], NEG) if masked else sts[hh]
                    m_new = jnp.maximum(m, jnp.max(st, axis=0, keepdims=True))
                    alpha = jnp.exp(m - m_new)
                    p = jnp.exp(st - m_new)
                    l = alpha * l + jnp.sum(p, axis=0, keepdims=True)
                    acc = acc * alpha + _dot(vt_ref[hh * 64:(hh + 1) * 64, pl.ds(c0, tk)], p.astype(BF16))
                    new.append((m_new, l, acc))
                return tuple(new)

            def kv_step(j, carry):
                sts, stats = carry
                nxt = scores(j + 1)
                return nxt, update(j, sts, stats, False)

            init = tuple((jnp.full((1, tq), NEG, F32), jnp.zeros((1, tq), F32), jnp.zeros((64, tq), F32))
                         for _ in range(2))
            sts, stats = lax.fori_loop(0, i, kv_step, (scores(0), init))
            (ma, la, acca), (mb, lb, accb) = update(i, sts, stats, True)
            ot = jnp.concatenate([acca / la, accb / lb], axis=0)
            ot_ref[:, pl.ds(r0, tq)] = ot.astype(BF16)
            o_ref[pl.ds(r0, tq), :] = ot.T.astype(BF16)
            lse_ref[:, pl.ds(r0, tq)] = jnp.concatenate(
                [ma + jnp.log(la), mb + jnp.log(lb), jnp.zeros((6, tq), F32)], axis=0)
            return 0

        lax.fori_loop(0, nq, q_step, 0)

    return pl.pallas_call(
        body, name="mla_fwd", grid=(4,),
        in_specs=[_full((s, 256), lambda p: (0, p)), _full((s, 256), lambda p: (0, p)),
                  _full((128, s), lambda p: (p, 0))],
        out_specs=[_full((s, 128), lambda p: (0, p)), _full((128, s), lambda p: (p, 0)),
                   _full((8, s), lambda p: (p, 0))],
        out_shape=[jax.ShapeDtypeStruct((s, 512), BF16), jax.ShapeDtypeStruct((512, s), BF16),
                   jax.ShapeDtypeStruct((32, s), F32)],
        compiler_params=_cparams(("parallel",)),
    )(q, k, vt)


def _band_masks(has_prev):
    r = lax.broadcasted_iota(jnp.int32, (Q_BLOCK, Q_BLOCK), 0)
    c = lax.broadcasted_iota(jnp.int32, (Q_BLOCK, Q_BLOCK), 1)
    return c >= r + jnp.where(has_prev, 0, Q_BLOCK), c <= r


def _dil_fwd(qa, ka, va, qo, ko, vo, d, name):
    s = qa.shape[0]
    nblk = s // Q_BLOCK
    per_seq = nblk // d

    def body(q_ref, k_ref, v_ref, o_ref, lse_ref):
        left = _left_mask()

        hms = (left, jnp.logical_not(left))

        def scores(b):
            r0 = pl.multiple_of(b * Q_BLOCK, Q_BLOCK)
            p0 = pl.multiple_of(jnp.maximum(b - 1, 0) * Q_BLOCK, Q_BLOCK)
            qb = q_ref[pl.ds(r0, Q_BLOCK), :]
            kp, kc = k_ref[pl.ds(p0, Q_BLOCK), :], k_ref[pl.ds(r0, Q_BLOCK), :]
            out = []
            for hh in range(2):
                qm = jnp.where(hms[hh], qb, jnp.zeros_like(qb))
                out.append((_dot_nt(qm, kp) * DIL_SCALE, _dot_nt(qm, kc) * DIL_SCALE))
            return out

        def finish(b, tiles):
            r0 = pl.multiple_of(b * Q_BLOCK, Q_BLOCK)
            p0 = pl.multiple_of(jnp.maximum(b - 1, 0) * Q_BLOCK, Q_BLOCK)
            pmask, cur_ok = _band_masks((b % per_seq) > 0)
            vp, vc = v_ref[pl.ds(p0, Q_BLOCK), :], v_ref[pl.ds(r0, Q_BLOCK), :]
            zero = jnp.zeros_like(vp)
            outs = []
            for hh in range(2):
                sp = jnp.where(pmask, tiles[hh][0], NEG)
                sc = jnp.where(cur_ok, tiles[hh][1], NEG)
                m = jnp.maximum(jnp.max(sp, axis=1, keepdims=True), jnp.max(sc, axis=1, keepdims=True))
                pp, pc = jnp.exp(sp - m), jnp.exp(sc - m)
                l = jnp.sum(pp, axis=1, keepdims=True) + jnp.sum(pc, axis=1, keepdims=True)
                acc = (_dot(pp.astype(BF16), jnp.where(hms[hh], vp, zero))
                       + _dot(pc.astype(BF16), jnp.where(hms[hh], vc, zero)))
                outs.append((acc / l, jnp.broadcast_to(m + jnp.log(l), (Q_BLOCK, LANES))))
            o_ref[pl.ds(r0, Q_BLOCK), :] = (outs[0][0] + outs[1][0]).astype(BF16)
            lse_ref[pl.ds(r0, Q_BLOCK), :] = jnp.where(left, outs[0][1], outs[1][1])

        def step(t, _):
            ta, tb = scores(2 * t), scores(2 * t + 1)
            finish(2 * t, ta)
            finish(2 * t + 1, tb)
            return 0

        lax.fori_loop(0, nblk // 2, step, 0)

    blk = lambda off: _full((s, 128), lambda p, off=off: (0, off + p))
    return pl.pallas_call(
        body, name=name, grid=(4,),
        in_specs=[blk(qo), blk(ko), blk(vo)],
        out_specs=[blk(0), blk(0)],
        out_shape=[jax.ShapeDtypeStruct((s, 512), BF16), jax.ShapeDtypeStruct((s, 512), F32)],
        compiler_params=_cparams(("parallel",)),
    )(qa, ka, va)


def _merge_fwd(proj, o_mla, od, lsed, wp_mla, wp_dil):
    s = proj.shape[0]
    tm = min(512, s)

    def body(zm_ref, zd_ref, gm_ref, gd_ref, om_ref, o0, o1, o2, l0, l1, l2, wm_ref, wd_ref,
             mg_out, ya_out, yd_out, odil_out, lse_out):
        la, lb, lc = l0[...], l1[...], l2[...]
        lmax = jnp.maximum(jnp.maximum(la, lb), lc)
        ea, eb, ec = jnp.exp(la - lmax), jnp.exp(lb - lmax), jnp.exp(lc - lmax)
        den = ea + eb + ec
        o_dil = (ea * o0[...].astype(F32) + eb * o1[...].astype(F32) + ec * o2[...].astype(F32)) / den
        o_dil = o_dil.astype(BF16)
        odil_out[...] = o_dil
        lse_out[...] = lmax + jnp.log(den)
        zm, zd = zm_ref[...].astype(F32), zd_ref[...].astype(F32)
        pa = (om_ref[...].astype(F32) * (zm * _sigmoid(zm))).astype(BF16)
        pd = (o_dil.astype(F32) * (zd * _sigmoid(zd))).astype(BF16)
        ya = _dot(pa, wm_ref[...])
        yd = _dot(pd, wd_ref[...])
        ya_out[...] = ya.astype(BF16)
        yd_out[...] = yd.astype(BF16)
        mg_out[...] = (_sigmoid(gm_ref[...].astype(F32)) * ya + _sigmoid(gd_ref[...].astype(F32)) * yd).astype(BF16)

    row = lambda i: (i, 0)
    cst = lambda i: (0, 0)
    r512 = _full((tm, 512), row)
    r1024 = _full((tm, 1024), row)
    return pl.pallas_call(
        body, name="merge_fwd", grid=(s // tm,),
        in_specs=[_full((tm, 512), lambda i: (i, Z_MLA // 512)), _full((tm, 512), lambda i: (i, Z_DIL // 512)),
                  _full((tm, 1024), lambda i: (i, G_MLA // 1024)), _full((tm, 1024), lambda i: (i, G_DIL // 1024)),
                  r512, r512, r512, r512, r512, r512, r512, _full((512, 1024), cst), _full((512, 1024), cst)],
        out_specs=[r1024, r1024, r1024, r512, r512],
        out_shape=[jax.ShapeDtypeStruct((s, 1024), BF16), jax.ShapeDtypeStruct((s, 1024), BF16),
                   jax.ShapeDtypeStruct((s, 1024), BF16), jax.ShapeDtypeStruct((s, 512), BF16),
                   jax.ShapeDtypeStruct((s, 512), F32)],
        compiler_params=_cparams(("parallel",)),
    )(proj, proj, proj, proj, o_mla, *od, *lsed, wp_mla, wp_dil)


def _out_loss(merged, w_out, x, target, gpost):
    s = x.shape[0]
    tm = min(512, s)

    def body(mg_ref, w_ref, x_ref, t_ref, g_ref, do_out, dy_out, loss_out, dg_out):
        i = pl.program_id(0)

        @pl.when(i == 0)
        def _():
            loss_out[...] = jnp.zeros_like(loss_out)
            dg_out[...] = jnp.zeros_like(dg_out)

        o = _dot(mg_ref[...], w_ref[...])
        g = g_ref[...]
        n, u, r = _rms(o, g)
        e = (x_ref[...] + n) - t_ref[...]
        loss_out[...] += 0.5 * jnp.sum(jnp.mean(e * e, axis=-1, keepdims=True))
        dy = e * (1.0 / D_MODEL)
        dy_out[...] = dy
        dg_out[...] += jnp.sum(dy * u, axis=0, keepdims=True)
        do_out[...] = _rms_bwd(dy, g, u, r).astype(BF16)

    row = lambda i: (i, 0)
    cst = lambda i: (0, 0)
    return pl.pallas_call(
        body, name="out_loss", grid=(s // tm,),
        in_specs=[_full((tm, 1024), row), _full((1024, 1024), cst), _full((tm, 1024), row), _full((tm, 1024), row),
                  _full((1, 1024), cst)],
        out_specs=[_full((tm, 1024), row), _full((tm, 1024), row), _full((8, LANES), cst), _full((1, 1024), cst)],
        out_shape=[jax.ShapeDtypeStruct((s, 1024), BF16), jax.ShapeDtypeStruct((s, 1024), F32),
                   jax.ShapeDtypeStruct((8, LANES), F32), jax.ShapeDtypeStruct((1, 1024), F32)],
        compiler_params=_cparams(("arbitrary",)),
    )(merged, w_out, x, target, gpost)


def _seg_sum64(x, ones_bd):
    hi = x.astype(BF16)
    lo = (x - hi.astype(F32)).astype(BF16)
    return _dot(hi, ones_bd) + _dot(lo, ones_bd)


def _merge_bwd(do, w_out, merged, proj, ya, yd, o_mla, o_dil, wp_mla, wp_dil):
    s = do.shape[0]
    tm = min(256, s)
    seg = jnp.arange(512) // 64
    ones_bd = (seg[:, None] == seg[None, :]).astype(BF16)

    def body(do_ref, wo_ref, mg_ref, zm_ref, zd_ref, gm_ref, gd_ref, ya_ref, yd_ref, om_ref, od_ref, wm_ref, wd_ref,
             bd_ref, dzm_out, dzd_out, dgm_out, dgd_out, dom_out, dod_out, domt_out, dd_out, dwo_out, dwm_out,
             dwd_out):
        i = pl.program_id(0)

        @pl.when(i == 0)
        def _():
            dwo_out[...] = jnp.zeros_like(dwo_out)
            dwm_out[...] = jnp.zeros_like(dwm_out)
            dwd_out[...] = jnp.zeros_like(dwd_out)

        dov = do_ref[...]
        dwo_out[...] += _dot_tn(mg_ref[...], dov)
        dm = _dot_nt(dov, wo_ref[...])
        for g_ref, y_ref, z_ref, o_ref, w_ref, dz_out, dg_out, dob_out, dd_o, dw_out in (
                (gm_ref, ya_ref, zm_ref, om_ref, wm_ref, dzm_out, dgm_out, dom_out, None, dwm_out),
                (gd_ref, yd_ref, zd_ref, od_ref, wd_ref, dzd_out, dgd_out, dod_out, dd_out, dwd_out)):
            sg = _sigmoid(g_ref[...].astype(F32))
            dg_out[...] = (dm * y_ref[...].astype(F32) * sg * (1.0 - sg)).astype(BF16)
            dy = (dm * sg).astype(BF16)
            z = z_ref[...].astype(F32)
            sz = _sigmoid(z)
            silu = z * sz
            ob = o_ref[...].astype(F32)
            dw_out[...] += _dot_tn((ob * silu).astype(BF16), dy)
            dp = _dot_nt(dy, w_ref[...])
            dz_out[...] = (dp * ob * (sz * (1.0 + z * (1.0 - sz)))).astype(BF16)
            dob = dp * silu
            dob_out[...] = dob.astype(BF16)
            if dd_o is None:
                domt_out[...] = dob.T.astype(BF16)
            else:
                dd_o[...] = _seg_sum64(dob * ob, bd_ref[...])

    row = lambda i: (i, 0)
    cst = lambda i: (0, 0)
    r512 = _full((tm, 512), row)
    r1024 = _full((tm, 1024), row)
    return pl.pallas_call(
        body, name="merge_bwd", grid=(s // tm,),
        in_specs=[r1024, _full((1024, 1024), cst), r1024,
                  _full((tm, 512), lambda i: (i, Z_MLA // 512)), _full((tm, 512), lambda i: (i, Z_DIL // 512)),
                  _full((tm, 1024), lambda i: (i, G_MLA // 1024)), _full((tm, 1024), lambda i: (i, G_DIL // 1024)),
                  r1024, r1024, r512, r512, _full((512, 1024), cst), _full((512, 1024), cst), _full((512, 512), cst)],
        out_specs=[r512, r512, r1024, r1024, r512, r512, _full((512, tm), lambda i: (0, i)), r512,
                   _full((1024, 1024), cst), _full((512, 1024), cst), _full((512, 1024), cst)],
        out_shape=[jax.ShapeDtypeStruct((s, 512), BF16), jax.ShapeDtypeStruct((s, 512), BF16),
                   jax.ShapeDtypeStruct((s, 1024), BF16), jax.ShapeDtypeStruct((s, 1024), BF16),
                   jax.ShapeDtypeStruct((s, 512), BF16), jax.ShapeDtypeStruct((s, 512), BF16),
                   jax.ShapeDtypeStruct((512, s), BF16), jax.ShapeDtypeStruct((s, 512), F32),
                   jax.ShapeDtypeStruct((1024, 1024), F32), jax.ShapeDtypeStruct((512, 1024), F32),
                   jax.ShapeDtypeStruct((512, 1024), F32)],
        compiler_params=_cparams(("arbitrary",)),
    )(do, w_out, merged, proj, proj, proj, proj, ya, yd, o_mla, o_dil, wp_mla, wp_dil, ones_bd)


def _mla_bwd(q, qt, k, kt, v, do, dot, ot, lse):
    s = q.shape[0]
    tq = tk = min(256, s)
    nq = s // tq

    def body(q_ref, qt_ref, k_ref, kt_ref, v_ref, do_ref, dot_ref, ot_ref, lse_ref, dqt_out, dkt_out, dvt_out,
             dqt_acc):
        left = _left_mask()
        causal = (lax.broadcasted_iota(jnp.int32, (tk, tq), 0) <= lax.broadcasted_iota(jnp.int32, (tk, tq), 1))
        dqt_acc[...] = jnp.zeros_like(dqt_acc)

        def kv_step(j, _):
            c0 = pl.multiple_of(j * tk, tk)
            vv = v_ref[pl.ds(c0, tk), :]
            khs = [k_ref[pl.ds(c0, tk), hh * 128:(hh + 1) * 128] for hh in range(2)]
            kths = [kt_ref[hh * 128:(hh + 1) * 128, pl.ds(c0, tk)] for hh in range(2)]
            vms = [jnp.where(left if hh == 0 else jnp.logical_not(left), vv, jnp.zeros_like(vv)) for hh in range(2)]

            def scores(i):
                r0 = pl.multiple_of(jnp.minimum(i, nq - 1) * tq, tq)
                dov = do_ref[pl.ds(r0, tq), :]
                return tuple((_dot_nt(khs[hh], q_ref[pl.ds(r0, tq), hh * 128:(hh + 1) * 128]) * MLA_SCALE,
                              _dot_nt(vms[hh], dov)) for hh in range(2))

            def update(i, tiles, acc, masked):
                r0 = pl.multiple_of(i * tq, tq)
                new = []
                for hh in range(2):
                    dkt, dvt = acc[hh]
                    st, dp = tiles[hh]
                    hrows = slice(hh * 128, (hh + 1) * 128)
                    drows = slice(hh * 64, (hh + 1) * 64)
                    doth = dot_ref[drows, pl.ds(r0, tq)]
                    dd = jnp.sum(doth.astype(F32) * ot_ref[drows, pl.ds(r0, tq)].astype(F32), axis=0, keepdims=True)
                    p = jnp.exp(st - lse_ref[hh:hh + 1, pl.ds(r0, tq)])
                    if masked:
                        p = jnp.where(causal, p, 0.0)
                    ds = (p * (dp - dd) * MLA_SCALE).astype(BF16)
                    dvt = dvt + _dot_nt(doth, p.astype(BF16))
                    dkt = dkt + _dot_nt(qt_ref[hrows, pl.ds(r0, tq)], ds)
                    dqt_acc[hrows, pl.ds(r0, tq)] += _dot(kths[hh], ds)
                    new.append((dkt, dvt))
                return tuple(new)

            init = tuple((jnp.zeros((128, tk), F32), jnp.zeros((64, tk), F32)) for _ in range(2))
            acc = update(j, scores(j), init, True)
            acc = lax.fori_loop(j + 1, nq, lambda i, a: update(i, scores(i), a, False), acc)
            for hh in range(2):
                dkt_out[hh * 128:(hh + 1) * 128, pl.ds(c0, tk)] = acc[hh][0].astype(BF16)
                dvt_out[hh * 64:(hh + 1) * 64, pl.ds(c0, tk)] = acc[hh][1].astype(BF16)
            return 0

        lax.fori_loop(0, nq, kv_step, 0)
        dqt_out[...] = dqt_acc[...].astype(BF16)

    b256 = _full((s, 256), lambda p: (0, p))
    b128 = _full((s, 128), lambda p: (0, p))
    t256 = _full((256, s), lambda p: (p, 0))
    t128 = _full((128, s), lambda p: (p, 0))
    return pl.pallas_call(
        body, name="mla_bwd", grid=(4,),
        in_specs=[b256, t256, b256, t256, b128, b128, t128, t128, _full((8, s), lambda p: (p, 0))],
        out_specs=[t256, t256, t128],
        out_shape=[jax.ShapeDtypeStruct((1024, s), BF16), jax.ShapeDtypeStruct((1024, s), BF16),
                   jax.ShapeDtypeStruct((512, s), BF16)],
        scratch_shapes=[pltpu.VMEM((256, s), F32)],
        compiler_params=_cparams(("parallel",)),
    )(q, qt, k, kt, v, do, dot, ot, lse)


def _mla_prep_bwd(dq, dk, dv, proj, gq, gkv, w_uq, w_uk, w_uv, q_tab, k_tab):
    s = proj.shape[0]
    tm = min(256, s)

    def body(dqt_ref, dkt_ref, dvt_ref, cq_ref, ckv_ref, gq_ref, gkv_ref, wq_ref, wk_ref, wv_ref,
             qc, qsp, qsm, kc, ksp, ksm,
             dcq_out, dkr_out, dckv_out, dwq_out, dwk_out, dwv_out, dgq_out, dgkv_out):
        i = pl.program_id(0)

        @pl.when(i == 0)
        def _():
            for r in (dwq_out, dwk_out, dwv_out, dgq_out, dgkv_out):
                r[...] = jnp.zeros_like(r)

        dqu = _unrope(dqt_ref[...].astype(F32).T, _tile_lanes(qc[...], 1024), _tile_lanes(qsp[...], 1024),
                      _tile_lanes(qsm[...], 1024), 16).astype(BF16)
        gq = gq_ref[...]
        cqn, xh, r = _rms(cq_ref[...].astype(F32), gq)
        dwq_out[...] += _dot_tn(cqn.astype(BF16), dqu)
        dcqn = _dot_nt(dqu, wq_ref[...])
        dgq_out[...] += jnp.sum(dcqn * xh, axis=0, keepdims=True)
        dcq_out[...] = _rms_bwd(dcqn, gq, xh, r).astype(BF16)

        dkf = dkt_ref[...].astype(F32).T
        dkb = dkf.astype(BF16)
        dsum = dkf[:, 0:128]
        for h in range(1, MLA_HEADS):
            dsum = dsum + dkf[:, h * 128:(h + 1) * 128]
        dkr_out[...] = _unrope(pltpu.roll(dsum, 64, 1), kc[...], ksp[...], ksm[...], 16).astype(BF16)

        dvb = dvt_ref[...].astype(F32).T.astype(BF16)
        gkv = gkv_ref[...]
        ckvn, xh2, r2 = _rms(ckv_ref[...].astype(F32), gkv)
        ckvn = ckvn.astype(BF16)
        dwk_out[...] += _dot_tn(ckvn, dkb)
        dwv_out[...] += _dot_tn(ckvn, dvb)
        dckvn = _dot_nt(dkb, wk_ref[...]) + _dot_nt(dvb, wv_ref[...])
        dgkv_out[...] += jnp.sum(dckvn * xh2, axis=0, keepdims=True)
        dckv_out[...] = _rms_bwd(dckvn, gkv, xh2, r2).astype(BF16)

    row = lambda i: (i, 0)
    cst = lambda i: (0, 0)
    tabs = [_full((tm, LANES), row)] * 6
    return pl.pallas_call(
        body, name="mla_prep_bwd", grid=(s // tm,),
        in_specs=[_full((1024, tm), lambda i: (0, i)), _full((1024, tm), lambda i: (0, i)),
                  _full((512, tm), lambda i: (0, i)),
                  _full((tm, 384), lambda i: (i, CQ_OFF // 384)), _full((tm, 256), lambda i: (i, CKV_OFF // 256)),
                  _full((1, 384), cst), _full((1, 256), cst),
                  _full((384, 1024), cst), _full((256, 1024), cst), _full((256, 512), cst)] + tabs,
        out_specs=[_full((tm, 384), row), _full((tm, 128), row), _full((tm, 256), row),
                   _full((384, 1024), cst), _full((256, 1024), cst), _full((256, 512), cst),
                   _full((1, 384), cst), _full((1, 256), cst)],
        out_shape=[jax.ShapeDtypeStruct((s, 384), BF16), jax.ShapeDtypeStruct((s, 128), BF16),
                   jax.ShapeDtypeStruct((s, 256), BF16),
                   jax.ShapeDtypeStruct((384, 1024), F32), jax.ShapeDtypeStruct((256, 1024), F32),
                   jax.ShapeDtypeStruct((256, 512), F32),
                   jax.ShapeDtypeStruct((1, 384), F32), jax.ShapeDtypeStruct((1, 256), F32)],
        compiler_params=_cparams(("arbitrary",)),
    )(dq, dk, dv, proj, proj, gq, gkv, w_uq, w_uk, w_uv, *q_tab, *k_tab)


def _dil_bwd(qa, ka, va, qo, ko, vo, do, lse, dd, tabs, d, name):
    s = qa.shape[0]
    nblk = s // Q_BLOCK
    per_seq = nblk // d

    def body(q_ref, k_ref, v_ref, do_ref, lse_ref, dd_ref, c_ref, sp_ref, sm_ref, dq_out, dk_out, dv_out,
             dk_acc, dv_acc):
        left = _left_mask()
        dk_acc[...] = jnp.zeros_like(dk_acc)
        dv_acc[...] = jnp.zeros_like(dv_acc)

        hms = (left, jnp.logical_not(left))

        def rows_of(b):
            r0 = pl.multiple_of(b * Q_BLOCK, Q_BLOCK)
            p0 = pl.multiple_of(jnp.maximum(b - 1, 0) * Q_BLOCK, Q_BLOCK)
            return pl.ds(r0, Q_BLOCK), pl.ds(p0, Q_BLOCK)

        def scores(b):
            rows, prow = rows_of(b)
            qb, dob = q_ref[rows, :], do_ref[rows, :]
            kp, kc, vp, vc = k_ref[prow, :], k_ref[rows, :], v_ref[prow, :], v_ref[rows, :]
            zero = jnp.zeros_like(qb)
            out = []
            for hh in range(2):
                qm, dom = jnp.where(hms[hh], qb, zero), jnp.where(hms[hh], dob, zero)
                out.append((_dot_nt(qm, kp) * DIL_SCALE, _dot_nt(qm, kc) * DIL_SCALE,
                            _dot_nt(dom, vp), _dot_nt(dom, vc)))
            return out

        def finish(b, tiles):
            rows, prow = rows_of(b)
            pmask, cur_ok = _band_masks((b % per_seq) > 0)
            qb, dob = q_ref[rows, :], do_ref[rows, :]
            kp, kc = k_ref[prow, :], k_ref[rows, :]
            lse_b, dd_b = lse_ref[rows, :], dd_ref[rows, :]
            zero = jnp.zeros_like(qb)
            dq = jnp.zeros((Q_BLOCK, LANES), F32)
            dkp = jnp.zeros((Q_BLOCK, LANES), F32)
            dkc = jnp.zeros((Q_BLOCK, LANES), F32)
            dvp = jnp.zeros((Q_BLOCK, LANES), F32)
            dvc = jnp.zeros((Q_BLOCK, LANES), F32)
            for hh in range(2):
                hm = hms[hh]
                qm, dom = jnp.where(hm, qb, zero), jnp.where(hm, dob, zero)
                lse_h, dd_h = _expand_half(lse_b, hh, left), _expand_half(dd_b, hh, left)
                sp, sc, dpp, dpc = tiles[hh]
                pp = jnp.where(pmask, jnp.exp(sp - lse_h), 0.0)
                pc = jnp.where(cur_ok, jnp.exp(sc - lse_h), 0.0)
                dsp = (pp * (dpp - dd_h) * DIL_SCALE).astype(BF16)
                dsc = (pc * (dpc - dd_h) * DIL_SCALE).astype(BF16)
                dq = dq + _dot(dsp, jnp.where(hm, kp, zero)) + _dot(dsc, jnp.where(hm, kc, zero))
                dkp, dkc = dkp + _dot_tn(dsp, qm), dkc + _dot_tn(dsc, qm)
                dvp, dvc = dvp + _dot_tn(pp.astype(BF16), dom), dvc + _dot_tn(pc.astype(BF16), dom)
            dq_out[rows, :] = _unrope(dq, c_ref[rows, :], sp_ref[rows, :], sm_ref[rows, :], 8).astype(BF16)
            dk_acc[prow, :] += dkp
            dv_acc[prow, :] += dvp
            dk_acc[rows, :] += dkc
            dv_acc[rows, :] += dvc

        def step(t, _):
            ta, tb = scores(2 * t), scores(2 * t + 1)
            finish(2 * t, ta)
            finish(2 * t + 1, tb)
            return 0

        lax.fori_loop(0, nblk // 2, step, 0)
        dk_out[...] = _unrope(dk_acc[...], c_ref[...], sp_ref[...], sm_ref[...], 8).astype(BF16)
        dv_out[...] = dv_acc[...].astype(BF16)

    blk = lambda off: _full((s, 128), lambda p, off=off: (0, off + p))
    tab = _full((s, 128), lambda p: (0, 0))
    return pl.pallas_call(
        body, name=name, grid=(4,),
        in_specs=[blk(qo), blk(ko), blk(vo), blk(0), blk(0), blk(0), tab, tab, tab],
        out_specs=[blk(0), blk(0), blk(0)],
        out_shape=[jax.ShapeDtypeStruct((s, 512), BF16)] * 3,
        scratch_shapes=[pltpu.VMEM((s, 128), F32), pltpu.VMEM((s, 128), F32)],
        compiler_params=_cparams(("parallel",)),
    )(qa, ka, va, do, lse, dd, *tabs)


def _dh_bwd(dproj, w_pad, x, gpre, dy):
    s = x.shape[0]
    tm, tk = min(512, s), 1408
    nk = IN_PAD // tk

    def body(dp_ref, w_ref, x_ref, g_ref, dy_ref, gx_out, dg_out, acc):
        i, kk = pl.program_id(0), pl.program_id(1)

        @pl.when(jnp.logical_and(i == 0, kk == 0))
        def _():
            dg_out[...] = jnp.zeros_like(dg_out)

        @pl.when(kk == 0)
        def _():
            acc[...] = jnp.zeros_like(acc)

        acc[...] += _dot_nt(dp_ref[...], w_ref[...])

        @pl.when(kk == nk - 1)
        def _():
            g = g_ref[...]
            _, xh, r = _rms(x_ref[...], g)
            dh = acc[...]
            dg_out[...] += jnp.sum(dh * xh, axis=0, keepdims=True)
            gx_out[...] = dy_ref[...] + _rms_bwd(dh, g, xh, r)

    row = lambda i, k: (i, 0)
    return pl.pallas_call(
        body, name="dh_bwd", grid=(s // tm, nk),
        in_specs=[_full((tm, tk), lambda i, k: (i, k)), _full((1024, tk), lambda i, k: (0, k)),
                  _full((tm, 1024), row), _full((1, 1024), lambda i, k: (0, 0)), _full((tm, 1024), row)],
        out_specs=[_full((tm, 1024), row), _full((1, 1024), lambda i, k: (0, 0))],
        out_shape=[jax.ShapeDtypeStruct((s, 1024), F32), jax.ShapeDtypeStruct((1, 1024), F32)],
        scratch_shapes=[pltpu.VMEM((tm, 1024), F32)],
        compiler_params=_cparams(("arbitrary", "arbitrary")),
    )(dproj, w_pad, x, gpre, dy)


def _dw_in(h, dproj):
    s = h.shape[0]
    ts, tn = min(1024, s), 768
    ns = s // ts

    def body(h_ref, dp_ref, o_ref, acc):
        k = pl.program_id(1)

        @pl.when(k == 0)
        def _():
            acc[...] = jnp.zeros_like(acc)

        acc[...] += _dot_tn(h_ref[...], dp_ref[...])

        @pl.when(k == ns - 1)
        def _():
            o_ref[...] = acc[...].astype(BF16)

    return pl.pallas_call(
        body, name="dw_in", grid=(IN_PAD // tn, ns),
        in_specs=[_full((ts, 1024), lambda j, k: (k, 0)), _full((ts, tn), lambda j, k: (k, j))],
        out_specs=_full((1024, tn), lambda j, k: (0, j)),
        out_shape=jax.ShapeDtypeStruct((1024, IN_PAD), BF16),
        scratch_shapes=[pltpu.VMEM((1024, tn), F32)],
        compiler_params=_cparams(("parallel", "arbitrary")),
    )(h, dproj)


def _remote(src, dst, sems, row, k, to):
    return pltpu.make_async_remote_copy(src_ref=src, dst_ref=dst, send_sem=sems.at[row, k], recv_sem=sems.at[row + 1, k],
                                        device_id=to, device_id_type=pl.DeviceIdType.MESH)


def _place():
    x, y, c = lax.axis_index("x"), lax.axis_index("y"), lax.axis_index("c")
    return x, y, c, [(1 - x, y), (x, 1 - y), (1 - x, 1 - y)]


def _gather_weights(arrays):
    n = len(arrays)

    def body(*refs):
        ins, outs, sems = refs[:n], refs[n:2 * n], refs[2 * n]
        x, y, c, chips = _place()
        me, sib = (x, y, c), (x, y, 1 - c)
        idx = lambda p: 4 * p[0] + 2 * p[1] + p[2]

        def copy(a, k, block, to, from_input=False):
            src = ins[a] if from_input else outs[a].at[idx(block)]
            return _remote(src, outs[a].at[idx(block)], sems, 2 * a, k, to)

        own = [pltpu.make_async_copy(ins[a], outs[a].at[idx(me)], sems.at[2 * a, 7]) for a in range(n)]
        first = [copy(a, 0, me, sib, True) for a in range(n)]
        first += [copy(a, 1 + j, me, (*chip, c), True) for j, chip in enumerate(chips) for a in range(n)]
        for cp in own + first:
            cp.start()
        passed = []
        for j, chip in enumerate(chips):
            for a in range(n):
                copy(a, 1 + j, (*chip, c), me).wait_recv()
                passed.append(copy(a, 4 + j, (*chip, c), sib))
                passed[-1].start()
        for a in range(n):
            copy(a, 0, sib, me).wait_recv()
        for j, chip in enumerate(chips):
            for a in range(n):
                copy(a, 4 + j, (*chip, 1 - c), me).wait_recv()
        for cp in first + passed:
            cp.wait_send()
        for cp in own:
            cp.wait()

    hbm = pl.BlockSpec(memory_space=pl.ANY)
    return pl.pallas_call(
        body, name="gather_weights", in_specs=[hbm] * n, out_specs=[hbm] * n,
        out_shape=[jax.ShapeDtypeStruct((N_DEV,) + a.shape, a.dtype) for a in arrays],
        scratch_shapes=[pltpu.SemaphoreType.DMA((2 * n, N_DEV))],
    )(*arrays)


def _pair_exchange(chunks):
    n = len(chunks)

    def body(*refs):
        ins, outs, sems = refs[:n], refs[n:2 * n], refs[2 * n]
        x, y, c, _ = _place()
        sent = [_remote(ins[a].at[2 * q + (1 - c)], outs[a].at[q], sems, 2 * a, q, (x, y, 1 - c))
                for a in range(n) for q in range(4)]
        for cp in sent:
            cp.start()
        for cp in sent:
            cp.wait_recv()
        for cp in sent:
            cp.wait_send()

    hbm = pl.BlockSpec(memory_space=pl.ANY)
    return pl.pallas_call(
        body, name="pair_exchange", in_specs=[hbm] * n, out_specs=[hbm] * n,
        out_shape=[jax.ShapeDtypeStruct((4,) + a.shape[1:], a.dtype) for a in chunks],
        scratch_shapes=[pltpu.SemaphoreType.DMA((2 * n, 4))],
    )(*chunks)


def _pair_sum(core, chunks, recv, name, tr):
    _, rows, cols = chunks.shape

    def body(c_ref, a_ref, b_ref, o_ref):
        o_ref[...] = (a_ref[...].astype(F32) + b_ref[...].astype(F32)).astype(BF16)

    blk = lambda f: _full((1, tr, cols), f)
    return pl.pallas_call(
        body, name=name, out_shape=jax.ShapeDtypeStruct((4, rows, cols), BF16),
        grid_spec=pltpu.PrefetchScalarGridSpec(
            num_scalar_prefetch=1, grid=(4, rows // tr),
            in_specs=[blk(lambda q, i, c: (2 * q + c[0], i, 0)), blk(lambda q, i, c: (q, i, 0))],
            out_specs=blk(lambda q, i, c: (q, i, 0))),
        compiler_params=_cparams(("parallel", "parallel")),
    )(core, chunks, recv)


def _chip_exchange(pairs, packet):
    n = len(pairs)

    def body(*refs):
        ins, pk, outs, pk_out, sems = refs[:n], refs[n], refs[n + 1:2 * n + 1], refs[2 * n + 1], refs[2 * n + 2]
        x, y, c, chips = _place()
        myq, me = 2 * x + y, 4 * x + 2 * y + c
        flip = lambda v, b: (1 - v) if b else v

        def chunk(a, j, outgoing):
            q = 2 * chips[j][0] + chips[j][1]
            return _remote(ins[a].at[q], outs[a].at[myq if outgoing else q], sems, 2 * a, j, (*chips[j], c))

        def small(j, outgoing):
            peer = (flip(x, (j >> 2) & 1), flip(y, (j >> 1) & 1), flip(c, j & 1))
            slot = me if outgoing else 4 * peer[0] + 2 * peer[1] + peer[2]
            return _remote(pk, pk_out.at[slot], sems, 2 * n, j, peer)

        own = [pltpu.make_async_copy(ins[a].at[myq], outs[a].at[myq], sems.at[2 * a, 3]) for a in range(n)]
        own.append(pltpu.make_async_copy(pk, pk_out.at[me], sems.at[2 * n, 0]))
        sent = [chunk(a, j, True) for j in range(3) for a in range(n)] + [small(j, True) for j in range(1, N_DEV)]
        for cp in own + sent:
            cp.start()
        for cp in [chunk(a, j, False) for j in range(3) for a in range(n)] + [small(j, False) for j in range(1, N_DEV)]:
            cp.wait_recv()
        for cp in sent:
            cp.wait_send()
        for cp in own:
            cp.wait()

    hbm = pl.BlockSpec(memory_space=pl.ANY)
    return pl.pallas_call(
        body, name="chip_exchange", in_specs=[hbm] * (n + 1), out_specs=[hbm] * (n + 1),
        out_shape=[jax.ShapeDtypeStruct(a.shape, a.dtype) for a in pairs]
        + [jax.ShapeDtypeStruct((N_DEV,) + packet.shape, packet.dtype)],
        scratch_shapes=[pltpu.SemaphoreType.DMA((2 * n + 2, N_DEV))],
    )(*pairs, packet)


def _adam_math(w, g, m, v):
    m = ADAM_B1 * m + (1.0 - ADAM_B1) * g
    v = ADAM_B2 * v + (1.0 - ADAM_B2) * (g * g)
    m_hat = m / (1.0 - ADAM_B1 ** ADAM_STEP)
    v_hat = v / (1.0 - ADAM_B2 ** ADAM_STEP)
    delta = -ADAM_LR * (m_hat / (jnp.sqrt(v_hat) + ADAM_EPS) + ADAM_WD * w)
    return delta, m, v


def _adam(recv, w, m, v, name, tr):
    _, rows, cols = w.shape

    def body(r_ref, w_ref, m_ref, v_ref, g_out, d_out, m_out, v_out):
        g = r_ref[0].astype(F32)
        for k in range(1, 4):
            g = g + r_ref[k].astype(F32)
        g_out[0] = g
        d_out[0], m_out[0], v_out[0] = _adam_math(w_ref[0], g, m_ref[0], v_ref[0])

    blk = _full((1, tr, cols), lambda i: (0, i, 0))
    return pl.pallas_call(
        body, name=name, grid=(rows // tr,),
        in_specs=[_full((4, tr, cols), lambda i: (0, i, 0)), blk, blk, blk],
        out_specs=[blk] * 4,
        out_shape=[jax.ShapeDtypeStruct(w.shape, F32)] * 4,
        compiler_params=_cparams(("parallel",)),
    )(recv, w, m, v)


def _adam_gains(recv, gains, gains_m, gains_v):
    def body(*refs):
        r_ref, w, m, v = refs[0], refs[1:5], refs[5:9], refs[9:13]
        g_out, d_out, m_out, v_out, loss_out = refs[13:17], refs[17:21], refs[21:25], refs[25:29], refs[29]
        tot = r_ref[0:1, :]
        for k in range(1, N_DEV):
            tot = tot + r_ref[k:k + 1, :]
        for t in range(4):
            g = tot[:, GAIN_OFFS[t]:GAIN_OFFS[t] + GAIN_WIDTHS[t]]
            g_out[t][...] = g
            d_out[t][...], m_out[t][...], v_out[t][...] = _adam_math(w[t][...], g, m[t][...], v[t][...])
        loss_out[...] = tot[:, LOSS_OFF:LOSS_OFF + LANES]

    shapes = [jax.ShapeDtypeStruct((1, n), F32) for n in GAIN_WIDTHS]
    return pl.pallas_call(
        body, name="adam_gains", out_shape=shapes * 4 + [jax.ShapeDtypeStruct((1, LANES), F32)],
    )(recv, *gains, *gains_m, *gains_v)


def _local_step(x, positions, gains, weights, target):
    gpre, gq, gkv, gpost = gains
    w_pad, w_uq, w_uk, w_uv, wp_mla, wp_dil, w_out = weights
    q_tab, k_tab, d_tab = _rope_tables(positions)

    proj, h = _inproj(x, gpre, w_pad, d_tab)
    q, k, v, qt, kt, vt = _mla_prep(proj, gq, gkv, w_uq, w_uk, w_uv, q_tab, k_tab)
    o_mla, ot_mla, lse_mla = _mla_fwd(q, k, vt)

    qkv_perm, od, lsed = [], [], []
    for g, d in enumerate(DIL_DILATIONS):
        if d == 1:
            arr, offs = proj, ((Q_OFF + 512 * g) // 128, (K_OFF + 512 * g) // 128, (V_OFF + 512 * g) // 128)
        else:
            cols = jnp.concatenate([proj[:, o + 512 * g:o + 512 * (g + 1)] for o in (Q_OFF, K_OFF, V_OFF)], axis=1)
            arr, offs = _perm(cols, d), (0, 4, 8)
        qkv_perm.append((arr, offs))
        o_g, lse_g = _dil_fwd(arr, arr, arr, *offs, d, "dil_fwd_%d" % g)
        od.append(_unperm(o_g, d))
        lsed.append(_unperm(lse_g, d))

    merged, ya, yd, o_dil, lse_dil = _merge_fwd(proj, o_mla, od, lsed, wp_mla, wp_dil)
    do, dy, loss, dgpost = _out_loss(merged, w_out, x, target, gpost)

    (dz_mla, dz_dil, dg_mla, dg_dil, do_mla, do_dil, dot_mla, dd_dil, dw_out, dwp_mla, dwp_dil) = _merge_bwd(
        do, w_out, merged, proj, ya, yd, o_mla, o_dil, wp_mla, wp_dil)

    dq, dk, dv = _mla_bwd(q, qt, k, kt, v, do_mla, dot_mla, ot_mla, lse_mla)
    dcq, dkr, dckv, dw_uq, dw_uk, dw_uv, dgq, dgkv = _mla_prep_bwd(dq, dk, dv, proj, gq, gkv, w_uq, w_uk, w_uv,
                                                                   q_tab, k_tab)

    dqs, dks, dvs = [], [], []
    for g, d in enumerate(DIL_DILATIONS):
        arr, offs = qkv_perm[g]
        tabs = tuple(_perm(t, d) for t in d_tab)
        dq_g, dk_g, dv_g = _dil_bwd(arr, arr, arr, *offs, _perm(do_dil, d), _perm(lse_dil, d), _perm(dd_dil, d),
                                    tabs, d, "dil_bwd_%d" % g)
        dqs.append(_unperm(dq_g, d))
        dks.append(_unperm(dk_g, d))
        dvs.append(_unperm(dv_g, d))

    dproj = jnp.concatenate([dz_mla, dz_dil, dg_mla, dg_dil] + dqs + dks + dvs + [dcq, dkr, dckv], axis=1)
    grad_x, dgpre = _dh_bwd(dproj, w_pad, x, gpre, dy)
    dw_in = _dw_in(h, dproj)
    return loss, grad_x, (dgpre, dgq, dgkv, dgpost), (dw_in, dw_uq, dw_uk, dw_uv, dwp_mla, dwp_dil, dw_out)


ADAM_ROWS = (256, 384, 256, 512, 512, 128)
PAIR_ROWS = (512, 384, 256, 512, 512, 128)


def kernel(x, positions, pre_norm_g, w_in, q_norm_g, w_uq, kv_norm_g, w_ukv, w_proj_mla, w_proj_dil, w_out, post_norm_g, loss_target, m_pre_norm_g, m_w_in, m_q_norm_g, m_w_uq, m_kv_norm_g, m_w_ukv, m_w_proj_mla, m_w_proj_dil, m_w_out, m_post_norm_g, v_pre_norm_g, v_w_in, v_q_norm_g, v_w_uq, v_kv_norm_g, v_w_ukv, v_w_proj_mla, v_w_proj_dil, v_w_out, v_post_norm_g):
    big_w = (w_in, w_uq, w_ukv, w_proj_mla, w_proj_dil, w_out)
    big_m = (m_w_in, m_w_uq, m_w_ukv, m_w_proj_mla, m_w_proj_dil, m_w_out)
    big_v = (v_w_in, v_w_uq, v_w_ukv, v_w_proj_mla, v_w_proj_dil, v_w_out)
    gains = (pre_norm_g, q_norm_g, kv_norm_g, post_norm_g)
    gains_m = (m_pre_norm_g, m_q_norm_g, m_kv_norm_g, m_post_norm_g)
    gains_v = (v_pre_norm_g, v_q_norm_g, v_kv_norm_g, v_post_norm_g)

    gathered = _gather_weights([w[0].astype(BF16) for w in big_w])
    weights = _assemble_weights(*gathered)

    loss, grad_x, dgains, dweights = _local_step(x[0], positions[0], gains, weights, loss_target[0])

    packet = jnp.concatenate(list(dgains) + [loss[0:1]], axis=1)
    chunks = _grad_chunks(*dweights)
    from_sibling = _pair_exchange(chunks)
    core = lax.axis_index("c").astype(jnp.int32).reshape(1)
    pairs = [_pair_sum(core, chunks[t], from_sibling[t], "pair_sum_%d" % t, PAIR_ROWS[t]) for t in range(6)]
    received = _chip_exchange(pairs, packet)

    big = [_adam(received[t], big_w[t], big_m[t], big_v[t], "adam_%d" % t, ADAM_ROWS[t]) for t in range(6)]
    small = _adam_gains(received[6].reshape(N_DEV, PACKET), gains, gains_m, gains_v)

    def interleave(kind):
        s_pre, s_q, s_kv, s_post = small[4 * kind:4 * kind + 4]
        b_in, b_uq, b_ukv, b_pm, b_pd, b_out = (big[t][kind] for t in range(6))
        return [s_pre, b_in, s_q, b_uq, s_kv, b_ukv, b_pm, b_pd, b_out, s_post]

    return (small[16][0, 0], grad_x[None], *interleave(0), *interleave(1), *interleave(2), *interleave(3))
```

```python
import numpy as np
import jax
import jax.numpy as jnp
from jax import lax
from jax.experimental import pallas as pl
from jax.experimental.pallas import tpu as pltpu

F32 = jnp.float32
BF16 = jnp.bfloat16

D_MODEL = 1024
NORM_EPS = 1e-6
ROPE_THETA = 500000.0
N_DEV = 8
LANES = 128
NEG = -1e30

MLA_HEADS = 8
MLA_Q_RANK = 384
MLA_KV_RANK = 256
MLA_SCALE = 96.0 ** -0.5
LOG2E = 1.4426950408889634
MLA_QSCALE = MLA_SCALE * LOG2E
DIL_DILATIONS = (1, 4, 16)
DIL_SCALE = 0.125
Q_BLOCK = 128

Z_MLA, Z_DIL, G_MLA, G_DIL = 0, 512, 1024, 2048
Q_OFF, K_OFF, V_OFF = 3072, 4608, 6144
CQ_OFF, KR_OFF, CKV_OFF, IN_PAD = 7680, 8064, 8192, 8448
IN_WIDTH = 8352
SHARD_W = IN_WIDTH // 8
IN_SEGS = ((0, 384, CQ_OFF), (384, 256, CKV_OFF), (640, 32, KR_OFF), (672, 1536, Q_OFF), (2208, 1536, K_OFF),
           (3744, 1536, V_OFF), (5280, 512, Z_MLA), (5792, 512, Z_DIL), (6304, 1024, G_MLA), (7328, 1024, G_DIL))

GAIN_OFFS = (0, 1024, 1408, 1664)
GAIN_WIDTHS = (1024, 384, 256, 1024)
LOSS_OFF, PACKET = 2688, 2816

ADAM_LR, ADAM_B1, ADAM_B2, ADAM_EPS, ADAM_WD, ADAM_STEP = 0.001, 0.9, 0.999, 1e-08, 0.01, 10

VMEM_LIMIT_MB = 56


def _cparams(sem=None, vmem_mb=VMEM_LIMIT_MB):
    return pltpu.CompilerParams(dimension_semantics=sem, vmem_limit_bytes=vmem_mb * 1024 * 1024)


def _dot(a, b):
    return jnp.dot(a, b, preferred_element_type=F32)


def _dot_nt(a, b):
    return lax.dot_general(a, b, (((1,), (1,)), ((), ())), preferred_element_type=F32)


def _dot_tn(a, b):
    return lax.dot_general(a, b, (((0,), (0,)), ((), ())), preferred_element_type=F32)


def _tile_lanes(t, width):
    return t if width == t.shape[1] else jnp.tile(t, (1, width // t.shape[1]))


def _rope(x, c, sp, sm, a):
    n = x.shape[1]
    return x * c + pltpu.roll(x, a, 1) * sp + pltpu.roll(x, n - a, 1) * sm


def _unrope(dy, c, sp, sm, a):
    n = dy.shape[1]
    return dy * c + pltpu.roll(dy * sp, n - a, 1) + pltpu.roll(dy * sm, a, 1)


def _sigmoid(z):
    return 1.0 / (1.0 + jnp.exp(-z))


def _left_mask():
    return lax.broadcasted_iota(jnp.int32, (1, LANES), 1) < 64


def _expand_half(x, hh, left):
    r = pltpu.roll(x, 64, 1)
    return jnp.where(left, x, r) if hh == 0 else jnp.where(left, r, x)


def _rms(xv, g):
    r = lax.rsqrt(jnp.mean(xv * xv, axis=-1, keepdims=True) + NORM_EPS)
    xh = xv * r
    return xh * g, xh, r


def _rms_bwd(dout, g, xh, r):
    dxh = dout * g
    return r * (dxh - xh * jnp.mean(dxh * xh, axis=-1, keepdims=True))


def _full(shape, index_map):
    return pl.BlockSpec(shape, index_map)


def _w_in_pieces():
    out = []
    for s, n, off in sorted(IN_SEGS, key=lambda t: t[2]):
        c = s
        while c < s + n:
            k = c // SHARD_W
            e = min(s + n, (k + 1) * SHARD_W)
            out.append((k, c - k * SHARD_W, e - c, off + (c - s)))
            c = e
    return out


def _assemble_w_in(g):
    parts, cur = [], 0
    for k, a, w, off in _w_in_pieces():
        if off > cur:
            parts.append(jnp.zeros((D_MODEL, off - cur), g.dtype))
        parts.append(g[k, :, a:a + w])
        cur = off + w
    if cur < IN_PAD:
        parts.append(jnp.zeros((D_MODEL, IN_PAD - cur), g.dtype))
    return jnp.concatenate(parts, axis=1)


def _dw_in_chunks(dw):
    chunks = []
    for dev in range(N_DEV):
        mine = sorted((p for p in _w_in_pieces() if p[0] == dev), key=lambda p: p[1])
        chunks.append(jnp.concatenate([dw[:, off:off + w] for k, a, w, off in mine], axis=1))
    return jnp.stack(chunks)


def _assemble_weights(g_in, g_uq, g_ukv, g_pm, g_pd, g_out):
    w_uq_pad = jnp.pad(g_uq.transpose(1, 0, 2), ((0, 0), (0, 0), (0, 32))).reshape(384, 1024)
    ukv = g_ukv.transpose(1, 0, 2)
    w_uk_pad = jnp.pad(ukv[:, :, :64], ((0, 0), (0, 0), (0, 64))).reshape(256, 1024)
    w_uv = ukv[:, :, 64:].reshape(256, 512)
    wp_mla = g_pm.transpose(1, 0, 2).reshape(512, 1024)
    wp_dil = g_pd.transpose(1, 0, 2).reshape(512, 1024)
    return _assemble_w_in(g_in), w_uq_pad, w_uk_pad, w_uv, wp_mla, wp_dil, g_out.reshape(1024, 1024)


def _grad_chunks(dw_in_pad, dw_uq_pad, dw_uk_pad, dw_uv, dwp_mla, dwp_dil, dw_out):
    a = _dw_in_chunks(dw_in_pad)
    b = dw_uq_pad.reshape(384, 8, 128)[:, :, :96].transpose(1, 0, 2)
    c = jnp.concatenate([dw_uk_pad.reshape(256, 8, 128)[:, :, :64], dw_uv.reshape(256, 8, 64)], axis=2)
    c = c.transpose(1, 0, 2)
    d = dwp_mla.reshape(512, N_DEV, 128).transpose(1, 0, 2)
    e = dwp_dil.reshape(512, N_DEV, 128).transpose(1, 0, 2)
    f = dw_out.reshape(N_DEV, 128, 1024)
    return [t.astype(BF16) for t in (a, b, c, d, e, f)]


def _lane_consts(freqs, half, first, period):
    rel = (np.arange(LANES) % period) - first
    rot = (rel >= 0) & (rel < 2 * half)
    freq = np.where(rot, freqs[np.clip(rel, 0, 2 * half - 1) % half], 0.0).astype(np.float32)
    x1 = (rot & (rel < half)).astype(np.float32)
    x2 = (rot & (rel >= half)).astype(np.float32)
    return freq[None, :], x1[None, :], x2[None, :]


def _rope_tables(pos):
    p = pos.astype(F32)[:, None]
    inv_m = np.float32(ROPE_THETA) ** (-(np.arange(0, 32, 2, dtype=np.float32) / np.float32(32)))
    inv_d = np.float32(ROPE_THETA) ** (-(np.arange(0, 16, 2, dtype=np.float32) / np.float32(16)))
    lane = np.arange(LANES)
    tabs = []
    for freqs, half, first, period, keep in ((inv_m, 16, 64, 128, lane < 96), (inv_m, 16, 0, 128, lane < 32),
                                              (inv_d, 8, 0, 64, lane >= 0)):
        freq, x1, x2 = _lane_consts(freqs, half, first, period)
        ang = p * freq
        sin = jnp.sin(ang)
        tabs.append((jnp.cos(ang) * keep.astype(np.float32)[None, :], sin * x2, sin * (-x1)))
    return tuple(tabs)


def _perm(a, d):
    if d == 1:
        return a
    s, c = a.shape
    return a.reshape(s // d, d, c).transpose(1, 0, 2).reshape(s, c)


def _unperm(a, d):
    if d == 1:
        return a
    s, c = a.shape
    return a.reshape(d, s // d, c).transpose(1, 0, 2).reshape(s, c)


def _inproj(x, gpre, w_pad, d_tab):
    s = x.shape[0]
    tm, tn = min(1024, s), 768
    rope_lo, rope_hi = Q_OFF // tn, V_OFF // tn

    def body(x_ref, g_ref, w_ref, c_ref, sp_ref, sm_ref, o_ref, h_ref):
        j = pl.program_id(1)

        @pl.when(j == 0)
        def _():
            hv, _, _ = _rms(x_ref[...], g_ref[...])
            h_ref[...] = hv.astype(BF16)

        acc = _dot(h_ref[...], w_ref[...])
        is_rope = jnp.logical_and(j >= rope_lo, j < rope_hi)

        @pl.when(is_rope)
        def _():
            o_ref[...] = _rope(acc, _tile_lanes(c_ref[...], tn), _tile_lanes(sp_ref[...], tn),
                               _tile_lanes(sm_ref[...], tn), 8).astype(BF16)

        @pl.when(jnp.logical_not(is_rope))
        def _():
            o_ref[...] = acc.astype(BF16)

    row = lambda i, j: (i, 0)
    return pl.pallas_call(
        body, name="inproj", grid=(s // tm, IN_PAD // tn),
        in_specs=[_full((tm, D_MODEL), row), _full((1, D_MODEL), lambda i, j: (0, 0)),
                  _full((D_MODEL, tn), lambda i, j: (0, j)),
                  _full((tm, LANES), row), _full((tm, LANES), row), _full((tm, LANES), row)],
        out_specs=[_full((tm, tn), lambda i, j: (i, j)), _full((tm, D_MODEL), row)],
        out_shape=[jax.ShapeDtypeStruct((s, IN_PAD), BF16), jax.ShapeDtypeStruct((s, D_MODEL), BF16)],
        compiler_params=_cparams(("parallel", "arbitrary")),
    )(x, gpre, w_pad, *d_tab)


def _mla_prep(proj, gq, gkv, w_uq, w_uk, w_uv, q_tab, k_tab):
    s = proj.shape[0]
    tm = min(512, s)

    def body(cq_ref, kr_ref, ckv_ref, gq_ref, gkv_ref, wq_ref, wk_ref, wv_ref,
             qc, qsp, qsm, kc, ksp, ksm, q_out, k_out, v_out, qt_out, kt_out, vt_out):
        cqn, _, _ = _rms(cq_ref[...].astype(F32), gq_ref[...])
        q = _dot(cqn.astype(BF16), wq_ref[...])
        q = _rope(q, _tile_lanes(qc[...], 1024), _tile_lanes(qsp[...], 1024), _tile_lanes(qsm[...], 1024), 16)
        q = q * MLA_QSCALE
        q_out[...] = q.astype(BF16)
        qt_out[...] = q.T.astype(BF16)
        ckvn, _, _ = _rms(ckv_ref[...].astype(F32), gkv_ref[...])
        ckvn = ckvn.astype(BF16)
        kr = _rope(kr_ref[...].astype(F32), kc[...], ksp[...], ksm[...], 16)
        k = _dot(ckvn, wk_ref[...]) + _tile_lanes(pltpu.roll(kr, 64, 1), 1024)
        k_out[...] = k.astype(BF16)
        kt_out[...] = k.T.astype(BF16)
        v = _dot(ckvn, wv_ref[...])
        v_out[...] = v.astype(BF16)
        vt_out[...] = v.T.astype(BF16)

    row = lambda i: (i, 0)
    col = lambda i: (0, i)
    cst = lambda i: (0, 0)
    tabs = [_full((tm, LANES), row)] * 6
    return pl.pallas_call(
        body, name="mla_prep", grid=(s // tm,),
        in_specs=[_full((tm, 384), lambda i: (i, CQ_OFF // 384)), _full((tm, 128), lambda i: (i, KR_OFF // 128)),
                  _full((tm, 256), lambda i: (i, CKV_OFF // 256)), _full((1, 384), cst), _full((1, 256), cst),
                  _full((384, 1024), cst), _full((256, 1024), cst), _full((256, 512), cst)] + tabs,
        out_specs=[_full((tm, 1024), row), _full((tm, 1024), row), _full((tm, 512), row),
                   _full((1024, tm), col), _full((1024, tm), col), _full((512, tm), col)],
        out_shape=[jax.ShapeDtypeStruct((s, 1024), BF16), jax.ShapeDtypeStruct((s, 1024), BF16),
                   jax.ShapeDtypeStruct((s, 512), BF16), jax.ShapeDtypeStruct((1024, s), BF16),
                   jax.ShapeDtypeStruct((1024, s), BF16), jax.ShapeDtypeStruct((512, s), BF16)],
        compiler_params=_cparams(("parallel",)),
    )(proj, proj, proj, gq, gkv, w_uq, w_uk, w_uv, *q_tab, *k_tab)


def _mla_fwd(q, k, vt):
    s = q.shape[0]
    tq = tk = min(256, s)
    nq = s // tq

    def body(q_ref, k_ref, vt_ref, o_ref, ot_ref, lse_ref):
        causal = (lax.broadcasted_iota(jnp.int32, (tk, tq), 0) <= lax.broadcasted_iota(jnp.int32, (tk, tq), 1))

        def q_step(i, _):
            r0 = pl.multiple_of(i * tq, tq)
            qs = [q_ref[pl.ds(r0, tq), hh * 128:(hh + 1) * 128] for hh in range(2)]

            def scores(j):
                c0 = pl.multiple_of(j * tk, tk)
                return tuple(_dot_nt(k_ref[pl.ds(c0, tk), hh * 128:(hh + 1) * 128], qs[hh])
                             for hh in range(2))

            def update(j, sts, stats, masked):
                c0 = pl.multiple_of(j * tk, tk)
                new = []
                for hh in range(2):
                    m, l, acc = stats[hh]
                    st = jnp.where(causal, sts[hh], NEG) if masked else sts[hh]
                    m_new = jnp.maximum(m, jnp.max(st, axis=0, keepdims=True))
                    alpha = jnp.exp2(m - m_new)
                    p = jnp.exp2(st - m_new)
                    l = alpha * l + jnp.sum(p, axis=0, keepdims=True)
                    acc = acc * alpha + _dot(vt_ref[hh * 64:(hh + 1) * 64, pl.ds(c0, tk)], p.astype(BF16))
                    new.append((m_new, l, acc))
                return tuple(new)

            def kv_step(j, carry):
                sts, stats = carry
                nxt = scores(j + 1)
                return nxt, update(j, sts, stats, False)

            init = tuple((jnp.full((1, tq), NEG, F32), jnp.zeros((1, tq), F32), jnp.zeros((64, tq), F32))
                         for _ in range(2))
            sts, stats = lax.fori_loop(0, i, kv_step, (scores(0), init))
            (ma, la, acca), (mb, lb, accb) = update(i, sts, stats, True)
            ot = jnp.concatenate([acca / la, accb / lb], axis=0)
            ot_ref[:, pl.ds(r0, tq)] = ot.astype(BF16)
            o_ref[pl.ds(r0, tq), :] = ot.T.astype(BF16)
            lse_ref[:, pl.ds(r0, tq)] = jnp.concatenate(
                [ma + jnp.log2(la), mb + jnp.log2(lb), jnp.zeros((6, tq), F32)], axis=0)
            return 0

        lax.fori_loop(0, nq, q_step, 0)

    return pl.pallas_call(
        body, name="mla_fwd", grid=(4,),
        in_specs=[_full((s, 256), lambda p: (0, p)), _full((s, 256), lambda p: (0, p)),
                  _full((128, s), lambda p: (p, 0))],
        out_specs=[_full((s, 128), lambda p: (0, p)), _full((128, s), lambda p: (p, 0)),
                   _full((8, s), lambda p: (p, 0))],
        out_shape=[jax.ShapeDtypeStruct((s, 512), BF16), jax.ShapeDtypeStruct((512, s), BF16),
                   jax.ShapeDtypeStruct((32, s), F32)],
        compiler_params=_cparams(("parallel",)),
    )(q, k, vt)


def _band_masks(has_prev):
    r = lax.broadcasted_iota(jnp.int32, (Q_BLOCK, Q_BLOCK), 0)
    c = lax.broadcasted_iota(jnp.int32, (Q_BLOCK, Q_BLOCK), 1)
    return c >= r + jnp.where(has_prev, 0, Q_BLOCK), c <= r


def _dil_fwd(qa, ka, va, qo, ko, vo, d, name):
    s = qa.shape[0]
    nblk = s // Q_BLOCK
    per_seq = nblk // d

    def body(q_ref, k_ref, v_ref, o_ref, lse_ref):
        left = _left_mask()

        hms = (left, jnp.logical_not(left))

        def scores(b):
            r0 = pl.multiple_of(b * Q_BLOCK, Q_BLOCK)
            p0 = pl.multiple_of(jnp.maximum(b - 1, 0) * Q_BLOCK, Q_BLOCK)
            qb = q_ref[pl.ds(r0, Q_BLOCK), :]
            kp, kc = k_ref[pl.ds(p0, Q_BLOCK), :], k_ref[pl.ds(r0, Q_BLOCK), :]
            out = []
            for hh in range(2):
                qm = jnp.where(hms[hh], qb, jnp.zeros_like(qb))
                out.append((_dot_nt(qm, kp) * DIL_SCALE, _dot_nt(qm, kc) * DIL_SCALE))
            return out

        def finish(b, tiles):
            r0 = pl.multiple_of(b * Q_BLOCK, Q_BLOCK)
            p0 = pl.multiple_of(jnp.maximum(b - 1, 0) * Q_BLOCK, Q_BLOCK)
            pmask, cur_ok = _band_masks((b % per_seq) > 0)
            vp, vc = v_ref[pl.ds(p0, Q_BLOCK), :], v_ref[pl.ds(r0, Q_BLOCK), :]
            zero = jnp.zeros_like(vp)
            outs = []
            for hh in range(2):
                sp = jnp.where(pmask, tiles[hh][0], NEG)
                sc = jnp.where(cur_ok, tiles[hh][1], NEG)
                m = jnp.maximum(jnp.max(sp, axis=1, keepdims=True), jnp.max(sc, axis=1, keepdims=True))
                pp, pc = jnp.exp(sp - m), jnp.exp(sc - m)
                l = jnp.sum(pp, axis=1, keepdims=True) + jnp.sum(pc, axis=1, keepdims=True)
                acc = (_dot(pp.astype(BF16), jnp.where(hms[hh], vp, zero))
                       + _dot(pc.astype(BF16), jnp.where(hms[hh], vc, zero)))
                outs.append((acc / l, jnp.broadcast_to(m + jnp.log(l), (Q_BLOCK, LANES))))
            o_ref[pl.ds(r0, Q_BLOCK), :] = (outs[0][0] + outs[1][0]).astype(BF16)
            lse_ref[pl.ds(r0, Q_BLOCK), :] = jnp.where(left, outs[0][1], outs[1][1])

        def step(t, _):
            ta, tb = scores(2 * t), scores(2 * t + 1)
            finish(2 * t, ta)
            finish(2 * t + 1, tb)
            return 0

        lax.fori_loop(0, nblk // 2, step, 0)

    blk = lambda off: _full((s, 128), lambda p, off=off: (0, off + p))
    return pl.pallas_call(
        body, name=name, grid=(4,),
        in_specs=[blk(qo), blk(ko), blk(vo)],
        out_specs=[blk(0), blk(0)],
        out_shape=[jax.ShapeDtypeStruct((s, 512), BF16), jax.ShapeDtypeStruct((s, 512), F32)],
        compiler_params=_cparams(("parallel",)),
    )(qa, ka, va)


def _merge_fwd(proj, o_mla, od, lsed, wp_mla, wp_dil):
    s = proj.shape[0]
    tm = min(512, s)

    def body(zm_ref, zd_ref, gm_ref, gd_ref, om_ref, o0, o1, o2, l0, l1, l2, wm_ref, wd_ref,
             mg_out, ya_out, yd_out, odil_out, lse_out):
        la, lb, lc = l0[...], l1[...], l2[...]
        lmax = jnp.maximum(jnp.maximum(la, lb), lc)
        ea, eb, ec = jnp.exp(la - lmax), jnp.exp(lb - lmax), jnp.exp(lc - lmax)
        den = ea + eb + ec
        o_dil = (ea * o0[...].astype(F32) + eb * o1[...].astype(F32) + ec * o2[...].astype(F32)) / den
        o_dil = o_dil.astype(BF16)
        odil_out[...] = o_dil
        lse_out[...] = lmax + jnp.log(den)
        zm, zd = zm_ref[...].astype(F32), zd_ref[...].astype(F32)
        pa = (om_ref[...].astype(F32) * (zm * _sigmoid(zm))).astype(BF16)
        pd = (o_dil.astype(F32) * (zd * _sigmoid(zd))).astype(BF16)
        ya = _dot(pa, wm_ref[...])
        yd = _dot(pd, wd_ref[...])
        ya_out[...] = ya.astype(BF16)
        yd_out[...] = yd.astype(BF16)
        mg_out[...] = (_sigmoid(gm_ref[...].astype(F32)) * ya + _sigmoid(gd_ref[...].astype(F32)) * yd).astype(BF16)

    row = lambda i: (i, 0)
    cst = lambda i: (0, 0)
    r512 = _full((tm, 512), row)
    r1024 = _full((tm, 1024), row)
    return pl.pallas_call(
        body, name="merge_fwd", grid=(s // tm,),
        in_specs=[_full((tm, 512), lambda i: (i, Z_MLA // 512)), _full((tm, 512), lambda i: (i, Z_DIL // 512)),
                  _full((tm, 1024), lambda i: (i, G_MLA // 1024)), _full((tm, 1024), lambda i: (i, G_DIL // 1024)),
                  r512, r512, r512, r512, r512, r512, r512, _full((512, 1024), cst), _full((512, 1024), cst)],
        out_specs=[r1024, r1024, r1024, r512, r512],
        out_shape=[jax.ShapeDtypeStruct((s, 1024), BF16), jax.ShapeDtypeStruct((s, 1024), BF16),
                   jax.ShapeDtypeStruct((s, 1024), BF16), jax.ShapeDtypeStruct((s, 512), BF16),
                   jax.ShapeDtypeStruct((s, 512), F32)],
        compiler_params=_cparams(("parallel",)),
    )(proj, proj, proj, proj, o_mla, *od, *lsed, wp_mla, wp_dil)


def _out_loss(merged, w_out, x, target, gpost):
    s = x.shape[0]
    tm = min(512, s)

    def body(mg_ref, w_ref, x_ref, t_ref, g_ref, do_out, dy_out, loss_out, dg_out):
        i = pl.program_id(0)

        @pl.when(i == 0)
        def _():
            loss_out[...] = jnp.zeros_like(loss_out)
            dg_out[...] = jnp.zeros_like(dg_out)

        o = _dot(mg_ref[...], w_ref[...])
        g = g_ref[...]
        n, u, r = _rms(o, g)
        e = (x_ref[...] + n) - t_ref[...]
        loss_out[...] += 0.5 * jnp.sum(jnp.mean(e * e, axis=-1, keepdims=True))
        dy = e * (1.0 / D_MODEL)
        dy_out[...] = dy
        dg_out[...] += jnp.sum(dy * u, axis=0, keepdims=True)
        do_out[...] = _rms_bwd(dy, g, u, r).astype(BF16)

    row = lambda i: (i, 0)
    cst = lambda i: (0, 0)
    return pl.pallas_call(
        body, name="out_loss", grid=(s // tm,),
        in_specs=[_full((tm, 1024), row), _full((1024, 1024), cst), _full((tm, 1024), row), _full((tm, 1024), row),
                  _full((1, 1024), cst)],
        out_specs=[_full((tm, 1024), row), _full((tm, 1024), row), _full((8, LANES), cst), _full((1, 1024), cst)],
        out_shape=[jax.ShapeDtypeStruct((s, 1024), BF16), jax.ShapeDtypeStruct((s, 1024), F32),
                   jax.ShapeDtypeStruct((8, LANES), F32), jax.ShapeDtypeStruct((1, 1024), F32)],
        compiler_params=_cparams(("arbitrary",)),
    )(merged, w_out, x, target, gpost)


def _seg_sum64(x, ones_bd):
    hi = x.astype(BF16)
    lo = (x - hi.astype(F32)).astype(BF16)
    return _dot(hi, ones_bd) + _dot(lo, ones_bd)


def _merge_bwd(do, w_out, merged, proj, ya, yd, o_mla, o_dil, wp_mla, wp_dil):
    s = do.shape[0]
    tm = min(256, s)
    seg = jnp.arange(512) // 64
    ones_bd = (seg[:, None] == seg[None, :]).astype(BF16)

    def body(do_ref, wo_ref, mg_ref, zm_ref, zd_ref, gm_ref, gd_ref, ya_ref, yd_ref, om_ref, od_ref, wm_ref, wd_ref,
             bd_ref, dzm_out, dzd_out, dgm_out, dgd_out, dom_out, dod_out, domt_out, dd_out, dwo_out, dwm_out,
             dwd_out):
        i = pl.program_id(0)

        @pl.when(i == 0)
        def _():
            dwo_out[...] = jnp.zeros_like(dwo_out)
            dwm_out[...] = jnp.zeros_like(dwm_out)
            dwd_out[...] = jnp.zeros_like(dwd_out)

        dov = do_ref[...]
        dwo_out[...] += _dot_tn(mg_ref[...], dov)
        dm = _dot_nt(dov, wo_ref[...])
        for g_ref, y_ref, z_ref, o_ref, w_ref, dz_out, dg_out, dob_out, dd_o, dw_out in (
                (gm_ref, ya_ref, zm_ref, om_ref, wm_ref, dzm_out, dgm_out, dom_out, None, dwm_out),
                (gd_ref, yd_ref, zd_ref, od_ref, wd_ref, dzd_out, dgd_out, dod_out, dd_out, dwd_out)):
            sg = _sigmoid(g_ref[...].astype(F32))
            dg_out[...] = (dm * y_ref[...].astype(F32) * sg * (1.0 - sg)).astype(BF16)
            dy = (dm * sg).astype(BF16)
            z = z_ref[...].astype(F32)
            sz = _sigmoid(z)
            silu = z * sz
            ob = o_ref[...].astype(F32)
            dw_out[...] += _dot_tn((ob * silu).astype(BF16), dy)
            dp = _dot_nt(dy, w_ref[...])
            dz_out[...] = (dp * ob * (sz * (1.0 + z * (1.0 - sz)))).astype(BF16)
            dob = dp * silu
            dob_out[...] = dob.astype(BF16)
            if dd_o is None:
                domt_out[...] = dob.T.astype(BF16)
            else:
                dd_o[...] = _seg_sum64(dob * ob, bd_ref[...])

    row = lambda i: (i, 0)
    cst = lambda i: (0, 0)
    r512 = _full((tm, 512), row)
    r1024 = _full((tm, 1024), row)
    return pl.pallas_call(
        body, name="merge_bwd", grid=(s // tm,),
        in_specs=[r1024, _full((1024, 1024), cst), r1024,
                  _full((tm, 512), lambda i: (i, Z_MLA // 512)), _full((tm, 512), lambda i: (i, Z_DIL // 512)),
                  _full((tm, 1024), lambda i: (i, G_MLA // 1024)), _full((tm, 1024), lambda i: (i, G_DIL // 1024)),
                  r1024, r1024, r512, r512, _full((512, 1024), cst), _full((512, 1024), cst), _full((512, 512), cst)],
        out_specs=[r512, r512, r1024, r1024, r512, r512, _full((512, tm), lambda i: (0, i)), r512,
                   _full((1024, 1024), cst), _full((512, 1024), cst), _full((512, 1024), cst)],
        out_shape=[jax.ShapeDtypeStruct((s, 512), BF16), jax.ShapeDtypeStruct((s, 512), BF16),
                   jax.ShapeDtypeStruct((s, 1024), BF16), jax.ShapeDtypeStruct((s, 1024), BF16),
                   jax.ShapeDtypeStruct((s, 512), BF16), jax.ShapeDtypeStruct((s, 512), BF16),
                   jax.ShapeDtypeStruct((512, s), BF16), jax.ShapeDtypeStruct((s, 512), F32),
                   jax.ShapeDtypeStruct((1024, 1024), F32), jax.ShapeDtypeStruct((512, 1024), F32),
                   jax.ShapeDtypeStruct((512, 1024), F32)],
        compiler_params=_cparams(("arbitrary",)),
    )(do, w_out, merged, proj, proj, proj, proj, ya, yd, o_mla, o_dil, wp_mla, wp_dil, ones_bd)


def _mla_bwd(q, qt, k, kt, v, do, dot, ot, lse):
    s = q.shape[0]
    tq = tk = min(256, s)
    nq = s // tq

    def body(q_ref, qt_ref, k_ref, kt_ref, v_ref, do_ref, dot_ref, ot_ref, lse_ref, dqt_out, dkt_out, dvt_out,
             dqt_acc):
        left = _left_mask()
        causal = (lax.broadcasted_iota(jnp.int32, (tk, tq), 0) <= lax.broadcasted_iota(jnp.int32, (tk, tq), 1))
        dqt_acc[...] = jnp.zeros_like(dqt_acc)

        def kv_step(j, _):
            c0 = pl.multiple_of(j * tk, tk)
            vv = v_ref[pl.ds(c0, tk), :]
            khs = [k_ref[pl.ds(c0, tk), hh * 128:(hh + 1) * 128] for hh in range(2)]
            kths = [kt_ref[hh * 128:(hh + 1) * 128, pl.ds(c0, tk)] for hh in range(2)]
            vms = [jnp.where(left if hh == 0 else jnp.logical_not(left), vv, jnp.zeros_like(vv)) for hh in range(2)]

            def scores(i):
                r0 = pl.multiple_of(jnp.minimum(i, nq - 1) * tq, tq)
                dov = do_ref[pl.ds(r0, tq), :]
                return tuple((_dot_nt(khs[hh], q_ref[pl.ds(r0, tq), hh * 128:(hh + 1) * 128]),
                              _dot_nt(vms[hh], dov)) for hh in range(2))

            def update(i, tiles, acc, masked):
                r0 = pl.multiple_of(i * tq, tq)
                new = []
                for hh in range(2):
                    dkt, dvt = acc[hh]
                    st, dp = tiles[hh]
                    hrows = slice(hh * 128, (hh + 1) * 128)
                    drows = slice(hh * 64, (hh + 1) * 64)
                    doth = dot_ref[drows, pl.ds(r0, tq)]
                    dd = jnp.sum(doth.astype(F32) * ot_ref[drows, pl.ds(r0, tq)].astype(F32), axis=0, keepdims=True)
                    p = jnp.exp2(st - lse_ref[hh:hh + 1, pl.ds(r0, tq)])
                    if masked:
                        p = jnp.where(causal, p, 0.0)
                    ds = (p * (dp - dd)).astype(BF16)
                    dvt = dvt + _dot_nt(doth, p.astype(BF16))
                    dkt = dkt + _dot_nt(qt_ref[hrows, pl.ds(r0, tq)], ds)
                    dqt_acc[hrows, pl.ds(r0, tq)] += _dot(kths[hh], ds)
                    new.append((dkt, dvt))
                return tuple(new)

            init = tuple((jnp.zeros((128, tk), F32), jnp.zeros((64, tk), F32)) for _ in range(2))
            acc = update(j, scores(j), init, True)
            acc = lax.fori_loop(j + 1, nq, lambda i, a: update(i, scores(i), a, False), acc)
            for hh in range(2):
                dkt_out[hh * 128:(hh + 1) * 128, pl.ds(c0, tk)] = (acc[hh][0] * (1.0 / LOG2E)).astype(BF16)
                dvt_out[hh * 64:(hh + 1) * 64, pl.ds(c0, tk)] = acc[hh][1].astype(BF16)
            return 0

        lax.fori_loop(0, nq, kv_step, 0)
        dqt_out[...] = (dqt_acc[...] * MLA_SCALE).astype(BF16)

    b256 = _full((s, 256), lambda p: (0, p))
    b128 = _full((s, 128), lambda p: (0, p))
    t256 = _full((256, s), lambda p: (p, 0))
    t128 = _full((128, s), lambda p: (p, 0))
    return pl.pallas_call(
        body, name="mla_bwd", grid=(4,),
        in_specs=[b256, t256, b256, t256, b128, b128, t128, t128, _full((8, s), lambda p: (p, 0))],
        out_specs=[t256, t256, t128],
        out_shape=[jax.ShapeDtypeStruct((1024, s), BF16), jax.ShapeDtypeStruct((1024, s), BF16),
                   jax.ShapeDtypeStruct((512, s), BF16)],
        scratch_shapes=[pltpu.VMEM((256, s), F32)],
        compiler_params=_cparams(("parallel",)),
    )(q, qt, k, kt, v, do, dot, ot, lse)


def _mla_prep_bwd(dq, dk, dv, proj, gq, gkv, w_uq, w_uk, w_uv, q_tab, k_tab):
    s = proj.shape[0]
    tm = min(256, s)

    def body(dqt_ref, dkt_ref, dvt_ref, cq_ref, ckv_ref, gq_ref, gkv_ref, wq_ref, wk_ref, wv_ref,
             qc, qsp, qsm, kc, ksp, ksm,
             dcq_out, dkr_out, dckv_out, dwq_out, dwk_out, dwv_out, dgq_out, dgkv_out):
        i = pl.program_id(0)

        @pl.when(i == 0)
        def _():
            for r in (dwq_out, dwk_out, dwv_out, dgq_out, dgkv_out):
                r[...] = jnp.zeros_like(r)

        dqu = _unrope(dqt_ref[...].astype(F32).T, _tile_lanes(qc[...], 1024), _tile_lanes(qsp[...], 1024),
                      _tile_lanes(qsm[...], 1024), 16).astype(BF16)
        gq = gq_ref[...]
        cqn, xh, r = _rms(cq_ref[...].astype(F32), gq)
        dwq_out[...] += _dot_tn(cqn.astype(BF16), dqu)
        dcqn = _dot_nt(dqu, wq_ref[...])
        dgq_out[...] += jnp.sum(dcqn * xh, axis=0, keepdims=True)
        dcq_out[...] = _rms_bwd(dcqn, gq, xh, r).astype(BF16)

        dkf = dkt_ref[...].astype(F32).T
        dkb = dkf.astype(BF16)
        dsum = dkf[:, 0:128]
        for h in range(1, MLA_HEADS):
            dsum = dsum + dkf[:, h * 128:(h + 1) * 128]
        dkr_out[...] = _unrope(pltpu.roll(dsum, 64, 1), kc[...], ksp[...], ksm[...], 16).astype(BF16)

        dvb = dvt_ref[...].astype(F32).T.astype(BF16)
        gkv = gkv_ref[...]
        ckvn, xh2, r2 = _rms(ckv_ref[...].astype(F32), gkv)
        ckvn = ckvn.astype(BF16)
        dwk_out[...] += _dot_tn(ckvn, dkb)
        dwv_out[...] += _dot_tn(ckvn, dvb)
        dckvn = _dot_nt(dkb, wk_ref[...]) + _dot_nt(dvb, wv_ref[...])
        dgkv_out[...] += jnp.sum(dckvn * xh2, axis=0, keepdims=True)
        dckv_out[...] = _rms_bwd(dckvn, gkv, xh2, r2).astype(BF16)

    row = lambda i: (i, 0)
    cst = lambda i: (0, 0)
    tabs = [_full((tm, LANES), row)] * 6
    return pl.pallas_call(
        body, name="mla_prep_bwd", grid=(s // tm,),
        in_specs=[_full((1024, tm), lambda i: (0, i)), _full((1024, tm), lambda i: (0, i)),
                  _full((512, tm), lambda i: (0, i)),
                  _full((tm, 384), lambda i: (i, CQ_OFF // 384)), _full((tm, 256), lambda i: (i, CKV_OFF // 256)),
                  _full((1, 384), cst), _full((1, 256), cst),
                  _full((384, 1024), cst), _full((256, 1024), cst), _full((256, 512), cst)] + tabs,
        out_specs=[_full((tm, 384), row), _full((tm, 128), row), _full((tm, 256), row),
                   _full((384, 1024), cst), _full((256, 1024), cst), _full((256, 512), cst),
                   _full((1, 384), cst), _full((1, 256), cst)],
        out_shape=[jax.ShapeDtypeStruct((s, 384), BF16), jax.ShapeDtypeStruct((s, 128), BF16),
                   jax.ShapeDtypeStruct((s, 256), BF16),
                   jax.ShapeDtypeStruct((384, 1024), F32), jax.ShapeDtypeStruct((256, 1024), F32),
                   jax.ShapeDtypeStruct((256, 512), F32),
                   jax.ShapeDtypeStruct((1, 384), F32), jax.ShapeDtypeStruct((1, 256), F32)],
        compiler_params=_cparams(("arbitrary",)),
    )(dq, dk, dv, proj, proj, gq, gkv, w_uq, w_uk, w_uv, *q_tab, *k_tab)


def _dil_bwd(qa, ka, va, qo, ko, vo, do, lse, dd, tabs, d, name):
    s = qa.shape[0]
    nblk = s // Q_BLOCK
    per_seq = nblk // d

    def body(q_ref, k_ref, v_ref, do_ref, lse_ref, dd_ref, c_ref, sp_ref, sm_ref, dq_out, dk_out, dv_out,
             dk_acc, dv_acc):
        left = _left_mask()
        dk_acc[...] = jnp.zeros_like(dk_acc)
        dv_acc[...] = jnp.zeros_like(dv_acc)

        hms = (left, jnp.logical_not(left))

        def rows_of(b):
            r0 = pl.multiple_of(b * Q_BLOCK, Q_BLOCK)
            p0 = pl.multiple_of(jnp.maximum(b - 1, 0) * Q_BLOCK, Q_BLOCK)
            return pl.ds(r0, Q_BLOCK), pl.ds(p0, Q_BLOCK)

        def scores(b):
            rows, prow = rows_of(b)
            qb, dob = q_ref[rows, :], do_ref[rows, :]
            kp, kc, vp, vc = k_ref[prow, :], k_ref[rows, :], v_ref[prow, :], v_ref[rows, :]
            zero = jnp.zeros_like(qb)
            out = []
            for hh in range(2):
                qm, dom = jnp.where(hms[hh], qb, zero), jnp.where(hms[hh], dob, zero)
                out.append((_dot_nt(qm, kp) * DIL_SCALE, _dot_nt(qm, kc) * DIL_SCALE,
                            _dot_nt(dom, vp), _dot_nt(dom, vc)))
            return out

        def finish(b, tiles):
            rows, prow = rows_of(b)
            pmask, cur_ok = _band_masks((b % per_seq) > 0)
            qb, dob = q_ref[rows, :], do_ref[rows, :]
            kp, kc = k_ref[prow, :], k_ref[rows, :]
            lse_b, dd_b = lse_ref[rows, :], dd_ref[rows, :]
            zero = jnp.zeros_like(qb)
            dq = jnp.zeros((Q_BLOCK, LANES), F32)
            dkp = jnp.zeros((Q_BLOCK, LANES), F32)
            dkc = jnp.zeros((Q_BLOCK, LANES), F32)
            dvp = jnp.zeros((Q_BLOCK, LANES), F32)
            dvc = jnp.zeros((Q_BLOCK, LANES), F32)
            for hh in range(2):
                hm = hms[hh]
                qm, dom = jnp.where(hm, qb, zero), jnp.where(hm, dob, zero)
                lse_h, dd_h = _expand_half(lse_b, hh, left), _expand_half(dd_b, hh, left)
                sp, sc, dpp, dpc = tiles[hh]
                pp = jnp.where(pmask, jnp.exp(sp - lse_h), 0.0)
                pc = jnp.where(cur_ok, jnp.exp(sc - lse_h), 0.0)
                dsp = (pp * (dpp - dd_h) * DIL_SCALE).astype(BF16)
                dsc = (pc * (dpc - dd_h) * DIL_SCALE).astype(BF16)
                dq = dq + _dot(dsp, jnp.where(hm, kp, zero)) + _dot(dsc, jnp.where(hm, kc, zero))
                dkp, dkc = dkp + _dot_tn(dsp, qm), dkc + _dot_tn(dsc, qm)
                dvp, dvc = dvp + _dot_tn(pp.astype(BF16), dom), dvc + _dot_tn(pc.astype(BF16), dom)
            dq_out[rows, :] = _unrope(dq, c_ref[rows, :], sp_ref[rows, :], sm_ref[rows, :], 8).astype(BF16)
            dk_acc[prow, :] += dkp
            dv_acc[prow, :] += dvp
            dk_acc[rows, :] += dkc
            dv_acc[rows, :] += dvc

        def step(t, _):
            ta, tb = scores(2 * t), scores(2 * t + 1)
            finish(2 * t, ta)
            finish(2 * t + 1, tb)
            return 0

        lax.fori_loop(0, nblk // 2, step, 0)
        dk_out[...] = _unrope(dk_acc[...], c_ref[...], sp_ref[...], sm_ref[...], 8).astype(BF16)
        dv_out[...] = dv_acc[...].astype(BF16)

    blk = lambda off: _full((s, 128), lambda p, off=off: (0, off + p))
    tab = _full((s, 128), lambda p: (0, 0))
    return pl.pallas_call(
        body, name=name, grid=(4,),
        in_specs=[blk(qo), blk(ko), blk(vo), blk(0), blk(0), blk(0), tab, tab, tab],
        out_specs=[blk(0), blk(0), blk(0)],
        out_shape=[jax.ShapeDtypeStruct((s, 512), BF16)] * 3,
        scratch_shapes=[pltpu.VMEM((s, 128), F32), pltpu.VMEM((s, 128), F32)],
        compiler_params=_cparams(("parallel",)),
    )(qa, ka, va, do, lse, dd, *tabs)


def _dh_bwd(dproj, w_pad, x, gpre, dy):
    s = x.shape[0]
    tm, tk = min(512, s), 1408
    nk = IN_PAD // tk

    def body(dp_ref, w_ref, x_ref, g_ref, dy_ref, gx_out, dg_out, acc):
        i, kk = pl.program_id(0), pl.program_id(1)

        @pl.when(jnp.logical_and(i == 0, kk == 0))
        def _():
            dg_out[...] = jnp.zeros_like(dg_out)

        @pl.when(kk == 0)
        def _():
            acc[...] = jnp.zeros_like(acc)

        acc[...] += _dot_nt(dp_ref[...], w_ref[...])

        @pl.when(kk == nk - 1)
        def _():
            g = g_ref[...]
            _, xh, r = _rms(x_ref[...], g)
            dh = acc[...]
            dg_out[...] += jnp.sum(dh * xh, axis=0, keepdims=True)
            gx_out[...] = dy_ref[...] + _rms_bwd(dh, g, xh, r)

    row = lambda i, k: (i, 0)
    return pl.pallas_call(
        body, name="dh_bwd", grid=(s // tm, nk),
        in_specs=[_full((tm, tk), lambda i, k: (i, k)), _full((1024, tk), lambda i, k: (0, k)),
                  _full((tm, 1024), row), _full((1, 1024), lambda i, k: (0, 0)), _full((tm, 1024), row)],
        out_specs=[_full((tm, 1024), row), _full((1, 1024), lambda i, k: (0, 0))],
        out_shape=[jax.ShapeDtypeStruct((s, 1024), F32), jax.ShapeDtypeStruct((1, 1024), F32)],
        scratch_shapes=[pltpu.VMEM((tm, 1024), F32)],
        compiler_params=_cparams(("arbitrary", "arbitrary")),
    )(dproj, w_pad, x, gpre, dy)


def _dw_in(h, dproj):
    s = h.shape[0]
    ts, tn = min(1024, s), 768
    ns = s // ts

    def body(h_ref, dp_ref, o_ref, acc):
        k = pl.program_id(1)

        @pl.when(k == 0)
        def _():
            acc[...] = jnp.zeros_like(acc)

        acc[...] += _dot_tn(h_ref[...], dp_ref[...])

        @pl.when(k == ns - 1)
        def _():
            o_ref[...] = acc[...].astype(BF16)

    return pl.pallas_call(
        body, name="dw_in", grid=(IN_PAD // tn, ns),
        in_specs=[_full((ts, 1024), lambda j, k: (k, 0)), _full((ts, tn), lambda j, k: (k, j))],
        out_specs=_full((1024, tn), lambda j, k: (0, j)),
        out_shape=jax.ShapeDtypeStruct((1024, IN_PAD), BF16),
        scratch_shapes=[pltpu.VMEM((1024, tn), F32)],
        compiler_params=_cparams(("parallel", "arbitrary")),
    )(h, dproj)


def _remote(src, dst, sems, row, k, to):
    return pltpu.make_async_remote_copy(src_ref=src, dst_ref=dst, send_sem=sems.at[row, k], recv_sem=sems.at[row + 1, k],
                                        device_id=to, device_id_type=pl.DeviceIdType.MESH)


def _place():
    x, y, c = lax.axis_index("x"), lax.axis_index("y"), lax.axis_index("c")
    return x, y, c, [(1 - x, y), (x, 1 - y), (1 - x, 1 - y)]


def _gather_weights(arrays):
    n = len(arrays)

    def body(*refs):
        ins, outs, sems = refs[:n], refs[n:2 * n], refs[2 * n]
        x, y, c, chips = _place()
        me, sib = (x, y, c), (x, y, 1 - c)
        idx = lambda p: 4 * p[0] + 2 * p[1] + p[2]

        def copy(a, k, block, to, from_input=False):
            src = ins[a] if from_input else outs[a].at[idx(block)]
            return _remote(src, outs[a].at[idx(block)], sems, 2 * a, k, to)

        own = [pltpu.make_async_copy(ins[a], outs[a].at[idx(me)], sems.at[2 * a, 7]) for a in range(n)]
        first = [copy(a, 0, me, sib, True) for a in range(n)]
        first += [copy(a, 1 + j, me, (*chip, c), True) for j, chip in enumerate(chips) for a in range(n)]
        for cp in own + first:
            cp.start()
        passed = []
        for j, chip in enumerate(chips):
            for a in range(n):
                copy(a, 1 + j, (*chip, c), me).wait_recv()
                passed.append(copy(a, 4 + j, (*chip, c), sib))
                passed[-1].start()
        for a in range(n):
            copy(a, 0, sib, me).wait_recv()
        for j, chip in enumerate(chips):
            for a in range(n):
                copy(a, 4 + j, (*chip, 1 - c), me).wait_recv()
        for cp in first + passed:
            cp.wait_send()
        for cp in own:
            cp.wait()

    hbm = pl.BlockSpec(memory_space=pl.ANY)
    return pl.pallas_call(
        body, name="gather_weights", in_specs=[hbm] * n, out_specs=[hbm] * n,
        out_shape=[jax.ShapeDtypeStruct((N_DEV,) + a.shape, a.dtype) for a in arrays],
        scratch_shapes=[pltpu.SemaphoreType.DMA((2 * n, N_DEV))],
    )(*arrays)


def _pair_exchange(chunks):
    n = len(chunks)

    def body(*refs):
        ins, outs, sems = refs[:n], refs[n:2 * n], refs[2 * n]
        x, y, c, _ = _place()
        sent = [_remote(ins[a].at[2 * q + (1 - c)], outs[a].at[q], sems, 2 * a, q, (x, y, 1 - c))
                for a in range(n) for q in range(4)]
        for cp in sent:
            cp.start()
        for cp in sent:
            cp.wait_recv()
        for cp in sent:
            cp.wait_send()

    hbm = pl.BlockSpec(memory_space=pl.ANY)
    return pl.pallas_call(
        body, name="pair_exchange", in_specs=[hbm] * n, out_specs=[hbm] * n,
        out_shape=[jax.ShapeDtypeStruct((4,) + a.shape[1:], a.dtype) for a in chunks],
        scratch_shapes=[pltpu.SemaphoreType.DMA((2 * n, 4))],
    )(*chunks)


def _pair_sum(core, chunks, recv, name, tr):
    _, rows, cols = chunks.shape

    def body(c_ref, a_ref, b_ref, o_ref):
        o_ref[...] = (a_ref[...].astype(F32) + b_ref[...].astype(F32)).astype(BF16)

    blk = lambda f: _full((1, tr, cols), f)
    return pl.pallas_call(
        body, name=name, out_shape=jax.ShapeDtypeStruct((4, rows, cols), BF16),
        grid_spec=pltpu.PrefetchScalarGridSpec(
            num_scalar_prefetch=1, grid=(4, rows // tr),
            in_specs=[blk(lambda q, i, c: (2 * q + c[0], i, 0)), blk(lambda q, i, c: (q, i, 0))],
            out_specs=blk(lambda q, i, c: (q, i, 0))),
        compiler_params=_cparams(("parallel", "parallel")),
    )(core, chunks, recv)


def _chip_exchange(pairs, packet):
    n = len(pairs)

    def body(*refs):
        ins, pk, outs, pk_out, sems = refs[:n], refs[n], refs[n + 1:2 * n + 1], refs[2 * n + 1], refs[2 * n + 2]
        x, y, c, chips = _place()
        myq, me = 2 * x + y, 4 * x + 2 * y + c
        flip = lambda v, b: (1 - v) if b else v

        def chunk(a, j, outgoing):
            q = 2 * chips[j][0] + chips[j][1]
            return _remote(ins[a].at[q], outs[a].at[myq if outgoing else q], sems, 2 * a, j, (*chips[j], c))

        def small(j, outgoing):
            peer = (flip(x, (j >> 2) & 1), flip(y, (j >> 1) & 1), flip(c, j & 1))
            slot = me if outgoing else 4 * peer[0] + 2 * peer[1] + peer[2]
            return _remote(pk, pk_out.at[slot], sems, 2 * n, j, peer)

        own = [pltpu.make_async_copy(ins[a].at[myq], outs[a].at[myq], sems.at[2 * a, 3]) for a in range(n)]
        own.append(pltpu.make_async_copy(pk, pk_out.at[me], sems.at[2 * n, 0]))
        sent = [chunk(a, j, True) for j in range(3) for a in range(n)] + [small(j, True) for j in range(1, N_DEV)]
        for cp in own + sent:
            cp.start()
        for cp in [chunk(a, j, False) for j in range(3) for a in range(n)] + [small(j, False) for j in range(1, N_DEV)]:
            cp.wait_recv()
        for cp in sent:
            cp.wait_send()
        for cp in own:
            cp.wait()

    hbm = pl.BlockSpec(memory_space=pl.ANY)
    return pl.pallas_call(
        body, name="chip_exchange", in_specs=[hbm] * (n + 1), out_specs=[hbm] * (n + 1),
        out_shape=[jax.ShapeDtypeStruct(a.shape, a.dtype) for a in pairs]
        + [jax.ShapeDtypeStruct((N_DEV,) + packet.shape, packet.dtype)],
        scratch_shapes=[pltpu.SemaphoreType.DMA((2 * n + 2, N_DEV))],
    )(*pairs, packet)


def _adam_math(w, g, m, v):
    m = ADAM_B1 * m + (1.0 - ADAM_B1) * g
    v = ADAM_B2 * v + (1.0 - ADAM_B2) * (g * g)
    m_hat = m / (1.0 - ADAM_B1 ** ADAM_STEP)
    v_hat = v / (1.0 - ADAM_B2 ** ADAM_STEP)
    delta = -ADAM_LR * (m_hat / (jnp.sqrt(v_hat) + ADAM_EPS) + ADAM_WD * w)
    return delta, m, v


def _adam(recv, w, m, v, name, tr):
    _, rows, cols = w.shape

    def body(r_ref, w_ref, m_ref, v_ref, g_out, d_out, m_out, v_out):
        g = r_ref[0].astype(F32)
        for k in range(1, 4):
            g = g + r_ref[k].astype(F32)
        g_out[0] = g
        d_out[0], m_out[0], v_out[0] = _adam_math(w_ref[0], g, m_ref[0], v_ref[0])

    blk = _full((1, tr, cols), lambda i: (0, i, 0))
    return pl.pallas_call(
        body, name=name, grid=(rows // tr,),
        in_specs=[_full((4, tr, cols), lambda i: (0, i, 0)), blk, blk, blk],
        out_specs=[blk] * 4,
        out_shape=[jax.ShapeDtypeStruct(w.shape, F32)] * 4,
        compiler_params=_cparams(("parallel",)),
    )(recv, w, m, v)


def _adam_gains(recv, gains, gains_m, gains_v):
    def body(*refs):
        r_ref, w, m, v = refs[0], refs[1:5], refs[5:9], refs[9:13]
        g_out, d_out, m_out, v_out, loss_out = refs[13:17], refs[17:21], refs[21:25], refs[25:29], refs[29]
        tot = r_ref[0:1, :]
        for k in range(1, N_DEV):
            tot = tot + r_ref[k:k + 1, :]
        for t in range(4):
            g = tot[:, GAIN_OFFS[t]:GAIN_OFFS[t] + GAIN_WIDTHS[t]]
            g_out[t][...] = g
            d_out[t][...], m_out[t][...], v_out[t][...] = _adam_math(w[t][...], g, m[t][...], v[t][...])
        loss_out[...] = tot[:, LOSS_OFF:LOSS_OFF + LANES]

    shapes = [jax.ShapeDtypeStruct((1, n), F32) for n in GAIN_WIDTHS]
    return pl.pallas_call(
        body, name="adam_gains", out_shape=shapes * 4 + [jax.ShapeDtypeStruct((1, LANES), F32)],
    )(recv, *gains, *gains_m, *gains_v)


def _local_step(x, positions, gains, weights, target):
    gpre, gq, gkv, gpost = gains
    w_pad, w_uq, w_uk, w_uv, wp_mla, wp_dil, w_out = weights
    q_tab, k_tab, d_tab = _rope_tables(positions)

    proj, h = _inproj(x, gpre, w_pad, d_tab)
    q, k, v, qt, kt, vt = _mla_prep(proj, gq, gkv, w_uq, w_uk, w_uv, q_tab, k_tab)
    o_mla, ot_mla, lse_mla = _mla_fwd(q, k, vt)

    qkv_perm, od, lsed = [], [], []
    for g, d in enumerate(DIL_DILATIONS):
        if d == 1:
            arr, offs = proj, ((Q_OFF + 512 * g) // 128, (K_OFF + 512 * g) // 128, (V_OFF + 512 * g) // 128)
        else:
            cols = jnp.concatenate([proj[:, o + 512 * g:o + 512 * (g + 1)] for o in (Q_OFF, K_OFF, V_OFF)], axis=1)
            arr, offs = _perm(cols, d), (0, 4, 8)
        qkv_perm.append((arr, offs))
        o_g, lse_g = _dil_fwd(arr, arr, arr, *offs, d, "dil_fwd_%d" % g)
        od.append(_unperm(o_g, d))
        lsed.append(_unperm(lse_g, d))

    merged, ya, yd, o_dil, lse_dil = _merge_fwd(proj, o_mla, od, lsed, wp_mla, wp_dil)
    do, dy, loss, dgpost = _out_loss(merged, w_out, x, target, gpost)

    (dz_mla, dz_dil, dg_mla, dg_dil, do_mla, do_dil, dot_mla, dd_dil, dw_out, dwp_mla, dwp_dil) = _merge_bwd(
        do, w_out, merged, proj, ya, yd, o_mla, o_dil, wp_mla, wp_dil)

    dq, dk, dv = _mla_bwd(q, qt, k, kt, v, do_mla, dot_mla, ot_mla, lse_mla)
    dcq, dkr, dckv, dw_uq, dw_uk, dw_uv, dgq, dgkv = _mla_prep_bwd(dq, dk, dv, proj, gq, gkv, w_uq, w_uk, w_uv,
                                                                   q_tab, k_tab)

    dqs, dks, dvs = [], [], []
    for g, d in enumerate(DIL_DILATIONS):
        arr, offs = qkv_perm[g]
        tabs = tuple(_perm(t, d) for t in d_tab)
        dq_g, dk_g, dv_g = _dil_bwd(arr, arr, arr, *offs, _perm(do_dil, d), _perm(lse_dil, d), _perm(dd_dil, d),
                                    tabs, d, "dil_bwd_%d" % g)
        dqs.append(_unperm(dq_g, d))
        dks.append(_unperm(dk_g, d))
        dvs.append(_unperm(dv_g, d))

    dproj = jnp.concatenate([dz_mla, dz_dil, dg_mla, dg_dil] + dqs + dks + dvs + [dcq, dkr, dckv], axis=1)
    grad_x, dgpre = _dh_bwd(dproj, w_pad, x, gpre, dy)
    dw_in = _dw_in(h, dproj)
    return loss, grad_x, (dgpre, dgq, dgkv, dgpost), (dw_in, dw_uq, dw_uk, dw_uv, dwp_mla, dwp_dil, dw_out)


ADAM_ROWS = (256, 384, 256, 512, 512, 128)
PAIR_ROWS = (512, 384, 256, 512, 512, 128)


def kernel(x, positions, pre_norm_g, w_in, q_norm_g, w_uq, kv_norm_g, w_ukv, w_proj_mla, w_proj_dil, w_out, post_norm_g, loss_target, m_pre_norm_g, m_w_in, m_q_norm_g, m_w_uq, m_kv_norm_g, m_w_ukv, m_w_proj_mla, m_w_proj_dil, m_w_out, m_post_norm_g, v_pre_norm_g, v_w_in, v_q_norm_g, v_w_uq, v_kv_norm_g, v_w_ukv, v_w_proj_mla, v_w_proj_dil, v_w_out, v_post_norm_g):
    big_w = (w_in, w_uq, w_ukv, w_proj_mla, w_proj_dil, w_out)
    big_m = (m_w_in, m_w_uq, m_w_ukv, m_w_proj_mla, m_w_proj_dil, m_w_out)
    big_v = (v_w_in, v_w_uq, v_w_ukv, v_w_proj_mla, v_w_proj_dil, v_w_out)
    gains = (pre_norm_g, q_norm_g, kv_norm_g, post_norm_g)
    gains_m = (m_pre_norm_g, m_q_norm_g, m_kv_norm_g, m_post_norm_g)
    gains_v = (v_pre_norm_g, v_q_norm_g, v_kv_norm_g, v_post_norm_g)

    gathered = _gather_weights([w[0].astype(BF16) for w in big_w])
    weights = _assemble_weights(*gathered)

    loss, grad_x, dgains, dweights = _local_step(x[0], positions[0], gains, weights, loss_target[0])

    packet = jnp.concatenate(list(dgains) + [loss[0:1]], axis=1)
    chunks = _grad_chunks(*dweights)
    from_sibling = _pair_exchange(chunks)
    core = lax.axis_index("c").astype(jnp.int32).reshape(1)
    pairs = [_pair_sum(core, chunks[t], from_sibling[t], "pair_sum_%d" % t, PAIR_ROWS[t]) for t in range(6)]
    received = _chip_exchange(pairs, packet)

    big = [_adam(received[t], big_w[t], big_m[t], big_v[t], "adam_%d" % t, ADAM_ROWS[t]) for t in range(6)]
    small = _adam_gains(received[6].reshape(N_DEV, PACKET), gains, gains_m, gains_v)

    def interleave(kind):
        s_pre, s_q, s_kv, s_post = small[4 * kind:4 * kind + 4]
        b_in, b_uq, b_ukv, b_pm, b_pd, b_out = (big[t][kind] for t in range(6))
        return [s_pre, b_in, s_q, b_uq, s_kv, b_ukv, b_pm, b_pd, b_out, s_post]

    return (small[16][0, 0], grad_x[None], *interleave(0), *interleave(1), *interleave(2), *interleave(3))
```

```python
import numpy as np
import jax
import jax.numpy as jnp
from jax import lax
from jax.experimental import pallas as pl
from jax.experimental.pallas import tpu as pltpu

F32 = jnp.float32
BF16 = jnp.bfloat16

D_MODEL = 1024
NORM_EPS = 1e-6
ROPE_THETA = 500000.0
N_DEV = 8
LANES = 128
NEG = -1e30

MLA_HEADS = 8
MLA_Q_RANK = 384
MLA_KV_RANK = 256
MLA_SCALE = 96.0 ** -0.5
LOG2E = 1.4426950408889634
MLA_QSCALE = MLA_SCALE * LOG2E
MLA_FWD_Q_PER_K = 2
DIL_DILATIONS = (1, 4, 16)
DIL_SCALE = 0.125
Q_BLOCK = 128

Z_MLA, Z_DIL, G_MLA, G_DIL = 0, 512, 1024, 2048
Q_OFF, K_OFF, V_OFF = 3072, 4608, 6144
CQ_OFF, KR_OFF, CKV_OFF, IN_PAD = 7680, 8064, 8192, 8448
IN_WIDTH = 8352
SHARD_W = IN_WIDTH // 8
IN_SEGS = ((0, 384, CQ_OFF), (384, 256, CKV_OFF), (640, 32, KR_OFF), (672, 1536, Q_OFF), (2208, 1536, K_OFF),
           (3744, 1536, V_OFF), (5280, 512, Z_MLA), (5792, 512, Z_DIL), (6304, 1024, G_MLA), (7328, 1024, G_DIL))

GAIN_OFFS = (0, 1024, 1408, 1664)
GAIN_WIDTHS = (1024, 384, 256, 1024)
LOSS_OFF, PACKET = 2688, 2816

ADAM_LR, ADAM_B1, ADAM_B2, ADAM_EPS, ADAM_WD, ADAM_STEP = 0.001, 0.9, 0.999, 1e-08, 0.01, 10

VMEM_LIMIT_MB = 56


def _cparams(sem=None, vmem_mb=VMEM_LIMIT_MB):
    return pltpu.CompilerParams(dimension_semantics=sem, vmem_limit_bytes=vmem_mb * 1024 * 1024)


def _dot(a, b):
    return jnp.dot(a, b, preferred_element_type=F32)


def _dot_nt(a, b):
    return lax.dot_general(a, b, (((1,), (1,)), ((), ())), preferred_element_type=F32)


def _dot_tn(a, b):
    return lax.dot_general(a, b, (((0,), (0,)), ((), ())), preferred_element_type=F32)


def _tile_lanes(t, width):
    return t if width == t.shape[1] else jnp.tile(t, (1, width // t.shape[1]))


def _rope(x, c, sp, sm, a):
    n = x.shape[1]
    return x * c + pltpu.roll(x, a, 1) * sp + pltpu.roll(x, n - a, 1) * sm


def _unrope(dy, c, sp, sm, a):
    n = dy.shape[1]
    return dy * c + pltpu.roll(dy * sp, n - a, 1) + pltpu.roll(dy * sm, a, 1)


def _sigmoid(z):
    return 1.0 / (1.0 + jnp.exp(-z))


def _left_mask():
    return lax.broadcasted_iota(jnp.int32, (1, LANES), 1) < 64


def _expand_half(x, hh, left):
    r = pltpu.roll(x, 64, 1)
    return jnp.where(left, x, r) if hh == 0 else jnp.where(left, r, x)


def _rms(xv, g):
    r = lax.rsqrt(jnp.mean(xv * xv, axis=-1, keepdims=True) + NORM_EPS)
    xh = xv * r
    return xh * g, xh, r


def _rms_bwd(dout, g, xh, r):
    dxh = dout * g
    return r * (dxh - xh * jnp.mean(dxh * xh, axis=-1, keepdims=True))


def _full(shape, index_map):
    return pl.BlockSpec(shape, index_map)


def _w_in_pieces():
    out = []
    for s, n, off in sorted(IN_SEGS, key=lambda t: t[2]):
        c = s
        while c < s + n:
            k = c // SHARD_W
            e = min(s + n, (k + 1) * SHARD_W)
            out.append((k, c - k * SHARD_W, e - c, off + (c - s)))
            c = e
    return out


def _assemble_w_in(g):
    parts, cur = [], 0
    for k, a, w, off in _w_in_pieces():
        if off > cur:
            parts.append(jnp.zeros((D_MODEL, off - cur), g.dtype))
        parts.append(g[k, :, a:a + w])
        cur = off + w
    if cur < IN_PAD:
        parts.append(jnp.zeros((D_MODEL, IN_PAD - cur), g.dtype))
    return jnp.concatenate(parts, axis=1)


def _dw_in_chunks(dw):
    chunks = []
    for dev in range(N_DEV):
        mine = sorted((p for p in _w_in_pieces() if p[0] == dev), key=lambda p: p[1])
        chunks.append(jnp.concatenate([dw[:, off:off + w] for k, a, w, off in mine], axis=1))
    return jnp.stack(chunks)


def _assemble_weights(g_in, g_uq, g_ukv, g_pm, g_pd, g_out):
    w_uq_pad = jnp.pad(g_uq.transpose(1, 0, 2), ((0, 0), (0, 0), (0, 32))).reshape(384, 1024)
    ukv = g_ukv.transpose(1, 0, 2)
    w_uk_pad = jnp.pad(ukv[:, :, :64], ((0, 0), (0, 0), (0, 64))).reshape(256, 1024)
    w_uv = ukv[:, :, 64:].reshape(256, 512)
    wp_mla = g_pm.transpose(1, 0, 2).reshape(512, 1024)
    wp_dil = g_pd.transpose(1, 0, 2).reshape(512, 1024)
    return _assemble_w_in(g_in), w_uq_pad, w_uk_pad, w_uv, wp_mla, wp_dil, g_out.reshape(1024, 1024)


def _grad_chunks(dw_in_pad, dw_uq_pad, dw_uk_pad, dw_uv, dwp_mla, dwp_dil, dw_out):
    a = _dw_in_chunks(dw_in_pad)
    b = dw_uq_pad.reshape(384, 8, 128)[:, :, :96].transpose(1, 0, 2)
    c = jnp.concatenate([dw_uk_pad.reshape(256, 8, 128)[:, :, :64], dw_uv.reshape(256, 8, 64)], axis=2)
    c = c.transpose(1, 0, 2)
    d = dwp_mla.reshape(512, N_DEV, 128).transpose(1, 0, 2)
    e = dwp_dil.reshape(512, N_DEV, 128).transpose(1, 0, 2)
    f = dw_out.reshape(N_DEV, 128, 1024)
    return [t.astype(BF16) for t in (a, b, c, d, e, f)]


def _lane_consts(freqs, half, first, period):
    rel = (np.arange(LANES) % period) - first
    rot = (rel >= 0) & (rel < 2 * half)
    freq = np.where(rot, freqs[np.clip(rel, 0, 2 * half - 1) % half], 0.0).astype(np.float32)
    x1 = (rot & (rel < half)).astype(np.float32)
    x2 = (rot & (rel >= half)).astype(np.float32)
    return freq[None, :], x1[None, :], x2[None, :]


def _rope_tables(pos):
    p = pos.astype(F32)[:, None]
    inv_m = np.float32(ROPE_THETA) ** (-(np.arange(0, 32, 2, dtype=np.float32) / np.float32(32)))
    inv_d = np.float32(ROPE_THETA) ** (-(np.arange(0, 16, 2, dtype=np.float32) / np.float32(16)))
    lane = np.arange(LANES)
    tabs = []
    for freqs, half, first, period, keep in ((inv_m, 16, 64, 128, lane < 96), (inv_m, 16, 0, 128, lane < 32),
                                              (inv_d, 8, 0, 64, lane >= 0)):
        freq, x1, x2 = _lane_consts(freqs, half, first, period)
        ang = p * freq
        sin = jnp.sin(ang)
        tabs.append((jnp.cos(ang) * keep.astype(np.float32)[None, :], sin * x2, sin * (-x1)))
    return tuple(tabs)


def _perm(a, d):
    if d == 1:
        return a
    s, c = a.shape
    return a.reshape(s // d, d, c).transpose(1, 0, 2).reshape(s, c)


def _unperm(a, d):
    if d == 1:
        return a
    s, c = a.shape
    return a.reshape(d, s // d, c).transpose(1, 0, 2).reshape(s, c)


def _inproj(x, gpre, w_pad, d_tab):
    s = x.shape[0]
    tm, tn = min(1024, s), 768
    rope_lo, rope_hi = Q_OFF // tn, V_OFF // tn

    def body(x_ref, g_ref, w_ref, c_ref, sp_ref, sm_ref, o_ref, h_ref):
        j = pl.program_id(1)

        @pl.when(j == 0)
        def _():
            hv, _, _ = _rms(x_ref[...], g_ref[...])
            h_ref[...] = hv.astype(BF16)

        acc = _dot(h_ref[...], w_ref[...])
        is_rope = jnp.logical_and(j >= rope_lo, j < rope_hi)

        @pl.when(is_rope)
        def _():
            o_ref[...] = _rope(acc, _tile_lanes(c_ref[...], tn), _tile_lanes(sp_ref[...], tn),
                               _tile_lanes(sm_ref[...], tn), 8).astype(BF16)

        @pl.when(jnp.logical_not(is_rope))
        def _():
            o_ref[...] = acc.astype(BF16)

    row = lambda i, j: (i, 0)
    return pl.pallas_call(
        body, name="inproj", grid=(s // tm, IN_PAD // tn),
        in_specs=[_full((tm, D_MODEL), row), _full((1, D_MODEL), lambda i, j: (0, 0)),
                  _full((D_MODEL, tn), lambda i, j: (0, j)),
                  _full((tm, LANES), row), _full((tm, LANES), row), _full((tm, LANES), row)],
        out_specs=[_full((tm, tn), lambda i, j: (i, j)), _full((tm, D_MODEL), row)],
        out_shape=[jax.ShapeDtypeStruct((s, IN_PAD), BF16), jax.ShapeDtypeStruct((s, D_MODEL), BF16)],
        compiler_params=_cparams(("parallel", "arbitrary")),
    )(x, gpre, w_pad, *d_tab)


def _mla_prep(proj, gq, gkv, w_uq, w_uk, w_uv, q_tab, k_tab):
    s = proj.shape[0]
    tm = min(512, s)

    def body(cq_ref, kr_ref, ckv_ref, gq_ref, gkv_ref, wq_ref, wk_ref, wv_ref,
             qc, qsp, qsm, kc, ksp, ksm, q_out, k_out, v_out, qt_out, kt_out, vt_out):
        cqn, _, _ = _rms(cq_ref[...].astype(F32), gq_ref[...])
        q = _dot(cqn.astype(BF16), wq_ref[...])
        q = _rope(q, _tile_lanes(qc[...], 1024), _tile_lanes(qsp[...], 1024), _tile_lanes(qsm[...], 1024), 16)
        q = q * MLA_QSCALE
        q_out[...] = q.astype(BF16)
        qt_out[...] = q.T.astype(BF16)
        ckvn, _, _ = _rms(ckv_ref[...].astype(F32), gkv_ref[...])
        ckvn = ckvn.astype(BF16)
        kr = _rope(kr_ref[...].astype(F32), kc[...], ksp[...], ksm[...], 16)
        k = _dot(ckvn, wk_ref[...]) + _tile_lanes(pltpu.roll(kr, 64, 1), 1024)
        k_out[...] = k.astype(BF16)
        kt_out[...] = k.T.astype(BF16)
        v = _dot(ckvn, wv_ref[...])
        v_out[...] = v.astype(BF16)
        vt_out[...] = v.T.astype(BF16)

    row = lambda i: (i, 0)
    col = lambda i: (0, i)
    cst = lambda i: (0, 0)
    tabs = [_full((tm, LANES), row)] * 6
    return pl.pallas_call(
        body, name="mla_prep", grid=(s // tm,),
        in_specs=[_full((tm, 384), lambda i: (i, CQ_OFF // 384)), _full((tm, 128), lambda i: (i, KR_OFF // 128)),
                  _full((tm, 256), lambda i: (i, CKV_OFF // 256)), _full((1, 384), cst), _full((1, 256), cst),
                  _full((384, 1024), cst), _full((256, 1024), cst), _full((256, 512), cst)] + tabs,
        out_specs=[_full((tm, 1024), row), _full((tm, 1024), row), _full((tm, 512), row),
                   _full((1024, tm), col), _full((1024, tm), col), _full((512, tm), col)],
        out_shape=[jax.ShapeDtypeStruct((s, 1024), BF16), jax.ShapeDtypeStruct((s, 1024), BF16),
                   jax.ShapeDtypeStruct((s, 512), BF16), jax.ShapeDtypeStruct((1024, s), BF16),
                   jax.ShapeDtypeStruct((1024, s), BF16), jax.ShapeDtypeStruct((512, s), BF16)],
        compiler_params=_cparams(("parallel",)),
    )(proj, proj, proj, gq, gkv, w_uq, w_uk, w_uv, *q_tab, *k_tab)


def _mla_fwd(q, k, vt):
    s = q.shape[0]
    tk = min(256, s)
    ratio = MLA_FWD_Q_PER_K if s >= MLA_FWD_Q_PER_K * tk else 1
    tq = ratio * tk
    nq = s // tq

    def body(q_ref, k_ref, vt_ref, o_ref, ot_ref, lse_ref):
        krow = lax.broadcasted_iota(jnp.int32, (tk, tq), 0)
        qcol = lax.broadcasted_iota(jnp.int32, (tk, tq), 1)

        def q_step(i, _):
            r0 = pl.multiple_of(i * tq, tq)
            qs = [q_ref[pl.ds(r0, tq), hh * 128:(hh + 1) * 128] for hh in range(2)]

            def scores(j):
                c0 = pl.multiple_of(j * tk, tk)
                return tuple(_dot_nt(k_ref[pl.ds(c0, tk), hh * 128:(hh + 1) * 128], qs[hh])
                             for hh in range(2))

            def update(j, sts, stats, masked):
                c0 = pl.multiple_of(j * tk, tk)
                new = []
                causal = (krow + (c0 - r0)) <= qcol
                for hh in range(2):
                    m, l, acc = stats[hh]
                    st = jnp.where(causal, sts[hh], NEG) if masked else sts[hh]
                    m_new = jnp.maximum(m, jnp.max(st, axis=0, keepdims=True))
                    alpha = jnp.exp2(m - m_new)
                    p = jnp.exp2(st - m_new)
                    l = alpha * l + jnp.sum(p, axis=0, keepdims=True)
                    acc = acc * alpha + _dot(vt_ref[hh * 64:(hh + 1) * 64, pl.ds(c0, tk)], p.astype(BF16))
                    new.append((m_new, l, acc))
                return tuple(new)

            def kv_step(j, carry):
                sts, stats = carry
                nxt = scores(j + 1)
                return nxt, update(j, sts, stats, False)

            init = tuple((jnp.full((1, tq), NEG, F32), jnp.zeros((1, tq), F32), jnp.zeros((64, tq), F32))
                         for _ in range(2))
            sts, stats = lax.fori_loop(0, ratio * i, kv_step, (scores(0), init))
            for d in range(ratio):
                if d > 0:
                    sts = scores(ratio * i + d)
                stats = update(ratio * i + d, sts, stats, True)
            (ma, la, acca), (mb, lb, accb) = stats
            ot = jnp.concatenate([acca / la, accb / lb], axis=0)
            ot_ref[:, pl.ds(r0, tq)] = ot.astype(BF16)
            o_ref[pl.ds(r0, tq), :] = ot.T.astype(BF16)
            lse_ref[:, pl.ds(r0, tq)] = jnp.concatenate(
                [ma + jnp.log2(la), mb + jnp.log2(lb), jnp.zeros((6, tq), F32)], axis=0)
            return 0

        lax.fori_loop(0, nq, q_step, 0)

    return pl.pallas_call(
        body, name="mla_fwd", grid=(4,),
        in_specs=[_full((s, 256), lambda p: (0, p)), _full((s, 256), lambda p: (0, p)),
                  _full((128, s), lambda p: (p, 0))],
        out_specs=[_full((s, 128), lambda p: (0, p)), _full((128, s), lambda p: (p, 0)),
                   _full((8, s), lambda p: (p, 0))],
        out_shape=[jax.ShapeDtypeStruct((s, 512), BF16), jax.ShapeDtypeStruct((512, s), BF16),
                   jax.ShapeDtypeStruct((32, s), F32)],
        compiler_params=_cparams(("parallel",)),
    )(q, k, vt)


def _band_masks(has_prev):
    r = lax.broadcasted_iota(jnp.int32, (Q_BLOCK, Q_BLOCK), 0)
    c = lax.broadcasted_iota(jnp.int32, (Q_BLOCK, Q_BLOCK), 1)
    return c >= r + jnp.where(has_prev, 0, Q_BLOCK), c <= r


def _dil_fwd(qa, ka, va, qo, ko, vo, d, name):
    s = qa.shape[0]
    nblk = s // Q_BLOCK
    per_seq = nblk // d

    def body(q_ref, k_ref, v_ref, o_ref, lse_ref):
        left = _left_mask()

        hms = (left, jnp.logical_not(left))

        def scores(b):
            r0 = pl.multiple_of(b * Q_BLOCK, Q_BLOCK)
            p0 = pl.multiple_of(jnp.maximum(b - 1, 0) * Q_BLOCK, Q_BLOCK)
            qb = q_ref[pl.ds(r0, Q_BLOCK), :]
            kp, kc = k_ref[pl.ds(p0, Q_BLOCK), :], k_ref[pl.ds(r0, Q_BLOCK), :]
            out = []
            for hh in range(2):
                qm = jnp.where(hms[hh], qb, jnp.zeros_like(qb))
                out.append((_dot_nt(qm, kp) * DIL_SCALE, _dot_nt(qm, kc) * DIL_SCALE))
            return out

        def finish(b, tiles):
            r0 = pl.multiple_of(b * Q_BLOCK, Q_BLOCK)
            p0 = pl.multiple_of(jnp.maximum(b - 1, 0) * Q_BLOCK, Q_BLOCK)
            pmask, cur_ok = _band_masks((b % per_seq) > 0)
            vp, vc = v_ref[pl.ds(p0, Q_BLOCK), :], v_ref[pl.ds(r0, Q_BLOCK), :]
            zero = jnp.zeros_like(vp)
            outs = []
            for hh in range(2):
                sp = jnp.where(pmask, tiles[hh][0], NEG)
                sc = jnp.where(cur_ok, tiles[hh][1], NEG)
                m = jnp.maximum(jnp.max(sp, axis=1, keepdims=True), jnp.max(sc, axis=1, keepdims=True))
                pp, pc = jnp.exp(sp - m), jnp.exp(sc - m)
                l = jnp.sum(pp, axis=1, keepdims=True) + jnp.sum(pc, axis=1, keepdims=True)
                acc = (_dot(pp.astype(BF16), jnp.where(hms[hh], vp, zero))
                       + _dot(pc.astype(BF16), jnp.where(hms[hh], vc, zero)))
                outs.append((acc / l, jnp.broadcast_to(m + jnp.log(l), (Q_BLOCK, LANES))))
            o_ref[pl.ds(r0, Q_BLOCK), :] = (outs[0][0] + outs[1][0]).astype(BF16)
            lse_ref[pl.ds(r0, Q_BLOCK), :] = jnp.where(left, outs[0][1], outs[1][1])

        def step(t, _):
            ta, tb = scores(2 * t), scores(2 * t + 1)
            finish(2 * t, ta)
            finish(2 * t + 1, tb)
            return 0

        lax.fori_loop(0, nblk // 2, step, 0)

    blk = lambda off: _full((s, 128), lambda p, off=off: (0, off + p))
    return pl.pallas_call(
        body, name=name, grid=(4,),
        in_specs=[blk(qo), blk(ko), blk(vo)],
        out_specs=[blk(0), blk(0)],
        out_shape=[jax.ShapeDtypeStruct((s, 512), BF16), jax.ShapeDtypeStruct((s, 512), F32)],
        compiler_params=_cparams(("parallel",)),
    )(qa, ka, va)


def _merge_fwd(proj, o_mla, od, lsed, wp_mla, wp_dil):
    s = proj.shape[0]
    tm = min(512, s)

    def body(zm_ref, zd_ref, gm_ref, gd_ref, om_ref, o0, o1, o2, l0, l1, l2, wm_ref, wd_ref,
             mg_out, ya_out, yd_out, odil_out, lse_out):
        la, lb, lc = l0[...], l1[...], l2[...]
        lmax = jnp.maximum(jnp.maximum(la, lb), lc)
        ea, eb, ec = jnp.exp(la - lmax), jnp.exp(lb - lmax), jnp.exp(lc - lmax)
        den = ea + eb + ec
        o_dil = (ea * o0[...].astype(F32) + eb * o1[...].astype(F32) + ec * o2[...].astype(F32)) / den
        o_dil = o_dil.astype(BF16)
        odil_out[...] = o_dil
        lse_out[...] = lmax + jnp.log(den)
        zm, zd = zm_ref[...].astype(F32), zd_ref[...].astype(F32)
        pa = (om_ref[...].astype(F32) * (zm * _sigmoid(zm))).astype(BF16)
        pd = (o_dil.astype(F32) * (zd * _sigmoid(zd))).astype(BF16)
        ya = _dot(pa, wm_ref[...])
        yd = _dot(pd, wd_ref[...])
        ya_out[...] = ya.astype(BF16)
        yd_out[...] = yd.astype(BF16)
        mg_out[...] = (_sigmoid(gm_ref[...].astype(F32)) * ya + _sigmoid(gd_ref[...].astype(F32)) * yd).astype(BF16)

    row = lambda i: (i, 0)
    cst = lambda i: (0, 0)
    r512 = _full((tm, 512), row)
    r1024 = _full((tm, 1024), row)
    return pl.pallas_call(
        body, name="merge_fwd", grid=(s // tm,),
        in_specs=[_full((tm, 512), lambda i: (i, Z_MLA // 512)), _full((tm, 512), lambda i: (i, Z_DIL // 512)),
                  _full((tm, 1024), lambda i: (i, G_MLA // 1024)), _full((tm, 1024), lambda i: (i, G_DIL // 1024)),
                  r512, r512, r512, r512, r512, r512, r512, _full((512, 1024), cst), _full((512, 1024), cst)],
        out_specs=[r1024, r1024, r1024, r512, r512],
        out_shape=[jax.ShapeDtypeStruct((s, 1024), BF16), jax.ShapeDtypeStruct((s, 1024), BF16),
                   jax.ShapeDtypeStruct((s, 1024), BF16), jax.ShapeDtypeStruct((s, 512), BF16),
                   jax.ShapeDtypeStruct((s, 512), F32)],
        compiler_params=_cparams(("parallel",)),
    )(proj, proj, proj, proj, o_mla, *od, *lsed, wp_mla, wp_dil)


def _out_loss(merged, w_out, x, target, gpost):
    s = x.shape[0]
    tm = min(512, s)

    def body(mg_ref, w_ref, x_ref, t_ref, g_ref, do_out, dy_out, loss_out, dg_out):
        i = pl.program_id(0)

        @pl.when(i == 0)
        def _():
            loss_out[...] = jnp.zeros_like(loss_out)
            dg_out[...] = jnp.zeros_like(dg_out)

        o = _dot(mg_ref[...], w_ref[...])
        g = g_ref[...]
        n, u, r = _rms(o, g)
        e = (x_ref[...] + n) - t_ref[...]
        loss_out[...] += 0.5 * jnp.sum(jnp.mean(e * e, axis=-1, keepdims=True))
        dy = e * (1.0 / D_MODEL)
        dy_out[...] = dy
        dg_out[...] += jnp.sum(dy * u, axis=0, keepdims=True)
        do_out[...] = _rms_bwd(dy, g, u, r).astype(BF16)

    row = lambda i: (i, 0)
    cst = lambda i: (0, 0)
    return pl.pallas_call(
        body, name="out_loss", grid=(s // tm,),
        in_specs=[_full((tm, 1024), row), _full((1024, 1024), cst), _full((tm, 1024), row), _full((tm, 1024), row),
                  _full((1, 1024), cst)],
        out_specs=[_full((tm, 1024), row), _full((tm, 1024), row), _full((8, LANES), cst), _full((1, 1024), cst)],
        out_shape=[jax.ShapeDtypeStruct((s, 1024), BF16), jax.ShapeDtypeStruct((s, 1024), F32),
                   jax.ShapeDtypeStruct((8, LANES), F32), jax.ShapeDtypeStruct((1, 1024), F32)],
        compiler_params=_cparams(("arbitrary",)),
    )(merged, w_out, x, target, gpost)


def _seg_sum64(x, ones_bd):
    hi = x.astype(BF16)
    lo = (x - hi.astype(F32)).astype(BF16)
    return _dot(hi, ones_bd) + _dot(lo, ones_bd)


def _merge_bwd(do, w_out, merged, proj, ya, yd, o_mla, o_dil, wp_mla, wp_dil):
    s = do.shape[0]
    tm = min(256, s)
    seg = jnp.arange(512) // 64
    ones_bd = (seg[:, None] == seg[None, :]).astype(BF16)

    def body(do_ref, wo_ref, mg_ref, zm_ref, zd_ref, gm_ref, gd_ref, ya_ref, yd_ref, om_ref, od_ref, wm_ref, wd_ref,
             bd_ref, dzm_out, dzd_out, dgm_out, dgd_out, dom_out, dod_out, domt_out, dd_out, dwo_out, dwm_out,
             dwd_out):
        i = pl.program_id(0)

        @pl.when(i == 0)
        def _():
            dwo_out[...] = jnp.zeros_like(dwo_out)
            dwm_out[...] = jnp.zeros_like(dwm_out)
            dwd_out[...] = jnp.zeros_like(dwd_out)

        dov = do_ref[...]
        dwo_out[...] += _dot_tn(mg_ref[...], dov)
        dm = _dot_nt(dov, wo_ref[...])
        for g_ref, y_ref, z_ref, o_ref, w_ref, dz_out, dg_out, dob_out, dd_o, dw_out in (
                (gm_ref, ya_ref, zm_ref, om_ref, wm_ref, dzm_out, dgm_out, dom_out, None, dwm_out),
                (gd_ref, yd_ref, zd_ref, od_ref, wd_ref, dzd_out, dgd_out, dod_out, dd_out, dwd_out)):
            sg = _sigmoid(g_ref[...].astype(F32))
            dg_out[...] = (dm * y_ref[...].astype(F32) * sg * (1.0 - sg)).astype(BF16)
            dy = (dm * sg).astype(BF16)
            z = z_ref[...].astype(F32)
            sz = _sigmoid(z)
            silu = z * sz
            ob = o_ref[...].astype(F32)
            dw_out[...] += _dot_tn((ob * silu).astype(BF16), dy)
            dp = _dot_nt(dy, w_ref[...])
            dz_out[...] = (dp * ob * (sz * (1.0 + z * (1.0 - sz)))).astype(BF16)
            dob = dp * silu
            dob_out[...] = dob.astype(BF16)
            if dd_o is None:
                domt_out[...] = dob.T.astype(BF16)
            else:
                dd_o[...] = _seg_sum64(dob * ob, bd_ref[...])

    row = lambda i: (i, 0)
    cst = lambda i: (0, 0)
    r512 = _full((tm, 512), row)
    r1024 = _full((tm, 1024), row)
    return pl.pallas_call(
        body, name="merge_bwd", grid=(s // tm,),
        in_specs=[r1024, _full((1024, 1024), cst), r1024,
                  _full((tm, 512), lambda i: (i, Z_MLA // 512)), _full((tm, 512), lambda i: (i, Z_DIL // 512)),
                  _full((tm, 1024), lambda i: (i, G_MLA // 1024)), _full((tm, 1024), lambda i: (i, G_DIL // 1024)),
                  r1024, r1024, r512, r512, _full((512, 1024), cst), _full((512, 1024), cst), _full((512, 512), cst)],
        out_specs=[r512, r512, r1024, r1024, r512, r512, _full((512, tm), lambda i: (0, i)), r512,
                   _full((1024, 1024), cst), _full((512, 1024), cst), _full((512, 1024), cst)],
        out_shape=[jax.ShapeDtypeStruct((s, 512), BF16), jax.ShapeDtypeStruct((s, 512), BF16),
                   jax.ShapeDtypeStruct((s, 1024), BF16), jax.ShapeDtypeStruct((s, 1024), BF16),
                   jax.ShapeDtypeStruct((s, 512), BF16), jax.ShapeDtypeStruct((s, 512), BF16),
                   jax.ShapeDtypeStruct((512, s), BF16), jax.ShapeDtypeStruct((s, 512), F32),
                   jax.ShapeDtypeStruct((1024, 1024), F32), jax.ShapeDtypeStruct((512, 1024), F32),
                   jax.ShapeDtypeStruct((512, 1024), F32)],
        compiler_params=_cparams(("arbitrary",)),
    )(do, w_out, merged, proj, proj, proj, proj, ya, yd, o_mla, o_dil, wp_mla, wp_dil, ones_bd)


def _mla_bwd(q, qt, k, kt, v, do, dot, ot, lse):
    s = q.shape[0]
    tq = tk = min(256, s)
    nq = s // tq

    def body(q_ref, qt_ref, k_ref, kt_ref, v_ref, do_ref, dot_ref, ot_ref, lse_ref, dqt_out, dkt_out, dvt_out,
             dqt_acc):
        left = _left_mask()
        causal = (lax.broadcasted_iota(jnp.int32, (tk, tq), 0) <= lax.broadcasted_iota(jnp.int32, (tk, tq), 1))
        dqt_acc[...] = jnp.zeros_like(dqt_acc)

        def kv_step(j, _):
            c0 = pl.multiple_of(j * tk, tk)
            vv = v_ref[pl.ds(c0, tk), :]
            khs = [k_ref[pl.ds(c0, tk), hh * 128:(hh + 1) * 128] for hh in range(2)]
            kths = [kt_ref[hh * 128:(hh + 1) * 128, pl.ds(c0, tk)] for hh in range(2)]
            vms = [jnp.where(left if hh == 0 else jnp.logical_not(left), vv, jnp.zeros_like(vv)) for hh in range(2)]

            def scores(i):
                r0 = pl.multiple_of(jnp.minimum(i, nq - 1) * tq, tq)
                dov = do_ref[pl.ds(r0, tq), :]
                return tuple((_dot_nt(khs[hh], q_ref[pl.ds(r0, tq), hh * 128:(hh + 1) * 128]),
                              _dot_nt(vms[hh], dov)) for hh in range(2))

            def update(i, tiles, acc, masked):
                r0 = pl.multiple_of(i * tq, tq)
                new = []
                for hh in range(2):
                    dkt, dvt = acc[hh]
                    st, dp = tiles[hh]
                    hrows = slice(hh * 128, (hh + 1) * 128)
                    drows = slice(hh * 64, (hh + 1) * 64)
                    doth = dot_ref[drows, pl.ds(r0, tq)]
                    dd = jnp.sum(doth.astype(F32) * ot_ref[drows, pl.ds(r0, tq)].astype(F32), axis=0, keepdims=True)
                    p = jnp.exp2(st - lse_ref[hh:hh + 1, pl.ds(r0, tq)])
                    if masked:
                        p = jnp.where(causal, p, 0.0)
                    ds = (p * (dp - dd)).astype(BF16)
                    dvt = dvt + _dot_nt(doth, p.astype(BF16))
                    dkt = dkt + _dot_nt(qt_ref[hrows, pl.ds(r0, tq)], ds)
                    dqt_acc[hrows, pl.ds(r0, tq)] += _dot(kths[hh], ds)
                    new.append((dkt, dvt))
                return tuple(new)

            init = tuple((jnp.zeros((128, tk), F32), jnp.zeros((64, tk), F32)) for _ in range(2))
            acc = update(j, scores(j), init, True)
            acc = lax.fori_loop(j + 1, nq, lambda i, a: update(i, scores(i), a, False), acc)
            for hh in range(2):
                dkt_out[hh * 128:(hh + 1) * 128, pl.ds(c0, tk)] = (acc[hh][0] * (1.0 / LOG2E)).astype(BF16)
                dvt_out[hh * 64:(hh + 1) * 64, pl.ds(c0, tk)] = acc[hh][1].astype(BF16)
            return 0

        lax.fori_loop(0, nq, kv_step, 0)
        dqt_out[...] = (dqt_acc[...] * MLA_SCALE).astype(BF16)

    b256 = _full((s, 256), lambda p: (0, p))
    b128 = _full((s, 128), lambda p: (0, p))
    t256 = _full((256, s), lambda p: (p, 0))
    t128 = _full((128, s), lambda p: (p, 0))
    return pl.pallas_call(
        body, name="mla_bwd", grid=(4,),
        in_specs=[b256, t256, b256, t256, b128, b128, t128, t128, _full((8, s), lambda p: (p, 0))],
        out_specs=[t256, t256, t128],
        out_shape=[jax.ShapeDtypeStruct((1024, s), BF16), jax.ShapeDtypeStruct((1024, s), BF16),
                   jax.ShapeDtypeStruct((512, s), BF16)],
        scratch_shapes=[pltpu.VMEM((256, s), F32)],
        compiler_params=_cparams(("parallel",)),
    )(q, qt, k, kt, v, do, dot, ot, lse)


def _mla_prep_bwd(dq, dk, dv, proj, gq, gkv, w_uq, w_uk, w_uv, q_tab, k_tab):
    s = proj.shape[0]
    tm = min(256, s)

    def body(dqt_ref, dkt_ref, dvt_ref, cq_ref, ckv_ref, gq_ref, gkv_ref, wq_ref, wk_ref, wv_ref,
             qc, qsp, qsm, kc, ksp, ksm,
             dcq_out, dkr_out, dckv_out, dwq_out, dwk_out, dwv_out, dgq_out, dgkv_out):
        i = pl.program_id(0)

        @pl.when(i == 0)
        def _():
            for r in (dwq_out, dwk_out, dwv_out, dgq_out, dgkv_out):
                r[...] = jnp.zeros_like(r)

        dqu = _unrope(dqt_ref[...].astype(F32).T, _tile_lanes(qc[...], 1024), _tile_lanes(qsp[...], 1024),
                      _tile_lanes(qsm[...], 1024), 16).astype(BF16)
        gq = gq_ref[...]
        cqn, xh, r = _rms(cq_ref[...].astype(F32), gq)
        dwq_out[...] += _dot_tn(cqn.astype(BF16), dqu)
        dcqn = _dot_nt(dqu, wq_ref[...])
        dgq_out[...] += jnp.sum(dcqn * xh, axis=0, keepdims=True)
        dcq_out[...] = _rms_bwd(dcqn, gq, xh, r).astype(BF16)

        dkf = dkt_ref[...].astype(F32).T
        dkb = dkf.astype(BF16)
        dsum = dkf[:, 0:128]
        for h in range(1, MLA_HEADS):
            dsum = dsum + dkf[:, h * 128:(h + 1) * 128]
        dkr_out[...] = _unrope(pltpu.roll(dsum, 64, 1), kc[...], ksp[...], ksm[...], 16).astype(BF16)

        dvb = dvt_ref[...].astype(F32).T.astype(BF16)
        gkv = gkv_ref[...]
        ckvn, xh2, r2 = _rms(ckv_ref[...].astype(F32), gkv)
        ckvn = ckvn.astype(BF16)
        dwk_out[...] += _dot_tn(ckvn, dkb)
        dwv_out[...] += _dot_tn(ckvn, dvb)
        dckvn = _dot_nt(dkb, wk_ref[...]) + _dot_nt(dvb, wv_ref[...])
        dgkv_out[...] += jnp.sum(dckvn * xh2, axis=0, keepdims=True)
        dckv_out[...] = _rms_bwd(dckvn, gkv, xh2, r2).astype(BF16)

    row = lambda i: (i, 0)
    cst = lambda i: (0, 0)
    tabs = [_full((tm, LANES), row)] * 6
    return pl.pallas_call(
        body, name="mla_prep_bwd", grid=(s // tm,),
        in_specs=[_full((1024, tm), lambda i: (0, i)), _full((1024, tm), lambda i: (0, i)),
                  _full((512, tm), lambda i: (0, i)),
                  _full((tm, 384), lambda i: (i, CQ_OFF // 384)), _full((tm, 256), lambda i: (i, CKV_OFF // 256)),
                  _full((1, 384), cst), _full((1, 256), cst),
                  _full((384, 1024), cst), _full((256, 1024), cst), _full((256, 512), cst)] + tabs,
        out_specs=[_full((tm, 384), row), _full((tm, 128), row), _full((tm, 256), row),
                   _full((384, 1024), cst), _full((256, 1024), cst), _full((256, 512), cst),
                   _full((1, 384), cst), _full((1, 256), cst)],
        out_shape=[jax.ShapeDtypeStruct((s, 384), BF16), jax.ShapeDtypeStruct((s, 128), BF16),
                   jax.ShapeDtypeStruct((s, 256), BF16),
                   jax.ShapeDtypeStruct((384, 1024), F32), jax.ShapeDtypeStruct((256, 1024), F32),
                   jax.ShapeDtypeStruct((256, 512), F32),
                   jax.ShapeDtypeStruct((1, 384), F32), jax.ShapeDtypeStruct((1, 256), F32)],
        compiler_params=_cparams(("arbitrary",)),
    )(dq, dk, dv, proj, proj, gq, gkv, w_uq, w_uk, w_uv, *q_tab, *k_tab)


def _dil_bwd(qa, ka, va, qo, ko, vo, do, lse, dd, tabs, d, name):
    s = qa.shape[0]
    nblk = s // Q_BLOCK
    per_seq = nblk // d

    def body(q_ref, k_ref, v_ref, do_ref, lse_ref, dd_ref, c_ref, sp_ref, sm_ref, dq_out, dk_out, dv_out,
             dk_acc, dv_acc):
        left = _left_mask()
        dk_acc[...] = jnp.zeros_like(dk_acc)
        dv_acc[...] = jnp.zeros_like(dv_acc)

        hms = (left, jnp.logical_not(left))

        def rows_of(b):
            r0 = pl.multiple_of(b * Q_BLOCK, Q_BLOCK)
            p0 = pl.multiple_of(jnp.maximum(b - 1, 0) * Q_BLOCK, Q_BLOCK)
            return pl.ds(r0, Q_BLOCK), pl.ds(p0, Q_BLOCK)

        def scores(b):
            rows, prow = rows_of(b)
            qb, dob = q_ref[rows, :], do_ref[rows, :]
            kp, kc, vp, vc = k_ref[prow, :], k_ref[rows, :], v_ref[prow, :], v_ref[rows, :]
            zero = jnp.zeros_like(qb)
            out = []
            for hh in range(2):
                qm, dom = jnp.where(hms[hh], qb, zero), jnp.where(hms[hh], dob, zero)
                out.append((_dot_nt(qm, kp) * DIL_SCALE, _dot_nt(qm, kc) * DIL_SCALE,
                            _dot_nt(dom, vp), _dot_nt(dom, vc)))
            return out

        def finish(b, tiles):
            rows, prow = rows_of(b)
            pmask, cur_ok = _band_masks((b % per_seq) > 0)
            qb, dob = q_ref[rows, :], do_ref[rows, :]
            kp, kc = k_ref[prow, :], k_ref[rows, :]
            lse_b, dd_b = lse_ref[rows, :], dd_ref[rows, :]
            zero = jnp.zeros_like(qb)
            dq = jnp.zeros((Q_BLOCK, LANES), F32)
            dkp = jnp.zeros((Q_BLOCK, LANES), F32)
            dkc = jnp.zeros((Q_BLOCK, LANES), F32)
            dvp = jnp.zeros((Q_BLOCK, LANES), F32)
            dvc = jnp.zeros((Q_BLOCK, LANES), F32)
            for hh in range(2):
                hm = hms[hh]
                qm, dom = jnp.where(hm, qb, zero), jnp.where(hm, dob, zero)
                lse_h, dd_h = _expand_half(lse_b, hh, left), _expand_half(dd_b, hh, left)
                sp, sc, dpp, dpc = tiles[hh]
                pp = jnp.where(pmask, jnp.exp(sp - lse_h), 0.0)
                pc = jnp.where(cur_ok, jnp.exp(sc - lse_h), 0.0)
                dsp = (pp * (dpp - dd_h) * DIL_SCALE).astype(BF16)
                dsc = (pc * (dpc - dd_h) * DIL_SCALE).astype(BF16)
                dq = dq + _dot(dsp, jnp.where(hm, kp, zero)) + _dot(dsc, jnp.where(hm, kc, zero))
                dkp, dkc = dkp + _dot_tn(dsp, qm), dkc + _dot_tn(dsc, qm)
                dvp, dvc = dvp + _dot_tn(pp.astype(BF16), dom), dvc + _dot_tn(pc.astype(BF16), dom)
            dq_out[rows, :] = _unrope(dq, c_ref[rows, :], sp_ref[rows, :], sm_ref[rows, :], 8).astype(BF16)
            dk_acc[prow, :] += dkp
            dv_acc[prow, :] += dvp
            dk_acc[rows, :] += dkc
            dv_acc[rows, :] += dvc

        def step(t, _):
            ta, tb = scores(2 * t), scores(2 * t + 1)
            finish(2 * t, ta)
            finish(2 * t + 1, tb)
            return 0

        lax.fori_loop(0, nblk // 2, step, 0)
        dk_out[...] = _unrope(dk_acc[...], c_ref[...], sp_ref[...], sm_ref[...], 8).astype(BF16)
        dv_out[...] = dv_acc[...].astype(BF16)

    blk = lambda off: _full((s, 128), lambda p, off=off: (0, off + p))
    tab = _full((s, 128), lambda p: (0, 0))
    return pl.pallas_call(
        body, name=name, grid=(4,),
        in_specs=[blk(qo), blk(ko), blk(vo), blk(0), blk(0), blk(0), tab, tab, tab],
        out_specs=[blk(0), blk(0), blk(0)],
        out_shape=[jax.ShapeDtypeStruct((s, 512), BF16)] * 3,
        scratch_shapes=[pltpu.VMEM((s, 128), F32), pltpu.VMEM((s, 128), F32)],
        compiler_params=_cparams(("parallel",)),
    )(qa, ka, va, do, lse, dd, *tabs)


def _chip_copies(ins, outs, sems, outgoing):
    x, y, c, chips = _place()
    myq = 2 * x + y
    n = len(ins)

    def chunk(a, j):
        q = 2 * chips[j][0] + chips[j][1]
        return _remote(ins[a].at[q], outs[a].at[myq if outgoing else q], sems, 2 * a, j, (*chips[j], c))

    if outgoing is None:
        return [pltpu.make_async_copy(ins[a].at[myq], outs[a].at[myq], sems.at[2 * a, 3]) for a in range(n)]
    return [chunk(a, j) for j in range(3) for a in range(n)]


def _dh_bwd(dproj, w_pad, x, gpre, dy, pairs):
    s = x.shape[0]
    tm, tk = min(512, s), 1408
    ni, nk = s // tm, IN_PAD // tk
    n = len(pairs)

    def body(*refs):
        dp_ref, w_ref, x_ref, g_ref, dy_ref = refs[:5]
        ins, gx_out, dg_out, outs = refs[5:5 + n], refs[5 + n], refs[6 + n], refs[7 + n:7 + 2 * n]
        acc, sems = refs[7 + 2 * n], refs[8 + 2 * n]
        i, kk = pl.program_id(0), pl.program_id(1)

        @pl.when(jnp.logical_and(i == 0, kk == 0))
        def _():
            dg_out[...] = jnp.zeros_like(dg_out)
            for cp in _chip_copies(ins, outs, sems, None) + _chip_copies(ins, outs, sems, True):
                cp.start()

        @pl.when(kk == 0)
        def _():
            acc[...] = jnp.zeros_like(acc)

        acc[...] += _dot_nt(dp_ref[...], w_ref[...])

        @pl.when(kk == nk - 1)
        def _():
            g = g_ref[...]
            _, xh, r = _rms(x_ref[...], g)
            dh = acc[...]
            dg_out[...] += jnp.sum(dh * xh, axis=0, keepdims=True)
            gx_out[...] = dy_ref[...] + _rms_bwd(dh, g, xh, r)

        @pl.when(jnp.logical_and(i == ni - 1, kk == nk - 1))
        def _():
            for cp in _chip_copies(ins, outs, sems, False):
                cp.wait_recv()
            for cp in _chip_copies(ins, outs, sems, True):
                cp.wait_send()
            for cp in _chip_copies(ins, outs, sems, None):
                cp.wait()

    row = lambda i, k: (i, 0)
    hbm = pl.BlockSpec(memory_space=pl.ANY)
    res = pl.pallas_call(
        body, name="dh_bwd", grid=(ni, nk),
        in_specs=[_full((tm, tk), lambda i, k: (i, k)), _full((1024, tk), lambda i, k: (0, k)),
                  _full((tm, 1024), row), _full((1, 1024), lambda i, k: (0, 0)), _full((tm, 1024), row)] + [hbm] * n,
        out_specs=[_full((tm, 1024), row), _full((1, 1024), lambda i, k: (0, 0))] + [hbm] * n,
        out_shape=[jax.ShapeDtypeStruct((s, 1024), F32), jax.ShapeDtypeStruct((1, 1024), F32)]
        + [jax.ShapeDtypeStruct(a.shape, a.dtype) for a in pairs],
        scratch_shapes=[pltpu.VMEM((tm, 1024), F32), pltpu.SemaphoreType.DMA((2 * n, 4))],
        compiler_params=_cparams(("arbitrary", "arbitrary")),
    )(dproj, w_pad, x, gpre, dy, *pairs)
    return res[0], res[1], res[2:]


def _dw_in(h, dproj):
    s = h.shape[0]
    ts, tn = min(1024, s), 768
    ns = s // ts

    def body(h_ref, dp_ref, o_ref, acc):
        k = pl.program_id(1)

        @pl.when(k == 0)
        def _():
            acc[...] = jnp.zeros_like(acc)

        acc[...] += _dot_tn(h_ref[...], dp_ref[...])

        @pl.when(k == ns - 1)
        def _():
            o_ref[...] = acc[...].astype(BF16)

    return pl.pallas_call(
        body, name="dw_in", grid=(IN_PAD // tn, ns),
        in_specs=[_full((ts, 1024), lambda j, k: (k, 0)), _full((ts, tn), lambda j, k: (k, j))],
        out_specs=_full((1024, tn), lambda j, k: (0, j)),
        out_shape=jax.ShapeDtypeStruct((1024, IN_PAD), BF16),
        scratch_shapes=[pltpu.VMEM((1024, tn), F32)],
        compiler_params=_cparams(("parallel", "arbitrary")),
    )(h, dproj)


def _remote(src, dst, sems, row, k, to):
    return pltpu.make_async_remote_copy(src_ref=src, dst_ref=dst, send_sem=sems.at[row, k], recv_sem=sems.at[row + 1, k],
                                        device_id=to, device_id_type=pl.DeviceIdType.MESH)


def _place():
    x, y, c = lax.axis_index("x"), lax.axis_index("y"), lax.axis_index("c")
    return x, y, c, [(1 - x, y), (x, 1 - y), (1 - x, 1 - y)]


def _gather_weights(arrays):
    n = len(arrays)

    def body(*refs):
        ins, outs, sems = refs[:n], refs[n:2 * n], refs[2 * n]
        x, y, c, chips = _place()
        me, sib = (x, y, c), (x, y, 1 - c)
        idx = lambda p: 4 * p[0] + 2 * p[1] + p[2]

        def copy(a, k, block, to, from_input=False):
            src = ins[a] if from_input else outs[a].at[idx(block)]
            return _remote(src, outs[a].at[idx(block)], sems, 2 * a, k, to)

        own = [pltpu.make_async_copy(ins[a], outs[a].at[idx(me)], sems.at[2 * a, 7]) for a in range(n)]
        first = [copy(a, 0, me, sib, True) for a in range(n)]
        first += [copy(a, 1 + j, me, (*chip, c), True) for j, chip in enumerate(chips) for a in range(n)]
        for cp in own + first:
            cp.start()
        passed = []
        for j, chip in enumerate(chips):
            for a in range(n):
                copy(a, 1 + j, (*chip, c), me).wait_recv()
                passed.append(copy(a, 4 + j, (*chip, c), sib))
                passed[-1].start()
        for a in range(n):
            copy(a, 0, sib, me).wait_recv()
        for j, chip in enumerate(chips):
            for a in range(n):
                copy(a, 4 + j, (*chip, 1 - c), me).wait_recv()
        for cp in first + passed:
            cp.wait_send()
        for cp in own:
            cp.wait()

    hbm = pl.BlockSpec(memory_space=pl.ANY)
    return pl.pallas_call(
        body, name="gather_weights", in_specs=[hbm] * n, out_specs=[hbm] * n,
        out_shape=[jax.ShapeDtypeStruct((N_DEV,) + a.shape, a.dtype) for a in arrays],
        scratch_shapes=[pltpu.SemaphoreType.DMA((2 * n, N_DEV))],
    )(*arrays)


def _pair_exchange(chunks):
    n = len(chunks)

    def body(*refs):
        ins, outs, sems = refs[:n], refs[n:2 * n], refs[2 * n]
        x, y, c, _ = _place()
        sent = [_remote(ins[a].at[2 * q + (1 - c)], outs[a].at[q], sems, 2 * a, q, (x, y, 1 - c))
                for a in range(n) for q in range(4)]
        for cp in sent:
            cp.start()
        for cp in sent:
            cp.wait_recv()
        for cp in sent:
            cp.wait_send()

    hbm = pl.BlockSpec(memory_space=pl.ANY)
    return pl.pallas_call(
        body, name="pair_exchange", in_specs=[hbm] * n, out_specs=[hbm] * n,
        out_shape=[jax.ShapeDtypeStruct((4,) + a.shape[1:], a.dtype) for a in chunks],
        scratch_shapes=[pltpu.SemaphoreType.DMA((2 * n, 4))],
    )(*chunks)


def _pair_sum(core, chunks, recv, name, tr):
    _, rows, cols = chunks.shape

    def body(c_ref, a_ref, b_ref, o_ref):
        o_ref[...] = (a_ref[...].astype(F32) + b_ref[...].astype(F32)).astype(BF16)

    blk = lambda f: _full((1, tr, cols), f)
    return pl.pallas_call(
        body, name=name, out_shape=jax.ShapeDtypeStruct((4, rows, cols), BF16),
        grid_spec=pltpu.PrefetchScalarGridSpec(
            num_scalar_prefetch=1, grid=(4, rows // tr),
            in_specs=[blk(lambda q, i, c: (2 * q + c[0], i, 0)), blk(lambda q, i, c: (q, i, 0))],
            out_specs=blk(lambda q, i, c: (q, i, 0))),
        compiler_params=_cparams(("parallel", "parallel")),
    )(core, chunks, recv)


def _packet_exchange(packet):
    def body(pk, pk_out, sems):
        x, y, c, _ = _place()
        me = 4 * x + 2 * y + c
        flip = lambda v, b: (1 - v) if b else v

        def small(j, outgoing):
            peer = (flip(x, (j >> 2) & 1), flip(y, (j >> 1) & 1), flip(c, j & 1))
            slot = me if outgoing else 4 * peer[0] + 2 * peer[1] + peer[2]
            return _remote(pk, pk_out.at[slot], sems, 0, j, peer)

        own = pltpu.make_async_copy(pk, pk_out.at[me], sems.at[0, 0])
        sent = [small(j, True) for j in range(1, N_DEV)]
        for cp in [own] + sent:
            cp.start()
        for j in range(1, N_DEV):
            small(j, False).wait_recv()
        for cp in sent:
            cp.wait_send()
        own.wait()

    hbm = pl.BlockSpec(memory_space=pl.ANY)
    return pl.pallas_call(
        body, name="packet_exchange", in_specs=[hbm], out_specs=hbm,
        out_shape=jax.ShapeDtypeStruct((N_DEV,) + packet.shape, packet.dtype),
        scratch_shapes=[pltpu.SemaphoreType.DMA((2, N_DEV))],
    )(packet)


def _adam_math(w, g, m, v):
    m = ADAM_B1 * m + (1.0 - ADAM_B1) * g
    v = ADAM_B2 * v + (1.0 - ADAM_B2) * (g * g)
    m_hat = m / (1.0 - ADAM_B1 ** ADAM_STEP)
    v_hat = v / (1.0 - ADAM_B2 ** ADAM_STEP)
    delta = -ADAM_LR * (m_hat / (jnp.sqrt(v_hat) + ADAM_EPS) + ADAM_WD * w)
    return delta, m, v


def _adam(recv, w, m, v, name, tr):
    _, rows, cols = w.shape

    def body(r_ref, w_ref, m_ref, v_ref, g_out, d_out, m_out, v_out):
        g = r_ref[0].astype(F32)
        for k in range(1, 4):
            g = g + r_ref[k].astype(F32)
        g_out[0] = g
        d_out[0], m_out[0], v_out[0] = _adam_math(w_ref[0], g, m_ref[0], v_ref[0])

    blk = _full((1, tr, cols), lambda i: (0, i, 0))
    return pl.pallas_call(
        body, name=name, grid=(rows // tr,),
        in_specs=[_full((4, tr, cols), lambda i: (0, i, 0)), blk, blk, blk],
        out_specs=[blk] * 4,
        out_shape=[jax.ShapeDtypeStruct(w.shape, F32)] * 4,
        compiler_params=_cparams(("parallel",)),
    )(recv, w, m, v)


def _adam_gains(recv, gains, gains_m, gains_v):
    def body(*refs):
        r_ref, w, m, v = refs[0], refs[1:5], refs[5:9], refs[9:13]
        g_out, d_out, m_out, v_out, loss_out = refs[13:17], refs[17:21], refs[21:25], refs[25:29], refs[29]
        tot = r_ref[0:1, :]
        for k in range(1, N_DEV):
            tot = tot + r_ref[k:k + 1, :]
        for t in range(4):
            g = tot[:, GAIN_OFFS[t]:GAIN_OFFS[t] + GAIN_WIDTHS[t]]
            g_out[t][...] = g
            d_out[t][...], m_out[t][...], v_out[t][...] = _adam_math(w[t][...], g, m[t][...], v[t][...])
        loss_out[...] = tot[:, LOSS_OFF:LOSS_OFF + LANES]

    shapes = [jax.ShapeDtypeStruct((1, n), F32) for n in GAIN_WIDTHS]
    return pl.pallas_call(
        body, name="adam_gains", out_shape=shapes * 4 + [jax.ShapeDtypeStruct((1, LANES), F32)],
    )(recv, *gains, *gains_m, *gains_v)


def _local_step(x, positions, gains, weights, target):
    gpre, gq, gkv, gpost = gains
    w_pad, w_uq, w_uk, w_uv, wp_mla, wp_dil, w_out = weights
    q_tab, k_tab, d_tab = _rope_tables(positions)

    proj, h = _inproj(x, gpre, w_pad, d_tab)
    q, k, v, qt, kt, vt = _mla_prep(proj, gq, gkv, w_uq, w_uk, w_uv, q_tab, k_tab)
    o_mla, ot_mla, lse_mla = _mla_fwd(q, k, vt)

    qkv_perm, od, lsed = [], [], []
    for g, d in enumerate(DIL_DILATIONS):
        if d == 1:
            arr, offs = proj, ((Q_OFF + 512 * g) // 128, (K_OFF + 512 * g) // 128, (V_OFF + 512 * g) // 128)
        else:
            cols = jnp.concatenate([proj[:, o + 512 * g:o + 512 * (g + 1)] for o in (Q_OFF, K_OFF, V_OFF)], axis=1)
            arr, offs = _perm(cols, d), (0, 4, 8)
        qkv_perm.append((arr, offs))
        o_g, lse_g = _dil_fwd(arr, arr, arr, *offs, d, "dil_fwd_%d" % g)
        od.append(_unperm(o_g, d))
        lsed.append(_unperm(lse_g, d))

    merged, ya, yd, o_dil, lse_dil = _merge_fwd(proj, o_mla, od, lsed, wp_mla, wp_dil)
    do, dy, loss, dgpost = _out_loss(merged, w_out, x, target, gpost)

    (dz_mla, dz_dil, dg_mla, dg_dil, do_mla, do_dil, dot_mla, dd_dil, dw_out, dwp_mla, dwp_dil) = _merge_bwd(
        do, w_out, merged, proj, ya, yd, o_mla, o_dil, wp_mla, wp_dil)

    dq, dk, dv = _mla_bwd(q, qt, k, kt, v, do_mla, dot_mla, ot_mla, lse_mla)
    dcq, dkr, dckv, dw_uq, dw_uk, dw_uv, dgq, dgkv = _mla_prep_bwd(dq, dk, dv, proj, gq, gkv, w_uq, w_uk, w_uv,
                                                                   q_tab, k_tab)

    dqs, dks, dvs = [], [], []
    for g, d in enumerate(DIL_DILATIONS):
        arr, offs = qkv_perm[g]
        tabs = tuple(_perm(t, d) for t in d_tab)
        dq_g, dk_g, dv_g = _dil_bwd(arr, arr, arr, *offs, _perm(do_dil, d), _perm(lse_dil, d), _perm(dd_dil, d),
                                    tabs, d, "dil_bwd_%d" % g)
        dqs.append(_unperm(dq_g, d))
        dks.append(_unperm(dk_g, d))
        dvs.append(_unperm(dv_g, d))

    dproj = jnp.concatenate([dz_mla, dz_dil, dg_mla, dg_dil] + dqs + dks + dvs + [dcq, dkr, dckv], axis=1)
    dw_in = _dw_in(h, dproj)
    return loss, (dproj, dy), (dgq, dgkv, dgpost), (dw_in, dw_uq, dw_uk, dw_uv, dwp_mla, dwp_dil, dw_out)


ADAM_ROWS = (256, 384, 256, 512, 512, 128)
PAIR_ROWS = (512, 384, 256, 512, 512, 128)


def kernel(x, positions, pre_norm_g, w_in, q_norm_g, w_uq, kv_norm_g, w_ukv, w_proj_mla, w_proj_dil, w_out, post_norm_g, loss_target, m_pre_norm_g, m_w_in, m_q_norm_g, m_w_uq, m_kv_norm_g, m_w_ukv, m_w_proj_mla, m_w_proj_dil, m_w_out, m_post_norm_g, v_pre_norm_g, v_w_in, v_q_norm_g, v_w_uq, v_kv_norm_g, v_w_ukv, v_w_proj_mla, v_w_proj_dil, v_w_out, v_post_norm_g):
    big_w = (w_in, w_uq, w_ukv, w_proj_mla, w_proj_dil, w_out)
    big_m = (m_w_in, m_w_uq, m_w_ukv, m_w_proj_mla, m_w_proj_dil, m_w_out)
    big_v = (v_w_in, v_w_uq, v_w_ukv, v_w_proj_mla, v_w_proj_dil, v_w_out)
    gains = (pre_norm_g, q_norm_g, kv_norm_g, post_norm_g)
    gains_m = (m_pre_norm_g, m_q_norm_g, m_kv_norm_g, m_post_norm_g)
    gains_v = (v_pre_norm_g, v_q_norm_g, v_kv_norm_g, v_post_norm_g)

    gathered = _gather_weights([w[0].astype(BF16) for w in big_w])
    weights = _assemble_weights(*gathered)

    loss, (dproj, dy), (dgq, dgkv, dgpost), dweights = _local_step(x[0], positions[0], gains, weights,
                                                                   loss_target[0])

    chunks = _grad_chunks(*dweights)
    from_sibling = _pair_exchange(chunks)
    core = lax.axis_index("c").astype(jnp.int32).reshape(1)
    pairs = [_pair_sum(core, chunks[t], from_sibling[t], "pair_sum_%d" % t, PAIR_ROWS[t]) for t in range(6)]
    grad_x, dgpre, received = _dh_bwd(dproj, weights[0], x[0], pre_norm_g, dy, pairs)
    packet = _packet_exchange(jnp.concatenate([dgpre, dgq, dgkv, dgpost, loss[0:1]], axis=1))

    big = [_adam(received[t], big_w[t], big_m[t], big_v[t], "adam_%d" % t, ADAM_ROWS[t]) for t in range(6)]
    small = _adam_gains(packet.reshape(N_DEV, PACKET), gains, gains_m, gains_v)

    def interleave(kind):
        s_pre, s_q, s_kv, s_post = small[4 * kind:4 * kind + 4]
        b_in, b_uq, b_ukv, b_pm, b_pd, b_out = (big[t][kind] for t in range(6))
        return [s_pre, b_in, s_q, b_uq, s_kv, b_ukv, b_pm, b_pd, b_out, s_post]

    return (small[16][0, 0], grad_x[None], *interleave(0), *interleave(1), *interleave(2), *interleave(3))
```

```python
import numpy as np
import jax
import jax.numpy as jnp
from jax import lax
from jax.experimental import pallas as pl
from jax.experimental.pallas import tpu as pltpu

F32 = jnp.float32
BF16 = jnp.bfloat16

D_MODEL = 1024
NORM_EPS = 1e-6
ROPE_THETA = 500000.0
N_DEV = 8
LANES = 128
NEG = -1e30

MLA_HEADS = 8
MLA_Q_RANK = 384
MLA_KV_RANK = 256
MLA_SCALE = 96.0 ** -0.5
LOG2E = 1.4426950408889634
MLA_QSCALE = MLA_SCALE * LOG2E
MLA_FWD_Q_PER_K = 2
MLA_BWD_Q_PER_K = 2
DIL_DILATIONS = (1, 4, 16)
DIL_SCALE = 0.125
Q_BLOCK = 128
DIL_BLOCKS_PER_STEP = 4

Z_MLA, Z_DIL, G_MLA, G_DIL = 0, 512, 1024, 2048
Q_OFF, K_OFF, V_OFF = 3072, 4608, 6144
CQ_OFF, KR_OFF, CKV_OFF, IN_PAD = 7680, 8064, 8192, 8448
IN_WIDTH = 8352
SHARD_W = IN_WIDTH // 8
IN_SEGS = ((0, 384, CQ_OFF), (384, 256, CKV_OFF), (640, 32, KR_OFF), (672, 1536, Q_OFF), (2208, 1536, K_OFF),
           (3744, 1536, V_OFF), (5280, 512, Z_MLA), (5792, 512, Z_DIL), (6304, 1024, G_MLA), (7328, 1024, G_DIL))

GAIN_OFFS = (0, 1024, 1408, 1664)
GAIN_WIDTHS = (1024, 384, 256, 1024)
LOSS_OFF, PACKET = 2688, 2816

ADAM_LR, ADAM_B1, ADAM_B2, ADAM_EPS, ADAM_WD, ADAM_STEP = 0.001, 0.9, 0.999, 1e-08, 0.01, 10

VMEM_LIMIT_MB = 56


def _cparams(sem=None, vmem_mb=VMEM_LIMIT_MB):
    return pltpu.CompilerParams(dimension_semantics=sem, vmem_limit_bytes=vmem_mb * 1024 * 1024)


def _dot(a, b):
    return jnp.dot(a, b, preferred_element_type=F32)


def _dot_nt(a, b):
    return lax.dot_general(a, b, (((1,), (1,)), ((), ())), preferred_element_type=F32)


def _dot_tn(a, b):
    return lax.dot_general(a, b, (((0,), (0,)), ((), ())), preferred_element_type=F32)


def _tile_lanes(t, width):
    return t if width == t.shape[1] else jnp.tile(t, (1, width // t.shape[1]))


def _rope(x, c, sp, sm, a):
    n = x.shape[1]
    return x * c + pltpu.roll(x, a, 1) * sp + pltpu.roll(x, n - a, 1) * sm


def _unrope(dy, c, sp, sm, a):
    n = dy.shape[1]
    return dy * c + pltpu.roll(dy * sp, n - a, 1) + pltpu.roll(dy * sm, a, 1)


def _sigmoid(z):
    return 1.0 / (1.0 + jnp.exp(-z))


def _left_mask():
    return lax.broadcasted_iota(jnp.int32, (1, LANES), 1) < 64


def _expand_half(x, hh, left):
    r = pltpu.roll(x, 64, 1)
    return jnp.where(left, x, r) if hh == 0 else jnp.where(left, r, x)


def _rms(xv, g):
    r = lax.rsqrt(jnp.mean(xv * xv, axis=-1, keepdims=True) + NORM_EPS)
    xh = xv * r
    return xh * g, xh, r


def _rms_bwd(dout, g, xh, r):
    dxh = dout * g
    return r * (dxh - xh * jnp.mean(dxh * xh, axis=-1, keepdims=True))


def _full(shape, index_map):
    return pl.BlockSpec(shape, index_map)


def _w_in_pieces():
    out = []
    for s, n, off in sorted(IN_SEGS, key=lambda t: t[2]):
        c = s
        while c < s + n:
            k = c // SHARD_W
            e = min(s + n, (k + 1) * SHARD_W)
            out.append((k, c - k * SHARD_W, e - c, off + (c - s)))
            c = e
    return out


def _assemble_w_in(g):
    parts, cur = [], 0
    for k, a, w, off in _w_in_pieces():
        if off > cur:
            parts.append(jnp.zeros((D_MODEL, off - cur), g.dtype))
        parts.append(g[k, :, a:a + w])
        cur = off + w
    if cur < IN_PAD:
        parts.append(jnp.zeros((D_MODEL, IN_PAD - cur), g.dtype))
    return jnp.concatenate(parts, axis=1)


def _dw_in_chunks(dw):
    chunks = []
    for dev in range(N_DEV):
        mine = sorted((p for p in _w_in_pieces() if p[0] == dev), key=lambda p: p[1])
        chunks.append(jnp.concatenate([dw[:, off:off + w] for k, a, w, off in mine], axis=1))
    return jnp.stack(chunks)


def _assemble_weights(g_in, g_uq, g_ukv, g_pm, g_pd, g_out):
    w_uq_pad = jnp.pad(g_uq.transpose(1, 0, 2), ((0, 0), (0, 0), (0, 32))).reshape(384, 1024)
    ukv = g_ukv.transpose(1, 0, 2)
    w_uk_pad = jnp.pad(ukv[:, :, :64], ((0, 0), (0, 0), (0, 64))).reshape(256, 1024)
    w_uv = ukv[:, :, 64:].reshape(256, 512)
    wp_mla = g_pm.transpose(1, 0, 2).reshape(512, 1024)
    wp_dil = g_pd.transpose(1, 0, 2).reshape(512, 1024)
    return _assemble_w_in(g_in), w_uq_pad, w_uk_pad, w_uv, wp_mla, wp_dil, g_out.reshape(1024, 1024)


def _grad_chunks(dw_in_pad, dw_uq_pad, dw_uk_pad, dw_uv, dwp_mla, dwp_dil, dw_out):
    a = _dw_in_chunks(dw_in_pad)
    b = dw_uq_pad.reshape(384, 8, 128)[:, :, :96].transpose(1, 0, 2)
    c = jnp.concatenate([dw_uk_pad.reshape(256, 8, 128)[:, :, :64], dw_uv.reshape(256, 8, 64)], axis=2)
    c = c.transpose(1, 0, 2)
    d = dwp_mla.reshape(512, N_DEV, 128).transpose(1, 0, 2)
    e = dwp_dil.reshape(512, N_DEV, 128).transpose(1, 0, 2)
    f = dw_out.reshape(N_DEV, 128, 1024)
    return [t.astype(BF16) for t in (a, b, c, d, e, f)]


def _lane_consts(freqs, half, first, period):
    rel = (np.arange(LANES) % period) - first
    rot = (rel >= 0) & (rel < 2 * half)
    freq = np.where(rot, freqs[np.clip(rel, 0, 2 * half - 1) % half], 0.0).astype(np.float32)
    x1 = (rot & (rel < half)).astype(np.float32)
    x2 = (rot & (rel >= half)).astype(np.float32)
    return freq[None, :], x1[None, :], x2[None, :]


def _rope_tables(pos):
    p = pos.astype(F32)[:, None]
    inv_m = np.float32(ROPE_THETA) ** (-(np.arange(0, 32, 2, dtype=np.float32) / np.float32(32)))
    inv_d = np.float32(ROPE_THETA) ** (-(np.arange(0, 16, 2, dtype=np.float32) / np.float32(16)))
    lane = np.arange(LANES)
    tabs = []
    for freqs, half, first, period, keep in ((inv_m, 16, 64, 128, lane < 96), (inv_m, 16, 0, 128, lane < 32),
                                              (inv_d, 8, 0, 64, lane >= 0)):
        freq, x1, x2 = _lane_consts(freqs, half, first, period)
        ang = p * freq
        sin = jnp.sin(ang)
        tabs.append((jnp.cos(ang) * keep.astype(np.float32)[None, :], sin * x2, sin * (-x1)))
    return tuple(tabs)


def _perm(a, d):
    if d == 1:
        return a
    s, c = a.shape
    return a.reshape(s // d, d, c).transpose(1, 0, 2).reshape(s, c)


def _unperm(a, d):
    if d == 1:
        return a
    s, c = a.shape
    return a.reshape(d, s // d, c).transpose(1, 0, 2).reshape(s, c)


def _inproj(x, gpre, w_pad, d_tab):
    s = x.shape[0]
    tm, tn = min(1024, s), 768
    rope_lo, rope_hi = Q_OFF // tn, V_OFF // tn

    def body(x_ref, g_ref, w_ref, c_ref, sp_ref, sm_ref, o_ref, h_ref):
        j = pl.program_id(1)

        @pl.when(j == 0)
        def _():
            hv, _, _ = _rms(x_ref[...], g_ref[...])
            h_ref[...] = hv.astype(BF16)

        acc = _dot(h_ref[...], w_ref[...])
        is_rope = jnp.logical_and(j >= rope_lo, j < rope_hi)

        @pl.when(is_rope)
        def _():
            o_ref[...] = _rope(acc, _tile_lanes(c_ref[...], tn), _tile_lanes(sp_ref[...], tn),
                               _tile_lanes(sm_ref[...], tn), 8).astype(BF16)

        @pl.when(jnp.logical_not(is_rope))
        def _():
            o_ref[...] = acc.astype(BF16)

    row = lambda i, j: (i, 0)
    return pl.pallas_call(
        body, name="inproj", grid=(s // tm, IN_PAD // tn),
        in_specs=[_full((tm, D_MODEL), row), _full((1, D_MODEL), lambda i, j: (0, 0)),
                  _full((D_MODEL, tn), lambda i, j: (0, j)),
                  _full((tm, LANES), row), _full((tm, LANES), row), _full((tm, LANES), row)],
        out_specs=[_full((tm, tn), lambda i, j: (i, j)), _full((tm, D_MODEL), row)],
        out_shape=[jax.ShapeDtypeStruct((s, IN_PAD), BF16), jax.ShapeDtypeStruct((s, D_MODEL), BF16)],
        compiler_params=_cparams(("parallel", "arbitrary")),
    )(x, gpre, w_pad, *d_tab)


def _mla_prep(proj, gq, gkv, w_uq, w_uk, w_uv, q_tab, k_tab):
    s = proj.shape[0]
    tm = min(512, s)

    def body(cq_ref, kr_ref, ckv_ref, gq_ref, gkv_ref, wq_ref, wk_ref, wv_ref,
             qc, qsp, qsm, kc, ksp, ksm, q_out, k_out, v_out, qt_out, kt_out, vt_out):
        cqn, _, _ = _rms(cq_ref[...].astype(F32), gq_ref[...])
        q = _dot(cqn.astype(BF16), wq_ref[...])
        q = _rope(q, _tile_lanes(qc[...], 1024), _tile_lanes(qsp[...], 1024), _tile_lanes(qsm[...], 1024), 16)
        q = q * MLA_QSCALE
        q_out[...] = q.astype(BF16)
        qt_out[...] = q.T.astype(BF16)
        ckvn, _, _ = _rms(ckv_ref[...].astype(F32), gkv_ref[...])
        ckvn = ckvn.astype(BF16)
        kr = _rope(kr_ref[...].astype(F32), kc[...], ksp[...], ksm[...], 16)
        k = _dot(ckvn, wk_ref[...]) + _tile_lanes(pltpu.roll(kr, 64, 1), 1024)
        k_out[...] = k.astype(BF16)
        kt_out[...] = k.T.astype(BF16)
        v = _dot(ckvn, wv_ref[...])
        v_out[...] = v.astype(BF16)
        vt_out[...] = v.T.astype(BF16)

    row = lambda i: (i, 0)
    col = lambda i: (0, i)
    cst = lambda i: (0, 0)
    tabs = [_full((tm, LANES), row)] * 6
    return pl.pallas_call(
        body, name="mla_prep", grid=(s // tm,),
        in_specs=[_full((tm, 384), lambda i: (i, CQ_OFF // 384)), _full((tm, 128), lambda i: (i, KR_OFF // 128)),
                  _full((tm, 256), lambda i: (i, CKV_OFF // 256)), _full((1, 384), cst), _full((1, 256), cst),
                  _full((384, 1024), cst), _full((256, 1024), cst), _full((256, 512), cst)] + tabs,
        out_specs=[_full((tm, 1024), row), _full((tm, 1024), row), _full((tm, 512), row),
                   _full((1024, tm), col), _full((1024, tm), col), _full((512, tm), col)],
        out_shape=[jax.ShapeDtypeStruct((s, 1024), BF16), jax.ShapeDtypeStruct((s, 1024), BF16),
                   jax.ShapeDtypeStruct((s, 512), BF16), jax.ShapeDtypeStruct((1024, s), BF16),
                   jax.ShapeDtypeStruct((1024, s), BF16), jax.ShapeDtypeStruct((512, s), BF16)],
        compiler_params=_cparams(("parallel",)),
    )(proj, proj, proj, gq, gkv, w_uq, w_uk, w_uv, *q_tab, *k_tab)


def _mla_fwd(q, k, vt):
    s = q.shape[0]
    tk = min(256, s)
    ratio = MLA_FWD_Q_PER_K if s >= MLA_FWD_Q_PER_K * tk else 1
    tq = ratio * tk
    nq = s // tq

    def body(q_ref, k_ref, vt_ref, o_ref, ot_ref, lse_ref):
        krow = lax.broadcasted_iota(jnp.int32, (tk, tq), 0)
        qcol = lax.broadcasted_iota(jnp.int32, (tk, tq), 1)

        def q_step(i, _):
            r0 = pl.multiple_of(i * tq, tq)
            qs = [q_ref[pl.ds(r0, tq), hh * 128:(hh + 1) * 128] for hh in range(2)]

            def scores(j):
                c0 = pl.multiple_of(j * tk, tk)
                return tuple(_dot_nt(k_ref[pl.ds(c0, tk), hh * 128:(hh + 1) * 128], qs[hh])
                             for hh in range(2))

            def update(j, sts, stats, masked):
                c0 = pl.multiple_of(j * tk, tk)
                new = []
                causal = (krow + (c0 - r0)) <= qcol
                for hh in range(2):
                    m, l, acc = stats[hh]
                    st = jnp.where(causal, sts[hh], NEG) if masked else sts[hh]
                    m_new = jnp.maximum(m, jnp.max(st, axis=0, keepdims=True))
                    alpha = jnp.exp2(m - m_new)
                    p = jnp.exp2(st - m_new)
                    l = alpha * l + jnp.sum(p, axis=0, keepdims=True)
                    acc = acc * alpha + _dot(vt_ref[hh * 64:(hh + 1) * 64, pl.ds(c0, tk)], p.astype(BF16))
                    new.append((m_new, l, acc))
                return tuple(new)

            def kv_step(j, carry):
                sts, stats = carry
                nxt = scores(j + 1)
                return nxt, update(j, sts, stats, False)

            init = tuple((jnp.full((1, tq), NEG, F32), jnp.zeros((1, tq), F32), jnp.zeros((64, tq), F32))
                         for _ in range(2))
            sts, stats = lax.fori_loop(0, ratio * i, kv_step, (scores(0), init))
            for d in range(ratio):
                if d > 0:
                    sts = scores(ratio * i + d)
                stats = update(ratio * i + d, sts, stats, True)
            (ma, la, acca), (mb, lb, accb) = stats
            ot = jnp.concatenate([acca / la, accb / lb], axis=0)
            ot_ref[:, pl.ds(r0, tq)] = ot.astype(BF16)
            o_ref[pl.ds(r0, tq), :] = ot.T.astype(BF16)
            lse_ref[:, pl.ds(r0, tq)] = jnp.concatenate(
                [ma + jnp.log2(la), mb + jnp.log2(lb), jnp.zeros((6, tq), F32)], axis=0)
            return 0

        lax.fori_loop(0, nq, q_step, 0)

    return pl.pallas_call(
        body, name="mla_fwd", grid=(4,),
        in_specs=[_full((s, 256), lambda p: (0, p)), _full((s, 256), lambda p: (0, p)),
                  _full((128, s), lambda p: (p, 0))],
        out_specs=[_full((s, 128), lambda p: (0, p)), _full((128, s), lambda p: (p, 0)),
                   _full((8, s), lambda p: (p, 0))],
        out_shape=[jax.ShapeDtypeStruct((s, 512), BF16), jax.ShapeDtypeStruct((512, s), BF16),
                   jax.ShapeDtypeStruct((32, s), F32)],
        compiler_params=_cparams(("parallel",)),
    )(q, k, vt)


def _band_masks(has_prev):
    r = lax.broadcasted_iota(jnp.int32, (Q_BLOCK, Q_BLOCK), 0)
    c = lax.broadcasted_iota(jnp.int32, (Q_BLOCK, Q_BLOCK), 1)
    return c >= r + jnp.where(has_prev, 0, Q_BLOCK), c <= r


def _dil_fwd(qa, ka, va, qo, ko, vo, d, name):
    s = qa.shape[0]
    nblk = s // Q_BLOCK
    per_seq = nblk // d

    def body(q_ref, k_ref, v_ref, o_ref, lse_ref):
        left = _left_mask()

        hms = (left, jnp.logical_not(left))

        def scores(b):
            r0 = pl.multiple_of(b * Q_BLOCK, Q_BLOCK)
            p0 = pl.multiple_of(jnp.maximum(b - 1, 0) * Q_BLOCK, Q_BLOCK)
            qb = q_ref[pl.ds(r0, Q_BLOCK), :]
            kp, kc = k_ref[pl.ds(p0, Q_BLOCK), :], k_ref[pl.ds(r0, Q_BLOCK), :]
            out = []
            for hh in range(2):
                qm = jnp.where(hms[hh], qb, jnp.zeros_like(qb))
                out.append((_dot_nt(qm, kp) * DIL_SCALE, _dot_nt(qm, kc) * DIL_SCALE))
            return out

        def finish(b, tiles):
            r0 = pl.multiple_of(b * Q_BLOCK, Q_BLOCK)
            p0 = pl.multiple_of(jnp.maximum(b - 1, 0) * Q_BLOCK, Q_BLOCK)
            pmask, cur_ok = _band_masks((b % per_seq) > 0)
            vp, vc = v_ref[pl.ds(p0, Q_BLOCK), :], v_ref[pl.ds(r0, Q_BLOCK), :]
            zero = jnp.zeros_like(vp)
            outs = []
            for hh in range(2):
                sp = jnp.where(pmask, tiles[hh][0], NEG)
                sc = jnp.where(cur_ok, tiles[hh][1], NEG)
                m = jnp.maximum(jnp.max(sp, axis=1, keepdims=True), jnp.max(sc, axis=1, keepdims=True))
                pp, pc = jnp.exp(sp - m), jnp.exp(sc - m)
                l = jnp.sum(pp, axis=1, keepdims=True) + jnp.sum(pc, axis=1, keepdims=True)
                acc = (_dot(pp.astype(BF16), jnp.where(hms[hh], vp, zero))
                       + _dot(pc.astype(BF16), jnp.where(hms[hh], vc, zero)))
                outs.append((acc / l, jnp.broadcast_to(m + jnp.log(l), (Q_BLOCK, LANES))))
            o_ref[pl.ds(r0, Q_BLOCK), :] = (outs[0][0] + outs[1][0]).astype(BF16)
            lse_ref[pl.ds(r0, Q_BLOCK), :] = jnp.where(left, outs[0][1], outs[1][1])

        def step(t, _):
            tiles = [scores(DIL_BLOCKS_PER_STEP * t + u) for u in range(DIL_BLOCKS_PER_STEP)]
            for u in range(DIL_BLOCKS_PER_STEP):
                finish(DIL_BLOCKS_PER_STEP * t + u, tiles[u])
            return 0

        lax.fori_loop(0, nblk // DIL_BLOCKS_PER_STEP, step, 0)

    blk = lambda off: _full((s, 128), lambda p, off=off: (0, off + p))
    return pl.pallas_call(
        body, name=name, grid=(4,),
        in_specs=[blk(qo), blk(ko), blk(vo)],
        out_specs=[blk(0), blk(0)],
        out_shape=[jax.ShapeDtypeStruct((s, 512), BF16), jax.ShapeDtypeStruct((s, 512), F32)],
        compiler_params=_cparams(("parallel",)),
    )(qa, ka, va)


def _merge_fwd(proj, o_mla, od, lsed, wp_mla, wp_dil):
    s = proj.shape[0]
    tm = min(512, s)

    def body(zm_ref, zd_ref, gm_ref, gd_ref, om_ref, o0, o1, o2, l0, l1, l2, wm_ref, wd_ref,
             mg_out, ya_out, yd_out, odil_out, lse_out):
        la, lb, lc = l0[...], l1[...], l2[...]
        lmax = jnp.maximum(jnp.maximum(la, lb), lc)
        ea, eb, ec = jnp.exp(la - lmax), jnp.exp(lb - lmax), jnp.exp(lc - lmax)
        den = ea + eb + ec
        o_dil = (ea * o0[...].astype(F32) + eb * o1[...].astype(F32) + ec * o2[...].astype(F32)) / den
        o_dil = o_dil.astype(BF16)
        odil_out[...] = o_dil
        lse_out[...] = lmax + jnp.log(den)
        zm, zd = zm_ref[...].astype(F32), zd_ref[...].astype(F32)
        pa = (om_ref[...].astype(F32) * (zm * _sigmoid(zm))).astype(BF16)
        pd = (o_dil.astype(F32) * (zd * _sigmoid(zd))).astype(BF16)
        ya = _dot(pa, wm_ref[...])
        yd = _dot(pd, wd_ref[...])
        ya_out[...] = ya.astype(BF16)
        yd_out[...] = yd.astype(BF16)
        mg_out[...] = (_sigmoid(gm_ref[...].astype(F32)) * ya + _sigmoid(gd_ref[...].astype(F32)) * yd).astype(BF16)

    row = lambda i: (i, 0)
    cst = lambda i: (0, 0)
    r512 = _full((tm, 512), row)
    r1024 = _full((tm, 1024), row)
    return pl.pallas_call(
        body, name="merge_fwd", grid=(s // tm,),
        in_specs=[_full((tm, 512), lambda i: (i, Z_MLA // 512)), _full((tm, 512), lambda i: (i, Z_DIL // 512)),
                  _full((tm, 1024), lambda i: (i, G_MLA // 1024)), _full((tm, 1024), lambda i: (i, G_DIL // 1024)),
                  r512, r512, r512, r512, r512, r512, r512, _full((512, 1024), cst), _full((512, 1024), cst)],
        out_specs=[r1024, r1024, r1024, r512, r512],
        out_shape=[jax.ShapeDtypeStruct((s, 1024), BF16), jax.ShapeDtypeStruct((s, 1024), BF16),
                   jax.ShapeDtypeStruct((s, 1024), BF16), jax.ShapeDtypeStruct((s, 512), BF16),
                   jax.ShapeDtypeStruct((s, 512), F32)],
        compiler_params=_cparams(("parallel",)),
    )(proj, proj, proj, proj, o_mla, *od, *lsed, wp_mla, wp_dil)


def _out_loss(merged, w_out, x, target, gpost):
    s = x.shape[0]
    tm = min(512, s)

    def body(mg_ref, w_ref, x_ref, t_ref, g_ref, do_out, dy_out, loss_out, dg_out):
        i = pl.program_id(0)

        @pl.when(i == 0)
        def _():
            loss_out[...] = jnp.zeros_like(loss_out)
            dg_out[...] = jnp.zeros_like(dg_out)

        o = _dot(mg_ref[...], w_ref[...])
        g = g_ref[...]
        n, u, r = _rms(o, g)
        e = (x_ref[...] + n) - t_ref[...]
        loss_out[...] += 0.5 * jnp.sum(jnp.mean(e * e, axis=-1, keepdims=True))
        dy = e * (1.0 / D_MODEL)
        dy_out[...] = dy
        dg_out[...] += jnp.sum(dy * u, axis=0, keepdims=True)
        do_out[...] = _rms_bwd(dy, g, u, r).astype(BF16)

    row = lambda i: (i, 0)
    cst = lambda i: (0, 0)
    return pl.pallas_call(
        body, name="out_loss", grid=(s // tm,),
        in_specs=[_full((tm, 1024), row), _full((1024, 1024), cst), _full((tm, 1024), row), _full((tm, 1024), row),
                  _full((1, 1024), cst)],
        out_specs=[_full((tm, 1024), row), _full((tm, 1024), row), _full((8, LANES), cst), _full((1, 1024), cst)],
        out_shape=[jax.ShapeDtypeStruct((s, 1024), BF16), jax.ShapeDtypeStruct((s, 1024), F32),
                   jax.ShapeDtypeStruct((8, LANES), F32), jax.ShapeDtypeStruct((1, 1024), F32)],
        compiler_params=_cparams(("arbitrary",)),
    )(merged, w_out, x, target, gpost)


def _seg_sum64(x, ones_bd):
    hi = x.astype(BF16)
    lo = (x - hi.astype(F32)).astype(BF16)
    return _dot(hi, ones_bd) + _dot(lo, ones_bd)


def _merge_bwd(do, w_out, merged, proj, ya, yd, o_mla, o_dil, wp_mla, wp_dil):
    s = do.shape[0]
    tm = min(256, s)
    seg = jnp.arange(512) // 64
    ones_bd = (seg[:, None] == seg[None, :]).astype(BF16)

    def body(do_ref, wo_ref, mg_ref, zm_ref, zd_ref, gm_ref, gd_ref, ya_ref, yd_ref, om_ref, od_ref, wm_ref, wd_ref,
             bd_ref, dzm_out, dzd_out, dgm_out, dgd_out, dom_out, dod_out, domt_out, dd_out, dwo_out, dwm_out,
             dwd_out):
        i = pl.program_id(0)

        @pl.when(i == 0)
        def _():
            dwo_out[...] = jnp.zeros_like(dwo_out)
            dwm_out[...] = jnp.zeros_like(dwm_out)
            dwd_out[...] = jnp.zeros_like(dwd_out)

        dov = do_ref[...]
        dwo_out[...] += _dot_tn(mg_ref[...], dov)
        dm = _dot_nt(dov, wo_ref[...])
        for g_ref, y_ref, z_ref, o_ref, w_ref, dz_out, dg_out, dob_out, dd_o, dw_out in (
                (gm_ref, ya_ref, zm_ref, om_ref, wm_ref, dzm_out, dgm_out, dom_out, None, dwm_out),
                (gd_ref, yd_ref, zd_ref, od_ref, wd_ref, dzd_out, dgd_out, dod_out, dd_out, dwd_out)):
            sg = _sigmoid(g_ref[...].astype(F32))
            dg_out[...] = (dm * y_ref[...].astype(F32) * sg * (1.0 - sg)).astype(BF16)
            dy = (dm * sg).astype(BF16)
            z = z_ref[...].astype(F32)
            sz = _sigmoid(z)
            silu = z * sz
            ob = o_ref[...].astype(F32)
            dw_out[...] += _dot_tn((ob * silu).astype(BF16), dy)
            dp = _dot_nt(dy, w_ref[...])
            dz_out[...] = (dp * ob * (sz * (1.0 + z * (1.0 - sz)))).astype(BF16)
            dob = dp * silu
            dob_out[...] = dob.astype(BF16)
            if dd_o is None:
                domt_out[...] = dob.T.astype(BF16)
            else:
                dd_o[...] = _seg_sum64(dob * ob, bd_ref[...])

    row = lambda i: (i, 0)
    cst = lambda i: (0, 0)
    r512 = _full((tm, 512), row)
    r1024 = _full((tm, 1024), row)
    return pl.pallas_call(
        body, name="merge_bwd", grid=(s // tm,),
        in_specs=[r1024, _full((1024, 1024), cst), r1024,
                  _full((tm, 512), lambda i: (i, Z_MLA // 512)), _full((tm, 512), lambda i: (i, Z_DIL // 512)),
                  _full((tm, 1024), lambda i: (i, G_MLA // 1024)), _full((tm, 1024), lambda i: (i, G_DIL // 1024)),
                  r1024, r1024, r512, r512, _full((512, 1024), cst), _full((512, 1024), cst), _full((512, 512), cst)],
        out_specs=[r512, r512, r1024, r1024, r512, r512, _full((512, tm), lambda i: (0, i)), r512,
                   _full((1024, 1024), cst), _full((512, 1024), cst), _full((512, 1024), cst)],
        out_shape=[jax.ShapeDtypeStruct((s, 512), BF16), jax.ShapeDtypeStruct((s, 512), BF16),
                   jax.ShapeDtypeStruct((s, 1024), BF16), jax.ShapeDtypeStruct((s, 1024), BF16),
                   jax.ShapeDtypeStruct((s, 512), BF16), jax.ShapeDtypeStruct((s, 512), BF16),
                   jax.ShapeDtypeStruct((512, s), BF16), jax.ShapeDtypeStruct((s, 512), F32),
                   jax.ShapeDtypeStruct((1024, 1024), F32), jax.ShapeDtypeStruct((512, 1024), F32),
                   jax.ShapeDtypeStruct((512, 1024), F32)],
        compiler_params=_cparams(("arbitrary",)),
    )(do, w_out, merged, proj, proj, proj, proj, ya, yd, o_mla, o_dil, wp_mla, wp_dil, ones_bd)


def _mla_bwd(q, qt, k, kt, v, do, dot, ot, lse):
    s = q.shape[0]
    tk = min(256, s)
    ratio = MLA_BWD_Q_PER_K if s >= MLA_BWD_Q_PER_K * tk else 1
    tq = ratio * tk
    nq, nk = s // tq, s // tk

    def body(q_ref, qt_ref, k_ref, kt_ref, v_ref, do_ref, dot_ref, ot_ref, lse_ref, dqt_out, dkt_out, dvt_out,
             dqt_acc):
        left = _left_mask()
        krow = lax.broadcasted_iota(jnp.int32, (tk, tq), 0)
        qcol = lax.broadcasted_iota(jnp.int32, (tk, tq), 1)
        dqt_acc[...] = jnp.zeros_like(dqt_acc)

        def kv_step(j, _):
            c0 = pl.multiple_of(j * tk, tk)
            vv = v_ref[pl.ds(c0, tk), :]
            khs = [k_ref[pl.ds(c0, tk), hh * 128:(hh + 1) * 128] for hh in range(2)]
            kths = [kt_ref[hh * 128:(hh + 1) * 128, pl.ds(c0, tk)] for hh in range(2)]
            vms = [jnp.where(left if hh == 0 else jnp.logical_not(left), vv, jnp.zeros_like(vv)) for hh in range(2)]

            def scores(i):
                r0 = pl.multiple_of(jnp.minimum(i, nq - 1) * tq, tq)
                dov = do_ref[pl.ds(r0, tq), :]
                return tuple((_dot_nt(khs[hh], q_ref[pl.ds(r0, tq), hh * 128:(hh + 1) * 128]),
                              _dot_nt(vms[hh], dov)) for hh in range(2))

            def update(i, tiles, acc, masked):
                r0 = pl.multiple_of(i * tq, tq)
                new = []
                for hh in range(2):
                    dkt, dvt = acc[hh]
                    st, dp = tiles[hh]
                    hrows = slice(hh * 128, (hh + 1) * 128)
                    drows = slice(hh * 64, (hh + 1) * 64)
                    doth = dot_ref[drows, pl.ds(r0, tq)]
                    dd = jnp.sum(doth.astype(F32) * ot_ref[drows, pl.ds(r0, tq)].astype(F32), axis=0, keepdims=True)
                    p = jnp.exp2(st - lse_ref[hh:hh + 1, pl.ds(r0, tq)])
                    if masked:
                        p = jnp.where((krow + (c0 - r0)) <= qcol, p, 0.0)
                    ds = (p * (dp - dd)).astype(BF16)
                    dvt = dvt + _dot_nt(doth, p.astype(BF16))
                    dkt = dkt + _dot_nt(qt_ref[hrows, pl.ds(r0, tq)], ds)
                    dqt_acc[hrows, pl.ds(r0, tq)] += _dot(kths[hh], ds)
                    new.append((dkt, dvt))
                return tuple(new)

            init = tuple((jnp.zeros((128, tk), F32), jnp.zeros((64, tk), F32)) for _ in range(2))
            i0 = j // ratio
            acc = update(i0, scores(i0), init, True)
            acc = lax.fori_loop(i0 + 1, nq, lambda i, a: update(i, scores(i), a, False), acc)
            for hh in range(2):
                dkt_out[hh * 128:(hh + 1) * 128, pl.ds(c0, tk)] = (acc[hh][0] * (1.0 / LOG2E)).astype(BF16)
                dvt_out[hh * 64:(hh + 1) * 64, pl.ds(c0, tk)] = acc[hh][1].astype(BF16)
            return 0

        lax.fori_loop(0, nk, kv_step, 0)
        dqt_out[...] = (dqt_acc[...] * MLA_SCALE).astype(BF16)

    b256 = _full((s, 256), lambda p: (0, p))
    b128 = _full((s, 128), lambda p: (0, p))
    t256 = _full((256, s), lambda p: (p, 0))
    t128 = _full((128, s), lambda p: (p, 0))
    return pl.pallas_call(
        body, name="mla_bwd", grid=(4,),
        in_specs=[b256, t256, b256, t256, b128, b128, t128, t128, _full((8, s), lambda p: (p, 0))],
        out_specs=[t256, t256, t128],
        out_shape=[jax.ShapeDtypeStruct((1024, s), BF16), jax.ShapeDtypeStruct((1024, s), BF16),
                   jax.ShapeDtypeStruct((512, s), BF16)],
        scratch_shapes=[pltpu.VMEM((256, s), F32)],
        compiler_params=_cparams(("parallel",)),
    )(q, qt, k, kt, v, do, dot, ot, lse)


def _mla_prep_bwd(dq, dk, dv, proj, gq, gkv, w_uq, w_uk, w_uv, q_tab, k_tab):
    s = proj.shape[0]
    tm = min(256, s)

    def body(dqt_ref, dkt_ref, dvt_ref, cq_ref, ckv_ref, gq_ref, gkv_ref, wq_ref, wk_ref, wv_ref,
             qc, qsp, qsm, kc, ksp, ksm,
             dcq_out, dkr_out, dckv_out, dwq_out, dwk_out, dwv_out, dgq_out, dgkv_out):
        i = pl.program_id(0)

        @pl.when(i == 0)
        def _():
            for r in (dwq_out, dwk_out, dwv_out, dgq_out, dgkv_out):
                r[...] = jnp.zeros_like(r)

        dqu = _unrope(dqt_ref[...].astype(F32).T, _tile_lanes(qc[...], 1024), _tile_lanes(qsp[...], 1024),
                      _tile_lanes(qsm[...], 1024), 16).astype(BF16)
        gq = gq_ref[...]
        cqn, xh, r = _rms(cq_ref[...].astype(F32), gq)
        dwq_out[...] += _dot_tn(cqn.astype(BF16), dqu)
        dcqn = _dot_nt(dqu, wq_ref[...])
        dgq_out[...] += jnp.sum(dcqn * xh, axis=0, keepdims=True)
        dcq_out[...] = _rms_bwd(dcqn, gq, xh, r).astype(BF16)

        dkf = dkt_ref[...].astype(F32).T
        dkb = dkf.astype(BF16)
        dsum = dkf[:, 0:128]
        for h in range(1, MLA_HEADS):
            dsum = dsum + dkf[:, h * 128:(h + 1) * 128]
        dkr_out[...] = _unrope(pltpu.roll(dsum, 64, 1), kc[...], ksp[...], ksm[...], 16).astype(BF16)

        dvb = dvt_ref[...].astype(F32).T.astype(BF16)
        gkv = gkv_ref[...]
        ckvn, xh2, r2 = _rms(ckv_ref[...].astype(F32), gkv)
        ckvn = ckvn.astype(BF16)
        dwk_out[...] += _dot_tn(ckvn, dkb)
        dwv_out[...] += _dot_tn(ckvn, dvb)
        dckvn = _dot_nt(dkb, wk_ref[...]) + _dot_nt(dvb, wv_ref[...])
        dgkv_out[...] += jnp.sum(dckvn * xh2, axis=0, keepdims=True)
        dckv_out[...] = _rms_bwd(dckvn, gkv, xh2, r2).astype(BF16)

    row = lambda i: (i, 0)
    cst = lambda i: (0, 0)
    tabs = [_full((tm, LANES), row)] * 6
    return pl.pallas_call(
        body, name="mla_prep_bwd", grid=(s // tm,),
        in_specs=[_full((1024, tm), lambda i: (0, i)), _full((1024, tm), lambda i: (0, i)),
                  _full((512, tm), lambda i: (0, i)),
                  _full((tm, 384), lambda i: (i, CQ_OFF // 384)), _full((tm, 256), lambda i: (i, CKV_OFF // 256)),
                  _full((1, 384), cst), _full((1, 256), cst),
                  _full((384, 1024), cst), _full((256, 1024), cst), _full((256, 512), cst)] + tabs,
        out_specs=[_full((tm, 384), row), _full((tm, 128), row), _full((tm, 256), row),
                   _full((384, 1024), cst), _full((256, 1024), cst), _full((256, 512), cst),
                   _full((1, 384), cst), _full((1, 256), cst)],
        out_shape=[jax.ShapeDtypeStruct((s, 384), BF16), jax.ShapeDtypeStruct((s, 128), BF16),
                   jax.ShapeDtypeStruct((s, 256), BF16),
                   jax.ShapeDtypeStruct((384, 1024), F32), jax.ShapeDtypeStruct((256, 1024), F32),
                   jax.ShapeDtypeStruct((256, 512), F32),
                   jax.ShapeDtypeStruct((1, 384), F32), jax.ShapeDtypeStruct((1, 256), F32)],
        compiler_params=_cparams(("arbitrary",)),
    )(dq, dk, dv, proj, proj, gq, gkv, w_uq, w_uk, w_uv, *q_tab, *k_tab)


def _dil_bwd(qa, ka, va, qo, ko, vo, do, lse, dd, tabs, d, name):
    s = qa.shape[0]
    nblk = s // Q_BLOCK
    per_seq = nblk // d

    def body(q_ref, k_ref, v_ref, do_ref, lse_ref, dd_ref, c_ref, sp_ref, sm_ref, dq_out, dk_out, dv_out,
             dk_acc, dv_acc):
        left = _left_mask()
        dk_acc[...] = jnp.zeros_like(dk_acc)
        dv_acc[...] = jnp.zeros_like(dv_acc)

        hms = (left, jnp.logical_not(left))

        def rows_of(b):
            r0 = pl.multiple_of(b * Q_BLOCK, Q_BLOCK)
            p0 = pl.multiple_of(jnp.maximum(b - 1, 0) * Q_BLOCK, Q_BLOCK)
            return pl.ds(r0, Q_BLOCK), pl.ds(p0, Q_BLOCK)

        def scores(b):
            rows, prow = rows_of(b)
            qb, dob = q_ref[rows, :], do_ref[rows, :]
            kp, kc, vp, vc = k_ref[prow, :], k_ref[rows, :], v_ref[prow, :], v_ref[rows, :]
            zero = jnp.zeros_like(qb)
            out = []
            for hh in range(2):
                qm, dom = jnp.where(hms[hh], qb, zero), jnp.where(hms[hh], dob, zero)
                out.append((_dot_nt(qm, kp) * DIL_SCALE, _dot_nt(qm, kc) * DIL_SCALE,
                            _dot_nt(dom, vp), _dot_nt(dom, vc)))
            return out

        def finish(b, tiles):
            rows, prow = rows_of(b)
            pmask, cur_ok = _band_masks((b % per_seq) > 0)
            qb, dob = q_ref[rows, :], do_ref[rows, :]
            kp, kc = k_ref[prow, :], k_ref[rows, :]
            lse_b, dd_b = lse_ref[rows, :], dd_ref[rows, :]
            zero = jnp.zeros_like(qb)
            dq = jnp.zeros((Q_BLOCK, LANES), F32)
            dkp = jnp.zeros((Q_BLOCK, LANES), F32)
            dkc = jnp.zeros((Q_BLOCK, LANES), F32)
            dvp = jnp.zeros((Q_BLOCK, LANES), F32)
            dvc = jnp.zeros((Q_BLOCK, LANES), F32)
            for hh in range(2):
                hm = hms[hh]
                qm, dom = jnp.where(hm, qb, zero), jnp.where(hm, dob, zero)
                lse_h, dd_h = _expand_half(lse_b, hh, left), _expand_half(dd_b, hh, left)
                sp, sc, dpp, dpc = tiles[hh]
                pp = jnp.where(pmask, jnp.exp(sp - lse_h), 0.0)
                pc = jnp.where(cur_ok, jnp.exp(sc - lse_h), 0.0)
                dsp = (pp * (dpp - dd_h) * DIL_SCALE).astype(BF16)
                dsc = (pc * (dpc - dd_h) * DIL_SCALE).astype(BF16)
                dq = dq + _dot(dsp, jnp.where(hm, kp, zero)) + _dot(dsc, jnp.where(hm, kc, zero))
                dkp, dkc = dkp + _dot_tn(dsp, qm), dkc + _dot_tn(dsc, qm)
                dvp, dvc = dvp + _dot_tn(pp.astype(BF16), dom), dvc + _dot_tn(pc.astype(BF16), dom)
            dq_out[rows, :] = _unrope(dq, c_ref[rows, :], sp_ref[rows, :], sm_ref[rows, :], 8).astype(BF16)
            dk_acc[prow, :] += dkp
            dv_acc[prow, :] += dvp
            dk_acc[rows, :] += dkc
            dv_acc[rows, :] += dvc

        def step(t, _):
            tiles = [scores(DIL_BLOCKS_PER_STEP * t + u) for u in range(DIL_BLOCKS_PER_STEP)]
            for u in range(DIL_BLOCKS_PER_STEP):
                finish(DIL_BLOCKS_PER_STEP * t + u, tiles[u])
            return 0

        lax.fori_loop(0, nblk // DIL_BLOCKS_PER_STEP, step, 0)
        dk_out[...] = _unrope(dk_acc[...], c_ref[...], sp_ref[...], sm_ref[...], 8).astype(BF16)
        dv_out[...] = dv_acc[...].astype(BF16)

    blk = lambda off: _full((s, 128), lambda p, off=off: (0, off + p))
    tab = _full((s, 128), lambda p: (0, 0))
    return pl.pallas_call(
        body, name=name, grid=(4,),
        in_specs=[blk(qo), blk(ko), blk(vo), blk(0), blk(0), blk(0), tab, tab, tab],
        out_specs=[blk(0), blk(0), blk(0)],
        out_shape=[jax.ShapeDtypeStruct((s, 512), BF16)] * 3,
        scratch_shapes=[pltpu.VMEM((s, 128), F32), pltpu.VMEM((s, 128), F32)],
        compiler_params=_cparams(("parallel",)),
    )(qa, ka, va, do, lse, dd, *tabs)


def _chip_copies(ins, outs, sems, outgoing):
    x, y, c, chips = _place()
    myq = 2 * x + y
    n = len(ins)

    def chunk(a, j):
        q = 2 * chips[j][0] + chips[j][1]
        return _remote(ins[a].at[q], outs[a].at[myq if outgoing else q], sems, 2 * a, j, (*chips[j], c))

    if outgoing is None:
        return [pltpu.make_async_copy(ins[a].at[myq], outs[a].at[myq], sems.at[2 * a, 3]) for a in range(n)]
    return [chunk(a, j) for j in range(3) for a in range(n)]


def _dh_bwd(dproj, w_pad, x, gpre, dy, pairs):
    s = x.shape[0]
    tm, tk = min(512, s), 1408
    ni, nk = s // tm, IN_PAD // tk
    n = len(pairs)

    def body(*refs):
        dp_ref, w_ref, x_ref, g_ref, dy_ref = refs[:5]
        ins, gx_out, dg_out, outs = refs[5:5 + n], refs[5 + n], refs[6 + n], refs[7 + n:7 + 2 * n]
        acc, sems = refs[7 + 2 * n], refs[8 + 2 * n]
        i, kk = pl.program_id(0), pl.program_id(1)

        @pl.when(jnp.logical_and(i == 0, kk == 0))
        def _():
            dg_out[...] = jnp.zeros_like(dg_out)
            for cp in _chip_copies(ins, outs, sems, None) + _chip_copies(ins, outs, sems, True):
                cp.start()

        @pl.when(kk == 0)
        def _():
            acc[...] = jnp.zeros_like(acc)

        acc[...] += _dot_nt(dp_ref[...], w_ref[...])

        @pl.when(kk == nk - 1)
        def _():
            g = g_ref[...]
            _, xh, r = _rms(x_ref[...], g)
            dh = acc[...]
            dg_out[...] += jnp.sum(dh * xh, axis=0, keepdims=True)
            gx_out[...] = dy_ref[...] + _rms_bwd(dh, g, xh, r)

        @pl.when(jnp.logical_and(i == ni - 1, kk == nk - 1))
        def _():
            for cp in _chip_copies(ins, outs, sems, False):
                cp.wait_recv()
            for cp in _chip_copies(ins, outs, sems, True):
                cp.wait_send()
            for cp in _chip_copies(ins, outs, sems, None):
                cp.wait()

    row = lambda i, k: (i, 0)
    hbm = pl.BlockSpec(memory_space=pl.ANY)
    res = pl.pallas_call(
        body, name="dh_bwd", grid=(ni, nk),
        in_specs=[_full((tm, tk), lambda i, k: (i, k)), _full((1024, tk), lambda i, k: (0, k)),
                  _full((tm, 1024), row), _full((1, 1024), lambda i, k: (0, 0)), _full((tm, 1024), row)] + [hbm] * n,
        out_specs=[_full((tm, 1024), row), _full((1, 1024), lambda i, k: (0, 0))] + [hbm] * n,
        out_shape=[jax.ShapeDtypeStruct((s, 1024), F32), jax.ShapeDtypeStruct((1, 1024), F32)]
        + [jax.ShapeDtypeStruct(a.shape, a.dtype) for a in pairs],
        scratch_shapes=[pltpu.VMEM((tm, 1024), F32), pltpu.SemaphoreType.DMA((2 * n, 4))],
        compiler_params=_cparams(("arbitrary", "arbitrary")),
    )(dproj, w_pad, x, gpre, dy, *pairs)
    return res[0], res[1], res[2:]


def _dw_in(h, dproj):
    s = h.shape[0]
    ts, tn = min(1024, s), 768
    ns = s // ts

    def body(h_ref, dp_ref, o_ref, acc):
        k = pl.program_id(1)

        @pl.when(k == 0)
        def _():
            acc[...] = jnp.zeros_like(acc)

        acc[...] += _dot_tn(h_ref[...], dp_ref[...])

        @pl.when(k == ns - 1)
        def _():
            o_ref[...] = acc[...].astype(BF16)

    return pl.pallas_call(
        body, name="dw_in", grid=(IN_PAD // tn, ns),
        in_specs=[_full((ts, 1024), lambda j, k: (k, 0)), _full((ts, tn), lambda j, k: (k, j))],
        out_specs=_full((1024, tn), lambda j, k: (0, j)),
        out_shape=jax.ShapeDtypeStruct((1024, IN_PAD), BF16),
        scratch_shapes=[pltpu.VMEM((1024, tn), F32)],
        compiler_params=_cparams(("parallel", "arbitrary")),
    )(h, dproj)


def _remote(src, dst, sems, row, k, to):
    return pltpu.make_async_remote_copy(src_ref=src, dst_ref=dst, send_sem=sems.at[row, k], recv_sem=sems.at[row + 1, k],
                                        device_id=to, device_id_type=pl.DeviceIdType.MESH)


def _place():
    x, y, c = lax.axis_index("x"), lax.axis_index("y"), lax.axis_index("c")
    return x, y, c, [(1 - x, y), (x, 1 - y), (1 - x, 1 - y)]


def _gather_weights(arrays):
    n = len(arrays)

    def body(*refs):
        ins, outs, sems = refs[:n], refs[n:2 * n], refs[2 * n]
        x, y, c, chips = _place()
        me, sib = (x, y, c), (x, y, 1 - c)
        idx = lambda p: 4 * p[0] + 2 * p[1] + p[2]

        def copy(a, k, block, to, from_input=False):
            src = ins[a] if from_input else outs[a].at[idx(block)]
            return _remote(src, outs[a].at[idx(block)], sems, 2 * a, k, to)

        own = [pltpu.make_async_copy(ins[a], outs[a].at[idx(me)], sems.at[2 * a, 7]) for a in range(n)]
        first = [copy(a, 0, me, sib, True) for a in range(n)]
        first += [copy(a, 1 + j, me, (*chip, c), True) for j, chip in enumerate(chips) for a in range(n)]
        for cp in own + first:
            cp.start()
        passed = []
        for j, chip in enumerate(chips):
            for a in range(n):
                copy(a, 1 + j, (*chip, c), me).wait_recv()
                passed.append(copy(a, 4 + j, (*chip, c), sib))
                passed[-1].start()
        for a in range(n):
            copy(a, 0, sib, me).wait_recv()
        for j, chip in enumerate(chips):
            for a in range(n):
                copy(a, 4 + j, (*chip, 1 - c), me).wait_recv()
        for cp in first + passed:
            cp.wait_send()
        for cp in own:
            cp.wait()

    hbm = pl.BlockSpec(memory_space=pl.ANY)
    return pl.pallas_call(
        body, name="gather_weights", in_specs=[hbm] * n, out_specs=[hbm] * n,
        out_shape=[jax.ShapeDtypeStruct((N_DEV,) + a.shape, a.dtype) for a in arrays],
        scratch_shapes=[pltpu.SemaphoreType.DMA((2 * n, N_DEV))],
    )(*arrays)


def _pair_exchange(chunks):
    n = len(chunks)

    def body(*refs):
        ins, outs, sems = refs[:n], refs[n:2 * n], refs[2 * n]
        x, y, c, _ = _place()
        sent = [_remote(ins[a].at[2 * q + (1 - c)], outs[a].at[q], sems, 2 * a, q, (x, y, 1 - c))
                for a in range(n) for q in range(4)]
        for cp in sent:
            cp.start()
        for cp in sent:
            cp.wait_recv()
        for cp in sent:
            cp.wait_send()

    hbm = pl.BlockSpec(memory_space=pl.ANY)
    return pl.pallas_call(
        body, name="pair_exchange", in_specs=[hbm] * n, out_specs=[hbm] * n,
        out_shape=[jax.ShapeDtypeStruct((4,) + a.shape[1:], a.dtype) for a in chunks],
        scratch_shapes=[pltpu.SemaphoreType.DMA((2 * n, 4))],
    )(*chunks)


def _pair_sum(core, chunks, recv, name, tr):
    _, rows, cols = chunks.shape

    def body(c_ref, a_ref, b_ref, o_ref):
        o_ref[...] = (a_ref[...].astype(F32) + b_ref[...].astype(F32)).astype(BF16)

    blk = lambda f: _full((1, tr, cols), f)
    return pl.pallas_call(
        body, name=name, out_shape=jax.ShapeDtypeStruct((4, rows, cols), BF16),
        grid_spec=pltpu.PrefetchScalarGridSpec(
            num_scalar_prefetch=1, grid=(4, rows // tr),
            in_specs=[blk(lambda q, i, c: (2 * q + c[0], i, 0)), blk(lambda q, i, c: (q, i, 0))],
            out_specs=blk(lambda q, i, c: (q, i, 0))),
        compiler_params=_cparams(("parallel", "parallel")),
    )(core, chunks, recv)


def _packet_exchange(packet):
    def body(pk, pk_out, sems):
        x, y, c, _ = _place()
        me = 4 * x + 2 * y + c
        flip = lambda v, b: (1 - v) if b else v

        def small(j, outgoing):
            peer = (flip(x, (j >> 2) & 1), flip(y, (j >> 1) & 1), flip(c, j & 1))
            slot = me if outgoing else 4 * peer[0] + 2 * peer[1] + peer[2]
            return _remote(pk, pk_out.at[slot], sems, 0, j, peer)

        own = pltpu.make_async_copy(pk, pk_out.at[me], sems.at[0, 0])
        sent = [small(j, True) for j in range(1, N_DEV)]
        for cp in [own] + sent:
            cp.start()
        for j in range(1, N_DEV):
            small(j, False).wait_recv()
        for cp in sent:
            cp.wait_send()
        own.wait()

    hbm = pl.BlockSpec(memory_space=pl.ANY)
    return pl.pallas_call(
        body, name="packet_exchange", in_specs=[hbm], out_specs=hbm,
        out_shape=jax.ShapeDtypeStruct((N_DEV,) + packet.shape, packet.dtype),
        scratch_shapes=[pltpu.SemaphoreType.DMA((2, N_DEV))],
    )(packet)


def _adam_math(w, g, m, v):
    m = ADAM_B1 * m + (1.0 - ADAM_B1) * g
    v = ADAM_B2 * v + (1.0 - ADAM_B2) * (g * g)
    m_hat = m / (1.0 - ADAM_B1 ** ADAM_STEP)
    v_hat = v / (1.0 - ADAM_B2 ** ADAM_STEP)
    delta = -ADAM_LR * (m_hat / (jnp.sqrt(v_hat) + ADAM_EPS) + ADAM_WD * w)
    return delta, m, v


def _adam(recv, w, m, v, name, tr):
    _, rows, cols = w.shape

    def body(r_ref, w_ref, m_ref, v_ref, g_out, d_out, m_out, v_out):
        g = r_ref[0].astype(F32)
        for k in range(1, 4):
            g = g + r_ref[k].astype(F32)
        g_out[0] = g
        d_out[0], m_out[0], v_out[0] = _adam_math(w_ref[0], g, m_ref[0], v_ref[0])

    blk = _full((1, tr, cols), lambda i: (0, i, 0))
    return pl.pallas_call(
        body, name=name, grid=(rows // tr,),
        in_specs=[_full((4, tr, cols), lambda i: (0, i, 0)), blk, blk, blk],
        out_specs=[blk] * 4,
        out_shape=[jax.ShapeDtypeStruct(w.shape, F32)] * 4,
        compiler_params=_cparams(("parallel",)),
    )(recv, w, m, v)


def _adam_gains(recv, gains, gains_m, gains_v):
    def body(*refs):
        r_ref, w, m, v = refs[0], refs[1:5], refs[5:9], refs[9:13]
        g_out, d_out, m_out, v_out, loss_out = refs[13:17], refs[17:21], refs[21:25], refs[25:29], refs[29]
        tot = r_ref[0:1, :]
        for k in range(1, N_DEV):
            tot = tot + r_ref[k:k + 1, :]
        for t in range(4):
            g = tot[:, GAIN_OFFS[t]:GAIN_OFFS[t] + GAIN_WIDTHS[t]]
            g_out[t][...] = g
            d_out[t][...], m_out[t][...], v_out[t][...] = _adam_math(w[t][...], g, m[t][...], v[t][...])
        loss_out[...] = tot[:, LOSS_OFF:LOSS_OFF + LANES]

    shapes = [jax.ShapeDtypeStruct((1, n), F32) for n in GAIN_WIDTHS]
    return pl.pallas_call(
        body, name="adam_gains", out_shape=shapes * 4 + [jax.ShapeDtypeStruct((1, LANES), F32)],
    )(recv, *gains, *gains_m, *gains_v)


def _local_step(x, positions, gains, weights, target):
    gpre, gq, gkv, gpost = gains
    w_pad, w_uq, w_uk, w_uv, wp_mla, wp_dil, w_out = weights
    q_tab, k_tab, d_tab = _rope_tables(positions)

    proj, h = _inproj(x, gpre, w_pad, d_tab)
    q, k, v, qt, kt, vt = _mla_prep(proj, gq, gkv, w_uq, w_uk, w_uv, q_tab, k_tab)
    o_mla, ot_mla, lse_mla = _mla_fwd(q, k, vt)

    qkv_perm, od, lsed = [], [], []
    for g, d in enumerate(DIL_DILATIONS):
        if d == 1:
            arr, offs = proj, ((Q_OFF + 512 * g) // 128, (K_OFF + 512 * g) // 128, (V_OFF + 512 * g) // 128)
        else:
            cols = jnp.concatenate([proj[:, o + 512 * g:o + 512 * (g + 1)] for o in (Q_OFF, K_OFF, V_OFF)], axis=1)
            arr, offs = _perm(cols, d), (0, 4, 8)
        qkv_perm.append((arr, offs))
        o_g, lse_g = _dil_fwd(arr, arr, arr, *offs, d, "dil_fwd_%d" % g)
        od.append(_unperm(o_g, d))
        lsed.append(_unperm(lse_g, d))

    merged, ya, yd, o_dil, lse_dil = _merge_fwd(proj, o_mla, od, lsed, wp_mla, wp_dil)
    do, dy, loss, dgpost = _out_loss(merged, w_out, x, target, gpost)

    (dz_mla, dz_dil, dg_mla, dg_dil, do_mla, do_dil, dot_mla, dd_dil, dw_out, dwp_mla, dwp_dil) = _merge_bwd(
        do, w_out, merged, proj, ya, yd, o_mla, o_dil, wp_mla, wp_dil)

    dq, dk, dv = _mla_bwd(q, qt, k, kt, v, do_mla, dot_mla, ot_mla, lse_mla)
    dcq, dkr, dckv, dw_uq, dw_uk, dw_uv, dgq, dgkv = _mla_prep_bwd(dq, dk, dv, proj, gq, gkv, w_uq, w_uk, w_uv,
                                                                   q_tab, k_tab)

    dqs, dks, dvs = [], [], []
    for g, d in enumerate(DIL_DILATIONS):
        arr, offs = qkv_perm[g]
        tabs = tuple(_perm(t, d) for t in d_tab)
        dq_g, dk_g, dv_g = _dil_bwd(arr, arr, arr, *offs, _perm(do_dil, d), _perm(lse_dil, d), _perm(dd_dil, d),
                                    tabs, d, "dil_bwd_%d" % g)
        dqs.append(_unperm(dq_g, d))
        dks.append(_unperm(dk_g, d))
        dvs.append(_unperm(dv_g, d))

    dproj = jnp.concatenate([dz_mla, dz_dil, dg_mla, dg_dil] + dqs + dks + dvs + [dcq, dkr, dckv], axis=1)
    dw_in = _dw_in(h, dproj)
    return loss, (dproj, dy), (dgq, dgkv, dgpost), (dw_in, dw_uq, dw_uk, dw_uv, dwp_mla, dwp_dil, dw_out)


ADAM_ROWS = (256, 384, 256, 512, 512, 128)
PAIR_ROWS = (512, 384, 256, 512, 512, 128)


def kernel(x, positions, pre_norm_g, w_in, q_norm_g, w_uq, kv_norm_g, w_ukv, w_proj_mla, w_proj_dil, w_out, post_norm_g, loss_target, m_pre_norm_g, m_w_in, m_q_norm_g, m_w_uq, m_kv_norm_g, m_w_ukv, m_w_proj_mla, m_w_proj_dil, m_w_out, m_post_norm_g, v_pre_norm_g, v_w_in, v_q_norm_g, v_w_uq, v_kv_norm_g, v_w_ukv, v_w_proj_mla, v_w_proj_dil, v_w_out, v_post_norm_g):
    big_w = (w_in, w_uq, w_ukv, w_proj_mla, w_proj_dil, w_out)
    big_m = (m_w_in, m_w_uq, m_w_ukv, m_w_proj_mla, m_w_proj_dil, m_w_out)
    big_v = (v_w_in, v_w_uq, v_w_ukv, v_w_proj_mla, v_w_proj_dil, v_w_out)
    gains = (pre_norm_g, q_norm_g, kv_norm_g, post_norm_g)
    gains_m = (m_pre_norm_g, m_q_norm_g, m_kv_norm_g, m_post_norm_g)
    gains_v = (v_pre_norm_g, v_q_norm_g, v_kv_norm_g, v_post_norm_g)

    gathered = _gather_weights([w[0].astype(BF16) for w in big_w])
    weights = _assemble_weights(*gathered)

    loss, (dproj, dy), (dgq, dgkv, dgpost), dweights = _local_step(x[0], positions[0], gains, weights,
                                                                   loss_target[0])

    chunks = _grad_chunks(*dweights)
    from_sibling = _pair_exchange(chunks)
    core = lax.axis_index("c").astype(jnp.int32).reshape(1)
    pairs = [_pair_sum(core, chunks[t], from_sibling[t], "pair_sum_%d" % t, PAIR_ROWS[t]) for t in range(6)]
    grad_x, dgpre, received = _dh_bwd(dproj, weights[0], x[0], pre_norm_g, dy, pairs)
    packet = _packet_exchange(jnp.concatenate([dgpre, dgq, dgkv, dgpost, loss[0:1]], axis=1))

    big = [_adam(received[t], big_w[t], big_m[t], big_v[t], "adam_%d" % t, ADAM_ROWS[t]) for t in range(6)]
    small = _adam_gains(packet.reshape(N_DEV, PACKET), gains, gains_m, gains_v)

    def interleave(kind):
        s_pre, s_q, s_kv, s_post = small[4 * kind:4 * kind + 4]
        b_in, b_uq, b_ukv, b_pm, b_pd, b_out = (big[t][kind] for t in range(6))
        return [s_pre, b_in, s_q, b_uq, s_kv, b_ukv, b_pm, b_pd, b_out, s_post]

    return (small[16][0, 0], grad_x[None], *interleave(0), *interleave(1), *interleave(2), *interleave(3))
```

```python
import numpy as np
import jax
import jax.numpy as jnp
from jax import lax
from jax.experimental import pallas as pl
from jax.experimental.pallas import tpu as pltpu

F32 = jnp.float32
BF16 = jnp.bfloat16

D_MODEL = 1024
NORM_EPS = 1e-6
ROPE_THETA = 500000.0
N_DEV = 8
LANES = 128
NEG = -1e30

MLA_HEADS = 8
MLA_Q_RANK = 384
MLA_KV_RANK = 256
MLA_SCALE = 96.0 ** -0.5
LOG2E = 1.4426950408889634
MLA_QSCALE = MLA_SCALE * LOG2E
MLA_FWD_Q_PER_K = 2
MLA_BWD_Q_PER_K = 2
DIL_DILATIONS = (1, 4, 16)
DIL_SCALE = 0.125
Q_BLOCK = 128
DIL_BLOCKS_PER_STEP = 4

Z_MLA, Z_DIL, G_MLA, G_DIL = 0, 512, 1024, 2048
Q_OFF, K_OFF, V_OFF = 3072, 4608, 6144
CQ_OFF, KR_OFF, CKV_OFF, IN_PAD = 7680, 8064, 8192, 8448
IN_WIDTH = 8352
SHARD_W = IN_WIDTH // 8
IN_SEGS = ((0, 384, CQ_OFF), (384, 256, CKV_OFF), (640, 32, KR_OFF), (672, 1536, Q_OFF), (2208, 1536, K_OFF),
           (3744, 1536, V_OFF), (5280, 512, Z_MLA), (5792, 512, Z_DIL), (6304, 1024, G_MLA), (7328, 1024, G_DIL))

GAIN_OFFS = (0, 1024, 1408, 1664)
GAIN_WIDTHS = (1024, 384, 256, 1024)
LOSS_OFF, PACKET = 2688, 2816

ADAM_LR, ADAM_B1, ADAM_B2, ADAM_EPS, ADAM_WD, ADAM_STEP = 0.001, 0.9, 0.999, 1e-08, 0.01, 10

VMEM_LIMIT_MB = 56


def _cparams(sem=None, vmem_mb=VMEM_LIMIT_MB):
    return pltpu.CompilerParams(dimension_semantics=sem, vmem_limit_bytes=vmem_mb * 1024 * 1024)


def _dot(a, b):
    return jnp.dot(a, b, preferred_element_type=F32)


def _dot_nt(a, b):
    return lax.dot_general(a, b, (((1,), (1,)), ((), ())), preferred_element_type=F32)


def _dot_tn(a, b):
    return lax.dot_general(a, b, (((0,), (0,)), ((), ())), preferred_element_type=F32)


def _tile_lanes(t, width):
    return t if width == t.shape[1] else jnp.tile(t, (1, width // t.shape[1]))


def _rope(x, c, sp, sm, a):
    n = x.shape[1]
    return x * c + pltpu.roll(x, a, 1) * sp + pltpu.roll(x, n - a, 1) * sm


def _unrope(dy, c, sp, sm, a):
    n = dy.shape[1]
    return dy * c + pltpu.roll(dy * sp, n - a, 1) + pltpu.roll(dy * sm, a, 1)


def _sigmoid(z):
    return 1.0 / (1.0 + jnp.exp(-z))


def _left_mask():
    return lax.broadcasted_iota(jnp.int32, (1, LANES), 1) < 64


def _expand_half(x, hh, left):
    r = pltpu.roll(x, 64, 1)
    return jnp.where(left, x, r) if hh == 0 else jnp.where(left, r, x)


def _rms(xv, g):
    r = lax.rsqrt(jnp.mean(xv * xv, axis=-1, keepdims=True) + NORM_EPS)
    xh = xv * r
    return xh * g, xh, r


def _rms_bwd(dout, g, xh, r):
    dxh = dout * g
    return r * (dxh - xh * jnp.mean(dxh * xh, axis=-1, keepdims=True))


def _full(shape, index_map):
    return pl.BlockSpec(shape, index_map)


def _w_in_pieces():
    out = []
    for s, n, off in sorted(IN_SEGS, key=lambda t: t[2]):
        c = s
        while c < s + n:
            k = c // SHARD_W
            e = min(s + n, (k + 1) * SHARD_W)
            out.append((k, c - k * SHARD_W, e - c, off + (c - s)))
            c = e
    return out


def _assemble_w_in(g):
    parts, cur = [], 0
    for k, a, w, off in _w_in_pieces():
        if off > cur:
            parts.append(jnp.zeros((D_MODEL, off - cur), g.dtype))
        parts.append(g[k, :, a:a + w])
        cur = off + w
    if cur < IN_PAD:
        parts.append(jnp.zeros((D_MODEL, IN_PAD - cur), g.dtype))
    return jnp.concatenate(parts, axis=1)


def _dw_in_chunks(dw):
    chunks = []
    for dev in range(N_DEV):
        mine = sorted((p for p in _w_in_pieces() if p[0] == dev), key=lambda p: p[1])
        chunks.append(jnp.concatenate([dw[:, off:off + w] for k, a, w, off in mine], axis=1))
    return jnp.stack(chunks)


def _assemble_weights(g_in, g_uq, g_ukv, g_pm, g_pd, g_out):
    w_uq_pad = jnp.pad(g_uq.transpose(1, 0, 2), ((0, 0), (0, 0), (0, 32))).reshape(384, 1024)
    ukv = g_ukv.transpose(1, 0, 2)
    w_uk_pad = jnp.pad(ukv[:, :, :64], ((0, 0), (0, 0), (0, 64))).reshape(256, 1024)
    w_uv = ukv[:, :, 64:].reshape(256, 512)
    wp_mla = g_pm.transpose(1, 0, 2).reshape(512, 1024)
    wp_dil = g_pd.transpose(1, 0, 2).reshape(512, 1024)
    return _assemble_w_in(g_in), w_uq_pad, w_uk_pad, w_uv, wp_mla, wp_dil, g_out.reshape(1024, 1024)


def _grad_chunks(dw_in_pad, dw_uq_pad, dw_uk_pad, dw_uv, dwp_mla, dwp_dil, dw_out):
    a = _dw_in_chunks(dw_in_pad)
    b = dw_uq_pad.reshape(384, 8, 128)[:, :, :96].transpose(1, 0, 2)
    c = jnp.concatenate([dw_uk_pad.reshape(256, 8, 128)[:, :, :64], dw_uv.reshape(256, 8, 64)], axis=2)
    c = c.transpose(1, 0, 2)
    d = dwp_mla.reshape(512, N_DEV, 128).transpose(1, 0, 2)
    e = dwp_dil.reshape(512, N_DEV, 128).transpose(1, 0, 2)
    f = dw_out.reshape(N_DEV, 128, 1024)
    return [t.astype(BF16) for t in (a, b, c, d, e, f)]


def _lane_consts(freqs, half, first, period):
    rel = (np.arange(LANES) % period) - first
    rot = (rel >= 0) & (rel < 2 * half)
    freq = np.where(rot, freqs[np.clip(rel, 0, 2 * half - 1) % half], 0.0).astype(np.float32)
    x1 = (rot & (rel < half)).astype(np.float32)
    x2 = (rot & (rel >= half)).astype(np.float32)
    return freq[None, :], x1[None, :], x2[None, :]


def _rope_tables(pos):
    p = pos.astype(F32)[:, None]
    inv_m = np.float32(ROPE_THETA) ** (-(np.arange(0, 32, 2, dtype=np.float32) / np.float32(32)))
    inv_d = np.float32(ROPE_THETA) ** (-(np.arange(0, 16, 2, dtype=np.float32) / np.float32(16)))
    lane = np.arange(LANES)
    tabs = []
    for freqs, half, first, period, keep in ((inv_m, 16, 64, 128, lane < 96), (inv_m, 16, 0, 128, lane < 32),
                                              (inv_d, 8, 0, 64, lane >= 0)):
        freq, x1, x2 = _lane_consts(freqs, half, first, period)
        ang = p * freq
        sin = jnp.sin(ang)
        tabs.append((jnp.cos(ang) * keep.astype(np.float32)[None, :], sin * x2, sin * (-x1)))
    return tuple(tabs)


def _inproj(x, gpre, w_pad, d_tab):
    s = x.shape[0]
    tm, tn = min(1024, s), 768
    rope_lo, rope_hi = Q_OFF // tn, V_OFF // tn

    def body(x_ref, g_ref, w_ref, c_ref, sp_ref, sm_ref, o_ref, h_ref):
        j = pl.program_id(1)

        @pl.when(j == 0)
        def _():
            hv, _, _ = _rms(x_ref[...], g_ref[...])
            h_ref[...] = hv.astype(BF16)

        acc = _dot(h_ref[...], w_ref[...])
        is_rope = jnp.logical_and(j >= rope_lo, j < rope_hi)

        @pl.when(is_rope)
        def _():
            o_ref[...] = _rope(acc, _tile_lanes(c_ref[...], tn), _tile_lanes(sp_ref[...], tn),
                               _tile_lanes(sm_ref[...], tn), 8).astype(BF16)

        @pl.when(jnp.logical_not(is_rope))
        def _():
            o_ref[...] = acc.astype(BF16)

    row = lambda i, j: (i, 0)
    return pl.pallas_call(
        body, name="inproj", grid=(s // tm, IN_PAD // tn),
        in_specs=[_full((tm, D_MODEL), row), _full((1, D_MODEL), lambda i, j: (0, 0)),
                  _full((D_MODEL, tn), lambda i, j: (0, j)),
                  _full((tm, LANES), row), _full((tm, LANES), row), _full((tm, LANES), row)],
        out_specs=[_full((tm, tn), lambda i, j: (i, j)), _full((tm, D_MODEL), row)],
        out_shape=[jax.ShapeDtypeStruct((s, IN_PAD), BF16), jax.ShapeDtypeStruct((s, D_MODEL), BF16)],
        compiler_params=_cparams(("parallel", "arbitrary")),
    )(x, gpre, w_pad, *d_tab)


def _mla_prep(proj, gq, gkv, w_uq, w_uk, w_uv, q_tab, k_tab):
    s = proj.shape[0]
    tm = min(512, s)

    def body(cq_ref, kr_ref, ckv_ref, gq_ref, gkv_ref, wq_ref, wk_ref, wv_ref,
             qc, qsp, qsm, kc, ksp, ksm, q_out, k_out, v_out, qt_out, kt_out, vt_out):
        cqn, _, _ = _rms(cq_ref[...].astype(F32), gq_ref[...])
        q = _dot(cqn.astype(BF16), wq_ref[...])
        q = _rope(q, _tile_lanes(qc[...], 1024), _tile_lanes(qsp[...], 1024), _tile_lanes(qsm[...], 1024), 16)
        q = q * MLA_QSCALE
        q_out[...] = q.astype(BF16)
        qt_out[...] = q.T.astype(BF16)
        ckvn, _, _ = _rms(ckv_ref[...].astype(F32), gkv_ref[...])
        ckvn = ckvn.astype(BF16)
        kr = _rope(kr_ref[...].astype(F32), kc[...], ksp[...], ksm[...], 16)
        k = _dot(ckvn, wk_ref[...]) + _tile_lanes(pltpu.roll(kr, 64, 1), 1024)
        k_out[...] = k.astype(BF16)
        kt_out[...] = k.T.astype(BF16)
        v = _dot(ckvn, wv_ref[...])
        v_out[...] = v.astype(BF16)
        vt_out[...] = v.T.astype(BF16)

    row = lambda i: (i, 0)
    col = lambda i: (0, i)
    cst = lambda i: (0, 0)
    tabs = [_full((tm, LANES), row)] * 6
    return pl.pallas_call(
        body, name="mla_prep", grid=(s // tm,),
        in_specs=[_full((tm, 384), lambda i: (i, CQ_OFF // 384)), _full((tm, 128), lambda i: (i, KR_OFF // 128)),
                  _full((tm, 256), lambda i: (i, CKV_OFF // 256)), _full((1, 384), cst), _full((1, 256), cst),
                  _full((384, 1024), cst), _full((256, 1024), cst), _full((256, 512), cst)] + tabs,
        out_specs=[_full((tm, 1024), row), _full((tm, 1024), row), _full((tm, 512), row),
                   _full((1024, tm), col), _full((1024, tm), col), _full((512, tm), col)],
        out_shape=[jax.ShapeDtypeStruct((s, 1024), BF16), jax.ShapeDtypeStruct((s, 1024), BF16),
                   jax.ShapeDtypeStruct((s, 512), BF16), jax.ShapeDtypeStruct((1024, s), BF16),
                   jax.ShapeDtypeStruct((1024, s), BF16), jax.ShapeDtypeStruct((512, s), BF16)],
        compiler_params=_cparams(("parallel",)),
    )(proj, proj, proj, gq, gkv, w_uq, w_uk, w_uv, *q_tab, *k_tab)


def _mla_fwd(q, k, vt):
    s = q.shape[0]
    tk = min(256, s)
    ratio = MLA_FWD_Q_PER_K if s >= MLA_FWD_Q_PER_K * tk else 1
    tq = ratio * tk
    nq = s // tq

    def body(q_ref, k_ref, vt_ref, o_ref, ot_ref, lse_ref):
        krow = lax.broadcasted_iota(jnp.int32, (tk, tq), 0)
        qcol = lax.broadcasted_iota(jnp.int32, (tk, tq), 1)

        def q_step(i, _):
            r0 = pl.multiple_of(i * tq, tq)
            qs = [q_ref[pl.ds(r0, tq), hh * 128:(hh + 1) * 128] for hh in range(2)]

            def scores(j):
                c0 = pl.multiple_of(j * tk, tk)
                return tuple(_dot_nt(k_ref[pl.ds(c0, tk), hh * 128:(hh + 1) * 128], qs[hh])
                             for hh in range(2))

            def update(j, sts, stats, masked):
                c0 = pl.multiple_of(j * tk, tk)
                new = []
                causal = (krow + (c0 - r0)) <= qcol
                for hh in range(2):
                    m, l, acc = stats[hh]
                    st = jnp.where(causal, sts[hh], NEG) if masked else sts[hh]
                    m_new = jnp.maximum(m, jnp.max(st, axis=0, keepdims=True))
                    alpha = jnp.exp2(m - m_new)
                    p = jnp.exp2(st - m_new)
                    l = alpha * l + jnp.sum(p, axis=0, keepdims=True)
                    acc = acc * alpha + _dot(vt_ref[hh * 64:(hh + 1) * 64, pl.ds(c0, tk)], p.astype(BF16))
                    new.append((m_new, l, acc))
                return tuple(new)

            def kv_step(j, carry):
                sts, stats = carry
                nxt = scores(j + 1)
                return nxt, update(j, sts, stats, False)

            init = tuple((jnp.full((1, tq), NEG, F32), jnp.zeros((1, tq), F32), jnp.zeros((64, tq), F32))
                         for _ in range(2))
            sts, stats = lax.fori_loop(0, ratio * i, kv_step, (scores(0), init))
            for d in range(ratio):
                if d > 0:
                    sts = scores(ratio * i + d)
                stats = update(ratio * i + d, sts, stats, True)
            (ma, la, acca), (mb, lb, accb) = stats
            ot = jnp.concatenate([acca / la, accb / lb], axis=0)
            ot_ref[:, pl.ds(r0, tq)] = ot.astype(BF16)
            o_ref[pl.ds(r0, tq), :] = ot.T.astype(BF16)
            lse_ref[:, pl.ds(r0, tq)] = jnp.concatenate(
                [ma + jnp.log2(la), mb + jnp.log2(lb), jnp.zeros((6, tq), F32)], axis=0)
            return 0

        lax.fori_loop(0, nq, q_step, 0)

    return pl.pallas_call(
        body, name="mla_fwd", grid=(4,),
        in_specs=[_full((s, 256), lambda p: (0, p)), _full((s, 256), lambda p: (0, p)),
                  _full((128, s), lambda p: (p, 0))],
        out_specs=[_full((s, 128), lambda p: (0, p)), _full((128, s), lambda p: (p, 0)),
                   _full((8, s), lambda p: (p, 0))],
        out_shape=[jax.ShapeDtypeStruct((s, 512), BF16), jax.ShapeDtypeStruct((512, s), BF16),
                   jax.ShapeDtypeStruct((32, s), F32)],
        compiler_params=_cparams(("parallel",)),
    )(q, k, vt)


def _band_masks(has_prev):
    r = lax.broadcasted_iota(jnp.int32, (Q_BLOCK, Q_BLOCK), 0)
    c = lax.broadcasted_iota(jnp.int32, (Q_BLOCK, Q_BLOCK), 1)
    return c >= r + jnp.where(has_prev, 0, Q_BLOCK), c <= r


def _dil_rows(b, d, per_seq):
    r, n = b // per_seq, b % per_seq
    start = r + (d * Q_BLOCK) * n
    prev = start - jnp.where(n > 0, d * Q_BLOCK, 0)
    if d == 1:
        return pl.ds(pl.multiple_of(start, Q_BLOCK), Q_BLOCK), pl.ds(pl.multiple_of(prev, Q_BLOCK), Q_BLOCK)
    return pl.ds(start, Q_BLOCK, stride=d), pl.ds(prev, Q_BLOCK, stride=d)


def _dil_fwd(proj, g, d, name):
    s = proj.shape[0]
    nblk = s // Q_BLOCK
    per_seq = nblk // d

    def body(q_ref, k_ref, v_ref, o_ref, lse_ref, qf, kf, vf, of, lf):
        left = _left_mask()
        hms = (left, jnp.logical_not(left))
        qf[...] = q_ref[...].astype(F32)
        kf[...] = k_ref[...].astype(F32)
        vf[...] = v_ref[...].astype(F32)

        def scores(b):
            rows, prow = _dil_rows(b, d, per_seq)
            qb = qf[rows, :].astype(BF16)
            kp, kc = kf[prow, :].astype(BF16), kf[rows, :].astype(BF16)
            out = []
            for hh in range(2):
                qm = jnp.where(hms[hh], qb, jnp.zeros_like(qb))
                out.append((_dot_nt(qm, kp) * DIL_SCALE, _dot_nt(qm, kc) * DIL_SCALE))
            return out

        def finish(b, tiles):
            rows, prow = _dil_rows(b, d, per_seq)
            pmask, cur_ok = _band_masks((b % per_seq) > 0)
            vp, vc = vf[prow, :].astype(BF16), vf[rows, :].astype(BF16)
            zero = jnp.zeros_like(vp)
            outs = []
            for hh in range(2):
                sp = jnp.where(pmask, tiles[hh][0], NEG)
                sc = jnp.where(cur_ok, tiles[hh][1], NEG)
                m = jnp.maximum(jnp.max(sp, axis=1, keepdims=True), jnp.max(sc, axis=1, keepdims=True))
                pp, pc = jnp.exp(sp - m), jnp.exp(sc - m)
                l = jnp.sum(pp, axis=1, keepdims=True) + jnp.sum(pc, axis=1, keepdims=True)
                acc = (_dot(pp.astype(BF16), jnp.where(hms[hh], vp, zero))
                       + _dot(pc.astype(BF16), jnp.where(hms[hh], vc, zero)))
                outs.append((acc / l, jnp.broadcast_to(m + jnp.log(l), (Q_BLOCK, LANES))))
            of[rows, :] = outs[0][0] + outs[1][0]
            lf[rows, :] = jnp.where(left, outs[0][1], outs[1][1])

        def step(t, _):
            tiles = [scores(DIL_BLOCKS_PER_STEP * t + u) for u in range(DIL_BLOCKS_PER_STEP)]
            for u in range(DIL_BLOCKS_PER_STEP):
                finish(DIL_BLOCKS_PER_STEP * t + u, tiles[u])
            return 0

        lax.fori_loop(0, nblk // DIL_BLOCKS_PER_STEP, step, 0)
        o_ref[...] = of[...].astype(BF16)
        lse_ref[...] = lf[...]

    blk = lambda off: _full((s, 128), lambda p, off=off: (0, off + p))
    qo, ko, vo = ((off + 512 * g) // 128 for off in (Q_OFF, K_OFF, V_OFF))
    return pl.pallas_call(
        body, name=name, grid=(4,),
        in_specs=[blk(qo), blk(ko), blk(vo)],
        out_specs=[blk(0), blk(0)],
        out_shape=[jax.ShapeDtypeStruct((s, 512), BF16), jax.ShapeDtypeStruct((s, 512), F32)],
        scratch_shapes=[pltpu.VMEM((s, 128), F32)] * 5,
        compiler_params=_cparams(("parallel",)),
    )(proj, proj, proj)


def _merge_fwd(proj, o_mla, od, lsed, wp_mla, wp_dil):
    s = proj.shape[0]
    tm = min(512, s)

    def body(zm_ref, zd_ref, gm_ref, gd_ref, om_ref, o0, o1, o2, l0, l1, l2, wm_ref, wd_ref,
             mg_out, ya_out, yd_out, odil_out, lse_out):
        la, lb, lc = l0[...], l1[...], l2[...]
        lmax = jnp.maximum(jnp.maximum(la, lb), lc)
        ea, eb, ec = jnp.exp(la - lmax), jnp.exp(lb - lmax), jnp.exp(lc - lmax)
        den = ea + eb + ec
        o_dil = (ea * o0[...].astype(F32) + eb * o1[...].astype(F32) + ec * o2[...].astype(F32)) / den
        o_dil = o_dil.astype(BF16)
        odil_out[...] = o_dil
        lse_out[...] = lmax + jnp.log(den)
        zm, zd = zm_ref[...].astype(F32), zd_ref[...].astype(F32)
        pa = (om_ref[...].astype(F32) * (zm * _sigmoid(zm))).astype(BF16)
        pd = (o_dil.astype(F32) * (zd * _sigmoid(zd))).astype(BF16)
        ya = _dot(pa, wm_ref[...])
        yd = _dot(pd, wd_ref[...])
        ya_out[...] = ya.astype(BF16)
        yd_out[...] = yd.astype(BF16)
        mg_out[...] = (_sigmoid(gm_ref[...].astype(F32)) * ya + _sigmoid(gd_ref[...].astype(F32)) * yd).astype(BF16)

    row = lambda i: (i, 0)
    cst = lambda i: (0, 0)
    r512 = _full((tm, 512), row)
    r1024 = _full((tm, 1024), row)
    return pl.pallas_call(
        body, name="merge_fwd", grid=(s // tm,),
        in_specs=[_full((tm, 512), lambda i: (i, Z_MLA // 512)), _full((tm, 512), lambda i: (i, Z_DIL // 512)),
                  _full((tm, 1024), lambda i: (i, G_MLA // 1024)), _full((tm, 1024), lambda i: (i, G_DIL // 1024)),
                  r512, r512, r512, r512, r512, r512, r512, _full((512, 1024), cst), _full((512, 1024), cst)],
        out_specs=[r1024, r1024, r1024, r512, r512],
        out_shape=[jax.ShapeDtypeStruct((s, 1024), BF16), jax.ShapeDtypeStruct((s, 1024), BF16),
                   jax.ShapeDtypeStruct((s, 1024), BF16), jax.ShapeDtypeStruct((s, 512), BF16),
                   jax.ShapeDtypeStruct((s, 512), F32)],
        compiler_params=_cparams(("parallel",)),
    )(proj, proj, proj, proj, o_mla, *od, *lsed, wp_mla, wp_dil)


def _out_loss(merged, w_out, x, target, gpost):
    s = x.shape[0]
    tm = min(512, s)

    def body(mg_ref, w_ref, x_ref, t_ref, g_ref, do_out, dy_out, loss_out, dg_out):
        i = pl.program_id(0)

        @pl.when(i == 0)
        def _():
            loss_out[...] = jnp.zeros_like(loss_out)
            dg_out[...] = jnp.zeros_like(dg_out)

        o = _dot(mg_ref[...], w_ref[...])
        g = g_ref[...]
        n, u, r = _rms(o, g)
        e = (x_ref[...] + n) - t_ref[...]
        loss_out[...] += 0.5 * jnp.sum(jnp.mean(e * e, axis=-1, keepdims=True))
        dy = e * (1.0 / D_MODEL)
        dy_out[...] = dy
        dg_out[...] += jnp.sum(dy * u, axis=0, keepdims=True)
        do_out[...] = _rms_bwd(dy, g, u, r).astype(BF16)

    row = lambda i: (i, 0)
    cst = lambda i: (0, 0)
    return pl.pallas_call(
        body, name="out_loss", grid=(s // tm,),
        in_specs=[_full((tm, 1024), row), _full((1024, 1024), cst), _full((tm, 1024), row), _full((tm, 1024), row),
                  _full((1, 1024), cst)],
        out_specs=[_full((tm, 1024), row), _full((tm, 1024), row), _full((8, LANES), cst), _full((1, 1024), cst)],
        out_shape=[jax.ShapeDtypeStruct((s, 1024), BF16), jax.ShapeDtypeStruct((s, 1024), F32),
                   jax.ShapeDtypeStruct((8, LANES), F32), jax.ShapeDtypeStruct((1, 1024), F32)],
        compiler_params=_cparams(("arbitrary",)),
    )(merged, w_out, x, target, gpost)


def _seg_sum64(x, ones_bd):
    hi = x.astype(BF16)
    lo = (x - hi.astype(F32)).astype(BF16)
    return _dot(hi, ones_bd) + _dot(lo, ones_bd)


def _merge_bwd(do, w_out, merged, proj, ya, yd, o_mla, o_dil, wp_mla, wp_dil):
    s = do.shape[0]
    tm = min(256, s)
    seg = jnp.arange(512) // 64
    ones_bd = (seg[:, None] == seg[None, :]).astype(BF16)

    def body(do_ref, wo_ref, mg_ref, zm_ref, zd_ref, gm_ref, gd_ref, ya_ref, yd_ref, om_ref, od_ref, wm_ref, wd_ref,
             bd_ref, dzm_out, dzd_out, dgm_out, dgd_out, dom_out, dod_out, domt_out, dd_out, dwo_out, dwm_out,
             dwd_out):
        i = pl.program_id(0)

        @pl.when(i == 0)
        def _():
            dwo_out[...] = jnp.zeros_like(dwo_out)
            dwm_out[...] = jnp.zeros_like(dwm_out)
            dwd_out[...] = jnp.zeros_like(dwd_out)

        dov = do_ref[...]
        dwo_out[...] += _dot_tn(mg_ref[...], dov)
        dm = _dot_nt(dov, wo_ref[...])
        for g_ref, y_ref, z_ref, o_ref, w_ref, dz_out, dg_out, dob_out, dd_o, dw_out in (
                (gm_ref, ya_ref, zm_ref, om_ref, wm_ref, dzm_out, dgm_out, dom_out, None, dwm_out),
                (gd_ref, yd_ref, zd_ref, od_ref, wd_ref, dzd_out, dgd_out, dod_out, dd_out, dwd_out)):
            sg = _sigmoid(g_ref[...].astype(F32))
            dg_out[...] = (dm * y_ref[...].astype(F32) * sg * (1.0 - sg)).astype(BF16)
            dy = (dm * sg).astype(BF16)
            z = z_ref[...].astype(F32)
            sz = _sigmoid(z)
            silu = z * sz
            ob = o_ref[...].astype(F32)
            dw_out[...] += _dot_tn((ob * silu).astype(BF16), dy)
            dp = _dot_nt(dy, w_ref[...])
            dz_out[...] = (dp * ob * (sz * (1.0 + z * (1.0 - sz)))).astype(BF16)
            dob = dp * silu
            dob_out[...] = dob.astype(BF16)
            if dd_o is None:
                domt_out[...] = dob.T.astype(BF16)
            else:
                dd_o[...] = _seg_sum64(dob * ob, bd_ref[...])

    row = lambda i: (i, 0)
    cst = lambda i: (0, 0)
    r512 = _full((tm, 512), row)
    r1024 = _full((tm, 1024), row)
    return pl.pallas_call(
        body, name="merge_bwd", grid=(s // tm,),
        in_specs=[r1024, _full((1024, 1024), cst), r1024,
                  _full((tm, 512), lambda i: (i, Z_MLA // 512)), _full((tm, 512), lambda i: (i, Z_DIL // 512)),
                  _full((tm, 1024), lambda i: (i, G_MLA // 1024)), _full((tm, 1024), lambda i: (i, G_DIL // 1024)),
                  r1024, r1024, r512, r512, _full((512, 1024), cst), _full((512, 1024), cst), _full((512, 512), cst)],
        out_specs=[r512, r512, r1024, r1024, r512, r512, _full((512, tm), lambda i: (0, i)), r512,
                   _full((1024, 1024), cst), _full((512, 1024), cst), _full((512, 1024), cst)],
        out_shape=[jax.ShapeDtypeStruct((s, 512), BF16), jax.ShapeDtypeStruct((s, 512), BF16),
                   jax.ShapeDtypeStruct((s, 1024), BF16), jax.ShapeDtypeStruct((s, 1024), BF16),
                   jax.ShapeDtypeStruct((s, 512), BF16), jax.ShapeDtypeStruct((s, 512), BF16),
                   jax.ShapeDtypeStruct((512, s), BF16), jax.ShapeDtypeStruct((s, 512), F32),
                   jax.ShapeDtypeStruct((1024, 1024), F32), jax.ShapeDtypeStruct((512, 1024), F32),
                   jax.ShapeDtypeStruct((512, 1024), F32)],
        compiler_params=_cparams(("arbitrary",)),
    )(do, w_out, merged, proj, proj, proj, proj, ya, yd, o_mla, o_dil, wp_mla, wp_dil, ones_bd)


def _mla_bwd(q, qt, k, kt, v, do, dot, ot, lse):
    s = q.shape[0]
    tk = min(256, s)
    ratio = MLA_BWD_Q_PER_K if s >= MLA_BWD_Q_PER_K * tk else 1
    tq = ratio * tk
    nq, nk = s // tq, s // tk

    def body(q_ref, qt_ref, k_ref, kt_ref, v_ref, do_ref, dot_ref, ot_ref, lse_ref, dqt_out, dkt_out, dvt_out,
             dqt_acc):
        left = _left_mask()
        krow = lax.broadcasted_iota(jnp.int32, (tk, tq), 0)
        qcol = lax.broadcasted_iota(jnp.int32, (tk, tq), 1)
        dqt_acc[...] = jnp.zeros_like(dqt_acc)

        def kv_step(j, _):
            c0 = pl.multiple_of(j * tk, tk)
            vv = v_ref[pl.ds(c0, tk), :]
            khs = [k_ref[pl.ds(c0, tk), hh * 128:(hh + 1) * 128] for hh in range(2)]
            kths = [kt_ref[hh * 128:(hh + 1) * 128, pl.ds(c0, tk)] for hh in range(2)]
            vms = [jnp.where(left if hh == 0 else jnp.logical_not(left), vv, jnp.zeros_like(vv)) for hh in range(2)]

            def scores(i):
                r0 = pl.multiple_of(jnp.minimum(i, nq - 1) * tq, tq)
                dov = do_ref[pl.ds(r0, tq), :]
                return tuple((_dot_nt(khs[hh], q_ref[pl.ds(r0, tq), hh * 128:(hh + 1) * 128]),
                              _dot_nt(vms[hh], dov)) for hh in range(2))

            def update(i, tiles, acc, masked):
                r0 = pl.multiple_of(i * tq, tq)
                new = []
                for hh in range(2):
                    dkt, dvt = acc[hh]
                    st, dp = tiles[hh]
                    hrows = slice(hh * 128, (hh + 1) * 128)
                    drows = slice(hh * 64, (hh + 1) * 64)
                    doth = dot_ref[drows, pl.ds(r0, tq)]
                    dd = jnp.sum(doth.astype(F32) * ot_ref[drows, pl.ds(r0, tq)].astype(F32), axis=0, keepdims=True)
                    p = jnp.exp2(st - lse_ref[hh:hh + 1, pl.ds(r0, tq)])
                    if masked:
                        p = jnp.where((krow + (c0 - r0)) <= qcol, p, 0.0)
                    ds = (p * (dp - dd)).astype(BF16)
                    dvt = dvt + _dot_nt(doth, p.astype(BF16))
                    dkt = dkt + _dot_nt(qt_ref[hrows, pl.ds(r0, tq)], ds)
                    dqt_acc[hrows, pl.ds(r0, tq)] += _dot(kths[hh], ds)
                    new.append((dkt, dvt))
                return tuple(new)

            init = tuple((jnp.zeros((128, tk), F32), jnp.zeros((64, tk), F32)) for _ in range(2))
            i0 = j // ratio
            acc = update(i0, scores(i0), init, True)
            acc = lax.fori_loop(i0 + 1, nq, lambda i, a: update(i, scores(i), a, False), acc)
            for hh in range(2):
                dkt_out[hh * 128:(hh + 1) * 128, pl.ds(c0, tk)] = (acc[hh][0] * (1.0 / LOG2E)).astype(BF16)
                dvt_out[hh * 64:(hh + 1) * 64, pl.ds(c0, tk)] = acc[hh][1].astype(BF16)
            return 0

        lax.fori_loop(0, nk, kv_step, 0)
        dqt_out[...] = (dqt_acc[...] * MLA_SCALE).astype(BF16)

    b256 = _full((s, 256), lambda p: (0, p))
    b128 = _full((s, 128), lambda p: (0, p))
    t256 = _full((256, s), lambda p: (p, 0))
    t128 = _full((128, s), lambda p: (p, 0))
    return pl.pallas_call(
        body, name="mla_bwd", grid=(4,),
        in_specs=[b256, t256, b256, t256, b128, b128, t128, t128, _full((8, s), lambda p: (p, 0))],
        out_specs=[t256, t256, t128],
        out_shape=[jax.ShapeDtypeStruct((1024, s), BF16), jax.ShapeDtypeStruct((1024, s), BF16),
                   jax.ShapeDtypeStruct((512, s), BF16)],
        scratch_shapes=[pltpu.VMEM((256, s), F32)],
        compiler_params=_cparams(("parallel",)),
    )(q, qt, k, kt, v, do, dot, ot, lse)


def _mla_prep_bwd(dq, dk, dv, proj, gq, gkv, w_uq, w_uk, w_uv, q_tab, k_tab):
    s = proj.shape[0]
    tm = min(256, s)

    def body(dqt_ref, dkt_ref, dvt_ref, cq_ref, ckv_ref, gq_ref, gkv_ref, wq_ref, wk_ref, wv_ref,
             qc, qsp, qsm, kc, ksp, ksm,
             dcq_out, dkr_out, dckv_out, dwq_out, dwk_out, dwv_out, dgq_out, dgkv_out):
        i = pl.program_id(0)

        @pl.when(i == 0)
        def _():
            for r in (dwq_out, dwk_out, dwv_out, dgq_out, dgkv_out):
                r[...] = jnp.zeros_like(r)

        dqu = _unrope(dqt_ref[...].astype(F32).T, _tile_lanes(qc[...], 1024), _tile_lanes(qsp[...], 1024),
                      _tile_lanes(qsm[...], 1024), 16).astype(BF16)
        gq = gq_ref[...]
        cqn, xh, r = _rms(cq_ref[...].astype(F32), gq)
        dwq_out[...] += _dot_tn(cqn.astype(BF16), dqu)
        dcqn = _dot_nt(dqu, wq_ref[...])
        dgq_out[...] += jnp.sum(dcqn * xh, axis=0, keepdims=True)
        dcq_out[...] = _rms_bwd(dcqn, gq, xh, r).astype(BF16)

        dkf = dkt_ref[...].astype(F32).T
        dkb = dkf.astype(BF16)
        dsum = dkf[:, 0:128]
        for h in range(1, MLA_HEADS):
            dsum = dsum + dkf[:, h * 128:(h + 1) * 128]
        dkr_out[...] = _unrope(pltpu.roll(dsum, 64, 1), kc[...], ksp[...], ksm[...], 16).astype(BF16)

        dvb = dvt_ref[...].astype(F32).T.astype(BF16)
        gkv = gkv_ref[...]
        ckvn, xh2, r2 = _rms(ckv_ref[...].astype(F32), gkv)
        ckvn = ckvn.astype(BF16)
        dwk_out[...] += _dot_tn(ckvn, dkb)
        dwv_out[...] += _dot_tn(ckvn, dvb)
        dckvn = _dot_nt(dkb, wk_ref[...]) + _dot_nt(dvb, wv_ref[...])
        dgkv_out[...] += jnp.sum(dckvn * xh2, axis=0, keepdims=True)
        dckv_out[...] = _rms_bwd(dckvn, gkv, xh2, r2).astype(BF16)

    row = lambda i: (i, 0)
    cst = lambda i: (0, 0)
    tabs = [_full((tm, LANES), row)] * 6
    return pl.pallas_call(
        body, name="mla_prep_bwd", grid=(s // tm,),
        in_specs=[_full((1024, tm), lambda i: (0, i)), _full((1024, tm), lambda i: (0, i)),
                  _full((512, tm), lambda i: (0, i)),
                  _full((tm, 384), lambda i: (i, CQ_OFF // 384)), _full((tm, 256), lambda i: (i, CKV_OFF // 256)),
                  _full((1, 384), cst), _full((1, 256), cst),
                  _full((384, 1024), cst), _full((256, 1024), cst), _full((256, 512), cst)] + tabs,
        out_specs=[_full((tm, 384), row), _full((tm, 128), row), _full((tm, 256), row),
                   _full((384, 1024), cst), _full((256, 1024), cst), _full((256, 512), cst),
                   _full((1, 384), cst), _full((1, 256), cst)],
        out_shape=[jax.ShapeDtypeStruct((s, 384), BF16), jax.ShapeDtypeStruct((s, 128), BF16),
                   jax.ShapeDtypeStruct((s, 256), BF16),
                   jax.ShapeDtypeStruct((384, 1024), F32), jax.ShapeDtypeStruct((256, 1024), F32),
                   jax.ShapeDtypeStruct((256, 512), F32),
                   jax.ShapeDtypeStruct((1, 384), F32), jax.ShapeDtypeStruct((1, 256), F32)],
        compiler_params=_cparams(("arbitrary",)),
    )(dq, dk, dv, proj, proj, gq, gkv, w_uq, w_uk, w_uv, *q_tab, *k_tab)


def _dil_bwd(proj, g, do, lse, dd, tabs, d, name):
    s = proj.shape[0]
    nblk = s // Q_BLOCK
    per_seq = nblk // d

    def body(q_ref, k_ref, v_ref, do_ref, lse_ref, dd_ref, c_ref, sp_ref, sm_ref, dq_out, dk_out, dv_out,
             qf, kf, vf, dof, dq_acc, dk_acc, dv_acc):
        left = _left_mask()
        hms = (left, jnp.logical_not(left))
        qf[...] = q_ref[...].astype(F32)
        kf[...] = k_ref[...].astype(F32)
        vf[...] = v_ref[...].astype(F32)
        dof[...] = do_ref[...].astype(F32)
        dk_acc[...] = jnp.zeros_like(dk_acc)
        dv_acc[...] = jnp.zeros_like(dv_acc)

        def scores(b):
            rows, prow = _dil_rows(b, d, per_seq)
            qb, dob = qf[rows, :].astype(BF16), dof[rows, :].astype(BF16)
            kp, kc = kf[prow, :].astype(BF16), kf[rows, :].astype(BF16)
            vp, vc = vf[prow, :].astype(BF16), vf[rows, :].astype(BF16)
            zero = jnp.zeros_like(qb)
            out = []
            for hh in range(2):
                qm, dom = jnp.where(hms[hh], qb, zero), jnp.where(hms[hh], dob, zero)
                out.append((_dot_nt(qm, kp) * DIL_SCALE, _dot_nt(qm, kc) * DIL_SCALE,
                            _dot_nt(dom, vp), _dot_nt(dom, vc)))
            return out

        def finish(b, tiles):
            rows, prow = _dil_rows(b, d, per_seq)
            pmask, cur_ok = _band_masks((b % per_seq) > 0)
            qb, dob = qf[rows, :].astype(BF16), dof[rows, :].astype(BF16)
            kp, kc = kf[prow, :].astype(BF16), kf[rows, :].astype(BF16)
            lse_b, dd_b = lse_ref[rows, :], dd_ref[rows, :]
            zero = jnp.zeros_like(qb)
            dq = jnp.zeros((Q_BLOCK, LANES), F32)
            dkp = jnp.zeros((Q_BLOCK, LANES), F32)
            dkc = jnp.zeros((Q_BLOCK, LANES), F32)
            dvp = jnp.zeros((Q_BLOCK, LANES), F32)
            dvc = jnp.zeros((Q_BLOCK, LANES), F32)
            for hh in range(2):
                hm = hms[hh]
                qm, dom = jnp.where(hm, qb, zero), jnp.where(hm, dob, zero)
                lse_h, dd_h = _expand_half(lse_b, hh, left), _expand_half(dd_b, hh, left)
                sp, sc, dpp, dpc = tiles[hh]
                pp = jnp.where(pmask, jnp.exp(sp - lse_h), 0.0)
                pc = jnp.where(cur_ok, jnp.exp(sc - lse_h), 0.0)
                dsp = (pp * (dpp - dd_h) * DIL_SCALE).astype(BF16)
                dsc = (pc * (dpc - dd_h) * DIL_SCALE).astype(BF16)
                dq = dq + _dot(dsp, jnp.where(hm, kp, zero)) + _dot(dsc, jnp.where(hm, kc, zero))
                dkp, dkc = dkp + _dot_tn(dsp, qm), dkc + _dot_tn(dsc, qm)
                dvp, dvc = dvp + _dot_tn(pp.astype(BF16), dom), dvc + _dot_tn(pc.astype(BF16), dom)
            dq_acc[rows, :] = dq
            dk_acc[prow, :] += dkp
            dv_acc[prow, :] += dvp
            dk_acc[rows, :] += dkc
            dv_acc[rows, :] += dvc

        def step(t, _):
            tiles = [scores(DIL_BLOCKS_PER_STEP * t + u) for u in range(DIL_BLOCKS_PER_STEP)]
            for u in range(DIL_BLOCKS_PER_STEP):
                finish(DIL_BLOCKS_PER_STEP * t + u, tiles[u])
            return 0

        lax.fori_loop(0, nblk // DIL_BLOCKS_PER_STEP, step, 0)
        dq_out[...] = _unrope(dq_acc[...], c_ref[...], sp_ref[...], sm_ref[...], 8).astype(BF16)
        dk_out[...] = _unrope(dk_acc[...], c_ref[...], sp_ref[...], sm_ref[...], 8).astype(BF16)
        dv_out[...] = dv_acc[...].astype(BF16)

    blk = lambda off: _full((s, 128), lambda p, off=off: (0, off + p))
    tab = _full((s, 128), lambda p: (0, 0))
    qo, ko, vo = ((off + 512 * g) // 128 for off in (Q_OFF, K_OFF, V_OFF))
    return pl.pallas_call(
        body, name=name, grid=(4,),
        in_specs=[blk(qo), blk(ko), blk(vo), blk(0), blk(0), blk(0), tab, tab, tab],
        out_specs=[blk(0), blk(0), blk(0)],
        out_shape=[jax.ShapeDtypeStruct((s, 512), BF16)] * 3,
        scratch_shapes=[pltpu.VMEM((s, 128), F32)] * 7,
        compiler_params=_cparams(("parallel",)),
    )(proj, proj, proj, do, lse, dd, *tabs)


def _chip_copies(ins, outs, sems, outgoing):
    x, y, c, chips = _place()
    myq = 2 * x + y
    n = len(ins)

    def chunk(a, j):
        q = 2 * chips[j][0] + chips[j][1]
        return _remote(ins[a].at[q], outs[a].at[myq if outgoing else q], sems, 2 * a, j, (*chips[j], c))

    if outgoing is None:
        return [pltpu.make_async_copy(ins[a].at[myq], outs[a].at[myq], sems.at[2 * a, 3]) for a in range(n)]
    return [chunk(a, j) for j in range(3) for a in range(n)]


def _dh_bwd(dproj, w_pad, x, gpre, dy, pairs):
    s = x.shape[0]
    tm, tk = min(512, s), 1408
    ni, nk = s // tm, IN_PAD // tk
    n = len(pairs)

    def body(*refs):
        dp_ref, w_ref, x_ref, g_ref, dy_ref = refs[:5]
        ins, gx_out, dg_out, outs = refs[5:5 + n], refs[5 + n], refs[6 + n], refs[7 + n:7 + 2 * n]
        acc, sems = refs[7 + 2 * n], refs[8 + 2 * n]
        i, kk = pl.program_id(0), pl.program_id(1)

        @pl.when(jnp.logical_and(i == 0, kk == 0))
        def _():
            dg_out[...] = jnp.zeros_like(dg_out)
            for cp in _chip_copies(ins, outs, sems, None) + _chip_copies(ins, outs, sems, True):
                cp.start()

        @pl.when(kk == 0)
        def _():
            acc[...] = jnp.zeros_like(acc)

        acc[...] += _dot_nt(dp_ref[...], w_ref[...])

        @pl.when(kk == nk - 1)
        def _():
            g = g_ref[...]
            _, xh, r = _rms(x_ref[...], g)
            dh = acc[...]
            dg_out[...] += jnp.sum(dh * xh, axis=0, keepdims=True)
            gx_out[...] = dy_ref[...] + _rms_bwd(dh, g, xh, r)

        @pl.when(jnp.logical_and(i == ni - 1, kk == nk - 1))
        def _():
            for cp in _chip_copies(ins, outs, sems, False):
                cp.wait_recv()
            for cp in _chip_copies(ins, outs, sems, True):
                cp.wait_send()
            for cp in _chip_copies(ins, outs, sems, None):
                cp.wait()

    row = lambda i, k: (i, 0)
    hbm = pl.BlockSpec(memory_space=pl.ANY)
    res = pl.pallas_call(
        body, name="dh_bwd", grid=(ni, nk),
        in_specs=[_full((tm, tk), lambda i, k: (i, k)), _full((1024, tk), lambda i, k: (0, k)),
                  _full((tm, 1024), row), _full((1, 1024), lambda i, k: (0, 0)), _full((tm, 1024), row)] + [hbm] * n,
        out_specs=[_full((tm, 1024), row), _full((1, 1024), lambda i, k: (0, 0))] + [hbm] * n,
        out_shape=[jax.ShapeDtypeStruct((s, 1024), F32), jax.ShapeDtypeStruct((1, 1024), F32)]
        + [jax.ShapeDtypeStruct(a.shape, a.dtype) for a in pairs],
        scratch_shapes=[pltpu.VMEM((tm, 1024), F32), pltpu.SemaphoreType.DMA((2 * n, 4))],
        compiler_params=_cparams(("arbitrary", "arbitrary")),
    )(dproj, w_pad, x, gpre, dy, *pairs)
    return res[0], res[1], res[2:]


def _dw_in(h, dproj):
    s = h.shape[0]
    ts, tn = min(1024, s), 768
    ns = s // ts

    def body(h_ref, dp_ref, o_ref, acc):
        k = pl.program_id(1)

        @pl.when(k == 0)
        def _():
            acc[...] = jnp.zeros_like(acc)

        acc[...] += _dot_tn(h_ref[...], dp_ref[...])

        @pl.when(k == ns - 1)
        def _():
            o_ref[...] = acc[...].astype(BF16)

    return pl.pallas_call(
        body, name="dw_in", grid=(IN_PAD // tn, ns),
        in_specs=[_full((ts, 1024), lambda j, k: (k, 0)), _full((ts, tn), lambda j, k: (k, j))],
        out_specs=_full((1024, tn), lambda j, k: (0, j)),
        out_shape=jax.ShapeDtypeStruct((1024, IN_PAD), BF16),
        scratch_shapes=[pltpu.VMEM((1024, tn), F32)],
        compiler_params=_cparams(("parallel", "arbitrary")),
    )(h, dproj)


def _remote(src, dst, sems, row, k, to):
    return pltpu.make_async_remote_copy(src_ref=src, dst_ref=dst, send_sem=sems.at[row, k], recv_sem=sems.at[row + 1, k],
                                        device_id=to, device_id_type=pl.DeviceIdType.MESH)


def _place():
    x, y, c = lax.axis_index("x"), lax.axis_index("y"), lax.axis_index("c")
    return x, y, c, [(1 - x, y), (x, 1 - y), (1 - x, 1 - y)]


def _gather_weights(arrays):
    n = len(arrays)

    def body(*refs):
        ins, outs, sems = refs[:n], refs[n:2 * n], refs[2 * n]
        x, y, c, chips = _place()
        me, sib = (x, y, c), (x, y, 1 - c)
        idx = lambda p: 4 * p[0] + 2 * p[1] + p[2]

        def copy(a, k, block, to, from_input=False):
            src = ins[a] if from_input else outs[a].at[idx(block)]
            return _remote(src, outs[a].at[idx(block)], sems, 2 * a, k, to)

        own = [pltpu.make_async_copy(ins[a], outs[a].at[idx(me)], sems.at[2 * a, 7]) for a in range(n)]
        first = [copy(a, 0, me, sib, True) for a in range(n)]
        first += [copy(a, 1 + j, me, (*chip, c), True) for j, chip in enumerate(chips) for a in range(n)]
        for cp in own + first:
            cp.start()
        passed = []
        for j, chip in enumerate(chips):
            for a in range(n):
                copy(a, 1 + j, (*chip, c), me).wait_recv()
                passed.append(copy(a, 4 + j, (*chip, c), sib))
                passed[-1].start()
        for a in range(n):
            copy(a, 0, sib, me).wait_recv()
        for j, chip in enumerate(chips):
            for a in range(n):
                copy(a, 4 + j, (*chip, 1 - c), me).wait_recv()
        for cp in first + passed:
            cp.wait_send()
        for cp in own:
            cp.wait()

    hbm = pl.BlockSpec(memory_space=pl.ANY)
    return pl.pallas_call(
        body, name="gather_weights", in_specs=[hbm] * n, out_specs=[hbm] * n,
        out_shape=[jax.ShapeDtypeStruct((N_DEV,) + a.shape, a.dtype) for a in arrays],
        scratch_shapes=[pltpu.SemaphoreType.DMA((2 * n, N_DEV))],
    )(*arrays)


def _pair_exchange(chunks):
    n = len(chunks)

    def body(*refs):
        ins, outs, sems = refs[:n], refs[n:2 * n], refs[2 * n]
        x, y, c, _ = _place()
        sent = [_remote(ins[a].at[2 * q + (1 - c)], outs[a].at[q], sems, 2 * a, q, (x, y, 1 - c))
                for a in range(n) for q in range(4)]
        for cp in sent:
            cp.start()
        for cp in sent:
            cp.wait_recv()
        for cp in sent:
            cp.wait_send()

    hbm = pl.BlockSpec(memory_space=pl.ANY)
    return pl.pallas_call(
        body, name="pair_exchange", in_specs=[hbm] * n, out_specs=[hbm] * n,
        out_shape=[jax.ShapeDtypeStruct((4,) + a.shape[1:], a.dtype) for a in chunks],
        scratch_shapes=[pltpu.SemaphoreType.DMA((2 * n, 4))],
    )(*chunks)


def _pair_sum(core, chunks, recv, name, tr):
    _, rows, cols = chunks.shape

    def body(c_ref, a_ref, b_ref, o_ref):
        o_ref[...] = (a_ref[...].astype(F32) + b_ref[...].astype(F32)).astype(BF16)

    blk = lambda f: _full((1, tr, cols), f)
    return pl.pallas_call(
        body, name=name, out_shape=jax.ShapeDtypeStruct((4, rows, cols), BF16),
        grid_spec=pltpu.PrefetchScalarGridSpec(
            num_scalar_prefetch=1, grid=(4, rows // tr),
            in_specs=[blk(lambda q, i, c: (2 * q + c[0], i, 0)), blk(lambda q, i, c: (q, i, 0))],
            out_specs=blk(lambda q, i, c: (q, i, 0))),
        compiler_params=_cparams(("parallel", "parallel")),
    )(core, chunks, recv)


def _packet_exchange(packet):
    def body(pk, pk_out, sems):
        x, y, c, _ = _place()
        me = 4 * x + 2 * y + c
        flip = lambda v, b: (1 - v) if b else v

        def small(j, outgoing):
            peer = (flip(x, (j >> 2) & 1), flip(y, (j >> 1) & 1), flip(c, j & 1))
            slot = me if outgoing else 4 * peer[0] + 2 * peer[1] + peer[2]
            return _remote(pk, pk_out.at[slot], sems, 0, j, peer)

        own = pltpu.make_async_copy(pk, pk_out.at[me], sems.at[0, 0])
        sent = [small(j, True) for j in range(1, N_DEV)]
        for cp in [own] + sent:
            cp.start()
        for j in range(1, N_DEV):
            small(j, False).wait_recv()
        for cp in sent:
            cp.wait_send()
        own.wait()

    hbm = pl.BlockSpec(memory_space=pl.ANY)
    return pl.pallas_call(
        body, name="packet_exchange", in_specs=[hbm], out_specs=hbm,
        out_shape=jax.ShapeDtypeStruct((N_DEV,) + packet.shape, packet.dtype),
        scratch_shapes=[pltpu.SemaphoreType.DMA((2, N_DEV))],
    )(packet)


def _adam_math(w, g, m, v):
    m = ADAM_B1 * m + (1.0 - ADAM_B1) * g
    v = ADAM_B2 * v + (1.0 - ADAM_B2) * (g * g)
    m_hat = m / (1.0 - ADAM_B1 ** ADAM_STEP)
    v_hat = v / (1.0 - ADAM_B2 ** ADAM_STEP)
    delta = -ADAM_LR * (m_hat / (jnp.sqrt(v_hat) + ADAM_EPS) + ADAM_WD * w)
    return delta, m, v


def _adam(recv, w, m, v, name, tr):
    _, rows, cols = w.shape

    def body(r_ref, w_ref, m_ref, v_ref, g_out, d_out, m_out, v_out):
        g = r_ref[0].astype(F32)
        for k in range(1, 4):
            g = g + r_ref[k].astype(F32)
        g_out[0] = g
        d_out[0], m_out[0], v_out[0] = _adam_math(w_ref[0], g, m_ref[0], v_ref[0])

    blk = _full((1, tr, cols), lambda i: (0, i, 0))
    return pl.pallas_call(
        body, name=name, grid=(rows // tr,),
        in_specs=[_full((4, tr, cols), lambda i: (0, i, 0)), blk, blk, blk],
        out_specs=[blk] * 4,
        out_shape=[jax.ShapeDtypeStruct(w.shape, F32)] * 4,
        compiler_params=_cparams(("parallel",)),
    )(recv, w, m, v)


def _adam_gains(recv, gains, gains_m, gains_v):
    def body(*refs):
        r_ref, w, m, v = refs[0], refs[1:5], refs[5:9], refs[9:13]
        g_out, d_out, m_out, v_out, loss_out = refs[13:17], refs[17:21], refs[21:25], refs[25:29], refs[29]
        tot = r_ref[0:1, :]
        for k in range(1, N_DEV):
            tot = tot + r_ref[k:k + 1, :]
        for t in range(4):
            g = tot[:, GAIN_OFFS[t]:GAIN_OFFS[t] + GAIN_WIDTHS[t]]
            g_out[t][...] = g
            d_out[t][...], m_out[t][...], v_out[t][...] = _adam_math(w[t][...], g, m[t][...], v[t][...])
        loss_out[...] = tot[:, LOSS_OFF:LOSS_OFF + LANES]

    shapes = [jax.ShapeDtypeStruct((1, n), F32) for n in GAIN_WIDTHS]
    return pl.pallas_call(
        body, name="adam_gains", out_shape=shapes * 4 + [jax.ShapeDtypeStruct((1, LANES), F32)],
    )(recv, *gains, *gains_m, *gains_v)


def _local_step(x, positions, gains, weights, target):
    gpre, gq, gkv, gpost = gains
    w_pad, w_uq, w_uk, w_uv, wp_mla, wp_dil, w_out = weights
    q_tab, k_tab, d_tab = _rope_tables(positions)

    proj, h = _inproj(x, gpre, w_pad, d_tab)
    q, k, v, qt, kt, vt = _mla_prep(proj, gq, gkv, w_uq, w_uk, w_uv, q_tab, k_tab)
    o_mla, ot_mla, lse_mla = _mla_fwd(q, k, vt)

    od, lsed = [], []
    for g, d in enumerate(DIL_DILATIONS):
        o_g, lse_g = _dil_fwd(proj, g, d, "dil_fwd_%d" % g)
        od.append(o_g)
        lsed.append(lse_g)

    merged, ya, yd, o_dil, lse_dil = _merge_fwd(proj, o_mla, od, lsed, wp_mla, wp_dil)
    do, dy, loss, dgpost = _out_loss(merged, w_out, x, target, gpost)

    (dz_mla, dz_dil, dg_mla, dg_dil, do_mla, do_dil, dot_mla, dd_dil, dw_out, dwp_mla, dwp_dil) = _merge_bwd(
        do, w_out, merged, proj, ya, yd, o_mla, o_dil, wp_mla, wp_dil)

    dq, dk, dv = _mla_bwd(q, qt, k, kt, v, do_mla, dot_mla, ot_mla, lse_mla)
    dcq, dkr, dckv, dw_uq, dw_uk, dw_uv, dgq, dgkv = _mla_prep_bwd(dq, dk, dv, proj, gq, gkv, w_uq, w_uk, w_uv,
                                                                   q_tab, k_tab)

    dqs, dks, dvs = [], [], []
    for g, d in enumerate(DIL_DILATIONS):
        dq_g, dk_g, dv_g = _dil_bwd(proj, g, do_dil, lse_dil, dd_dil, d_tab, d, "dil_bwd_%d" % g)
        dqs.append(dq_g)
        dks.append(dk_g)
        dvs.append(dv_g)

    dproj = jnp.concatenate([dz_mla, dz_dil, dg_mla, dg_dil] + dqs + dks + dvs + [dcq, dkr, dckv], axis=1)
    dw_in = _dw_in(h, dproj)
    return loss, (dproj, dy), (dgq, dgkv, dgpost), (dw_in, dw_uq, dw_uk, dw_uv, dwp_mla, dwp_dil, dw_out)


ADAM_ROWS = (256, 384, 256, 512, 512, 128)
PAIR_ROWS = (512, 384, 256, 512, 512, 128)


def kernel(x, positions, pre_norm_g, w_in, q_norm_g, w_uq, kv_norm_g, w_ukv, w_proj_mla, w_proj_dil, w_out, post_norm_g, loss_target, m_pre_norm_g, m_w_in, m_q_norm_g, m_w_uq, m_kv_norm_g, m_w_ukv, m_w_proj_mla, m_w_proj_dil, m_w_out, m_post_norm_g, v_pre_norm_g, v_w_in, v_q_norm_g, v_w_uq, v_kv_norm_g, v_w_ukv, v_w_proj_mla, v_w_proj_dil, v_w_out, v_post_norm_g):
    big_w = (w_in, w_uq, w_ukv, w_proj_mla, w_proj_dil, w_out)
    big_m = (m_w_in, m_w_uq, m_w_ukv, m_w_proj_mla, m_w_proj_dil, m_w_out)
    big_v = (v_w_in, v_w_uq, v_w_ukv, v_w_proj_mla, v_w_proj_dil, v_w_out)
    gains = (pre_norm_g, q_norm_g, kv_norm_g, post_norm_g)
    gains_m = (m_pre_norm_g, m_q_norm_g, m_kv_norm_g, m_post_norm_g)
    gains_v = (v_pre_norm_g, v_q_norm_g, v_kv_norm_g, v_post_norm_g)

    gathered = _gather_weights([w[0].astype(BF16) for w in big_w])
    weights = _assemble_weights(*gathered)

    loss, (dproj, dy), (dgq, dgkv, dgpost), dweights = _local_step(x[0], positions[0], gains, weights,
                                                                   loss_target[0])

    chunks = _grad_chunks(*dweights)
    from_sibling = _pair_exchange(chunks)
    core = lax.axis_index("c").astype(jnp.int32).reshape(1)
    pairs = [_pair_sum(core, chunks[t], from_sibling[t], "pair_sum_%d" % t, PAIR_ROWS[t]) for t in range(6)]
    grad_x, dgpre, received = _dh_bwd(dproj, weights[0], x[0], pre_norm_g, dy, pairs)
    packet = _packet_exchange(jnp.concatenate([dgpre, dgq, dgkv, dgpost, loss[0:1]], axis=1))

    big = [_adam(received[t], big_w[t], big_m[t], big_v[t], "adam_%d" % t, ADAM_ROWS[t]) for t in range(6)]
    small = _adam_gains(packet.reshape(N_DEV, PACKET), gains, gains_m, gains_v)

    def interleave(kind):
        s_pre, s_q, s_kv, s_post = small[4 * kind:4 * kind + 4]
        b_in, b_uq, b_ukv, b_pm, b_pd, b_out = (big[t][kind] for t in range(6))
        return [s_pre, b_in, s_q, b_uq, s_kv, b_ukv, b_pm, b_pd, b_out, s_post]

    return (small[16][0, 0], grad_x[None], *interleave(0), *interleave(1), *interleave(2), *interleave(3))
```

```python
import numpy as np
import jax
import jax.numpy as jnp
from jax import lax
from jax.experimental import pallas as pl
from jax.experimental.pallas import tpu as pltpu

F32 = jnp.float32
BF16 = jnp.bfloat16

D_MODEL = 1024
NORM_EPS = 1e-6
ROPE_THETA = 500000.0
N_DEV = 8
LANES = 128
NEG = -1e30

MLA_HEADS = 8
MLA_Q_RANK = 384
MLA_KV_RANK = 256
MLA_SCALE = 96.0 ** -0.5
LOG2E = 1.4426950408889634
MLA_QSCALE = MLA_SCALE * LOG2E
MLA_FWD_Q_PER_K = 2
MLA_BWD_Q_PER_K = 2
DIL_DILATIONS = (1, 4, 16)
DIL_SCALE = 0.125
Q_BLOCK = 128
DIL_BLOCKS_PER_STEP = 4

Z_MLA, Z_DIL, G_MLA, G_DIL = 0, 512, 1024, 2048
Q_OFF, K_OFF, V_OFF = 3072, 4608, 6144
CQ_OFF, KR_OFF, CKV_OFF, IN_PAD = 7680, 8064, 8192, 8448
IN_WIDTH = 8352
SHARD_W = IN_WIDTH // 8
IN_SEGS = ((0, 384, CQ_OFF), (384, 256, CKV_OFF), (640, 32, KR_OFF), (672, 1536, Q_OFF), (2208, 1536, K_OFF),
           (3744, 1536, V_OFF), (5280, 512, Z_MLA), (5792, 512, Z_DIL), (6304, 1024, G_MLA), (7328, 1024, G_DIL))

GAIN_OFFS = (0, 1024, 1408, 1664)
GAIN_WIDTHS = (1024, 384, 256, 1024)
LOSS_OFF, PACKET = 2688, 2816

ADAM_LR, ADAM_B1, ADAM_B2, ADAM_EPS, ADAM_WD, ADAM_STEP = 0.001, 0.9, 0.999, 1e-08, 0.01, 10

VMEM_LIMIT_MB = 56


def _cparams(sem=None, vmem_mb=VMEM_LIMIT_MB):
    return pltpu.CompilerParams(dimension_semantics=sem, vmem_limit_bytes=vmem_mb * 1024 * 1024)


def _dot(a, b):
    return jnp.dot(a, b, preferred_element_type=F32)


def _dot_nt(a, b):
    return lax.dot_general(a, b, (((1,), (1,)), ((), ())), preferred_element_type=F32)


def _dot_tn(a, b):
    return lax.dot_general(a, b, (((0,), (0,)), ((), ())), preferred_element_type=F32)


def _tile_lanes(t, width):
    return t if width == t.shape[1] else jnp.tile(t, (1, width // t.shape[1]))


def _rope(x, c, sp, sm, a):
    n = x.shape[1]
    return x * c + pltpu.roll(x, a, 1) * sp + pltpu.roll(x, n - a, 1) * sm


def _unrope(dy, c, sp, sm, a):
    n = dy.shape[1]
    return dy * c + pltpu.roll(dy * sp, n - a, 1) + pltpu.roll(dy * sm, a, 1)


def _sigmoid(z):
    return 1.0 / (1.0 + jnp.exp(-z))


def _left_mask():
    return lax.broadcasted_iota(jnp.int32, (1, LANES), 1) < 64


def _expand_half(x, hh, left):
    r = pltpu.roll(x, 64, 1)
    return jnp.where(left, x, r) if hh == 0 else jnp.where(left, r, x)


def _rms(xv, g):
    r = lax.rsqrt(jnp.mean(xv * xv, axis=-1, keepdims=True) + NORM_EPS)
    xh = xv * r
    return xh * g, xh, r


def _rms_bwd(dout, g, xh, r):
    dxh = dout * g
    return r * (dxh - xh * jnp.mean(dxh * xh, axis=-1, keepdims=True))


def _full(shape, index_map):
    return pl.BlockSpec(shape, index_map)


def _w_in_pieces():
    out = []
    for s, n, off in sorted(IN_SEGS, key=lambda t: t[2]):
        c = s
        while c < s + n:
            k = c // SHARD_W
            e = min(s + n, (k + 1) * SHARD_W)
            out.append((k, c - k * SHARD_W, e - c, off + (c - s)))
            c = e
    return out


def _assemble_w_in(g):
    parts, cur = [], 0
    for k, a, w, off in _w_in_pieces():
        if off > cur:
            parts.append(jnp.zeros((D_MODEL, off - cur), g.dtype))
        parts.append(g[k, :, a:a + w])
        cur = off + w
    if cur < IN_PAD:
        parts.append(jnp.zeros((D_MODEL, IN_PAD - cur), g.dtype))
    return jnp.concatenate(parts, axis=1)


def _dw_in_chunks(dw):
    chunks = []
    for dev in range(N_DEV):
        mine = sorted((p for p in _w_in_pieces() if p[0] == dev), key=lambda p: p[1])
        chunks.append(jnp.concatenate([dw[:, off:off + w] for k, a, w, off in mine], axis=1))
    return jnp.stack(chunks)


def _assemble_weights(g_in, g_uq, g_ukv, g_pm, g_pd, g_out):
    w_uq_pad = jnp.pad(g_uq.transpose(1, 0, 2), ((0, 0), (0, 0), (0, 32))).reshape(384, 1024)
    ukv = g_ukv.transpose(1, 0, 2)
    w_uk_pad = jnp.pad(ukv[:, :, :64], ((0, 0), (0, 0), (0, 64))).reshape(256, 1024)
    w_uv = ukv[:, :, 64:].reshape(256, 512)
    wp_mla = g_pm.transpose(1, 0, 2).reshape(512, 1024)
    wp_dil = g_pd.transpose(1, 0, 2).reshape(512, 1024)
    return _assemble_w_in(g_in), w_uq_pad, w_uk_pad, w_uv, wp_mla, wp_dil, g_out.reshape(1024, 1024)


def _grad_chunks(dw_in_pad, dw_uq_pad, dw_uk_pad, dw_uv, dwp_mla, dwp_dil, dw_out):
    a = _dw_in_chunks(dw_in_pad)
    b = dw_uq_pad.reshape(384, 8, 128)[:, :, :96].transpose(1, 0, 2)
    c = jnp.concatenate([dw_uk_pad.reshape(256, 8, 128)[:, :, :64], dw_uv.reshape(256, 8, 64)], axis=2)
    c = c.transpose(1, 0, 2)
    d = dwp_mla.reshape(512, N_DEV, 128).transpose(1, 0, 2)
    e = dwp_dil.reshape(512, N_DEV, 128).transpose(1, 0, 2)
    f = dw_out.reshape(N_DEV, 128, 1024)
    return [t.astype(BF16) for t in (a, b, c, d, e, f)]


def _lane_consts(freqs, half, first, period):
    rel = (np.arange(LANES) % period) - first
    rot = (rel >= 0) & (rel < 2 * half)
    freq = np.where(rot, freqs[np.clip(rel, 0, 2 * half - 1) % half], 0.0).astype(np.float32)
    x1 = (rot & (rel < half)).astype(np.float32)
    x2 = (rot & (rel >= half)).astype(np.float32)
    return freq[None, :], x1[None, :], x2[None, :]


def _rope_tables(pos):
    p = pos.astype(F32)[:, None]
    inv_m = np.float32(ROPE_THETA) ** (-(np.arange(0, 32, 2, dtype=np.float32) / np.float32(32)))
    inv_d = np.float32(ROPE_THETA) ** (-(np.arange(0, 16, 2, dtype=np.float32) / np.float32(16)))
    lane = np.arange(LANES)
    tabs = []
    for freqs, half, first, period, keep in ((inv_m, 16, 64, 128, lane < 96), (inv_m, 16, 0, 128, lane < 32),
                                              (inv_d, 8, 0, 64, lane >= 0)):
        freq, x1, x2 = _lane_consts(freqs, half, first, period)
        ang = p * freq
        sin = jnp.sin(ang)
        tabs.append((jnp.cos(ang) * keep.astype(np.float32)[None, :], sin * x2, sin * (-x1)))
    return tuple(tabs)


def _inproj(x, gpre, w_pad, d_tab):
    s = x.shape[0]
    tm, tn = min(1024, s), 768
    rope_lo, rope_hi = Q_OFF // tn, V_OFF // tn

    def body(x_ref, g_ref, w_ref, c_ref, sp_ref, sm_ref, o_ref, h_ref):
        j = pl.program_id(1)

        @pl.when(j == 0)
        def _():
            hv, _, _ = _rms(x_ref[...], g_ref[...])
            h_ref[...] = hv.astype(BF16)

        acc = _dot(h_ref[...], w_ref[...])
        is_rope = jnp.logical_and(j >= rope_lo, j < rope_hi)

        @pl.when(is_rope)
        def _():
            o_ref[...] = _rope(acc, _tile_lanes(c_ref[...], tn), _tile_lanes(sp_ref[...], tn),
                               _tile_lanes(sm_ref[...], tn), 8).astype(BF16)

        @pl.when(jnp.logical_not(is_rope))
        def _():
            o_ref[...] = acc.astype(BF16)

    row = lambda i, j: (i, 0)
    return pl.pallas_call(
        body, name="inproj", grid=(s // tm, IN_PAD // tn),
        in_specs=[_full((tm, D_MODEL), row), _full((1, D_MODEL), lambda i, j: (0, 0)),
                  _full((D_MODEL, tn), lambda i, j: (0, j)),
                  _full((tm, LANES), row), _full((tm, LANES), row), _full((tm, LANES), row)],
        out_specs=[_full((tm, tn), lambda i, j: (i, j)), _full((tm, D_MODEL), row)],
        out_shape=[jax.ShapeDtypeStruct((s, IN_PAD), BF16), jax.ShapeDtypeStruct((s, D_MODEL), BF16)],
        compiler_params=_cparams(("parallel", "arbitrary")),
    )(x, gpre, w_pad, *d_tab)


def _mla_prep(proj, gq, gkv, w_uq, w_uk, w_uv, q_tab, k_tab):
    s = proj.shape[0]
    tm = min(512, s)

    def body(cq_ref, kr_ref, ckv_ref, gq_ref, gkv_ref, wq_ref, wk_ref, wv_ref,
             qc, qsp, qsm, kc, ksp, ksm, q_out, k_out, v_out, qt_out, kt_out, vt_out):
        cqn, _, _ = _rms(cq_ref[...].astype(F32), gq_ref[...])
        q = _dot(cqn.astype(BF16), wq_ref[...])
        q = _rope(q, _tile_lanes(qc[...], 1024), _tile_lanes(qsp[...], 1024), _tile_lanes(qsm[...], 1024), 16)
        q = q * MLA_QSCALE
        q_out[...] = q.astype(BF16)
        qt_out[...] = q.T.astype(BF16)
        ckvn, _, _ = _rms(ckv_ref[...].astype(F32), gkv_ref[...])
        ckvn = ckvn.astype(BF16)
        kr = _rope(kr_ref[...].astype(F32), kc[...], ksp[...], ksm[...], 16)
        k = _dot(ckvn, wk_ref[...]) + _tile_lanes(pltpu.roll(kr, 64, 1), 1024)
        k_out[...] = k.astype(BF16)
        kt_out[...] = k.T.astype(BF16)
        v = _dot(ckvn, wv_ref[...])
        v_out[...] = v.astype(BF16)
        vt_out[...] = v.T.astype(BF16)

    row = lambda i: (i, 0)
    col = lambda i: (0, i)
    cst = lambda i: (0, 0)
    tabs = [_full((tm, LANES), row)] * 6
    return pl.pallas_call(
        body, name="mla_prep", grid=(s // tm,),
        in_specs=[_full((tm, 384), lambda i: (i, CQ_OFF // 384)), _full((tm, 128), lambda i: (i, KR_OFF // 128)),
                  _full((tm, 256), lambda i: (i, CKV_OFF // 256)), _full((1, 384), cst), _full((1, 256), cst),
                  _full((384, 1024), cst), _full((256, 1024), cst), _full((256, 512), cst)] + tabs,
        out_specs=[_full((tm, 1024), row), _full((tm, 1024), row), _full((tm, 512), row),
                   _full((1024, tm), col), _full((1024, tm), col), _full((512, tm), col)],
        out_shape=[jax.ShapeDtypeStruct((s, 1024), BF16), jax.ShapeDtypeStruct((s, 1024), BF16),
                   jax.ShapeDtypeStruct((s, 512), BF16), jax.ShapeDtypeStruct((1024, s), BF16),
                   jax.ShapeDtypeStruct((1024, s), BF16), jax.ShapeDtypeStruct((512, s), BF16)],
        compiler_params=_cparams(("parallel",)),
    )(proj, proj, proj, gq, gkv, w_uq, w_uk, w_uv, *q_tab, *k_tab)


def _mla_fwd(q, k, vt):
    s = q.shape[0]
    tk = min(256, s)
    ratio = MLA_FWD_Q_PER_K if s >= MLA_FWD_Q_PER_K * tk else 1
    tq = ratio * tk
    nq = s // tq

    def body(q_ref, k_ref, vt_ref, o_ref, ot_ref, lse_ref):
        krow = lax.broadcasted_iota(jnp.int32, (tk, tq), 0)
        qcol = lax.broadcasted_iota(jnp.int32, (tk, tq), 1)

        def q_step(i, _):
            r0 = pl.multiple_of(i * tq, tq)
            qs = [q_ref[pl.ds(r0, tq), hh * 128:(hh + 1) * 128] for hh in range(2)]

            def scores(j):
                c0 = pl.multiple_of(j * tk, tk)
                return tuple(_dot_nt(k_ref[pl.ds(c0, tk), hh * 128:(hh + 1) * 128], qs[hh])
                             for hh in range(2))

            def update(j, sts, stats, masked):
                c0 = pl.multiple_of(j * tk, tk)
                new = []
                causal = (krow + (c0 - r0)) <= qcol
                for hh in range(2):
                    m, l, acc = stats[hh]
                    st = jnp.where(causal, sts[hh], NEG) if masked else sts[hh]
                    m_new = jnp.maximum(m, jnp.max(st, axis=0, keepdims=True))
                    alpha = jnp.exp2(m - m_new)
                    p = jnp.exp2(st - m_new)
                    l = alpha * l + jnp.sum(p, axis=0, keepdims=True)
                    acc = acc * alpha + _dot(vt_ref[hh * 64:(hh + 1) * 64, pl.ds(c0, tk)], p.astype(BF16))
                    new.append((m_new, l, acc))
                return tuple(new)

            def kv_step(j, carry):
                sts, stats = carry
                nxt = scores(j + 1)
                return nxt, update(j, sts, stats, False)

            init = tuple((jnp.full((1, tq), NEG, F32), jnp.zeros((1, tq), F32), jnp.zeros((64, tq), F32))
                         for _ in range(2))
            sts, stats = lax.fori_loop(0, ratio * i, kv_step, (scores(0), init))
            for d in range(ratio):
                if d > 0:
                    sts = scores(ratio * i + d)
                stats = update(ratio * i + d, sts, stats, True)
            (ma, la, acca), (mb, lb, accb) = stats
            ot = jnp.concatenate([acca / la, accb / lb], axis=0)
            ot_ref[:, pl.ds(r0, tq)] = ot.astype(BF16)
            o_ref[pl.ds(r0, tq), :] = ot.T.astype(BF16)
            lse_ref[:, pl.ds(r0, tq)] = jnp.concatenate(
                [ma + jnp.log2(la), mb + jnp.log2(lb), jnp.zeros((6, tq), F32)], axis=0)
            return 0

        lax.fori_loop(0, nq, q_step, 0)

    return pl.pallas_call(
        body, name="mla_fwd", grid=(4,),
        in_specs=[_full((s, 256), lambda p: (0, p)), _full((s, 256), lambda p: (0, p)),
                  _full((128, s), lambda p: (p, 0))],
        out_specs=[_full((s, 128), lambda p: (0, p)), _full((128, s), lambda p: (p, 0)),
                   _full((8, s), lambda p: (p, 0))],
        out_shape=[jax.ShapeDtypeStruct((s, 512), BF16), jax.ShapeDtypeStruct((512, s), BF16),
                   jax.ShapeDtypeStruct((32, s), F32)],
        compiler_params=_cparams(("parallel",)),
    )(q, k, vt)


def _band_masks(has_prev):
    r = lax.broadcasted_iota(jnp.int32, (Q_BLOCK, Q_BLOCK), 0)
    c = lax.broadcasted_iota(jnp.int32, (Q_BLOCK, Q_BLOCK), 1)
    return c >= r + jnp.where(has_prev, 0, Q_BLOCK), c <= r


def _dil_rows(b, d, per_seq):
    r, n = b // per_seq, b % per_seq
    start = r + (d * Q_BLOCK) * n
    prev = start - jnp.where(n > 0, d * Q_BLOCK, 0)
    if d == 1:
        return pl.ds(pl.multiple_of(start, Q_BLOCK), Q_BLOCK), pl.ds(pl.multiple_of(prev, Q_BLOCK), Q_BLOCK)
    return pl.ds(start, Q_BLOCK, stride=d), pl.ds(prev, Q_BLOCK, stride=d)


def _dil_fwd(proj, g, d, name):
    s = proj.shape[0]
    nblk = s // Q_BLOCK
    per_seq = nblk // d

    def body(q_ref, k_ref, v_ref, o_ref, lse_ref, qf, kf, vf, of, lf):
        left = _left_mask()
        hms = (left, jnp.logical_not(left))
        qf[...] = q_ref[...].astype(F32)
        kf[...] = k_ref[...].astype(F32)
        vf[...] = v_ref[...].astype(F32)

        def scores(b):
            rows, prow = _dil_rows(b, d, per_seq)
            qb = qf[rows, :].astype(BF16)
            kp, kc = kf[prow, :].astype(BF16), kf[rows, :].astype(BF16)
            out = []
            for hh in range(2):
                qm = jnp.where(hms[hh], qb, jnp.zeros_like(qb))
                out.append((_dot_nt(qm, kp) * DIL_SCALE, _dot_nt(qm, kc) * DIL_SCALE))
            return out

        def finish(b, tiles):
            rows, prow = _dil_rows(b, d, per_seq)
            pmask, cur_ok = _band_masks((b % per_seq) > 0)
            vp, vc = vf[prow, :].astype(BF16), vf[rows, :].astype(BF16)
            zero = jnp.zeros_like(vp)
            outs = []
            for hh in range(2):
                sp = jnp.where(pmask, tiles[hh][0], NEG)
                sc = jnp.where(cur_ok, tiles[hh][1], NEG)
                m = jnp.maximum(jnp.max(sp, axis=1, keepdims=True), jnp.max(sc, axis=1, keepdims=True))
                pp, pc = jnp.exp(sp - m), jnp.exp(sc - m)
                l = jnp.sum(pp, axis=1, keepdims=True) + jnp.sum(pc, axis=1, keepdims=True)
                acc = (_dot(pp.astype(BF16), jnp.where(hms[hh], vp, zero))
                       + _dot(pc.astype(BF16), jnp.where(hms[hh], vc, zero)))
                outs.append((acc / l, jnp.broadcast_to(m + jnp.log(l), (Q_BLOCK, LANES))))
            of[rows, :] = outs[0][0] + outs[1][0]
            lf[rows, :] = jnp.where(left, outs[0][1], outs[1][1])

        def step(t, _):
            tiles = [scores(DIL_BLOCKS_PER_STEP * t + u) for u in range(DIL_BLOCKS_PER_STEP)]
            for u in range(DIL_BLOCKS_PER_STEP):
                finish(DIL_BLOCKS_PER_STEP * t + u, tiles[u])
            return 0

        lax.fori_loop(0, nblk // DIL_BLOCKS_PER_STEP, step, 0)
        o_ref[...] = of[...].astype(BF16)
        lse_ref[...] = lf[...]

    blk = lambda off: _full((s, 128), lambda p, off=off: (0, off + p))
    qo, ko, vo = ((off + 512 * g) // 128 for off in (Q_OFF, K_OFF, V_OFF))
    return pl.pallas_call(
        body, name=name, grid=(4,),
        in_specs=[blk(qo), blk(ko), blk(vo)],
        out_specs=[blk(0), blk(0)],
        out_shape=[jax.ShapeDtypeStruct((s, 512), BF16), jax.ShapeDtypeStruct((s, 512), F32)],
        scratch_shapes=[pltpu.VMEM((s, 128), F32)] * 5,
        compiler_params=_cparams(("parallel",)),
    )(proj, proj, proj)


def _merge_fwd(proj, o_mla, od, lsed, wp_mla, wp_dil):
    s = proj.shape[0]
    tm = min(512, s)

    def body(zm_ref, zd_ref, gm_ref, gd_ref, om_ref, o0, o1, o2, l0, l1, l2, wm_ref, wd_ref,
             mg_out, ya_out, yd_out, odil_out, lse_out):
        la, lb, lc = l0[...], l1[...], l2[...]
        lmax = jnp.maximum(jnp.maximum(la, lb), lc)
        ea, eb, ec = jnp.exp(la - lmax), jnp.exp(lb - lmax), jnp.exp(lc - lmax)
        den = ea + eb + ec
        o_dil = (ea * o0[...].astype(F32) + eb * o1[...].astype(F32) + ec * o2[...].astype(F32)) / den
        o_dil = o_dil.astype(BF16)
        odil_out[...] = o_dil
        lse_out[...] = lmax + jnp.log(den)
        zm, zd = zm_ref[...].astype(F32), zd_ref[...].astype(F32)
        pa = (om_ref[...].astype(F32) * (zm * _sigmoid(zm))).astype(BF16)
        pd = (o_dil.astype(F32) * (zd * _sigmoid(zd))).astype(BF16)
        ya = _dot(pa, wm_ref[...])
        yd = _dot(pd, wd_ref[...])
        ya_out[...] = ya.astype(BF16)
        yd_out[...] = yd.astype(BF16)
        mg_out[...] = (_sigmoid(gm_ref[...].astype(F32)) * ya + _sigmoid(gd_ref[...].astype(F32)) * yd).astype(BF16)

    row = lambda i: (i, 0)
    cst = lambda i: (0, 0)
    r512 = _full((tm, 512), row)
    r1024 = _full((tm, 1024), row)
    return pl.pallas_call(
        body, name="merge_fwd", grid=(s // tm,),
        in_specs=[_full((tm, 512), lambda i: (i, Z_MLA // 512)), _full((tm, 512), lambda i: (i, Z_DIL // 512)),
                  _full((tm, 1024), lambda i: (i, G_MLA // 1024)), _full((tm, 1024), lambda i: (i, G_DIL // 1024)),
                  r512, r512, r512, r512, r512, r512, r512, _full((512, 1024), cst), _full((512, 1024), cst)],
        out_specs=[r1024, r1024, r1024, r512, r512],
        out_shape=[jax.ShapeDtypeStruct((s, 1024), BF16), jax.ShapeDtypeStruct((s, 1024), BF16),
                   jax.ShapeDtypeStruct((s, 1024), BF16), jax.ShapeDtypeStruct((s, 512), BF16),
                   jax.ShapeDtypeStruct((s, 512), F32)],
        compiler_params=_cparams(("parallel",)),
    )(proj, proj, proj, proj, o_mla, *od, *lsed, wp_mla, wp_dil)


def _out_loss(merged, w_out, x, target, gpost):
    s = x.shape[0]
    tm = min(512, s)

    def body(mg_ref, w_ref, x_ref, t_ref, g_ref, do_out, dy_out, loss_out, dg_out):
        i = pl.program_id(0)

        @pl.when(i == 0)
        def _():
            loss_out[...] = jnp.zeros_like(loss_out)
            dg_out[...] = jnp.zeros_like(dg_out)

        o = _dot(mg_ref[...], w_ref[...])
        g = g_ref[...]
        n, u, r = _rms(o, g)
        e = (x_ref[...] + n) - t_ref[...]
        loss_out[...] += 0.5 * jnp.sum(jnp.mean(e * e, axis=-1, keepdims=True))
        dy = e * (1.0 / D_MODEL)
        dy_out[...] = dy
        dg_out[...] += jnp.sum(dy * u, axis=0, keepdims=True)
        do_out[...] = _rms_bwd(dy, g, u, r).astype(BF16)

    row = lambda i: (i, 0)
    cst = lambda i: (0, 0)
    return pl.pallas_call(
        body, name="out_loss", grid=(s // tm,),
        in_specs=[_full((tm, 1024), row), _full((1024, 1024), cst), _full((tm, 1024), row), _full((tm, 1024), row),
                  _full((1, 1024), cst)],
        out_specs=[_full((tm, 1024), row), _full((tm, 1024), row), _full((8, LANES), cst), _full((1, 1024), cst)],
        out_shape=[jax.ShapeDtypeStruct((s, 1024), BF16), jax.ShapeDtypeStruct((s, 1024), F32),
                   jax.ShapeDtypeStruct((8, LANES), F32), jax.ShapeDtypeStruct((1, 1024), F32)],
        compiler_params=_cparams(("arbitrary",)),
    )(merged, w_out, x, target, gpost)


def _seg_sum64(x, ones_bd):
    hi = x.astype(BF16)
    lo = (x - hi.astype(F32)).astype(BF16)
    return _dot(hi, ones_bd) + _dot(lo, ones_bd)


def _merge_bwd(do, w_out, merged, proj, ya, yd, o_mla, o_dil, wp_mla, wp_dil):
    s = do.shape[0]
    tm = min(256, s)
    seg = jnp.arange(512) // 64
    ones_bd = (seg[:, None] == seg[None, :]).astype(BF16)

    def body(do_ref, wo_ref, mg_ref, zm_ref, zd_ref, gm_ref, gd_ref, ya_ref, yd_ref, om_ref, od_ref, wm_ref, wd_ref,
             bd_ref, dzm_out, dzd_out, dgm_out, dgd_out, dom_out, dod_out, domt_out, dd_out, dwo_out, dwm_out,
             dwd_out):
        i = pl.program_id(0)

        @pl.when(i == 0)
        def _():
            dwo_out[...] = jnp.zeros_like(dwo_out)
            dwm_out[...] = jnp.zeros_like(dwm_out)
            dwd_out[...] = jnp.zeros_like(dwd_out)

        dov = do_ref[...]
        dwo_out[...] += _dot_tn(mg_ref[...], dov)
        dm = _dot_nt(dov, wo_ref[...])
        for g_ref, y_ref, z_ref, o_ref, w_ref, dz_out, dg_out, dob_out, dd_o, dw_out in (
                (gm_ref, ya_ref, zm_ref, om_ref, wm_ref, dzm_out, dgm_out, dom_out, None, dwm_out),
                (gd_ref, yd_ref, zd_ref, od_ref, wd_ref, dzd_out, dgd_out, dod_out, dd_out, dwd_out)):
            sg = _sigmoid(g_ref[...].astype(F32))
            dg_out[...] = (dm * y_ref[...].astype(F32) * sg * (1.0 - sg)).astype(BF16)
            dy = (dm * sg).astype(BF16)
            z = z_ref[...].astype(F32)
            sz = _sigmoid(z)
            silu = z * sz
            ob = o_ref[...].astype(F32)
            dw_out[...] += _dot_tn((ob * silu).astype(BF16), dy)
            dp = _dot_nt(dy, w_ref[...])
            dz_out[...] = (dp * ob * (sz * (1.0 + z * (1.0 - sz)))).astype(BF16)
            dob = dp * silu
            dob_out[...] = dob.astype(BF16)
            if dd_o is None:
                domt_out[...] = dob.T.astype(BF16)
            else:
                dd_o[...] = _seg_sum64(dob * ob, bd_ref[...])

    row = lambda i: (i, 0)
    cst = lambda i: (0, 0)
    r512 = _full((tm, 512), row)
    r1024 = _full((tm, 1024), row)
    return pl.pallas_call(
        body, name="merge_bwd", grid=(s // tm,),
        in_specs=[r1024, _full((1024, 1024), cst), r1024,
                  _full((tm, 512), lambda i: (i, Z_MLA // 512)), _full((tm, 512), lambda i: (i, Z_DIL // 512)),
                  _full((tm, 1024), lambda i: (i, G_MLA // 1024)), _full((tm, 1024), lambda i: (i, G_DIL // 1024)),
                  r1024, r1024, r512, r512, _full((512, 1024), cst), _full((512, 1024), cst), _full((512, 512), cst)],
        out_specs=[r512, r512, r1024, r1024, r512, r512, _full((512, tm), lambda i: (0, i)), r512,
                   _full((1024, 1024), cst), _full((512, 1024), cst), _full((512, 1024), cst)],
        out_shape=[jax.ShapeDtypeStruct((s, 512), BF16), jax.ShapeDtypeStruct((s, 512), BF16),
                   jax.ShapeDtypeStruct((s, 1024), BF16), jax.ShapeDtypeStruct((s, 1024), BF16),
                   jax.ShapeDtypeStruct((s, 512), BF16), jax.ShapeDtypeStruct((s, 512), BF16),
                   jax.ShapeDtypeStruct((512, s), BF16), jax.ShapeDtypeStruct((s, 512), F32),
                   jax.ShapeDtypeStruct((1024, 1024), F32), jax.ShapeDtypeStruct((512, 1024), F32),
                   jax.ShapeDtypeStruct((512, 1024), F32)],
        compiler_params=_cparams(("arbitrary",)),
    )(do, w_out, merged, proj, proj, proj, proj, ya, yd, o_mla, o_dil, wp_mla, wp_dil, ones_bd)


def _mla_bwd(q, qt, k, kt, v, do, dot, ot, lse):
    s = q.shape[0]
    tk = min(256, s)
    ratio = MLA_BWD_Q_PER_K if s >= MLA_BWD_Q_PER_K * tk else 1
    tq = ratio * tk
    nq, nk = s // tq, s // tk

    def body(q_ref, qt_ref, k_ref, kt_ref, v_ref, do_ref, dot_ref, ot_ref, lse_ref, dqt_out, dkt_out, dvt_out,
             dqt_acc):
        left = _left_mask()
        krow = lax.broadcasted_iota(jnp.int32, (tk, tq), 0)
        qcol = lax.broadcasted_iota(jnp.int32, (tk, tq), 1)
        dqt_acc[...] = jnp.zeros_like(dqt_acc)

        def kv_step(j, _):
            c0 = pl.multiple_of(j * tk, tk)
            vv = v_ref[pl.ds(c0, tk), :]
            khs = [k_ref[pl.ds(c0, tk), hh * 128:(hh + 1) * 128] for hh in range(2)]
            kths = [kt_ref[hh * 128:(hh + 1) * 128, pl.ds(c0, tk)] for hh in range(2)]
            vms = [jnp.where(left if hh == 0 else jnp.logical_not(left), vv, jnp.zeros_like(vv)) for hh in range(2)]

            def scores(i):
                r0 = pl.multiple_of(jnp.minimum(i, nq - 1) * tq, tq)
                dov = do_ref[pl.ds(r0, tq), :]
                return tuple((_dot_nt(khs[hh], q_ref[pl.ds(r0, tq), hh * 128:(hh + 1) * 128]),
                              _dot_nt(vms[hh], dov)) for hh in range(2))

            def update(i, tiles, acc, masked):
                r0 = pl.multiple_of(i * tq, tq)
                new = []
                for hh in range(2):
                    dkt, dvt = acc[hh]
                    st, dp = tiles[hh]
                    hrows = slice(hh * 128, (hh + 1) * 128)
                    drows = slice(hh * 64, (hh + 1) * 64)
                    doth = dot_ref[drows, pl.ds(r0, tq)]
                    dd = jnp.sum(doth.astype(F32) * ot_ref[drows, pl.ds(r0, tq)].astype(F32), axis=0, keepdims=True)
                    p = jnp.exp2(st - lse_ref[hh:hh + 1, pl.ds(r0, tq)])
                    if masked:
                        p = jnp.where((krow + (c0 - r0)) <= qcol, p, 0.0)
                    ds = (p * (dp - dd)).astype(BF16)
                    dvt = dvt + _dot_nt(doth, p.astype(BF16))
                    dkt = dkt + _dot_nt(qt_ref[hrows, pl.ds(r0, tq)], ds)
                    dqt_acc[hrows, pl.ds(r0, tq)] += _dot(kths[hh], ds)
                    new.append((dkt, dvt))
                return tuple(new)

            init = tuple((jnp.zeros((128, tk), F32), jnp.zeros((64, tk), F32)) for _ in range(2))
            i0 = j // ratio
            acc = update(i0, scores(i0), init, True)
            acc = lax.fori_loop(i0 + 1, nq, lambda i, a: update(i, scores(i), a, False), acc)
            for hh in range(2):
                dkt_out[hh * 128:(hh + 1) * 128, pl.ds(c0, tk)] = (acc[hh][0] * (1.0 / LOG2E)).astype(BF16)
                dvt_out[hh * 64:(hh + 1) * 64, pl.ds(c0, tk)] = acc[hh][1].astype(BF16)
            return 0

        lax.fori_loop(0, nk, kv_step, 0)
        dqt_out[...] = (dqt_acc[...] * MLA_SCALE).astype(BF16)

    b256 = _full((s, 256), lambda p: (0, p))
    b128 = _full((s, 128), lambda p: (0, p))
    t256 = _full((256, s), lambda p: (p, 0))
    t128 = _full((128, s), lambda p: (p, 0))
    return pl.pallas_call(
        body, name="mla_bwd", grid=(4,),
        in_specs=[b256, t256, b256, t256, b128, b128, t128, t128, _full((8, s), lambda p: (p, 0))],
        out_specs=[t256, t256, t128],
        out_shape=[jax.ShapeDtypeStruct((1024, s), BF16), jax.ShapeDtypeStruct((1024, s), BF16),
                   jax.ShapeDtypeStruct((512, s), BF16)],
        scratch_shapes=[pltpu.VMEM((256, s), F32)],
        compiler_params=_cparams(("parallel",)),
    )(q, qt, k, kt, v, do, dot, ot, lse)


def _mla_prep_bwd(dq, dk, dv, proj, gq, gkv, w_uq, w_uk, w_uv, q_tab, k_tab):
    s = proj.shape[0]
    tm = min(256, s)

    def body(dqt_ref, dkt_ref, dvt_ref, cq_ref, ckv_ref, gq_ref, gkv_ref, wq_ref, wk_ref, wv_ref,
             qc, qsp, qsm, kc, ksp, ksm,
             dcq_out, dkr_out, dckv_out, dwq_out, dwk_out, dwv_out, dgq_out, dgkv_out):
        i = pl.program_id(0)

        @pl.when(i == 0)
        def _():
            for r in (dwq_out, dwk_out, dwv_out, dgq_out, dgkv_out):
                r[...] = jnp.zeros_like(r)

        dqu = _unrope(dqt_ref[...].astype(F32).T, _tile_lanes(qc[...], 1024), _tile_lanes(qsp[...], 1024),
                      _tile_lanes(qsm[...], 1024), 16).astype(BF16)
        gq = gq_ref[...]
        cqn, xh, r = _rms(cq_ref[...].astype(F32), gq)
        dwq_out[...] += _dot_tn(cqn.astype(BF16), dqu)
        dcqn = _dot_nt(dqu, wq_ref[...])
        dgq_out[...] += jnp.sum(dcqn * xh, axis=0, keepdims=True)
        dcq_out[...] = _rms_bwd(dcqn, gq, xh, r).astype(BF16)

        dkf = dkt_ref[...].astype(F32).T
        dkb = dkf.astype(BF16)
        dsum = dkf[:, 0:128]
        for h in range(1, MLA_HEADS):
            dsum = dsum + dkf[:, h * 128:(h + 1) * 128]
        dkr_out[...] = _unrope(pltpu.roll(dsum, 64, 1), kc[...], ksp[...], ksm[...], 16).astype(BF16)

        dvb = dvt_ref[...].astype(F32).T.astype(BF16)
        gkv = gkv_ref[...]
        ckvn, xh2, r2 = _rms(ckv_ref[...].astype(F32), gkv)
        ckvn = ckvn.astype(BF16)
        dwk_out[...] += _dot_tn(ckvn, dkb)
        dwv_out[...] += _dot_tn(ckvn, dvb)
        dckvn = _dot_nt(dkb, wk_ref[...]) + _dot_nt(dvb, wv_ref[...])
        dgkv_out[...] += jnp.sum(dckvn * xh2, axis=0, keepdims=True)
        dckv_out[...] = _rms_bwd(dckvn, gkv, xh2, r2).astype(BF16)

    row = lambda i: (i, 0)
    cst = lambda i: (0, 0)
    tabs = [_full((tm, LANES), row)] * 6
    return pl.pallas_call(
        body, name="mla_prep_bwd", grid=(s // tm,),
        in_specs=[_full((1024, tm), lambda i: (0, i)), _full((1024, tm), lambda i: (0, i)),
                  _full((512, tm), lambda i: (0, i)),
                  _full((tm, 384), lambda i: (i, CQ_OFF // 384)), _full((tm, 256), lambda i: (i, CKV_OFF // 256)),
                  _full((1, 384), cst), _full((1, 256), cst),
                  _full((384, 1024), cst), _full((256, 1024), cst), _full((256, 512), cst)] + tabs,
        out_specs=[_full((tm, 384), row), _full((tm, 128), row), _full((tm, 256), row),
                   _full((384, 1024), cst), _full((256, 1024), cst), _full((256, 512), cst),
                   _full((1, 384), cst), _full((1, 256), cst)],
        out_shape=[jax.ShapeDtypeStruct((s, 384), BF16), jax.ShapeDtypeStruct((s, 128), BF16),
                   jax.ShapeDtypeStruct((s, 256), BF16),
                   jax.ShapeDtypeStruct((384, 1024), F32), jax.ShapeDtypeStruct((256, 1024), F32),
                   jax.ShapeDtypeStruct((256, 512), F32),
                   jax.ShapeDtypeStruct((1, 384), F32), jax.ShapeDtypeStruct((1, 256), F32)],
        compiler_params=_cparams(("arbitrary",)),
    )(dq, dk, dv, proj, proj, gq, gkv, w_uq, w_uk, w_uv, *q_tab, *k_tab)


def _dil_bwd(proj, g, do, lse, dd, tabs, d, name):
    s = proj.shape[0]
    nblk = s // Q_BLOCK
    per_seq = nblk // d

    def body(q_ref, k_ref, v_ref, do_ref, lse_ref, dd_ref, c_ref, sp_ref, sm_ref, dq_out, dk_out, dv_out,
             qf, kf, vf, dof, dq_acc, dk_acc, dv_acc):
        left = _left_mask()
        hms = (left, jnp.logical_not(left))
        qf[...] = q_ref[...].astype(F32)
        kf[...] = k_ref[...].astype(F32)
        vf[...] = v_ref[...].astype(F32)
        dof[...] = do_ref[...].astype(F32)
        dk_acc[...] = jnp.zeros_like(dk_acc)
        dv_acc[...] = jnp.zeros_like(dv_acc)

        def scores(b):
            rows, prow = _dil_rows(b, d, per_seq)
            qb, dob = qf[rows, :].astype(BF16), dof[rows, :].astype(BF16)
            kp, kc = kf[prow, :].astype(BF16), kf[rows, :].astype(BF16)
            vp, vc = vf[prow, :].astype(BF16), vf[rows, :].astype(BF16)
            zero = jnp.zeros_like(qb)
            out = []
            for hh in range(2):
                qm, dom = jnp.where(hms[hh], qb, zero), jnp.where(hms[hh], dob, zero)
                out.append((_dot_nt(qm, kp) * DIL_SCALE, _dot_nt(qm, kc) * DIL_SCALE,
                            _dot_nt(dom, vp), _dot_nt(dom, vc)))
            return out, (qb, dob, kp, kc)

        def finish(b, loaded):
            tiles, (qb, dob, kp, kc) = loaded
            rows, prow = _dil_rows(b, d, per_seq)
            has_prev = (b % per_seq) > 0
            pmask, cur_ok = _band_masks(has_prev)
            lse_b, dd_b = lse_ref[rows, :], dd_ref[rows, :]
            zero = jnp.zeros_like(qb)
            dq = jnp.zeros((Q_BLOCK, LANES), F32)
            dkp = jnp.zeros((Q_BLOCK, LANES), F32)
            dkc = jnp.zeros((Q_BLOCK, LANES), F32)
            dvp = jnp.zeros((Q_BLOCK, LANES), F32)
            dvc = jnp.zeros((Q_BLOCK, LANES), F32)
            for hh in range(2):
                hm = hms[hh]
                qm, dom = jnp.where(hm, qb, zero), jnp.where(hm, dob, zero)
                lse_h, dd_h = _expand_half(lse_b, hh, left), _expand_half(dd_b, hh, left)
                sp, sc, dpp, dpc = tiles[hh]
                pp = jnp.where(pmask, jnp.exp(sp - lse_h), 0.0)
                pc = jnp.where(cur_ok, jnp.exp(sc - lse_h), 0.0)
                dsp = (pp * (dpp - dd_h) * DIL_SCALE).astype(BF16)
                dsc = (pc * (dpc - dd_h) * DIL_SCALE).astype(BF16)
                dq = dq + _dot(dsp, jnp.where(hm, kp, zero)) + _dot(dsc, jnp.where(hm, kc, zero))
                dkp, dkc = dkp + _dot_tn(dsp, qm), dkc + _dot_tn(dsc, qm)
                dvp, dvc = dvp + _dot_tn(pp.astype(BF16), dom), dvc + _dot_tn(pc.astype(BF16), dom)
            dq_acc[rows, :] = dq
            dk_acc[rows, :] += dkc
            dv_acc[rows, :] += dvc

            @pl.when(has_prev)
            def _():
                dk_acc[prow, :] += dkp
                dv_acc[prow, :] += dvp

        def step(t, _):
            tiles = [scores(DIL_BLOCKS_PER_STEP * t + u) for u in range(DIL_BLOCKS_PER_STEP)]
            for u in range(DIL_BLOCKS_PER_STEP):
                finish(DIL_BLOCKS_PER_STEP * t + u, tiles[u])
            return 0

        lax.fori_loop(0, nblk // DIL_BLOCKS_PER_STEP, step, 0)
        dq_out[...] = _unrope(dq_acc[...], c_ref[...], sp_ref[...], sm_ref[...], 8).astype(BF16)
        dk_out[...] = _unrope(dk_acc[...], c_ref[...], sp_ref[...], sm_ref[...], 8).astype(BF16)
        dv_out[...] = dv_acc[...].astype(BF16)

    blk = lambda off: _full((s, 128), lambda p, off=off: (0, off + p))
    tab = _full((s, 128), lambda p: (0, 0))
    qo, ko, vo = ((off + 512 * g) // 128 for off in (Q_OFF, K_OFF, V_OFF))
    return pl.pallas_call(
        body, name=name, grid=(4,),
        in_specs=[blk(qo), blk(ko), blk(vo), blk(0), blk(0), blk(0), tab, tab, tab],
        out_specs=[blk(0), blk(0), blk(0)],
        out_shape=[jax.ShapeDtypeStruct((s, 512), BF16)] * 3,
        scratch_shapes=[pltpu.VMEM((s, 128), F32)] * 7,
        compiler_params=_cparams(("parallel",)),
    )(proj, proj, proj, do, lse, dd, *tabs)


def _chip_copies(ins, outs, sems, outgoing):
    x, y, c, chips = _place()
    myq = 2 * x + y
    n = len(ins)

    def chunk(a, j):
        q = 2 * chips[j][0] + chips[j][1]
        return _remote(ins[a].at[q], outs[a].at[myq if outgoing else q], sems, 2 * a, j, (*chips[j], c))

    if outgoing is None:
        return [pltpu.make_async_copy(ins[a].at[myq], outs[a].at[myq], sems.at[2 * a, 3]) for a in range(n)]
    return [chunk(a, j) for j in range(3) for a in range(n)]


def _dh_bwd(dproj, w_pad, x, gpre, dy, pairs):
    s = x.shape[0]
    tm, tk = min(512, s), 1408
    ni, nk = s // tm, IN_PAD // tk
    n = len(pairs)

    def body(*refs):
        dp_ref, w_ref, x_ref, g_ref, dy_ref = refs[:5]
        ins, gx_out, dg_out, outs = refs[5:5 + n], refs[5 + n], refs[6 + n], refs[7 + n:7 + 2 * n]
        acc, sems = refs[7 + 2 * n], refs[8 + 2 * n]
        i, kk = pl.program_id(0), pl.program_id(1)

        @pl.when(jnp.logical_and(i == 0, kk == 0))
        def _():
            dg_out[...] = jnp.zeros_like(dg_out)
            for cp in _chip_copies(ins, outs, sems, None) + _chip_copies(ins, outs, sems, True):
                cp.start()

        @pl.when(kk == 0)
        def _():
            acc[...] = jnp.zeros_like(acc)

        acc[...] += _dot_nt(dp_ref[...], w_ref[...])

        @pl.when(kk == nk - 1)
        def _():
            g = g_ref[...]
            _, xh, r = _rms(x_ref[...], g)
            dh = acc[...]
            dg_out[...] += jnp.sum(dh * xh, axis=0, keepdims=True)
            gx_out[...] = dy_ref[...] + _rms_bwd(dh, g, xh, r)

        @pl.when(jnp.logical_and(i == ni - 1, kk == nk - 1))
        def _():
            for cp in _chip_copies(ins, outs, sems, False):
                cp.wait_recv()
            for cp in _chip_copies(ins, outs, sems, True):
                cp.wait_send()
            for cp in _chip_copies(ins, outs, sems, None):
                cp.wait()

    row = lambda i, k: (i, 0)
    hbm = pl.BlockSpec(memory_space=pl.ANY)
    res = pl.pallas_call(
        body, name="dh_bwd", grid=(ni, nk),
        in_specs=[_full((tm, tk), lambda i, k: (i, k)), _full((1024, tk), lambda i, k: (0, k)),
                  _full((tm, 1024), row), _full((1, 1024), lambda i, k: (0, 0)), _full((tm, 1024), row)] + [hbm] * n,
        out_specs=[_full((tm, 1024), row), _full((1, 1024), lambda i, k: (0, 0))] + [hbm] * n,
        out_shape=[jax.ShapeDtypeStruct((s, 1024), F32), jax.ShapeDtypeStruct((1, 1024), F32)]
        + [jax.ShapeDtypeStruct(a.shape, a.dtype) for a in pairs],
        scratch_shapes=[pltpu.VMEM((tm, 1024), F32), pltpu.SemaphoreType.DMA((2 * n, 4))],
        compiler_params=_cparams(("arbitrary", "arbitrary")),
    )(dproj, w_pad, x, gpre, dy, *pairs)
    return res[0], res[1], res[2:]


def _dw_in(h, dproj):
    s = h.shape[0]
    ts, tn = min(1024, s), 768
    ns = s // ts

    def body(h_ref, dp_ref, o_ref, acc):
        k = pl.program_id(1)

        @pl.when(k == 0)
        def _():
            acc[...] = jnp.zeros_like(acc)

        acc[...] += _dot_tn(h_ref[...], dp_ref[...])

        @pl.when(k == ns - 1)
        def _():
            o_ref[...] = acc[...].astype(BF16)

    return pl.pallas_call(
        body, name="dw_in", grid=(IN_PAD // tn, ns),
        in_specs=[_full((ts, 1024), lambda j, k: (k, 0)), _full((ts, tn), lambda j, k: (k, j))],
        out_specs=_full((1024, tn), lambda j, k: (0, j)),
        out_shape=jax.ShapeDtypeStruct((1024, IN_PAD), BF16),
        scratch_shapes=[pltpu.VMEM((1024, tn), F32)],
        compiler_params=_cparams(("parallel", "arbitrary")),
    )(h, dproj)


def _remote(src, dst, sems, row, k, to):
    return pltpu.make_async_remote_copy(src_ref=src, dst_ref=dst, send_sem=sems.at[row, k], recv_sem=sems.at[row + 1, k],
                                        device_id=to, device_id_type=pl.DeviceIdType.MESH)


def _place():
    x, y, c = lax.axis_index("x"), lax.axis_index("y"), lax.axis_index("c")
    return x, y, c, [(1 - x, y), (x, 1 - y), (1 - x, 1 - y)]


def _gather_weights(arrays):
    n = len(arrays)

    def body(*refs):
        ins, outs, sems = refs[:n], refs[n:2 * n], refs[2 * n]
        x, y, c, chips = _place()
        me, sib = (x, y, c), (x, y, 1 - c)
        idx = lambda p: 4 * p[0] + 2 * p[1] + p[2]

        def copy(a, k, block, to, from_input=False):
            src = ins[a] if from_input else outs[a].at[idx(block)]
            return _remote(src, outs[a].at[idx(block)], sems, 2 * a, k, to)

        own = [pltpu.make_async_copy(ins[a], outs[a].at[idx(me)], sems.at[2 * a, 7]) for a in range(n)]
        first = [copy(a, 0, me, sib, True) for a in range(n)]
        first += [copy(a, 1 + j, me, (*chip, c), True) for j, chip in enumerate(chips) for a in range(n)]
        for cp in own + first:
            cp.start()
        passed = []
        for j, chip in enumerate(chips):
            for a in range(n):
                copy(a, 1 + j, (*chip, c), me).wait_recv()
                passed.append(copy(a, 4 + j, (*chip, c), sib))
                passed[-1].start()
        for a in range(n):
            copy(a, 0, sib, me).wait_recv()
        for j, chip in enumerate(chips):
            for a in range(n):
                copy(a, 4 + j, (*chip, 1 - c), me).wait_recv()
        for cp in first + passed:
            cp.wait_send()
        for cp in own:
            cp.wait()

    hbm = pl.BlockSpec(memory_space=pl.ANY)
    return pl.pallas_call(
        body, name="gather_weights", in_specs=[hbm] * n, out_specs=[hbm] * n,
        out_shape=[jax.ShapeDtypeStruct((N_DEV,) + a.shape, a.dtype) for a in arrays],
        scratch_shapes=[pltpu.SemaphoreType.DMA((2 * n, N_DEV))],
    )(*arrays)


def _pair_exchange(chunks):
    n = len(chunks)

    def body(*refs):
        ins, outs, sems = refs[:n], refs[n:2 * n], refs[2 * n]
        x, y, c, _ = _place()
        sent = [_remote(ins[a].at[2 * q + (1 - c)], outs[a].at[q], sems, 2 * a, q, (x, y, 1 - c))
                for a in range(n) for q in range(4)]
        for cp in sent:
            cp.start()
        for cp in sent:
            cp.wait_recv()
        for cp in sent:
            cp.wait_send()

    hbm = pl.BlockSpec(memory_space=pl.ANY)
    return pl.pallas_call(
        body, name="pair_exchange", in_specs=[hbm] * n, out_specs=[hbm] * n,
        out_shape=[jax.ShapeDtypeStruct((4,) + a.shape[1:], a.dtype) for a in chunks],
        scratch_shapes=[pltpu.SemaphoreType.DMA((2 * n, 4))],
    )(*chunks)


def _pair_sum(core, chunks, recv, name, tr):
    _, rows, cols = chunks.shape

    def body(c_ref, a_ref, b_ref, o_ref):
        o_ref[...] = (a_ref[...].astype(F32) + b_ref[...].astype(F32)).astype(BF16)

    blk = lambda f: _full((1, tr, cols), f)
    return pl.pallas_call(
        body, name=name, out_shape=jax.ShapeDtypeStruct((4, rows, cols), BF16),
        grid_spec=pltpu.PrefetchScalarGridSpec(
            num_scalar_prefetch=1, grid=(4, rows // tr),
            in_specs=[blk(lambda q, i, c: (2 * q + c[0], i, 0)), blk(lambda q, i, c: (q, i, 0))],
            out_specs=blk(lambda q, i, c: (q, i, 0))),
        compiler_params=_cparams(("parallel", "parallel")),
    )(core, chunks, recv)


def _packet_exchange(packet):
    def body(pk, pk_out, sems):
        x, y, c, _ = _place()
        me = 4 * x + 2 * y + c
        flip = lambda v, b: (1 - v) if b else v

        def small(j, outgoing):
            peer = (flip(x, (j >> 2) & 1), flip(y, (j >> 1) & 1), flip(c, j & 1))
            slot = me if outgoing else 4 * peer[0] + 2 * peer[1] + peer[2]
            return _remote(pk, pk_out.at[slot], sems, 0, j, peer)

        own = pltpu.make_async_copy(pk, pk_out.at[me], sems.at[0, 0])
        sent = [small(j, True) for j in range(1, N_DEV)]
        for cp in [own] + sent:
            cp.start()
        for j in range(1, N_DEV):
            small(j, False).wait_recv()
        for cp in sent:
            cp.wait_send()
        own.wait()

    hbm = pl.BlockSpec(memory_space=pl.ANY)
    return pl.pallas_call(
        body, name="packet_exchange", in_specs=[hbm], out_specs=hbm,
        out_shape=jax.ShapeDtypeStruct((N_DEV,) + packet.shape, packet.dtype),
        scratch_shapes=[pltpu.SemaphoreType.DMA((2, N_DEV))],
    )(packet)


def _adam_math(w, g, m, v):
    m = ADAM_B1 * m + (1.0 - ADAM_B1) * g
    v = ADAM_B2 * v + (1.0 - ADAM_B2) * (g * g)
    m_hat = m / (1.0 - ADAM_B1 ** ADAM_STEP)
    v_hat = v / (1.0 - ADAM_B2 ** ADAM_STEP)
    delta = -ADAM_LR * (m_hat / (jnp.sqrt(v_hat) + ADAM_EPS) + ADAM_WD * w)
    return delta, m, v


def _adam(recv, w, m, v, name, tr):
    _, rows, cols = w.shape

    def body(r_ref, w_ref, m_ref, v_ref, g_out, d_out, m_out, v_out):
        g = r_ref[0].astype(F32)
        for k in range(1, 4):
            g = g + r_ref[k].astype(F32)
        g_out[0] = g
        d_out[0], m_out[0], v_out[0] = _adam_math(w_ref[0], g, m_ref[0], v_ref[0])

    blk = _full((1, tr, cols), lambda i: (0, i, 0))
    return pl.pallas_call(
        body, name=name, grid=(rows // tr,),
        in_specs=[_full((4, tr, cols), lambda i: (0, i, 0)), blk, blk, blk],
        out_specs=[blk] * 4,
        out_shape=[jax.ShapeDtypeStruct(w.shape, F32)] * 4,
        compiler_params=_cparams(("parallel",)),
    )(recv, w, m, v)


def _adam_cols(recv, w, m, v, name, tc):
    rows, cols = w.shape

    def body(r_ref, w_ref, m_ref, v_ref, g_out, d_out, m_out, v_out):
        g = r_ref[0].astype(F32)
        for k in range(1, 4):
            g = g + r_ref[k].astype(F32)
        g_out[...] = g
        d_out[...], m_out[...], v_out[...] = _adam_math(w_ref[...], g, m_ref[...], v_ref[...])

    blk = _full((rows, tc), lambda i: (0, i))
    return pl.pallas_call(
        body, name=name, grid=(cols // tc,),
        in_specs=[_full((4, rows, tc), lambda i: (0, 0, i)), blk, blk, blk],
        out_specs=[blk] * 4,
        out_shape=[jax.ShapeDtypeStruct(w.shape, F32)] * 4,
        compiler_params=_cparams(("parallel",)),
    )(recv, w, m, v)


def _adam_gains(recv, gains, gains_m, gains_v):
    def body(*refs):
        r_ref, w, m, v = refs[0], refs[1:5], refs[5:9], refs[9:13]
        g_out, d_out, m_out, v_out, loss_out = refs[13:17], refs[17:21], refs[21:25], refs[25:29], refs[29]
        tot = r_ref[0:1, :]
        for k in range(1, N_DEV):
            tot = tot + r_ref[k:k + 1, :]
        for t in range(4):
            g = tot[:, GAIN_OFFS[t]:GAIN_OFFS[t] + GAIN_WIDTHS[t]]
            g_out[t][...] = g
            d_out[t][...], m_out[t][...], v_out[t][...] = _adam_math(w[t][...], g, m[t][...], v[t][...])
        loss_out[...] = tot[:, LOSS_OFF:LOSS_OFF + LANES]

    shapes = [jax.ShapeDtypeStruct((1, n), F32) for n in GAIN_WIDTHS]
    return pl.pallas_call(
        body, name="adam_gains", out_shape=shapes * 4 + [jax.ShapeDtypeStruct((1, LANES), F32)],
    )(recv, *gains, *gains_m, *gains_v)


def _local_step(x, positions, gains, weights, target):
    gpre, gq, gkv, gpost = gains
    w_pad, w_uq, w_uk, w_uv, wp_mla, wp_dil, w_out = weights
    q_tab, k_tab, d_tab = _rope_tables(positions)

    proj, h = _inproj(x, gpre, w_pad, d_tab)
    q, k, v, qt, kt, vt = _mla_prep(proj, gq, gkv, w_uq, w_uk, w_uv, q_tab, k_tab)
    o_mla, ot_mla, lse_mla = _mla_fwd(q, k, vt)

    od, lsed = [], []
    for g, d in enumerate(DIL_DILATIONS):
        o_g, lse_g = _dil_fwd(proj, g, d, "dil_fwd_%d" % g)
        od.append(o_g)
        lsed.append(lse_g)

    merged, ya, yd, o_dil, lse_dil = _merge_fwd(proj, o_mla, od, lsed, wp_mla, wp_dil)
    do, dy, loss, dgpost = _out_loss(merged, w_out, x, target, gpost)

    (dz_mla, dz_dil, dg_mla, dg_dil, do_mla, do_dil, dot_mla, dd_dil, dw_out, dwp_mla, dwp_dil) = _merge_bwd(
        do, w_out, merged, proj, ya, yd, o_mla, o_dil, wp_mla, wp_dil)

    dq, dk, dv = _mla_bwd(q, qt, k, kt, v, do_mla, dot_mla, ot_mla, lse_mla)
    dcq, dkr, dckv, dw_uq, dw_uk, dw_uv, dgq, dgkv = _mla_prep_bwd(dq, dk, dv, proj, gq, gkv, w_uq, w_uk, w_uv,
                                                                   q_tab, k_tab)

    dqs, dks, dvs = [], [], []
    for g, d in enumerate(DIL_DILATIONS):
        dq_g, dk_g, dv_g = _dil_bwd(proj, g, do_dil, lse_dil, dd_dil, d_tab, d, "dil_bwd_%d" % g)
        dqs.append(dq_g)
        dks.append(dk_g)
        dvs.append(dv_g)

    dproj = jnp.concatenate([dz_mla, dz_dil, dg_mla, dg_dil] + dqs + dks + dvs + [dcq, dkr, dckv], axis=1)
    dw_in = _dw_in(h, dproj)
    return loss, (dproj, dy), (dgq, dgkv, dgpost), (dw_in, dw_uq, dw_uk, dw_uv, dwp_mla, dwp_dil, dw_out)


ADAM_ROWS = (256, 384, 256, 512, 512, 128)
PAIR_ROWS = (512, 384, 256, 512, 512, 128)


def kernel(x, positions, pre_norm_g, w_in, q_norm_g, w_uq, kv_norm_g, w_ukv, w_proj_mla, w_proj_dil, w_out, post_norm_g, loss_target, m_pre_norm_g, m_w_in, m_q_norm_g, m_w_uq, m_kv_norm_g, m_w_ukv, m_w_proj_mla, m_w_proj_dil, m_w_out, m_post_norm_g, v_pre_norm_g, v_w_in, v_q_norm_g, v_w_uq, v_kv_norm_g, v_w_ukv, v_w_proj_mla, v_w_proj_dil, v_w_out, v_post_norm_g):
    big_w = (w_in, w_uq, w_ukv, w_proj_mla, w_proj_dil, w_out)
    big_m = (m_w_in, m_w_uq, m_w_ukv, m_w_proj_mla, m_w_proj_dil, m_w_out)
    big_v = (v_w_in, v_w_uq, v_w_ukv, v_w_proj_mla, v_w_proj_dil, v_w_out)
    gains = (pre_norm_g, q_norm_g, kv_norm_g, post_norm_g)
    gains_m = (m_pre_norm_g, m_q_norm_g, m_kv_norm_g, m_post_norm_g)
    gains_v = (v_pre_norm_g, v_q_norm_g, v_kv_norm_g, v_post_norm_g)

    gathered = _gather_weights([w[0].astype(BF16) for w in big_w])
    weights = _assemble_weights(*gathered)

    loss, (dproj, dy), (dgq, dgkv, dgpost), dweights = _local_step(x[0], positions[0], gains, weights,
                                                                   loss_target[0])

    chunks = _grad_chunks(*dweights)
    from_sibling = _pair_exchange(chunks)
    core = lax.axis_index("c").astype(jnp.int32).reshape(1)
    pairs = [_pair_sum(core, chunks[t], from_sibling[t], "pair_sum_%d" % t, PAIR_ROWS[t]) for t in range(6)]
    grad_x, dgpre, received = _dh_bwd(dproj, weights[0], x[0], pre_norm_g, dy, pairs)
    packet = _packet_exchange(jnp.concatenate([dgpre, dgq, dgkv, dgpost, loss[0:1]], axis=1))

    w_in_t = _adam_cols(received[0].transpose(0, 2, 1), w_in[0].T, m_w_in[0].T, v_w_in[0].T, "adam_0", 256)
    big = [[o.T[None] for o in w_in_t]]
    big += [_adam(received[t], big_w[t], big_m[t], big_v[t], "adam_%d" % t, ADAM_ROWS[t]) for t in range(1, 6)]
    small = _adam_gains(packet.reshape(N_DEV, PACKET), gains, gains_m, gains_v)

    def interleave(kind):
        s_pre, s_q, s_kv, s_post = small[4 * kind:4 * kind + 4]
        b_in, b_uq, b_ukv, b_pm, b_pd, b_out = (big[t][kind] for t in range(6))
        return [s_pre, b_in, s_q, b_uq, s_kv, b_ukv, b_pm, b_pd, b_out, s_post]

    return (small[16][0, 0], grad_x[None], *interleave(0), *interleave(1), *interleave(2), *interleave(3))
```

```python
import numpy as np
import jax
import jax.numpy as jnp
from jax import lax
from jax.experimental import pallas as pl
from jax.experimental.pallas import tpu as pltpu

F32 = jnp.float32
BF16 = jnp.bfloat16

D_MODEL = 1024
NORM_EPS = 1e-6
ROPE_THETA = 500000.0
N_DEV = 8
LANES = 128
NEG = -1e30

MLA_HEADS = 8
MLA_Q_RANK = 384
MLA_KV_RANK = 256
MLA_SCALE = 96.0 ** -0.5
LOG2E = 1.4426950408889634
MLA_QSCALE = MLA_SCALE * LOG2E
MLA_FWD_TK, MLA_FWD_Q_PER_K = 256, 2
MLA_BWD_Q_PER_K = 2
DIL_DILATIONS = (1, 4, 16)
DIL_SCALE = 0.125
Q_BLOCK = 128
DIL_BLOCKS_PER_STEP = 4

Z_MLA, Z_DIL, G_MLA, G_DIL = 0, 512, 1024, 2048
Q_OFF, K_OFF, V_OFF = 3072, 4608, 6144
CQ_OFF, KR_OFF, CKV_OFF, IN_PAD = 7680, 8064, 8192, 8448
IN_WIDTH = 8352
SHARD_W = IN_WIDTH // 8
IN_SEGS = ((0, 384, CQ_OFF), (384, 256, CKV_OFF), (640, 32, KR_OFF), (672, 1536, Q_OFF), (2208, 1536, K_OFF),
           (3744, 1536, V_OFF), (5280, 512, Z_MLA), (5792, 512, Z_DIL), (6304, 1024, G_MLA), (7328, 1024, G_DIL))

GAIN_OFFS = (0, 1024, 1408, 1664)
GAIN_WIDTHS = (1024, 384, 256, 1024)
LOSS_OFF, PACKET = 2688, 2816

ADAM_LR, ADAM_B1, ADAM_B2, ADAM_EPS, ADAM_WD, ADAM_STEP = 0.001, 0.9, 0.999, 1e-08, 0.01, 10

VMEM_LIMIT_MB = 56


def _cparams(sem=None, vmem_mb=VMEM_LIMIT_MB):
    return pltpu.CompilerParams(dimension_semantics=sem, vmem_limit_bytes=vmem_mb * 1024 * 1024)


def _dot(a, b):
    return jnp.dot(a, b, preferred_element_type=F32)


def _dot_nt(a, b):
    return lax.dot_general(a, b, (((1,), (1,)), ((), ())), preferred_element_type=F32)


def _dot_tn(a, b):
    return lax.dot_general(a, b, (((0,), (0,)), ((), ())), preferred_element_type=F32)


def _tile_lanes(t, width):
    return t if width == t.shape[1] else jnp.tile(t, (1, width // t.shape[1]))


def _rope(x, c, sp, sm, a):
    n = x.shape[1]
    return x * c + pltpu.roll(x, a, 1) * sp + pltpu.roll(x, n - a, 1) * sm


def _unrope(dy, c, sp, sm, a):
    n = dy.shape[1]
    return dy * c + pltpu.roll(dy * sp, n - a, 1) + pltpu.roll(dy * sm, a, 1)


def _sigmoid(z):
    return 1.0 / (1.0 + jnp.exp(-z))


def _left_mask():
    return lax.broadcasted_iota(jnp.int32, (1, LANES), 1) < 64


def _expand_half(x, hh, left):
    r = pltpu.roll(x, 64, 1)
    return jnp.where(left, x, r) if hh == 0 else jnp.where(left, r, x)


def _rms(xv, g):
    r = lax.rsqrt(jnp.mean(xv * xv, axis=-1, keepdims=True) + NORM_EPS)
    xh = xv * r
    return xh * g, xh, r


def _rms_bwd(dout, g, xh, r):
    dxh = dout * g
    return r * (dxh - xh * jnp.mean(dxh * xh, axis=-1, keepdims=True))


def _full(shape, index_map):
    return pl.BlockSpec(shape, index_map)


def _w_in_pieces():
    out = []
    for s, n, off in sorted(IN_SEGS, key=lambda t: t[2]):
        c = s
        while c < s + n:
            k = c // SHARD_W
            e = min(s + n, (k + 1) * SHARD_W)
            out.append((k, c - k * SHARD_W, e - c, off + (c - s)))
            c = e
    return out


def _assemble_w_in(g):
    parts, cur = [], 0
    for k, a, w, off in _w_in_pieces():
        if off > cur:
            parts.append(jnp.zeros((D_MODEL, off - cur), g.dtype))
        parts.append(g[k, :, a:a + w])
        cur = off + w
    if cur < IN_PAD:
        parts.append(jnp.zeros((D_MODEL, IN_PAD - cur), g.dtype))
    return jnp.concatenate(parts, axis=1)


def _dw_in_chunks(dw):
    chunks = []
    for dev in range(N_DEV):
        mine = sorted((p for p in _w_in_pieces() if p[0] == dev), key=lambda p: p[1])
        chunks.append(jnp.concatenate([dw[:, off:off + w] for k, a, w, off in mine], axis=1))
    return jnp.stack(chunks)


def _assemble_weights(g_in, g_uq, g_ukv, g_pm, g_pd, g_out):
    w_uq_pad = jnp.pad(g_uq.transpose(1, 0, 2), ((0, 0), (0, 0), (0, 32))).reshape(384, 1024)
    ukv = g_ukv.transpose(1, 0, 2)
    w_uk_pad = jnp.pad(ukv[:, :, :64], ((0, 0), (0, 0), (0, 64))).reshape(256, 1024)
    w_uv = ukv[:, :, 64:].reshape(256, 512)
    wp_mla = g_pm.transpose(1, 0, 2).reshape(512, 1024)
    wp_dil = g_pd.transpose(1, 0, 2).reshape(512, 1024)
    return _assemble_w_in(g_in), w_uq_pad, w_uk_pad, w_uv, wp_mla, wp_dil, g_out.reshape(1024, 1024)


def _grad_chunks(dw_in_pad, dw_uq_pad, dw_uk_pad, dw_uv, dwp_mla, dwp_dil, dw_out):
    a = _dw_in_chunks(dw_in_pad)
    b = dw_uq_pad.reshape(384, 8, 128)[:, :, :96].transpose(1, 0, 2)
    c = jnp.concatenate([dw_uk_pad.reshape(256, 8, 128)[:, :, :64], dw_uv.reshape(256, 8, 64)], axis=2)
    c = c.transpose(1, 0, 2)
    d = dwp_mla.reshape(512, N_DEV, 128).transpose(1, 0, 2)
    e = dwp_dil.reshape(512, N_DEV, 128).transpose(1, 0, 2)
    f = dw_out.reshape(N_DEV, 128, 1024)
    return [t.astype(BF16) for t in (a, b, c, d, e, f)]


def _lane_consts(freqs, half, first, period):
    rel = (np.arange(LANES) % period) - first
    rot = (rel >= 0) & (rel < 2 * half)
    freq = np.where(rot, freqs[np.clip(rel, 0, 2 * half - 1) % half], 0.0).astype(np.float32)
    x1 = (rot & (rel < half)).astype(np.float32)
    x2 = (rot & (rel >= half)).astype(np.float32)
    return freq[None, :], x1[None, :], x2[None, :]


def _rope_tables(pos):
    p = pos.astype(F32)[:, None]
    inv_m = np.float32(ROPE_THETA) ** (-(np.arange(0, 32, 2, dtype=np.float32) / np.float32(32)))
    inv_d = np.float32(ROPE_THETA) ** (-(np.arange(0, 16, 2, dtype=np.float32) / np.float32(16)))
    lane = np.arange(LANES)
    tabs = []
    for freqs, half, first, period, keep in ((inv_m, 16, 64, 128, lane < 96), (inv_m, 16, 0, 128, lane < 32),
                                              (inv_d, 8, 0, 64, lane >= 0)):
        freq, x1, x2 = _lane_consts(freqs, half, first, period)
        ang = p * freq
        sin = jnp.sin(ang)
        tabs.append((jnp.cos(ang) * keep.astype(np.float32)[None, :], sin * x2, sin * (-x1)))
    return tuple(tabs)


def _inproj(x, gpre, w_pad, d_tab):
    s = x.shape[0]
    tm, tn = min(1024, s), 768
    rope_lo, rope_hi = Q_OFF // tn, V_OFF // tn

    def body(x_ref, g_ref, w_ref, c_ref, sp_ref, sm_ref, o_ref, h_ref):
        j = pl.program_id(1)

        @pl.when(j == 0)
        def _():
            hv, _, _ = _rms(x_ref[...], g_ref[...])
            h_ref[...] = hv.astype(BF16)

        acc = _dot(h_ref[...], w_ref[...])
        is_rope = jnp.logical_and(j >= rope_lo, j < rope_hi)

        @pl.when(is_rope)
        def _():
            o_ref[...] = _rope(acc, _tile_lanes(c_ref[...], tn), _tile_lanes(sp_ref[...], tn),
                               _tile_lanes(sm_ref[...], tn), 8).astype(BF16)

        @pl.when(jnp.logical_not(is_rope))
        def _():
            o_ref[...] = acc.astype(BF16)

    row = lambda i, j: (i, 0)
    return pl.pallas_call(
        body, name="inproj", grid=(s // tm, IN_PAD // tn),
        in_specs=[_full((tm, D_MODEL), row), _full((1, D_MODEL), lambda i, j: (0, 0)),
                  _full((D_MODEL, tn), lambda i, j: (0, j)),
                  _full((tm, LANES), row), _full((tm, LANES), row), _full((tm, LANES), row)],
        out_specs=[_full((tm, tn), lambda i, j: (i, j)), _full((tm, D_MODEL), row)],
        out_shape=[jax.ShapeDtypeStruct((s, IN_PAD), BF16), jax.ShapeDtypeStruct((s, D_MODEL), BF16)],
        compiler_params=_cparams(("parallel", "arbitrary")),
    )(x, gpre, w_pad, *d_tab)


def _mla_prep(proj, gq, gkv, w_uq, w_uk, w_uv, q_tab, k_tab):
    s = proj.shape[0]
    tm = min(512, s)

    def body(cq_ref, kr_ref, ckv_ref, gq_ref, gkv_ref, wq_ref, wk_ref, wv_ref,
             qc, qsp, qsm, kc, ksp, ksm, q_out, k_out, v_out, qt_out, kt_out, vt_out):
        cqn, _, _ = _rms(cq_ref[...].astype(F32), gq_ref[...])
        q = _dot(cqn.astype(BF16), wq_ref[...])
        q = _rope(q, _tile_lanes(qc[...], 1024), _tile_lanes(qsp[...], 1024), _tile_lanes(qsm[...], 1024), 16)
        q = q * MLA_QSCALE
        q_out[...] = q.astype(BF16)
        qt_out[...] = q.T.astype(BF16)
        ckvn, _, _ = _rms(ckv_ref[...].astype(F32), gkv_ref[...])
        ckvn = ckvn.astype(BF16)
        kr = _rope(kr_ref[...].astype(F32), kc[...], ksp[...], ksm[...], 16)
        k = _dot(ckvn, wk_ref[...]) + _tile_lanes(pltpu.roll(kr, 64, 1), 1024)
        k_out[...] = k.astype(BF16)
        kt_out[...] = k.T.astype(BF16)
        v = _dot(ckvn, wv_ref[...])
        v_out[...] = v.astype(BF16)
        vt_out[...] = v.T.astype(BF16)

    row = lambda i: (i, 0)
    col = lambda i: (0, i)
    cst = lambda i: (0, 0)
    tabs = [_full((tm, LANES), row)] * 6
    return pl.pallas_call(
        body, name="mla_prep", grid=(s // tm,),
        in_specs=[_full((tm, 384), lambda i: (i, CQ_OFF // 384)), _full((tm, 128), lambda i: (i, KR_OFF // 128)),
                  _full((tm, 256), lambda i: (i, CKV_OFF // 256)), _full((1, 384), cst), _full((1, 256), cst),
                  _full((384, 1024), cst), _full((256, 1024), cst), _full((256, 512), cst)] + tabs,
        out_specs=[_full((tm, 1024), row), _full((tm, 1024), row), _full((tm, 512), row),
                   _full((1024, tm), col), _full((1024, tm), col), _full((512, tm), col)],
        out_shape=[jax.ShapeDtypeStruct((s, 1024), BF16), jax.ShapeDtypeStruct((s, 1024), BF16),
                   jax.ShapeDtypeStruct((s, 512), BF16), jax.ShapeDtypeStruct((1024, s), BF16),
                   jax.ShapeDtypeStruct((1024, s), BF16), jax.ShapeDtypeStruct((512, s), BF16)],
        compiler_params=_cparams(("parallel",)),
    )(proj, proj, proj, gq, gkv, w_uq, w_uk, w_uv, *q_tab, *k_tab)


def _mla_fwd(q, k, vt):
    s = q.shape[0]
    tk = min(MLA_FWD_TK, s)
    ratio = MLA_FWD_Q_PER_K if s >= MLA_FWD_Q_PER_K * tk else 1
    tq = ratio * tk
    nq = s // tq

    def body(q_ref, k_ref, vt_ref, o_ref, ot_ref, lse_ref):
        krow = lax.broadcasted_iota(jnp.int32, (tk, tq), 0)
        qcol = lax.broadcasted_iota(jnp.int32, (tk, tq), 1)

        def q_step(i, _):
            r0 = pl.multiple_of(i * tq, tq)
            qs = [q_ref[pl.ds(r0, tq), hh * 128:(hh + 1) * 128] for hh in range(2)]

            def scores(j):
                c0 = pl.multiple_of(j * tk, tk)
                return tuple(_dot_nt(k_ref[pl.ds(c0, tk), hh * 128:(hh + 1) * 128], qs[hh])
                             for hh in range(2))

            def update(j, sts, stats, masked):
                c0 = pl.multiple_of(j * tk, tk)
                new = []
                causal = (krow + (c0 - r0)) <= qcol
                for hh in range(2):
                    m, l, acc = stats[hh]
                    st = jnp.where(causal, sts[hh], NEG) if masked else sts[hh]
                    m_new = jnp.maximum(m, jnp.max(st, axis=0, keepdims=True))
                    alpha = jnp.exp2(m - m_new)
                    p = jnp.exp2(st - m_new)
                    l = alpha * l + jnp.sum(p, axis=0, keepdims=True)
                    acc = acc * alpha + _dot(vt_ref[hh * 64:(hh + 1) * 64, pl.ds(c0, tk)], p.astype(BF16))
                    new.append((m_new, l, acc))
                return tuple(new)

            init = tuple((jnp.full((1, tq), NEG, F32), jnp.zeros((1, tq), F32), jnp.zeros((64, tq), F32))
                         for _ in range(2))
            stats = lax.fori_loop(0, ratio * i, lambda j, st: update(j, scores(j), st, False), init)
            for d in range(ratio):
                stats = update(ratio * i + d, scores(ratio * i + d), stats, True)
            (ma, la, acca), (mb, lb, accb) = stats
            ot = jnp.concatenate([acca / la, accb / lb], axis=0)
            ot_ref[:, pl.ds(r0, tq)] = ot.astype(BF16)
            o_ref[pl.ds(r0, tq), :] = ot.T.astype(BF16)
            lse_ref[:, pl.ds(r0, tq)] = jnp.concatenate(
                [ma + jnp.log2(la), mb + jnp.log2(lb), jnp.zeros((6, tq), F32)], axis=0)
            return 0

        lax.fori_loop(0, nq, q_step, 0)

    return pl.pallas_call(
        body, name="mla_fwd", grid=(4,),
        in_specs=[_full((s, 256), lambda p: (0, p)), _full((s, 256), lambda p: (0, p)),
                  _full((128, s), lambda p: (p, 0))],
        out_specs=[_full((s, 128), lambda p: (0, p)), _full((128, s), lambda p: (p, 0)),
                   _full((8, s), lambda p: (p, 0))],
        out_shape=[jax.ShapeDtypeStruct((s, 512), BF16), jax.ShapeDtypeStruct((512, s), BF16),
                   jax.ShapeDtypeStruct((32, s), F32)],
        compiler_params=_cparams(("parallel",)),
    )(q, k, vt)


def _band_mask(has_prev):
    r = lax.broadcasted_iota(jnp.int32, (Q_BLOCK, 2 * Q_BLOCK), 0)
    c = lax.broadcasted_iota(jnp.int32, (Q_BLOCK, 2 * Q_BLOCK), 1)
    lo = jnp.where(has_prev, r, Q_BLOCK)
    return jnp.logical_and(c >= lo, c <= r + Q_BLOCK)


def _dil_rows(b, d, per_seq):
    r, n = b // per_seq, b % per_seq
    start = r + (d * Q_BLOCK) * n
    prev = start - jnp.where(n > 0, d * Q_BLOCK, 0)
    if d == 1:
        return pl.ds(pl.multiple_of(start, Q_BLOCK), Q_BLOCK), pl.ds(pl.multiple_of(prev, Q_BLOCK), Q_BLOCK)
    return pl.ds(start, Q_BLOCK, stride=d), pl.ds(prev, Q_BLOCK, stride=d)


def _dil_fwd(proj, g, d, name):
    s = proj.shape[0]
    nblk = s // Q_BLOCK
    per_seq = nblk // d

    def body(q_ref, k_ref, v_ref, o_ref, lse_ref, qf, kf, vf, of, lf):
        left = _left_mask()
        hms = (left, jnp.logical_not(left))
        qf[...] = q_ref[...].astype(F32)
        kf[...] = k_ref[...].astype(F32)
        vf[...] = v_ref[...].astype(F32)

        def scores(b):
            rows, prow = _dil_rows(b, d, per_seq)
            qb = qf[rows, :].astype(BF16)
            kk = jnp.concatenate([kf[prow, :], kf[rows, :]], axis=0).astype(BF16)
            return [_dot_nt(jnp.where(hms[hh], qb, jnp.zeros_like(qb)), kk) * DIL_SCALE for hh in range(2)]

        def finish(b, tiles):
            rows, prow = _dil_rows(b, d, per_seq)
            mask = _band_mask((b % per_seq) > 0)
            vv = jnp.concatenate([vf[prow, :], vf[rows, :]], axis=0).astype(BF16)
            outs = []
            for hh in range(2):
                sc = jnp.where(mask, tiles[hh], NEG)
                m = jnp.max(sc, axis=1, keepdims=True)
                p = jnp.exp(sc - m)
                l = jnp.sum(p, axis=1, keepdims=True)
                acc = _dot(p.astype(BF16), jnp.where(hms[hh], vv, jnp.zeros_like(vv)))
                outs.append((acc / l, jnp.broadcast_to(m + jnp.log(l), (Q_BLOCK, LANES))))
            of[rows, :] = outs[0][0] + outs[1][0]
            lf[rows, :] = jnp.where(left, outs[0][1], outs[1][1])

        def step(t, _):
            tiles = [scores(DIL_BLOCKS_PER_STEP * t + u) for u in range(DIL_BLOCKS_PER_STEP)]
            for u in range(DIL_BLOCKS_PER_STEP):
                finish(DIL_BLOCKS_PER_STEP * t + u, tiles[u])
            return 0

        lax.fori_loop(0, nblk // DIL_BLOCKS_PER_STEP, step, 0)
        o_ref[...] = of[...].astype(BF16)
        lse_ref[...] = lf[...]

    blk = lambda off: _full((s, 128), lambda p, off=off: (0, off + p))
    qo, ko, vo = ((off + 512 * g) // 128 for off in (Q_OFF, K_OFF, V_OFF))
    return pl.pallas_call(
        body, name=name, grid=(4,),
        in_specs=[blk(qo), blk(ko), blk(vo)],
        out_specs=[blk(0), blk(0)],
        out_shape=[jax.ShapeDtypeStruct((s, 512), BF16), jax.ShapeDtypeStruct((s, 512), F32)],
        scratch_shapes=[pltpu.VMEM((s, 128), F32)] * 5,
        compiler_params=_cparams(("parallel",)),
    )(proj, proj, proj)


def _merge_fwd(proj, o_mla, od, lsed, wp_mla, wp_dil):
    s = proj.shape[0]
    tm = min(512, s)

    def body(zm_ref, zd_ref, gm_ref, gd_ref, om_ref, o0, o1, o2, l0, l1, l2, wm_ref, wd_ref,
             mg_out, ya_out, yd_out, odil_out, lse_out):
        la, lb, lc = l0[...], l1[...], l2[...]
        lmax = jnp.maximum(jnp.maximum(la, lb), lc)
        ea, eb, ec = jnp.exp(la - lmax), jnp.exp(lb - lmax), jnp.exp(lc - lmax)
        den = ea + eb + ec
        o_dil = (ea * o0[...].astype(F32) + eb * o1[...].astype(F32) + ec * o2[...].astype(F32)) / den
        o_dil = o_dil.astype(BF16)
        odil_out[...] = o_dil
        lse_out[...] = lmax + jnp.log(den)
        zm, zd = zm_ref[...].astype(F32), zd_ref[...].astype(F32)
        pa = (om_ref[...].astype(F32) * (zm * _sigmoid(zm))).astype(BF16)
        pd = (o_dil.astype(F32) * (zd * _sigmoid(zd))).astype(BF16)
        ya = _dot(pa, wm_ref[...])
        yd = _dot(pd, wd_ref[...])
        ya_out[...] = ya.astype(BF16)
        yd_out[...] = yd.astype(BF16)
        mg_out[...] = (_sigmoid(gm_ref[...].astype(F32)) * ya + _sigmoid(gd_ref[...].astype(F32)) * yd).astype(BF16)

    row = lambda i: (i, 0)
    cst = lambda i: (0, 0)
    r512 = _full((tm, 512), row)
    r1024 = _full((tm, 1024), row)
    return pl.pallas_call(
        body, name="merge_fwd", grid=(s // tm,),
        in_specs=[_full((tm, 512), lambda i: (i, Z_MLA // 512)), _full((tm, 512), lambda i: (i, Z_DIL // 512)),
                  _full((tm, 1024), lambda i: (i, G_MLA // 1024)), _full((tm, 1024), lambda i: (i, G_DIL // 1024)),
                  r512, r512, r512, r512, r512, r512, r512, _full((512, 1024), cst), _full((512, 1024), cst)],
        out_specs=[r1024, r1024, r1024, r512, r512],
        out_shape=[jax.ShapeDtypeStruct((s, 1024), BF16), jax.ShapeDtypeStruct((s, 1024), BF16),
                   jax.ShapeDtypeStruct((s, 1024), BF16), jax.ShapeDtypeStruct((s, 512), BF16),
                   jax.ShapeDtypeStruct((s, 512), F32)],
        compiler_params=_cparams(("parallel",)),
    )(proj, proj, proj, proj, o_mla, *od, *lsed, wp_mla, wp_dil)


def _out_loss(merged, w_out, x, target, gpost):
    s = x.shape[0]
    tm = min(512, s)

    def body(mg_ref, w_ref, x_ref, t_ref, g_ref, do_out, dy_out, loss_out, dg_out):
        i = pl.program_id(0)

        @pl.when(i == 0)
        def _():
            loss_out[...] = jnp.zeros_like(loss_out)
            dg_out[...] = jnp.zeros_like(dg_out)

        o = _dot(mg_ref[...], w_ref[...])
        g = g_ref[...]
        n, u, r = _rms(o, g)
        e = (x_ref[...] + n) - t_ref[...]
        loss_out[...] += 0.5 * jnp.sum(jnp.mean(e * e, axis=-1, keepdims=True))
        dy = e * (1.0 / D_MODEL)
        dy_out[...] = dy
        dg_out[...] += jnp.sum(dy * u, axis=0, keepdims=True)
        do_out[...] = _rms_bwd(dy, g, u, r).astype(BF16)

    row = lambda i: (i, 0)
    cst = lambda i: (0, 0)
    return pl.pallas_call(
        body, name="out_loss", grid=(s // tm,),
        in_specs=[_full((tm, 1024), row), _full((1024, 1024), cst), _full((tm, 1024), row), _full((tm, 1024), row),
                  _full((1, 1024), cst)],
        out_specs=[_full((tm, 1024), row), _full((tm, 1024), row), _full((8, LANES), cst), _full((1, 1024), cst)],
        out_shape=[jax.ShapeDtypeStruct((s, 1024), BF16), jax.ShapeDtypeStruct((s, 1024), F32),
                   jax.ShapeDtypeStruct((8, LANES), F32), jax.ShapeDtypeStruct((1, 1024), F32)],
        compiler_params=_cparams(("arbitrary",)),
    )(merged, w_out, x, target, gpost)


def _seg_sum64(x, ones_bd):
    hi = x.astype(BF16)
    lo = (x - hi.astype(F32)).astype(BF16)
    return _dot(hi, ones_bd) + _dot(lo, ones_bd)


def _merge_bwd(do, w_out, merged, proj, ya, yd, o_mla, o_dil, wp_mla, wp_dil):
    s = do.shape[0]
    tm = min(256, s)
    seg = jnp.arange(512) // 64
    ones_bd = (seg[:, None] == seg[None, :]).astype(BF16)

    def body(do_ref, wo_ref, mg_ref, zm_ref, zd_ref, gm_ref, gd_ref, ya_ref, yd_ref, om_ref, od_ref, wm_ref, wd_ref,
             bd_ref, dzm_out, dzd_out, dgm_out, dgd_out, dom_out, dod_out, domt_out, dd_out, dwo_out, dwm_out,
             dwd_out):
        i = pl.program_id(0)

        @pl.when(i == 0)
        def _():
            dwo_out[...] = jnp.zeros_like(dwo_out)
            dwm_out[...] = jnp.zeros_like(dwm_out)
            dwd_out[...] = jnp.zeros_like(dwd_out)

        dov = do_ref[...]
        dwo_out[...] += _dot_tn(mg_ref[...], dov)
        dm = _dot_nt(dov, wo_ref[...])
        for g_ref, y_ref, z_ref, o_ref, w_ref, dz_out, dg_out, dob_out, dd_o, dw_out in (
                (gm_ref, ya_ref, zm_ref, om_ref, wm_ref, dzm_out, dgm_out, dom_out, None, dwm_out),
                (gd_ref, yd_ref, zd_ref, od_ref, wd_ref, dzd_out, dgd_out, dod_out, dd_out, dwd_out)):
            sg = _sigmoid(g_ref[...].astype(F32))
            dg_out[...] = (dm * y_ref[...].astype(F32) * sg * (1.0 - sg)).astype(BF16)
            dy = (dm * sg).astype(BF16)
            z = z_ref[...].astype(F32)
            sz = _sigmoid(z)
            silu = z * sz
            ob = o_ref[...].astype(F32)
            dw_out[...] += _dot_tn((ob * silu).astype(BF16), dy)
            dp = _dot_nt(dy, w_ref[...])
            dz_out[...] = (dp * ob * (sz * (1.0 + z * (1.0 - sz)))).astype(BF16)
            dob = dp * silu
            dob_out[...] = dob.astype(BF16)
            if dd_o is None:
                domt_out[...] = dob.T.astype(BF16)
            else:
                dd_o[...] = _seg_sum64(dob * ob, bd_ref[...])

    row = lambda i: (i, 0)
    cst = lambda i: (0, 0)
    r512 = _full((tm, 512), row)
    r1024 = _full((tm, 1024), row)
    return pl.pallas_call(
        body, name="merge_bwd", grid=(s // tm,),
        in_specs=[r1024, _full((1024, 1024), cst), r1024,
                  _full((tm, 512), lambda i: (i, Z_MLA // 512)), _full((tm, 512), lambda i: (i, Z_DIL // 512)),
                  _full((tm, 1024), lambda i: (i, G_MLA // 1024)), _full((tm, 1024), lambda i: (i, G_DIL // 1024)),
                  r1024, r1024, r512, r512, _full((512, 1024), cst), _full((512, 1024), cst), _full((512, 512), cst)],
        out_specs=[r512, r512, r1024, r1024, r512, r512, _full((512, tm), lambda i: (0, i)), r512,
                   _full((1024, 1024), cst), _full((512, 1024), cst), _full((512, 1024), cst)],
        out_shape=[jax.ShapeDtypeStruct((s, 512), BF16), jax.ShapeDtypeStruct((s, 512), BF16),
                   jax.ShapeDtypeStruct((s, 1024), BF16), jax.ShapeDtypeStruct((s, 1024), BF16),
                   jax.ShapeDtypeStruct((s, 512), BF16), jax.ShapeDtypeStruct((s, 512), BF16),
                   jax.ShapeDtypeStruct((512, s), BF16), jax.ShapeDtypeStruct((s, 512), F32),
                   jax.ShapeDtypeStruct((1024, 1024), F32), jax.ShapeDtypeStruct((512, 1024), F32),
                   jax.ShapeDtypeStruct((512, 1024), F32)],
        compiler_params=_cparams(("arbitrary",)),
    )(do, w_out, merged, proj, proj, proj, proj, ya, yd, o_mla, o_dil, wp_mla, wp_dil, ones_bd)


def _mla_bwd(q, qt, k, kt, v, do, dot, ot, lse):
    s = q.shape[0]
    tk = min(256, s)
    ratio = MLA_BWD_Q_PER_K if s >= MLA_BWD_Q_PER_K * tk else 1
    tq = ratio * tk
    nq, nk = s // tq, s // tk

    def body(q_ref, qt_ref, k_ref, kt_ref, v_ref, do_ref, dot_ref, ot_ref, lse_ref, dqt_out, dkt_out, dvt_out,
             dqt_acc):
        left = _left_mask()
        krow = lax.broadcasted_iota(jnp.int32, (tk, tq), 0)
        qcol = lax.broadcasted_iota(jnp.int32, (tk, tq), 1)
        dqt_acc[...] = jnp.zeros_like(dqt_acc)

        def kv_step(j, _):
            c0 = pl.multiple_of(j * tk, tk)
            vv = v_ref[pl.ds(c0, tk), :]
            khs = [k_ref[pl.ds(c0, tk), hh * 128:(hh + 1) * 128] for hh in range(2)]
            kths = [kt_ref[hh * 128:(hh + 1) * 128, pl.ds(c0, tk)] for hh in range(2)]
            vms = [jnp.where(left if hh == 0 else jnp.logical_not(left), vv, jnp.zeros_like(vv)) for hh in range(2)]

            def scores(i):
                r0 = pl.multiple_of(jnp.minimum(i, nq - 1) * tq, tq)
                dov = do_ref[pl.ds(r0, tq), :]
                return tuple((_dot_nt(khs[hh], q_ref[pl.ds(r0, tq), hh * 128:(hh + 1) * 128]),
                              _dot_nt(vms[hh], dov)) for hh in range(2))

            def update(i, tiles, acc, masked):
                r0 = pl.multiple_of(i * tq, tq)
                new = []
                for hh in range(2):
                    dkt, dvt = acc[hh]
                    st, dp = tiles[hh]
                    hrows = slice(hh * 128, (hh + 1) * 128)
                    drows = slice(hh * 64, (hh + 1) * 64)
                    doth = dot_ref[drows, pl.ds(r0, tq)]
                    dd = jnp.sum(doth.astype(F32) * ot_ref[drows, pl.ds(r0, tq)].astype(F32), axis=0, keepdims=True)
                    p = jnp.exp2(st - lse_ref[hh:hh + 1, pl.ds(r0, tq)])
                    if masked:
                        p = jnp.where((krow + (c0 - r0)) <= qcol, p, 0.0)
                    ds = (p * (dp - dd)).astype(BF16)
                    dvt = dvt + _dot_nt(doth, p.astype(BF16))
                    dkt = dkt + _dot_nt(qt_ref[hrows, pl.ds(r0, tq)], ds)
                    dqt_acc[hrows, pl.ds(r0, tq)] += _dot(kths[hh], ds)
                    new.append((dkt, dvt))
                return tuple(new)

            init = tuple((jnp.zeros((128, tk), F32), jnp.zeros((64, tk), F32)) for _ in range(2))
            i0 = j // ratio
            acc = update(i0, scores(i0), init, True)
            acc = lax.fori_loop(i0 + 1, nq, lambda i, a: update(i, scores(i), a, False), acc)
            for hh in range(2):
                dkt_out[hh * 128:(hh + 1) * 128, pl.ds(c0, tk)] = (acc[hh][0] * (1.0 / LOG2E)).astype(BF16)
                dvt_out[hh * 64:(hh + 1) * 64, pl.ds(c0, tk)] = acc[hh][1].astype(BF16)
            return 0

        lax.fori_loop(0, nk, kv_step, 0)
        dqt_out[...] = (dqt_acc[...] * MLA_SCALE).astype(BF16)

    b256 = _full((s, 256), lambda p: (0, p))
    b128 = _full((s, 128), lambda p: (0, p))
    t256 = _full((256, s), lambda p: (p, 0))
    t128 = _full((128, s), lambda p: (p, 0))
    return pl.pallas_call(
        body, name="mla_bwd", grid=(4,),
        in_specs=[b256, t256, b256, t256, b128, b128, t128, t128, _full((8, s), lambda p: (p, 0))],
        out_specs=[t256, t256, t128],
        out_shape=[jax.ShapeDtypeStruct((1024, s), BF16), jax.ShapeDtypeStruct((1024, s), BF16),
                   jax.ShapeDtypeStruct((512, s), BF16)],
        scratch_shapes=[pltpu.VMEM((256, s), F32)],
        compiler_params=_cparams(("parallel",)),
    )(q, qt, k, kt, v, do, dot, ot, lse)


def _mla_prep_bwd(dq, dk, dv, proj, gq, gkv, w_uq, w_uk, w_uv, q_tab, k_tab):
    s = proj.shape[0]
    tm = min(256, s)

    def body(dqt_ref, dkt_ref, dvt_ref, cq_ref, ckv_ref, gq_ref, gkv_ref, wq_ref, wk_ref, wv_ref,
             qc, qsp, qsm, kc, ksp, ksm,
             dcq_out, dkr_out, dckv_out, dwq_out, dwk_out, dwv_out, dgq_out, dgkv_out):
        i = pl.program_id(0)

        @pl.when(i == 0)
        def _():
            for r in (dwq_out, dwk_out, dwv_out, dgq_out, dgkv_out):
                r[...] = jnp.zeros_like(r)

        dqu = _unrope(dqt_ref[...].astype(F32).T, _tile_lanes(qc[...], 1024), _tile_lanes(qsp[...], 1024),
                      _tile_lanes(qsm[...], 1024), 16).astype(BF16)
        gq = gq_ref[...]
        cqn, xh, r = _rms(cq_ref[...].astype(F32), gq)
        dwq_out[...] += _dot_tn(cqn.astype(BF16), dqu)
        dcqn = _dot_nt(dqu, wq_ref[...])
        dgq_out[...] += jnp.sum(dcqn * xh, axis=0, keepdims=True)
        dcq_out[...] = _rms_bwd(dcqn, gq, xh, r).astype(BF16)

        dkf = dkt_ref[...].astype(F32).T
        dkb = dkf.astype(BF16)
        dsum = dkf[:, 0:128]
        for h in range(1, MLA_HEADS):
            dsum = dsum + dkf[:, h * 128:(h + 1) * 128]
        dkr_out[...] = _unrope(pltpu.roll(dsum, 64, 1), kc[...], ksp[...], ksm[...], 16).astype(BF16)

        dvb = dvt_ref[...].astype(F32).T.astype(BF16)
        gkv = gkv_ref[...]
        ckvn, xh2, r2 = _rms(ckv_ref[...].astype(F32), gkv)
        ckvn = ckvn.astype(BF16)
        dwk_out[...] += _dot_tn(ckvn, dkb)
        dwv_out[...] += _dot_tn(ckvn, dvb)
        dckvn = _dot_nt(dkb, wk_ref[...]) + _dot_nt(dvb, wv_ref[...])
        dgkv_out[...] += jnp.sum(dckvn * xh2, axis=0, keepdims=True)
        dckv_out[...] = _rms_bwd(dckvn, gkv, xh2, r2).astype(BF16)

    row = lambda i: (i, 0)
    cst = lambda i: (0, 0)
    tabs = [_full((tm, LANES), row)] * 6
    return pl.pallas_call(
        body, name="mla_prep_bwd", grid=(s // tm,),
        in_specs=[_full((1024, tm), lambda i: (0, i)), _full((1024, tm), lambda i: (0, i)),
                  _full((512, tm), lambda i: (0, i)),
                  _full((tm, 384), lambda i: (i, CQ_OFF // 384)), _full((tm, 256), lambda i: (i, CKV_OFF // 256)),
                  _full((1, 384), cst), _full((1, 256), cst),
                  _full((384, 1024), cst), _full((256, 1024), cst), _full((256, 512), cst)] + tabs,
        out_specs=[_full((tm, 384), row), _full((tm, 128), row), _full((tm, 256), row),
                   _full((384, 1024), cst), _full((256, 1024), cst), _full((256, 512), cst),
                   _full((1, 384), cst), _full((1, 256), cst)],
        out_shape=[jax.ShapeDtypeStruct((s, 384), BF16), jax.ShapeDtypeStruct((s, 128), BF16),
                   jax.ShapeDtypeStruct((s, 256), BF16),
                   jax.ShapeDtypeStruct((384, 1024), F32), jax.ShapeDtypeStruct((256, 1024), F32),
                   jax.ShapeDtypeStruct((256, 512), F32),
                   jax.ShapeDtypeStruct((1, 384), F32), jax.ShapeDtypeStruct((1, 256), F32)],
        compiler_params=_cparams(("arbitrary",)),
    )(dq, dk, dv, proj, proj, gq, gkv, w_uq, w_uk, w_uv, *q_tab, *k_tab)


def _dil_bwd(proj, g, do, lse, dd, tabs, d, name):
    s = proj.shape[0]
    nblk = s // Q_BLOCK
    per_seq = nblk // d

    def body(q_ref, k_ref, v_ref, do_ref, lse_ref, dd_ref, c_ref, sp_ref, sm_ref, dq_out, dk_out, dv_out,
             qf, kf, vf, dof, dq_acc, dk_acc, dv_acc):
        left = _left_mask()
        hms = (left, jnp.logical_not(left))
        qf[...] = q_ref[...].astype(F32)
        kf[...] = k_ref[...].astype(F32)
        vf[...] = v_ref[...].astype(F32)
        dof[...] = do_ref[...].astype(F32)
        dk_acc[...] = jnp.zeros_like(dk_acc)
        dv_acc[...] = jnp.zeros_like(dv_acc)

        def scores(b):
            rows, prow = _dil_rows(b, d, per_seq)
            qb, dob = qf[rows, :].astype(BF16), dof[rows, :].astype(BF16)
            kk = jnp.concatenate([kf[prow, :], kf[rows, :]], axis=0).astype(BF16)
            vv = jnp.concatenate([vf[prow, :], vf[rows, :]], axis=0).astype(BF16)
            zero = jnp.zeros_like(qb)
            out = []
            for hh in range(2):
                qm, dom = jnp.where(hms[hh], qb, zero), jnp.where(hms[hh], dob, zero)
                out.append((_dot_nt(qm, kk) * DIL_SCALE, _dot_nt(dom, vv)))
            return out

        def finish(b, tiles):
            rows, prow = _dil_rows(b, d, per_seq)
            mask = _band_mask((b % per_seq) > 0)
            qb, dob = qf[rows, :].astype(BF16), dof[rows, :].astype(BF16)
            kk = jnp.concatenate([kf[prow, :], kf[rows, :]], axis=0).astype(BF16)
            lse_b, dd_b = lse_ref[rows, :], dd_ref[rows, :]
            zero = jnp.zeros_like(qb)
            dq = jnp.zeros((Q_BLOCK, LANES), F32)
            dk = jnp.zeros((2 * Q_BLOCK, LANES), F32)
            dv = jnp.zeros((2 * Q_BLOCK, LANES), F32)
            for hh in range(2):
                hm = hms[hh]
                qm, dom = jnp.where(hm, qb, zero), jnp.where(hm, dob, zero)
                lse_h = _tile_lanes(_expand_half(lse_b, hh, left), 2 * Q_BLOCK)
                dd_h = _tile_lanes(_expand_half(dd_b, hh, left), 2 * Q_BLOCK)
                sc, dp = tiles[hh]
                p = jnp.where(mask, jnp.exp(sc - lse_h), 0.0)
                ds = (p * (dp - dd_h) * DIL_SCALE).astype(BF16)
                dq = dq + _dot(ds, jnp.where(hm, kk, jnp.zeros_like(kk)))
                dk = dk + _dot_tn(ds, qm)
                dv = dv + _dot_tn(p.astype(BF16), dom)
            dq_acc[rows, :] = dq
            dk_acc[prow, :] += dk[0:Q_BLOCK]
            dv_acc[prow, :] += dv[0:Q_BLOCK]
            dk_acc[rows, :] += dk[Q_BLOCK:]
            dv_acc[rows, :] += dv[Q_BLOCK:]

        def step(t, _):
            tiles = [scores(DIL_BLOCKS_PER_STEP * t + u) for u in range(DIL_BLOCKS_PER_STEP)]
            for u in range(DIL_BLOCKS_PER_STEP):
                finish(DIL_BLOCKS_PER_STEP * t + u, tiles[u])
            return 0

        lax.fori_loop(0, nblk // DIL_BLOCKS_PER_STEP, step, 0)
        dq_out[...] = _unrope(dq_acc[...], c_ref[...], sp_ref[...], sm_ref[...], 8).astype(BF16)
        dk_out[...] = _unrope(dk_acc[...], c_ref[...], sp_ref[...], sm_ref[...], 8).astype(BF16)
        dv_out[...] = dv_acc[...].astype(BF16)

    blk = lambda off: _full((s, 128), lambda p, off=off: (0, off + p))
    tab = _full((s, 128), lambda p: (0, 0))
    qo, ko, vo = ((off + 512 * g) // 128 for off in (Q_OFF, K_OFF, V_OFF))
    return pl.pallas_call(
        body, name=name, grid=(4,),
        in_specs=[blk(qo), blk(ko), blk(vo), blk(0), blk(0), blk(0), tab, tab, tab],
        out_specs=[blk(0), blk(0), blk(0)],
        out_shape=[jax.ShapeDtypeStruct((s, 512), BF16)] * 3,
        scratch_shapes=[pltpu.VMEM((s, 128), F32)] * 7,
        compiler_params=_cparams(("parallel",)),
    )(proj, proj, proj, do, lse, dd, *tabs)


def _chip_copies(ins, outs, sems, outgoing):
    x, y, c, chips = _place()
    myq = 2 * x + y
    n = len(ins)

    def chunk(a, j):
        q = 2 * chips[j][0] + chips[j][1]
        return _remote(ins[a].at[q], outs[a].at[myq if outgoing else q], sems, 2 * a, j, (*chips[j], c))

    if outgoing is None:
        return [pltpu.make_async_copy(ins[a].at[myq], outs[a].at[myq], sems.at[2 * a, 3]) for a in range(n)]
    return [chunk(a, j) for j in range(3) for a in range(n)]


def _dh_bwd(dproj, w_pad, x, gpre, dy, pairs):
    s = x.shape[0]
    tm, tk = min(512, s), 1408
    ni, nk = s // tm, IN_PAD // tk
    n = len(pairs)

    def body(*refs):
        dp_ref, w_ref, x_ref, g_ref, dy_ref = refs[:5]
        ins, gx_out, dg_out, outs = refs[5:5 + n], refs[5 + n], refs[6 + n], refs[7 + n:7 + 2 * n]
        acc, sems = refs[7 + 2 * n], refs[8 + 2 * n]
        i, kk = pl.program_id(0), pl.program_id(1)

        @pl.when(jnp.logical_and(i == 0, kk == 0))
        def _():
            dg_out[...] = jnp.zeros_like(dg_out)
            for cp in _chip_copies(ins, outs, sems, None) + _chip_copies(ins, outs, sems, True):
                cp.start()

        @pl.when(kk == 0)
        def _():
            acc[...] = jnp.zeros_like(acc)

        acc[...] += _dot_nt(dp_ref[...], w_ref[...])

        @pl.when(kk == nk - 1)
        def _():
            g = g_ref[...]
            _, xh, r = _rms(x_ref[...], g)
            dh = acc[...]
            dg_out[...] += jnp.sum(dh * xh, axis=0, keepdims=True)
            gx_out[...] = dy_ref[...] + _rms_bwd(dh, g, xh, r)

        @pl.when(jnp.logical_and(i == ni - 1, kk == nk - 1))
        def _():
            for cp in _chip_copies(ins, outs, sems, False):
                cp.wait_recv()
            for cp in _chip_copies(ins, outs, sems, True):
                cp.wait_send()
            for cp in _chip_copies(ins, outs, sems, None):
                cp.wait()

    row = lambda i, k: (i, 0)
    hbm = pl.BlockSpec(memory_space=pl.ANY)
    res = pl.pallas_call(
        body, name="dh_bwd", grid=(ni, nk),
        in_specs=[_full((tm, tk), lambda i, k: (i, k)), _full((1024, tk), lambda i, k: (0, k)),
                  _full((tm, 1024), row), _full((1, 1024), lambda i, k: (0, 0)), _full((tm, 1024), row)] + [hbm] * n,
        out_specs=[_full((tm, 1024), row), _full((1, 1024), lambda i, k: (0, 0))] + [hbm] * n,
        out_shape=[jax.ShapeDtypeStruct((s, 1024), F32), jax.ShapeDtypeStruct((1, 1024), F32)]
        + [jax.ShapeDtypeStruct(a.shape, a.dtype) for a in pairs],
        scratch_shapes=[pltpu.VMEM((tm, 1024), F32), pltpu.SemaphoreType.DMA((2 * n, 4))],
        compiler_params=_cparams(("arbitrary", "arbitrary")),
    )(dproj, w_pad, x, gpre, dy, *pairs)
    return res[0], res[1], res[2:]


def _dw_in(h, dproj):
    s = h.shape[0]
    ts, tn = min(1024, s), 768
    ns = s // ts

    def body(h_ref, dp_ref, o_ref, acc):
        k = pl.program_id(1)

        @pl.when(k == 0)
        def _():
            acc[...] = jnp.zeros_like(acc)

        acc[...] += _dot_tn(h_ref[...], dp_ref[...])

        @pl.when(k == ns - 1)
        def _():
            o_ref[...] = acc[...].astype(BF16)

    return pl.pallas_call(
        body, name="dw_in", grid=(IN_PAD // tn, ns),
        in_specs=[_full((ts, 1024), lambda j, k: (k, 0)), _full((ts, tn), lambda j, k: (k, j))],
        out_specs=_full((1024, tn), lambda j, k: (0, j)),
        out_shape=jax.ShapeDtypeStruct((1024, IN_PAD), BF16),
        scratch_shapes=[pltpu.VMEM((1024, tn), F32)],
        compiler_params=_cparams(("parallel", "arbitrary")),
    )(h, dproj)


def _remote(src, dst, sems, row, k, to):
    return pltpu.make_async_remote_copy(src_ref=src, dst_ref=dst, send_sem=sems.at[row, k], recv_sem=sems.at[row + 1, k],
                                        device_id=to, device_id_type=pl.DeviceIdType.MESH)


def _place():
    x, y, c = lax.axis_index("x"), lax.axis_index("y"), lax.axis_index("c")
    return x, y, c, [(1 - x, y), (x, 1 - y), (1 - x, 1 - y)]


def _gather_weights(arrays):
    n = len(arrays)

    def body(*refs):
        ins, outs, sems = refs[:n], refs[n:2 * n], refs[2 * n]
        x, y, c, chips = _place()
        me, sib = (x, y, c), (x, y, 1 - c)
        idx = lambda p: 4 * p[0] + 2 * p[1] + p[2]

        def copy(a, k, block, to, from_input=False):
            src = ins[a] if from_input else outs[a].at[idx(block)]
            return _remote(src, outs[a].at[idx(block)], sems, 2 * a, k, to)

        own = [pltpu.make_async_copy(ins[a], outs[a].at[idx(me)], sems.at[2 * a, 7]) for a in range(n)]
        first = [copy(a, 0, me, sib, True) for a in range(n)]
        first += [copy(a, 1 + j, me, (*chip, c), True) for j, chip in enumerate(chips) for a in range(n)]
        for cp in own + first:
            cp.start()
        passed = []
        for j, chip in enumerate(chips):
            for a in range(n):
                copy(a, 1 + j, (*chip, c), me).wait_recv()
                passed.append(copy(a, 4 + j, (*chip, c), sib))
                passed[-1].start()
        for a in range(n):
            copy(a, 0, sib, me).wait_recv()
        for j, chip in enumerate(chips):
            for a in range(n):
                copy(a, 4 + j, (*chip, 1 - c), me).wait_recv()
        for cp in first + passed:
            cp.wait_send()
        for cp in own:
            cp.wait()

    hbm = pl.BlockSpec(memory_space=pl.ANY)
    return pl.pallas_call(
        body, name="gather_weights", in_specs=[hbm] * n, out_specs=[hbm] * n,
        out_shape=[jax.ShapeDtypeStruct((N_DEV,) + a.shape, a.dtype) for a in arrays],
        scratch_shapes=[pltpu.SemaphoreType.DMA((2 * n, N_DEV))],
    )(*arrays)


def _pair_exchange(chunks):
    n = len(chunks)

    def body(*refs):
        ins, outs, sems = refs[:n], refs[n:2 * n], refs[2 * n]
        x, y, c, _ = _place()
        sent = [_remote(ins[a].at[2 * q + (1 - c)], outs[a].at[q], sems, 2 * a, q, (x, y, 1 - c))
                for a in range(n) for q in range(4)]
        for cp in sent:
            cp.start()
        for cp in sent:
            cp.wait_recv()
        for cp in sent:
            cp.wait_send()

    hbm = pl.BlockSpec(memory_space=pl.ANY)
    return pl.pallas_call(
        body, name="pair_exchange", in_specs=[hbm] * n, out_specs=[hbm] * n,
        out_shape=[jax.ShapeDtypeStruct((4,) + a.shape[1:], a.dtype) for a in chunks],
        scratch_shapes=[pltpu.SemaphoreType.DMA((2 * n, 4))],
    )(*chunks)


def _pair_sum(core, chunks, recv, name, tr):
    _, rows, cols = chunks.shape

    def body(c_ref, a_ref, b_ref, o_ref):
        o_ref[...] = (a_ref[...].astype(F32) + b_ref[...].astype(F32)).astype(BF16)

    blk = lambda f: _full((1, tr, cols), f)
    return pl.pallas_call(
        body, name=name, out_shape=jax.ShapeDtypeStruct((4, rows, cols), BF16),
        grid_spec=pltpu.PrefetchScalarGridSpec(
            num_scalar_prefetch=1, grid=(4, rows // tr),
            in_specs=[blk(lambda q, i, c: (2 * q + c[0], i, 0)), blk(lambda q, i, c: (q, i, 0))],
            out_specs=blk(lambda q, i, c: (q, i, 0))),
        compiler_params=_cparams(("parallel", "parallel")),
    )(core, chunks, recv)


def _packet_exchange(packet):
    def body(pk, pk_out, sems):
        x, y, c, _ = _place()
        me = 4 * x + 2 * y + c
        flip = lambda v, b: (1 - v) if b else v

        def small(j, outgoing):
            peer = (flip(x, (j >> 2) & 1), flip(y, (j >> 1) & 1), flip(c, j & 1))
            slot = me if outgoing else 4 * peer[0] + 2 * peer[1] + peer[2]
            return _remote(pk, pk_out.at[slot], sems, 0, j, peer)

        own = pltpu.make_async_copy(pk, pk_out.at[me], sems.at[0, 0])
        sent = [small(j, True) for j in range(1, N_DEV)]
        for cp in [own] + sent:
            cp.start()
        for j in range(1, N_DEV):
            small(j, False).wait_recv()
        for cp in sent:
            cp.wait_send()
        own.wait()

    hbm = pl.BlockSpec(memory_space=pl.ANY)
    return pl.pallas_call(
        body, name="packet_exchange", in_specs=[hbm], out_specs=hbm,
        out_shape=jax.ShapeDtypeStruct((N_DEV,) + packet.shape, packet.dtype),
        scratch_shapes=[pltpu.SemaphoreType.DMA((2, N_DEV))],
    )(packet)


def _adam_math(w, g, m, v):
    m = ADAM_B1 * m + (1.0 - ADAM_B1) * g
    v = ADAM_B2 * v + (1.0 - ADAM_B2) * (g * g)
    m_hat = m / (1.0 - ADAM_B1 ** ADAM_STEP)
    v_hat = v / (1.0 - ADAM_B2 ** ADAM_STEP)
    delta = -ADAM_LR * (m_hat / (jnp.sqrt(v_hat) + ADAM_EPS) + ADAM_WD * w)
    return delta, m, v


def _adam(recv, w, m, v, name, tr):
    _, rows, cols = w.shape

    def body(r_ref, w_ref, m_ref, v_ref, g_out, d_out, m_out, v_out):
        g = r_ref[0].astype(F32)
        for k in range(1, 4):
            g = g + r_ref[k].astype(F32)
        g_out[0] = g
        d_out[0], m_out[0], v_out[0] = _adam_math(w_ref[0], g, m_ref[0], v_ref[0])

    blk = _full((1, tr, cols), lambda i: (0, i, 0))
    return pl.pallas_call(
        body, name=name, grid=(rows // tr,),
        in_specs=[_full((4, tr, cols), lambda i: (0, i, 0)), blk, blk, blk],
        out_specs=[blk] * 4,
        out_shape=[jax.ShapeDtypeStruct(w.shape, F32)] * 4,
        compiler_params=_cparams(("parallel",)),
    )(recv, w, m, v)


def _adam_gains(recv, gains, gains_m, gains_v):
    def body(*refs):
        r_ref, w, m, v = refs[0], refs[1:5], refs[5:9], refs[9:13]
        g_out, d_out, m_out, v_out, loss_out = refs[13:17], refs[17:21], refs[21:25], refs[25:29], refs[29]
        tot = r_ref[0:1, :]
        for k in range(1, N_DEV):
            tot = tot + r_ref[k:k + 1, :]
        for t in range(4):
            g = tot[:, GAIN_OFFS[t]:GAIN_OFFS[t] + GAIN_WIDTHS[t]]
            g_out[t][...] = g
            d_out[t][...], m_out[t][...], v_out[t][...] = _adam_math(w[t][...], g, m[t][...], v[t][...])
        loss_out[...] = tot[:, LOSS_OFF:LOSS_OFF + LANES]

    shapes = [jax.ShapeDtypeStruct((1, n), F32) for n in GAIN_WIDTHS]
    return pl.pallas_call(
        body, name="adam_gains", out_shape=shapes * 4 + [jax.ShapeDtypeStruct((1, LANES), F32)],
    )(recv, *gains, *gains_m, *gains_v)


def _local_step(x, positions, gains, weights, target):
    gpre, gq, gkv, gpost = gains
    w_pad, w_uq, w_uk, w_uv, wp_mla, wp_dil, w_out = weights
    q_tab, k_tab, d_tab = _rope_tables(positions)

    proj, h = _inproj(x, gpre, w_pad, d_tab)
    q, k, v, qt, kt, vt = _mla_prep(proj, gq, gkv, w_uq, w_uk, w_uv, q_tab, k_tab)
    o_mla, ot_mla, lse_mla = _mla_fwd(q, k, vt)

    od, lsed = [], []
    for g, d in enumerate(DIL_DILATIONS):
        o_g, lse_g = _dil_fwd(proj, g, d, "dil_fwd_%d" % g)
        od.append(o_g)
        lsed.append(lse_g)

    merged, ya, yd, o_dil, lse_dil = _merge_fwd(proj, o_mla, od, lsed, wp_mla, wp_dil)
    do, dy, loss, dgpost = _out_loss(merged, w_out, x, target, gpost)

    (dz_mla, dz_dil, dg_mla, dg_dil, do_mla, do_dil, dot_mla, dd_dil, dw_out, dwp_mla, dwp_dil) = _merge_bwd(
        do, w_out, merged, proj, ya, yd, o_mla, o_dil, wp_mla, wp_dil)

    dq, dk, dv = _mla_bwd(q, qt, k, kt, v, do_mla, dot_mla, ot_mla, lse_mla)
    dcq, dkr, dckv, dw_uq, dw_uk, dw_uv, dgq, dgkv = _mla_prep_bwd(dq, dk, dv, proj, gq, gkv, w_uq, w_uk, w_uv,
                                                                   q_tab, k_tab)

    dqs, dks, dvs = [], [], []
    for g, d in enumerate(DIL_DILATIONS):
        dq_g, dk_g, dv_g = _dil_bwd(proj, g, do_dil, lse_dil, dd_dil, d_tab, d, "dil_bwd_%d" % g)
        dqs.append(dq_g)
        dks.append(dk_g)
        dvs.append(dv_g)

    dproj = jnp.concatenate([dz_mla, dz_dil, dg_mla, dg_dil] + dqs + dks + dvs + [dcq, dkr, dckv], axis=1)
    dw_in = _dw_in(h, dproj)
    return loss, (dproj, dy), (dgq, dgkv, dgpost), (dw_in, dw_uq, dw_uk, dw_uv, dwp_mla, dwp_dil, dw_out)


ADAM_ROWS = (256, 384, 256, 512, 512, 128)
PAIR_ROWS = (512, 384, 256, 512, 512, 128)


def kernel(x, positions, pre_norm_g, w_in, q_norm_g, w_uq, kv_norm_g, w_ukv, w_proj_mla, w_proj_dil, w_out, post_norm_g, loss_target, m_pre_norm_g, m_w_in, m_q_norm_g, m_w_uq, m_kv_norm_g, m_w_ukv, m_w_proj_mla, m_w_proj_dil, m_w_out, m_post_norm_g, v_pre_norm_g, v_w_in, v_q_norm_g, v_w_uq, v_kv_norm_g, v_w_ukv, v_w_proj_mla, v_w_proj_dil, v_w_out, v_post_norm_g):
    big_w = (w_in, w_uq, w_ukv, w_proj_mla, w_proj_dil, w_out)
    big_m = (m_w_in, m_w_uq, m_w_ukv, m_w_proj_mla, m_w_proj_dil, m_w_out)
    big_v = (v_w_in, v_w_uq, v_w_ukv, v_w_proj_mla, v_w_proj_dil, v_w_out)
    gains = (pre_norm_g, q_norm_g, kv_norm_g, post_norm_g)
    gains_m = (m_pre_norm_g, m_q_norm_g, m_kv_norm_g, m_post_norm_g)
    gains_v = (v_pre_norm_g, v_q_norm_g, v_kv_norm_g, v_post_norm_g)

    gathered = _gather_weights([w[0].astype(BF16) for w in big_w])
    weights = _assemble_weights(*gathered)

    loss, (dproj, dy), (dgq, dgkv, dgpost), dweights = _local_step(x[0], positions[0], gains, weights,
                                                                   loss_target[0])

    chunks = _grad_chunks(*dweights)
    from_sibling = _pair_exchange(chunks)
    core = lax.axis_index("c").astype(jnp.int32).reshape(1)
    pairs = [_pair_sum(core, chunks[t], from_sibling[t], "pair_sum_%d" % t, PAIR_ROWS[t]) for t in range(6)]
    grad_x, dgpre, received = _dh_bwd(dproj, weights[0], x[0], pre_norm_g, dy, pairs)
    packet = _packet_exchange(jnp.concatenate([dgpre, dgq, dgkv, dgpost, loss[0:1]], axis=1))

    big = [_adam(received[t], big_w[t], big_m[t], big_v[t], "adam_%d" % t, ADAM_ROWS[t]) for t in range(6)]
    small = _adam_gains(packet.reshape(N_DEV, PACKET), gains, gains_m, gains_v)

    def interleave(kind):
        s_pre, s_q, s_kv, s_post = small[4 * kind:4 * kind + 4]
        b_in, b_uq, b_ukv, b_pm, b_pd, b_out = (big[t][kind] for t in range(6))
        return [s_pre, b_in, s_q, b_uq, s_kv, b_ukv, b_pm, b_pd, b_out, s_post]

    return (small[16][0, 0], grad_x[None], *interleave(0), *interleave(1), *interleave(2), *interleave(3))
```

```python
import numpy as np
import jax
import jax.numpy as jnp
from jax import lax
from jax.experimental import pallas as pl
from jax.experimental.pallas import tpu as pltpu

F32 = jnp.float32
BF16 = jnp.bfloat16

D_MODEL = 1024
NORM_EPS = 1e-6
ROPE_THETA = 500000.0
N_DEV = 8
LANES = 128
NEG = -1e30

MLA_HEADS = 8
MLA_Q_RANK = 384
MLA_KV_RANK = 256
MLA_SCALE = 96.0 ** -0.5
LOG2E = 1.4426950408889634
MLA_QSCALE = MLA_SCALE * LOG2E
MLA_FWD_TK, MLA_FWD_Q_PER_K = 256, 2
MLA_BWD_Q_PER_K = 2
DIL_DILATIONS = (1, 4, 16)
DIL_SCALE = 0.125
Q_BLOCK = 128
DIL_BLOCKS_PER_STEP = 4

Z_MLA, Z_DIL, G_MLA, G_DIL = 0, 512, 1024, 2048
Q_OFF, K_OFF, V_OFF = 3072, 4608, 6144
CQ_OFF, KR_OFF, CKV_OFF, IN_PAD = 7680, 8064, 8192, 8448
IN_WIDTH = 8352
SHARD_W = IN_WIDTH // 8
IN_SEGS = ((0, 384, CQ_OFF), (384, 256, CKV_OFF), (640, 32, KR_OFF), (672, 1536, Q_OFF), (2208, 1536, K_OFF),
           (3744, 1536, V_OFF), (5280, 512, Z_MLA), (5792, 512, Z_DIL), (6304, 1024, G_MLA), (7328, 1024, G_DIL))

GAIN_OFFS = (0, 1024, 1408, 1664)
GAIN_WIDTHS = (1024, 384, 256, 1024)
LOSS_OFF, PACKET = 2688, 2816

ADAM_LR, ADAM_B1, ADAM_B2, ADAM_EPS, ADAM_WD, ADAM_STEP = 0.001, 0.9, 0.999, 1e-08, 0.01, 10

VMEM_LIMIT_MB = 56


def _cparams(sem=None, vmem_mb=VMEM_LIMIT_MB):
    return pltpu.CompilerParams(dimension_semantics=sem, vmem_limit_bytes=vmem_mb * 1024 * 1024)


def _dot(a, b):
    return jnp.dot(a, b, preferred_element_type=F32)


def _dot_nt(a, b):
    return lax.dot_general(a, b, (((1,), (1,)), ((), ())), preferred_element_type=F32)


def _dot_tn(a, b):
    return lax.dot_general(a, b, (((0,), (0,)), ((), ())), preferred_element_type=F32)


def _tile_lanes(t, width):
    return t if width == t.shape[1] else jnp.tile(t, (1, width // t.shape[1]))


def _rope(x, c, sp, sm, a):
    n = x.shape[1]
    return x * c + pltpu.roll(x, a, 1) * sp + pltpu.roll(x, n - a, 1) * sm


def _unrope(dy, c, sp, sm, a):
    n = dy.shape[1]
    return dy * c + pltpu.roll(dy * sp, n - a, 1) + pltpu.roll(dy * sm, a, 1)


def _sigmoid(z):
    return 1.0 / (1.0 + jnp.exp(-z))


def _left_mask():
    return lax.broadcasted_iota(jnp.int32, (1, LANES), 1) < 64


def _expand_half(x, hh, left):
    r = pltpu.roll(x, 64, 1)
    return jnp.where(left, x, r) if hh == 0 else jnp.where(left, r, x)


def _rms(xv, g):
    r = lax.rsqrt(jnp.mean(xv * xv, axis=-1, keepdims=True) + NORM_EPS)
    xh = xv * r
    return xh * g, xh, r


def _rms_bwd(dout, g, xh, r):
    dxh = dout * g
    return r * (dxh - xh * jnp.mean(dxh * xh, axis=-1, keepdims=True))


def _full(shape, index_map):
    return pl.BlockSpec(shape, index_map)


def _w_in_pieces():
    out = []
    for s, n, off in sorted(IN_SEGS, key=lambda t: t[2]):
        c = s
        while c < s + n:
            k = c // SHARD_W
            e = min(s + n, (k + 1) * SHARD_W)
            out.append((k, c - k * SHARD_W, e - c, off + (c - s)))
            c = e
    return out


def _assemble_w_in(g):
    parts, cur = [], 0
    for k, a, w, off in _w_in_pieces():
        if off > cur:
            parts.append(jnp.zeros((D_MODEL, off - cur), g.dtype))
        parts.append(g[k, :, a:a + w])
        cur = off + w
    if cur < IN_PAD:
        parts.append(jnp.zeros((D_MODEL, IN_PAD - cur), g.dtype))
    return jnp.concatenate(parts, axis=1)


def _dw_in_chunks(dw):
    chunks = []
    for dev in range(N_DEV):
        mine = sorted((p for p in _w_in_pieces() if p[0] == dev), key=lambda p: p[1])
        chunks.append(jnp.concatenate([dw[:, off:off + w] for k, a, w, off in mine], axis=1))
    return jnp.stack(chunks)


def _assemble_small(g_uq, g_ukv, g_pm, g_pd, g_out):
    w_uq_pad = jnp.pad(g_uq.transpose(1, 0, 2), ((0, 0), (0, 0), (0, 32))).reshape(384, 1024)
    ukv = g_ukv.transpose(1, 0, 2)
    w_uk_pad = jnp.pad(ukv[:, :, :64], ((0, 0), (0, 0), (0, 64))).reshape(256, 1024)
    w_uv = ukv[:, :, 64:].reshape(256, 512)
    wp_mla = g_pm.transpose(1, 0, 2).reshape(512, 1024)
    wp_dil = g_pd.transpose(1, 0, 2).reshape(512, 1024)
    return w_uq_pad, w_uk_pad, w_uv, wp_mla, wp_dil, g_out.reshape(1024, 1024)


def _grad_chunks(dw_in_pad, dw_uq_pad, dw_uk_pad, dw_uv, dwp_mla, dwp_dil, dw_out):
    a = _dw_in_chunks(dw_in_pad)
    b = dw_uq_pad.reshape(384, 8, 128)[:, :, :96].transpose(1, 0, 2)
    c = jnp.concatenate([dw_uk_pad.reshape(256, 8, 128)[:, :, :64], dw_uv.reshape(256, 8, 64)], axis=2)
    c = c.transpose(1, 0, 2)
    d = dwp_mla.reshape(512, N_DEV, 128).transpose(1, 0, 2)
    e = dwp_dil.reshape(512, N_DEV, 128).transpose(1, 0, 2)
    f = dw_out.reshape(N_DEV, 128, 1024)
    return [t.astype(BF16) for t in (a, b, c, d, e, f)]


def _lane_consts(freqs, half, first, period):
    rel = (np.arange(LANES) % period) - first
    rot = (rel >= 0) & (rel < 2 * half)
    freq = np.where(rot, freqs[np.clip(rel, 0, 2 * half - 1) % half], 0.0).astype(np.float32)
    x1 = (rot & (rel < half)).astype(np.float32)
    x2 = (rot & (rel >= half)).astype(np.float32)
    return freq[None, :], x1[None, :], x2[None, :]


def _rope_tables(pos):
    p = pos.astype(F32)[:, None]
    inv_m = np.float32(ROPE_THETA) ** (-(np.arange(0, 32, 2, dtype=np.float32) / np.float32(32)))
    inv_d = np.float32(ROPE_THETA) ** (-(np.arange(0, 16, 2, dtype=np.float32) / np.float32(16)))
    lane = np.arange(LANES)
    tabs = []
    for freqs, half, first, period, keep in ((inv_m, 16, 64, 128, lane < 96), (inv_m, 16, 0, 128, lane < 32),
                                              (inv_d, 8, 0, 64, lane >= 0)):
        freq, x1, x2 = _lane_consts(freqs, half, first, period)
        ang = p * freq
        sin = jnp.sin(ang)
        tabs.append((jnp.cos(ang) * keep.astype(np.float32)[None, :], sin * x2, sin * (-x1)))
    return tuple(tabs)


def _inproj(x, gpre, w_pad, d_tab, shards):
    s = x.shape[0]
    tm, tn = min(1024, s), 768
    ni, nj = s // tm, IN_PAD // tn
    rope_lo, rope_hi = Q_OFF // tn, V_OFF // tn
    n = len(shards)
    forward_step = min(nj, ni * nj - 2)

    def body(*refs):
        x_ref, g_ref, w_ref, c_ref, sp_ref, sm_ref = refs[:6]
        ins, o_ref, ht_ref, outs = refs[6:6 + n], refs[6 + n], refs[7 + n], refs[8 + n:8 + 2 * n]
        h_ref, sems = refs[8 + 2 * n], refs[9 + 2 * n]
        i, j = pl.program_id(0), pl.program_id(1)
        step = i * nj + j

        @pl.when(j == 0)
        def _():
            hv, _, _ = _rms(x_ref[...], g_ref[...])
            h_ref[...] = hv.astype(BF16)
            ht_ref[...] = hv.T.astype(BF16)

        if n:
            @pl.when(step == 0)
            def _():
                _gather_start(ins, outs, sems)

            @pl.when(step == forward_step)
            def _():
                _gather_forward(ins, outs, sems)

        acc = _dot(h_ref[...], w_ref[...])
        is_rope = jnp.logical_and(j >= rope_lo, j < rope_hi)

        @pl.when(is_rope)
        def _():
            o_ref[...] = _rope(acc, _tile_lanes(c_ref[...], tn), _tile_lanes(sp_ref[...], tn),
                               _tile_lanes(sm_ref[...], tn), 8).astype(BF16)

        @pl.when(jnp.logical_not(is_rope))
        def _():
            o_ref[...] = acc.astype(BF16)

        if n:
            @pl.when(step == ni * nj - 1)
            def _():
                _gather_finish(ins, outs, sems)

    row = lambda i, j: (i, 0)
    hbm = pl.BlockSpec(memory_space=pl.ANY)
    res = pl.pallas_call(
        body, name="inproj", grid=(ni, nj),
        in_specs=[_full((tm, D_MODEL), row), _full((1, D_MODEL), lambda i, j: (0, 0)),
                  _full((D_MODEL, tn), lambda i, j: (0, j)),
                  _full((tm, LANES), row), _full((tm, LANES), row), _full((tm, LANES), row)] + [hbm] * n,
        out_specs=[_full((tm, tn), lambda i, j: (i, j)), _full((D_MODEL, tm), lambda i, j: (0, i))] + [hbm] * n,
        out_shape=[jax.ShapeDtypeStruct((s, IN_PAD), BF16), jax.ShapeDtypeStruct((D_MODEL, s), BF16)]
        + [jax.ShapeDtypeStruct((N_DEV,) + a.shape, a.dtype) for a in shards],
        scratch_shapes=[pltpu.VMEM((tm, D_MODEL), BF16), pltpu.SemaphoreType.DMA((max(2 * n, 2), N_DEV))],
        compiler_params=_cparams(("arbitrary", "arbitrary")),
    )(x, gpre, w_pad, *d_tab, *shards)
    return res[0], res[1], res[2:]


def _mla_prep(proj, gq, gkv, w_uq, w_uk, w_uv, q_tab, k_tab):
    s = proj.shape[0]
    tm = min(512, s)

    def body(cq_ref, kr_ref, ckv_ref, gq_ref, gkv_ref, wq_ref, wk_ref, wv_ref,
             qc, qsp, qsm, kc, ksp, ksm, q_out, k_out, v_out, qt_out, kt_out, vt_out):
        cqn, _, _ = _rms(cq_ref[...].astype(F32), gq_ref[...])
        q = _dot(cqn.astype(BF16), wq_ref[...])
        q = _rope(q, _tile_lanes(qc[...], 1024), _tile_lanes(qsp[...], 1024), _tile_lanes(qsm[...], 1024), 16)
        q = q * MLA_QSCALE
        q_out[...] = q.astype(BF16)
        qt_out[...] = q.T.astype(BF16)
        ckvn, _, _ = _rms(ckv_ref[...].astype(F32), gkv_ref[...])
        ckvn = ckvn.astype(BF16)
        kr = _rope(kr_ref[...].astype(F32), kc[...], ksp[...], ksm[...], 16)
        k = _dot(ckvn, wk_ref[...]) + _tile_lanes(pltpu.roll(kr, 64, 1), 1024)
        k_out[...] = k.astype(BF16)
        kt_out[...] = k.T.astype(BF16)
        v = _dot(ckvn, wv_ref[...])
        v_out[...] = v.astype(BF16)
        vt_out[...] = v.T.astype(BF16)

    row = lambda i: (i, 0)
    col = lambda i: (0, i)
    cst = lambda i: (0, 0)
    tabs = [_full((tm, LANES), row)] * 6
    return pl.pallas_call(
        body, name="mla_prep", grid=(s // tm,),
        in_specs=[_full((tm, 384), lambda i: (i, CQ_OFF // 384)), _full((tm, 128), lambda i: (i, KR_OFF // 128)),
                  _full((tm, 256), lambda i: (i, CKV_OFF // 256)), _full((1, 384), cst), _full((1, 256), cst),
                  _full((384, 1024), cst), _full((256, 1024), cst), _full((256, 512), cst)] + tabs,
        out_specs=[_full((tm, 1024), row), _full((tm, 1024), row), _full((tm, 512), row),
                   _full((1024, tm), col), _full((1024, tm), col), _full((512, tm), col)],
        out_shape=[jax.ShapeDtypeStruct((s, 1024), BF16), jax.ShapeDtypeStruct((s, 1024), BF16),
                   jax.ShapeDtypeStruct((s, 512), BF16), jax.ShapeDtypeStruct((1024, s), BF16),
                   jax.ShapeDtypeStruct((1024, s), BF16), jax.ShapeDtypeStruct((512, s), BF16)],
        compiler_params=_cparams(("parallel",)),
    )(proj, proj, proj, gq, gkv, w_uq, w_uk, w_uv, *q_tab, *k_tab)


def _mla_fwd(q, k, vt):
    s = q.shape[0]
    tk = min(MLA_FWD_TK, s)
    ratio = MLA_FWD_Q_PER_K if s >= MLA_FWD_Q_PER_K * tk else 1
    tq = ratio * tk
    nq = s // tq

    def body(q_ref, k_ref, vt_ref, o_ref, ot_ref, lse_ref):
        krow = lax.broadcasted_iota(jnp.int32, (tk, tq), 0)
        qcol = lax.broadcasted_iota(jnp.int32, (tk, tq), 1)

        def q_step(i, _):
            r0 = pl.multiple_of(i * tq, tq)
            qs = [q_ref[pl.ds(r0, tq), hh * 128:(hh + 1) * 128] for hh in range(2)]

            def scores(j):
                c0 = pl.multiple_of(j * tk, tk)
                return tuple(_dot_nt(k_ref[pl.ds(c0, tk), hh * 128:(hh + 1) * 128], qs[hh])
                             for hh in range(2))

            def update(j, sts, stats, masked):
                c0 = pl.multiple_of(j * tk, tk)
                new = []
                causal = (krow + (c0 - r0)) <= qcol
                for hh in range(2):
                    m, l, acc = stats[hh]
                    st = jnp.where(causal, sts[hh], NEG) if masked else sts[hh]
                    m_new = jnp.maximum(m, jnp.max(st, axis=0, keepdims=True))
                    alpha = jnp.exp2(m - m_new)
                    p = jnp.exp2(st - m_new)
                    l = alpha * l + jnp.sum(p, axis=0, keepdims=True)
                    acc = acc * alpha + _dot(vt_ref[hh * 64:(hh + 1) * 64, pl.ds(c0, tk)], p.astype(BF16))
                    new.append((m_new, l, acc))
                return tuple(new)

            init = tuple((jnp.full((1, tq), NEG, F32), jnp.zeros((1, tq), F32), jnp.zeros((64, tq), F32))
                         for _ in range(2))
            stats = lax.fori_loop(0, ratio * i, lambda j, st: update(j, scores(j), st, False), init)
            for d in range(ratio):
                stats = update(ratio * i + d, scores(ratio * i + d), stats, True)
            (ma, la, acca), (mb, lb, accb) = stats
            ot = jnp.concatenate([acca / la, accb / lb], axis=0)
            ot_ref[:, pl.ds(r0, tq)] = ot.astype(BF16)
            o_ref[pl.ds(r0, tq), :] = ot.T.astype(BF16)
            lse_ref[:, pl.ds(r0, tq)] = jnp.concatenate(
                [ma + jnp.log2(la), mb + jnp.log2(lb), jnp.zeros((6, tq), F32)], axis=0)
            return 0

        lax.fori_loop(0, nq, q_step, 0)

    return pl.pallas_call(
        body, name="mla_fwd", grid=(4,),
        in_specs=[_full((s, 256), lambda p: (0, p)), _full((s, 256), lambda p: (0, p)),
                  _full((128, s), lambda p: (p, 0))],
        out_specs=[_full((s, 128), lambda p: (0, p)), _full((128, s), lambda p: (p, 0)),
                   _full((8, s), lambda p: (p, 0))],
        out_shape=[jax.ShapeDtypeStruct((s, 512), BF16), jax.ShapeDtypeStruct((512, s), BF16),
                   jax.ShapeDtypeStruct((32, s), F32)],
        compiler_params=_cparams(("parallel",)),
    )(q, k, vt)


def _band_mask(has_prev):
    r = lax.broadcasted_iota(jnp.int32, (Q_BLOCK, 2 * Q_BLOCK), 0)
    c = lax.broadcasted_iota(jnp.int32, (Q_BLOCK, 2 * Q_BLOCK), 1)
    lo = jnp.where(has_prev, r, Q_BLOCK)
    return jnp.logical_and(c >= lo, c <= r + Q_BLOCK)


def _dil_rows(b, d, per_seq):
    r, n = b // per_seq, b % per_seq
    start = r + (d * Q_BLOCK) * n
    prev = start - jnp.where(n > 0, d * Q_BLOCK, 0)
    if d == 1:
        return pl.ds(pl.multiple_of(start, Q_BLOCK), Q_BLOCK), pl.ds(pl.multiple_of(prev, Q_BLOCK), Q_BLOCK)
    return pl.ds(start, Q_BLOCK, stride=d), pl.ds(prev, Q_BLOCK, stride=d)


def _dil_fwd(proj, g, d, name):
    s = proj.shape[0]
    nblk = s // Q_BLOCK
    per_seq = nblk // d

    def body(q_ref, k_ref, v_ref, o_ref, lse_ref, qf, kf, vf, of, lf):
        left = _left_mask()
        hms = (left, jnp.logical_not(left))
        qf[...] = q_ref[...].astype(F32)
        kf[...] = k_ref[...].astype(F32)
        vf[...] = v_ref[...].astype(F32)

        def scores(b):
            rows, prow = _dil_rows(b, d, per_seq)
            qb = qf[rows, :].astype(BF16)
            kk = jnp.concatenate([kf[prow, :], kf[rows, :]], axis=0).astype(BF16)
            return [_dot_nt(jnp.where(hms[hh], qb, jnp.zeros_like(qb)), kk) * DIL_SCALE for hh in range(2)]

        def finish(b, tiles):
            rows, prow = _dil_rows(b, d, per_seq)
            mask = _band_mask((b % per_seq) > 0)
            vv = jnp.concatenate([vf[prow, :], vf[rows, :]], axis=0).astype(BF16)
            outs = []
            for hh in range(2):
                sc = jnp.where(mask, tiles[hh], NEG)
                m = jnp.max(sc, axis=1, keepdims=True)
                p = jnp.exp(sc - m)
                l = jnp.sum(p, axis=1, keepdims=True)
                acc = _dot(p.astype(BF16), jnp.where(hms[hh], vv, jnp.zeros_like(vv)))
                outs.append((acc / l, jnp.broadcast_to(m + jnp.log(l), (Q_BLOCK, LANES))))
            of[rows, :] = outs[0][0] + outs[1][0]
            lf[rows, :] = jnp.where(left, outs[0][1], outs[1][1])

        def step(t, _):
            tiles = [scores(DIL_BLOCKS_PER_STEP * t + u) for u in range(DIL_BLOCKS_PER_STEP)]
            for u in range(DIL_BLOCKS_PER_STEP):
                finish(DIL_BLOCKS_PER_STEP * t + u, tiles[u])
            return 0

        lax.fori_loop(0, nblk // DIL_BLOCKS_PER_STEP, step, 0)
        o_ref[...] = of[...].astype(BF16)
        lse_ref[...] = lf[...]

    blk = lambda off: _full((s, 128), lambda p, off=off: (0, off + p))
    qo, ko, vo = ((off + 512 * g) // 128 for off in (Q_OFF, K_OFF, V_OFF))
    return pl.pallas_call(
        body, name=name, grid=(4,),
        in_specs=[blk(qo), blk(ko), blk(vo)],
        out_specs=[blk(0), blk(0)],
        out_shape=[jax.ShapeDtypeStruct((s, 512), BF16), jax.ShapeDtypeStruct((s, 512), F32)],
        scratch_shapes=[pltpu.VMEM((s, 128), F32)] * 5,
        compiler_params=_cparams(("parallel",)),
    )(proj, proj, proj)


def _merge_fwd(proj, o_mla, od, lsed, wp_mla, wp_dil):
    s = proj.shape[0]
    tm = min(512, s)

    def body(zm_ref, zd_ref, gm_ref, gd_ref, om_ref, o0, o1, o2, l0, l1, l2, wm_ref, wd_ref,
             mg_out, ya_out, yd_out, odil_out, lse_out):
        la, lb, lc = l0[...], l1[...], l2[...]
        lmax = jnp.maximum(jnp.maximum(la, lb), lc)
        ea, eb, ec = jnp.exp(la - lmax), jnp.exp(lb - lmax), jnp.exp(lc - lmax)
        den = ea + eb + ec
        o_dil = (ea * o0[...].astype(F32) + eb * o1[...].astype(F32) + ec * o2[...].astype(F32)) / den
        o_dil = o_dil.astype(BF16)
        odil_out[...] = o_dil
        lse_out[...] = lmax + jnp.log(den)
        zm, zd = zm_ref[...].astype(F32), zd_ref[...].astype(F32)
        pa = (om_ref[...].astype(F32) * (zm * _sigmoid(zm))).astype(BF16)
        pd = (o_dil.astype(F32) * (zd * _sigmoid(zd))).astype(BF16)
        ya = _dot(pa, wm_ref[...])
        yd = _dot(pd, wd_ref[...])
        ya_out[...] = ya.astype(BF16)
        yd_out[...] = yd.astype(BF16)
        mg_out[...] = (_sigmoid(gm_ref[...].astype(F32)) * ya + _sigmoid(gd_ref[...].astype(F32)) * yd).astype(BF16)

    row = lambda i: (i, 0)
    cst = lambda i: (0, 0)
    r512 = _full((tm, 512), row)
    r1024 = _full((tm, 1024), row)
    return pl.pallas_call(
        body, name="merge_fwd", grid=(s // tm,),
        in_specs=[_full((tm, 512), lambda i: (i, Z_MLA // 512)), _full((tm, 512), lambda i: (i, Z_DIL // 512)),
                  _full((tm, 1024), lambda i: (i, G_MLA // 1024)), _full((tm, 1024), lambda i: (i, G_DIL // 1024)),
                  r512, r512, r512, r512, r512, r512, r512, _full((512, 1024), cst), _full((512, 1024), cst)],
        out_specs=[r1024, r1024, r1024, r512, r512],
        out_shape=[jax.ShapeDtypeStruct((s, 1024), BF16), jax.ShapeDtypeStruct((s, 1024), BF16),
                   jax.ShapeDtypeStruct((s, 1024), BF16), jax.ShapeDtypeStruct((s, 512), BF16),
                   jax.ShapeDtypeStruct((s, 512), F32)],
        compiler_params=_cparams(("parallel",)),
    )(proj, proj, proj, proj, o_mla, *od, *lsed, wp_mla, wp_dil)


def _out_loss(merged, w_out, x, target, gpost):
    s = x.shape[0]
    tm = min(512, s)

    def body(mg_ref, w_ref, x_ref, t_ref, g_ref, do_out, dy_out, loss_out, dg_out):
        i = pl.program_id(0)

        @pl.when(i == 0)
        def _():
            loss_out[...] = jnp.zeros_like(loss_out)
            dg_out[...] = jnp.zeros_like(dg_out)

        o = _dot(mg_ref[...], w_ref[...])
        g = g_ref[...]
        n, u, r = _rms(o, g)
        e = (x_ref[...] + n) - t_ref[...]
        loss_out[...] += 0.5 * jnp.sum(jnp.mean(e * e, axis=-1, keepdims=True))
        dy = e * (1.0 / D_MODEL)
        dy_out[...] = dy
        dg_out[...] += jnp.sum(dy * u, axis=0, keepdims=True)
        do_out[...] = _rms_bwd(dy, g, u, r).astype(BF16)

    row = lambda i: (i, 0)
    cst = lambda i: (0, 0)
    return pl.pallas_call(
        body, name="out_loss", grid=(s // tm,),
        in_specs=[_full((tm, 1024), row), _full((1024, 1024), cst), _full((tm, 1024), row), _full((tm, 1024), row),
                  _full((1, 1024), cst)],
        out_specs=[_full((tm, 1024), row), _full((tm, 1024), row), _full((8, LANES), cst), _full((1, 1024), cst)],
        out_shape=[jax.ShapeDtypeStruct((s, 1024), BF16), jax.ShapeDtypeStruct((s, 1024), F32),
                   jax.ShapeDtypeStruct((8, LANES), F32), jax.ShapeDtypeStruct((1, 1024), F32)],
        compiler_params=_cparams(("arbitrary",)),
    )(merged, w_out, x, target, gpost)


def _seg_sum64(x, ones_bd):
    hi = x.astype(BF16)
    lo = (x - hi.astype(F32)).astype(BF16)
    return _dot(hi, ones_bd) + _dot(lo, ones_bd)


def _merge_bwd(do, w_out, merged, proj, ya, yd, o_mla, o_dil, wp_mla, wp_dil):
    s = do.shape[0]
    tm = min(256, s)
    seg = jnp.arange(512) // 64
    ones_bd = (seg[:, None] == seg[None, :]).astype(BF16)

    def body(do_ref, wo_ref, mg_ref, zm_ref, zd_ref, gm_ref, gd_ref, ya_ref, yd_ref, om_ref, od_ref, wm_ref, wd_ref,
             bd_ref, dzm_out, dzd_out, dgm_out, dgd_out, dom_out, dod_out, domt_out, dd_out, dwo_out, dwm_out,
             dwd_out):
        i = pl.program_id(0)

        @pl.when(i == 0)
        def _():
            dwo_out[...] = jnp.zeros_like(dwo_out)
            dwm_out[...] = jnp.zeros_like(dwm_out)
            dwd_out[...] = jnp.zeros_like(dwd_out)

        dov = do_ref[...]
        dwo_out[...] += _dot_tn(mg_ref[...], dov)
        dm = _dot_nt(dov, wo_ref[...])
        for g_ref, y_ref, z_ref, o_ref, w_ref, dz_out, dg_out, dob_out, dd_o, dw_out in (
                (gm_ref, ya_ref, zm_ref, om_ref, wm_ref, dzm_out, dgm_out, dom_out, None, dwm_out),
                (gd_ref, yd_ref, zd_ref, od_ref, wd_ref, dzd_out, dgd_out, dod_out, dd_out, dwd_out)):
            sg = _sigmoid(g_ref[...].astype(F32))
            dg_out[...] = (dm * y_ref[...].astype(F32) * sg * (1.0 - sg)).astype(BF16)
            dy = (dm * sg).astype(BF16)
            z = z_ref[...].astype(F32)
            sz = _sigmoid(z)
            silu = z * sz
            ob = o_ref[...].astype(F32)
            dw_out[...] += _dot_tn((ob * silu).astype(BF16), dy)
            dp = _dot_nt(dy, w_ref[...])
            dz_out[...] = (dp * ob * (sz * (1.0 + z * (1.0 - sz)))).astype(BF16)
            dob = dp * silu
            dob_out[...] = dob.astype(BF16)
            if dd_o is None:
                domt_out[...] = dob.T.astype(BF16)
            else:
                dd_o[...] = _seg_sum64(dob * ob, bd_ref[...])

    row = lambda i: (i, 0)
    cst = lambda i: (0, 0)
    r512 = _full((tm, 512), row)
    r1024 = _full((tm, 1024), row)
    return pl.pallas_call(
        body, name="merge_bwd", grid=(s // tm,),
        in_specs=[r1024, _full((1024, 1024), cst), r1024,
                  _full((tm, 512), lambda i: (i, Z_MLA // 512)), _full((tm, 512), lambda i: (i, Z_DIL // 512)),
                  _full((tm, 1024), lambda i: (i, G_MLA // 1024)), _full((tm, 1024), lambda i: (i, G_DIL // 1024)),
                  r1024, r1024, r512, r512, _full((512, 1024), cst), _full((512, 1024), cst), _full((512, 512), cst)],
        out_specs=[r512, r512, r1024, r1024, r512, r512, _full((512, tm), lambda i: (0, i)), r512,
                   _full((1024, 1024), cst), _full((512, 1024), cst), _full((512, 1024), cst)],
        out_shape=[jax.ShapeDtypeStruct((s, 512), BF16), jax.ShapeDtypeStruct((s, 512), BF16),
                   jax.ShapeDtypeStruct((s, 1024), BF16), jax.ShapeDtypeStruct((s, 1024), BF16),
                   jax.ShapeDtypeStruct((s, 512), BF16), jax.ShapeDtypeStruct((s, 512), BF16),
                   jax.ShapeDtypeStruct((512, s), BF16), jax.ShapeDtypeStruct((s, 512), F32),
                   jax.ShapeDtypeStruct((1024, 1024), F32), jax.ShapeDtypeStruct((512, 1024), F32),
                   jax.ShapeDtypeStruct((512, 1024), F32)],
        compiler_params=_cparams(("arbitrary",)),
    )(do, w_out, merged, proj, proj, proj, proj, ya, yd, o_mla, o_dil, wp_mla, wp_dil, ones_bd)


def _mla_bwd(q, qt, k, kt, v, do, dot, ot, lse):
    s = q.shape[0]
    tk = min(256, s)
    ratio = MLA_BWD_Q_PER_K if s >= MLA_BWD_Q_PER_K * tk else 1
    tq = ratio * tk
    nq, nk = s // tq, s // tk

    def body(q_ref, qt_ref, k_ref, kt_ref, v_ref, do_ref, dot_ref, ot_ref, lse_ref, dqt_out, dkt_out, dvt_out,
             dqt_acc):
        left = _left_mask()
        krow = lax.broadcasted_iota(jnp.int32, (tk, tq), 0)
        qcol = lax.broadcasted_iota(jnp.int32, (tk, tq), 1)
        dqt_acc[...] = jnp.zeros_like(dqt_acc)

        def kv_step(j, _):
            c0 = pl.multiple_of(j * tk, tk)
            vv = v_ref[pl.ds(c0, tk), :]
            khs = [k_ref[pl.ds(c0, tk), hh * 128:(hh + 1) * 128] for hh in range(2)]
            kths = [kt_ref[hh * 128:(hh + 1) * 128, pl.ds(c0, tk)] for hh in range(2)]
            vms = [jnp.where(left if hh == 0 else jnp.logical_not(left), vv, jnp.zeros_like(vv)) for hh in range(2)]

            def scores(i):
                r0 = pl.multiple_of(jnp.minimum(i, nq - 1) * tq, tq)
                dov = do_ref[pl.ds(r0, tq), :]
                return tuple((_dot_nt(khs[hh], q_ref[pl.ds(r0, tq), hh * 128:(hh + 1) * 128]),
                              _dot_nt(vms[hh], dov)) for hh in range(2))

            def update(i, tiles, acc, masked):
                r0 = pl.multiple_of(i * tq, tq)
                new = []
                for hh in range(2):
                    dkt, dvt = acc[hh]
                    st, dp = tiles[hh]
                    hrows = slice(hh * 128, (hh + 1) * 128)
                    drows = slice(hh * 64, (hh + 1) * 64)
                    doth = dot_ref[drows, pl.ds(r0, tq)]
                    dd = jnp.sum(doth.astype(F32) * ot_ref[drows, pl.ds(r0, tq)].astype(F32), axis=0, keepdims=True)
                    p = jnp.exp2(st - lse_ref[hh:hh + 1, pl.ds(r0, tq)])
                    if masked:
                        p = jnp.where((krow + (c0 - r0)) <= qcol, p, 0.0)
                    ds = (p * (dp - dd)).astype(BF16)
                    dvt = dvt + _dot_nt(doth, p.astype(BF16))
                    dkt = dkt + _dot_nt(qt_ref[hrows, pl.ds(r0, tq)], ds)
                    dqt_acc[hrows, pl.ds(r0, tq)] += _dot(kths[hh], ds)
                    new.append((dkt, dvt))
                return tuple(new)

            init = tuple((jnp.zeros((128, tk), F32), jnp.zeros((64, tk), F32)) for _ in range(2))
            i0 = j // ratio
            acc = update(i0, scores(i0), init, True)
            acc = lax.fori_loop(i0 + 1, nq, lambda i, a: update(i, scores(i), a, False), acc)
            for hh in range(2):
                dkt_out[hh * 128:(hh + 1) * 128, pl.ds(c0, tk)] = (acc[hh][0] * (1.0 / LOG2E)).astype(BF16)
                dvt_out[hh * 64:(hh + 1) * 64, pl.ds(c0, tk)] = acc[hh][1].astype(BF16)
            return 0

        lax.fori_loop(0, nk, kv_step, 0)
        dqt_out[...] = (dqt_acc[...] * MLA_SCALE).astype(BF16)

    b256 = _full((s, 256), lambda p: (0, p))
    b128 = _full((s, 128), lambda p: (0, p))
    t256 = _full((256, s), lambda p: (p, 0))
    t128 = _full((128, s), lambda p: (p, 0))
    return pl.pallas_call(
        body, name="mla_bwd", grid=(4,),
        in_specs=[b256, t256, b256, t256, b128, b128, t128, t128, _full((8, s), lambda p: (p, 0))],
        out_specs=[t256, t256, t128],
        out_shape=[jax.ShapeDtypeStruct((1024, s), BF16), jax.ShapeDtypeStruct((1024, s), BF16),
                   jax.ShapeDtypeStruct((512, s), BF16)],
        scratch_shapes=[pltpu.VMEM((256, s), F32)],
        compiler_params=_cparams(("parallel",)),
    )(q, qt, k, kt, v, do, dot, ot, lse)


def _mla_prep_bwd(dq, dk, dv, proj, gq, gkv, w_uq, w_uk, w_uv, q_tab, k_tab):
    s = proj.shape[0]
    tm = min(256, s)

    def body(dqt_ref, dkt_ref, dvt_ref, cq_ref, ckv_ref, gq_ref, gkv_ref, wq_ref, wk_ref, wv_ref,
             qc, qsp, qsm, kc, ksp, ksm,
             dcq_out, dkr_out, dckv_out, dwq_out, dwk_out, dwv_out, dgq_out, dgkv_out):
        i = pl.program_id(0)

        @pl.when(i == 0)
        def _():
            for r in (dwq_out, dwk_out, dwv_out, dgq_out, dgkv_out):
                r[...] = jnp.zeros_like(r)

        dqu = _unrope(dqt_ref[...].astype(F32).T, _tile_lanes(qc[...], 1024), _tile_lanes(qsp[...], 1024),
                      _tile_lanes(qsm[...], 1024), 16).astype(BF16)
        gq = gq_ref[...]
        cqn, xh, r = _rms(cq_ref[...].astype(F32), gq)
        dwq_out[...] += _dot_tn(cqn.astype(BF16), dqu)
        dcqn = _dot_nt(dqu, wq_ref[...])
        dgq_out[...] += jnp.sum(dcqn * xh, axis=0, keepdims=True)
        dcq_out[...] = _rms_bwd(dcqn, gq, xh, r).astype(BF16)

        dkf = dkt_ref[...].astype(F32).T
        dkb = dkf.astype(BF16)
        dsum = dkf[:, 0:128]
        for h in range(1, MLA_HEADS):
            dsum = dsum + dkf[:, h * 128:(h + 1) * 128]
        dkr_out[...] = _unrope(pltpu.roll(dsum, 64, 1), kc[...], ksp[...], ksm[...], 16).astype(BF16)

        dvb = dvt_ref[...].astype(F32).T.astype(BF16)
        gkv = gkv_ref[...]
        ckvn, xh2, r2 = _rms(ckv_ref[...].astype(F32), gkv)
        ckvn = ckvn.astype(BF16)
        dwk_out[...] += _dot_tn(ckvn, dkb)
        dwv_out[...] += _dot_tn(ckvn, dvb)
        dckvn = _dot_nt(dkb, wk_ref[...]) + _dot_nt(dvb, wv_ref[...])
        dgkv_out[...] += jnp.sum(dckvn * xh2, axis=0, keepdims=True)
        dckv_out[...] = _rms_bwd(dckvn, gkv, xh2, r2).astype(BF16)

    row = lambda i: (i, 0)
    cst = lambda i: (0, 0)
    tabs = [_full((tm, LANES), row)] * 6
    return pl.pallas_call(
        body, name="mla_prep_bwd", grid=(s // tm,),
        in_specs=[_full((1024, tm), lambda i: (0, i)), _full((1024, tm), lambda i: (0, i)),
                  _full((512, tm), lambda i: (0, i)),
                  _full((tm, 384), lambda i: (i, CQ_OFF // 384)), _full((tm, 256), lambda i: (i, CKV_OFF // 256)),
                  _full((1, 384), cst), _full((1, 256), cst),
                  _full((384, 1024), cst), _full((256, 1024), cst), _full((256, 512), cst)] + tabs,
        out_specs=[_full((tm, 384), row), _full((tm, 128), row), _full((tm, 256), row),
                   _full((384, 1024), cst), _full((256, 1024), cst), _full((256, 512), cst),
                   _full((1, 384), cst), _full((1, 256), cst)],
        out_shape=[jax.ShapeDtypeStruct((s, 384), BF16), jax.ShapeDtypeStruct((s, 128), BF16),
                   jax.ShapeDtypeStruct((s, 256), BF16),
                   jax.ShapeDtypeStruct((384, 1024), F32), jax.ShapeDtypeStruct((256, 1024), F32),
                   jax.ShapeDtypeStruct((256, 512), F32),
                   jax.ShapeDtypeStruct((1, 384), F32), jax.ShapeDtypeStruct((1, 256), F32)],
        compiler_params=_cparams(("arbitrary",)),
    )(dq, dk, dv, proj, proj, gq, gkv, w_uq, w_uk, w_uv, *q_tab, *k_tab)


def _dil_bwd(proj, g, do, lse, dd, tabs, d, name):
    s = proj.shape[0]
    nblk = s // Q_BLOCK
    per_seq = nblk // d

    def body(q_ref, k_ref, v_ref, do_ref, lse_ref, dd_ref, c_ref, sp_ref, sm_ref, dq_out, dk_out, dv_out,
             qf, kf, vf, dof, dq_acc, dk_acc, dv_acc):
        left = _left_mask()
        hms = (left, jnp.logical_not(left))
        qf[...] = q_ref[...].astype(F32)
        kf[...] = k_ref[...].astype(F32)
        vf[...] = v_ref[...].astype(F32)
        dof[...] = do_ref[...].astype(F32)
        dk_acc[...] = jnp.zeros_like(dk_acc)
        dv_acc[...] = jnp.zeros_like(dv_acc)

        def scores(b):
            rows, prow = _dil_rows(b, d, per_seq)
            qb, dob = qf[rows, :].astype(BF16), dof[rows, :].astype(BF16)
            kk = jnp.concatenate([kf[prow, :], kf[rows, :]], axis=0).astype(BF16)
            vv = jnp.concatenate([vf[prow, :], vf[rows, :]], axis=0).astype(BF16)
            zero = jnp.zeros_like(qb)
            out = []
            for hh in range(2):
                qm, dom = jnp.where(hms[hh], qb, zero), jnp.where(hms[hh], dob, zero)
                out.append((_dot_nt(qm, kk) * DIL_SCALE, _dot_nt(dom, vv)))
            return out

        def finish(b, tiles):
            rows, prow = _dil_rows(b, d, per_seq)
            mask = _band_mask((b % per_seq) > 0)
            qb, dob = qf[rows, :].astype(BF16), dof[rows, :].astype(BF16)
            kk = jnp.concatenate([kf[prow, :], kf[rows, :]], axis=0).astype(BF16)
            lse_b, dd_b = lse_ref[rows, :], dd_ref[rows, :]
            zero = jnp.zeros_like(qb)
            dq = jnp.zeros((Q_BLOCK, LANES), F32)
            dk = jnp.zeros((2 * Q_BLOCK, LANES), F32)
            dv = jnp.zeros((2 * Q_BLOCK, LANES), F32)
            for hh in range(2):
                hm = hms[hh]
                qm, dom = jnp.where(hm, qb, zero), jnp.where(hm, dob, zero)
                lse_h = _tile_lanes(_expand_half(lse_b, hh, left), 2 * Q_BLOCK)
                dd_h = _tile_lanes(_expand_half(dd_b, hh, left), 2 * Q_BLOCK)
                sc, dp = tiles[hh]
                p = jnp.where(mask, jnp.exp(sc - lse_h), 0.0)
                ds = (p * (dp - dd_h) * DIL_SCALE).astype(BF16)
                dq = dq + _dot(ds, jnp.where(hm, kk, jnp.zeros_like(kk)))
                dk = dk + _dot_tn(ds, qm)
                dv = dv + _dot_tn(p.astype(BF16), dom)
            dq_acc[rows, :] = dq
            dk_acc[prow, :] += dk[0:Q_BLOCK]
            dv_acc[prow, :] += dv[0:Q_BLOCK]
            dk_acc[rows, :] += dk[Q_BLOCK:]
            dv_acc[rows, :] += dv[Q_BLOCK:]

        def step(t, _):
            tiles = [scores(DIL_BLOCKS_PER_STEP * t + u) for u in range(DIL_BLOCKS_PER_STEP)]
            for u in range(DIL_BLOCKS_PER_STEP):
                finish(DIL_BLOCKS_PER_STEP * t + u, tiles[u])
            return 0

        lax.fori_loop(0, nblk // DIL_BLOCKS_PER_STEP, step, 0)
        dq_out[...] = _unrope(dq_acc[...], c_ref[...], sp_ref[...], sm_ref[...], 8).astype(BF16)
        dk_out[...] = _unrope(dk_acc[...], c_ref[...], sp_ref[...], sm_ref[...], 8).astype(BF16)
        dv_out[...] = dv_acc[...].astype(BF16)

    blk = lambda off: _full((s, 128), lambda p, off=off: (0, off + p))
    tab = _full((s, 128), lambda p: (0, 0))
    qo, ko, vo = ((off + 512 * g) // 128 for off in (Q_OFF, K_OFF, V_OFF))
    return pl.pallas_call(
        body, name=name, grid=(4,),
        in_specs=[blk(qo), blk(ko), blk(vo), blk(0), blk(0), blk(0), tab, tab, tab],
        out_specs=[blk(0), blk(0), blk(0)],
        out_shape=[jax.ShapeDtypeStruct((s, 512), BF16)] * 3,
        scratch_shapes=[pltpu.VMEM((s, 128), F32)] * 7,
        compiler_params=_cparams(("parallel",)),
    )(proj, proj, proj, do, lse, dd, *tabs)


def _chip_copies(ins, outs, sems, outgoing):
    x, y, c, chips = _place()
    myq = 2 * x + y
    n = len(ins)

    def chunk(a, j):
        q = 2 * chips[j][0] + chips[j][1]
        return _remote(ins[a].at[q], outs[a].at[myq if outgoing else q], sems, 2 * a, j, (*chips[j], c))

    if outgoing is None:
        return [pltpu.make_async_copy(ins[a].at[myq], outs[a].at[myq], sems.at[2 * a, 3]) for a in range(n)]
    return [chunk(a, j) for j in range(3) for a in range(n)]


def _dh_bwd(dproj, w_pad, x, gpre, dy, pairs):
    s = x.shape[0]
    tm, tk = min(1024, s), 1408
    ni, nk = s // tm, IN_PAD // tk
    n = len(pairs)

    def body(*refs):
        dp_ref, w_ref, x_ref, g_ref, dy_ref = refs[:5]
        ins, gx_out, dg_out, outs = refs[5:5 + n], refs[5 + n], refs[6 + n], refs[7 + n:7 + 2 * n]
        acc, sems = refs[7 + 2 * n], refs[8 + 2 * n]
        i, kk = pl.program_id(0), pl.program_id(1)

        @pl.when(jnp.logical_and(i == 0, kk == 0))
        def _():
            dg_out[...] = jnp.zeros_like(dg_out)
            for cp in _chip_copies(ins, outs, sems, None) + _chip_copies(ins, outs, sems, True):
                cp.start()

        @pl.when(kk == 0)
        def _():
            acc[...] = jnp.zeros_like(acc)

        acc[...] += _dot_nt(dp_ref[...], w_ref[...])

        @pl.when(kk == nk - 1)
        def _():
            g = g_ref[...]
            _, xh, r = _rms(x_ref[...], g)
            dh = acc[...]
            dg_out[...] += jnp.sum(dh * xh, axis=0, keepdims=True)
            gx_out[...] = dy_ref[...] + _rms_bwd(dh, g, xh, r)

        @pl.when(jnp.logical_and(i == ni - 1, kk == nk - 1))
        def _():
            for cp in _chip_copies(ins, outs, sems, False):
                cp.wait_recv()
            for cp in _chip_copies(ins, outs, sems, True):
                cp.wait_send()
            for cp in _chip_copies(ins, outs, sems, None):
                cp.wait()

    row = lambda i, k: (i, 0)
    hbm = pl.BlockSpec(memory_space=pl.ANY)
    res = pl.pallas_call(
        body, name="dh_bwd", grid=(ni, nk),
        in_specs=[_full((tm, tk), lambda i, k: (i, k)), _full((1024, tk), lambda i, k: (0, k)),
                  _full((tm, 1024), row), _full((1, 1024), lambda i, k: (0, 0)), _full((tm, 1024), row)] + [hbm] * n,
        out_specs=[_full((tm, 1024), row), _full((1, 1024), lambda i, k: (0, 0))] + [hbm] * n,
        out_shape=[jax.ShapeDtypeStruct((s, 1024), F32), jax.ShapeDtypeStruct((1, 1024), F32)]
        + [jax.ShapeDtypeStruct(a.shape, a.dtype) for a in pairs],
        scratch_shapes=[pltpu.VMEM((tm, 1024), F32), pltpu.SemaphoreType.DMA((2 * n, 4))],
        compiler_params=_cparams(("arbitrary", "arbitrary")),
    )(dproj, w_pad, x, gpre, dy, *pairs)
    return res[0], res[1], res[2:]


def _dw_in(ht, dproj):
    s = ht.shape[1]
    tn = 768

    def body(ht_ref, dp_ref, o_ref):
        o_ref[...] = _dot(ht_ref[...], dp_ref[...]).astype(BF16)

    return pl.pallas_call(
        body, name="dw_in", grid=(IN_PAD // tn,),
        in_specs=[_full((1024, s), lambda j: (0, 0)), _full((s, tn), lambda j: (0, j))],
        out_specs=_full((1024, tn), lambda j: (0, j)),
        out_shape=jax.ShapeDtypeStruct((1024, IN_PAD), BF16),
        compiler_params=_cparams(("parallel",)),
    )(ht, dproj)


def _remote(src, dst, sems, row, k, to):
    return pltpu.make_async_remote_copy(src_ref=src, dst_ref=dst, send_sem=sems.at[row, k], recv_sem=sems.at[row + 1, k],
                                        device_id=to, device_id_type=pl.DeviceIdType.MESH)


def _place():
    x, y, c = lax.axis_index("x"), lax.axis_index("y"), lax.axis_index("c")
    return x, y, c, [(1 - x, y), (x, 1 - y), (1 - x, 1 - y)]


def _gather_parts(ins, outs, sems):
    n = len(ins)
    x, y, c, chips = _place()
    me, sib = (x, y, c), (x, y, 1 - c)
    idx = lambda p: 4 * p[0] + 2 * p[1] + p[2]

    def copy(a, k, block, to, from_input=False):
        src = ins[a] if from_input else outs[a].at[idx(block)]
        return _remote(src, outs[a].at[idx(block)], sems, 2 * a, k, to)

    own = lambda: [pltpu.make_async_copy(ins[a], outs[a].at[idx(me)], sems.at[2 * a, 7]) for a in range(n)]
    first = lambda: ([copy(a, 0, me, sib, True) for a in range(n)]
                     + [copy(a, 1 + j, me, (*chips[j], c), True) for j in range(3) for a in range(n)])
    from_chip = lambda j: [copy(a, 1 + j, (*chips[j], c), me) for a in range(n)]
    passed = lambda j: [copy(a, 4 + j, (*chips[j], c), sib) for a in range(n)]
    from_sibling = lambda: ([copy(a, 0, sib, me) for a in range(n)]
                            + [copy(a, 4 + j, (*chips[j], 1 - c), me) for j in range(3) for a in range(n)])
    return own, first, from_chip, passed, from_sibling


def _gather_start(ins, outs, sems):
    own, first, _, _, _ = _gather_parts(ins, outs, sems)
    for cp in own() + first():
        cp.start()


def _gather_forward(ins, outs, sems):
    _, _, from_chip, passed, _ = _gather_parts(ins, outs, sems)
    for j in range(3):
        for cp in from_chip(j):
            cp.wait_recv()
        for cp in passed(j):
            cp.start()


def _gather_finish(ins, outs, sems):
    own, first, _, passed, from_sibling = _gather_parts(ins, outs, sems)
    for cp in from_sibling():
        cp.wait_recv()
    for cp in first() + [cp for j in range(3) for cp in passed(j)]:
        cp.wait_send()
    for cp in own():
        cp.wait()


def _gather_weights(arrays):
    n = len(arrays)

    def body(*refs):
        ins, outs, sems = refs[:n], refs[n:2 * n], refs[2 * n]
        _gather_start(ins, outs, sems)
        _gather_forward(ins, outs, sems)
        _gather_finish(ins, outs, sems)

    hbm = pl.BlockSpec(memory_space=pl.ANY)
    return pl.pallas_call(
        body, name="gather_weights", in_specs=[hbm] * n, out_specs=[hbm] * n,
        out_shape=[jax.ShapeDtypeStruct((N_DEV,) + a.shape, a.dtype) for a in arrays],
        scratch_shapes=[pltpu.SemaphoreType.DMA((2 * n, N_DEV))],
    )(*arrays)


def _pair_exchange(chunks):
    n = len(chunks)

    def body(*refs):
        ins, outs, sems = refs[:n], refs[n:2 * n], refs[2 * n]
        x, y, c, _ = _place()
        sent = [_remote(ins[a].at[2 * q + (1 - c)], outs[a].at[q], sems, 2 * a, q, (x, y, 1 - c))
                for a in range(n) for q in range(4)]
        for cp in sent:
            cp.start()
        for cp in sent:
            cp.wait_recv()
        for cp in sent:
            cp.wait_send()

    hbm = pl.BlockSpec(memory_space=pl.ANY)
    return pl.pallas_call(
        body, name="pair_exchange", in_specs=[hbm] * n, out_specs=[hbm] * n,
        out_shape=[jax.ShapeDtypeStruct((4,) + a.shape[1:], a.dtype) for a in chunks],
        scratch_shapes=[pltpu.SemaphoreType.DMA((2 * n, 4))],
    )(*chunks)


def _pair_sum(core, chunks, recv, name, tr):
    _, rows, cols = chunks.shape

    def body(c_ref, a_ref, b_ref, o_ref):
        o_ref[...] = (a_ref[...].astype(F32) + b_ref[...].astype(F32)).astype(BF16)

    blk = lambda f: _full((1, tr, cols), f)
    return pl.pallas_call(
        body, name=name, out_shape=jax.ShapeDtypeStruct((4, rows, cols), BF16),
        grid_spec=pltpu.PrefetchScalarGridSpec(
            num_scalar_prefetch=1, grid=(4, rows // tr),
            in_specs=[blk(lambda q, i, c: (2 * q + c[0], i, 0)), blk(lambda q, i, c: (q, i, 0))],
            out_specs=blk(lambda q, i, c: (q, i, 0))),
        compiler_params=_cparams(("parallel", "parallel")),
    )(core, chunks, recv)


def _packet_exchange(packet):
    def body(pk, pk_out, sems):
        x, y, c, _ = _place()
        me = 4 * x + 2 * y + c
        flip = lambda v, b: (1 - v) if b else v

        def small(j, outgoing):
            peer = (flip(x, (j >> 2) & 1), flip(y, (j >> 1) & 1), flip(c, j & 1))
            slot = me if outgoing else 4 * peer[0] + 2 * peer[1] + peer[2]
            return _remote(pk, pk_out.at[slot], sems, 0, j, peer)

        own = pltpu.make_async_copy(pk, pk_out.at[me], sems.at[0, 0])
        sent = [small(j, True) for j in range(1, N_DEV)]
        for cp in [own] + sent:
            cp.start()
        for j in range(1, N_DEV):
            small(j, False).wait_recv()
        for cp in sent:
            cp.wait_send()
        own.wait()

    hbm = pl.BlockSpec(memory_space=pl.ANY)
    return pl.pallas_call(
        body, name="packet_exchange", in_specs=[hbm], out_specs=hbm,
        out_shape=jax.ShapeDtypeStruct((N_DEV,) + packet.shape, packet.dtype),
        scratch_shapes=[pltpu.SemaphoreType.DMA((2, N_DEV))],
    )(packet)


def _adam_math(w, g, m, v):
    m = ADAM_B1 * m + (1.0 - ADAM_B1) * g
    v = ADAM_B2 * v + (1.0 - ADAM_B2) * (g * g)
    m_hat = m / (1.0 - ADAM_B1 ** ADAM_STEP)
    v_hat = v / (1.0 - ADAM_B2 ** ADAM_STEP)
    delta = -ADAM_LR * (m_hat / (jnp.sqrt(v_hat) + ADAM_EPS) + ADAM_WD * w)
    return delta, m, v


def _adam(recv, w, m, v, name, tr):
    _, rows, cols = w.shape

    def body(r_ref, w_ref, m_ref, v_ref, g_out, d_out, m_out, v_out):
        g = r_ref[0].astype(F32)
        for k in range(1, 4):
            g = g + r_ref[k].astype(F32)
        g_out[0] = g
        d_out[0], m_out[0], v_out[0] = _adam_math(w_ref[0], g, m_ref[0], v_ref[0])

    blk = _full((1, tr, cols), lambda i: (0, i, 0))
    return pl.pallas_call(
        body, name=name, grid=(rows // tr,),
        in_specs=[_full((4, tr, cols), lambda i: (0, i, 0)), blk, blk, blk],
        out_specs=[blk] * 4,
        out_shape=[jax.ShapeDtypeStruct(w.shape, F32)] * 4,
        compiler_params=_cparams(("parallel",)),
    )(recv, w, m, v)


def _adam_gains(recv, gains, gains_m, gains_v):
    def body(*refs):
        r_ref, w, m, v = refs[0], refs[1:5], refs[5:9], refs[9:13]
        g_out, d_out, m_out, v_out, loss_out = refs[13:17], refs[17:21], refs[21:25], refs[25:29], refs[29]
        tot = r_ref[0:1, :]
        for k in range(1, N_DEV):
            tot = tot + r_ref[k:k + 1, :]
        for t in range(4):
            g = tot[:, GAIN_OFFS[t]:GAIN_OFFS[t] + GAIN_WIDTHS[t]]
            g_out[t][...] = g
            d_out[t][...], m_out[t][...], v_out[t][...] = _adam_math(w[t][...], g, m[t][...], v[t][...])
        loss_out[...] = tot[:, LOSS_OFF:LOSS_OFF + LANES]

    shapes = [jax.ShapeDtypeStruct((1, n), F32) for n in GAIN_WIDTHS]
    return pl.pallas_call(
        body, name="adam_gains", out_shape=shapes * 4 + [jax.ShapeDtypeStruct((1, LANES), F32)],
    )(recv, *gains, *gains_m, *gains_v)


def _local_step(x, positions, gains, w_pad, small_shards, small_weights, target):
    gpre, gq, gkv, gpost = gains
    q_tab, k_tab, d_tab = _rope_tables(positions)

    proj, ht, gathered = _inproj(x, gpre, w_pad, d_tab, small_shards)
    w_uq, w_uk, w_uv, wp_mla, wp_dil, w_out = _assemble_small(*gathered) if small_shards else small_weights
    q, k, v, qt, kt, vt = _mla_prep(proj, gq, gkv, w_uq, w_uk, w_uv, q_tab, k_tab)
    o_mla, ot_mla, lse_mla = _mla_fwd(q, k, vt)

    od, lsed = [], []
    for g, d in enumerate(DIL_DILATIONS):
        o_g, lse_g = _dil_fwd(proj, g, d, "dil_fwd_%d" % g)
        od.append(o_g)
        lsed.append(lse_g)

    merged, ya, yd, o_dil, lse_dil = _merge_fwd(proj, o_mla, od, lsed, wp_mla, wp_dil)
    do, dy, loss, dgpost = _out_loss(merged, w_out, x, target, gpost)

    (dz_mla, dz_dil, dg_mla, dg_dil, do_mla, do_dil, dot_mla, dd_dil, dw_out, dwp_mla, dwp_dil) = _merge_bwd(
        do, w_out, merged, proj, ya, yd, o_mla, o_dil, wp_mla, wp_dil)

    dq, dk, dv = _mla_bwd(q, qt, k, kt, v, do_mla, dot_mla, ot_mla, lse_mla)
    dcq, dkr, dckv, dw_uq, dw_uk, dw_uv, dgq, dgkv = _mla_prep_bwd(dq, dk, dv, proj, gq, gkv, w_uq, w_uk, w_uv,
                                                                   q_tab, k_tab)

    dqs, dks, dvs = [], [], []
    for g, d in enumerate(DIL_DILATIONS):
        dq_g, dk_g, dv_g = _dil_bwd(proj, g, do_dil, lse_dil, dd_dil, d_tab, d, "dil_bwd_%d" % g)
        dqs.append(dq_g)
        dks.append(dk_g)
        dvs.append(dv_g)

    dproj = jnp.concatenate([dz_mla, dz_dil, dg_mla, dg_dil] + dqs + dks + dvs + [dcq, dkr, dckv], axis=1)
    dw_in = _dw_in(ht, dproj)
    return loss, (dproj, dy), (dgq, dgkv, dgpost), (dw_in, dw_uq, dw_uk, dw_uv, dwp_mla, dwp_dil, dw_out)


ADAM_ROWS = (256, 384, 256, 512, 512, 128)
PAIR_ROWS = (512, 384, 256, 512, 512, 128)


def kernel(x, positions, pre_norm_g, w_in, q_norm_g, w_uq, kv_norm_g, w_ukv, w_proj_mla, w_proj_dil, w_out, post_norm_g, loss_target, m_pre_norm_g, m_w_in, m_q_norm_g, m_w_uq, m_kv_norm_g, m_w_ukv, m_w_proj_mla, m_w_proj_dil, m_w_out, m_post_norm_g, v_pre_norm_g, v_w_in, v_q_norm_g, v_w_uq, v_kv_norm_g, v_w_ukv, v_w_proj_mla, v_w_proj_dil, v_w_out, v_post_norm_g):
    big_w = (w_in, w_uq, w_ukv, w_proj_mla, w_proj_dil, w_out)
    big_m = (m_w_in, m_w_uq, m_w_ukv, m_w_proj_mla, m_w_proj_dil, m_w_out)
    big_v = (v_w_in, v_w_uq, v_w_ukv, v_w_proj_mla, v_w_proj_dil, v_w_out)
    gains = (pre_norm_g, q_norm_g, kv_norm_g, post_norm_g)
    gains_m = (m_pre_norm_g, m_q_norm_g, m_kv_norm_g, m_post_norm_g)
    gains_v = (v_pre_norm_g, v_q_norm_g, v_kv_norm_g, v_post_norm_g)

    shards = [w[0].astype(BF16) for w in big_w]
    w_pad = _assemble_w_in(_gather_weights(shards[:1])[0])

    loss, (dproj, dy), (dgq, dgkv, dgpost), dweights = _local_step(x[0], positions[0], gains, w_pad, shards[1:],
                                                                   None, loss_target[0])

    chunks = _grad_chunks(*dweights)
    from_sibling = _pair_exchange(chunks)
    core = lax.axis_index("c").astype(jnp.int32).reshape(1)
    pairs = [_pair_sum(core, chunks[t], from_sibling[t], "pair_sum_%d" % t, PAIR_ROWS[t]) for t in range(6)]
    grad_x, dgpre, received = _dh_bwd(dproj, w_pad, x[0], pre_norm_g, dy, pairs)
    packet = _packet_exchange(jnp.concatenate([dgpre, dgq, dgkv, dgpost, loss[0:1]], axis=1))

    big = [_adam(received[t], big_w[t], big_m[t], big_v[t], "adam_%d" % t, ADAM_ROWS[t]) for t in range(6)]
    small = _adam_gains(packet.reshape(N_DEV, PACKET), gains, gains_m, gains_v)

    def interleave(kind):
        s_pre, s_q, s_kv, s_post = small[4 * kind:4 * kind + 4]
        b_in, b_uq, b_ukv, b_pm, b_pd, b_out = (big[t][kind] for t in range(6))
        return [s_pre, b_in, s_q, b_uq, s_kv, b_ukv, b_pm, b_pd, b_out, s_post]

    return (small[16][0, 0], grad_x[None], *interleave(0), *interleave(1), *interleave(2), *interleave(3))
```

```python
import numpy as np
import jax
import jax.numpy as jnp
from jax import lax
from jax.experimental import pallas as pl
from jax.experimental.pallas import tpu as pltpu

F32 = jnp.float32
BF16 = jnp.bfloat16

D_MODEL = 1024
NORM_EPS = 1e-6
ROPE_THETA = 500000.0
N_DEV = 8
LANES = 128
NEG = -1e30

MLA_HEADS = 8
MLA_Q_RANK = 384
MLA_KV_RANK = 256
MLA_SCALE = 96.0 ** -0.5
LOG2E = 1.4426950408889634
MLA_QSCALE = MLA_SCALE * LOG2E
MLA_FWD_TK, MLA_FWD_Q_PER_K = 256, 2
MLA_BWD_Q_PER_K = 2
DIL_DILATIONS = (1, 4, 16)
DIL_SCALE = 0.125
Q_BLOCK = 128
DIL_BLOCKS_PER_STEP = 4

Z_MLA, Z_DIL, G_MLA, G_DIL = 0, 512, 1024, 2048
Q_OFF, K_OFF, V_OFF = 3072, 4608, 6144
CQ_OFF, KR_OFF, CKV_OFF, IN_PAD = 7680, 8064, 8192, 8448
IN_WIDTH = 8352
SHARD_W = IN_WIDTH // 8
IN_SEGS = ((0, 384, CQ_OFF), (384, 256, CKV_OFF), (640, 32, KR_OFF), (672, 1536, Q_OFF), (2208, 1536, K_OFF),
           (3744, 1536, V_OFF), (5280, 512, Z_MLA), (5792, 512, Z_DIL), (6304, 1024, G_MLA), (7328, 1024, G_DIL))

GAIN_OFFS = (0, 1024, 1408, 1664)
GAIN_WIDTHS = (1024, 384, 256, 1024)
LOSS_OFF, PACKET = 2688, 2816

ADAM_LR, ADAM_B1, ADAM_B2, ADAM_EPS, ADAM_WD, ADAM_STEP = 0.001, 0.9, 0.999, 1e-08, 0.01, 10

VMEM_LIMIT_MB = 56


def _cparams(sem=None, vmem_mb=VMEM_LIMIT_MB):
    return pltpu.CompilerParams(dimension_semantics=sem, vmem_limit_bytes=vmem_mb * 1024 * 1024)


def _dot(a, b):
    return jnp.dot(a, b, preferred_element_type=F32)


def _dot_nt(a, b):
    return lax.dot_general(a, b, (((1,), (1,)), ((), ())), preferred_element_type=F32)


def _dot_tn(a, b):
    return lax.dot_general(a, b, (((0,), (0,)), ((), ())), preferred_element_type=F32)


def _tile_lanes(t, width):
    return t if width == t.shape[1] else jnp.tile(t, (1, width // t.shape[1]))


def _rope(x, c, sp, sm, a):
    n = x.shape[1]
    return x * c + pltpu.roll(x, a, 1) * sp + pltpu.roll(x, n - a, 1) * sm


def _unrope(dy, c, sp, sm, a):
    n = dy.shape[1]
    return dy * c + pltpu.roll(dy * sp, n - a, 1) + pltpu.roll(dy * sm, a, 1)


def _sigmoid(z):
    return 1.0 / (1.0 + jnp.exp(-z))


def _left_mask():
    return lax.broadcasted_iota(jnp.int32, (1, LANES), 1) < 64


def _expand_half(x, hh, left):
    r = pltpu.roll(x, 64, 1)
    return jnp.where(left, x, r) if hh == 0 else jnp.where(left, r, x)


def _rms(xv, g):
    r = lax.rsqrt(jnp.mean(xv * xv, axis=-1, keepdims=True) + NORM_EPS)
    xh = xv * r
    return xh * g, xh, r


def _rms_bwd(dout, g, xh, r):
    dxh = dout * g
    return r * (dxh - xh * jnp.mean(dxh * xh, axis=-1, keepdims=True))


def _full(shape, index_map):
    return pl.BlockSpec(shape, index_map)


def _w_in_pieces():
    out = []
    for s, n, off in sorted(IN_SEGS, key=lambda t: t[2]):
        c = s
        while c < s + n:
            k = c // SHARD_W
            e = min(s + n, (k + 1) * SHARD_W)
            out.append((k, c - k * SHARD_W, e - c, off + (c - s)))
            c = e
    return out


def _assemble_w_in(g):
    parts, cur = [], 0
    for k, a, w, off in _w_in_pieces():
        if off > cur:
            parts.append(jnp.zeros((D_MODEL, off - cur), g.dtype))
        parts.append(g[k, :, a:a + w])
        cur = off + w
    if cur < IN_PAD:
        parts.append(jnp.zeros((D_MODEL, IN_PAD - cur), g.dtype))
    return jnp.concatenate(parts, axis=1)


def _dw_in_chunks(dw):
    chunks = []
    for dev in range(N_DEV):
        mine = sorted((p for p in _w_in_pieces() if p[0] == dev), key=lambda p: p[1])
        chunks.append(jnp.concatenate([dw[:, off:off + w] for k, a, w, off in mine], axis=1))
    return jnp.stack(chunks)


def _assemble_small(g_uq, g_ukv, g_pm, g_pd, g_out):
    w_uq_pad = jnp.pad(g_uq.transpose(1, 0, 2), ((0, 0), (0, 0), (0, 32))).reshape(384, 1024)
    ukv = g_ukv.transpose(1, 0, 2)
    w_uk_pad = jnp.pad(ukv[:, :, :64], ((0, 0), (0, 0), (0, 64))).reshape(256, 1024)
    w_uv = ukv[:, :, 64:].reshape(256, 512)
    wp_mla = g_pm.transpose(1, 0, 2).reshape(512, 1024)
    wp_dil = g_pd.transpose(1, 0, 2).reshape(512, 1024)
    return w_uq_pad, w_uk_pad, w_uv, wp_mla, wp_dil, g_out.reshape(1024, 1024)


def _small_chunks(dw_uq_pad, dw_uk_pad, dw_uv, dwp_mla, dwp_dil, dw_out):
    b = dw_uq_pad.reshape(384, 8, 128)[:, :, :96].transpose(1, 0, 2)
    c = jnp.concatenate([dw_uk_pad.reshape(256, 8, 128)[:, :, :64], dw_uv.reshape(256, 8, 64)], axis=2)
    c = c.transpose(1, 0, 2)
    d = dwp_mla.reshape(512, N_DEV, 128).transpose(1, 0, 2)
    e = dwp_dil.reshape(512, N_DEV, 128).transpose(1, 0, 2)
    f = dw_out.reshape(N_DEV, 128, 1024)
    return [t.astype(BF16) for t in (b, c, d, e, f)]


def _lane_consts(freqs, half, first, period):
    rel = (np.arange(LANES) % period) - first
    rot = (rel >= 0) & (rel < 2 * half)
    freq = np.where(rot, freqs[np.clip(rel, 0, 2 * half - 1) % half], 0.0).astype(np.float32)
    x1 = (rot & (rel < half)).astype(np.float32)
    x2 = (rot & (rel >= half)).astype(np.float32)
    return freq[None, :], x1[None, :], x2[None, :]


def _rope_tables(pos):
    p = pos.astype(F32)[:, None]
    inv_m = np.float32(ROPE_THETA) ** (-(np.arange(0, 32, 2, dtype=np.float32) / np.float32(32)))
    inv_d = np.float32(ROPE_THETA) ** (-(np.arange(0, 16, 2, dtype=np.float32) / np.float32(16)))
    lane = np.arange(LANES)
    tabs = []
    for freqs, half, first, period, keep in ((inv_m, 16, 64, 128, lane < 96), (inv_m, 16, 0, 128, lane < 32),
                                              (inv_d, 8, 0, 64, lane >= 0)):
        freq, x1, x2 = _lane_consts(freqs, half, first, period)
        ang = p * freq
        sin = jnp.sin(ang)
        tabs.append((jnp.cos(ang) * keep.astype(np.float32)[None, :], sin * x2, sin * (-x1)))
    return tuple(tabs)


def _inproj(x, gpre, w_pad, d_tab, shards):
    s = x.shape[0]
    tm, tn = min(1024, s), 768
    ni, nj = s // tm, IN_PAD // tn
    rope_lo, rope_hi = Q_OFF // tn, V_OFF // tn
    n = len(shards)
    forward_step = min(nj, ni * nj - 2)

    def body(*refs):
        x_ref, g_ref, w_ref, c_ref, sp_ref, sm_ref = refs[:6]
        ins, o_ref, ht_ref, outs = refs[6:6 + n], refs[6 + n], refs[7 + n], refs[8 + n:8 + 2 * n]
        h_ref, sems = refs[8 + 2 * n], refs[9 + 2 * n]
        i, j = pl.program_id(0), pl.program_id(1)
        step = i * nj + j

        @pl.when(j == 0)
        def _():
            hv, _, _ = _rms(x_ref[...], g_ref[...])
            h_ref[...] = hv.astype(BF16)
            ht_ref[...] = hv.astype(BF16).T

        if n:
            @pl.when(step == 0)
            def _():
                _gather_start(ins, outs, sems)

            @pl.when(step == forward_step)
            def _():
                _gather_forward(ins, outs, sems)

        acc = _dot(h_ref[...], w_ref[...])
        is_rope = jnp.logical_and(j >= rope_lo, j < rope_hi)

        @pl.when(is_rope)
        def _():
            o_ref[...] = _rope(acc, _tile_lanes(c_ref[...], tn), _tile_lanes(sp_ref[...], tn),
                               _tile_lanes(sm_ref[...], tn), 8).astype(BF16)

        @pl.when(jnp.logical_not(is_rope))
        def _():
            o_ref[...] = acc.astype(BF16)

        if n:
            @pl.when(step == ni * nj - 1)
            def _():
                _gather_finish(ins, outs, sems)

    row = lambda i, j: (i, 0)
    hbm = pl.BlockSpec(memory_space=pl.ANY)
    res = pl.pallas_call(
        body, name="inproj", grid=(ni, nj),
        in_specs=[_full((tm, D_MODEL), row), _full((1, D_MODEL), lambda i, j: (0, 0)),
                  _full((D_MODEL, tn), lambda i, j: (0, j)),
                  _full((tm, LANES), row), _full((tm, LANES), row), _full((tm, LANES), row)] + [hbm] * n,
        out_specs=[_full((tm, tn), lambda i, j: (i, j)), _full((D_MODEL, tm), lambda i, j: (0, i))] + [hbm] * n,
        out_shape=[jax.ShapeDtypeStruct((s, IN_PAD), BF16), jax.ShapeDtypeStruct((D_MODEL, s), BF16)]
        + [jax.ShapeDtypeStruct((N_DEV,) + a.shape, a.dtype) for a in shards],
        scratch_shapes=[pltpu.VMEM((tm, D_MODEL), BF16), pltpu.SemaphoreType.DMA((max(2 * n, 2), N_DEV))],
        compiler_params=_cparams(("arbitrary", "arbitrary")),
    )(x, gpre, w_pad, *d_tab, *shards)
    return res[0], res[1], res[2:]


def _mla_prep(proj, gq, gkv, w_uq, w_uk, w_uv, q_tab, k_tab):
    s = proj.shape[0]
    tm = min(512, s)

    def body(cq_ref, kr_ref, ckv_ref, gq_ref, gkv_ref, wq_ref, wk_ref, wv_ref,
             qc, qsp, qsm, kc, ksp, ksm, q_out, k_out, v_out, qt_out, kt_out, vt_out):
        cqn, _, _ = _rms(cq_ref[...].astype(F32), gq_ref[...])
        q = _dot(cqn.astype(BF16), wq_ref[...])
        q = _rope(q, _tile_lanes(qc[...], 1024), _tile_lanes(qsp[...], 1024), _tile_lanes(qsm[...], 1024), 16)
        q = q * MLA_QSCALE
        q_out[...] = q.astype(BF16)
        qt_out[...] = q.T.astype(BF16)
        ckvn, _, _ = _rms(ckv_ref[...].astype(F32), gkv_ref[...])
        ckvn = ckvn.astype(BF16)
        kr = _rope(kr_ref[...].astype(F32), kc[...], ksp[...], ksm[...], 16)
        k = _dot(ckvn, wk_ref[...]) + _tile_lanes(pltpu.roll(kr, 64, 1), 1024)
        k_out[...] = k.astype(BF16)
        kt_out[...] = k.T.astype(BF16)
        v = _dot(ckvn, wv_ref[...])
        v_out[...] = v.astype(BF16)
        vt_out[...] = v.T.astype(BF16)

    row = lambda i: (i, 0)
    col = lambda i: (0, i)
    cst = lambda i: (0, 0)
    tabs = [_full((tm, LANES), row)] * 6
    return pl.pallas_call(
        body, name="mla_prep", grid=(s // tm,),
        in_specs=[_full((tm, 384), lambda i: (i, CQ_OFF // 384)), _full((tm, 128), lambda i: (i, KR_OFF // 128)),
                  _full((tm, 256), lambda i: (i, CKV_OFF // 256)), _full((1, 384), cst), _full((1, 256), cst),
                  _full((384, 1024), cst), _full((256, 1024), cst), _full((256, 512), cst)] + tabs,
        out_specs=[_full((tm, 1024), row), _full((tm, 1024), row), _full((tm, 512), row),
                   _full((1024, tm), col), _full((1024, tm), col), _full((512, tm), col)],
        out_shape=[jax.ShapeDtypeStruct((s, 1024), BF16), jax.ShapeDtypeStruct((s, 1024), BF16),
                   jax.ShapeDtypeStruct((s, 512), BF16), jax.ShapeDtypeStruct((1024, s), BF16),
                   jax.ShapeDtypeStruct((1024, s), BF16), jax.ShapeDtypeStruct((512, s), BF16)],
        compiler_params=_cparams(("parallel",)),
    )(proj, proj, proj, gq, gkv, w_uq, w_uk, w_uv, *q_tab, *k_tab)


def _mla_fwd(q, k, vt):
    s = q.shape[0]
    tk = min(MLA_FWD_TK, s)
    ratio = MLA_FWD_Q_PER_K if s >= MLA_FWD_Q_PER_K * tk else 1
    tq = ratio * tk
    nq = s // tq

    def body(q_ref, k_ref, vt_ref, o_ref, ot_ref, lse_ref):
        krow = lax.broadcasted_iota(jnp.int32, (tk, tq), 0)
        qcol = lax.broadcasted_iota(jnp.int32, (tk, tq), 1)

        def q_step(i, _):
            r0 = pl.multiple_of(i * tq, tq)
            qs = [q_ref[pl.ds(r0, tq), hh * 128:(hh + 1) * 128] for hh in range(2)]

            def scores(j):
                c0 = pl.multiple_of(j * tk, tk)
                return tuple(_dot_nt(k_ref[pl.ds(c0, tk), hh * 128:(hh + 1) * 128], qs[hh])
                             for hh in range(2))

            def update(j, sts, stats, masked):
                c0 = pl.multiple_of(j * tk, tk)
                new = []
                causal = (krow + (c0 - r0)) <= qcol
                for hh in range(2):
                    m, l, acc = stats[hh]
                    st = jnp.where(causal, sts[hh], NEG) if masked else sts[hh]
                    m_new = jnp.maximum(m, jnp.max(st, axis=0, keepdims=True))
                    alpha = jnp.exp2(m - m_new)
                    p = jnp.exp2(st - m_new)
                    l = alpha * l + jnp.sum(p, axis=0, keepdims=True)
                    acc = acc * alpha + _dot(vt_ref[hh * 64:(hh + 1) * 64, pl.ds(c0, tk)], p.astype(BF16))
                    new.append((m_new, l, acc))
                return tuple(new)

            init = tuple((jnp.full((1, tq), NEG, F32), jnp.zeros((1, tq), F32), jnp.zeros((64, tq), F32))
                         for _ in range(2))
            stats = lax.fori_loop(0, ratio * i, lambda j, st: update(j, scores(j), st, False), init)
            for d in range(ratio):
                stats = update(ratio * i + d, scores(ratio * i + d), stats, True)
            (ma, la, acca), (mb, lb, accb) = stats
            ot = jnp.concatenate([acca / la, accb / lb], axis=0)
            ot_ref[:, pl.ds(r0, tq)] = ot.astype(BF16)
            o_ref[pl.ds(r0, tq), :] = ot.T.astype(BF16)
            lse_ref[:, pl.ds(r0, tq)] = jnp.concatenate(
                [ma + jnp.log2(la), mb + jnp.log2(lb), jnp.zeros((6, tq), F32)], axis=0)
            return 0

        lax.fori_loop(0, nq, q_step, 0)

    return pl.pallas_call(
        body, name="mla_fwd", grid=(4,),
        in_specs=[_full((s, 256), lambda p: (0, p)), _full((s, 256), lambda p: (0, p)),
                  _full((128, s), lambda p: (p, 0))],
        out_specs=[_full((s, 128), lambda p: (0, p)), _full((128, s), lambda p: (p, 0)),
                   _full((8, s), lambda p: (p, 0))],
        out_shape=[jax.ShapeDtypeStruct((s, 512), BF16), jax.ShapeDtypeStruct((512, s), BF16),
                   jax.ShapeDtypeStruct((32, s), F32)],
        compiler_params=_cparams(("parallel",)),
    )(q, k, vt)


def _band_mask(has_prev):
    r = lax.broadcasted_iota(jnp.int32, (Q_BLOCK, 2 * Q_BLOCK), 0)
    c = lax.broadcasted_iota(jnp.int32, (Q_BLOCK, 2 * Q_BLOCK), 1)
    lo = jnp.where(has_prev, r, Q_BLOCK)
    return jnp.logical_and(c >= lo, c <= r + Q_BLOCK)


def _dil_rows(b, d, per_seq):
    r, n = b // per_seq, b % per_seq
    start = r + (d * Q_BLOCK) * n
    prev = start - jnp.where(n > 0, d * Q_BLOCK, 0)
    if d == 1:
        return pl.ds(pl.multiple_of(start, Q_BLOCK), Q_BLOCK), pl.ds(pl.multiple_of(prev, Q_BLOCK), Q_BLOCK)
    return pl.ds(start, Q_BLOCK, stride=d), pl.ds(prev, Q_BLOCK, stride=d)


def _dil_fwd(proj, g, d, name):
    s = proj.shape[0]
    nblk = s // Q_BLOCK
    per_seq = nblk // d

    def body(q_ref, k_ref, v_ref, o_ref, lse_ref, qf, kf, vf, of, lf):
        left = _left_mask()
        hms = (left, jnp.logical_not(left))
        qf[...] = q_ref[...].astype(F32)
        kf[...] = k_ref[...].astype(F32)
        vf[...] = v_ref[...].astype(F32)

        def scores(b):
            rows, prow = _dil_rows(b, d, per_seq)
            qb = qf[rows, :].astype(BF16)
            kk = jnp.concatenate([kf[prow, :], kf[rows, :]], axis=0).astype(BF16)
            return [_dot_nt(jnp.where(hms[hh], qb, jnp.zeros_like(qb)), kk) * DIL_SCALE for hh in range(2)]

        def finish(b, tiles):
            rows, prow = _dil_rows(b, d, per_seq)
            mask = _band_mask((b % per_seq) > 0)
            vv = jnp.concatenate([vf[prow, :], vf[rows, :]], axis=0).astype(BF16)
            outs = []
            for hh in range(2):
                sc = jnp.where(mask, tiles[hh], NEG)
                m = jnp.max(sc, axis=1, keepdims=True)
                p = jnp.exp(sc - m)
                l = jnp.sum(p, axis=1, keepdims=True)
                acc = _dot(p.astype(BF16), jnp.where(hms[hh], vv, jnp.zeros_like(vv)))
                outs.append((acc / l, jnp.broadcast_to(m + jnp.log(l), (Q_BLOCK, LANES))))
            of[rows, :] = outs[0][0] + outs[1][0]
            lf[rows, :] = jnp.where(left, outs[0][1], outs[1][1])

        def step(t, _):
            tiles = [scores(DIL_BLOCKS_PER_STEP * t + u) for u in range(DIL_BLOCKS_PER_STEP)]
            for u in range(DIL_BLOCKS_PER_STEP):
                finish(DIL_BLOCKS_PER_STEP * t + u, tiles[u])
            return 0

        lax.fori_loop(0, nblk // DIL_BLOCKS_PER_STEP, step, 0)
        o_ref[...] = of[...].astype(BF16)
        lse_ref[...] = lf[...]

    blk = lambda off: _full((s, 128), lambda p, off=off: (0, off + p))
    qo, ko, vo = ((off + 512 * g) // 128 for off in (Q_OFF, K_OFF, V_OFF))
    return pl.pallas_call(
        body, name=name, grid=(4,),
        in_specs=[blk(qo), blk(ko), blk(vo)],
        out_specs=[blk(0), blk(0)],
        out_shape=[jax.ShapeDtypeStruct((s, 512), BF16), jax.ShapeDtypeStruct((s, 512), F32)],
        scratch_shapes=[pltpu.VMEM((s, 128), F32)] * 5,
        compiler_params=_cparams(("parallel",)),
    )(proj, proj, proj)


def _merge_fwd(proj, o_mla, od, lsed, wp_mla, wp_dil):
    s = proj.shape[0]
    tm = min(512, s)

    def body(zm_ref, zd_ref, gm_ref, gd_ref, om_ref, o0, o1, o2, l0, l1, l2, wm_ref, wd_ref,
             mg_out, ya_out, yd_out, odil_out, lse_out):
        la, lb, lc = l0[...], l1[...], l2[...]
        lmax = jnp.maximum(jnp.maximum(la, lb), lc)
        ea, eb, ec = jnp.exp(la - lmax), jnp.exp(lb - lmax), jnp.exp(lc - lmax)
        den = ea + eb + ec
        o_dil = (ea * o0[...].astype(F32) + eb * o1[...].astype(F32) + ec * o2[...].astype(F32)) / den
        o_dil = o_dil.astype(BF16)
        odil_out[...] = o_dil
        lse_out[...] = lmax + jnp.log(den)
        zm, zd = zm_ref[...].astype(F32), zd_ref[...].astype(F32)
        pa = (om_ref[...].astype(F32) * (zm * _sigmoid(zm))).astype(BF16)
        pd = (o_dil.astype(F32) * (zd * _sigmoid(zd))).astype(BF16)
        ya = _dot(pa, wm_ref[...])
        yd = _dot(pd, wd_ref[...])
        ya_out[...] = ya.astype(BF16)
        yd_out[...] = yd.astype(BF16)
        mg_out[...] = (_sigmoid(gm_ref[...].astype(F32)) * ya + _sigmoid(gd_ref[...].astype(F32)) * yd).astype(BF16)

    row = lambda i: (i, 0)
    cst = lambda i: (0, 0)
    r512 = _full((tm, 512), row)
    r1024 = _full((tm, 1024), row)
    return pl.pallas_call(
        body, name="merge_fwd", grid=(s // tm,),
        in_specs=[_full((tm, 512), lambda i: (i, Z_MLA // 512)), _full((tm, 512), lambda i: (i, Z_DIL // 512)),
                  _full((tm, 1024), lambda i: (i, G_MLA // 1024)), _full((tm, 1024), lambda i: (i, G_DIL // 1024)),
                  r512, r512, r512, r512, r512, r512, r512, _full((512, 1024), cst), _full((512, 1024), cst)],
        out_specs=[r1024, r1024, r1024, r512, r512],
        out_shape=[jax.ShapeDtypeStruct((s, 1024), BF16), jax.ShapeDtypeStruct((s, 1024), BF16),
                   jax.ShapeDtypeStruct((s, 1024), BF16), jax.ShapeDtypeStruct((s, 512), BF16),
                   jax.ShapeDtypeStruct((s, 512), F32)],
        compiler_params=_cparams(("parallel",)),
    )(proj, proj, proj, proj, o_mla, *od, *lsed, wp_mla, wp_dil)


def _out_loss(merged, w_out, x, target, gpost):
    s = x.shape[0]
    tm = min(512, s)

    def body(mg_ref, w_ref, x_ref, t_ref, g_ref, do_out, dy_out, loss_out, dg_out):
        i = pl.program_id(0)

        @pl.when(i == 0)
        def _():
            loss_out[...] = jnp.zeros_like(loss_out)
            dg_out[...] = jnp.zeros_like(dg_out)

        o = _dot(mg_ref[...], w_ref[...])
        g = g_ref[...]
        n, u, r = _rms(o, g)
        e = (x_ref[...] + n) - t_ref[...]
        loss_out[...] += 0.5 * jnp.sum(jnp.mean(e * e, axis=-1, keepdims=True))
        dy = e * (1.0 / D_MODEL)
        dy_out[...] = dy
        dg_out[...] += jnp.sum(dy * u, axis=0, keepdims=True)
        do_out[...] = _rms_bwd(dy, g, u, r).astype(BF16)

    row = lambda i: (i, 0)
    cst = lambda i: (0, 0)
    return pl.pallas_call(
        body, name="out_loss", grid=(s // tm,),
        in_specs=[_full((tm, 1024), row), _full((1024, 1024), cst), _full((tm, 1024), row), _full((tm, 1024), row),
                  _full((1, 1024), cst)],
        out_specs=[_full((tm, 1024), row), _full((tm, 1024), row), _full((8, LANES), cst), _full((1, 1024), cst)],
        out_shape=[jax.ShapeDtypeStruct((s, 1024), BF16), jax.ShapeDtypeStruct((s, 1024), F32),
                   jax.ShapeDtypeStruct((8, LANES), F32), jax.ShapeDtypeStruct((1, 1024), F32)],
        compiler_params=_cparams(("arbitrary",)),
    )(merged, w_out, x, target, gpost)


def _seg_sum64(x, ones_bd):
    hi = x.astype(BF16)
    lo = (x - hi.astype(F32)).astype(BF16)
    return _dot(hi, ones_bd) + _dot(lo, ones_bd)


def _merge_bwd(do, w_out, merged, proj, ya, yd, o_mla, o_dil, wp_mla, wp_dil):
    s = do.shape[0]
    tm = min(256, s)
    seg = jnp.arange(512) // 64
    ones_bd = (seg[:, None] == seg[None, :]).astype(BF16)

    def body(do_ref, wo_ref, mg_ref, zm_ref, zd_ref, gm_ref, gd_ref, ya_ref, yd_ref, om_ref, od_ref, wm_ref, wd_ref,
             bd_ref, dzm_out, dzd_out, dgm_out, dgd_out, dom_out, dod_out, domt_out, dd_out, dwo_out, dwm_out,
             dwd_out):
        i = pl.program_id(0)

        @pl.when(i == 0)
        def _():
            dwo_out[...] = jnp.zeros_like(dwo_out)
            dwm_out[...] = jnp.zeros_like(dwm_out)
            dwd_out[...] = jnp.zeros_like(dwd_out)

        dov = do_ref[...]
        dwo_out[...] += _dot_tn(mg_ref[...], dov)
        dm = _dot_nt(dov, wo_ref[...])
        for g_ref, y_ref, z_ref, o_ref, w_ref, dz_out, dg_out, dob_out, dd_o, dw_out in (
                (gm_ref, ya_ref, zm_ref, om_ref, wm_ref, dzm_out, dgm_out, dom_out, None, dwm_out),
                (gd_ref, yd_ref, zd_ref, od_ref, wd_ref, dzd_out, dgd_out, dod_out, dd_out, dwd_out)):
            sg = _sigmoid(g_ref[...].astype(F32))
            dg_out[...] = (dm * y_ref[...].astype(F32) * sg * (1.0 - sg)).astype(BF16)
            dy = (dm * sg).astype(BF16)
            z = z_ref[...].astype(F32)
            sz = _sigmoid(z)
            silu = z * sz
            ob = o_ref[...].astype(F32)
            dw_out[...] += _dot_tn((ob * silu).astype(BF16), dy)
            dp = _dot_nt(dy, w_ref[...])
            dz_out[...] = (dp * ob * (sz * (1.0 + z * (1.0 - sz)))).astype(BF16)
            dob = dp * silu
            dob_out[...] = dob.astype(BF16)
            if dd_o is None:
                domt_out[...] = dob.T.astype(BF16)
            else:
                dd_o[...] = _seg_sum64(dob * ob, bd_ref[...])

    row = lambda i: (i, 0)
    cst = lambda i: (0, 0)
    r512 = _full((tm, 512), row)
    r1024 = _full((tm, 1024), row)
    return pl.pallas_call(
        body, name="merge_bwd", grid=(s // tm,),
        in_specs=[r1024, _full((1024, 1024), cst), r1024,
                  _full((tm, 512), lambda i: (i, Z_MLA // 512)), _full((tm, 512), lambda i: (i, Z_DIL // 512)),
                  _full((tm, 1024), lambda i: (i, G_MLA // 1024)), _full((tm, 1024), lambda i: (i, G_DIL // 1024)),
                  r1024, r1024, r512, r512, _full((512, 1024), cst), _full((512, 1024), cst), _full((512, 512), cst)],
        out_specs=[r512, r512, r1024, r1024, r512, r512, _full((512, tm), lambda i: (0, i)), r512,
                   _full((1024, 1024), cst), _full((512, 1024), cst), _full((512, 1024), cst)],
        out_shape=[jax.ShapeDtypeStruct((s, 512), BF16), jax.ShapeDtypeStruct((s, 512), BF16),
                   jax.ShapeDtypeStruct((s, 1024), BF16), jax.ShapeDtypeStruct((s, 1024), BF16),
                   jax.ShapeDtypeStruct((s, 512), BF16), jax.ShapeDtypeStruct((s, 512), BF16),
                   jax.ShapeDtypeStruct((512, s), BF16), jax.ShapeDtypeStruct((s, 512), F32),
                   jax.ShapeDtypeStruct((1024, 1024), F32), jax.ShapeDtypeStruct((512, 1024), F32),
                   jax.ShapeDtypeStruct((512, 1024), F32)],
        compiler_params=_cparams(("arbitrary",)),
    )(do, w_out, merged, proj, proj, proj, proj, ya, yd, o_mla, o_dil, wp_mla, wp_dil, ones_bd)


def _mla_bwd(q, qt, k, kt, v, do, dot, ot, lse):
    s = q.shape[0]
    tk = min(256, s)
    ratio = MLA_BWD_Q_PER_K if s >= MLA_BWD_Q_PER_K * tk else 1
    tq = ratio * tk
    nq, nk = s // tq, s // tk

    def body(q_ref, qt_ref, k_ref, kt_ref, v_ref, do_ref, dot_ref, ot_ref, lse_ref, dqt_out, dkt_out, dvt_out,
             dqt_acc):
        left = _left_mask()
        krow = lax.broadcasted_iota(jnp.int32, (tk, tq), 0)
        qcol = lax.broadcasted_iota(jnp.int32, (tk, tq), 1)
        dqt_acc[...] = jnp.zeros_like(dqt_acc)

        def kv_step(j, _):
            c0 = pl.multiple_of(j * tk, tk)
            vv = v_ref[pl.ds(c0, tk), :]
            khs = [k_ref[pl.ds(c0, tk), hh * 128:(hh + 1) * 128] for hh in range(2)]
            kths = [kt_ref[hh * 128:(hh + 1) * 128, pl.ds(c0, tk)] for hh in range(2)]
            vms = [jnp.where(left if hh == 0 else jnp.logical_not(left), vv, jnp.zeros_like(vv)) for hh in range(2)]

            def scores(i):
                r0 = pl.multiple_of(jnp.minimum(i, nq - 1) * tq, tq)
                dov = do_ref[pl.ds(r0, tq), :]
                return tuple((_dot_nt(khs[hh], q_ref[pl.ds(r0, tq), hh * 128:(hh + 1) * 128]),
                              _dot_nt(vms[hh], dov)) for hh in range(2))

            def update(i, tiles, acc, masked):
                r0 = pl.multiple_of(i * tq, tq)
                new = []
                for hh in range(2):
                    dkt, dvt = acc[hh]
                    st, dp = tiles[hh]
                    hrows = slice(hh * 128, (hh + 1) * 128)
                    drows = slice(hh * 64, (hh + 1) * 64)
                    doth = dot_ref[drows, pl.ds(r0, tq)]
                    dd = jnp.sum(doth.astype(F32) * ot_ref[drows, pl.ds(r0, tq)].astype(F32), axis=0, keepdims=True)
                    p = jnp.exp2(st - lse_ref[hh:hh + 1, pl.ds(r0, tq)])
                    if masked:
                        p = jnp.where((krow + (c0 - r0)) <= qcol, p, 0.0)
                    ds = (p * (dp - dd)).astype(BF16)
                    dvt = dvt + _dot_nt(doth, p.astype(BF16))
                    dkt = dkt + _dot_nt(qt_ref[hrows, pl.ds(r0, tq)], ds)
                    dqt_acc[hrows, pl.ds(r0, tq)] += _dot(kths[hh], ds)
                    new.append((dkt, dvt))
                return tuple(new)

            init = tuple((jnp.zeros((128, tk), F32), jnp.zeros((64, tk), F32)) for _ in range(2))
            i0 = j // ratio
            acc = update(i0, scores(i0), init, True)
            acc = lax.fori_loop(i0 + 1, nq, lambda i, a: update(i, scores(i), a, False), acc)
            for hh in range(2):
                dkt_out[hh * 128:(hh + 1) * 128, pl.ds(c0, tk)] = (acc[hh][0] * (1.0 / LOG2E)).astype(BF16)
                dvt_out[hh * 64:(hh + 1) * 64, pl.ds(c0, tk)] = acc[hh][1].astype(BF16)
            return 0

        lax.fori_loop(0, nk, kv_step, 0)
        dqt_out[...] = (dqt_acc[...] * MLA_SCALE).astype(BF16)

    b256 = _full((s, 256), lambda p: (0, p))
    b128 = _full((s, 128), lambda p: (0, p))
    t256 = _full((256, s), lambda p: (p, 0))
    t128 = _full((128, s), lambda p: (p, 0))
    return pl.pallas_call(
        body, name="mla_bwd", grid=(4,),
        in_specs=[b256, t256, b256, t256, b128, b128, t128, t128, _full((8, s), lambda p: (p, 0))],
        out_specs=[t256, t256, t128],
        out_shape=[jax.ShapeDtypeStruct((1024, s), BF16), jax.ShapeDtypeStruct((1024, s), BF16),
                   jax.ShapeDtypeStruct((512, s), BF16)],
        scratch_shapes=[pltpu.VMEM((256, s), F32)],
        compiler_params=_cparams(("parallel",)),
    )(q, qt, k, kt, v, do, dot, ot, lse)


def _mla_prep_bwd(dq, dk, dv, proj, gq, gkv, w_uq, w_uk, w_uv, q_tab, k_tab):
    s = proj.shape[0]
    tm = min(256, s)

    def body(dqt_ref, dkt_ref, dvt_ref, cq_ref, ckv_ref, gq_ref, gkv_ref, wq_ref, wk_ref, wv_ref,
             qc, qsp, qsm, kc, ksp, ksm,
             dcq_out, dkr_out, dckv_out, dwq_out, dwk_out, dwv_out, dgq_out, dgkv_out):
        i = pl.program_id(0)

        @pl.when(i == 0)
        def _():
            for r in (dwq_out, dwk_out, dwv_out, dgq_out, dgkv_out):
                r[...] = jnp.zeros_like(r)

        dqu = _unrope(dqt_ref[...].astype(F32).T, _tile_lanes(qc[...], 1024), _tile_lanes(qsp[...], 1024),
                      _tile_lanes(qsm[...], 1024), 16).astype(BF16)
        gq = gq_ref[...]
        cqn, xh, r = _rms(cq_ref[...].astype(F32), gq)
        dwq_out[...] += _dot_tn(cqn.astype(BF16), dqu)
        dcqn = _dot_nt(dqu, wq_ref[...])
        dgq_out[...] += jnp.sum(dcqn * xh, axis=0, keepdims=True)
        dcq_out[...] = _rms_bwd(dcqn, gq, xh, r).astype(BF16)

        dkf = dkt_ref[...].astype(F32).T
        dkb = dkf.astype(BF16)
        dsum = dkf[:, 0:128]
        for h in range(1, MLA_HEADS):
            dsum = dsum + dkf[:, h * 128:(h + 1) * 128]
        dkr_out[...] = _unrope(pltpu.roll(dsum, 64, 1), kc[...], ksp[...], ksm[...], 16).astype(BF16)

        dvb = dvt_ref[...].astype(F32).T.astype(BF16)
        gkv = gkv_ref[...]
        ckvn, xh2, r2 = _rms(ckv_ref[...].astype(F32), gkv)
        ckvn = ckvn.astype(BF16)
        dwk_out[...] += _dot_tn(ckvn, dkb)
        dwv_out[...] += _dot_tn(ckvn, dvb)
        dckvn = _dot_nt(dkb, wk_ref[...]) + _dot_nt(dvb, wv_ref[...])
        dgkv_out[...] += jnp.sum(dckvn * xh2, axis=0, keepdims=True)
        dckv_out[...] = _rms_bwd(dckvn, gkv, xh2, r2).astype(BF16)

    row = lambda i: (i, 0)
    cst = lambda i: (0, 0)
    tabs = [_full((tm, LANES), row)] * 6
    return pl.pallas_call(
        body, name="mla_prep_bwd", grid=(s // tm,),
        in_specs=[_full((1024, tm), lambda i: (0, i)), _full((1024, tm), lambda i: (0, i)),
                  _full((512, tm), lambda i: (0, i)),
                  _full((tm, 384), lambda i: (i, CQ_OFF // 384)), _full((tm, 256), lambda i: (i, CKV_OFF // 256)),
                  _full((1, 384), cst), _full((1, 256), cst),
                  _full((384, 1024), cst), _full((256, 1024), cst), _full((256, 512), cst)] + tabs,
        out_specs=[_full((tm, 384), row), _full((tm, 128), row), _full((tm, 256), row),
                   _full((384, 1024), cst), _full((256, 1024), cst), _full((256, 512), cst),
                   _full((1, 384), cst), _full((1, 256), cst)],
        out_shape=[jax.ShapeDtypeStruct((s, 384), BF16), jax.ShapeDtypeStruct((s, 128), BF16),
                   jax.ShapeDtypeStruct((s, 256), BF16),
                   jax.ShapeDtypeStruct((384, 1024), F32), jax.ShapeDtypeStruct((256, 1024), F32),
                   jax.ShapeDtypeStruct((256, 512), F32),
                   jax.ShapeDtypeStruct((1, 384), F32), jax.ShapeDtypeStruct((1, 256), F32)],
        compiler_params=_cparams(("arbitrary",)),
    )(dq, dk, dv, proj, proj, gq, gkv, w_uq, w_uk, w_uv, *q_tab, *k_tab)


def _dil_bwd(proj, g, do, lse, dd, tabs, d, name):
    s = proj.shape[0]
    nblk = s // Q_BLOCK
    per_seq = nblk // d

    def body(q_ref, k_ref, v_ref, do_ref, lse_ref, dd_ref, c_ref, sp_ref, sm_ref, dq_out, dk_out, dv_out,
             qf, kf, vf, dof, dq_acc, dk_acc, dv_acc):
        left = _left_mask()
        hms = (left, jnp.logical_not(left))
        qf[...] = q_ref[...].astype(F32)
        kf[...] = k_ref[...].astype(F32)
        vf[...] = v_ref[...].astype(F32)
        dof[...] = do_ref[...].astype(F32)
        dk_acc[...] = jnp.zeros_like(dk_acc)
        dv_acc[...] = jnp.zeros_like(dv_acc)

        def scores(b):
            rows, prow = _dil_rows(b, d, per_seq)
            qb, dob = qf[rows, :].astype(BF16), dof[rows, :].astype(BF16)
            kk = jnp.concatenate([kf[prow, :], kf[rows, :]], axis=0).astype(BF16)
            vv = jnp.concatenate([vf[prow, :], vf[rows, :]], axis=0).astype(BF16)
            zero = jnp.zeros_like(qb)
            out = []
            for hh in range(2):
                qm, dom = jnp.where(hms[hh], qb, zero), jnp.where(hms[hh], dob, zero)
                out.append((_dot_nt(qm, kk) * DIL_SCALE, _dot_nt(dom, vv)))
            return out

        def finish(b, tiles):
            rows, prow = _dil_rows(b, d, per_seq)
            mask = _band_mask((b % per_seq) > 0)
            qb, dob = qf[rows, :].astype(BF16), dof[rows, :].astype(BF16)
            kk = jnp.concatenate([kf[prow, :], kf[rows, :]], axis=0).astype(BF16)
            lse_b, dd_b = lse_ref[rows, :], dd_ref[rows, :]
            zero = jnp.zeros_like(qb)
            dq = jnp.zeros((Q_BLOCK, LANES), F32)
            dk = jnp.zeros((2 * Q_BLOCK, LANES), F32)
            dv = jnp.zeros((2 * Q_BLOCK, LANES), F32)
            for hh in range(2):
                hm = hms[hh]
                qm, dom = jnp.where(hm, qb, zero), jnp.where(hm, dob, zero)
                lse_h = _tile_lanes(_expand_half(lse_b, hh, left), 2 * Q_BLOCK)
                dd_h = _tile_lanes(_expand_half(dd_b, hh, left), 2 * Q_BLOCK)
                sc, dp = tiles[hh]
                p = jnp.where(mask, jnp.exp(sc - lse_h), 0.0)
                ds = (p * (dp - dd_h) * DIL_SCALE).astype(BF16)
                dq = dq + _dot(ds, jnp.where(hm, kk, jnp.zeros_like(kk)))
                dk = dk + _dot_tn(ds, qm)
                dv = dv + _dot_tn(p.astype(BF16), dom)
            dq_acc[rows, :] = dq
            dk_acc[prow, :] += dk[0:Q_BLOCK]
            dv_acc[prow, :] += dv[0:Q_BLOCK]
            dk_acc[rows, :] += dk[Q_BLOCK:]
            dv_acc[rows, :] += dv[Q_BLOCK:]

        def step(t, _):
            tiles = [scores(DIL_BLOCKS_PER_STEP * t + u) for u in range(DIL_BLOCKS_PER_STEP)]
            for u in range(DIL_BLOCKS_PER_STEP):
                finish(DIL_BLOCKS_PER_STEP * t + u, tiles[u])
            return 0

        lax.fori_loop(0, nblk // DIL_BLOCKS_PER_STEP, step, 0)
        dq_out[...] = _unrope(dq_acc[...], c_ref[...], sp_ref[...], sm_ref[...], 8).astype(BF16)
        dk_out[...] = _unrope(dk_acc[...], c_ref[...], sp_ref[...], sm_ref[...], 8).astype(BF16)
        dv_out[...] = dv_acc[...].astype(BF16)

    blk = lambda off: _full((s, 128), lambda p, off=off: (0, off + p))
    tab = _full((s, 128), lambda p: (0, 0))
    qo, ko, vo = ((off + 512 * g) // 128 for off in (Q_OFF, K_OFF, V_OFF))
    return pl.pallas_call(
        body, name=name, grid=(4,),
        in_specs=[blk(qo), blk(ko), blk(vo), blk(0), blk(0), blk(0), tab, tab, tab],
        out_specs=[blk(0), blk(0), blk(0)],
        out_shape=[jax.ShapeDtypeStruct((s, 512), BF16)] * 3,
        scratch_shapes=[pltpu.VMEM((s, 128), F32)] * 7,
        compiler_params=_cparams(("parallel",)),
    )(proj, proj, proj, do, lse, dd, *tabs)


def _chip_copies(ins, outs, sems, outgoing):
    x, y, c, chips = _place()
    myq = 2 * x + y
    n = len(ins)

    def chunk(a, j):
        q = 2 * chips[j][0] + chips[j][1]
        return _remote(ins[a].at[q], outs[a].at[myq if outgoing else q], sems, 2 * a, j, (*chips[j], c))

    if outgoing is None:
        return [pltpu.make_async_copy(ins[a].at[myq], outs[a].at[myq], sems.at[2 * a, 3]) for a in range(n)]
    return [chunk(a, j) for j in range(3) for a in range(n)]


def _dh_bwd(dproj, w_pad, x, gpre, dy, pairs):
    s = x.shape[0]
    tm, tk = min(1024, s), 1408
    ni, nk = s // tm, IN_PAD // tk
    n = len(pairs)

    def body(*refs):
        dp_ref, w_ref, x_ref, g_ref, dy_ref = refs[:5]
        ins, gx_out, dg_out, outs = refs[5:5 + n], refs[5 + n], refs[6 + n], refs[7 + n:7 + 2 * n]
        acc, sems = refs[7 + 2 * n], refs[8 + 2 * n]
        i, kk = pl.program_id(0), pl.program_id(1)

        @pl.when(jnp.logical_and(i == 0, kk == 0))
        def _():
            dg_out[...] = jnp.zeros_like(dg_out)
            for cp in _chip_copies(ins, outs, sems, None) + _chip_copies(ins, outs, sems, True):
                cp.start()

        @pl.when(kk == 0)
        def _():
            acc[...] = jnp.zeros_like(acc)

        acc[...] += _dot_nt(dp_ref[...], w_ref[...])

        @pl.when(kk == nk - 1)
        def _():
            g = g_ref[...]
            _, xh, r = _rms(x_ref[...], g)
            dh = acc[...]
            dg_out[...] += jnp.sum(dh * xh, axis=0, keepdims=True)
            gx_out[...] = dy_ref[...] + _rms_bwd(dh, g, xh, r)

        @pl.when(jnp.logical_and(i == ni - 1, kk == nk - 1))
        def _():
            for cp in _chip_copies(ins, outs, sems, False):
                cp.wait_recv()
            for cp in _chip_copies(ins, outs, sems, True):
                cp.wait_send()
            for cp in _chip_copies(ins, outs, sems, None):
                cp.wait()

    row = lambda i, k: (i, 0)
    hbm = pl.BlockSpec(memory_space=pl.ANY)
    res = pl.pallas_call(
        body, name="dh_bwd", grid=(ni, nk),
        in_specs=[_full((tm, tk), lambda i, k: (i, k)), _full((1024, tk), lambda i, k: (0, k)),
                  _full((tm, 1024), row), _full((1, 1024), lambda i, k: (0, 0)), _full((tm, 1024), row)] + [hbm] * n,
        out_specs=[_full((tm, 1024), row), _full((1, 1024), lambda i, k: (0, 0))] + [hbm] * n,
        out_shape=[jax.ShapeDtypeStruct((s, 1024), F32), jax.ShapeDtypeStruct((1, 1024), F32)]
        + [jax.ShapeDtypeStruct(a.shape, a.dtype) for a in pairs],
        scratch_shapes=[pltpu.VMEM((tm, 1024), F32), pltpu.SemaphoreType.DMA((2 * n, 4))],
        compiler_params=_cparams(("arbitrary", "arbitrary")),
    )(dproj, w_pad, x, gpre, dy, *pairs)
    return res[0], res[1], res[2:]


def _dw_in(ht, dproj, pairs):
    s = ht.shape[1]
    tn = 768
    nj = IN_PAD // tn
    n = len(pairs)

    def body(*refs):
        ht_ref, dp_ref = refs[:2]
        ins, o_ref, outs, sems = refs[2:2 + n], refs[2 + n], refs[3 + n:3 + 2 * n], refs[3 + 2 * n]
        j = pl.program_id(0)

        if n:
            @pl.when(j == 0)
            def _():
                for cp in _chip_copies(ins, outs, sems, None) + _chip_copies(ins, outs, sems, True):
                    cp.start()

        o_ref[...] = _dot(ht_ref[...], dp_ref[...]).astype(BF16)

        if n:
            @pl.when(j == nj - 1)
            def _():
                for cp in _chip_copies(ins, outs, sems, False):
                    cp.wait_recv()
                for cp in _chip_copies(ins, outs, sems, True):
                    cp.wait_send()
                for cp in _chip_copies(ins, outs, sems, None):
                    cp.wait()

    hbm = pl.BlockSpec(memory_space=pl.ANY)
    res = pl.pallas_call(
        body, name="dw_in", grid=(nj,),
        in_specs=[_full((1024, s), lambda j: (0, 0)), _full((s, tn), lambda j: (0, j))] + [hbm] * n,
        out_specs=[_full((1024, tn), lambda j: (0, j))] + [hbm] * n,
        out_shape=[jax.ShapeDtypeStruct((1024, IN_PAD), BF16)] + [jax.ShapeDtypeStruct(a.shape, a.dtype) for a in pairs],
        scratch_shapes=[pltpu.SemaphoreType.DMA((max(2 * n, 2), 4))],
        compiler_params=_cparams(("arbitrary",)),
    )(ht, dproj, *pairs)
    return res[0], res[1:]


def _remote(src, dst, sems, row, k, to):
    return pltpu.make_async_remote_copy(src_ref=src, dst_ref=dst, send_sem=sems.at[row, k], recv_sem=sems.at[row + 1, k],
                                        device_id=to, device_id_type=pl.DeviceIdType.MESH)


def _place():
    x, y, c = lax.axis_index("x"), lax.axis_index("y"), lax.axis_index("c")
    return x, y, c, [(1 - x, y), (x, 1 - y), (1 - x, 1 - y)]


def _gather_parts(ins, outs, sems):
    n = len(ins)
    x, y, c, chips = _place()
    me, sib = (x, y, c), (x, y, 1 - c)
    idx = lambda p: 4 * p[0] + 2 * p[1] + p[2]

    def copy(a, k, block, to, from_input=False):
        src = ins[a] if from_input else outs[a].at[idx(block)]
        return _remote(src, outs[a].at[idx(block)], sems, 2 * a, k, to)

    own = lambda: [pltpu.make_async_copy(ins[a], outs[a].at[idx(me)], sems.at[2 * a, 7]) for a in range(n)]
    first = lambda: ([copy(a, 0, me, sib, True) for a in range(n)]
                     + [copy(a, 1 + j, me, (*chips[j], c), True) for j in range(3) for a in range(n)])
    from_chip = lambda j: [copy(a, 1 + j, (*chips[j], c), me) for a in range(n)]
    passed = lambda j: [copy(a, 4 + j, (*chips[j], c), sib) for a in range(n)]
    from_sibling = lambda: ([copy(a, 0, sib, me) for a in range(n)]
                            + [copy(a, 4 + j, (*chips[j], 1 - c), me) for j in range(3) for a in range(n)])
    return own, first, from_chip, passed, from_sibling


def _gather_start(ins, outs, sems):
    own, first, _, _, _ = _gather_parts(ins, outs, sems)
    for cp in own() + first():
        cp.start()


def _gather_forward(ins, outs, sems):
    _, _, from_chip, passed, _ = _gather_parts(ins, outs, sems)
    for j in range(3):
        for cp in from_chip(j):
            cp.wait_recv()
        for cp in passed(j):
            cp.start()


def _gather_finish(ins, outs, sems):
    own, first, _, passed, from_sibling = _gather_parts(ins, outs, sems)
    for cp in from_sibling():
        cp.wait_recv()
    for cp in first() + [cp for j in range(3) for cp in passed(j)]:
        cp.wait_send()
    for cp in own():
        cp.wait()


def _gather_weights(arrays):
    n = len(arrays)

    def body(*refs):
        ins, outs, sems = refs[:n], refs[n:2 * n], refs[2 * n]
        _gather_start(ins, outs, sems)
        _gather_forward(ins, outs, sems)
        _gather_finish(ins, outs, sems)

    hbm = pl.BlockSpec(memory_space=pl.ANY)
    return pl.pallas_call(
        body, name="gather_weights", in_specs=[hbm] * n, out_specs=[hbm] * n,
        out_shape=[jax.ShapeDtypeStruct((N_DEV,) + a.shape, a.dtype) for a in arrays],
        scratch_shapes=[pltpu.SemaphoreType.DMA((2 * n, N_DEV))],
    )(*arrays)


def _pair_exchange(chunks, name):
    n = len(chunks)

    def body(*refs):
        ins, outs, sems = refs[:n], refs[n:2 * n], refs[2 * n]
        x, y, c, _ = _place()
        sent = [_remote(ins[a].at[2 * q + (1 - c)], outs[a].at[q], sems, 2 * a, q, (x, y, 1 - c))
                for a in range(n) for q in range(4)]
        for cp in sent:
            cp.start()
        for cp in sent:
            cp.wait_recv()
        for cp in sent:
            cp.wait_send()

    hbm = pl.BlockSpec(memory_space=pl.ANY)
    return pl.pallas_call(
        body, name=name, in_specs=[hbm] * n, out_specs=[hbm] * n,
        out_shape=[jax.ShapeDtypeStruct((4,) + a.shape[1:], a.dtype) for a in chunks],
        scratch_shapes=[pltpu.SemaphoreType.DMA((2 * n, 4))],
    )(*chunks)


def _pair_sum(core, chunks, recv, name, tr):
    _, rows, cols = chunks.shape

    def body(c_ref, a_ref, b_ref, o_ref):
        o_ref[...] = (a_ref[...].astype(F32) + b_ref[...].astype(F32)).astype(BF16)

    blk = lambda f: _full((1, tr, cols), f)
    return pl.pallas_call(
        body, name=name, out_shape=jax.ShapeDtypeStruct((4, rows, cols), BF16),
        grid_spec=pltpu.PrefetchScalarGridSpec(
            num_scalar_prefetch=1, grid=(4, rows // tr),
            in_specs=[blk(lambda q, i, c: (2 * q + c[0], i, 0)), blk(lambda q, i, c: (q, i, 0))],
            out_specs=blk(lambda q, i, c: (q, i, 0))),
        compiler_params=_cparams(("parallel", "parallel")),
    )(core, chunks, recv)


def _packet_exchange(packet):
    def body(pk, pk_out, sems):
        x, y, c, _ = _place()
        me = 4 * x + 2 * y + c
        flip = lambda v, b: (1 - v) if b else v

        def small(j, outgoing):
            peer = (flip(x, (j >> 2) & 1), flip(y, (j >> 1) & 1), flip(c, j & 1))
            slot = me if outgoing else 4 * peer[0] + 2 * peer[1] + peer[2]
            return _remote(pk, pk_out.at[slot], sems, 0, j, peer)

        own = pltpu.make_async_copy(pk, pk_out.at[me], sems.at[0, 0])
        sent = [small(j, True) for j in range(1, N_DEV)]
        for cp in [own] + sent:
            cp.start()
        for j in range(1, N_DEV):
            small(j, False).wait_recv()
        for cp in sent:
            cp.wait_send()
        own.wait()

    hbm = pl.BlockSpec(memory_space=pl.ANY)
    return pl.pallas_call(
        body, name="packet_exchange", in_specs=[hbm], out_specs=hbm,
        out_shape=jax.ShapeDtypeStruct((N_DEV,) + packet.shape, packet.dtype),
        scratch_shapes=[pltpu.SemaphoreType.DMA((2, N_DEV))],
    )(packet)


def _adam_math(w, g, m, v):
    m = ADAM_B1 * m + (1.0 - ADAM_B1) * g
    v = ADAM_B2 * v + (1.0 - ADAM_B2) * (g * g)
    m_hat = m / (1.0 - ADAM_B1 ** ADAM_STEP)
    v_hat = v / (1.0 - ADAM_B2 ** ADAM_STEP)
    delta = -ADAM_LR * (m_hat / (jnp.sqrt(v_hat) + ADAM_EPS) + ADAM_WD * w)
    return delta, m, v


def _adam(recv, w, m, v, name, tr):
    _, rows, cols = w.shape

    def body(r_ref, w_ref, m_ref, v_ref, g_out, d_out, m_out, v_out):
        g = r_ref[0].astype(F32)
        for k in range(1, 4):
            g = g + r_ref[k].astype(F32)
        g_out[0] = g
        d_out[0], m_out[0], v_out[0] = _adam_math(w_ref[0], g, m_ref[0], v_ref[0])

    blk = _full((1, tr, cols), lambda i: (0, i, 0))
    return pl.pallas_call(
        body, name=name, grid=(rows // tr,),
        in_specs=[_full((4, tr, cols), lambda i: (0, i, 0)), blk, blk, blk],
        out_specs=[blk] * 4,
        out_shape=[jax.ShapeDtypeStruct(w.shape, F32)] * 4,
        compiler_params=_cparams(("parallel",)),
    )(recv, w, m, v)


def _adam_gains(recv, gains, gains_m, gains_v):
    def body(*refs):
        r_ref, w, m, v = refs[0], refs[1:5], refs[5:9], refs[9:13]
        g_out, d_out, m_out, v_out, loss_out = refs[13:17], refs[17:21], refs[21:25], refs[25:29], refs[29]
        tot = r_ref[0:1, :]
        for k in range(1, N_DEV):
            tot = tot + r_ref[k:k + 1, :]
        for t in range(4):
            g = tot[:, GAIN_OFFS[t]:GAIN_OFFS[t] + GAIN_WIDTHS[t]]
            g_out[t][...] = g
            d_out[t][...], m_out[t][...], v_out[t][...] = _adam_math(w[t][...], g, m[t][...], v[t][...])
        loss_out[...] = tot[:, LOSS_OFF:LOSS_OFF + LANES]

    shapes = [jax.ShapeDtypeStruct((1, n), F32) for n in GAIN_WIDTHS]
    return pl.pallas_call(
        body, name="adam_gains", out_shape=shapes * 4 + [jax.ShapeDtypeStruct((1, LANES), F32)],
    )(recv, *gains, *gains_m, *gains_v)


def _local_step(x, positions, gains, w_pad, small_shards, small_weights, target):
    gpre, gq, gkv, gpost = gains
    q_tab, k_tab, d_tab = _rope_tables(positions)

    proj, ht, gathered = _inproj(x, gpre, w_pad, d_tab, small_shards)
    w_uq, w_uk, w_uv, wp_mla, wp_dil, w_out = _assemble_small(*gathered) if small_shards else small_weights
    q, k, v, qt, kt, vt = _mla_prep(proj, gq, gkv, w_uq, w_uk, w_uv, q_tab, k_tab)
    o_mla, ot_mla, lse_mla = _mla_fwd(q, k, vt)

    od, lsed = [], []
    for g, d in enumerate(DIL_DILATIONS):
        o_g, lse_g = _dil_fwd(proj, g, d, "dil_fwd_%d" % g)
        od.append(o_g)
        lsed.append(lse_g)

    merged, ya, yd, o_dil, lse_dil = _merge_fwd(proj, o_mla, od, lsed, wp_mla, wp_dil)
    do, dy, loss, dgpost = _out_loss(merged, w_out, x, target, gpost)

    (dz_mla, dz_dil, dg_mla, dg_dil, do_mla, do_dil, dot_mla, dd_dil, dw_out, dwp_mla, dwp_dil) = _merge_bwd(
        do, w_out, merged, proj, ya, yd, o_mla, o_dil, wp_mla, wp_dil)

    dq, dk, dv = _mla_bwd(q, qt, k, kt, v, do_mla, dot_mla, ot_mla, lse_mla)
    dcq, dkr, dckv, dw_uq, dw_uk, dw_uv, dgq, dgkv = _mla_prep_bwd(dq, dk, dv, proj, gq, gkv, w_uq, w_uk, w_uv,
                                                                   q_tab, k_tab)

    dqs, dks, dvs = [], [], []
    for g, d in enumerate(DIL_DILATIONS):
        dq_g, dk_g, dv_g = _dil_bwd(proj, g, do_dil, lse_dil, dd_dil, d_tab, d, "dil_bwd_%d" % g)
        dqs.append(dq_g)
        dks.append(dk_g)
        dvs.append(dv_g)

    dproj = jnp.concatenate([dz_mla, dz_dil, dg_mla, dg_dil] + dqs + dks + dvs + [dcq, dkr, dckv], axis=1)
    small = _small_chunks(dw_uq, dw_uk, dw_uv, dwp_mla, dwp_dil, dw_out)
    dw_in, small = _dw_in(ht, dproj, _pair_stage(small, 1, "small")) if small_shards else (_dw_in(ht, dproj, [])[0], small)
    return loss, (dproj, dy), (dgq, dgkv, dgpost), dw_in, small


def _pair_stage(chunks, first, name):
    from_sibling = _pair_exchange(chunks, "pair_exchange_" + name)
    core = lax.axis_index("c").astype(jnp.int32).reshape(1)
    return [_pair_sum(core, chunks[t], from_sibling[t], "pair_sum_%d" % (first + t), PAIR_ROWS[first + t])
            for t in range(len(chunks))]


ADAM_ROWS = (256, 384, 256, 512, 512, 128)
PAIR_ROWS = (512, 384, 256, 512, 512, 128)


def kernel(x, positions, pre_norm_g, w_in, q_norm_g, w_uq, kv_norm_g, w_ukv, w_proj_mla, w_proj_dil, w_out, post_norm_g, loss_target, m_pre_norm_g, m_w_in, m_q_norm_g, m_w_uq, m_kv_norm_g, m_w_ukv, m_w_proj_mla, m_w_proj_dil, m_w_out, m_post_norm_g, v_pre_norm_g, v_w_in, v_q_norm_g, v_w_uq, v_kv_norm_g, v_w_ukv, v_w_proj_mla, v_w_proj_dil, v_w_out, v_post_norm_g):
    big_w = (w_in, w_uq, w_ukv, w_proj_mla, w_proj_dil, w_out)
    big_m = (m_w_in, m_w_uq, m_w_ukv, m_w_proj_mla, m_w_proj_dil, m_w_out)
    big_v = (v_w_in, v_w_uq, v_w_ukv, v_w_proj_mla, v_w_proj_dil, v_w_out)
    gains = (pre_norm_g, q_norm_g, kv_norm_g, post_norm_g)
    gains_m = (m_pre_norm_g, m_q_norm_g, m_kv_norm_g, m_post_norm_g)
    gains_v = (v_pre_norm_g, v_q_norm_g, v_kv_norm_g, v_post_norm_g)

    shards = [w[0].astype(BF16) for w in big_w]
    w_pad = _assemble_w_in(_gather_weights(shards[:1])[0])

    loss, (dproj, dy), (dgq, dgkv, dgpost), dw_in, received_small = _local_step(
        x[0], positions[0], gains, w_pad, shards[1:], None, loss_target[0])

    grad_x, dgpre, received_in = _dh_bwd(dproj, w_pad, x[0], pre_norm_g, dy,
                                         _pair_stage([_dw_in_chunks(dw_in)], 0, "in"))
    received = list(received_in) + list(received_small)
    packet = _packet_exchange(jnp.concatenate([dgpre, dgq, dgkv, dgpost, loss[0:1]], axis=1))

    big = [_adam(received[t], big_w[t], big_m[t], big_v[t], "adam_%d" % t, ADAM_ROWS[t]) for t in range(6)]
    small = _adam_gains(packet.reshape(N_DEV, PACKET), gains, gains_m, gains_v)

    def interleave(kind):
        s_pre, s_q, s_kv, s_post = small[4 * kind:4 * kind + 4]
        b_in, b_uq, b_ukv, b_pm, b_pd, b_out = (big[t][kind] for t in range(6))
        return [s_pre, b_in, s_q, b_uq, s_kv, b_ukv, b_pm, b_pd, b_out, s_post]

    return (small[16][0, 0], grad_x[None], *interleave(0), *interleave(1), *interleave(2), *interleave(3))
```

```python
import numpy as np
import jax
import jax.numpy as jnp
from jax import lax
from jax.experimental import pallas as pl
from jax.experimental.pallas import tpu as pltpu

F32 = jnp.float32
BF16 = jnp.bfloat16

D_MODEL = 1024
NORM_EPS = 1e-6
ROPE_THETA = 500000.0
N_DEV = 8
LANES = 128
NEG = -1e30

MLA_HEADS = 8
MLA_Q_RANK = 384
MLA_KV_RANK = 256
MLA_SCALE = 96.0 ** -0.5
LOG2E = 1.4426950408889634
MLA_QSCALE = MLA_SCALE * LOG2E
MLA_FWD_TK, MLA_FWD_Q_PER_K = 256, 2
MLA_BWD_Q_PER_K = 2
DIL_DILATIONS = (1, 4, 16)
DIL_SCALE = 0.125
Q_BLOCK = 128
DIL_BLOCKS_PER_STEP = 4

Z_MLA, Z_DIL, G_MLA, G_DIL = 0, 512, 1024, 2048
Q_OFF, K_OFF, V_OFF = 3072, 4608, 6144
CQ_OFF, KR_OFF, CKV_OFF, IN_PAD = 7680, 8064, 8192, 8448
IN_WIDTH = 8352
SHARD_W = IN_WIDTH // 8
IN_SEGS = ((0, 384, CQ_OFF), (384, 256, CKV_OFF), (640, 32, KR_OFF), (672, 1536, Q_OFF), (2208, 1536, K_OFF),
           (3744, 1536, V_OFF), (5280, 512, Z_MLA), (5792, 512, Z_DIL), (6304, 1024, G_MLA), (7328, 1024, G_DIL))

GAIN_OFFS = (0, 1024, 1408, 1664)
GAIN_WIDTHS = (1024, 384, 256, 1024)
LOSS_OFF, PACKET = 2688, 2816

ADAM_LR, ADAM_B1, ADAM_B2, ADAM_EPS, ADAM_WD, ADAM_STEP = 0.001, 0.9, 0.999, 1e-08, 0.01, 10

VMEM_LIMIT_MB = 56


def _cparams(sem=None, vmem_mb=VMEM_LIMIT_MB):
    return pltpu.CompilerParams(dimension_semantics=sem, vmem_limit_bytes=vmem_mb * 1024 * 1024)


def _dot(a, b):
    return jnp.dot(a, b, preferred_element_type=F32)


def _dot_nt(a, b):
    return lax.dot_general(a, b, (((1,), (1,)), ((), ())), preferred_element_type=F32)


def _dot_tn(a, b):
    return lax.dot_general(a, b, (((0,), (0,)), ((), ())), preferred_element_type=F32)


def _tile_lanes(t, width):
    return t if width == t.shape[1] else jnp.tile(t, (1, width // t.shape[1]))


def _rope(x, c, sp, sm, a):
    n = x.shape[1]
    return x * c + pltpu.roll(x, a, 1) * sp + pltpu.roll(x, n - a, 1) * sm


def _unrope(dy, c, sp, sm, a):
    n = dy.shape[1]
    return dy * c + pltpu.roll(dy * sp, n - a, 1) + pltpu.roll(dy * sm, a, 1)


def _sigmoid(z):
    return 1.0 / (1.0 + jnp.exp(-z))


def _left_mask():
    return lax.broadcasted_iota(jnp.int32, (1, LANES), 1) < 64


def _expand_half(x, hh, left):
    r = pltpu.roll(x, 64, 1)
    return jnp.where(left, x, r) if hh == 0 else jnp.where(left, r, x)


def _rms(xv, g):
    r = lax.rsqrt(jnp.mean(xv * xv, axis=-1, keepdims=True) + NORM_EPS)
    xh = xv * r
    return xh * g, xh, r


def _rms_bwd(dout, g, xh, r):
    dxh = dout * g
    return r * (dxh - xh * jnp.mean(dxh * xh, axis=-1, keepdims=True))


def _full(shape, index_map):
    return pl.BlockSpec(shape, index_map)


def _w_in_pieces():
    out = []
    for s, n, off in sorted(IN_SEGS, key=lambda t: t[2]):
        c = s
        while c < s + n:
            k = c // SHARD_W
            e = min(s + n, (k + 1) * SHARD_W)
            out.append((k, c - k * SHARD_W, e - c, off + (c - s)))
            c = e
    return out


def _assemble_w_in(g):
    parts, cur = [], 0
    for k, a, w, off in _w_in_pieces():
        if off > cur:
            parts.append(jnp.zeros((D_MODEL, off - cur), g.dtype))
        parts.append(g[k, :, a:a + w])
        cur = off + w
    if cur < IN_PAD:
        parts.append(jnp.zeros((D_MODEL, IN_PAD - cur), g.dtype))
    return jnp.concatenate(parts, axis=1)


def _dw_in_chunks(dw):
    chunks = []
    for dev in range(N_DEV):
        mine = sorted((p for p in _w_in_pieces() if p[0] == dev), key=lambda p: p[1])
        chunks.append(jnp.concatenate([dw[:, off:off + w] for k, a, w, off in mine], axis=1))
    return jnp.stack(chunks)


def _assemble_small(g_uq, g_ukv, g_pm, g_pd, g_out):
    w_uq_pad = jnp.pad(g_uq.transpose(1, 0, 2), ((0, 0), (0, 0), (0, 32))).reshape(384, 1024)
    ukv = g_ukv.transpose(1, 0, 2)
    w_uk_pad = jnp.pad(ukv[:, :, :64], ((0, 0), (0, 0), (0, 64))).reshape(256, 1024)
    w_uv = ukv[:, :, 64:].reshape(256, 512)
    wp_mla = g_pm.transpose(1, 0, 2).reshape(512, 1024)
    wp_dil = g_pd.transpose(1, 0, 2).reshape(512, 1024)
    return w_uq_pad, w_uk_pad, w_uv, wp_mla, wp_dil, g_out.reshape(1024, 1024)


def _small_chunks(dw_uq_pad, dw_uk_pad, dw_uv, dwp_mla, dwp_dil, dw_out):
    b = dw_uq_pad.reshape(384, 8, 128)[:, :, :96].transpose(1, 0, 2)
    c = jnp.concatenate([dw_uk_pad.reshape(256, 8, 128)[:, :, :64], dw_uv.reshape(256, 8, 64)], axis=2)
    c = c.transpose(1, 0, 2)
    d = dwp_mla.reshape(512, N_DEV, 128).transpose(1, 0, 2)
    e = dwp_dil.reshape(512, N_DEV, 128).transpose(1, 0, 2)
    f = dw_out.reshape(N_DEV, 128, 1024)
    return [t.astype(BF16) for t in (b, c, d, e, f)]


def _lane_consts(freqs, half, first, period):
    rel = (np.arange(LANES) % period) - first
    rot = (rel >= 0) & (rel < 2 * half)
    freq = np.where(rot, freqs[np.clip(rel, 0, 2 * half - 1) % half], 0.0).astype(np.float32)
    x1 = (rot & (rel < half)).astype(np.float32)
    x2 = (rot & (rel >= half)).astype(np.float32)
    return freq[None, :], x1[None, :], x2[None, :]


def _rope_tables(pos):
    p = pos.astype(F32)[:, None]
    inv_m = np.float32(ROPE_THETA) ** (-(np.arange(0, 32, 2, dtype=np.float32) / np.float32(32)))
    inv_d = np.float32(ROPE_THETA) ** (-(np.arange(0, 16, 2, dtype=np.float32) / np.float32(16)))
    lane = np.arange(LANES)
    tabs = []
    for freqs, half, first, period, keep in ((inv_m, 16, 64, 128, lane < 96), (inv_m, 16, 0, 128, lane < 32),
                                              (inv_d, 8, 0, 64, lane >= 0)):
        freq, x1, x2 = _lane_consts(freqs, half, first, period)
        ang = p * freq
        sin = jnp.sin(ang)
        tabs.append((jnp.cos(ang) * keep.astype(np.float32)[None, :], sin * x2, sin * (-x1)))
    return tuple(tabs)


def _inproj(x, gpre, w_pad, d_tab, shards):
    s = x.shape[0]
    tm, tn = min(1024, s), 768
    ni, nj = s // tm, IN_PAD // tn
    rope_lo, rope_hi = Q_OFF // tn, V_OFF // tn
    n = len(shards)
    forward_step = min(nj, ni * nj - 2)

    def body(*refs):
        x_ref, g_ref, w_ref, c_ref, sp_ref, sm_ref = refs[:6]
        ins, o_ref, ht_ref, outs = refs[6:6 + n], refs[6 + n], refs[7 + n], refs[8 + n:8 + 2 * n]
        h_ref, sems = refs[8 + 2 * n], refs[9 + 2 * n]
        i, j = pl.program_id(0), pl.program_id(1)
        step = i * nj + j

        @pl.when(j == 0)
        def _():
            hv, _, _ = _rms(x_ref[...], g_ref[...])
            h_ref[...] = hv.astype(BF16)
            ht_ref[...] = hv.astype(BF16).T

        if n:
            @pl.when(step == 0)
            def _():
                _gather_start(ins, outs, sems)

            @pl.when(step == forward_step)
            def _():
                _gather_forward(ins, outs, sems)

        acc = _dot(h_ref[...], w_ref[...])
        is_rope = jnp.logical_and(j >= rope_lo, j < rope_hi)

        @pl.when(is_rope)
        def _():
            o_ref[...] = _rope(acc, _tile_lanes(c_ref[...], tn), _tile_lanes(sp_ref[...], tn),
                               _tile_lanes(sm_ref[...], tn), 8).astype(BF16)

        @pl.when(jnp.logical_not(is_rope))
        def _():
            o_ref[...] = acc.astype(BF16)

        if n:
            @pl.when(step == ni * nj - 1)
            def _():
                _gather_finish(ins, outs, sems)

    row = lambda i, j: (i, 0)
    hbm = pl.BlockSpec(memory_space=pl.ANY)
    res = pl.pallas_call(
        body, name="inproj", grid=(ni, nj),
        in_specs=[_full((tm, D_MODEL), row), _full((1, D_MODEL), lambda i, j: (0, 0)),
                  _full((D_MODEL, tn), lambda i, j: (0, j)),
                  _full((tm, LANES), row), _full((tm, LANES), row), _full((tm, LANES), row)] + [hbm] * n,
        out_specs=[_full((tm, tn), lambda i, j: (i, j)), _full((D_MODEL, tm), lambda i, j: (0, i))] + [hbm] * n,
        out_shape=[jax.ShapeDtypeStruct((s, IN_PAD), BF16), jax.ShapeDtypeStruct((D_MODEL, s), BF16)]
        + [jax.ShapeDtypeStruct((N_DEV,) + a.shape, a.dtype) for a in shards],
        scratch_shapes=[pltpu.VMEM((tm, D_MODEL), BF16), pltpu.SemaphoreType.DMA((max(2 * n, 2), N_DEV))],
        compiler_params=_cparams(("arbitrary", "arbitrary")),
    )(x, gpre, w_pad, *d_tab, *shards)
    return res[0], res[1], res[2:]


def _mla_prep(proj, gq, gkv, w_uq, w_uk, w_uv, q_tab, k_tab):
    s = proj.shape[0]
    tm = min(512, s)

    def body(cq_ref, kr_ref, ckv_ref, gq_ref, gkv_ref, wq_ref, wk_ref, wv_ref,
             qc, qsp, qsm, kc, ksp, ksm, q_out, k_out, v_out, qt_out, kt_out, vt_out):
        cqn, _, _ = _rms(cq_ref[...].astype(F32), gq_ref[...])
        q = _dot(cqn.astype(BF16), wq_ref[...])
        q = _rope(q, _tile_lanes(qc[...], 1024), _tile_lanes(qsp[...], 1024), _tile_lanes(qsm[...], 1024), 16)
        q = q * MLA_QSCALE
        q_out[...] = q.astype(BF16)
        qt_out[...] = q.T.astype(BF16)
        ckvn, _, _ = _rms(ckv_ref[...].astype(F32), gkv_ref[...])
        ckvn = ckvn.astype(BF16)
        kr = _rope(kr_ref[...].astype(F32), kc[...], ksp[...], ksm[...], 16)
        k = _dot(ckvn, wk_ref[...]) + _tile_lanes(pltpu.roll(kr, 64, 1), 1024)
        k_out[...] = k.astype(BF16)
        kt_out[...] = k.T.astype(BF16)
        v = _dot(ckvn, wv_ref[...])
        v_out[...] = v.astype(BF16)
        vt_out[...] = v.T.astype(BF16)

    row = lambda i: (i, 0)
    col = lambda i: (0, i)
    cst = lambda i: (0, 0)
    tabs = [_full((tm, LANES), row)] * 6
    return pl.pallas_call(
        body, name="mla_prep", grid=(s // tm,),
        in_specs=[_full((tm, 384), lambda i: (i, CQ_OFF // 384)), _full((tm, 128), lambda i: (i, KR_OFF // 128)),
                  _full((tm, 256), lambda i: (i, CKV_OFF // 256)), _full((1, 384), cst), _full((1, 256), cst),
                  _full((384, 1024), cst), _full((256, 1024), cst), _full((256, 512), cst)] + tabs,
        out_specs=[_full((tm, 1024), row), _full((tm, 1024), row), _full((tm, 512), row),
                   _full((1024, tm), col), _full((1024, tm), col), _full((512, tm), col)],
        out_shape=[jax.ShapeDtypeStruct((s, 1024), BF16), jax.ShapeDtypeStruct((s, 1024), BF16),
                   jax.ShapeDtypeStruct((s, 512), BF16), jax.ShapeDtypeStruct((1024, s), BF16),
                   jax.ShapeDtypeStruct((1024, s), BF16), jax.ShapeDtypeStruct((512, s), BF16)],
        compiler_params=_cparams(("parallel",)),
    )(proj, proj, proj, gq, gkv, w_uq, w_uk, w_uv, *q_tab, *k_tab)


def _mla_fwd(q, k, vt):
    s = q.shape[0]
    tk = min(MLA_FWD_TK, s)
    ratio = MLA_FWD_Q_PER_K if s >= MLA_FWD_Q_PER_K * tk else 1
    tq = ratio * tk
    nq = s // tq

    def body(q_ref, k_ref, vt_ref, o_ref, ot_ref, lse_ref):
        krow = lax.broadcasted_iota(jnp.int32, (tk, tq), 0)
        qcol = lax.broadcasted_iota(jnp.int32, (tk, tq), 1)

        def q_step(i, _):
            r0 = pl.multiple_of(i * tq, tq)
            qs = [q_ref[pl.ds(r0, tq), hh * 128:(hh + 1) * 128] for hh in range(2)]

            def scores(j):
                c0 = pl.multiple_of(j * tk, tk)
                return tuple(_dot_nt(k_ref[pl.ds(c0, tk), hh * 128:(hh + 1) * 128], qs[hh])
                             for hh in range(2))

            def update(j, sts, stats, masked):
                c0 = pl.multiple_of(j * tk, tk)
                new = []
                causal = (krow + (c0 - r0)) <= qcol
                for hh in range(2):
                    m, l, acc = stats[hh]
                    st = jnp.where(causal, sts[hh], NEG) if masked else sts[hh]
                    m_new = jnp.maximum(m, jnp.max(st, axis=0, keepdims=True))
                    alpha = jnp.exp2(m - m_new)
                    p = jnp.exp2(st - m_new)
                    l = alpha * l + jnp.sum(p, axis=0, keepdims=True)
                    acc = acc * alpha + _dot(vt_ref[hh * 64:(hh + 1) * 64, pl.ds(c0, tk)], p.astype(BF16))
                    new.append((m_new, l, acc))
                return tuple(new)

            init = tuple((jnp.full((1, tq), NEG, F32), jnp.zeros((1, tq), F32), jnp.zeros((64, tq), F32))
                         for _ in range(2))
            stats = lax.fori_loop(0, ratio * i, lambda j, st: update(j, scores(j), st, False), init)
            for d in range(ratio):
                stats = update(ratio * i + d, scores(ratio * i + d), stats, True)
            (ma, la, acca), (mb, lb, accb) = stats
            ot = jnp.concatenate([acca / la, accb / lb], axis=0)
            ot_ref[:, pl.ds(r0, tq)] = ot.astype(BF16)
            o_ref[pl.ds(r0, tq), :] = ot.T.astype(BF16)
            lse_ref[:, pl.ds(r0, tq)] = jnp.concatenate(
                [ma + jnp.log2(la), mb + jnp.log2(lb), jnp.zeros((6, tq), F32)], axis=0)
            return 0

        lax.fori_loop(0, nq, q_step, 0)

    return pl.pallas_call(
        body, name="mla_fwd", grid=(4,),
        in_specs=[_full((s, 256), lambda p: (0, p)), _full((s, 256), lambda p: (0, p)),
                  _full((128, s), lambda p: (p, 0))],
        out_specs=[_full((s, 128), lambda p: (0, p)), _full((128, s), lambda p: (p, 0)),
                   _full((8, s), lambda p: (p, 0))],
        out_shape=[jax.ShapeDtypeStruct((s, 512), BF16), jax.ShapeDtypeStruct((512, s), BF16),
                   jax.ShapeDtypeStruct((32, s), F32)],
        compiler_params=_cparams(("parallel",)),
    )(q, k, vt)


def _band_mask(has_prev):
    r = lax.broadcasted_iota(jnp.int32, (Q_BLOCK, 2 * Q_BLOCK), 0)
    c = lax.broadcasted_iota(jnp.int32, (Q_BLOCK, 2 * Q_BLOCK), 1)
    lo = jnp.where(has_prev, r, Q_BLOCK)
    return jnp.logical_and(c >= lo, c <= r + Q_BLOCK)


def _dil_rows(b, d, per_seq):
    r, n = b // per_seq, b % per_seq
    start = r + (d * Q_BLOCK) * n
    prev = start - jnp.where(n > 0, d * Q_BLOCK, 0)
    if d == 1:
        return pl.ds(pl.multiple_of(start, Q_BLOCK), Q_BLOCK), pl.ds(pl.multiple_of(prev, Q_BLOCK), Q_BLOCK)
    return pl.ds(start, Q_BLOCK, stride=d), pl.ds(prev, Q_BLOCK, stride=d)


def _dil_fwd(proj, g, d, name):
    s = proj.shape[0]
    nblk = s // Q_BLOCK
    per_seq = nblk // d

    def body(q_ref, k_ref, v_ref, o_ref, lse_ref, qf, kf, vf, of, lf):
        left = _left_mask()
        hms = (left, jnp.logical_not(left))
        qf[...] = q_ref[...].astype(F32)
        kf[...] = k_ref[...].astype(F32)
        vf[...] = v_ref[...].astype(F32)

        def scores(b):
            rows, prow = _dil_rows(b, d, per_seq)
            qb = qf[rows, :].astype(BF16)
            kk = jnp.concatenate([kf[prow, :], kf[rows, :]], axis=0).astype(BF16)
            return [_dot_nt(jnp.where(hms[hh], qb, jnp.zeros_like(qb)), kk) * DIL_SCALE for hh in range(2)]

        def finish(b, tiles):
            rows, prow = _dil_rows(b, d, per_seq)
            mask = _band_mask((b % per_seq) > 0)
            vv = jnp.concatenate([vf[prow, :], vf[rows, :]], axis=0).astype(BF16)
            outs = []
            for hh in range(2):
                sc = jnp.where(mask, tiles[hh], NEG)
                m = jnp.max(sc, axis=1, keepdims=True)
                p = jnp.exp(sc - m)
                l = jnp.sum(p, axis=1, keepdims=True)
                acc = _dot(p.astype(BF16), jnp.where(hms[hh], vv, jnp.zeros_like(vv)))
                outs.append((acc / l, jnp.broadcast_to(m + jnp.log(l), (Q_BLOCK, LANES))))
            of[rows, :] = outs[0][0] + outs[1][0]
            lf[rows, :] = jnp.where(left, outs[0][1], outs[1][1])

        def step(t, _):
            tiles = [scores(DIL_BLOCKS_PER_STEP * t + u) for u in range(DIL_BLOCKS_PER_STEP)]
            for u in range(DIL_BLOCKS_PER_STEP):
                finish(DIL_BLOCKS_PER_STEP * t + u, tiles[u])
            return 0

        lax.fori_loop(0, nblk // DIL_BLOCKS_PER_STEP, step, 0)
        o_ref[...] = of[...].astype(BF16)
        lse_ref[...] = lf[...]

    blk = lambda off: _full((s, 128), lambda p, off=off: (0, off + p))
    qo, ko, vo = ((off + 512 * g) // 128 for off in (Q_OFF, K_OFF, V_OFF))
    return pl.pallas_call(
        body, name=name, grid=(4,),
        in_specs=[blk(qo), blk(ko), blk(vo)],
        out_specs=[blk(0), blk(0)],
        out_shape=[jax.ShapeDtypeStruct((s, 512), BF16), jax.ShapeDtypeStruct((s, 512), F32)],
        scratch_shapes=[pltpu.VMEM((s, 128), F32)] * 5,
        compiler_params=_cparams(("parallel",)),
    )(proj, proj, proj)


def _merge_fwd(proj, o_mla, od, lsed, wp_mla, wp_dil):
    s = proj.shape[0]
    tm = min(512, s)

    def body(zm_ref, zd_ref, gm_ref, gd_ref, om_ref, o0, o1, o2, l0, l1, l2, wm_ref, wd_ref,
             mg_out, ya_out, yd_out, odil_out, lse_out):
        la, lb, lc = l0[...], l1[...], l2[...]
        lmax = jnp.maximum(jnp.maximum(la, lb), lc)
        ea, eb, ec = jnp.exp(la - lmax), jnp.exp(lb - lmax), jnp.exp(lc - lmax)
        den = ea + eb + ec
        o_dil = (ea * o0[...].astype(F32) + eb * o1[...].astype(F32) + ec * o2[...].astype(F32)) / den
        o_dil = o_dil.astype(BF16)
        odil_out[...] = o_dil
        lse_out[...] = lmax + jnp.log(den)
        zm, zd = zm_ref[...].astype(F32), zd_ref[...].astype(F32)
        pa = (om_ref[...].astype(F32) * (zm * _sigmoid(zm))).astype(BF16)
        pd = (o_dil.astype(F32) * (zd * _sigmoid(zd))).astype(BF16)
        ya = _dot(pa, wm_ref[...])
        yd = _dot(pd, wd_ref[...])
        ya_out[...] = ya.astype(BF16)
        yd_out[...] = yd.astype(BF16)
        mg_out[...] = (_sigmoid(gm_ref[...].astype(F32)) * ya + _sigmoid(gd_ref[...].astype(F32)) * yd).astype(BF16)

    row = lambda i: (i, 0)
    cst = lambda i: (0, 0)
    r512 = _full((tm, 512), row)
    r1024 = _full((tm, 1024), row)
    return pl.pallas_call(
        body, name="merge_fwd", grid=(s // tm,),
        in_specs=[_full((tm, 512), lambda i: (i, Z_MLA // 512)), _full((tm, 512), lambda i: (i, Z_DIL // 512)),
                  _full((tm, 1024), lambda i: (i, G_MLA // 1024)), _full((tm, 1024), lambda i: (i, G_DIL // 1024)),
                  r512, r512, r512, r512, r512, r512, r512, _full((512, 1024), cst), _full((512, 1024), cst)],
        out_specs=[r1024, r1024, r1024, r512, r512],
        out_shape=[jax.ShapeDtypeStruct((s, 1024), BF16), jax.ShapeDtypeStruct((s, 1024), BF16),
                   jax.ShapeDtypeStruct((s, 1024), BF16), jax.ShapeDtypeStruct((s, 512), BF16),
                   jax.ShapeDtypeStruct((s, 512), F32)],
        compiler_params=_cparams(("parallel",)),
    )(proj, proj, proj, proj, o_mla, *od, *lsed, wp_mla, wp_dil)


def _out_loss(merged, w_out, x, target, gpost):
    s = x.shape[0]
    tm = min(512, s)

    def body(mg_ref, w_ref, x_ref, t_ref, g_ref, do_out, dy_out, loss_out, dg_out):
        i = pl.program_id(0)

        @pl.when(i == 0)
        def _():
            loss_out[...] = jnp.zeros_like(loss_out)
            dg_out[...] = jnp.zeros_like(dg_out)

        o = _dot(mg_ref[...], w_ref[...])
        g = g_ref[...]
        n, u, r = _rms(o, g)
        e = (x_ref[...] + n) - t_ref[...]
        loss_out[...] += 0.5 * jnp.sum(jnp.mean(e * e, axis=-1, keepdims=True))
        dy = e * (1.0 / D_MODEL)
        dy_out[...] = dy
        dg_out[...] += jnp.sum(dy * u, axis=0, keepdims=True)
        do_out[...] = _rms_bwd(dy, g, u, r).astype(BF16)

    row = lambda i: (i, 0)
    cst = lambda i: (0, 0)
    return pl.pallas_call(
        body, name="out_loss", grid=(s // tm,),
        in_specs=[_full((tm, 1024), row), _full((1024, 1024), cst), _full((tm, 1024), row), _full((tm, 1024), row),
                  _full((1, 1024), cst)],
        out_specs=[_full((tm, 1024), row), _full((tm, 1024), row), _full((8, LANES), cst), _full((1, 1024), cst)],
        out_shape=[jax.ShapeDtypeStruct((s, 1024), BF16), jax.ShapeDtypeStruct((s, 1024), F32),
                   jax.ShapeDtypeStruct((8, LANES), F32), jax.ShapeDtypeStruct((1, 1024), F32)],
        compiler_params=_cparams(("arbitrary",)),
    )(merged, w_out, x, target, gpost)


def _seg_sum64(x, ones_bd):
    hi = x.astype(BF16)
    lo = (x - hi.astype(F32)).astype(BF16)
    return _dot(hi, ones_bd) + _dot(lo, ones_bd)


def _merge_bwd(do, w_out, merged, proj, ya, yd, o_mla, o_dil, wp_mla, wp_dil):
    s = do.shape[0]
    tm = min(256, s)
    seg = jnp.arange(512) // 64
    ones_bd = (seg[:, None] == seg[None, :]).astype(BF16)

    def body(do_ref, wo_ref, mg_ref, zm_ref, zd_ref, gm_ref, gd_ref, ya_ref, yd_ref, om_ref, od_ref, wm_ref, wd_ref,
             bd_ref, dp_out, dom_out, dod_out, domt_out, dd_out, dwo_out, dwm_out, dwd_out):
        i = pl.program_id(0)

        @pl.when(i == 0)
        def _():
            dwo_out[...] = jnp.zeros_like(dwo_out)
            dwm_out[...] = jnp.zeros_like(dwm_out)
            dwd_out[...] = jnp.zeros_like(dwd_out)

        dov = do_ref[...]
        dwo_out[...] += _dot_tn(mg_ref[...], dov)
        dm = _dot_nt(dov, wo_ref[...])
        for g_ref, y_ref, z_ref, o_ref, w_ref, z_off, g_off, dob_out, dd_o, dw_out in (
                (gm_ref, ya_ref, zm_ref, om_ref, wm_ref, Z_MLA, G_MLA, dom_out, None, dwm_out),
                (gd_ref, yd_ref, zd_ref, od_ref, wd_ref, Z_DIL, G_DIL, dod_out, dd_out, dwd_out)):
            sg = _sigmoid(g_ref[...].astype(F32))
            dp_out[:, g_off:g_off + 1024] = (dm * y_ref[...].astype(F32) * sg * (1.0 - sg)).astype(BF16)
            dy = (dm * sg).astype(BF16)
            z = z_ref[...].astype(F32)
            sz = _sigmoid(z)
            silu = z * sz
            ob = o_ref[...].astype(F32)
            dw_out[...] += _dot_tn((ob * silu).astype(BF16), dy)
            dp = _dot_nt(dy, w_ref[...])
            dp_out[:, z_off:z_off + 512] = (dp * ob * (sz * (1.0 + z * (1.0 - sz)))).astype(BF16)
            dob = dp * silu
            dob_out[...] = dob.astype(BF16)
            if dd_o is None:
                domt_out[...] = dob.T.astype(BF16)
            else:
                dd_o[...] = _seg_sum64(dob * ob, bd_ref[...])

    row = lambda i: (i, 0)
    cst = lambda i: (0, 0)
    r512 = _full((tm, 512), row)
    r1024 = _full((tm, 1024), row)
    return pl.pallas_call(
        body, name="merge_bwd", grid=(s // tm,),
        in_specs=[r1024, _full((1024, 1024), cst), r1024,
                  _full((tm, 512), lambda i: (i, Z_MLA // 512)), _full((tm, 512), lambda i: (i, Z_DIL // 512)),
                  _full((tm, 1024), lambda i: (i, G_MLA // 1024)), _full((tm, 1024), lambda i: (i, G_DIL // 1024)),
                  r1024, r1024, r512, r512, _full((512, 1024), cst), _full((512, 1024), cst), _full((512, 512), cst)],
        out_specs=[_full((tm, Q_OFF), row), r512, r512, _full((512, tm), lambda i: (0, i)), r512,
                   _full((1024, 1024), cst), _full((512, 1024), cst), _full((512, 1024), cst)],
        out_shape=[jax.ShapeDtypeStruct((s, IN_PAD), BF16),
                   jax.ShapeDtypeStruct((s, 512), BF16), jax.ShapeDtypeStruct((s, 512), BF16),
                   jax.ShapeDtypeStruct((512, s), BF16), jax.ShapeDtypeStruct((s, 512), F32),
                   jax.ShapeDtypeStruct((1024, 1024), F32), jax.ShapeDtypeStruct((512, 1024), F32),
                   jax.ShapeDtypeStruct((512, 1024), F32)],
        compiler_params=_cparams(("arbitrary",)),
    )(do, w_out, merged, proj, proj, proj, proj, ya, yd, o_mla, o_dil, wp_mla, wp_dil, ones_bd)


def _mla_bwd(q, qt, k, kt, v, do, dot, ot, lse):
    s = q.shape[0]
    tk = min(256, s)
    ratio = MLA_BWD_Q_PER_K if s >= MLA_BWD_Q_PER_K * tk else 1
    tq = ratio * tk
    nq, nk = s // tq, s // tk

    def body(q_ref, qt_ref, k_ref, kt_ref, v_ref, do_ref, dot_ref, ot_ref, lse_ref, dqt_out, dkt_out, dvt_out,
             dqt_acc):
        left = _left_mask()
        krow = lax.broadcasted_iota(jnp.int32, (tk, tq), 0)
        qcol = lax.broadcasted_iota(jnp.int32, (tk, tq), 1)
        dqt_acc[...] = jnp.zeros_like(dqt_acc)

        def kv_step(j, _):
            c0 = pl.multiple_of(j * tk, tk)
            vv = v_ref[pl.ds(c0, tk), :]
            khs = [k_ref[pl.ds(c0, tk), hh * 128:(hh + 1) * 128] for hh in range(2)]
            kths = [kt_ref[hh * 128:(hh + 1) * 128, pl.ds(c0, tk)] for hh in range(2)]
            vms = [jnp.where(left if hh == 0 else jnp.logical_not(left), vv, jnp.zeros_like(vv)) for hh in range(2)]

            def scores(i):
                r0 = pl.multiple_of(jnp.minimum(i, nq - 1) * tq, tq)
                dov = do_ref[pl.ds(r0, tq), :]
                return tuple((_dot_nt(khs[hh], q_ref[pl.ds(r0, tq), hh * 128:(hh + 1) * 128]),
                              _dot_nt(vms[hh], dov)) for hh in range(2))

            def update(i, tiles, acc, masked):
                r0 = pl.multiple_of(i * tq, tq)
                new = []
                for hh in range(2):
                    dkt, dvt = acc[hh]
                    st, dp = tiles[hh]
                    hrows = slice(hh * 128, (hh + 1) * 128)
                    drows = slice(hh * 64, (hh + 1) * 64)
                    doth = dot_ref[drows, pl.ds(r0, tq)]
                    dd = jnp.sum(doth.astype(F32) * ot_ref[drows, pl.ds(r0, tq)].astype(F32), axis=0, keepdims=True)
                    p = jnp.exp2(st - lse_ref[hh:hh + 1, pl.ds(r0, tq)])
                    if masked:
                        p = jnp.where((krow + (c0 - r0)) <= qcol, p, 0.0)
                    ds = (p * (dp - dd)).astype(BF16)
                    dvt = dvt + _dot_nt(doth, p.astype(BF16))
                    dkt = dkt + _dot_nt(qt_ref[hrows, pl.ds(r0, tq)], ds)
                    dqt_acc[hrows, pl.ds(r0, tq)] += _dot(kths[hh], ds)
                    new.append((dkt, dvt))
                return tuple(new)

            init = tuple((jnp.zeros((128, tk), F32), jnp.zeros((64, tk), F32)) for _ in range(2))
            i0 = j // ratio
            acc = update(i0, scores(i0), init, True)
            acc = lax.fori_loop(i0 + 1, nq, lambda i, a: update(i, scores(i), a, False), acc)
            for hh in range(2):
                dkt_out[hh * 128:(hh + 1) * 128, pl.ds(c0, tk)] = (acc[hh][0] * (1.0 / LOG2E)).astype(BF16)
                dvt_out[hh * 64:(hh + 1) * 64, pl.ds(c0, tk)] = acc[hh][1].astype(BF16)
            return 0

        lax.fori_loop(0, nk, kv_step, 0)
        dqt_out[...] = (dqt_acc[...] * MLA_SCALE).astype(BF16)

    b256 = _full((s, 256), lambda p: (0, p))
    b128 = _full((s, 128), lambda p: (0, p))
    t256 = _full((256, s), lambda p: (p, 0))
    t128 = _full((128, s), lambda p: (p, 0))
    return pl.pallas_call(
        body, name="mla_bwd", grid=(4,),
        in_specs=[b256, t256, b256, t256, b128, b128, t128, t128, _full((8, s), lambda p: (p, 0))],
        out_specs=[t256, t256, t128],
        out_shape=[jax.ShapeDtypeStruct((1024, s), BF16), jax.ShapeDtypeStruct((1024, s), BF16),
                   jax.ShapeDtypeStruct((512, s), BF16)],
        scratch_shapes=[pltpu.VMEM((256, s), F32)],
        compiler_params=_cparams(("parallel",)),
    )(q, qt, k, kt, v, do, dot, ot, lse)


def _mla_prep_bwd(dq, dk, dv, proj, dproj, gq, gkv, w_uq, w_uk, w_uv, q_tab, k_tab):
    s = proj.shape[0]
    tm = min(256, s)

    def body(dqt_ref, dkt_ref, dvt_ref, cq_ref, ckv_ref, dp_in, gq_ref, gkv_ref, wq_ref, wk_ref, wv_ref,
             qc, qsp, qsm, kc, ksp, ksm,
             dp_out, dwq_out, dwk_out, dwv_out, dgq_out, dgkv_out):
        i = pl.program_id(0)

        @pl.when(i == 0)
        def _():
            for r in (dwq_out, dwk_out, dwv_out, dgq_out, dgkv_out):
                r[...] = jnp.zeros_like(r)

        dqu = _unrope(dqt_ref[...].astype(F32).T, _tile_lanes(qc[...], 1024), _tile_lanes(qsp[...], 1024),
                      _tile_lanes(qsm[...], 1024), 16).astype(BF16)
        gq = gq_ref[...]
        cqn, xh, r = _rms(cq_ref[...].astype(F32), gq)
        dwq_out[...] += _dot_tn(cqn.astype(BF16), dqu)
        dcqn = _dot_nt(dqu, wq_ref[...])
        dgq_out[...] += jnp.sum(dcqn * xh, axis=0, keepdims=True)
        dp_out[:, 0:384] = _rms_bwd(dcqn, gq, xh, r).astype(BF16)

        dkf = dkt_ref[...].astype(F32).T
        dkb = dkf.astype(BF16)
        dsum = dkf[:, 0:128]
        for h in range(1, MLA_HEADS):
            dsum = dsum + dkf[:, h * 128:(h + 1) * 128]
        dp_out[:, 384:512] = _unrope(pltpu.roll(dsum, 64, 1), kc[...], ksp[...], ksm[...], 16).astype(BF16)

        dvb = dvt_ref[...].astype(F32).T.astype(BF16)
        gkv = gkv_ref[...]
        ckvn, xh2, r2 = _rms(ckv_ref[...].astype(F32), gkv)
        ckvn = ckvn.astype(BF16)
        dwk_out[...] += _dot_tn(ckvn, dkb)
        dwv_out[...] += _dot_tn(ckvn, dvb)
        dckvn = _dot_nt(dkb, wk_ref[...]) + _dot_nt(dvb, wv_ref[...])
        dgkv_out[...] += jnp.sum(dckvn * xh2, axis=0, keepdims=True)
        dp_out[:, 512:768] = _rms_bwd(dckvn, gkv, xh2, r2).astype(BF16)

    row = lambda i: (i, 0)
    cst = lambda i: (0, 0)
    tabs = [_full((tm, LANES), row)] * 6
    return pl.pallas_call(
        body, name="mla_prep_bwd", grid=(s // tm,),
        in_specs=[_full((1024, tm), lambda i: (0, i)), _full((1024, tm), lambda i: (0, i)),
                  _full((512, tm), lambda i: (0, i)),
                  _full((tm, 384), lambda i: (i, CQ_OFF // 384)), _full((tm, 256), lambda i: (i, CKV_OFF // 256)),
                  pl.BlockSpec(memory_space=pl.ANY), _full((1, 384), cst), _full((1, 256), cst),
                  _full((384, 1024), cst), _full((256, 1024), cst), _full((256, 512), cst)] + tabs,
        out_specs=[_full((tm, IN_PAD - CQ_OFF), lambda i: (i, CQ_OFF // (IN_PAD - CQ_OFF))),
                   _full((384, 1024), cst), _full((256, 1024), cst), _full((256, 512), cst),
                   _full((1, 384), cst), _full((1, 256), cst)],
        out_shape=[jax.ShapeDtypeStruct((s, IN_PAD), BF16),
                   jax.ShapeDtypeStruct((384, 1024), F32), jax.ShapeDtypeStruct((256, 1024), F32),
                   jax.ShapeDtypeStruct((256, 512), F32),
                   jax.ShapeDtypeStruct((1, 384), F32), jax.ShapeDtypeStruct((1, 256), F32)],
        input_output_aliases={5: 0},
        compiler_params=_cparams(("arbitrary",)),
    )(dq, dk, dv, proj, proj, dproj, gq, gkv, w_uq, w_uk, w_uv, *q_tab, *k_tab)


def _dil_bwd(proj, dproj, g, do, lse, dd, tabs, d, name):
    s = proj.shape[0]
    nblk = s // Q_BLOCK
    per_seq = nblk // d

    def body(q_ref, k_ref, v_ref, do_ref, lse_ref, dd_ref, c_ref, sp_ref, sm_ref, dp_in, dp_out,
             qf, kf, vf, dof, dq_acc, dk_acc, dv_acc, staged, sems):
        left = _left_mask()
        hms = (left, jnp.logical_not(left))
        qf[...] = q_ref[...].astype(F32)
        kf[...] = k_ref[...].astype(F32)
        vf[...] = v_ref[...].astype(F32)
        dof[...] = do_ref[...].astype(F32)
        dk_acc[...] = jnp.zeros_like(dk_acc)
        dv_acc[...] = jnp.zeros_like(dv_acc)

        def scores(b):
            rows, prow = _dil_rows(b, d, per_seq)
            qb, dob = qf[rows, :].astype(BF16), dof[rows, :].astype(BF16)
            kk = jnp.concatenate([kf[prow, :], kf[rows, :]], axis=0).astype(BF16)
            vv = jnp.concatenate([vf[prow, :], vf[rows, :]], axis=0).astype(BF16)
            zero = jnp.zeros_like(qb)
            out = []
            for hh in range(2):
                qm, dom = jnp.where(hms[hh], qb, zero), jnp.where(hms[hh], dob, zero)
                out.append((_dot_nt(qm, kk) * DIL_SCALE, _dot_nt(dom, vv)))
            return out

        def finish(b, tiles):
            rows, prow = _dil_rows(b, d, per_seq)
            mask = _band_mask((b % per_seq) > 0)
            qb, dob = qf[rows, :].astype(BF16), dof[rows, :].astype(BF16)
            kk = jnp.concatenate([kf[prow, :], kf[rows, :]], axis=0).astype(BF16)
            lse_b, dd_b = lse_ref[rows, :], dd_ref[rows, :]
            zero = jnp.zeros_like(qb)
            dq = jnp.zeros((Q_BLOCK, LANES), F32)
            dk = jnp.zeros((2 * Q_BLOCK, LANES), F32)
            dv = jnp.zeros((2 * Q_BLOCK, LANES), F32)
            for hh in range(2):
                hm = hms[hh]
                qm, dom = jnp.where(hm, qb, zero), jnp.where(hm, dob, zero)
                lse_h = _tile_lanes(_expand_half(lse_b, hh, left), 2 * Q_BLOCK)
                dd_h = _tile_lanes(_expand_half(dd_b, hh, left), 2 * Q_BLOCK)
                sc, dp = tiles[hh]
                p = jnp.where(mask, jnp.exp(sc - lse_h), 0.0)
                ds = (p * (dp - dd_h) * DIL_SCALE).astype(BF16)
                dq = dq + _dot(ds, jnp.where(hm, kk, jnp.zeros_like(kk)))
                dk = dk + _dot_tn(ds, qm)
                dv = dv + _dot_tn(p.astype(BF16), dom)
            dq_acc[rows, :] = dq
            dk_acc[prow, :] += dk[0:Q_BLOCK]
            dv_acc[prow, :] += dv[0:Q_BLOCK]
            dk_acc[rows, :] += dk[Q_BLOCK:]
            dv_acc[rows, :] += dv[Q_BLOCK:]

        def step(t, _):
            tiles = [scores(DIL_BLOCKS_PER_STEP * t + u) for u in range(DIL_BLOCKS_PER_STEP)]
            for u in range(DIL_BLOCKS_PER_STEP):
                finish(DIL_BLOCKS_PER_STEP * t + u, tiles[u])
            return 0

        lax.fori_loop(0, nblk // DIL_BLOCKS_PER_STEP, step, 0)
        staged[0] = _unrope(dq_acc[...], c_ref[...], sp_ref[...], sm_ref[...], 8).astype(BF16)
        staged[1] = _unrope(dk_acc[...], c_ref[...], sp_ref[...], sm_ref[...], 8).astype(BF16)
        staged[2] = dv_acc[...].astype(BF16)
        pair = pl.program_id(0)
        copies = [pltpu.make_async_copy(
            staged.at[t], dp_out.at[:, pl.ds(pl.multiple_of(off + 512 * g + 128 * pair, 128), 128)], sems.at[t])
            for t, off in enumerate((Q_OFF, K_OFF, V_OFF))]
        for cp in copies:
            cp.start()
        for cp in copies:
            cp.wait()

    blk = lambda off: _full((s, 128), lambda p, off=off: (0, off + p))
    tab = _full((s, 128), lambda p: (0, 0))
    qo, ko, vo = ((off + 512 * g) // 128 for off in (Q_OFF, K_OFF, V_OFF))
    return pl.pallas_call(
        body, name=name, grid=(4,),
        in_specs=[blk(qo), blk(ko), blk(vo), blk(0), blk(0), blk(0), tab, tab, tab, pl.BlockSpec(memory_space=pl.ANY)],
        out_specs=pl.BlockSpec(memory_space=pl.ANY),
        out_shape=jax.ShapeDtypeStruct((s, IN_PAD), BF16),
        scratch_shapes=[pltpu.VMEM((s, 128), F32)] * 7 + [pltpu.VMEM((3, s, 128), BF16), pltpu.SemaphoreType.DMA((3,))],
        input_output_aliases={9: 0},
        compiler_params=_cparams(("arbitrary",)),
    )(proj, proj, proj, do, lse, dd, *tabs, dproj)


def _chip_copies(ins, outs, sems, outgoing):
    x, y, c, chips = _place()
    myq = 2 * x + y
    n = len(ins)

    def chunk(a, j):
        q = 2 * chips[j][0] + chips[j][1]
        return _remote(ins[a].at[q], outs[a].at[myq if outgoing else q], sems, 2 * a, j, (*chips[j], c))

    if outgoing is None:
        return [pltpu.make_async_copy(ins[a].at[myq], outs[a].at[myq], sems.at[2 * a, 3]) for a in range(n)]
    return [chunk(a, j) for j in range(3) for a in range(n)]


def _dh_bwd(dproj, w_pad, x, gpre, dy, pairs):
    s = x.shape[0]
    tm, tk = min(1024, s), 1408
    ni, nk = s // tm, IN_PAD // tk
    n = len(pairs)

    def body(*refs):
        dp_ref, w_ref, x_ref, g_ref, dy_ref = refs[:5]
        ins, gx_out, dg_out, outs = refs[5:5 + n], refs[5 + n], refs[6 + n], refs[7 + n:7 + 2 * n]
        acc, sems = refs[7 + 2 * n], refs[8 + 2 * n]
        i, kk = pl.program_id(0), pl.program_id(1)

        @pl.when(jnp.logical_and(i == 0, kk == 0))
        def _():
            dg_out[...] = jnp.zeros_like(dg_out)
            for cp in _chip_copies(ins, outs, sems, None) + _chip_copies(ins, outs, sems, True):
                cp.start()

        @pl.when(kk == 0)
        def _():
            acc[...] = jnp.zeros_like(acc)

        acc[...] += _dot_nt(dp_ref[...], w_ref[...])

        @pl.when(kk == nk - 1)
        def _():
            g = g_ref[...]
            _, xh, r = _rms(x_ref[...], g)
            dh = acc[...]
            dg_out[...] += jnp.sum(dh * xh, axis=0, keepdims=True)
            gx_out[...] = dy_ref[...] + _rms_bwd(dh, g, xh, r)

        @pl.when(jnp.logical_and(i == ni - 1, kk == nk - 1))
        def _():
            for cp in _chip_copies(ins, outs, sems, False):
                cp.wait_recv()
            for cp in _chip_copies(ins, outs, sems, True):
                cp.wait_send()
            for cp in _chip_copies(ins, outs, sems, None):
                cp.wait()

    row = lambda i, k: (i, 0)
    hbm = pl.BlockSpec(memory_space=pl.ANY)
    res = pl.pallas_call(
        body, name="dh_bwd", grid=(ni, nk),
        in_specs=[_full((tm, tk), lambda i, k: (i, k)), _full((1024, tk), lambda i, k: (0, k)),
                  _full((tm, 1024), row), _full((1, 1024), lambda i, k: (0, 0)), _full((tm, 1024), row)] + [hbm] * n,
        out_specs=[_full((tm, 1024), row), _full((1, 1024), lambda i, k: (0, 0))] + [hbm] * n,
        out_shape=[jax.ShapeDtypeStruct((s, 1024), F32), jax.ShapeDtypeStruct((1, 1024), F32)]
        + [jax.ShapeDtypeStruct(a.shape, a.dtype) for a in pairs],
        scratch_shapes=[pltpu.VMEM((tm, 1024), F32), pltpu.SemaphoreType.DMA((2 * n, 4))],
        compiler_params=_cparams(("arbitrary", "arbitrary")),
    )(dproj, w_pad, x, gpre, dy, *pairs)
    return res[0], res[1], res[2:]


def _dw_in(ht, dproj, pairs):
    s = ht.shape[1]
    tn = 768
    nj = IN_PAD // tn
    n = len(pairs)

    def body(*refs):
        ht_ref, dp_ref = refs[:2]
        ins, o_ref, outs, sems = refs[2:2 + n], refs[2 + n], refs[3 + n:3 + 2 * n], refs[3 + 2 * n]
        j = pl.program_id(0)

        if n:
            @pl.when(j == 0)
            def _():
                for cp in _chip_copies(ins, outs, sems, None) + _chip_copies(ins, outs, sems, True):
                    cp.start()

        o_ref[...] = _dot(ht_ref[...], dp_ref[...]).astype(BF16)

        if n:
            @pl.when(j == nj - 1)
            def _():
                for cp in _chip_copies(ins, outs, sems, False):
                    cp.wait_recv()
                for cp in _chip_copies(ins, outs, sems, True):
                    cp.wait_send()
                for cp in _chip_copies(ins, outs, sems, None):
                    cp.wait()

    hbm = pl.BlockSpec(memory_space=pl.ANY)
    res = pl.pallas_call(
        body, name="dw_in", grid=(nj,),
        in_specs=[_full((1024, s), lambda j: (0, 0)), _full((s, tn), lambda j: (0, j))] + [hbm] * n,
        out_specs=[_full((1024, tn), lambda j: (0, j))] + [hbm] * n,
        out_shape=[jax.ShapeDtypeStruct((1024, IN_PAD), BF16)] + [jax.ShapeDtypeStruct(a.shape, a.dtype) for a in pairs],
        scratch_shapes=[pltpu.SemaphoreType.DMA((max(2 * n, 2), 4))],
        compiler_params=_cparams(("arbitrary",)),
    )(ht, dproj, *pairs)
    return res[0], res[1:]


def _remote(src, dst, sems, row, k, to):
    return pltpu.make_async_remote_copy(src_ref=src, dst_ref=dst, send_sem=sems.at[row, k], recv_sem=sems.at[row + 1, k],
                                        device_id=to, device_id_type=pl.DeviceIdType.MESH)


def _place():
    x, y, c = lax.axis_index("x"), lax.axis_index("y"), lax.axis_index("c")
    return x, y, c, [(1 - x, y), (x, 1 - y), (1 - x, 1 - y)]


def _gather_parts(ins, outs, sems):
    n = len(ins)
    x, y, c, chips = _place()
    me, sib = (x, y, c), (x, y, 1 - c)
    idx = lambda p: 4 * p[0] + 2 * p[1] + p[2]

    def copy(a, k, block, to, from_input=False):
        src = ins[a] if from_input else outs[a].at[idx(block)]
        return _remote(src, outs[a].at[idx(block)], sems, 2 * a, k, to)

    own = lambda: [pltpu.make_async_copy(ins[a], outs[a].at[idx(me)], sems.at[2 * a, 7]) for a in range(n)]
    first = lambda: ([copy(a, 0, me, sib, True) for a in range(n)]
                     + [copy(a, 1 + j, me, (*chips[j], c), True) for j in range(3) for a in range(n)])
    from_chip = lambda j: [copy(a, 1 + j, (*chips[j], c), me) for a in range(n)]
    passed = lambda j: [copy(a, 4 + j, (*chips[j], c), sib) for a in range(n)]
    from_sibling = lambda: ([copy(a, 0, sib, me) for a in range(n)]
                            + [copy(a, 4 + j, (*chips[j], 1 - c), me) for j in range(3) for a in range(n)])
    return own, first, from_chip, passed, from_sibling


def _gather_start(ins, outs, sems):
    own, first, _, _, _ = _gather_parts(ins, outs, sems)
    for cp in own() + first():
        cp.start()


def _gather_forward(ins, outs, sems):
    _, _, from_chip, passed, _ = _gather_parts(ins, outs, sems)
    for j in range(3):
        for cp in from_chip(j):
            cp.wait_recv()
        for cp in passed(j):
            cp.start()


def _gather_finish(ins, outs, sems):
    own, first, _, passed, from_sibling = _gather_parts(ins, outs, sems)
    for cp in from_sibling():
        cp.wait_recv()
    for cp in first() + [cp for j in range(3) for cp in passed(j)]:
        cp.wait_send()
    for cp in own():
        cp.wait()


def _gather_weights(arrays):
    n = len(arrays)

    def body(*refs):
        ins, outs, sems = refs[:n], refs[n:2 * n], refs[2 * n]
        _gather_start(ins, outs, sems)
        _gather_forward(ins, outs, sems)
        _gather_finish(ins, outs, sems)

    hbm = pl.BlockSpec(memory_space=pl.ANY)
    return pl.pallas_call(
        body, name="gather_weights", in_specs=[hbm] * n, out_specs=[hbm] * n,
        out_shape=[jax.ShapeDtypeStruct((N_DEV,) + a.shape, a.dtype) for a in arrays],
        scratch_shapes=[pltpu.SemaphoreType.DMA((2 * n, N_DEV))],
    )(*arrays)


def _pair_exchange(chunks, name):
    n = len(chunks)

    def body(*refs):
        ins, outs, sems = refs[:n], refs[n:2 * n], refs[2 * n]
        x, y, c, _ = _place()
        sent = [_remote(ins[a].at[2 * q + (1 - c)], outs[a].at[q], sems, 2 * a, q, (x, y, 1 - c))
                for a in range(n) for q in range(4)]
        for cp in sent:
            cp.start()
        for cp in sent:
            cp.wait_recv()
        for cp in sent:
            cp.wait_send()

    hbm = pl.BlockSpec(memory_space=pl.ANY)
    return pl.pallas_call(
        body, name=name, in_specs=[hbm] * n, out_specs=[hbm] * n,
        out_shape=[jax.ShapeDtypeStruct((4,) + a.shape[1:], a.dtype) for a in chunks],
        scratch_shapes=[pltpu.SemaphoreType.DMA((2 * n, 4))],
    )(*chunks)


def _pair_sum(core, chunks, recv, name, tr):
    _, rows, cols = chunks.shape

    def body(c_ref, a_ref, b_ref, o_ref):
        o_ref[...] = (a_ref[...].astype(F32) + b_ref[...].astype(F32)).astype(BF16)

    blk = lambda f: _full((1, tr, cols), f)
    return pl.pallas_call(
        body, name=name, out_shape=jax.ShapeDtypeStruct((4, rows, cols), BF16),
        grid_spec=pltpu.PrefetchScalarGridSpec(
            num_scalar_prefetch=1, grid=(4, rows // tr),
            in_specs=[blk(lambda q, i, c: (2 * q + c[0], i, 0)), blk(lambda q, i, c: (q, i, 0))],
            out_specs=blk(lambda q, i, c: (q, i, 0))),
        compiler_params=_cparams(("parallel", "parallel")),
    )(core, chunks, recv)


def _packet_exchange(packet):
    def body(pk, pk_out, sems):
        x, y, c, _ = _place()
        me = 4 * x + 2 * y + c
        flip = lambda v, b: (1 - v) if b else v

        def small(j, outgoing):
            peer = (flip(x, (j >> 2) & 1), flip(y, (j >> 1) & 1), flip(c, j & 1))
            slot = me if outgoing else 4 * peer[0] + 2 * peer[1] + peer[2]
            return _remote(pk, pk_out.at[slot], sems, 0, j, peer)

        own = pltpu.make_async_copy(pk, pk_out.at[me], sems.at[0, 0])
        sent = [small(j, True) for j in range(1, N_DEV)]
        for cp in [own] + sent:
            cp.start()
        for j in range(1, N_DEV):
            small(j, False).wait_recv()
        for cp in sent:
            cp.wait_send()
        own.wait()

    hbm = pl.BlockSpec(memory_space=pl.ANY)
    return pl.pallas_call(
        body, name="packet_exchange", in_specs=[hbm], out_specs=hbm,
        out_shape=jax.ShapeDtypeStruct((N_DEV,) + packet.shape, packet.dtype),
        scratch_shapes=[pltpu.SemaphoreType.DMA((2, N_DEV))],
    )(packet)


def _adam_math(w, g, m, v):
    m = ADAM_B1 * m + (1.0 - ADAM_B1) * g
    v = ADAM_B2 * v + (1.0 - ADAM_B2) * (g * g)
    m_hat = m / (1.0 - ADAM_B1 ** ADAM_STEP)
    v_hat = v / (1.0 - ADAM_B2 ** ADAM_STEP)
    delta = -ADAM_LR * (m_hat / (jnp.sqrt(v_hat) + ADAM_EPS) + ADAM_WD * w)
    return delta, m, v


def _adam(recv, w, m, v, name, tr):
    _, rows, cols = w.shape

    def body(r_ref, w_ref, m_ref, v_ref, g_out, d_out, m_out, v_out):
        g = r_ref[0].astype(F32)
        for k in range(1, 4):
            g = g + r_ref[k].astype(F32)
        g_out[0] = g
        d_out[0], m_out[0], v_out[0] = _adam_math(w_ref[0], g, m_ref[0], v_ref[0])

    blk = _full((1, tr, cols), lambda i: (0, i, 0))
    return pl.pallas_call(
        body, name=name, grid=(rows // tr,),
        in_specs=[_full((4, tr, cols), lambda i: (0, i, 0)), blk, blk, blk],
        out_specs=[blk] * 4,
        out_shape=[jax.ShapeDtypeStruct(w.shape, F32)] * 4,
        compiler_params=_cparams(("parallel",)),
    )(recv, w, m, v)


def _adam_gains(recv, gains, gains_m, gains_v):
    def body(*refs):
        r_ref, w, m, v = refs[0], refs[1:5], refs[5:9], refs[9:13]
        g_out, d_out, m_out, v_out, loss_out = refs[13:17], refs[17:21], refs[21:25], refs[25:29], refs[29]
        tot = r_ref[0:1, :]
        for k in range(1, N_DEV):
            tot = tot + r_ref[k:k + 1, :]
        for t in range(4):
            g = tot[:, GAIN_OFFS[t]:GAIN_OFFS[t] + GAIN_WIDTHS[t]]
            g_out[t][...] = g
            d_out[t][...], m_out[t][...], v_out[t][...] = _adam_math(w[t][...], g, m[t][...], v[t][...])
        loss_out[...] = tot[:, LOSS_OFF:LOSS_OFF + LANES]

    shapes = [jax.ShapeDtypeStruct((1, n), F32) for n in GAIN_WIDTHS]
    return pl.pallas_call(
        body, name="adam_gains", out_shape=shapes * 4 + [jax.ShapeDtypeStruct((1, LANES), F32)],
    )(recv, *gains, *gains_m, *gains_v)


def _local_step(x, positions, gains, w_pad, small_shards, small_weights, target):
    gpre, gq, gkv, gpost = gains
    q_tab, k_tab, d_tab = _rope_tables(positions)

    proj, ht, gathered = _inproj(x, gpre, w_pad, d_tab, small_shards)
    w_uq, w_uk, w_uv, wp_mla, wp_dil, w_out = _assemble_small(*gathered) if small_shards else small_weights
    q, k, v, qt, kt, vt = _mla_prep(proj, gq, gkv, w_uq, w_uk, w_uv, q_tab, k_tab)
    o_mla, ot_mla, lse_mla = _mla_fwd(q, k, vt)

    od, lsed = [], []
    for g, d in enumerate(DIL_DILATIONS):
        o_g, lse_g = _dil_fwd(proj, g, d, "dil_fwd_%d" % g)
        od.append(o_g)
        lsed.append(lse_g)

    merged, ya, yd, o_dil, lse_dil = _merge_fwd(proj, o_mla, od, lsed, wp_mla, wp_dil)
    do, dy, loss, dgpost = _out_loss(merged, w_out, x, target, gpost)

    dproj, do_mla, do_dil, dot_mla, dd_dil, dw_out, dwp_mla, dwp_dil = _merge_bwd(
        do, w_out, merged, proj, ya, yd, o_mla, o_dil, wp_mla, wp_dil)

    dq, dk, dv = _mla_bwd(q, qt, k, kt, v, do_mla, dot_mla, ot_mla, lse_mla)
    dproj, dw_uq, dw_uk, dw_uv, dgq, dgkv = _mla_prep_bwd(dq, dk, dv, proj, dproj, gq, gkv, w_uq, w_uk, w_uv,
                                                          q_tab, k_tab)
    for g, d in enumerate(DIL_DILATIONS):
        dproj = _dil_bwd(proj, dproj, g, do_dil, lse_dil, dd_dil, d_tab, d, "dil_bwd_%d" % g)
    small = _small_chunks(dw_uq, dw_uk, dw_uv, dwp_mla, dwp_dil, dw_out)
    dw_in, small = _dw_in(ht, dproj, _pair_stage(small, 1, "small")) if small_shards else (_dw_in(ht, dproj, [])[0], small)
    return loss, (dproj, dy), (dgq, dgkv, dgpost), dw_in, small


def _pair_stage(chunks, first, name):
    from_sibling = _pair_exchange(chunks, "pair_exchange_" + name)
    core = lax.axis_index("c").astype(jnp.int32).reshape(1)
    return [_pair_sum(core, chunks[t], from_sibling[t], "pair_sum_%d" % (first + t), PAIR_ROWS[first + t])
            for t in range(len(chunks))]


ADAM_ROWS = (256, 384, 256, 512, 512, 128)
PAIR_ROWS = (512, 384, 256, 512, 512, 128)


def kernel(x, positions, pre_norm_g, w_in, q_norm_g, w_uq, kv_norm_g, w_ukv, w_proj_mla, w_proj_dil, w_out, post_norm_g, loss_target, m_pre_norm_g, m_w_in, m_q_norm_g, m_w_uq, m_kv_norm_g, m_w_ukv, m_w_proj_mla, m_w_proj_dil, m_w_out, m_post_norm_g, v_pre_norm_g, v_w_in, v_q_norm_g, v_w_uq, v_kv_norm_g, v_w_ukv, v_w_proj_mla, v_w_proj_dil, v_w_out, v_post_norm_g):
    big_w = (w_in, w_uq, w_ukv, w_proj_mla, w_proj_dil, w_out)
    big_m = (m_w_in, m_w_uq, m_w_ukv, m_w_proj_mla, m_w_proj_dil, m_w_out)
    big_v = (v_w_in, v_w_uq, v_w_ukv, v_w_proj_mla, v_w_proj_dil, v_w_out)
    gains = (pre_norm_g, q_norm_g, kv_norm_g, post_norm_g)
    gains_m = (m_pre_norm_g, m_q_norm_g, m_kv_norm_g, m_post_norm_g)
    gains_v = (v_pre_norm_g, v_q_norm_g, v_kv_norm_g, v_post_norm_g)

    shards = [w[0].astype(BF16) for w in big_w]
    w_pad = _assemble_w_in(_gather_weights(shards[:1])[0])

    loss, (dproj, dy), (dgq, dgkv, dgpost), dw_in, received_small = _local_step(
        x[0], positions[0], gains, w_pad, shards[1:], None, loss_target[0])

    grad_x, dgpre, received_in = _dh_bwd(dproj, w_pad, x[0], pre_norm_g, dy,
                                         _pair_stage([_dw_in_chunks(dw_in)], 0, "in"))
    received = list(received_in) + list(received_small)
    packet = _packet_exchange(jnp.concatenate([dgpre, dgq, dgkv, dgpost, loss[0:1]], axis=1))

    big = [_adam(received[t], big_w[t], big_m[t], big_v[t], "adam_%d" % t, ADAM_ROWS[t]) for t in range(6)]
    small = _adam_gains(packet.reshape(N_DEV, PACKET), gains, gains_m, gains_v)

    def interleave(kind):
        s_pre, s_q, s_kv, s_post = small[4 * kind:4 * kind + 4]
        b_in, b_uq, b_ukv, b_pm, b_pd, b_out = (big[t][kind] for t in range(6))
        return [s_pre, b_in, s_q, b_uq, s_kv, b_ukv, b_pm, b_pd, b_out, s_post]

    return (small[16][0, 0], grad_x[None], *interleave(0), *interleave(1), *interleave(2), *interleave(3))
```

```python
import numpy as np
import jax
import jax.numpy as jnp
from jax import lax
from jax.experimental import pallas as pl
from jax.experimental.pallas import tpu as pltpu

F32 = jnp.float32
BF16 = jnp.bfloat16

D_MODEL = 1024
NORM_EPS = 1e-6
ROPE_THETA = 500000.0
N_DEV = 8
LANES = 128
NEG = -1e30

MLA_HEADS = 8
MLA_Q_RANK = 384
MLA_KV_RANK = 256
MLA_SCALE = 96.0 ** -0.5
LOG2E = 1.4426950408889634
MLA_QSCALE = MLA_SCALE * LOG2E
MLA_FWD_TK, MLA_FWD_Q_PER_K = 256, 2
MLA_BWD_Q_PER_K = 2
DIL_DILATIONS = (1, 4, 16)
DIL_SCALE = 0.125
Q_BLOCK = 128
DIL_BLOCKS_PER_STEP = 4

Z_MLA, Z_DIL, G_MLA, G_DIL = 0, 512, 1024, 2048
Q_OFF, K_OFF, V_OFF = 3072, 4608, 6144
CQ_OFF, KR_OFF, CKV_OFF, IN_PAD = 7680, 8064, 8192, 8448
IN_WIDTH = 8352
SHARD_W = IN_WIDTH // 8
IN_SEGS = ((0, 384, CQ_OFF), (384, 256, CKV_OFF), (640, 32, KR_OFF), (672, 1536, Q_OFF), (2208, 1536, K_OFF),
           (3744, 1536, V_OFF), (5280, 512, Z_MLA), (5792, 512, Z_DIL), (6304, 1024, G_MLA), (7328, 1024, G_DIL))

GAIN_OFFS = (0, 1024, 1408, 1664)
GAIN_WIDTHS = (1024, 384, 256, 1024)
LOSS_OFF, PACKET = 2688, 2816

ADAM_LR, ADAM_B1, ADAM_B2, ADAM_EPS, ADAM_WD, ADAM_STEP = 0.001, 0.9, 0.999, 1e-08, 0.01, 10

VMEM_LIMIT_MB = 56


def _cparams(sem=None, vmem_mb=VMEM_LIMIT_MB):
    return pltpu.CompilerParams(dimension_semantics=sem, vmem_limit_bytes=vmem_mb * 1024 * 1024)


def _dot(a, b):
    return jnp.dot(a, b, preferred_element_type=F32)


def _dot_nt(a, b):
    return lax.dot_general(a, b, (((1,), (1,)), ((), ())), preferred_element_type=F32)


def _dot_tn(a, b):
    return lax.dot_general(a, b, (((0,), (0,)), ((), ())), preferred_element_type=F32)


def _tile_lanes(t, width):
    return t if width == t.shape[1] else jnp.tile(t, (1, width // t.shape[1]))


def _rope(x, c, sp, sm, a):
    n = x.shape[1]
    return x * c + pltpu.roll(x, a, 1) * sp + pltpu.roll(x, n - a, 1) * sm


def _unrope(dy, c, sp, sm, a):
    n = dy.shape[1]
    return dy * c + pltpu.roll(dy * sp, n - a, 1) + pltpu.roll(dy * sm, a, 1)


def _sigmoid(z):
    return 1.0 / (1.0 + jnp.exp(-z))


def _left_mask():
    return lax.broadcasted_iota(jnp.int32, (1, LANES), 1) < 64


def _expand_half(x, hh, left):
    r = pltpu.roll(x, 64, 1)
    return jnp.where(left, x, r) if hh == 0 else jnp.where(left, r, x)


def _rms(xv, g):
    r = lax.rsqrt(jnp.mean(xv * xv, axis=-1, keepdims=True) + NORM_EPS)
    xh = xv * r
    return xh * g, xh, r


def _rms_bwd(dout, g, xh, r):
    dxh = dout * g
    return r * (dxh - xh * jnp.mean(dxh * xh, axis=-1, keepdims=True))


def _full(shape, index_map):
    return pl.BlockSpec(shape, index_map)


def _w_in_pieces():
    out = []
    for s, n, off in sorted(IN_SEGS, key=lambda t: t[2]):
        c = s
        while c < s + n:
            k = c // SHARD_W
            e = min(s + n, (k + 1) * SHARD_W)
            out.append((k, c - k * SHARD_W, e - c, off + (c - s)))
            c = e
    return out


def _assemble_w_in(g):
    parts, cur = [], 0
    for k, a, w, off in _w_in_pieces():
        if off > cur:
            parts.append(jnp.zeros((D_MODEL, off - cur), g.dtype))
        parts.append(g[k, :, a:a + w])
        cur = off + w
    if cur < IN_PAD:
        parts.append(jnp.zeros((D_MODEL, IN_PAD - cur), g.dtype))
    return jnp.concatenate(parts, axis=1)


def _dw_in_chunks(dw):
    chunks = []
    for dev in range(N_DEV):
        mine = sorted((p for p in _w_in_pieces() if p[0] == dev), key=lambda p: p[1])
        chunks.append(jnp.concatenate([dw[:, off:off + w] for k, a, w, off in mine], axis=1))
    return jnp.stack(chunks)


def _assemble_small(g_uq, g_ukv, g_pm, g_pd, g_out):
    w_uq_pad = jnp.pad(g_uq.transpose(1, 0, 2), ((0, 0), (0, 0), (0, 32))).reshape(384, 1024)
    ukv = g_ukv.transpose(1, 0, 2)
    w_uk_pad = jnp.pad(ukv[:, :, :64], ((0, 0), (0, 0), (0, 64))).reshape(256, 1024)
    w_uv = ukv[:, :, 64:].reshape(256, 512)
    wp_mla = g_pm.transpose(1, 0, 2).reshape(512, 1024)
    wp_dil = g_pd.transpose(1, 0, 2).reshape(512, 1024)
    return w_uq_pad, w_uk_pad, w_uv, wp_mla, wp_dil, g_out.reshape(1024, 1024)


def _small_chunks(dw_uq_pad, dw_uk_pad, dw_uv, dwp_mla, dwp_dil, dw_out):
    b = dw_uq_pad.reshape(384, 8, 128)[:, :, :96].transpose(1, 0, 2)
    c = jnp.concatenate([dw_uk_pad.reshape(256, 8, 128)[:, :, :64], dw_uv.reshape(256, 8, 64)], axis=2)
    c = c.transpose(1, 0, 2)
    d = dwp_mla.reshape(512, N_DEV, 128).transpose(1, 0, 2)
    e = dwp_dil.reshape(512, N_DEV, 128).transpose(1, 0, 2)
    f = dw_out.reshape(N_DEV, 128, 1024)
    return [t.astype(BF16) for t in (b, c, d, e, f)]


def _lane_consts(freqs, half, first, period):
    rel = (np.arange(LANES) % period) - first
    rot = (rel >= 0) & (rel < 2 * half)
    freq = np.where(rot, freqs[np.clip(rel, 0, 2 * half - 1) % half], 0.0).astype(np.float32)
    x1 = (rot & (rel < half)).astype(np.float32)
    x2 = (rot & (rel >= half)).astype(np.float32)
    return freq[None, :], x1[None, :], x2[None, :]


def _rope_tables(pos):
    p = pos.astype(F32)[:, None]
    inv_m = np.float32(ROPE_THETA) ** (-(np.arange(0, 32, 2, dtype=np.float32) / np.float32(32)))
    inv_d = np.float32(ROPE_THETA) ** (-(np.arange(0, 16, 2, dtype=np.float32) / np.float32(16)))
    lane = np.arange(LANES)
    tabs = []
    for freqs, half, first, period, keep in ((inv_m, 16, 64, 128, lane < 96), (inv_m, 16, 0, 128, lane < 32),
                                              (inv_d, 8, 0, 64, lane >= 0)):
        freq, x1, x2 = _lane_consts(freqs, half, first, period)
        ang = p * freq
        sin = jnp.sin(ang)
        tabs.append((jnp.cos(ang) * keep.astype(np.float32)[None, :], sin * x2, sin * (-x1)))
    return tuple(tabs)


def _inproj(x, gpre, w_pad, d_tab, shards):
    s = x.shape[0]
    tm, tn = min(1024, s), 768
    ni, nj = s // tm, IN_PAD // tn
    rope_lo, rope_hi = Q_OFF // tn, V_OFF // tn
    n = len(shards)
    forward_step = min(nj, ni * nj - 2)

    def body(*refs):
        x_ref, g_ref, w_ref, c_ref, sp_ref, sm_ref = refs[:6]
        ins, o_ref, ht_ref, outs = refs[6:6 + n], refs[6 + n], refs[7 + n], refs[8 + n:8 + 2 * n]
        h_ref, sems = refs[8 + 2 * n], refs[9 + 2 * n]
        i, j = pl.program_id(0), pl.program_id(1)
        step = i * nj + j

        @pl.when(j == 0)
        def _():
            hv, _, _ = _rms(x_ref[...], g_ref[...])
            h_ref[...] = hv.astype(BF16)
            ht_ref[...] = hv.astype(BF16).T

        if n:
            @pl.when(step == 0)
            def _():
                _gather_start(ins, outs, sems)

            @pl.when(step == forward_step)
            def _():
                _gather_forward(ins, outs, sems)

        acc = _dot(h_ref[...], w_ref[...])
        is_rope = jnp.logical_and(j >= rope_lo, j < rope_hi)

        @pl.when(is_rope)
        def _():
            o_ref[...] = _rope(acc, _tile_lanes(c_ref[...], tn), _tile_lanes(sp_ref[...], tn),
                               _tile_lanes(sm_ref[...], tn), 8).astype(BF16)

        @pl.when(jnp.logical_not(is_rope))
        def _():
            o_ref[...] = acc.astype(BF16)

        if n:
            @pl.when(step == ni * nj - 1)
            def _():
                _gather_finish(ins, outs, sems)

    row = lambda i, j: (i, 0)
    hbm = pl.BlockSpec(memory_space=pl.ANY)
    res = pl.pallas_call(
        body, name="inproj", grid=(ni, nj),
        in_specs=[_full((tm, D_MODEL), row), _full((1, D_MODEL), lambda i, j: (0, 0)),
                  _full((D_MODEL, tn), lambda i, j: (0, j)),
                  _full((tm, LANES), row), _full((tm, LANES), row), _full((tm, LANES), row)] + [hbm] * n,
        out_specs=[_full((tm, tn), lambda i, j: (i, j)), _full((D_MODEL, tm), lambda i, j: (0, i))] + [hbm] * n,
        out_shape=[jax.ShapeDtypeStruct((s, IN_PAD), BF16), jax.ShapeDtypeStruct((D_MODEL, s), BF16)]
        + [jax.ShapeDtypeStruct((N_DEV,) + a.shape, a.dtype) for a in shards],
        scratch_shapes=[pltpu.VMEM((tm, D_MODEL), BF16), pltpu.SemaphoreType.DMA((max(2 * n, 2), N_DEV))],
        compiler_params=_cparams(("arbitrary", "arbitrary")),
    )(x, gpre, w_pad, *d_tab, *shards)
    return res[0], res[1], res[2:]


def _mla_prep(proj, gq, gkv, w_uq, w_uk, w_uv, q_tab, k_tab):
    s = proj.shape[0]
    tm = min(512, s)

    def body(cq_ref, kr_ref, ckv_ref, gq_ref, gkv_ref, wq_ref, wk_ref, wv_ref,
             qc, qsp, qsm, kc, ksp, ksm, q_out, k_out, v_out, qt_out, kt_out, vt_out):
        cqn, _, _ = _rms(cq_ref[...].astype(F32), gq_ref[...])
        q = _dot(cqn.astype(BF16), wq_ref[...])
        q = _rope(q, _tile_lanes(qc[...], 1024), _tile_lanes(qsp[...], 1024), _tile_lanes(qsm[...], 1024), 16)
        q = q * MLA_QSCALE
        q_out[...] = q.astype(BF16)
        qt_out[...] = q.T.astype(BF16)
        ckvn, _, _ = _rms(ckv_ref[...].astype(F32), gkv_ref[...])
        ckvn = ckvn.astype(BF16)
        kr = _rope(kr_ref[...].astype(F32), kc[...], ksp[...], ksm[...], 16)
        k = _dot(ckvn, wk_ref[...]) + _tile_lanes(pltpu.roll(kr, 64, 1), 1024)
        k_out[...] = k.astype(BF16)
        kt_out[...] = k.T.astype(BF16)
        v = _dot(ckvn, wv_ref[...])
        v_out[...] = v.astype(BF16)
        vt_out[...] = v.T.astype(BF16)

    row = lambda i: (i, 0)
    col = lambda i: (0, i)
    cst = lambda i: (0, 0)
    tabs = [_full((tm, LANES), row)] * 6
    return pl.pallas_call(
        body, name="mla_prep", grid=(s // tm,),
        in_specs=[_full((tm, 384), lambda i: (i, CQ_OFF // 384)), _full((tm, 128), lambda i: (i, KR_OFF // 128)),
                  _full((tm, 256), lambda i: (i, CKV_OFF // 256)), _full((1, 384), cst), _full((1, 256), cst),
                  _full((384, 1024), cst), _full((256, 1024), cst), _full((256, 512), cst)] + tabs,
        out_specs=[_full((tm, 1024), row), _full((tm, 1024), row), _full((tm, 512), row),
                   _full((1024, tm), col), _full((1024, tm), col), _full((512, tm), col)],
        out_shape=[jax.ShapeDtypeStruct((s, 1024), BF16), jax.ShapeDtypeStruct((s, 1024), BF16),
                   jax.ShapeDtypeStruct((s, 512), BF16), jax.ShapeDtypeStruct((1024, s), BF16),
                   jax.ShapeDtypeStruct((1024, s), BF16), jax.ShapeDtypeStruct((512, s), BF16)],
        compiler_params=_cparams(("parallel",)),
    )(proj, proj, proj, gq, gkv, w_uq, w_uk, w_uv, *q_tab, *k_tab)


def _mla_fwd(q, k, vt):
    s = q.shape[0]
    tk = min(MLA_FWD_TK, s)
    ratio = MLA_FWD_Q_PER_K if s >= MLA_FWD_Q_PER_K * tk else 1
    tq = ratio * tk
    nq = s // tq

    def body(q_ref, k_ref, vt_ref, o_ref, ot_ref, lse_ref):
        krow = lax.broadcasted_iota(jnp.int32, (tk, tq), 0)
        qcol = lax.broadcasted_iota(jnp.int32, (tk, tq), 1)

        def q_step(i, _):
            r0 = pl.multiple_of(i * tq, tq)
            qs = [q_ref[pl.ds(r0, tq), hh * 128:(hh + 1) * 128] for hh in range(2)]

            def scores(j):
                c0 = pl.multiple_of(j * tk, tk)
                return tuple(_dot_nt(k_ref[pl.ds(c0, tk), hh * 128:(hh + 1) * 128], qs[hh])
                             for hh in range(2))

            def update(j, sts, stats, masked):
                c0 = pl.multiple_of(j * tk, tk)
                new = []
                causal = (krow + (c0 - r0)) <= qcol
                for hh in range(2):
                    m, l, acc = stats[hh]
                    st = jnp.where(causal, sts[hh], NEG) if masked else sts[hh]
                    m_new = jnp.maximum(m, jnp.max(st, axis=0, keepdims=True))
                    alpha = jnp.exp2(m - m_new)
                    p = jnp.exp2(st - m_new)
                    l = alpha * l + jnp.sum(p, axis=0, keepdims=True)
                    acc = acc * alpha + _dot(vt_ref[hh * 64:(hh + 1) * 64, pl.ds(c0, tk)], p.astype(BF16))
                    new.append((m_new, l, acc))
                return tuple(new)

            init = tuple((jnp.full((1, tq), NEG, F32), jnp.zeros((1, tq), F32), jnp.zeros((64, tq), F32))
                         for _ in range(2))
            stats = lax.fori_loop(0, ratio * i, lambda j, st: update(j, scores(j), st, False), init)
            for d in range(ratio):
                stats = update(ratio * i + d, scores(ratio * i + d), stats, True)
            (ma, la, acca), (mb, lb, accb) = stats
            ot = jnp.concatenate([acca / la, accb / lb], axis=0)
            ot_ref[:, pl.ds(r0, tq)] = ot.astype(BF16)
            o_ref[pl.ds(r0, tq), :] = ot.T.astype(BF16)
            lse_ref[:, pl.ds(r0, tq)] = jnp.concatenate(
                [ma + jnp.log2(la), mb + jnp.log2(lb), jnp.zeros((6, tq), F32)], axis=0)
            return 0

        lax.fori_loop(0, nq, q_step, 0)

    return pl.pallas_call(
        body, name="mla_fwd", grid=(4,),
        in_specs=[_full((s, 256), lambda p: (0, p)), _full((s, 256), lambda p: (0, p)),
                  _full((128, s), lambda p: (p, 0))],
        out_specs=[_full((s, 128), lambda p: (0, p)), _full((128, s), lambda p: (p, 0)),
                   _full((8, s), lambda p: (p, 0))],
        out_shape=[jax.ShapeDtypeStruct((s, 512), BF16), jax.ShapeDtypeStruct((512, s), BF16),
                   jax.ShapeDtypeStruct((32, s), F32)],
        compiler_params=_cparams(("parallel",)),
    )(q, k, vt)


def _band_mask(has_prev):
    r = lax.broadcasted_iota(jnp.int32, (Q_BLOCK, 2 * Q_BLOCK), 0)
    c = lax.broadcasted_iota(jnp.int32, (Q_BLOCK, 2 * Q_BLOCK), 1)
    lo = jnp.where(has_prev, r, Q_BLOCK)
    return jnp.logical_and(c >= lo, c <= r + Q_BLOCK)


def _dil_rows(b, d, per_seq):
    r, n = b // per_seq, b % per_seq
    start = r + (d * Q_BLOCK) * n
    prev = start - jnp.where(n > 0, d * Q_BLOCK, 0)
    if d == 1:
        return pl.ds(pl.multiple_of(start, Q_BLOCK), Q_BLOCK), pl.ds(pl.multiple_of(prev, Q_BLOCK), Q_BLOCK)
    return pl.ds(start, Q_BLOCK, stride=d), pl.ds(prev, Q_BLOCK, stride=d)


def _dil_fwd(proj, g, d, name):
    s = proj.shape[0]
    nblk = s // Q_BLOCK
    per_seq = nblk // d

    def body(q_ref, k_ref, v_ref, o_ref, lse_ref, qf, kf, vf, of, lf):
        left = _left_mask()
        hms = (left, jnp.logical_not(left))
        qf[...] = q_ref[...].astype(F32)
        kf[...] = k_ref[...].astype(F32)
        vf[...] = v_ref[...].astype(F32)

        def scores(b):
            rows, prow = _dil_rows(b, d, per_seq)
            qb = qf[rows, :].astype(BF16)
            kk = jnp.concatenate([kf[prow, :], kf[rows, :]], axis=0).astype(BF16)
            return [_dot_nt(jnp.where(hms[hh], qb, jnp.zeros_like(qb)), kk) * DIL_SCALE for hh in range(2)]

        def finish(b, tiles):
            rows, prow = _dil_rows(b, d, per_seq)
            mask = _band_mask((b % per_seq) > 0)
            vv = jnp.concatenate([vf[prow, :], vf[rows, :]], axis=0).astype(BF16)
            outs = []
            for hh in range(2):
                sc = jnp.where(mask, tiles[hh], NEG)
                m = jnp.max(sc, axis=1, keepdims=True)
                p = jnp.exp(sc - m)
                l = jnp.sum(p, axis=1, keepdims=True)
                acc = _dot(p.astype(BF16), jnp.where(hms[hh], vv, jnp.zeros_like(vv)))
                outs.append((acc / l, jnp.broadcast_to(m + jnp.log(l), (Q_BLOCK, LANES))))
            of[rows, :] = outs[0][0] + outs[1][0]
            lf[rows, :] = jnp.where(left, outs[0][1], outs[1][1])

        def step(t, _):
            tiles = [scores(DIL_BLOCKS_PER_STEP * t + u) for u in range(DIL_BLOCKS_PER_STEP)]
            for u in range(DIL_BLOCKS_PER_STEP):
                finish(DIL_BLOCKS_PER_STEP * t + u, tiles[u])
            return 0

        lax.fori_loop(0, nblk // DIL_BLOCKS_PER_STEP, step, 0)
        o_ref[...] = of[...].astype(BF16)
        lse_ref[...] = lf[...]

    blk = lambda off: _full((s, 128), lambda p, off=off: (0, off + p))
    qo, ko, vo = ((off + 512 * g) // 128 for off in (Q_OFF, K_OFF, V_OFF))
    return pl.pallas_call(
        body, name=name, grid=(4,),
        in_specs=[blk(qo), blk(ko), blk(vo)],
        out_specs=[blk(0), blk(0)],
        out_shape=[jax.ShapeDtypeStruct((s, 512), BF16), jax.ShapeDtypeStruct((s, 512), F32)],
        scratch_shapes=[pltpu.VMEM((s, 128), F32)] * 5,
        compiler_params=_cparams(("parallel",)),
    )(proj, proj, proj)


def _merge_fwd(proj, o_mla, od, lsed, wp_mla, wp_dil):
    s = proj.shape[0]
    tm = min(512, s)

    def body(zm_ref, zd_ref, gm_ref, gd_ref, om_ref, o0, o1, o2, l0, l1, l2, wm_ref, wd_ref,
             mg_out, ya_out, yd_out, odil_out, lse_out):
        la, lb, lc = l0[...], l1[...], l2[...]
        lmax = jnp.maximum(jnp.maximum(la, lb), lc)
        ea, eb, ec = jnp.exp(la - lmax), jnp.exp(lb - lmax), jnp.exp(lc - lmax)
        den = ea + eb + ec
        o_dil = (ea * o0[...].astype(F32) + eb * o1[...].astype(F32) + ec * o2[...].astype(F32)) / den
        o_dil = o_dil.astype(BF16)
        odil_out[...] = o_dil
        lse_out[...] = lmax + jnp.log(den)
        zm, zd = zm_ref[...].astype(F32), zd_ref[...].astype(F32)
        pa = (om_ref[...].astype(F32) * (zm * _sigmoid(zm))).astype(BF16)
        pd = (o_dil.astype(F32) * (zd * _sigmoid(zd))).astype(BF16)
        ya = _dot(pa, wm_ref[...])
        yd = _dot(pd, wd_ref[...])
        ya_out[...] = ya.astype(BF16)
        yd_out[...] = yd.astype(BF16)
        mg_out[...] = (_sigmoid(gm_ref[...].astype(F32)) * ya + _sigmoid(gd_ref[...].astype(F32)) * yd).astype(BF16)

    row = lambda i: (i, 0)
    cst = lambda i: (0, 0)
    r512 = _full((tm, 512), row)
    r1024 = _full((tm, 1024), row)
    return pl.pallas_call(
        body, name="merge_fwd", grid=(s // tm,),
        in_specs=[_full((tm, 512), lambda i: (i, Z_MLA // 512)), _full((tm, 512), lambda i: (i, Z_DIL // 512)),
                  _full((tm, 1024), lambda i: (i, G_MLA // 1024)), _full((tm, 1024), lambda i: (i, G_DIL // 1024)),
                  r512, r512, r512, r512, r512, r512, r512, _full((512, 1024), cst), _full((512, 1024), cst)],
        out_specs=[r1024, r1024, r1024, r512, r512],
        out_shape=[jax.ShapeDtypeStruct((s, 1024), BF16), jax.ShapeDtypeStruct((s, 1024), BF16),
                   jax.ShapeDtypeStruct((s, 1024), BF16), jax.ShapeDtypeStruct((s, 512), BF16),
                   jax.ShapeDtypeStruct((s, 512), F32)],
        compiler_params=_cparams(("parallel",)),
    )(proj, proj, proj, proj, o_mla, *od, *lsed, wp_mla, wp_dil)


def _out_loss(merged, w_out, x, target, gpost):
    s = x.shape[0]
    tm = min(512, s)

    def body(mg_ref, w_ref, x_ref, t_ref, g_ref, do_out, dy_out, loss_out, dg_out):
        i = pl.program_id(0)

        @pl.when(i == 0)
        def _():
            loss_out[...] = jnp.zeros_like(loss_out)
            dg_out[...] = jnp.zeros_like(dg_out)

        o = _dot(mg_ref[...], w_ref[...])
        g = g_ref[...]
        n, u, r = _rms(o, g)
        e = (x_ref[...] + n) - t_ref[...]
        loss_out[...] += 0.5 * jnp.sum(jnp.mean(e * e, axis=-1, keepdims=True))
        dy = e * (1.0 / D_MODEL)
        dy_out[...] = dy
        dg_out[...] += jnp.sum(dy * u, axis=0, keepdims=True)
        do_out[...] = _rms_bwd(dy, g, u, r).astype(BF16)

    row = lambda i: (i, 0)
    cst = lambda i: (0, 0)
    return pl.pallas_call(
        body, name="out_loss", grid=(s // tm,),
        in_specs=[_full((tm, 1024), row), _full((1024, 1024), cst), _full((tm, 1024), row), _full((tm, 1024), row),
                  _full((1, 1024), cst)],
        out_specs=[_full((tm, 1024), row), _full((tm, 1024), row), _full((8, LANES), cst), _full((1, 1024), cst)],
        out_shape=[jax.ShapeDtypeStruct((s, 1024), BF16), jax.ShapeDtypeStruct((s, 1024), F32),
                   jax.ShapeDtypeStruct((8, LANES), F32), jax.ShapeDtypeStruct((1, 1024), F32)],
        compiler_params=_cparams(("arbitrary",)),
    )(merged, w_out, x, target, gpost)


def _seg_sum64(x, ones_bd):
    hi = x.astype(BF16)
    lo = (x - hi.astype(F32)).astype(BF16)
    return _dot(hi, ones_bd) + _dot(lo, ones_bd)


def _merge_bwd(do, w_out, merged, proj, ya, yd, o_mla, o_dil, wp_mla, wp_dil):
    s = do.shape[0]
    tm = min(256, s)
    seg = jnp.arange(512) // 64
    ones_bd = (seg[:, None] == seg[None, :]).astype(BF16)

    def body(do_ref, wo_ref, mg_ref, zm_ref, zd_ref, gm_ref, gd_ref, ya_ref, yd_ref, om_ref, od_ref, wm_ref, wd_ref,
             bd_ref, dp_out, dom_out, dod_out, domt_out, dd_out, dwo_out, dwm_out, dwd_out):
        i = pl.program_id(0)

        @pl.when(i == 0)
        def _():
            dwo_out[...] = jnp.zeros_like(dwo_out)
            dwm_out[...] = jnp.zeros_like(dwm_out)
            dwd_out[...] = jnp.zeros_like(dwd_out)

        dov = do_ref[...]
        dwo_out[...] += _dot_tn(mg_ref[...], dov)
        dm = _dot_nt(dov, wo_ref[...])
        for g_ref, y_ref, z_ref, o_ref, w_ref, z_off, g_off, dob_out, dd_o, dw_out in (
                (gm_ref, ya_ref, zm_ref, om_ref, wm_ref, Z_MLA, G_MLA, dom_out, None, dwm_out),
                (gd_ref, yd_ref, zd_ref, od_ref, wd_ref, Z_DIL, G_DIL, dod_out, dd_out, dwd_out)):
            sg = _sigmoid(g_ref[...].astype(F32))
            dp_out[:, g_off:g_off + 1024] = (dm * y_ref[...].astype(F32) * sg * (1.0 - sg)).astype(BF16)
            dy = (dm * sg).astype(BF16)
            z = z_ref[...].astype(F32)
            sz = _sigmoid(z)
            silu = z * sz
            ob = o_ref[...].astype(F32)
            dw_out[...] += _dot_tn((ob * silu).astype(BF16), dy)
            dp = _dot_nt(dy, w_ref[...])
            dp_out[:, z_off:z_off + 512] = (dp * ob * (sz * (1.0 + z * (1.0 - sz)))).astype(BF16)
            dob = dp * silu
            dob_out[...] = dob.astype(BF16)
            if dd_o is None:
                domt_out[...] = dob.T.astype(BF16)
            else:
                dd_o[...] = _seg_sum64(dob * ob, bd_ref[...])

    row = lambda i: (i, 0)
    cst = lambda i: (0, 0)
    r512 = _full((tm, 512), row)
    r1024 = _full((tm, 1024), row)
    return pl.pallas_call(
        body, name="merge_bwd", grid=(s // tm,),
        in_specs=[r1024, _full((1024, 1024), cst), r1024,
                  _full((tm, 512), lambda i: (i, Z_MLA // 512)), _full((tm, 512), lambda i: (i, Z_DIL // 512)),
                  _full((tm, 1024), lambda i: (i, G_MLA // 1024)), _full((tm, 1024), lambda i: (i, G_DIL // 1024)),
                  r1024, r1024, r512, r512, _full((512, 1024), cst), _full((512, 1024), cst), _full((512, 512), cst)],
        out_specs=[_full((tm, Q_OFF), row), r512, r512, _full((512, tm), lambda i: (0, i)), r512,
                   _full((1024, 1024), cst), _full((512, 1024), cst), _full((512, 1024), cst)],
        out_shape=[jax.ShapeDtypeStruct((s, IN_PAD), BF16),
                   jax.ShapeDtypeStruct((s, 512), BF16), jax.ShapeDtypeStruct((s, 512), BF16),
                   jax.ShapeDtypeStruct((512, s), BF16), jax.ShapeDtypeStruct((s, 512), F32),
                   jax.ShapeDtypeStruct((1024, 1024), F32), jax.ShapeDtypeStruct((512, 1024), F32),
                   jax.ShapeDtypeStruct((512, 1024), F32)],
        compiler_params=_cparams(("arbitrary",)),
    )(do, w_out, merged, proj, proj, proj, proj, ya, yd, o_mla, o_dil, wp_mla, wp_dil, ones_bd)


def _mla_bwd(q, qt, k, kt, v, do, dot, ot, lse):
    s = q.shape[0]
    tk = min(256, s)
    ratio = MLA_BWD_Q_PER_K if s >= MLA_BWD_Q_PER_K * tk else 1
    tq = ratio * tk
    nq, nk = s // tq, s // tk

    def body(q_ref, qt_ref, k_ref, kt_ref, v_ref, do_ref, dot_ref, ot_ref, lse_ref, dqt_out, dkt_out, dvt_out,
             dqt_acc):
        left = _left_mask()
        krow = lax.broadcasted_iota(jnp.int32, (tk, tq), 0)
        qcol = lax.broadcasted_iota(jnp.int32, (tk, tq), 1)
        dqt_acc[...] = jnp.zeros_like(dqt_acc)

        def kv_step(j, _):
            c0 = pl.multiple_of(j * tk, tk)
            vv = v_ref[pl.ds(c0, tk), :]
            khs = [k_ref[pl.ds(c0, tk), hh * 128:(hh + 1) * 128] for hh in range(2)]
            kths = [kt_ref[hh * 128:(hh + 1) * 128, pl.ds(c0, tk)] for hh in range(2)]
            vms = [jnp.where(left if hh == 0 else jnp.logical_not(left), vv, jnp.zeros_like(vv)) for hh in range(2)]

            def scores(i):
                r0 = pl.multiple_of(jnp.minimum(i, nq - 1) * tq, tq)
                dov = do_ref[pl.ds(r0, tq), :]
                return tuple((_dot_nt(khs[hh], q_ref[pl.ds(r0, tq), hh * 128:(hh + 1) * 128]),
                              _dot_nt(vms[hh], dov)) for hh in range(2))

            def update(i, tiles, acc, masked):
                r0 = pl.multiple_of(i * tq, tq)
                new = []
                for hh in range(2):
                    dkt, dvt = acc[hh]
                    st, dp = tiles[hh]
                    hrows = slice(hh * 128, (hh + 1) * 128)
                    drows = slice(hh * 64, (hh + 1) * 64)
                    doth = dot_ref[drows, pl.ds(r0, tq)]
                    dd = jnp.sum(doth.astype(F32) * ot_ref[drows, pl.ds(r0, tq)].astype(F32), axis=0, keepdims=True)
                    p = jnp.exp2(st - lse_ref[hh:hh + 1, pl.ds(r0, tq)])
                    if masked:
                        p = jnp.where((krow + (c0 - r0)) <= qcol, p, 0.0)
                    ds = (p * (dp - dd)).astype(BF16)
                    dvt = dvt + _dot_nt(doth, p.astype(BF16))
                    dkt = dkt + _dot_nt(qt_ref[hrows, pl.ds(r0, tq)], ds)
                    dqt_acc[hrows, pl.ds(r0, tq)] += _dot(kths[hh], ds)
                    new.append((dkt, dvt))
                return tuple(new)

            init = tuple((jnp.zeros((128, tk), F32), jnp.zeros((64, tk), F32)) for _ in range(2))
            i0 = j // ratio
            acc = update(i0, scores(i0), init, True)
            acc = lax.fori_loop(i0 + 1, nq, lambda i, a: update(i, scores(i), a, False), acc)
            for hh in range(2):
                dkt_out[hh * 128:(hh + 1) * 128, pl.ds(c0, tk)] = (acc[hh][0] * (1.0 / LOG2E)).astype(BF16)
                dvt_out[hh * 64:(hh + 1) * 64, pl.ds(c0, tk)] = acc[hh][1].astype(BF16)
            return 0

        lax.fori_loop(0, nk, kv_step, 0)
        dqt_out[...] = (dqt_acc[...] * MLA_SCALE).astype(BF16)

    b256 = _full((s, 256), lambda p: (0, p))
    b128 = _full((s, 128), lambda p: (0, p))
    t256 = _full((256, s), lambda p: (p, 0))
    t128 = _full((128, s), lambda p: (p, 0))
    return pl.pallas_call(
        body, name="mla_bwd", grid=(4,),
        in_specs=[b256, t256, b256, t256, b128, b128, t128, t128, _full((8, s), lambda p: (p, 0))],
        out_specs=[t256, t256, t128],
        out_shape=[jax.ShapeDtypeStruct((1024, s), BF16), jax.ShapeDtypeStruct((1024, s), BF16),
                   jax.ShapeDtypeStruct((512, s), BF16)],
        scratch_shapes=[pltpu.VMEM((256, s), F32)],
        compiler_params=_cparams(("parallel",)),
    )(q, qt, k, kt, v, do, dot, ot, lse)


def _mla_prep_bwd(dq, dk, dv, proj, dproj, gq, gkv, w_uq, w_uk, w_uv, q_tab, k_tab):
    s = proj.shape[0]
    tm = min(256, s)

    def body(dqt_ref, dkt_ref, dvt_ref, cq_ref, ckv_ref, dp_in, gq_ref, gkv_ref, wq_ref, wk_ref, wv_ref,
             qc, qsp, qsm, kc, ksp, ksm,
             dp_out, dwq_out, dwk_out, dwv_out, dgq_out, dgkv_out):
        i = pl.program_id(0)

        @pl.when(i == 0)
        def _():
            for r in (dwq_out, dwk_out, dwv_out, dgq_out, dgkv_out):
                r[...] = jnp.zeros_like(r)

        dqu = _unrope(dqt_ref[...].astype(F32).T, _tile_lanes(qc[...], 1024), _tile_lanes(qsp[...], 1024),
                      _tile_lanes(qsm[...], 1024), 16).astype(BF16)
        gq = gq_ref[...]
        cqn, xh, r = _rms(cq_ref[...].astype(F32), gq)
        dwq_out[...] += _dot_tn(cqn.astype(BF16), dqu)
        dcqn = _dot_nt(dqu, wq_ref[...])
        dgq_out[...] += jnp.sum(dcqn * xh, axis=0, keepdims=True)
        dp_out[:, 0:384] = _rms_bwd(dcqn, gq, xh, r).astype(BF16)

        dkf = dkt_ref[...].astype(F32).T
        dkb = dkf.astype(BF16)
        dsum = dkf[:, 0:128]
        for h in range(1, MLA_HEADS):
            dsum = dsum + dkf[:, h * 128:(h + 1) * 128]
        dp_out[:, 384:512] = _unrope(pltpu.roll(dsum, 64, 1), kc[...], ksp[...], ksm[...], 16).astype(BF16)

        dvb = dvt_ref[...].astype(F32).T.astype(BF16)
        gkv = gkv_ref[...]
        ckvn, xh2, r2 = _rms(ckv_ref[...].astype(F32), gkv)
        ckvn = ckvn.astype(BF16)
        dwk_out[...] += _dot_tn(ckvn, dkb)
        dwv_out[...] += _dot_tn(ckvn, dvb)
        dckvn = _dot_nt(dkb, wk_ref[...]) + _dot_nt(dvb, wv_ref[...])
        dgkv_out[...] += jnp.sum(dckvn * xh2, axis=0, keepdims=True)
        dp_out[:, 512:768] = _rms_bwd(dckvn, gkv, xh2, r2).astype(BF16)

    row = lambda i: (i, 0)
    cst = lambda i: (0, 0)
    tabs = [_full((tm, LANES), row)] * 6
    return pl.pallas_call(
        body, name="mla_prep_bwd", grid=(s // tm,),
        in_specs=[_full((1024, tm), lambda i: (0, i)), _full((1024, tm), lambda i: (0, i)),
                  _full((512, tm), lambda i: (0, i)),
                  _full((tm, 384), lambda i: (i, CQ_OFF // 384)), _full((tm, 256), lambda i: (i, CKV_OFF // 256)),
                  pl.BlockSpec(memory_space=pl.ANY), _full((1, 384), cst), _full((1, 256), cst),
                  _full((384, 1024), cst), _full((256, 1024), cst), _full((256, 512), cst)] + tabs,
        out_specs=[_full((tm, IN_PAD - CQ_OFF), lambda i: (i, CQ_OFF // (IN_PAD - CQ_OFF))),
                   _full((384, 1024), cst), _full((256, 1024), cst), _full((256, 512), cst),
                   _full((1, 384), cst), _full((1, 256), cst)],
        out_shape=[jax.ShapeDtypeStruct((s, IN_PAD), BF16),
                   jax.ShapeDtypeStruct((384, 1024), F32), jax.ShapeDtypeStruct((256, 1024), F32),
                   jax.ShapeDtypeStruct((256, 512), F32),
                   jax.ShapeDtypeStruct((1, 384), F32), jax.ShapeDtypeStruct((1, 256), F32)],
        input_output_aliases={5: 0},
        compiler_params=_cparams(("arbitrary",)),
    )(dq, dk, dv, proj, proj, dproj, gq, gkv, w_uq, w_uk, w_uv, *q_tab, *k_tab)


def _dil_bwd(proj, dproj, g, do, lse, dd, tabs, d, name):
    s = proj.shape[0]
    nblk = s // Q_BLOCK
    per_seq = nblk // d

    def body(q_ref, k_ref, v_ref, do_ref, lse_ref, dd_ref, c_ref, sp_ref, sm_ref, dp_in, dp_out,
             qf, kf, vf, dof, dq_acc, dk_acc, dv_acc, staged, sems):
        left = _left_mask()
        hms = (left, jnp.logical_not(left))
        qf[...] = q_ref[...].astype(F32)
        kf[...] = k_ref[...].astype(F32)
        vf[...] = v_ref[...].astype(F32)
        dof[...] = do_ref[...].astype(F32)
        dk_acc[...] = jnp.zeros_like(dk_acc)
        dv_acc[...] = jnp.zeros_like(dv_acc)

        def scores(b):
            rows, prow = _dil_rows(b, d, per_seq)
            qb, dob = qf[rows, :].astype(BF16), dof[rows, :].astype(BF16)
            kk = jnp.concatenate([kf[prow, :], kf[rows, :]], axis=0).astype(BF16)
            vv = jnp.concatenate([vf[prow, :], vf[rows, :]], axis=0).astype(BF16)
            zero = jnp.zeros_like(qb)
            out = []
            for hh in range(2):
                qm, dom = jnp.where(hms[hh], qb, zero), jnp.where(hms[hh], dob, zero)
                out.append((_dot_nt(qm, kk) * DIL_SCALE, _dot_nt(dom, vv)))
            return out

        def finish(b, tiles):
            rows, prow = _dil_rows(b, d, per_seq)
            mask = _band_mask((b % per_seq) > 0)
            qb, dob = qf[rows, :].astype(BF16), dof[rows, :].astype(BF16)
            kk = jnp.concatenate([kf[prow, :], kf[rows, :]], axis=0).astype(BF16)
            lse_b, dd_b = lse_ref[rows, :], dd_ref[rows, :]
            zero = jnp.zeros_like(qb)
            dq = jnp.zeros((Q_BLOCK, LANES), F32)
            dk = jnp.zeros((2 * Q_BLOCK, LANES), F32)
            dv = jnp.zeros((2 * Q_BLOCK, LANES), F32)
            for hh in range(2):
                hm = hms[hh]
                qm, dom = jnp.where(hm, qb, zero), jnp.where(hm, dob, zero)
                lse_h = _tile_lanes(_expand_half(lse_b, hh, left), 2 * Q_BLOCK)
                dd_h = _tile_lanes(_expand_half(dd_b, hh, left), 2 * Q_BLOCK)
                sc, dp = tiles[hh]
                p = jnp.where(mask, jnp.exp(sc - lse_h), 0.0)
                ds = (p * (dp - dd_h) * DIL_SCALE).astype(BF16)
                dq = dq + _dot(ds, jnp.where(hm, kk, jnp.zeros_like(kk)))
                dk = dk + _dot_tn(ds, qm)
                dv = dv + _dot_tn(p.astype(BF16), dom)
            dq_acc[rows, :] = dq
            dk_acc[prow, :] += dk[0:Q_BLOCK]
            dv_acc[prow, :] += dv[0:Q_BLOCK]
            dk_acc[rows, :] += dk[Q_BLOCK:]
            dv_acc[rows, :] += dv[Q_BLOCK:]

        def step(t, _):
            tiles = [scores(DIL_BLOCKS_PER_STEP * t + u) for u in range(DIL_BLOCKS_PER_STEP)]
            for u in range(DIL_BLOCKS_PER_STEP):
                finish(DIL_BLOCKS_PER_STEP * t + u, tiles[u])
            return 0

        lax.fori_loop(0, nblk // DIL_BLOCKS_PER_STEP, step, 0)
        staged[0] = _unrope(dq_acc[...], c_ref[...], sp_ref[...], sm_ref[...], 8).astype(BF16)
        staged[1] = _unrope(dk_acc[...], c_ref[...], sp_ref[...], sm_ref[...], 8).astype(BF16)
        staged[2] = dv_acc[...].astype(BF16)
        pair = pl.program_id(0)
        copies = [pltpu.make_async_copy(
            staged.at[t], dp_out.at[:, pl.ds(pl.multiple_of(off + 512 * g + 128 * pair, 128), 128)], sems.at[t])
            for t, off in enumerate((Q_OFF, K_OFF, V_OFF))]
        for cp in copies:
            cp.start()
        for cp in copies:
            cp.wait()

    blk = lambda off: _full((s, 128), lambda p, off=off: (0, off + p))
    tab = _full((s, 128), lambda p: (0, 0))
    qo, ko, vo = ((off + 512 * g) // 128 for off in (Q_OFF, K_OFF, V_OFF))
    return pl.pallas_call(
        body, name=name, grid=(4,),
        in_specs=[blk(qo), blk(ko), blk(vo), blk(0), blk(0), blk(0), tab, tab, tab, pl.BlockSpec(memory_space=pl.ANY)],
        out_specs=pl.BlockSpec(memory_space=pl.ANY),
        out_shape=jax.ShapeDtypeStruct((s, IN_PAD), BF16),
        scratch_shapes=[pltpu.VMEM((s, 128), F32)] * 7 + [pltpu.VMEM((3, s, 128), BF16), pltpu.SemaphoreType.DMA((3,))],
        input_output_aliases={9: 0},
        compiler_params=_cparams(("arbitrary",)),
    )(proj, proj, proj, do, lse, dd, *tabs, dproj)


def _chip_copies(ins, outs, sems, outgoing):
    x, y, c, chips = _place()
    myq = 2 * x + y
    n = len(ins)

    def chunk(a, j):
        q = 2 * chips[j][0] + chips[j][1]
        return _remote(ins[a].at[q], outs[a].at[myq if outgoing else q], sems, 2 * a, j, (*chips[j], c))

    if outgoing is None:
        return [pltpu.make_async_copy(ins[a].at[myq], outs[a].at[myq], sems.at[2 * a, 3]) for a in range(n)]
    return [chunk(a, j) for j in range(3) for a in range(n)]


def _dh_bwd(dproj, w_pad, x, gpre, dy, pairs):
    s = x.shape[0]
    tm, tk = min(1024, s), 1408
    ni, nk = s // tm, IN_PAD // tk
    n = len(pairs)

    def body(*refs):
        dp_ref, w_ref, x_ref, g_ref, dy_ref = refs[:5]
        ins, gx_out, dg_out, outs = refs[5:5 + n], refs[5 + n], refs[6 + n], refs[7 + n:7 + 2 * n]
        acc, sems = refs[7 + 2 * n], refs[8 + 2 * n]
        i, kk = pl.program_id(0), pl.program_id(1)

        @pl.when(jnp.logical_and(i == 0, kk == 0))
        def _():
            dg_out[...] = jnp.zeros_like(dg_out)
            for cp in _chip_copies(ins, outs, sems, None) + _chip_copies(ins, outs, sems, True):
                cp.start()

        @pl.when(kk == 0)
        def _():
            acc[...] = jnp.zeros_like(acc)

        acc[...] += _dot_nt(dp_ref[...], w_ref[...])

        @pl.when(kk == nk - 1)
        def _():
            g = g_ref[...]
            _, xh, r = _rms(x_ref[...], g)
            dh = acc[...]
            dg_out[...] += jnp.sum(dh * xh, axis=0, keepdims=True)
            gx_out[...] = dy_ref[...] + _rms_bwd(dh, g, xh, r)

        @pl.when(jnp.logical_and(i == ni - 1, kk == nk - 1))
        def _():
            for cp in _chip_copies(ins, outs, sems, False):
                cp.wait_recv()
            for cp in _chip_copies(ins, outs, sems, True):
                cp.wait_send()
            for cp in _chip_copies(ins, outs, sems, None):
                cp.wait()

    row = lambda i, k: (i, 0)
    hbm = pl.BlockSpec(memory_space=pl.ANY)
    res = pl.pallas_call(
        body, name="dh_bwd", grid=(ni, nk),
        in_specs=[_full((tm, tk), lambda i, k: (i, k)), _full((1024, tk), lambda i, k: (0, k)),
                  _full((tm, 1024), row), _full((1, 1024), lambda i, k: (0, 0)), _full((tm, 1024), row)] + [hbm] * n,
        out_specs=[_full((tm, 1024), row), _full((1, 1024), lambda i, k: (0, 0))] + [hbm] * n,
        out_shape=[jax.ShapeDtypeStruct((s, 1024), F32), jax.ShapeDtypeStruct((1, 1024), F32)]
        + [jax.ShapeDtypeStruct(a.shape, a.dtype) for a in pairs],
        scratch_shapes=[pltpu.VMEM((tm, 1024), F32), pltpu.SemaphoreType.DMA((2 * n, 4))],
        compiler_params=_cparams(("arbitrary", "arbitrary")),
    )(dproj, w_pad, x, gpre, dy, *pairs)
    return res[0], res[1], res[2:]


def _dw_in(ht, dproj, pairs):
    s = ht.shape[1]
    tn = 768
    nj = IN_PAD // tn
    n = len(pairs)

    def body(*refs):
        ht_ref, dp_ref = refs[:2]
        ins, o_ref, outs, sems = refs[2:2 + n], refs[2 + n], refs[3 + n:3 + 2 * n], refs[3 + 2 * n]
        j = pl.program_id(0)

        if n:
            @pl.when(j == 0)
            def _():
                for cp in _chip_copies(ins, outs, sems, None) + _chip_copies(ins, outs, sems, True):
                    cp.start()

        o_ref[...] = _dot(ht_ref[...], dp_ref[...]).astype(BF16)

        if n:
            @pl.when(j == nj - 1)
            def _():
                for cp in _chip_copies(ins, outs, sems, False):
                    cp.wait_recv()
                for cp in _chip_copies(ins, outs, sems, True):
                    cp.wait_send()
                for cp in _chip_copies(ins, outs, sems, None):
                    cp.wait()

    hbm = pl.BlockSpec(memory_space=pl.ANY)
    res = pl.pallas_call(
        body, name="dw_in", grid=(nj,),
        in_specs=[_full((1024, s), lambda j: (0, 0)), _full((s, tn), lambda j: (0, j))] + [hbm] * n,
        out_specs=[_full((1024, tn), lambda j: (0, j))] + [hbm] * n,
        out_shape=[jax.ShapeDtypeStruct((1024, IN_PAD), BF16)] + [jax.ShapeDtypeStruct(a.shape, a.dtype) for a in pairs],
        scratch_shapes=[pltpu.SemaphoreType.DMA((max(2 * n, 2), 4))],
        compiler_params=_cparams(("arbitrary",)),
    )(ht, dproj, *pairs)
    return res[0], res[1:]


def _remote(src, dst, sems, row, k, to):
    return pltpu.make_async_remote_copy(src_ref=src, dst_ref=dst, send_sem=sems.at[row, k], recv_sem=sems.at[row + 1, k],
                                        device_id=to, device_id_type=pl.DeviceIdType.MESH)


def _place():
    x, y, c = lax.axis_index("x"), lax.axis_index("y"), lax.axis_index("c")
    return x, y, c, [(1 - x, y), (x, 1 - y), (1 - x, 1 - y)]


def _gather_parts(ins, outs, sems):
    n = len(ins)
    x, y, c, chips = _place()
    me, sib = (x, y, c), (x, y, 1 - c)
    idx = lambda p: 4 * p[0] + 2 * p[1] + p[2]

    def copy(a, k, block, to, from_input=False):
        src = ins[a] if from_input else outs[a].at[idx(block)]
        return _remote(src, outs[a].at[idx(block)], sems, 2 * a, k, to)

    own = lambda: [pltpu.make_async_copy(ins[a], outs[a].at[idx(me)], sems.at[2 * a, 7]) for a in range(n)]
    first = lambda: ([copy(a, 0, me, sib, True) for a in range(n)]
                     + [copy(a, 1 + j, me, (*chips[j], c), True) for j in range(3) for a in range(n)])
    from_chip = lambda j: [copy(a, 1 + j, (*chips[j], c), me) for a in range(n)]
    passed = lambda j: [copy(a, 4 + j, (*chips[j], c), sib) for a in range(n)]
    from_sibling = lambda: ([copy(a, 0, sib, me) for a in range(n)]
                            + [copy(a, 4 + j, (*chips[j], 1 - c), me) for j in range(3) for a in range(n)])
    return own, first, from_chip, passed, from_sibling


def _gather_start(ins, outs, sems):
    own, first, _, _, _ = _gather_parts(ins, outs, sems)
    for cp in own() + first():
        cp.start()


def _gather_forward(ins, outs, sems):
    _, _, from_chip, passed, _ = _gather_parts(ins, outs, sems)
    for j in range(3):
        for cp in from_chip(j):
            cp.wait_recv()
        for cp in passed(j):
            cp.start()


def _gather_finish(ins, outs, sems):
    own, first, _, passed, from_sibling = _gather_parts(ins, outs, sems)
    for cp in from_sibling():
        cp.wait_recv()
    for cp in first() + [cp for j in range(3) for cp in passed(j)]:
        cp.wait_send()
    for cp in own():
        cp.wait()


def _gather_weights(arrays):
    n = len(arrays)

    def body(*refs):
        ins, outs, sems = refs[:n], refs[n:2 * n], refs[2 * n]
        _gather_start(ins, outs, sems)
        _gather_forward(ins, outs, sems)
        _gather_finish(ins, outs, sems)

    hbm = pl.BlockSpec(memory_space=pl.ANY)
    return pl.pallas_call(
        body, name="gather_weights", in_specs=[hbm] * n, out_specs=[hbm] * n,
        out_shape=[jax.ShapeDtypeStruct((N_DEV,) + a.shape, a.dtype) for a in arrays],
        scratch_shapes=[pltpu.SemaphoreType.DMA((2 * n, N_DEV))],
    )(*arrays)


def _pair_exchange(chunks, name):
    n = len(chunks)

    def body(*refs):
        ins, outs, sems = refs[:n], refs[n:2 * n], refs[2 * n]
        x, y, c, _ = _place()
        sent = [_remote(ins[a].at[2 * q + (1 - c)], outs[a].at[q], sems, 2 * a, q, (x, y, 1 - c))
                for a in range(n) for q in range(4)]
        for cp in sent:
            cp.start()
        for cp in sent:
            cp.wait_recv()
        for cp in sent:
            cp.wait_send()

    hbm = pl.BlockSpec(memory_space=pl.ANY)
    return pl.pallas_call(
        body, name=name, in_specs=[hbm] * n, out_specs=[hbm] * n,
        out_shape=[jax.ShapeDtypeStruct((4,) + a.shape[1:], a.dtype) for a in chunks],
        scratch_shapes=[pltpu.SemaphoreType.DMA((2 * n, 4))],
    )(*chunks)


def _pair_sum(core, chunks, recv, name, tr):
    _, rows, cols = chunks.shape

    def body(c_ref, a_ref, b_ref, o_ref):
        o_ref[...] = (a_ref[...].astype(F32) + b_ref[...].astype(F32)).astype(BF16)

    blk = lambda f: _full((1, tr, cols), f)
    return pl.pallas_call(
        body, name=name, out_shape=jax.ShapeDtypeStruct((4, rows, cols), BF16),
        grid_spec=pltpu.PrefetchScalarGridSpec(
            num_scalar_prefetch=1, grid=(4, rows // tr),
            in_specs=[blk(lambda q, i, c: (2 * q + c[0], i, 0)), blk(lambda q, i, c: (q, i, 0))],
            out_specs=blk(lambda q, i, c: (q, i, 0))),
        compiler_params=_cparams(("parallel", "parallel")),
    )(core, chunks, recv)


def _packet_exchange(packet):
    def body(pk, pk_out, sems):
        x, y, c, _ = _place()
        me = 4 * x + 2 * y + c
        flip = lambda v, b: (1 - v) if b else v

        def small(j, outgoing):
            peer = (flip(x, (j >> 2) & 1), flip(y, (j >> 1) & 1), flip(c, j & 1))
            slot = me if outgoing else 4 * peer[0] + 2 * peer[1] + peer[2]
            return _remote(pk, pk_out.at[slot], sems, 0, j, peer)

        own = pltpu.make_async_copy(pk, pk_out.at[me], sems.at[0, 0])
        sent = [small(j, True) for j in range(1, N_DEV)]
        for cp in [own] + sent:
            cp.start()
        for j in range(1, N_DEV):
            small(j, False).wait_recv()
        for cp in sent:
            cp.wait_send()
        own.wait()

    hbm = pl.BlockSpec(memory_space=pl.ANY)
    return pl.pallas_call(
        body, name="packet_exchange", in_specs=[hbm], out_specs=hbm,
        out_shape=jax.ShapeDtypeStruct((N_DEV,) + packet.shape, packet.dtype),
        scratch_shapes=[pltpu.SemaphoreType.DMA((2, N_DEV))],
    )(packet)


def _adam_math(w, g, m, v):
    m = ADAM_B1 * m + (1.0 - ADAM_B1) * g
    v = ADAM_B2 * v + (1.0 - ADAM_B2) * (g * g)
    m_hat = m / (1.0 - ADAM_B1 ** ADAM_STEP)
    v_hat = v / (1.0 - ADAM_B2 ** ADAM_STEP)
    delta = -ADAM_LR * (m_hat / (jnp.sqrt(v_hat) + ADAM_EPS) + ADAM_WD * w)
    return delta, m, v


def _adam(recv, w, m, v, name, tr):
    _, rows, cols = w.shape

    def body(r_ref, w_ref, m_ref, v_ref, g_out, d_out, m_out, v_out):
        g = r_ref[0].astype(F32)
        for k in range(1, 4):
            g = g + r_ref[k].astype(F32)
        g_out[0] = g
        d_out[0], m_out[0], v_out[0] = _adam_math(w_ref[0], g, m_ref[0], v_ref[0])

    blk = _full((1, tr, cols), lambda i: (0, i, 0))
    return pl.pallas_call(
        body, name=name, grid=(rows // tr,),
        in_specs=[_full((4, tr, cols), lambda i: (0, i, 0)), blk, blk, blk],
        out_specs=[blk] * 4,
        out_shape=[jax.ShapeDtypeStruct(w.shape, F32)] * 4,
        compiler_params=_cparams(("parallel",)),
    )(recv, w, m, v)


def _adam_cols(recv, w, m, v, name, tc):
    cols = w.shape[0]

    def body(r_ref, w_ref, m_ref, v_ref, g_out, d_out, m_out, v_out):
        g = r_ref[0].astype(F32)
        for k in range(1, 4):
            g = g + r_ref[k].astype(F32)
        g_out[...] = g
        d_out[...], m_out[...], v_out[...] = _adam_math(w_ref[...], g, m_ref[...], v_ref[...])

    blk = _full((tc, 8, LANES), lambda i: (i, 0, 0))
    return pl.pallas_call(
        body, name=name, grid=(cols // tc,),
        in_specs=[_full((4, tc, 8, LANES), lambda i: (0, i, 0, 0)), blk, blk, blk],
        out_specs=[blk] * 4,
        out_shape=[jax.ShapeDtypeStruct(w.shape, F32)] * 4,
        compiler_params=_cparams(("parallel",)),
    )(recv, w, m, v)


def _adam_gains(recv, gains, gains_m, gains_v):
    def body(*refs):
        r_ref, w, m, v = refs[0], refs[1:5], refs[5:9], refs[9:13]
        g_out, d_out, m_out, v_out, loss_out = refs[13:17], refs[17:21], refs[21:25], refs[25:29], refs[29]
        tot = r_ref[0:1, :]
        for k in range(1, N_DEV):
            tot = tot + r_ref[k:k + 1, :]
        for t in range(4):
            g = tot[:, GAIN_OFFS[t]:GAIN_OFFS[t] + GAIN_WIDTHS[t]]
            g_out[t][...] = g
            d_out[t][...], m_out[t][...], v_out[t][...] = _adam_math(w[t][...], g, m[t][...], v[t][...])
        loss_out[...] = tot[:, LOSS_OFF:LOSS_OFF + LANES]

    shapes = [jax.ShapeDtypeStruct((1, n), F32) for n in GAIN_WIDTHS]
    return pl.pallas_call(
        body, name="adam_gains", out_shape=shapes * 4 + [jax.ShapeDtypeStruct((1, LANES), F32)],
    )(recv, *gains, *gains_m, *gains_v)


def _local_step(x, positions, gains, w_pad, small_shards, small_weights, target):
    gpre, gq, gkv, gpost = gains
    q_tab, k_tab, d_tab = _rope_tables(positions)

    proj, ht, gathered = _inproj(x, gpre, w_pad, d_tab, small_shards)
    w_uq, w_uk, w_uv, wp_mla, wp_dil, w_out = _assemble_small(*gathered) if small_shards else small_weights
    q, k, v, qt, kt, vt = _mla_prep(proj, gq, gkv, w_uq, w_uk, w_uv, q_tab, k_tab)
    o_mla, ot_mla, lse_mla = _mla_fwd(q, k, vt)

    od, lsed = [], []
    for g, d in enumerate(DIL_DILATIONS):
        o_g, lse_g = _dil_fwd(proj, g, d, "dil_fwd_%d" % g)
        od.append(o_g)
        lsed.append(lse_g)

    merged, ya, yd, o_dil, lse_dil = _merge_fwd(proj, o_mla, od, lsed, wp_mla, wp_dil)
    do, dy, loss, dgpost = _out_loss(merged, w_out, x, target, gpost)

    dproj, do_mla, do_dil, dot_mla, dd_dil, dw_out, dwp_mla, dwp_dil = _merge_bwd(
        do, w_out, merged, proj, ya, yd, o_mla, o_dil, wp_mla, wp_dil)

    dq, dk, dv = _mla_bwd(q, qt, k, kt, v, do_mla, dot_mla, ot_mla, lse_mla)
    dproj, dw_uq, dw_uk, dw_uv, dgq, dgkv = _mla_prep_bwd(dq, dk, dv, proj, dproj, gq, gkv, w_uq, w_uk, w_uv,
                                                          q_tab, k_tab)
    for g, d in enumerate(DIL_DILATIONS):
        dproj = _dil_bwd(proj, dproj, g, do_dil, lse_dil, dd_dil, d_tab, d, "dil_bwd_%d" % g)
    small = _small_chunks(dw_uq, dw_uk, dw_uv, dwp_mla, dwp_dil, dw_out)
    dw_in, small = _dw_in(ht, dproj, _pair_stage(small, 1, "small")) if small_shards else (_dw_in(ht, dproj, [])[0], small)
    return loss, (dproj, dy), (dgq, dgkv, dgpost), dw_in, small


def _pair_stage(chunks, first, name):
    from_sibling = _pair_exchange(chunks, "pair_exchange_" + name)
    core = lax.axis_index("c").astype(jnp.int32).reshape(1)
    return [_pair_sum(core, chunks[t], from_sibling[t], "pair_sum_%d" % (first + t), PAIR_ROWS[first + t])
            for t in range(len(chunks))]


ADAM_ROWS = (256, 384, 256, 512, 512, 128)
PAIR_ROWS = (512, 384, 256, 512, 512, 128)


def kernel(x, positions, pre_norm_g, w_in, q_norm_g, w_uq, kv_norm_g, w_ukv, w_proj_mla, w_proj_dil, w_out, post_norm_g, loss_target, m_pre_norm_g, m_w_in, m_q_norm_g, m_w_uq, m_kv_norm_g, m_w_ukv, m_w_proj_mla, m_w_proj_dil, m_w_out, m_post_norm_g, v_pre_norm_g, v_w_in, v_q_norm_g, v_w_uq, v_kv_norm_g, v_w_ukv, v_w_proj_mla, v_w_proj_dil, v_w_out, v_post_norm_g):
    big_w = (w_in, w_uq, w_ukv, w_proj_mla, w_proj_dil, w_out)
    big_m = (m_w_in, m_w_uq, m_w_ukv, m_w_proj_mla, m_w_proj_dil, m_w_out)
    big_v = (v_w_in, v_w_uq, v_w_ukv, v_w_proj_mla, v_w_proj_dil, v_w_out)
    gains = (pre_norm_g, q_norm_g, kv_norm_g, post_norm_g)
    gains_m = (m_pre_norm_g, m_q_norm_g, m_kv_norm_g, m_post_norm_g)
    gains_v = (v_pre_norm_g, v_q_norm_g, v_kv_norm_g, v_post_norm_g)

    shards = [w[0].astype(BF16) for w in big_w]
    w_pad = _assemble_w_in(_gather_weights(shards[:1])[0])

    loss, (dproj, dy), (dgq, dgkv, dgpost), dw_in, received_small = _local_step(
        x[0], positions[0], gains, w_pad, shards[1:], None, loss_target[0])

    grad_x, dgpre, received_in = _dh_bwd(dproj, w_pad, x[0], pre_norm_g, dy,
                                         _pair_stage([_dw_in_chunks(dw_in)], 0, "in"))
    received = list(received_in) + list(received_small)
    packet = _packet_exchange(jnp.concatenate([dgpre, dgq, dgkv, dgpost, loss[0:1]], axis=1))

    as_cols = lambda a: a[0].T.reshape(SHARD_W, 8, LANES)
    recv_in = received[0].transpose(0, 2, 1).reshape(4, SHARD_W, 8, LANES)
    outs_in = _adam_cols(recv_in, as_cols(w_in), as_cols(m_w_in), as_cols(v_w_in), "adam_0", SHARD_W // 4)
    big = [[o.reshape(SHARD_W, D_MODEL).T[None] for o in outs_in]]
    big += [_adam(received[t], big_w[t], big_m[t], big_v[t], "adam_%d" % t, ADAM_ROWS[t]) for t in range(1, 6)]
    small = _adam_gains(packet.reshape(N_DEV, PACKET), gains, gains_m, gains_v)

    def interleave(kind):
        s_pre, s_q, s_kv, s_post = small[4 * kind:4 * kind + 4]
        b_in, b_uq, b_ukv, b_pm, b_pd, b_out = (big[t][kind] for t in range(6))
        return [s_pre, b_in, s_q, b_uq, s_kv, b_ukv, b_pm, b_pd, b_out, s_post]

    return (small[16][0, 0], grad_x[None], *interleave(0), *interleave(1), *interleave(2), *interleave(3))
```

```python
import numpy as np
import jax
import jax.numpy as jnp
from jax import lax
from jax.experimental import pallas as pl
from jax.experimental.pallas import tpu as pltpu

F32 = jnp.float32
BF16 = jnp.bfloat16

D_MODEL = 1024
NORM_EPS = 1e-6
ROPE_THETA = 500000.0
N_DEV = 8
LANES = 128
NEG = -1e30

MLA_HEADS = 8
MLA_Q_RANK = 384
MLA_KV_RANK = 256
MLA_SCALE = 96.0 ** -0.5
LOG2E = 1.4426950408889634
MLA_QSCALE = MLA_SCALE * LOG2E
MLA_FWD_TK, MLA_FWD_Q_PER_K = 256, 2
MLA_BWD_Q_PER_K = 2
DIL_DILATIONS = (1, 4, 16)
DIL_SCALE = 0.125
Q_BLOCK = 128
DIL_BLOCKS_PER_STEP = 4

Z_MLA, Z_DIL, G_MLA, G_DIL = 0, 512, 1024, 2048
Q_OFF, K_OFF, V_OFF = 3072, 4608, 6144
CQ_OFF, KR_OFF, CKV_OFF, IN_PAD = 7680, 8064, 8192, 8448
IN_WIDTH = 8352
SHARD_W = IN_WIDTH // 8
IN_SEGS = ((0, 384, CQ_OFF), (384, 256, CKV_OFF), (640, 32, KR_OFF), (672, 1536, Q_OFF), (2208, 1536, K_OFF),
           (3744, 1536, V_OFF), (5280, 512, Z_MLA), (5792, 512, Z_DIL), (6304, 1024, G_MLA), (7328, 1024, G_DIL))

GAIN_OFFS = (0, 1024, 1408, 1664)
GAIN_WIDTHS = (1024, 384, 256, 1024)
LOSS_OFF, PACKET = 2688, 2816

ADAM_LR, ADAM_B1, ADAM_B2, ADAM_EPS, ADAM_WD, ADAM_STEP = 0.001, 0.9, 0.999, 1e-08, 0.01, 10

VMEM_LIMIT_MB = 56


def _cparams(sem=None, vmem_mb=VMEM_LIMIT_MB):
    return pltpu.CompilerParams(dimension_semantics=sem, vmem_limit_bytes=vmem_mb * 1024 * 1024)


def _dot(a, b):
    return jnp.dot(a, b, preferred_element_type=F32)


def _dot_nt(a, b):
    return lax.dot_general(a, b, (((1,), (1,)), ((), ())), preferred_element_type=F32)


def _dot_tn(a, b):
    return lax.dot_general(a, b, (((0,), (0,)), ((), ())), preferred_element_type=F32)


def _tile_lanes(t, width):
    return t if width == t.shape[1] else jnp.tile(t, (1, width // t.shape[1]))


def _rope(x, c, sp, sm, a):
    n = x.shape[1]
    return x * c + pltpu.roll(x, a, 1) * sp + pltpu.roll(x, n - a, 1) * sm


def _unrope(dy, c, sp, sm, a):
    n = dy.shape[1]
    return dy * c + pltpu.roll(dy * sp, n - a, 1) + pltpu.roll(dy * sm, a, 1)


def _sigmoid(z):
    return 1.0 / (1.0 + jnp.exp(-z))


def _left_mask():
    return lax.broadcasted_iota(jnp.int32, (1, LANES), 1) < 64


def _expand_half(x, hh, left):
    r = pltpu.roll(x, 64, 1)
    return jnp.where(left, x, r) if hh == 0 else jnp.where(left, r, x)


def _rms(xv, g):
    r = lax.rsqrt(jnp.mean(xv * xv, axis=-1, keepdims=True) + NORM_EPS)
    xh = xv * r
    return xh * g, xh, r


def _rms_bwd(dout, g, xh, r):
    dxh = dout * g
    return r * (dxh - xh * jnp.mean(dxh * xh, axis=-1, keepdims=True))


def _full(shape, index_map):
    return pl.BlockSpec(shape, index_map)


def _w_in_pieces():
    out = []
    for s, n, off in sorted(IN_SEGS, key=lambda t: t[2]):
        c = s
        while c < s + n:
            k = c // SHARD_W
            e = min(s + n, (k + 1) * SHARD_W)
            out.append((k, c - k * SHARD_W, e - c, off + (c - s)))
            c = e
    return out


def _runs(lo, pieces):
    out = []
    for key, a, w, off in pieces:
        s0, s1 = max(off, lo), min(off + w, lo + LANES)
        if s0 < s1:
            out.append((s0 - lo, s1 - s0, key, a + (s0 - off)))
    return out


def _gather_lanes(load_tile, runs, rows):
    lane = lax.broadcasted_iota(jnp.int32, (1, LANES), 1)
    out = jnp.zeros((rows, LANES), F32)
    for o, n, key, a in runs:
        shift = (o - a) % LANES
        for blk in range(a // LANES, (a + n - 1) // LANES + 1):
            t = load_tile(key, blk)
            t = pltpu.roll(t, shift, 1) if shift else t
            lo = max(a, LANES * blk) - a + o
            hi = min(a + n, LANES * (blk + 1)) - a + o
            out = jnp.where(jnp.logical_and(lane >= lo, lane < hi), t, out)
    return out


RELAYOUT_ROWS = 256
LAST_BLOCK = SHARD_W // LANES
LAST_LANES = SHARD_W - LAST_BLOCK * LANES


def _assemble_w_in(g):
    pieces = _w_in_pieces()
    tr = RELAYOUT_ROWS

    def body(g_ref, o_ref, tail):
        def load_tile(k, blk):
            if blk < LAST_BLOCK:
                return g_ref[k, :, blk * LANES:(blk + 1) * LANES].astype(F32)
            tail[...] = jnp.zeros_like(tail)
            tail[:, 0:LAST_LANES] = g_ref[k, :, LAST_BLOCK * LANES:SHARD_W].astype(F32)
            return tail[...]

        for j in range(IN_PAD // LANES):
            o_ref[:, j * LANES:(j + 1) * LANES] = _gather_lanes(load_tile, _runs(j * LANES, pieces), tr).astype(BF16)

    return pl.pallas_call(
        body, name="assemble_w_in", grid=(D_MODEL // tr,),
        in_specs=[_full((N_DEV, tr, SHARD_W), lambda i: (0, i, 0))],
        out_specs=_full((tr, IN_PAD), lambda i: (i, 0)),
        out_shape=jax.ShapeDtypeStruct((D_MODEL, IN_PAD), BF16),
        scratch_shapes=[pltpu.VMEM((tr, LANES), F32)],
        compiler_params=_cparams(("parallel",)),
    )(g)


def _dw_in_chunks(dw):
    tr = RELAYOUT_ROWS
    back = [[(0, off, w, a) for k, a, w, off in _w_in_pieces() if k == dev] for dev in range(N_DEV)]

    def body(dw_ref, o_ref):
        load_tile = lambda key, blk: dw_ref[:, blk * LANES:(blk + 1) * LANES].astype(F32)
        for dev in range(N_DEV):
            for blk in range(LAST_BLOCK + 1):
                t = _gather_lanes(load_tile, _runs(blk * LANES, back[dev]), tr).astype(BF16)
                if blk < LAST_BLOCK:
                    o_ref[dev, :, blk * LANES:(blk + 1) * LANES] = t
                else:
                    o_ref[dev, :, LAST_BLOCK * LANES:SHARD_W] = t[:, 0:LAST_LANES]

    return pl.pallas_call(
        body, name="dw_in_chunks", grid=(D_MODEL // tr,),
        in_specs=[_full((tr, IN_PAD), lambda i: (i, 0))],
        out_specs=_full((N_DEV, tr, SHARD_W), lambda i: (0, i, 0)),
        out_shape=jax.ShapeDtypeStruct((N_DEV, D_MODEL, SHARD_W), BF16),
        compiler_params=_cparams(("parallel",)),
    )(dw)


def _assemble_small(g_uq, g_ukv, g_pm, g_pd, g_out):
    w_uq_pad = jnp.pad(g_uq.transpose(1, 0, 2), ((0, 0), (0, 0), (0, 32))).reshape(384, 1024)
    ukv = g_ukv.transpose(1, 0, 2)
    w_uk_pad = jnp.pad(ukv[:, :, :64], ((0, 0), (0, 0), (0, 64))).reshape(256, 1024)
    w_uv = ukv[:, :, 64:].reshape(256, 512)
    wp_mla = g_pm.transpose(1, 0, 2).reshape(512, 1024)
    wp_dil = g_pd.transpose(1, 0, 2).reshape(512, 1024)
    return w_uq_pad, w_uk_pad, w_uv, wp_mla, wp_dil, g_out.reshape(1024, 1024)


def _small_chunks(dw_uq_pad, dw_uk_pad, dw_uv, dwp_mla, dwp_dil, dw_out):
    b = dw_uq_pad.reshape(384, 8, 128)[:, :, :96].transpose(1, 0, 2)
    c = jnp.concatenate([dw_uk_pad.reshape(256, 8, 128)[:, :, :64], dw_uv.reshape(256, 8, 64)], axis=2)
    c = c.transpose(1, 0, 2)
    d = dwp_mla.reshape(512, N_DEV, 128).transpose(1, 0, 2)
    e = dwp_dil.reshape(512, N_DEV, 128).transpose(1, 0, 2)
    f = dw_out.reshape(N_DEV, 128, 1024)
    return [t.astype(BF16) for t in (b, c, d, e, f)]


def _lane_consts(freqs, half, first, period):
    rel = (np.arange(LANES) % period) - first
    rot = (rel >= 0) & (rel < 2 * half)
    freq = np.where(rot, freqs[np.clip(rel, 0, 2 * half - 1) % half], 0.0).astype(np.float32)
    x1 = (rot & (rel < half)).astype(np.float32)
    x2 = (rot & (rel >= half)).astype(np.float32)
    return freq[None, :], x1[None, :], x2[None, :]


def _rope_tables(pos):
    p = pos.astype(F32)[:, None]
    inv_m = np.float32(ROPE_THETA) ** (-(np.arange(0, 32, 2, dtype=np.float32) / np.float32(32)))
    inv_d = np.float32(ROPE_THETA) ** (-(np.arange(0, 16, 2, dtype=np.float32) / np.float32(16)))
    lane = np.arange(LANES)
    tabs = []
    for freqs, half, first, period, keep in ((inv_m, 16, 64, 128, lane < 96), (inv_m, 16, 0, 128, lane < 32),
                                              (inv_d, 8, 0, 64, lane >= 0)):
        freq, x1, x2 = _lane_consts(freqs, half, first, period)
        ang = p * freq
        sin = jnp.sin(ang)
        tabs.append((jnp.cos(ang) * keep.astype(np.float32)[None, :], sin * x2, sin * (-x1)))
    return tuple(tabs)


def _inproj(x, gpre, w_pad, d_tab, shards):
    s = x.shape[0]
    tm, tn = min(1024, s), 768
    ni, nj = s // tm, IN_PAD // tn
    rope_lo, rope_hi = Q_OFF // tn, V_OFF // tn
    n = len(shards)
    forward_step = min(nj, ni * nj - 2)

    def body(*refs):
        x_ref, g_ref, w_ref, c_ref, sp_ref, sm_ref = refs[:6]
        ins, o_ref, ht_ref, outs = refs[6:6 + n], refs[6 + n], refs[7 + n], refs[8 + n:8 + 2 * n]
        h_ref, sems = refs[8 + 2 * n], refs[9 + 2 * n]
        i, j = pl.program_id(0), pl.program_id(1)
        step = i * nj + j

        @pl.when(j == 0)
        def _():
            hv, _, _ = _rms(x_ref[...], g_ref[...])
            h_ref[...] = hv.astype(BF16)
            ht_ref[...] = hv.astype(BF16).T

        if n:
            @pl.when(step == 0)
            def _():
                _gather_start(ins, outs, sems)

            @pl.when(step == forward_step)
            def _():
                _gather_forward(ins, outs, sems)

        acc = _dot(h_ref[...], w_ref[...])
        is_rope = jnp.logical_and(j >= rope_lo, j < rope_hi)

        @pl.when(is_rope)
        def _():
            o_ref[...] = _rope(acc, _tile_lanes(c_ref[...], tn), _tile_lanes(sp_ref[...], tn),
                               _tile_lanes(sm_ref[...], tn), 8).astype(BF16)

        @pl.when(jnp.logical_not(is_rope))
        def _():
            o_ref[...] = acc.astype(BF16)

        if n:
            @pl.when(step == ni * nj - 1)
            def _():
                _gather_finish(ins, outs, sems)

    row = lambda i, j: (i, 0)
    hbm = pl.BlockSpec(memory_space=pl.ANY)
    res = pl.pallas_call(
        body, name="inproj", grid=(ni, nj),
        in_specs=[_full((tm, D_MODEL), row), _full((1, D_MODEL), lambda i, j: (0, 0)),
                  _full((D_MODEL, tn), lambda i, j: (0, j)),
                  _full((tm, LANES), row), _full((tm, LANES), row), _full((tm, LANES), row)] + [hbm] * n,
        out_specs=[_full((tm, tn), lambda i, j: (i, j)), _full((D_MODEL, tm), lambda i, j: (0, i))] + [hbm] * n,
        out_shape=[jax.ShapeDtypeStruct((s, IN_PAD), BF16), jax.ShapeDtypeStruct((D_MODEL, s), BF16)]
        + [jax.ShapeDtypeStruct((N_DEV,) + a.shape, a.dtype) for a in shards],
        scratch_shapes=[pltpu.VMEM((tm, D_MODEL), BF16), pltpu.SemaphoreType.DMA((max(2 * n, 2), N_DEV))],
        compiler_params=_cparams(("arbitrary", "arbitrary")),
    )(x, gpre, w_pad, *d_tab, *shards)
    return res[0], res[1], res[2:]


def _mla_prep(proj, gq, gkv, w_uq, w_uk, w_uv, q_tab, k_tab):
    s = proj.shape[0]
    tm = min(512, s)

    def body(cq_ref, kr_ref, ckv_ref, gq_ref, gkv_ref, wq_ref, wk_ref, wv_ref,
             qc, qsp, qsm, kc, ksp, ksm, q_out, k_out, v_out, qt_out, kt_out, vt_out):
        cqn, _, _ = _rms(cq_ref[...].astype(F32), gq_ref[...])
        q = _dot(cqn.astype(BF16), wq_ref[...])
        q = _rope(q, _tile_lanes(qc[...], 1024), _tile_lanes(qsp[...], 1024), _tile_lanes(qsm[...], 1024), 16)
        q = q * MLA_QSCALE
        q_out[...] = q.astype(BF16)
        qt_out[...] = q.T.astype(BF16)
        ckvn, _, _ = _rms(ckv_ref[...].astype(F32), gkv_ref[...])
        ckvn = ckvn.astype(BF16)
        kr = _rope(kr_ref[...].astype(F32), kc[...], ksp[...], ksm[...], 16)
        k = _dot(ckvn, wk_ref[...]) + _tile_lanes(pltpu.roll(kr, 64, 1), 1024)
        k_out[...] = k.astype(BF16)
        kt_out[...] = k.T.astype(BF16)
        v = _dot(ckvn, wv_ref[...])
        v_out[...] = v.astype(BF16)
        vt_out[...] = v.T.astype(BF16)

    row = lambda i: (i, 0)
    col = lambda i: (0, i)
    cst = lambda i: (0, 0)
    tabs = [_full((tm, LANES), row)] * 6
    return pl.pallas_call(
        body, name="mla_prep", grid=(s // tm,),
        in_specs=[_full((tm, 384), lambda i: (i, CQ_OFF // 384)), _full((tm, 128), lambda i: (i, KR_OFF // 128)),
                  _full((tm, 256), lambda i: (i, CKV_OFF // 256)), _full((1, 384), cst), _full((1, 256), cst),
                  _full((384, 1024), cst), _full((256, 1024), cst), _full((256, 512), cst)] + tabs,
        out_specs=[_full((tm, 1024), row), _full((tm, 1024), row), _full((tm, 512), row),
                   _full((1024, tm), col), _full((1024, tm), col), _full((512, tm), col)],
        out_shape=[jax.ShapeDtypeStruct((s, 1024), BF16), jax.ShapeDtypeStruct((s, 1024), BF16),
                   jax.ShapeDtypeStruct((s, 512), BF16), jax.ShapeDtypeStruct((1024, s), BF16),
                   jax.ShapeDtypeStruct((1024, s), BF16), jax.ShapeDtypeStruct((512, s), BF16)],
        compiler_params=_cparams(("parallel",)),
    )(proj, proj, proj, gq, gkv, w_uq, w_uk, w_uv, *q_tab, *k_tab)


def _mla_fwd(q, k, vt):
    s = q.shape[0]
    tk = min(MLA_FWD_TK, s)
    ratio = MLA_FWD_Q_PER_K if s >= MLA_FWD_Q_PER_K * tk else 1
    tq = ratio * tk
    nq = s // tq

    def body(q_ref, k_ref, vt_ref, o_ref, ot_ref, lse_ref):
        krow = lax.broadcasted_iota(jnp.int32, (tk, tq), 0)
        qcol = lax.broadcasted_iota(jnp.int32, (tk, tq), 1)

        def q_step(i, _):
            r0 = pl.multiple_of(i * tq, tq)
            qs = [q_ref[pl.ds(r0, tq), hh * 128:(hh + 1) * 128] for hh in range(2)]

            def scores(j):
                c0 = pl.multiple_of(j * tk, tk)
                return tuple(_dot_nt(k_ref[pl.ds(c0, tk), hh * 128:(hh + 1) * 128], qs[hh])
                             for hh in range(2))

            def update(j, sts, stats, masked):
                c0 = pl.multiple_of(j * tk, tk)
                new = []
                causal = (krow + (c0 - r0)) <= qcol
                for hh in range(2):
                    m, l, acc = stats[hh]
                    st = jnp.where(causal, sts[hh], NEG) if masked else sts[hh]
                    m_new = jnp.maximum(m, jnp.max(st, axis=0, keepdims=True))
                    alpha = jnp.exp2(m - m_new)
                    p = jnp.exp2(st - m_new)
                    l = alpha * l + jnp.sum(p, axis=0, keepdims=True)
                    acc = acc * alpha + _dot(vt_ref[hh * 64:(hh + 1) * 64, pl.ds(c0, tk)], p.astype(BF16))
                    new.append((m_new, l, acc))
                return tuple(new)

            init = tuple((jnp.full((1, tq), NEG, F32), jnp.zeros((1, tq), F32), jnp.zeros((64, tq), F32))
                         for _ in range(2))
            stats = lax.fori_loop(0, ratio * i, lambda j, st: update(j, scores(j), st, False), init)
            for d in range(ratio):
                stats = update(ratio * i + d, scores(ratio * i + d), stats, True)
            (ma, la, acca), (mb, lb, accb) = stats
            ot = jnp.concatenate([acca / la, accb / lb], axis=0)
            ot_ref[:, pl.ds(r0, tq)] = ot.astype(BF16)
            o_ref[pl.ds(r0, tq), :] = ot.T.astype(BF16)
            lse_ref[:, pl.ds(r0, tq)] = jnp.concatenate(
                [ma + jnp.log2(la), mb + jnp.log2(lb), jnp.zeros((6, tq), F32)], axis=0)
            return 0

        lax.fori_loop(0, nq, q_step, 0)

    return pl.pallas_call(
        body, name="mla_fwd", grid=(4,),
        in_specs=[_full((s, 256), lambda p: (0, p)), _full((s, 256), lambda p: (0, p)),
                  _full((128, s), lambda p: (p, 0))],
        out_specs=[_full((s, 128), lambda p: (0, p)), _full((128, s), lambda p: (p, 0)),
                   _full((8, s), lambda p: (p, 0))],
        out_shape=[jax.ShapeDtypeStruct((s, 512), BF16), jax.ShapeDtypeStruct((512, s), BF16),
                   jax.ShapeDtypeStruct((32, s), F32)],
        compiler_params=_cparams(("parallel",)),
    )(q, k, vt)


def _band_mask(has_prev):
    r = lax.broadcasted_iota(jnp.int32, (Q_BLOCK, 2 * Q_BLOCK), 0)
    c = lax.broadcasted_iota(jnp.int32, (Q_BLOCK, 2 * Q_BLOCK), 1)
    lo = jnp.where(has_prev, r, Q_BLOCK)
    return jnp.logical_and(c >= lo, c <= r + Q_BLOCK)


def _dil_rows(b, d, per_seq):
    r, n = b // per_seq, b % per_seq
    start = r + (d * Q_BLOCK) * n
    prev = start - jnp.where(n > 0, d * Q_BLOCK, 0)
    if d == 1:
        return pl.ds(pl.multiple_of(start, Q_BLOCK), Q_BLOCK), pl.ds(pl.multiple_of(prev, Q_BLOCK), Q_BLOCK)
    return pl.ds(start, Q_BLOCK, stride=d), pl.ds(prev, Q_BLOCK, stride=d)


def _dil_fwd(proj, g, d, name):
    s = proj.shape[0]
    nblk = s // Q_BLOCK
    per_seq = nblk // d

    def body(q_ref, k_ref, v_ref, o_ref, lse_ref, qf, kf, vf, of, lf):
        left = _left_mask()
        hms = (left, jnp.logical_not(left))
        qf[...] = q_ref[...].astype(F32)
        kf[...] = k_ref[...].astype(F32)
        vf[...] = v_ref[...].astype(F32)

        def scores(b):
            rows, prow = _dil_rows(b, d, per_seq)
            qb = qf[rows, :].astype(BF16)
            kk = jnp.concatenate([kf[prow, :], kf[rows, :]], axis=0).astype(BF16)
            return [_dot_nt(jnp.where(hms[hh], qb, jnp.zeros_like(qb)), kk) * DIL_SCALE for hh in range(2)]

        def finish(b, tiles):
            rows, prow = _dil_rows(b, d, per_seq)
            mask = _band_mask((b % per_seq) > 0)
            vv = jnp.concatenate([vf[prow, :], vf[rows, :]], axis=0).astype(BF16)
            outs = []
            for hh in range(2):
                sc = jnp.where(mask, tiles[hh], NEG)
                m = jnp.max(sc, axis=1, keepdims=True)
                p = jnp.exp(sc - m)
                l = jnp.sum(p, axis=1, keepdims=True)
                acc = _dot(p.astype(BF16), jnp.where(hms[hh], vv, jnp.zeros_like(vv)))
                outs.append((acc / l, jnp.broadcast_to(m + jnp.log(l), (Q_BLOCK, LANES))))
            of[rows, :] = outs[0][0] + outs[1][0]
            lf[rows, :] = jnp.where(left, outs[0][1], outs[1][1])

        def step(t, _):
            tiles = [scores(DIL_BLOCKS_PER_STEP * t + u) for u in range(DIL_BLOCKS_PER_STEP)]
            for u in range(DIL_BLOCKS_PER_STEP):
                finish(DIL_BLOCKS_PER_STEP * t + u, tiles[u])
            return 0

        lax.fori_loop(0, nblk // DIL_BLOCKS_PER_STEP, step, 0)
        o_ref[...] = of[...].astype(BF16)
        lse_ref[...] = lf[...]

    blk = lambda off: _full((s, 128), lambda p, off=off: (0, off + p))
    qo, ko, vo = ((off + 512 * g) // 128 for off in (Q_OFF, K_OFF, V_OFF))
    return pl.pallas_call(
        body, name=name, grid=(4,),
        in_specs=[blk(qo), blk(ko), blk(vo)],
        out_specs=[blk(0), blk(0)],
        out_shape=[jax.ShapeDtypeStruct((s, 512), BF16), jax.ShapeDtypeStruct((s, 512), F32)],
        scratch_shapes=[pltpu.VMEM((s, 128), F32)] * 5,
        compiler_params=_cparams(("parallel",)),
    )(proj, proj, proj)


def _merge_fwd(proj, o_mla, od, lsed, wp_mla, wp_dil):
    s = proj.shape[0]
    tm = min(512, s)

    def body(zm_ref, zd_ref, gm_ref, gd_ref, om_ref, o0, o1, o2, l0, l1, l2, wm_ref, wd_ref,
             mg_out, ya_out, yd_out, odil_out, lse_out):
        la, lb, lc = l0[...], l1[...], l2[...]
        lmax = jnp.maximum(jnp.maximum(la, lb), lc)
        ea, eb, ec = jnp.exp(la - lmax), jnp.exp(lb - lmax), jnp.exp(lc - lmax)
        den = ea + eb + ec
        o_dil = (ea * o0[...].astype(F32) + eb * o1[...].astype(F32) + ec * o2[...].astype(F32)) / den
        o_dil = o_dil.astype(BF16)
        odil_out[...] = o_dil
        lse_out[...] = lmax + jnp.log(den)
        zm, zd = zm_ref[...].astype(F32), zd_ref[...].astype(F32)
        pa = (om_ref[...].astype(F32) * (zm * _sigmoid(zm))).astype(BF16)
        pd = (o_dil.astype(F32) * (zd * _sigmoid(zd))).astype(BF16)
        ya = _dot(pa, wm_ref[...])
        yd = _dot(pd, wd_ref[...])
        ya_out[...] = ya.astype(BF16)
        yd_out[...] = yd.astype(BF16)
        mg_out[...] = (_sigmoid(gm_ref[...].astype(F32)) * ya + _sigmoid(gd_ref[...].astype(F32)) * yd).astype(BF16)

    row = lambda i: (i, 0)
    cst = lambda i: (0, 0)
    r512 = _full((tm, 512), row)
    r1024 = _full((tm, 1024), row)
    return pl.pallas_call(
        body, name="merge_fwd", grid=(s // tm,),
        in_specs=[_full((tm, 512), lambda i: (i, Z_MLA // 512)), _full((tm, 512), lambda i: (i, Z_DIL // 512)),
                  _full((tm, 1024), lambda i: (i, G_MLA // 1024)), _full((tm, 1024), lambda i: (i, G_DIL // 1024)),
                  r512, r512, r512, r512, r512, r512, r512, _full((512, 1024), cst), _full((512, 1024), cst)],
        out_specs=[r1024, r1024, r1024, r512, r512],
        out_shape=[jax.ShapeDtypeStruct((s, 1024), BF16), jax.ShapeDtypeStruct((s, 1024), BF16),
                   jax.ShapeDtypeStruct((s, 1024), BF16), jax.ShapeDtypeStruct((s, 512), BF16),
                   jax.ShapeDtypeStruct((s, 512), F32)],
        compiler_params=_cparams(("parallel",)),
    )(proj, proj, proj, proj, o_mla, *od, *lsed, wp_mla, wp_dil)


def _out_loss(merged, w_out, x, target, gpost):
    s = x.shape[0]
    tm = min(512, s)

    def body(mg_ref, w_ref, x_ref, t_ref, g_ref, do_out, dy_out, loss_out, dg_out):
        i = pl.program_id(0)

        @pl.when(i == 0)
        def _():
            loss_out[...] = jnp.zeros_like(loss_out)
            dg_out[...] = jnp.zeros_like(dg_out)

        o = _dot(mg_ref[...], w_ref[...])
        g = g_ref[...]
        n, u, r = _rms(o, g)
        e = (x_ref[...] + n) - t_ref[...]
        loss_out[...] += 0.5 * jnp.sum(jnp.mean(e * e, axis=-1, keepdims=True))
        dy = e * (1.0 / D_MODEL)
        dy_out[...] = dy
        dg_out[...] += jnp.sum(dy * u, axis=0, keepdims=True)
        do_out[...] = _rms_bwd(dy, g, u, r).astype(BF16)

    row = lambda i: (i, 0)
    cst = lambda i: (0, 0)
    return pl.pallas_call(
        body, name="out_loss", grid=(s // tm,),
        in_specs=[_full((tm, 1024), row), _full((1024, 1024), cst), _full((tm, 1024), row), _full((tm, 1024), row),
                  _full((1, 1024), cst)],
        out_specs=[_full((tm, 1024), row), _full((tm, 1024), row), _full((8, LANES), cst), _full((1, 1024), cst)],
        out_shape=[jax.ShapeDtypeStruct((s, 1024), BF16), jax.ShapeDtypeStruct((s, 1024), F32),
                   jax.ShapeDtypeStruct((8, LANES), F32), jax.ShapeDtypeStruct((1, 1024), F32)],
        compiler_params=_cparams(("arbitrary",)),
    )(merged, w_out, x, target, gpost)


def _seg_sum64(x, ones_bd):
    hi = x.astype(BF16)
    lo = (x - hi.astype(F32)).astype(BF16)
    return _dot(hi, ones_bd) + _dot(lo, ones_bd)


def _merge_bwd(do, w_out, merged, proj, ya, yd, o_mla, o_dil, wp_mla, wp_dil):
    s = do.shape[0]
    tm = min(256, s)
    seg = jnp.arange(512) // 64
    ones_bd = (seg[:, None] == seg[None, :]).astype(BF16)

    def body(do_ref, wo_ref, mg_ref, zm_ref, zd_ref, gm_ref, gd_ref, ya_ref, yd_ref, om_ref, od_ref, wm_ref, wd_ref,
             bd_ref, dp_out, dom_out, dod_out, domt_out, dd_out, dwo_out, dwm_out, dwd_out):
        i = pl.program_id(0)

        @pl.when(i == 0)
        def _():
            dwo_out[...] = jnp.zeros_like(dwo_out)
            dwm_out[...] = jnp.zeros_like(dwm_out)
            dwd_out[...] = jnp.zeros_like(dwd_out)

        dov = do_ref[...]
        dwo_out[...] += _dot_tn(mg_ref[...], dov)
        dm = _dot_nt(dov, wo_ref[...])
        for g_ref, y_ref, z_ref, o_ref, w_ref, z_off, g_off, dob_out, dd_o, dw_out in (
                (gm_ref, ya_ref, zm_ref, om_ref, wm_ref, Z_MLA, G_MLA, dom_out, None, dwm_out),
                (gd_ref, yd_ref, zd_ref, od_ref, wd_ref, Z_DIL, G_DIL, dod_out, dd_out, dwd_out)):
            sg = _sigmoid(g_ref[...].astype(F32))
            dp_out[:, g_off:g_off + 1024] = (dm * y_ref[...].astype(F32) * sg * (1.0 - sg)).astype(BF16)
            dy = (dm * sg).astype(BF16)
            z = z_ref[...].astype(F32)
            sz = _sigmoid(z)
            silu = z * sz
            ob = o_ref[...].astype(F32)
            dw_out[...] += _dot_tn((ob * silu).astype(BF16), dy)
            dp = _dot_nt(dy, w_ref[...])
            dp_out[:, z_off:z_off + 512] = (dp * ob * (sz * (1.0 + z * (1.0 - sz)))).astype(BF16)
            dob = dp * silu
            dob_out[...] = dob.astype(BF16)
            if dd_o is None:
                domt_out[...] = dob.T.astype(BF16)
            else:
                dd_o[...] = _seg_sum64(dob * ob, bd_ref[...])

    row = lambda i: (i, 0)
    cst = lambda i: (0, 0)
    r512 = _full((tm, 512), row)
    r1024 = _full((tm, 1024), row)
    return pl.pallas_call(
        body, name="merge_bwd", grid=(s // tm,),
        in_specs=[r1024, _full((1024, 1024), cst), r1024,
                  _full((tm, 512), lambda i: (i, Z_MLA // 512)), _full((tm, 512), lambda i: (i, Z_DIL // 512)),
                  _full((tm, 1024), lambda i: (i, G_MLA // 1024)), _full((tm, 1024), lambda i: (i, G_DIL // 1024)),
                  r1024, r1024, r512, r512, _full((512, 1024), cst), _full((512, 1024), cst), _full((512, 512), cst)],
        out_specs=[_full((tm, Q_OFF), row), r512, r512, _full((512, tm), lambda i: (0, i)), r512,
                   _full((1024, 1024), cst), _full((512, 1024), cst), _full((512, 1024), cst)],
        out_shape=[jax.ShapeDtypeStruct((s, IN_PAD), BF16),
                   jax.ShapeDtypeStruct((s, 512), BF16), jax.ShapeDtypeStruct((s, 512), BF16),
                   jax.ShapeDtypeStruct((512, s), BF16), jax.ShapeDtypeStruct((s, 512), F32),
                   jax.ShapeDtypeStruct((1024, 1024), F32), jax.ShapeDtypeStruct((512, 1024), F32),
                   jax.ShapeDtypeStruct((512, 1024), F32)],
        compiler_params=_cparams(("arbitrary",)),
    )(do, w_out, merged, proj, proj, proj, proj, ya, yd, o_mla, o_dil, wp_mla, wp_dil, ones_bd)


def _mla_bwd(q, qt, k, kt, v, do, dot, ot, lse):
    s = q.shape[0]
    tk = min(256, s)
    ratio = MLA_BWD_Q_PER_K if s >= MLA_BWD_Q_PER_K * tk else 1
    tq = ratio * tk
    nq, nk = s // tq, s // tk

    def body(q_ref, qt_ref, k_ref, kt_ref, v_ref, do_ref, dot_ref, ot_ref, lse_ref, dqt_out, dkt_out, dvt_out,
             dqt_acc):
        left = _left_mask()
        krow = lax.broadcasted_iota(jnp.int32, (tk, tq), 0)
        qcol = lax.broadcasted_iota(jnp.int32, (tk, tq), 1)
        dqt_acc[...] = jnp.zeros_like(dqt_acc)

        def kv_step(j, _):
            c0 = pl.multiple_of(j * tk, tk)
            vv = v_ref[pl.ds(c0, tk), :]
            khs = [k_ref[pl.ds(c0, tk), hh * 128:(hh + 1) * 128] for hh in range(2)]
            kths = [kt_ref[hh * 128:(hh + 1) * 128, pl.ds(c0, tk)] for hh in range(2)]
            vms = [jnp.where(left if hh == 0 else jnp.logical_not(left), vv, jnp.zeros_like(vv)) for hh in range(2)]

            def scores(i):
                r0 = pl.multiple_of(jnp.minimum(i, nq - 1) * tq, tq)
                dov = do_ref[pl.ds(r0, tq), :]
                return tuple((_dot_nt(khs[hh], q_ref[pl.ds(r0, tq), hh * 128:(hh + 1) * 128]),
                              _dot_nt(vms[hh], dov)) for hh in range(2))

            def update(i, tiles, acc, masked):
                r0 = pl.multiple_of(i * tq, tq)
                new = []
                for hh in range(2):
                    dkt, dvt = acc[hh]
                    st, dp = tiles[hh]
                    hrows = slice(hh * 128, (hh + 1) * 128)
                    drows = slice(hh * 64, (hh + 1) * 64)
                    doth = dot_ref[drows, pl.ds(r0, tq)]
                    dd = jnp.sum(doth.astype(F32) * ot_ref[drows, pl.ds(r0, tq)].astype(F32), axis=0, keepdims=True)
                    p = jnp.exp2(st - lse_ref[hh:hh + 1, pl.ds(r0, tq)])
                    if masked:
                        p = jnp.where((krow + (c0 - r0)) <= qcol, p, 0.0)
                    ds = (p * (dp - dd)).astype(BF16)
                    dvt = dvt + _dot_nt(doth, p.astype(BF16))
                    dkt = dkt + _dot_nt(qt_ref[hrows, pl.ds(r0, tq)], ds)
                    dqt_acc[hrows, pl.ds(r0, tq)] += _dot(kths[hh], ds)
                    new.append((dkt, dvt))
                return tuple(new)

            init = tuple((jnp.zeros((128, tk), F32), jnp.zeros((64, tk), F32)) for _ in range(2))
            i0 = j // ratio
            acc = update(i0, scores(i0), init, True)
            acc = lax.fori_loop(i0 + 1, nq, lambda i, a: update(i, scores(i), a, False), acc)
            for hh in range(2):
                dkt_out[hh * 128:(hh + 1) * 128, pl.ds(c0, tk)] = (acc[hh][0] * (1.0 / LOG2E)).astype(BF16)
                dvt_out[hh * 64:(hh + 1) * 64, pl.ds(c0, tk)] = acc[hh][1].astype(BF16)
            return 0

        lax.fori_loop(0, nk, kv_step, 0)
        dqt_out[...] = (dqt_acc[...] * MLA_SCALE).astype(BF16)

    b256 = _full((s, 256), lambda p: (0, p))
    b128 = _full((s, 128), lambda p: (0, p))
    t256 = _full((256, s), lambda p: (p, 0))
    t128 = _full((128, s), lambda p: (p, 0))
    return pl.pallas_call(
        body, name="mla_bwd", grid=(4,),
        in_specs=[b256, t256, b256, t256, b128, b128, t128, t128, _full((8, s), lambda p: (p, 0))],
        out_specs=[t256, t256, t128],
        out_shape=[jax.ShapeDtypeStruct((1024, s), BF16), jax.ShapeDtypeStruct((1024, s), BF16),
                   jax.ShapeDtypeStruct((512, s), BF16)],
        scratch_shapes=[pltpu.VMEM((256, s), F32)],
        compiler_params=_cparams(("parallel",)),
    )(q, qt, k, kt, v, do, dot, ot, lse)


def _mla_prep_bwd(dq, dk, dv, proj, dproj, gq, gkv, w_uq, w_uk, w_uv, q_tab, k_tab):
    s = proj.shape[0]
    tm = min(256, s)

    def body(dqt_ref, dkt_ref, dvt_ref, cq_ref, ckv_ref, dp_in, gq_ref, gkv_ref, wq_ref, wk_ref, wv_ref,
             qc, qsp, qsm, kc, ksp, ksm,
             dp_out, dwq_out, dwk_out, dwv_out, dgq_out, dgkv_out):
        i = pl.program_id(0)

        @pl.when(i == 0)
        def _():
            for r in (dwq_out, dwk_out, dwv_out, dgq_out, dgkv_out):
                r[...] = jnp.zeros_like(r)

        dqu = _unrope(dqt_ref[...].astype(F32).T, _tile_lanes(qc[...], 1024), _tile_lanes(qsp[...], 1024),
                      _tile_lanes(qsm[...], 1024), 16).astype(BF16)
        gq = gq_ref[...]
        cqn, xh, r = _rms(cq_ref[...].astype(F32), gq)
        dwq_out[...] += _dot_tn(cqn.astype(BF16), dqu)
        dcqn = _dot_nt(dqu, wq_ref[...])
        dgq_out[...] += jnp.sum(dcqn * xh, axis=0, keepdims=True)
        dp_out[:, 0:384] = _rms_bwd(dcqn, gq, xh, r).astype(BF16)

        dkf = dkt_ref[...].astype(F32).T
        dkb = dkf.astype(BF16)
        dsum = dkf[:, 0:128]
        for h in range(1, MLA_HEADS):
            dsum = dsum + dkf[:, h * 128:(h + 1) * 128]
        dp_out[:, 384:512] = _unrope(pltpu.roll(dsum, 64, 1), kc[...], ksp[...], ksm[...], 16).astype(BF16)

        dvb = dvt_ref[...].astype(F32).T.astype(BF16)
        gkv = gkv_ref[...]
        ckvn, xh2, r2 = _rms(ckv_ref[...].astype(F32), gkv)
        ckvn = ckvn.astype(BF16)
        dwk_out[...] += _dot_tn(ckvn, dkb)
        dwv_out[...] += _dot_tn(ckvn, dvb)
        dckvn = _dot_nt(dkb, wk_ref[...]) + _dot_nt(dvb, wv_ref[...])
        dgkv_out[...] += jnp.sum(dckvn * xh2, axis=0, keepdims=True)
        dp_out[:, 512:768] = _rms_bwd(dckvn, gkv, xh2, r2).astype(BF16)

    row = lambda i: (i, 0)
    cst = lambda i: (0, 0)
    tabs = [_full((tm, LANES), row)] * 6
    return pl.pallas_call(
        body, name="mla_prep_bwd", grid=(s // tm,),
        in_specs=[_full((1024, tm), lambda i: (0, i)), _full((1024, tm), lambda i: (0, i)),
                  _full((512, tm), lambda i: (0, i)),
                  _full((tm, 384), lambda i: (i, CQ_OFF // 384)), _full((tm, 256), lambda i: (i, CKV_OFF // 256)),
                  pl.BlockSpec(memory_space=pl.ANY), _full((1, 384), cst), _full((1, 256), cst),
                  _full((384, 1024), cst), _full((256, 1024), cst), _full((256, 512), cst)] + tabs,
        out_specs=[_full((tm, IN_PAD - CQ_OFF), lambda i: (i, CQ_OFF // (IN_PAD - CQ_OFF))),
                   _full((384, 1024), cst), _full((256, 1024), cst), _full((256, 512), cst),
                   _full((1, 384), cst), _full((1, 256), cst)],
        out_shape=[jax.ShapeDtypeStruct((s, IN_PAD), BF16),
                   jax.ShapeDtypeStruct((384, 1024), F32), jax.ShapeDtypeStruct((256, 1024), F32),
                   jax.ShapeDtypeStruct((256, 512), F32),
                   jax.ShapeDtypeStruct((1, 384), F32), jax.ShapeDtypeStruct((1, 256), F32)],
        input_output_aliases={5: 0},
        compiler_params=_cparams(("arbitrary",)),
    )(dq, dk, dv, proj, proj, dproj, gq, gkv, w_uq, w_uk, w_uv, *q_tab, *k_tab)


def _dil_bwd(proj, dproj, g, do, lse, dd, tabs, d, name):
    s = proj.shape[0]
    nblk = s // Q_BLOCK
    per_seq = nblk // d

    def body(q_ref, k_ref, v_ref, do_ref, lse_ref, dd_ref, c_ref, sp_ref, sm_ref, dp_in, dp_out,
             qf, kf, vf, dof, dq_acc, dk_acc, dv_acc, staged, sems):
        left = _left_mask()
        hms = (left, jnp.logical_not(left))
        qf[...] = q_ref[...].astype(F32)
        kf[...] = k_ref[...].astype(F32)
        vf[...] = v_ref[...].astype(F32)
        dof[...] = do_ref[...].astype(F32)
        dk_acc[...] = jnp.zeros_like(dk_acc)
        dv_acc[...] = jnp.zeros_like(dv_acc)

        def scores(b):
            rows, prow = _dil_rows(b, d, per_seq)
            qb, dob = qf[rows, :].astype(BF16), dof[rows, :].astype(BF16)
            kk = jnp.concatenate([kf[prow, :], kf[rows, :]], axis=0).astype(BF16)
            vv = jnp.concatenate([vf[prow, :], vf[rows, :]], axis=0).astype(BF16)
            zero = jnp.zeros_like(qb)
            out = []
            for hh in range(2):
                qm, dom = jnp.where(hms[hh], qb, zero), jnp.where(hms[hh], dob, zero)
                out.append((_dot_nt(qm, kk) * DIL_SCALE, _dot_nt(dom, vv)))
            return out

        def finish(b, tiles):
            rows, prow = _dil_rows(b, d, per_seq)
            mask = _band_mask((b % per_seq) > 0)
            qb, dob = qf[rows, :].astype(BF16), dof[rows, :].astype(BF16)
            kk = jnp.concatenate([kf[prow, :], kf[rows, :]], axis=0).astype(BF16)
            lse_b, dd_b = lse_ref[rows, :], dd_ref[rows, :]
            zero = jnp.zeros_like(qb)
            dq = jnp.zeros((Q_BLOCK, LANES), F32)
            dk = jnp.zeros((2 * Q_BLOCK, LANES), F32)
            dv = jnp.zeros((2 * Q_BLOCK, LANES), F32)
            for hh in range(2):
                hm = hms[hh]
                qm, dom = jnp.where(hm, qb, zero), jnp.where(hm, dob, zero)
                lse_h = _tile_lanes(_expand_half(lse_b, hh, left), 2 * Q_BLOCK)
                dd_h = _tile_lanes(_expand_half(dd_b, hh, left), 2 * Q_BLOCK)
                sc, dp = tiles[hh]
                p = jnp.where(mask, jnp.exp(sc - lse_h), 0.0)
                ds = (p * (dp - dd_h) * DIL_SCALE).astype(BF16)
                dq = dq + _dot(ds, jnp.where(hm, kk, jnp.zeros_like(kk)))
                dk = dk + _dot_tn(ds, qm)
                dv = dv + _dot_tn(p.astype(BF16), dom)
            dq_acc[rows, :] = dq
            dk_acc[prow, :] += dk[0:Q_BLOCK]
            dv_acc[prow, :] += dv[0:Q_BLOCK]
            dk_acc[rows, :] += dk[Q_BLOCK:]
            dv_acc[rows, :] += dv[Q_BLOCK:]

        def step(t, _):
            tiles = [scores(DIL_BLOCKS_PER_STEP * t + u) for u in range(DIL_BLOCKS_PER_STEP)]
            for u in range(DIL_BLOCKS_PER_STEP):
                finish(DIL_BLOCKS_PER_STEP * t + u, tiles[u])
            return 0

        lax.fori_loop(0, nblk // DIL_BLOCKS_PER_STEP, step, 0)
        staged[0] = _unrope(dq_acc[...], c_ref[...], sp_ref[...], sm_ref[...], 8).astype(BF16)
        staged[1] = _unrope(dk_acc[...], c_ref[...], sp_ref[...], sm_ref[...], 8).astype(BF16)
        staged[2] = dv_acc[...].astype(BF16)
        pair = pl.program_id(0)
        copies = [pltpu.make_async_copy(
            staged.at[t], dp_out.at[:, pl.ds(pl.multiple_of(off + 512 * g + 128 * pair, 128), 128)], sems.at[t])
            for t, off in enumerate((Q_OFF, K_OFF, V_OFF))]
        for cp in copies:
            cp.start()
        for cp in copies:
            cp.wait()

    blk = lambda off: _full((s, 128), lambda p, off=off: (0, off + p))
    tab = _full((s, 128), lambda p: (0, 0))
    qo, ko, vo = ((off + 512 * g) // 128 for off in (Q_OFF, K_OFF, V_OFF))
    return pl.pallas_call(
        body, name=name, grid=(4,),
        in_specs=[blk(qo), blk(ko), blk(vo), blk(0), blk(0), blk(0), tab, tab, tab, pl.BlockSpec(memory_space=pl.ANY)],
        out_specs=pl.BlockSpec(memory_space=pl.ANY),
        out_shape=jax.ShapeDtypeStruct((s, IN_PAD), BF16),
        scratch_shapes=[pltpu.VMEM((s, 128), F32)] * 7 + [pltpu.VMEM((3, s, 128), BF16), pltpu.SemaphoreType.DMA((3,))],
        input_output_aliases={9: 0},
        compiler_params=_cparams(("arbitrary",)),
    )(proj, proj, proj, do, lse, dd, *tabs, dproj)


def _chip_copies(ins, outs, sems, outgoing):
    x, y, c, chips = _place()
    myq = 2 * x + y
    n = len(ins)

    def chunk(a, j):
        q = 2 * chips[j][0] + chips[j][1]
        return _remote(ins[a].at[q], outs[a].at[myq if outgoing else q], sems, 2 * a, j, (*chips[j], c))

    if outgoing is None:
        return [pltpu.make_async_copy(ins[a].at[myq], outs[a].at[myq], sems.at[2 * a, 3]) for a in range(n)]
    return [chunk(a, j) for j in range(3) for a in range(n)]


def _dh_bwd(dproj, w_pad, x, gpre, dy, pairs):
    s = x.shape[0]
    tm, tk = min(1024, s), 1408
    ni, nk = s // tm, IN_PAD // tk
    n = len(pairs)

    def body(*refs):
        dp_ref, w_ref, x_ref, g_ref, dy_ref = refs[:5]
        ins, gx_out, dg_out, outs = refs[5:5 + n], refs[5 + n], refs[6 + n], refs[7 + n:7 + 2 * n]
        acc, sems = refs[7 + 2 * n], refs[8 + 2 * n]
        i, kk = pl.program_id(0), pl.program_id(1)

        @pl.when(jnp.logical_and(i == 0, kk == 0))
        def _():
            dg_out[...] = jnp.zeros_like(dg_out)
            for cp in _chip_copies(ins, outs, sems, None) + _chip_copies(ins, outs, sems, True):
                cp.start()

        @pl.when(kk == 0)
        def _():
            acc[...] = jnp.zeros_like(acc)

        acc[...] += _dot_nt(dp_ref[...], w_ref[...])

        @pl.when(kk == nk - 1)
        def _():
            g = g_ref[...]
            _, xh, r = _rms(x_ref[...], g)
            dh = acc[...]
            dg_out[...] += jnp.sum(dh * xh, axis=0, keepdims=True)
            gx_out[...] = dy_ref[...] + _rms_bwd(dh, g, xh, r)

        @pl.when(jnp.logical_and(i == ni - 1, kk == nk - 1))
        def _():
            for cp in _chip_copies(ins, outs, sems, False):
                cp.wait_recv()
            for cp in _chip_copies(ins, outs, sems, True):
                cp.wait_send()
            for cp in _chip_copies(ins, outs, sems, None):
                cp.wait()

    row = lambda i, k: (i, 0)
    hbm = pl.BlockSpec(memory_space=pl.ANY)
    res = pl.pallas_call(
        body, name="dh_bwd", grid=(ni, nk),
        in_specs=[_full((tm, tk), lambda i, k: (i, k)), _full((1024, tk), lambda i, k: (0, k)),
                  _full((tm, 1024), row), _full((1, 1024), lambda i, k: (0, 0)), _full((tm, 1024), row)] + [hbm] * n,
        out_specs=[_full((tm, 1024), row), _full((1, 1024), lambda i, k: (0, 0))] + [hbm] * n,
        out_shape=[jax.ShapeDtypeStruct((s, 1024), F32), jax.ShapeDtypeStruct((1, 1024), F32)]
        + [jax.ShapeDtypeStruct(a.shape, a.dtype) for a in pairs],
        scratch_shapes=[pltpu.VMEM((tm, 1024), F32), pltpu.SemaphoreType.DMA((2 * n, 4))],
        compiler_params=_cparams(("arbitrary", "arbitrary")),
    )(dproj, w_pad, x, gpre, dy, *pairs)
    return res[0], res[1], res[2:]


def _dw_in(ht, dproj, pairs):
    s = ht.shape[1]
    tn = 768
    nj = IN_PAD // tn
    n = len(pairs)

    def body(*refs):
        ht_ref, dp_ref = refs[:2]
        ins, o_ref, outs, sems = refs[2:2 + n], refs[2 + n], refs[3 + n:3 + 2 * n], refs[3 + 2 * n]
        j = pl.program_id(0)

        if n:
            @pl.when(j == 0)
            def _():
                for cp in _chip_copies(ins, outs, sems, None) + _chip_copies(ins, outs, sems, True):
                    cp.start()

        o_ref[...] = _dot(ht_ref[...], dp_ref[...]).astype(BF16)

        if n:
            @pl.when(j == nj - 1)
            def _():
                for cp in _chip_copies(ins, outs, sems, False):
                    cp.wait_recv()
                for cp in _chip_copies(ins, outs, sems, True):
                    cp.wait_send()
                for cp in _chip_copies(ins, outs, sems, None):
                    cp.wait()

    hbm = pl.BlockSpec(memory_space=pl.ANY)
    res = pl.pallas_call(
        body, name="dw_in", grid=(nj,),
        in_specs=[_full((1024, s), lambda j: (0, 0)), _full((s, tn), lambda j: (0, j))] + [hbm] * n,
        out_specs=[_full((1024, tn), lambda j: (0, j))] + [hbm] * n,
        out_shape=[jax.ShapeDtypeStruct((1024, IN_PAD), BF16)] + [jax.ShapeDtypeStruct(a.shape, a.dtype) for a in pairs],
        scratch_shapes=[pltpu.SemaphoreType.DMA((max(2 * n, 2), 4))],
        compiler_params=_cparams(("arbitrary",)),
    )(ht, dproj, *pairs)
    return res[0], res[1:]


def _remote(src, dst, sems, row, k, to):
    return pltpu.make_async_remote_copy(src_ref=src, dst_ref=dst, send_sem=sems.at[row, k], recv_sem=sems.at[row + 1, k],
                                        device_id=to, device_id_type=pl.DeviceIdType.MESH)


def _place():
    x, y, c = lax.axis_index("x"), lax.axis_index("y"), lax.axis_index("c")
    return x, y, c, [(1 - x, y), (x, 1 - y), (1 - x, 1 - y)]


def _gather_parts(ins, outs, sems):
    n = len(ins)
    x, y, c, chips = _place()
    me, sib = (x, y, c), (x, y, 1 - c)
    idx = lambda p: 4 * p[0] + 2 * p[1] + p[2]

    def copy(a, k, block, to, from_input=False):
        src = ins[a] if from_input else outs[a].at[idx(block)]
        return _remote(src, outs[a].at[idx(block)], sems, 2 * a, k, to)

    own = lambda: [pltpu.make_async_copy(ins[a], outs[a].at[idx(me)], sems.at[2 * a, 7]) for a in range(n)]
    first = lambda: ([copy(a, 0, me, sib, True) for a in range(n)]
                     + [copy(a, 1 + j, me, (*chips[j], c), True) for j in range(3) for a in range(n)])
    from_chip = lambda j: [copy(a, 1 + j, (*chips[j], c), me) for a in range(n)]
    passed = lambda j: [copy(a, 4 + j, (*chips[j], c), sib) for a in range(n)]
    from_sibling = lambda: ([copy(a, 0, sib, me) for a in range(n)]
                            + [copy(a, 4 + j, (*chips[j], 1 - c), me) for j in range(3) for a in range(n)])
    return own, first, from_chip, passed, from_sibling


def _gather_start(ins, outs, sems):
    own, first, _, _, _ = _gather_parts(ins, outs, sems)
    for cp in own() + first():
        cp.start()


def _gather_forward(ins, outs, sems):
    _, _, from_chip, passed, _ = _gather_parts(ins, outs, sems)
    for j in range(3):
        for cp in from_chip(j):
            cp.wait_recv()
        for cp in passed(j):
            cp.start()


def _gather_finish(ins, outs, sems):
    own, first, _, passed, from_sibling = _gather_parts(ins, outs, sems)
    for cp in from_sibling():
        cp.wait_recv()
    for cp in first() + [cp for j in range(3) for cp in passed(j)]:
        cp.wait_send()
    for cp in own():
        cp.wait()


def _gather_weights(arrays):
    n = len(arrays)

    def body(*refs):
        ins, outs, sems = refs[:n], refs[n:2 * n], refs[2 * n]
        _gather_start(ins, outs, sems)
        _gather_forward(ins, outs, sems)
        _gather_finish(ins, outs, sems)

    hbm = pl.BlockSpec(memory_space=pl.ANY)
    return pl.pallas_call(
        body, name="gather_weights", in_specs=[hbm] * n, out_specs=[hbm] * n,
        out_shape=[jax.ShapeDtypeStruct((N_DEV,) + a.shape, a.dtype) for a in arrays],
        scratch_shapes=[pltpu.SemaphoreType.DMA((2 * n, N_DEV))],
    )(*arrays)


def _pair_exchange(chunks, name):
    n = len(chunks)

    def body(*refs):
        ins, outs, sems = refs[:n], refs[n:2 * n], refs[2 * n]
        x, y, c, _ = _place()
        sent = [_remote(ins[a].at[2 * q + (1 - c)], outs[a].at[q], sems, 2 * a, q, (x, y, 1 - c))
                for a in range(n) for q in range(4)]
        for cp in sent:
            cp.start()
        for cp in sent:
            cp.wait_recv()
        for cp in sent:
            cp.wait_send()

    hbm = pl.BlockSpec(memory_space=pl.ANY)
    return pl.pallas_call(
        body, name=name, in_specs=[hbm] * n, out_specs=[hbm] * n,
        out_shape=[jax.ShapeDtypeStruct((4,) + a.shape[1:], a.dtype) for a in chunks],
        scratch_shapes=[pltpu.SemaphoreType.DMA((2 * n, 4))],
    )(*chunks)


def _pair_sum(core, chunks, recv, name, tr):
    _, rows, cols = chunks.shape

    def body(c_ref, a_ref, b_ref, o_ref):
        o_ref[...] = (a_ref[...].astype(F32) + b_ref[...].astype(F32)).astype(BF16)

    blk = lambda f: _full((1, tr, cols), f)
    return pl.pallas_call(
        body, name=name, out_shape=jax.ShapeDtypeStruct((4, rows, cols), BF16),
        grid_spec=pltpu.PrefetchScalarGridSpec(
            num_scalar_prefetch=1, grid=(4, rows // tr),
            in_specs=[blk(lambda q, i, c: (2 * q + c[0], i, 0)), blk(lambda q, i, c: (q, i, 0))],
            out_specs=blk(lambda q, i, c: (q, i, 0))),
        compiler_params=_cparams(("parallel", "parallel")),
    )(core, chunks, recv)


def _packet_exchange(packet):
    def body(pk, pk_out, sems):
        x, y, c, _ = _place()
        me = 4 * x + 2 * y + c
        flip = lambda v, b: (1 - v) if b else v

        def small(j, outgoing):
            peer = (flip(x, (j >> 2) & 1), flip(y, (j >> 1) & 1), flip(c, j & 1))
            slot = me if outgoing else 4 * peer[0] + 2 * peer[1] + peer[2]
            return _remote(pk, pk_out.at[slot], sems, 0, j, peer)

        own = pltpu.make_async_copy(pk, pk_out.at[me], sems.at[0, 0])
        sent = [small(j, True) for j in range(1, N_DEV)]
        for cp in [own] + sent:
            cp.start()
        for j in range(1, N_DEV):
            small(j, False).wait_recv()
        for cp in sent:
            cp.wait_send()
        own.wait()

    hbm = pl.BlockSpec(memory_space=pl.ANY)
    return pl.pallas_call(
        body, name="packet_exchange", in_specs=[hbm], out_specs=hbm,
        out_shape=jax.ShapeDtypeStruct((N_DEV,) + packet.shape, packet.dtype),
        scratch_shapes=[pltpu.SemaphoreType.DMA((2, N_DEV))],
    )(packet)


def _adam_math(w, g, m, v):
    m = ADAM_B1 * m + (1.0 - ADAM_B1) * g
    v = ADAM_B2 * v + (1.0 - ADAM_B2) * (g * g)
    m_hat = m / (1.0 - ADAM_B1 ** ADAM_STEP)
    v_hat = v / (1.0 - ADAM_B2 ** ADAM_STEP)
    delta = -ADAM_LR * (m_hat / (jnp.sqrt(v_hat) + ADAM_EPS) + ADAM_WD * w)
    return delta, m, v


def _adam(recv, w, m, v, name, tr):
    _, rows, cols = w.shape

    def body(r_ref, w_ref, m_ref, v_ref, g_out, d_out, m_out, v_out):
        g = r_ref[0].astype(F32)
        for k in range(1, 4):
            g = g + r_ref[k].astype(F32)
        g_out[0] = g
        d_out[0], m_out[0], v_out[0] = _adam_math(w_ref[0], g, m_ref[0], v_ref[0])

    blk = _full((1, tr, cols), lambda i: (0, i, 0))
    return pl.pallas_call(
        body, name=name, grid=(rows // tr,),
        in_specs=[_full((4, tr, cols), lambda i: (0, i, 0)), blk, blk, blk],
        out_specs=[blk] * 4,
        out_shape=[jax.ShapeDtypeStruct(w.shape, F32)] * 4,
        compiler_params=_cparams(("parallel",)),
    )(recv, w, m, v)


def _adam_gains(recv, gains, gains_m, gains_v):
    def body(*refs):
        r_ref, w, m, v = refs[0], refs[1:5], refs[5:9], refs[9:13]
        g_out, d_out, m_out, v_out, loss_out = refs[13:17], refs[17:21], refs[21:25], refs[25:29], refs[29]
        tot = r_ref[0:1, :]
        for k in range(1, N_DEV):
            tot = tot + r_ref[k:k + 1, :]
        for t in range(4):
            g = tot[:, GAIN_OFFS[t]:GAIN_OFFS[t] + GAIN_WIDTHS[t]]
            g_out[t][...] = g
            d_out[t][...], m_out[t][...], v_out[t][...] = _adam_math(w[t][...], g, m[t][...], v[t][...])
        loss_out[...] = tot[:, LOSS_OFF:LOSS_OFF + LANES]

    shapes = [jax.ShapeDtypeStruct((1, n), F32) for n in GAIN_WIDTHS]
    return pl.pallas_call(
        body, name="adam_gains", out_shape=shapes * 4 + [jax.ShapeDtypeStruct((1, LANES), F32)],
    )(recv, *gains, *gains_m, *gains_v)


def _local_step(x, positions, gains, w_pad, small_shards, small_weights, target):
    gpre, gq, gkv, gpost = gains
    q_tab, k_tab, d_tab = _rope_tables(positions)

    proj, ht, gathered = _inproj(x, gpre, w_pad, d_tab, small_shards)
    w_uq, w_uk, w_uv, wp_mla, wp_dil, w_out = _assemble_small(*gathered) if small_shards else small_weights
    q, k, v, qt, kt, vt = _mla_prep(proj, gq, gkv, w_uq, w_uk, w_uv, q_tab, k_tab)
    o_mla, ot_mla, lse_mla = _mla_fwd(q, k, vt)

    od, lsed = [], []
    for g, d in enumerate(DIL_DILATIONS):
        o_g, lse_g = _dil_fwd(proj, g, d, "dil_fwd_%d" % g)
        od.append(o_g)
        lsed.append(lse_g)

    merged, ya, yd, o_dil, lse_dil = _merge_fwd(proj, o_mla, od, lsed, wp_mla, wp_dil)
    do, dy, loss, dgpost = _out_loss(merged, w_out, x, target, gpost)

    dproj, do_mla, do_dil, dot_mla, dd_dil, dw_out, dwp_mla, dwp_dil = _merge_bwd(
        do, w_out, merged, proj, ya, yd, o_mla, o_dil, wp_mla, wp_dil)

    dq, dk, dv = _mla_bwd(q, qt, k, kt, v, do_mla, dot_mla, ot_mla, lse_mla)
    dproj, dw_uq, dw_uk, dw_uv, dgq, dgkv = _mla_prep_bwd(dq, dk, dv, proj, dproj, gq, gkv, w_uq, w_uk, w_uv,
                                                          q_tab, k_tab)
    for g, d in enumerate(DIL_DILATIONS):
        dproj = _dil_bwd(proj, dproj, g, do_dil, lse_dil, dd_dil, d_tab, d, "dil_bwd_%d" % g)
    small = _small_chunks(dw_uq, dw_uk, dw_uv, dwp_mla, dwp_dil, dw_out)
    dw_in, small = _dw_in(ht, dproj, _pair_stage(small, 1, "small")) if small_shards else (_dw_in(ht, dproj, [])[0], small)
    return loss, (dproj, dy), (dgq, dgkv, dgpost), dw_in, small


def _pair_stage(chunks, first, name):
    from_sibling = _pair_exchange(chunks, "pair_exchange_" + name)
    core = lax.axis_index("c").astype(jnp.int32).reshape(1)
    return [_pair_sum(core, chunks[t], from_sibling[t], "pair_sum_%d" % (first + t), PAIR_ROWS[first + t])
            for t in range(len(chunks))]


ADAM_ROWS = (256, 384, 256, 512, 512, 128)
PAIR_ROWS = (512, 384, 256, 512, 512, 128)


def kernel(x, positions, pre_norm_g, w_in, q_norm_g, w_uq, kv_norm_g, w_ukv, w_proj_mla, w_proj_dil, w_out, post_norm_g, loss_target, m_pre_norm_g, m_w_in, m_q_norm_g, m_w_uq, m_kv_norm_g, m_w_ukv, m_w_proj_mla, m_w_proj_dil, m_w_out, m_post_norm_g, v_pre_norm_g, v_w_in, v_q_norm_g, v_w_uq, v_kv_norm_g, v_w_ukv, v_w_proj_mla, v_w_proj_dil, v_w_out, v_post_norm_g):
    big_w = (w_in, w_uq, w_ukv, w_proj_mla, w_proj_dil, w_out)
    big_m = (m_w_in, m_w_uq, m_w_ukv, m_w_proj_mla, m_w_proj_dil, m_w_out)
    big_v = (v_w_in, v_w_uq, v_w_ukv, v_w_proj_mla, v_w_proj_dil, v_w_out)
    gains = (pre_norm_g, q_norm_g, kv_norm_g, post_norm_g)
    gains_m = (m_pre_norm_g, m_q_norm_g, m_kv_norm_g, m_post_norm_g)
    gains_v = (v_pre_norm_g, v_q_norm_g, v_kv_norm_g, v_post_norm_g)

    shards = [w[0].astype(BF16) for w in big_w]
    w_pad = _assemble_w_in(_gather_weights(shards[:1])[0])

    loss, (dproj, dy), (dgq, dgkv, dgpost), dw_in, received_small = _local_step(
        x[0], positions[0], gains, w_pad, shards[1:], None, loss_target[0])

    grad_x, dgpre, received_in = _dh_bwd(dproj, w_pad, x[0], pre_norm_g, dy,
                                         _pair_stage([_dw_in_chunks(dw_in)], 0, "in"))
    received = list(received_in) + list(received_small)
    packet = _packet_exchange(jnp.concatenate([dgpre, dgq, dgkv, dgpost, loss[0:1]], axis=1))

    big = [_adam(received[t], big_w[t], big_m[t], big_v[t], "adam_%d" % t, ADAM_ROWS[t]) for t in range(6)]
    small = _adam_gains(packet.reshape(N_DEV, PACKET), gains, gains_m, gains_v)

    def interleave(kind):
        s_pre, s_q, s_kv, s_post = small[4 * kind:4 * kind + 4]
        b_in, b_uq, b_ukv, b_pm, b_pd, b_out = (big[t][kind] for t in range(6))
        return [s_pre, b_in, s_q, b_uq, s_kv, b_ukv, b_pm, b_pd, b_out, s_post]

    return (small[16][0, 0], grad_x[None], *interleave(0), *interleave(1), *interleave(2), *interleave(3))
```
